```python
import math
import jax, jax.numpy as jnp
from jax import lax
import numpy as np

D_MODEL = 2048
BATCH = 8
SEQ = 8192
DEPTH = 4

GDN_DK = 128
GDN_DV = 128
GDN_W = 3 * D_MODEL // 8
GDN_HEADS = GDN_W // GDN_DV
GDN_QK = GDN_HEADS * GDN_DK
ATT_HD = 64
ATT_W = 3 * D_MODEL // 8
ATT_HEADS = ATT_W // ATT_HD
CONV_CH = D_MODEL - GDN_W - ATT_W
MIX_W = CONV_CH + GDN_W + ATT_W

CONV_WIDTH = 31
SHORT_CONV = 4
GDN_CHUNK = 64

ROPE_THETA = 500000.0
ROPE_DIM = ATT_HD // 4
DIL_PATTERNS = ((128, 1), (512, 4), (2048, 16))
ATT_BLOCK = 128
NEG_INF = -1e30

IN_SPLITS = (
    2 * CONV_CH, CONV_CH,
    GDN_QK, GDN_QK, GDN_W, GDN_W, GDN_HEADS, GDN_HEADS,
    ATT_W, ATT_W, ATT_W, ATT_W,
)
IN_W = sum(IN_SPLITS)

kernel_name = "hymba_style_conformer_gdn_dilated_hybrid"


def rms_norm(x, w, eps=1e-6):
    xf = x.astype(jnp.float32)
    y = xf * lax.rsqrt(jnp.mean(xf * xf, axis=-1, keepdims=True) + eps)
    return (y * w.astype(jnp.float32)).astype(x.dtype)


def layer_norm(x, w, b, eps=1e-5):
    xf = x.astype(jnp.float32)
    mu = jnp.mean(xf, axis=-1, keepdims=True)
    var = jnp.mean(jnp.square(xf - mu), axis=-1, keepdims=True)
    y = (xf - mu) * lax.rsqrt(var + eps) * w.astype(jnp.float32) + b.astype(jnp.float32)
    return y.astype(x.dtype)


def l2_normalize(x, eps=1e-6):
    xf = x.astype(jnp.float32)
    return xf * lax.rsqrt(jnp.sum(xf * xf, axis=-1, keepdims=True) + eps)


def causal_dwconv(x, w):
    K, C = w.shape
    return lax.conv_general_dilated(
        x, w[:, None, :].astype(x.dtype), window_strides=(1,), padding=[(K - 1, 0)],
        dimension_numbers=("NWC", "WIO", "NWC"), feature_group_count=C)


def rope_tables(S):
    half = ROPE_DIM // 2
    inv = ROPE_THETA ** (-jnp.arange(half, dtype=jnp.float32) / half)
    ang = jnp.arange(S, dtype=jnp.float32)[:, None] * inv[None, :]
    return jnp.cos(ang), jnp.sin(ang)


def apply_partial_rope(x, cos, sin):
    half = ROPE_DIM // 2
    c, s = cos[None, :, None, :], sin[None, :, None, :]
    x1, x2, rest = x[..., :half], x[..., half:ROPE_DIM], x[..., ROPE_DIM:]
    return jnp.concatenate([x1 * c - x2 * s, x2 * c + x1 * s, rest], axis=-1)


def conformer_conv(u, dw_w, dw_b, ln_w, ln_b, pw_w):
    a, b = jnp.split(u, 2, axis=-1)
    h = a * jax.nn.sigmoid(b)
    h = causal_dwconv(h, dw_w) + dw_b.astype(h.dtype)
    h = layer_norm(h, ln_w, ln_b)
    h = jax.nn.silu(h)
    return h @ pw_w.astype(h.dtype)


def chunk_gated_delta_rule(q, k, v, g, beta):
    B, S, H, DK = q.shape
    DV = v.shape[-1]
    C = GDN_CHUNK
    N = S // C
    f32 = jnp.float32

    def chunks(t):
        return t.astype(f32).reshape(B, N, C, H, -1).transpose(0, 3, 1, 2, 4)

    def chunks_h(t):
        return t.astype(f32).reshape(B, N, C, H).transpose(0, 3, 1, 2)

    q, k, v = chunks(q), chunks(k), chunks(v)
    beta = chunks_h(beta)
    g = jnp.cumsum(chunks_h(g), axis=-1)

    causal = jnp.tril(jnp.ones((C, C), dtype=bool))
    strict = jnp.tril(jnp.ones((C, C), dtype=bool), -1)
    diff = g[..., :, None] - g[..., None, :]
    decay = jnp.where(causal, jnp.exp(jnp.where(causal, diff, 0.0)), 0.0)

    kk = jnp.einsum("bhncd,bhnmd->bhncm", k, k)
    lower = jnp.where(strict, beta[..., :, None] * kk * decay, 0.0)
    eye = jnp.eye(C, dtype=f32)
    T = lax.linalg.triangular_solve(eye + lower, jnp.broadcast_to(eye, lower.shape),
                                    left_side=True, lower=True, unit_diagonal=True)
    w_v = jnp.einsum("bhncm,bhnmd->bhncd", T, v * beta[..., None])
    w_k = jnp.einsum("bhncm,bhnmd->bhncd", T, k * (beta * jnp.exp(g))[..., None])
    qk = jnp.where(causal, jnp.einsum("bhncd,bhnmd->bhncm", q, k) * decay, 0.0)
    q_dec = q * jnp.exp(g)[..., None]
    k_dec = k * jnp.exp(g[..., -1:] - g)[..., None]
    g_last = jnp.exp(g[..., -1])

    def step(state, xs):
        qk_i, qd_i, wv_i, wk_i, kd_i, gl_i = xs
        v_new = wv_i - jnp.einsum("bhcd,bhde->bhce", wk_i, state)
        o_i = jnp.einsum("bhcd,bhde->bhce", qd_i, state) + jnp.einsum("bhcm,bhme->bhce", qk_i, v_new)
        state = state * gl_i[..., None, None] + jnp.einsum("bhcd,bhce->bhde", kd_i, v_new)
        return state, o_i

    xs = (jnp.moveaxis(qk, 2, 0), jnp.moveaxis(q_dec, 2, 0), jnp.moveaxis(w_v, 2, 0),
          jnp.moveaxis(w_k, 2, 0), jnp.moveaxis(k_dec, 2, 0), jnp.moveaxis(g_last, 2, 0))
    state0 = jnp.zeros((B, H, DK, DV), f32)
    _, o = lax.scan(step, state0, xs)
    return o.transpose(1, 0, 3, 2, 4).reshape(B, S, H, DV)


def gated_deltanet(q, k, v, z, beta_in, alpha_in, conv_w, a_log, dt_bias, norm_w):
    B, S, _ = q.shape
    qkv = jax.nn.silu(causal_dwconv(jnp.concatenate([q, k, v], axis=-1), conv_w))
    q, k, v = jnp.split(qkv, [GDN_QK, 2 * GDN_QK], axis=-1)
    q = l2_normalize(q.reshape(B, S, GDN_HEADS, GDN_DK)) * (GDN_DK ** -0.5)
    k = l2_normalize(k.reshape(B, S, GDN_HEADS, GDN_DK))
    v = v.reshape(B, S, GDN_HEADS, GDN_DV)
    beta = jax.nn.sigmoid(beta_in.astype(jnp.float32))
    g = -jnp.exp(a_log.astype(jnp.float32)) * jax.nn.softplus(
        alpha_in.astype(jnp.float32) + dt_bias.astype(jnp.float32))
    o = chunk_gated_delta_rule(q, k, v, g, beta).astype(z.dtype)
    o = rms_norm(o, norm_w) * jax.nn.silu(z.reshape(B, S, GDN_HEADS, GDN_DV))
    return o.reshape(B, S, GDN_W)


def strided_window_attention(q, k, v, span, dil):
    B, S, H, E = q.shape
    unit = dil * ATT_BLOCK
    Sp = -(-S // unit) * unit
    Lr = Sp // dil
    nb = Lr // ATT_BLOCK

    def to_strided(t):
        t = jnp.pad(t, ((0, 0), (0, Sp - S), (0, 0), (0, 0)))
        return t.reshape(B, Lr, dil, H, E).transpose(0, 2, 1, 3, 4).reshape(B, dil, nb, ATT_BLOCK, H, E)

    def with_prev(t):
        prev = jnp.pad(t, ((0, 0), (0, 0), (1, 0), (0, 0), (0, 0), (0, 0)))[:, :, :-1]
        return jnp.concatenate([prev, t], axis=3)

    qs = to_strided(q)
    kb, vb = with_prev(to_strided(k)), with_prev(to_strided(v))
    s = jnp.einsum("bdnqhe,bdnkhe->bdnhqk", qs, kb) * (E ** -0.5)
    qi = jnp.arange(ATT_BLOCK)[:, None]
    ki = jnp.arange(2 * ATT_BLOCK)[None, :]
    dist = qi + ATT_BLOCK - ki
    blk = jnp.arange(nb)[:, None, None]
    valid = (dist >= 0) & (dist <= span) & ((blk - 1) * ATT_BLOCK + ki >= 0)
    s = jnp.where(valid[:, None], s, NEG_INF)
    m = jnp.max(s, axis=-1, keepdims=True)
    p = jnp.exp(s - m)
    l = jnp.sum(p, axis=-1, keepdims=True)
    o = jnp.einsum("bdnhqk,bdnkhe->bdnqhe", p / l, vb)
    lse = (m + jnp.log(l))[..., 0]
    o = o.reshape(B, dil, Lr, H, E).transpose(0, 2, 1, 3, 4).reshape(B, Sp, H, E)[:, :S]
    lse = lse.transpose(0, 1, 2, 4, 3).reshape(B, dil, Lr, H).transpose(0, 2, 1, 3).reshape(B, Sp, H)[:, :S]
    return o, lse


def dilated_attention(q, k, v, cos, sin):
    B, S, _ = q.shape
    dt = q.dtype
    q = apply_partial_rope(q.astype(jnp.float32).reshape(B, S, ATT_HEADS, ATT_HD), cos, sin)
    k = apply_partial_rope(k.astype(jnp.float32).reshape(B, S, ATT_HEADS, ATT_HD), cos, sin)
    v = v.astype(jnp.float32).reshape(B, S, ATT_HEADS, ATT_HD)
    outs, lses = [], []
    for window, dil in DIL_PATTERNS:
        o_g, lse_g = strided_window_attention(q, k, v, window // dil, dil)
        outs.append(o_g)
        lses.append(lse_g)
    wts = jax.nn.softmax(jnp.stack(lses, axis=0), axis=0)
    o = jnp.einsum("pbsh,pbshe->bshe", wts, jnp.stack(outs, axis=0))
    return o.reshape(B, S, ATT_W).astype(dt)


def _fwd_setup_inputs(seed: int = 0) -> dict:
    key = jax.random.key(seed)
    ks = jax.random.split(key, 16)
    f32 = jnp.float32

    def nrm(k, shape, scale):
        return jax.random.normal(k, shape, f32) * scale

    x = nrm(ks[0], (BATCH, SEQ, D_MODEL), 1.0)
    norm_w = 1.0 + nrm(ks[1], (DEPTH, D_MODEL), 0.01)
    w_in = nrm(ks[2], (DEPTH, D_MODEL, IN_W), D_MODEL ** -0.5)
    conv_qkv_w = nrm(ks[3], (DEPTH, SHORT_CONV, 2 * GDN_QK + GDN_W), SHORT_CONV ** -0.5)
    a_log = jnp.log(jax.random.uniform(ks[4], (DEPTH, GDN_HEADS), f32, 1.0, 16.0))
    dt = jnp.exp(jax.random.uniform(ks[5], (DEPTH, GDN_HEADS), f32, math.log(1e-3), math.log(1e-1)))
    dt_bias = dt + jnp.log(-jnp.expm1(-dt))
    gdn_norm_w = 1.0 + nrm(ks[6], (DEPTH, GDN_DV), 0.01)
    conf_dw_w = nrm(ks[7], (DEPTH, CONV_WIDTH, CONV_CH), CONV_WIDTH ** -0.5)
    conf_dw_b = nrm(ks[8], (DEPTH, CONV_CH), 0.01)
    conf_ln_w = 1.0 + nrm(ks[9], (DEPTH, CONV_CH), 0.01)
    conf_ln_b = nrm(ks[10], (DEPTH, CONV_CH), 0.01)
    conf_pw_w = nrm(ks[11], (DEPTH, CONV_CH, CONV_CH), CONV_CH ** -0.5)
    w_out = nrm(ks[12], (DEPTH, MIX_W, D_MODEL), MIX_W ** -0.5)
    final_norm_w = 1.0 + nrm(ks[13], (D_MODEL,), 0.01)
    return {"x": x, "norm_w": norm_w, "w_in": w_in, "conv_qkv_w": conv_qkv_w, "a_log": a_log,
            "dt_bias": dt_bias, "gdn_norm_w": gdn_norm_w, "conf_dw_w": conf_dw_w,
            "conf_dw_b": conf_dw_b, "conf_ln_w": conf_ln_w, "conf_ln_b": conf_ln_b,
            "conf_pw_w": conf_pw_w, "w_out": w_out, "final_norm_w": final_norm_w}


def _fwd_reference(x, norm_w, w_in, conv_qkv_w, a_log, dt_bias, gdn_norm_w, conf_dw_w, conf_dw_b,
              conf_ln_w, conf_ln_b, conf_pw_w, w_out, final_norm_w):
    B, S, _ = x.shape
    cos, sin = rope_tables(S)
    cuts, acc = [], 0
    for n in IN_SPLITS[:-1]:
        acc += n
        cuts.append(acc)
    for l in range(DEPTH):
        h = rms_norm(x, norm_w[l])
        u = h @ w_in[l].astype(h.dtype)
        (c_in, c_gate, g_q, g_k, g_v, g_z, g_b, g_a,
         a_q, a_k, a_v, a_gate) = jnp.split(u, cuts, axis=-1)
        y_conv = conformer_conv(c_in, conf_dw_w[l], conf_dw_b[l], conf_ln_w[l], conf_ln_b[l],
                                conf_pw_w[l]) * jax.nn.silu(c_gate)
        y_gdn = gated_deltanet(g_q, g_k, g_v, g_z, g_b, g_a, conv_qkv_w[l], a_log[l], dt_bias[l],
                               gdn_norm_w[l])
        y_att = dilated_attention(a_q, a_k, a_v, cos, sin) * jax.nn.silu(a_gate)
        y = jnp.concatenate([y_conv, y_gdn, y_att], axis=-1)
        x = x + y @ w_out[l].astype(y.dtype)
    return rms_norm(x, final_norm_w)


import jax as _jax
import jax.numpy as _jnp

TWIN_FORMAT = 'train_step'
FWD_PARAMS = ['x', 'norm_w', 'w_in', 'conv_qkv_w', 'a_log', 'dt_bias', 'gdn_norm_w', 'conf_dw_w', 'conf_dw_b', 'conf_ln_w', 'conf_ln_b', 'conf_pw_w', 'w_out', 'final_norm_w']
TWIN_WEIGHTS = ['norm_w', 'w_in', 'conv_qkv_w', 'a_log', 'dt_bias', 'gdn_norm_w', 'conf_dw_w', 'conf_dw_b', 'conf_ln_w', 'conf_ln_b', 'conf_pw_w', 'w_out', 'final_norm_w']
TWIN_DIFF_INPUT = 'x'
TWIN_INPUTS = ['x', 'norm_w', 'w_in', 'conv_qkv_w', 'a_log', 'dt_bias', 'gdn_norm_w', 'conf_dw_w', 'conf_dw_b', 'conf_ln_w', 'conf_ln_b', 'conf_pw_w', 'w_out', 'final_norm_w', 'loss_target', 'm_norm_w', 'm_w_in', 'm_conv_qkv_w', 'm_a_log', 'm_dt_bias', 'm_gdn_norm_w', 'm_conf_dw_w', 'm_conf_dw_b', 'm_conf_ln_w', 'm_conf_ln_b', 'm_conf_pw_w', 'm_w_out', 'm_final_norm_w', 'v_norm_w', 'v_w_in', 'v_conv_qkv_w', 'v_a_log', 'v_dt_bias', 'v_gdn_norm_w', 'v_conf_dw_w', 'v_conf_dw_b', 'v_conf_ln_w', 'v_conf_ln_b', 'v_conf_pw_w', 'v_w_out', 'v_final_norm_w']
TWIN_OUTPUTS = ['loss', 'grad_x', 'grad_norm_w', 'grad_w_in', 'grad_conv_qkv_w', 'grad_a_log', 'grad_dt_bias', 'grad_gdn_norm_w', 'grad_conf_dw_w', 'grad_conf_dw_b', 'grad_conf_ln_w', 'grad_conf_ln_b', 'grad_conf_pw_w', 'grad_w_out', 'grad_final_norm_w', 'delta_norm_w', 'delta_w_in', 'delta_conv_qkv_w', 'delta_a_log', 'delta_dt_bias', 'delta_gdn_norm_w', 'delta_conf_dw_w', 'delta_conf_dw_b', 'delta_conf_ln_w', 'delta_conf_ln_b', 'delta_conf_pw_w', 'delta_w_out', 'delta_final_norm_w', 'new_m_norm_w', 'new_m_w_in', 'new_m_conv_qkv_w', 'new_m_a_log', 'new_m_dt_bias', 'new_m_gdn_norm_w', 'new_m_conf_dw_w', 'new_m_conf_dw_b', 'new_m_conf_ln_w', 'new_m_conf_ln_b', 'new_m_conf_pw_w', 'new_m_w_out', 'new_m_final_norm_w', 'new_v_norm_w', 'new_v_w_in', 'new_v_conv_qkv_w', 'new_v_a_log', 'new_v_dt_bias', 'new_v_gdn_norm_w', 'new_v_conf_dw_w', 'new_v_conf_dw_b', 'new_v_conf_ln_w', 'new_v_conf_ln_b', 'new_v_conf_pw_w', 'new_v_w_out', 'new_v_final_norm_w']
TWIN_LEAF_KINDS = {'loss': 'loss', 'grad_x': 'grad_x', 'grad_norm_w': 'grad_w', 'grad_w_in': 'grad_w', 'grad_conv_qkv_w': 'grad_w', 'grad_a_log': 'grad_w', 'grad_dt_bias': 'grad_w', 'grad_gdn_norm_w': 'grad_w', 'grad_conf_dw_w': 'grad_w', 'grad_conf_dw_b': 'grad_w', 'grad_conf_ln_w': 'grad_w', 'grad_conf_ln_b': 'grad_w', 'grad_conf_pw_w': 'grad_w', 'grad_w_out': 'grad_w', 'grad_final_norm_w': 'grad_w', 'delta_norm_w': 'delta_w', 'delta_w_in': 'delta_w', 'delta_conv_qkv_w': 'delta_w', 'delta_a_log': 'delta_w', 'delta_dt_bias': 'delta_w', 'delta_gdn_norm_w': 'delta_w', 'delta_conf_dw_w': 'delta_w', 'delta_conf_dw_b': 'delta_w', 'delta_conf_ln_w': 'delta_w', 'delta_conf_ln_b': 'delta_w', 'delta_conf_pw_w': 'delta_w', 'delta_w_out': 'delta_w', 'delta_final_norm_w': 'delta_w', 'new_m_norm_w': 'new_m', 'new_m_w_in': 'new_m', 'new_m_conv_qkv_w': 'new_m', 'new_m_a_log': 'new_m', 'new_m_dt_bias': 'new_m', 'new_m_gdn_norm_w': 'new_m', 'new_m_conf_dw_w': 'new_m', 'new_m_conf_dw_b': 'new_m', 'new_m_conf_ln_w': 'new_m', 'new_m_conf_ln_b': 'new_m', 'new_m_conf_pw_w': 'new_m', 'new_m_w_out': 'new_m', 'new_m_final_norm_w': 'new_m', 'new_v_norm_w': 'new_v', 'new_v_w_in': 'new_v', 'new_v_conv_qkv_w': 'new_v', 'new_v_a_log': 'new_v', 'new_v_dt_bias': 'new_v', 'new_v_gdn_norm_w': 'new_v', 'new_v_conf_dw_w': 'new_v', 'new_v_conf_dw_b': 'new_v', 'new_v_conf_ln_w': 'new_v', 'new_v_conf_ln_b': 'new_v', 'new_v_conf_pw_w': 'new_v', 'new_v_w_out': 'new_v', 'new_v_final_norm_w': 'new_v'}


def _forward(args):
    return _fwd_reference(*[args[k] for k in FWD_PARAMS])


def _output_shape():
    def fwd():
        inp = _fwd_setup_inputs(0)
        return _fwd_reference(*[inp[k] for k in FWD_PARAMS])
    out = _jax.eval_shape(fwd)
    return out.shape, out.dtype

N_MICROBATCH = 1
ADAM_LR = 0.001
ADAM_B1 = 0.9
ADAM_B2 = 0.999
ADAM_EPS = 1e-08
ADAM_WD = 0.01
ADAM_STEP = 10
PER_EXAMPLE_BATCH_AXIS = {'x': 0, 'loss_target': 0}
SHARED_INPUTS = []
_WEIGHT_DTYPES = {'norm_w': _jnp.float32, 'w_in': _jnp.float32, 'conv_qkv_w': _jnp.float32, 'a_log': _jnp.float32, 'dt_bias': _jnp.float32, 'gdn_norm_w': _jnp.float32, 'conf_dw_w': _jnp.float32, 'conf_dw_b': _jnp.float32, 'conf_ln_w': _jnp.float32, 'conf_ln_b': _jnp.float32, 'conf_pw_w': _jnp.float32, 'w_out': _jnp.float32, 'final_norm_w': _jnp.float32}
MOMENT_SCALE = {'norm_w': 8.501785e-02, 'w_in': 4.386297e-02, 'conv_qkv_w': 5.663370e-02, 'a_log': 5.935492e-01, 'dt_bias': 5.891260e-01, 'gdn_norm_w': 2.091875e-01, 'conf_dw_w': 4.642329e-02, 'conf_dw_b': 1.044706e-01, 'conf_ln_w': 5.412238e-02, 'conf_ln_b': 4.757950e-02, 'conf_pw_w': 4.491822e-02, 'w_out': 5.074166e-02, 'final_norm_w': 3.195415e+01}


def _to_microbatches(a, axis):
    t = _jnp.moveaxis(a, axis, 0)
    t = t.reshape((N_MICROBATCH, t.shape[0] // N_MICROBATCH) + t.shape[1:])
    return _jnp.moveaxis(t, 1, axis + 1)


def setup_inputs(seed: int = 0) -> dict:
    inp = _fwd_setup_inputs(seed)
    key = _jax.random.fold_in(_jax.random.key(seed), 7919)
    shape, _ = _output_shape()
    out = dict(inp)
    out["loss_target"] = _jax.random.normal(_jax.random.fold_in(key, 0), shape, _jnp.float32)
    for i, name in enumerate(TWIN_WEIGHTS):
        w = inp[name].astype(_jnp.float32)
        if MOMENT_SCALE is None:
            s = _jnp.sqrt(_jnp.mean(_jnp.square(w)) + 1e-30)
        else:
            s = MOMENT_SCALE[name]
        km, kv = _jax.random.split(_jax.random.fold_in(key, i + 1))
        out[name] = w
        out["m_" + name] = s * _jax.random.normal(km, w.shape, _jnp.float32)
        out["v_" + name] = (s * s) * _jax.random.uniform(kv, w.shape, _jnp.float32, 0.5, 1.5)
    if N_MICROBATCH > 1:
        for name, axis in PER_EXAMPLE_BATCH_AXIS.items():
            out[name] = _to_microbatches(out[name], axis)
    return {'x': out['x'], 'norm_w': out['norm_w'], 'w_in': out['w_in'], 'conv_qkv_w': out['conv_qkv_w'], 'a_log': out['a_log'], 'dt_bias': out['dt_bias'], 'gdn_norm_w': out['gdn_norm_w'], 'conf_dw_w': out['conf_dw_w'], 'conf_dw_b': out['conf_dw_b'], 'conf_ln_w': out['conf_ln_w'], 'conf_ln_b': out['conf_ln_b'], 'conf_pw_w': out['conf_pw_w'], 'w_out': out['w_out'], 'final_norm_w': out['final_norm_w'], 'loss_target': out['loss_target'], 'm_norm_w': out['m_norm_w'], 'm_w_in': out['m_w_in'], 'm_conv_qkv_w': out['m_conv_qkv_w'], 'm_a_log': out['m_a_log'], 'm_dt_bias': out['m_dt_bias'], 'm_gdn_norm_w': out['m_gdn_norm_w'], 'm_conf_dw_w': out['m_conf_dw_w'], 'm_conf_dw_b': out['m_conf_dw_b'], 'm_conf_ln_w': out['m_conf_ln_w'], 'm_conf_ln_b': out['m_conf_ln_b'], 'm_conf_pw_w': out['m_conf_pw_w'], 'm_w_out': out['m_w_out'], 'm_final_norm_w': out['m_final_norm_w'], 'v_norm_w': out['v_norm_w'], 'v_w_in': out['v_w_in'], 'v_conv_qkv_w': out['v_conv_qkv_w'], 'v_a_log': out['v_a_log'], 'v_dt_bias': out['v_dt_bias'], 'v_gdn_norm_w': out['v_gdn_norm_w'], 'v_conf_dw_w': out['v_conf_dw_w'], 'v_conf_dw_b': out['v_conf_dw_b'], 'v_conf_ln_w': out['v_conf_ln_w'], 'v_conf_ln_b': out['v_conf_ln_b'], 'v_conf_pw_w': out['v_conf_pw_w'], 'v_w_out': out['v_w_out'], 'v_final_norm_w': out['v_final_norm_w']}


def _loss(weights, diff, rest, loss_target):
    with _jax.named_scope("forward"):
        args = {**rest, TWIN_DIFF_INPUT: diff, **{k: w.astype(_WEIGHT_DTYPES[k]) for k, w in weights.items()}}
        y = _forward(args)
    with _jax.named_scope("loss_head"):
        err = _jnp.square(y.astype(_jnp.float32) - loss_target)
        return 0.5 * _jnp.sum(_jnp.mean(err, axis=-1)) if err.ndim else 0.5 * err


def _adamw(w, g, m, v):
    m = ADAM_B1 * m + (1.0 - ADAM_B1) * g
    v = ADAM_B2 * v + (1.0 - ADAM_B2) * _jnp.square(g)
    m_hat = m / (1.0 - ADAM_B1 ** ADAM_STEP)
    v_hat = v / (1.0 - ADAM_B2 ** ADAM_STEP)
    delta = -ADAM_LR * (m_hat / (_jnp.sqrt(v_hat) + ADAM_EPS) + ADAM_WD * w)
    return delta, m, v


def reference(x, norm_w, w_in, conv_qkv_w, a_log, dt_bias, gdn_norm_w, conf_dw_w, conf_dw_b, conf_ln_w, conf_ln_b, conf_pw_w, w_out, final_norm_w, loss_target, m_norm_w, m_w_in, m_conv_qkv_w, m_a_log, m_dt_bias, m_gdn_norm_w, m_conf_dw_w, m_conf_dw_b, m_conf_ln_w, m_conf_ln_b, m_conf_pw_w, m_w_out, m_final_norm_w, v_norm_w, v_w_in, v_conv_qkv_w, v_a_log, v_dt_bias, v_gdn_norm_w, v_conf_dw_w, v_conf_dw_b, v_conf_ln_w, v_conf_ln_b, v_conf_pw_w, v_w_out, v_final_norm_w):
    given = dict(x=x, norm_w=norm_w, w_in=w_in, conv_qkv_w=conv_qkv_w, a_log=a_log, dt_bias=dt_bias, gdn_norm_w=gdn_norm_w, conf_dw_w=conf_dw_w, conf_dw_b=conf_dw_b, conf_ln_w=conf_ln_w, conf_ln_b=conf_ln_b, conf_pw_w=conf_pw_w, w_out=w_out, final_norm_w=final_norm_w, loss_target=loss_target, m_norm_w=m_norm_w, m_w_in=m_w_in, m_conv_qkv_w=m_conv_qkv_w, m_a_log=m_a_log, m_dt_bias=m_dt_bias, m_gdn_norm_w=m_gdn_norm_w, m_conf_dw_w=m_conf_dw_w, m_conf_dw_b=m_conf_dw_b, m_conf_ln_w=m_conf_ln_w, m_conf_ln_b=m_conf_ln_b, m_conf_pw_w=m_conf_pw_w, m_w_out=m_w_out, m_final_norm_w=m_final_norm_w, v_norm_w=v_norm_w, v_w_in=v_w_in, v_conv_qkv_w=v_conv_qkv_w, v_a_log=v_a_log, v_dt_bias=v_dt_bias, v_gdn_norm_w=v_gdn_norm_w, v_conf_dw_w=v_conf_dw_w, v_conf_dw_b=v_conf_dw_b, v_conf_ln_w=v_conf_ln_w, v_conf_ln_b=v_conf_ln_b, v_conf_pw_w=v_conf_pw_w, v_w_out=v_w_out, v_final_norm_w=v_final_norm_w)
    weights = {n: given[n] for n in TWIN_WEIGHTS}
    shared = {n: given[n] for n in SHARED_INPUTS}
    per_example = {n: given[n] for n in ['x']}
    grad_fn = _jax.value_and_grad(_loss, argnums=(0, 1))

    def one_microbatch(ex, loss_target):
        ex = dict(ex)
        diff = ex.pop(TWIN_DIFF_INPUT)
        return grad_fn(weights, diff, {**shared, **ex}, loss_target)

    if N_MICROBATCH == 1:
        loss, (grad_w, grad_x) = one_microbatch(per_example, given["loss_target"])
    else:
        def body(carry, xs):
            loss_sum, grad_sum = carry
            l_k, (gw_k, gx_k) = one_microbatch(xs[0], xs[1])
            with _jax.named_scope("update"):
                return (loss_sum + l_k, _jax.tree.map(_jnp.add, grad_sum, gw_k)), gx_k

        init = (_jnp.zeros((), _jnp.float32), _jax.tree.map(_jnp.zeros_like, weights))
        (loss, grad_w), grad_x = _jax.lax.scan(body, init, (per_example, given["loss_target"]))
    with _jax.named_scope("update"):
        delta_w, new_m, new_v = {}, {}, {}
        for n in TWIN_WEIGHTS:
            delta_w[n], new_m[n], new_v[n] = _adamw(weights[n], grad_w[n], given["m_" + n], given["v_" + n])
    return (loss, grad_x, *[grad_w[n] for n in TWIN_WEIGHTS], *[delta_w[n] for n in TWIN_WEIGHTS],
            *[new_m[n] for n in TWIN_WEIGHTS], *[new_v[n] for n in TWIN_WEIGHTS])
```

```python
import functools
import math

import jax
import jax.numpy as jnp
from jax import lax
from jax.experimental import pallas as pl
from jax.experimental.pallas import tpu as pltpu

F32, BF16 = jnp.float32, jnp.bfloat16
HIGHEST = lax.Precision.HIGHEST
MESH = pl.DeviceIdType.MESH

D_MODEL = 2048
DEPTH = 4
CONV_CH = 512
GDN_W = 768
GDN_HEADS = 6
GDN_D = 128
ATT_W = 768
ATT_HEADS = 12
ATT_HD = 64
CONV_WIDTH = 31
SHORT_CONV = 4
GDN_CHUNK = 64
ROPE_THETA = 500000.0
ROPE_DIM = ATT_HD // 4
DIL_PATTERNS = ((128, 1), (512, 4), (2048, 16))
ATT_BLOCK = 128
NEG_INF = -1e30
IN_W = 7692

IN_WP = 8192
COL_BA = 7680
ORIG_BA = 4608
ORIG_ATT = 4620

ADAM_LR = 0.001
ADAM_B1 = 0.9
ADAM_B2 = 0.999
ADAM_EPS = 1e-08
ADAM_WD = 0.01
ADAM_STEP = 10

VMEM_LIMIT = 56 * 1024 * 1024


def _params(sem=None):
    return pltpu.CompilerParams(dimension_semantics=sem, vmem_limit_bytes=VMEM_LIMIT)


def _sigmoid(x):
    return 1.0 / (1.0 + jnp.exp(-x))


def _silu(x):
    return x * _sigmoid(x)


def _dsilu(x):
    s = _sigmoid(x)
    return s * (1.0 + x * (1.0 - s))


def _dot(a, b, dims, precision=None):
    return lax.dot_general(a, b, (dims, ((), ())), precision=precision, preferred_element_type=F32)


def _nn(a, b, precision=None):
    return _dot(a, b, ((1,), (0,)), precision)


def _nt(a, b, precision=None):
    return _dot(a, b, ((1,), (1,)), precision)


def _tn(a, b, precision=None):
    return _dot(a, b, ((0,), (0,)), precision)


def _matmul(a, b, *, name, ta=False, tb=False, out_dtype=F32, add=None, tm=512, tn=1024, tk=1024):
    if ta:
        k_dim, m_dim = a.shape
    else:
        m_dim, k_dim = a.shape
    n_dim = b.shape[0] if tb else b.shape[1]
    tm, tn, tk = min(tm, m_dim), min(tn, n_dim), min(tk, k_dim)
    assert m_dim % tm == 0 and n_dim % tn == 0 and k_dim % tk == 0, (name, a.shape, b.shape)
    nk = k_dim // tk
    a_spec = pl.BlockSpec((tk, tm), lambda i, j, k: (k, i)) if ta else pl.BlockSpec((tm, tk), lambda i, j, k: (i, k))
    b_spec = pl.BlockSpec((tn, tk), lambda i, j, k: (j, k)) if tb else pl.BlockSpec((tk, tn), lambda i, j, k: (k, j))
    o_spec = pl.BlockSpec((tm, tn), lambda i, j, k: (i, j))
    dims = ((0 if ta else 1,), (1 if tb else 0,))
    has_add = add is not None

    def body(*refs):
        if has_add:
            a_ref, b_ref, add_ref, o_ref, acc_ref = refs
        else:
            a_ref, b_ref, o_ref, acc_ref = refs
        k = pl.program_id(2)

        @pl.when(k == 0)
        def _():
            acc_ref[...] = jnp.zeros_like(acc_ref)

        acc_ref[...] += _dot(a_ref[...].astype(BF16), b_ref[...].astype(BF16), dims)

        @pl.when(k == nk - 1)
        def _():
            r = acc_ref[...]
            if has_add:
                r = r + add_ref[...]
            o_ref[...] = r.astype(out_dtype)

    ins = (a, b, add) if has_add else (a, b)
    specs = [a_spec, b_spec] + ([o_spec] if has_add else [])
    return pl.pallas_call(
        body, name=name, grid=(m_dim // tm, n_dim // tn, nk), in_specs=specs, out_specs=o_spec,
        out_shape=jax.ShapeDtypeStruct((m_dim, n_dim), out_dtype),
        scratch_shapes=[pltpu.VMEM((tm, tn), F32)],
        compiler_params=_params(("parallel", "parallel", "arbitrary")),
    )(*ins)


def _rms_fwd(x, w, *, name, tm=256):
    s_len, d = x.shape

    def body(x_ref, w_ref, h_ref):
        xv = x_ref[...]
        r = lax.rsqrt(jnp.mean(xv * xv, axis=-1, keepdims=True) + 1e-6)
        h_ref[...] = (xv * r * w_ref[...]).astype(BF16)

    return pl.pallas_call(
        body, name=name, grid=(s_len // tm,),
        in_specs=[pl.BlockSpec((tm, d), lambda i: (i, 0)), pl.BlockSpec((1, d), lambda i: (0, 0))],
        out_specs=pl.BlockSpec((tm, d), lambda i: (i, 0)),
        out_shape=jax.ShapeDtypeStruct((s_len, d), BF16),
        compiler_params=_params(("parallel",)),
    )(x, w)


def _rms_bwd(x, dh, w, dres, *, name, tm=256):
    s_len, d = x.shape
    nsteps = s_len // tm

    def body(x_ref, dh_ref, w_ref, dres_ref, dx_ref, dw_ref, acc_ref):
        i = pl.program_id(0)

        @pl.when(i == 0)
        def _():
            acc_ref[...] = jnp.zeros_like(acc_ref)

        xv = x_ref[...]
        r = lax.rsqrt(jnp.mean(xv * xv, axis=-1, keepdims=True) + 1e-6)
        xn = xv * r
        dy = dh_ref[...]
        dxn = dy * w_ref[...]
        dx_ref[...] = dres_ref[...] + r * (dxn - xn * jnp.mean(dxn * xn, axis=-1, keepdims=True))
        acc_ref[...] += (dy * xn).reshape(tm // 8, 8, d).sum(axis=0)

        @pl.when(i == nsteps - 1)
        def _():
            dw_ref[...] = jnp.sum(acc_ref[...], axis=0, keepdims=True)

    row = pl.BlockSpec((tm, d), lambda i: (i, 0))
    vec = pl.BlockSpec((1, d), lambda i: (0, 0))
    return pl.pallas_call(
        body, name=name, grid=(nsteps,), in_specs=[row, row, vec, row], out_specs=[row, vec],
        out_shape=[jax.ShapeDtypeStruct((s_len, d), F32), jax.ShapeDtypeStruct((1, d), F32)],
        scratch_shapes=[pltpu.VMEM((8, d), F32)],
        compiler_params=_params(("arbitrary",)),
    )(x, dh, w, dres)


def _loss_head(x, w, target, *, name, tm=256):
    s_len, d = x.shape
    nsteps = s_len // tm

    def body(x_ref, w_ref, t_ref, dx_ref, dw_ref, loss_ref, acc_ref, lacc_ref):
        i = pl.program_id(0)

        @pl.when(i == 0)
        def _():
            acc_ref[...] = jnp.zeros_like(acc_ref)
            lacc_ref[...] = jnp.zeros_like(lacc_ref)

        xv = x_ref[...]
        wv = w_ref[...]
        r = lax.rsqrt(jnp.mean(xv * xv, axis=-1, keepdims=True) + 1e-6)
        xn = xv * r
        err = xn * wv - t_ref[...]
        lacc_ref[...] += (err * err).reshape(tm // 8, 8, d).sum(axis=0)
        dy = err * (1.0 / d)
        dxn = dy * wv
        dx_ref[...] = r * (dxn - xn * jnp.mean(dxn * xn, axis=-1, keepdims=True))
        acc_ref[...] += (dy * xn).reshape(tm // 8, 8, d).sum(axis=0)

        @pl.when(i == nsteps - 1)
        def _():
            dw_ref[...] = jnp.sum(acc_ref[...], axis=0, keepdims=True)
            tot = jnp.sum(jnp.sum(lacc_ref[...], axis=0, keepdims=True), axis=1, keepdims=True)
            loss_ref[...] = jnp.broadcast_to(tot * (0.5 / d), (1, 128))

    row = pl.BlockSpec((tm, d), lambda i: (i, 0))
    vec = pl.BlockSpec((1, d), lambda i: (0, 0))
    return pl.pallas_call(
        body, name=name, grid=(nsteps,), in_specs=[row, vec, row],
        out_specs=[row, vec, pl.BlockSpec((1, 128), lambda i: (0, 0))],
        out_shape=[jax.ShapeDtypeStruct((s_len, d), F32), jax.ShapeDtypeStruct((1, d), F32),
                   jax.ShapeDtypeStruct((1, 128), F32)],
        scratch_shapes=[pltpu.VMEM((8, d), F32), pltpu.VMEM((8, d), F32)],
        compiler_params=_params(("arbitrary",)),
    )(x, w, target)


def _rows_block(shape, tr=256):
    lead, rows, cols = shape
    if rows % tr != 0:
        assert rows * cols <= 1 << 20, shape
        tr = rows
    return (lead, rows // tr), pl.BlockSpec((1, tr, cols), lambda a, i: (a, i, 0))


def _adamw(w, g, m, v, *, name):
    grid, spec = _rows_block(w.shape)
    c1 = 1.0 / (1.0 - ADAM_B1 ** ADAM_STEP)
    c2 = 1.0 / (1.0 - ADAM_B2 ** ADAM_STEP)

    def body(w_ref, g_ref, m_ref, v_ref, d_ref, nm_ref, nv_ref):
        gv = g_ref[...]
        nm = ADAM_B1 * m_ref[...] + (1.0 - ADAM_B1) * gv
        nv = ADAM_B2 * v_ref[...] + (1.0 - ADAM_B2) * (gv * gv)
        nm_ref[...] = nm
        nv_ref[...] = nv
        d_ref[...] = -ADAM_LR * ((nm * c1) / (jnp.sqrt(nv * c2) + ADAM_EPS) + ADAM_WD * w_ref[...])

    out = jax.ShapeDtypeStruct(w.shape, F32)
    return pl.pallas_call(
        body, name=name, grid=grid, in_specs=[spec] * 4, out_specs=[spec] * 3, out_shape=[out] * 3,
        compiler_params=_params(("parallel", "parallel")),
    )(w, g, m, v)


def _sum_arrays(arrs, *, name, out_dtype):
    grid, spec = _rows_block(arrs[0].shape)
    n = len(arrs)

    def body(*refs):
        acc = refs[0][...].astype(F32)
        for r in refs[1:n]:
            acc = acc + r[...].astype(F32)
        refs[n][...] = acc.astype(out_dtype)

    return pl.pallas_call(
        body, name=name, grid=grid, in_specs=[spec] * n, out_specs=spec,
        out_shape=jax.ShapeDtypeStruct(arrs[0].shape, out_dtype),
        compiler_params=_params(("parallel", "parallel")),
    )(*arrs)


HALO = 32


def _conf_fwd(u, dw_w, dw_b, ln_w, ln_b, *, name, tm=256):
    s_len = u.shape[0]
    c = CONV_CH

    def body(uc_ref, up_ref, dww_ref, dwb_ref, lnw_ref, lnb_ref, conv_ref, sw_ref, hbuf):
        i = pl.program_id(0)
        hbuf[HALO:, :] = uc_ref[:, :c] * _sigmoid(uc_ref[:, c:])
        hp = up_ref[:, :c] * _sigmoid(up_ref[:, c:])
        hbuf[:HALO, :] = jnp.where(i > 0, hp, 0.0)
        for cb in range(c // 128):
            cs = slice(128 * cb, 128 * (cb + 1))
            acc = jnp.zeros((tm, 128), F32)
            for j in range(CONV_WIDTH):
                acc = acc + hbuf[HALO - CONV_WIDTH + 1 + j:HALO - CONV_WIDTH + 1 + j + tm, cs] * dww_ref[j:j + 1, cs]
            conv_ref[:, cs] = acc + dwb_ref[:, cs]
        cv = conv_ref[...]
        mu = jnp.mean(cv, axis=-1, keepdims=True)
        xc = cv - mu
        var = jnp.mean(xc * xc, axis=-1, keepdims=True)
        ln = xc * lax.rsqrt(var + 1e-5) * lnw_ref[...] + lnb_ref[...]
        sw_ref[...] = _silu(ln).astype(BF16)

    vec = pl.BlockSpec((1, c), lambda i: (0, 0))
    return pl.pallas_call(
        body, name=name, grid=(s_len // tm,),
        in_specs=[pl.BlockSpec((tm, 2 * c), lambda i: (i, 0)),
                  pl.BlockSpec((HALO, 2 * c), lambda i: (jnp.maximum(i * (tm // HALO) - 1, 0), 0)),
                  pl.BlockSpec((HALO, c), lambda i: (0, 0)), vec, vec, vec],
        out_specs=[pl.BlockSpec((tm, c), lambda i: (i, 0))] * 2,
        out_shape=[jax.ShapeDtypeStruct((s_len, c), F32), jax.ShapeDtypeStruct((s_len, c), BF16)],
        scratch_shapes=[pltpu.VMEM((tm + HALO, c), F32)],
        compiler_params=_params(("parallel",)),
    )(u, u, dw_w, dw_b, ln_w, ln_b)


def _conf_bwd_ln(d_sw, conv, ln_w, ln_b, *, name, tm=256):
    s_len, c = conv.shape
    nsteps = s_len // tm

    def body(dsw_ref, conv_ref, lnw_ref, lnb_ref, dconv_ref, sums_ref):
        i = pl.program_id(0)

        @pl.when(i == 0)
        def _():
            sums_ref[...] = jnp.zeros_like(sums_ref)

        cv = conv_ref[...]
        mu = jnp.mean(cv, axis=-1, keepdims=True)
        xc = cv - mu
        rs = lax.rsqrt(jnp.mean(xc * xc, axis=-1, keepdims=True) + 1e-5)
        xhat = xc * rs
        lnw = lnw_ref[...]
        ln = xhat * lnw + lnb_ref[...]
        dln = dsw_ref[...] * _dsilu(ln)
        dxh = dln * lnw
        dconv = rs * (dxh - jnp.mean(dxh, axis=-1, keepdims=True)
                      - xhat * jnp.mean(dxh * xhat, axis=-1, keepdims=True))
        dconv_ref[...] = dconv
        sums_ref[0:1, :] += jnp.sum(dln * xhat, axis=0, keepdims=True)
        sums_ref[1:2, :] += jnp.sum(dln, axis=0, keepdims=True)
        sums_ref[2:3, :] += jnp.sum(dconv, axis=0, keepdims=True)

    row = pl.BlockSpec((tm, c), lambda i: (i, 0))
    vec = pl.BlockSpec((1, c), lambda i: (0, 0))
    return pl.pallas_call(
        body, name=name, grid=(nsteps,), in_specs=[row, row, vec, vec],
        out_specs=[row, pl.BlockSpec((8, c), lambda i: (0, 0))],
        out_shape=[jax.ShapeDtypeStruct((s_len, c), F32), jax.ShapeDtypeStruct((8, c), F32)],
        compiler_params=_params(("arbitrary",)),
    )(d_sw, conv, ln_w, ln_b)


def _conf_bwd_conv(u, dconv, dw_w, du, *, name, tm=256):
    s_len = u.shape[0]
    c = CONV_CH
    nsteps = s_len // tm
    off = HALO - CONV_WIDTH + 1

    def body(uc_ref, up_ref, dc_ref, dn_ref, dww_ref, du_in_ref, du_ref, ddw_ref, hbuf, dbuf, wacc):
        del du_in_ref
        i = pl.program_id(0)

        @pl.when(i == 0)
        def _():
            wacc[...] = jnp.zeros_like(wacc)

        hbuf[HALO:, :] = uc_ref[:, :c] * _sigmoid(uc_ref[:, c:])
        hp = up_ref[:, :c] * _sigmoid(up_ref[:, c:])
        hbuf[:HALO, :] = jnp.where(i > 0, hp, 0.0)
        dbuf[:tm, :] = dc_ref[...]
        dbuf[tm:, :] = jnp.where(i < nsteps - 1, dn_ref[...], 0.0)
        for cb in range(c // 128):
            cs = slice(128 * cb, 128 * (cb + 1))
            dcur = dbuf[0:tm, cs]
            acc = jnp.zeros((tm, 128), F32)
            for j in range(CONV_WIDTH):
                k = CONV_WIDTH - 1 - j
                acc = acc + dbuf[k:k + tm, cs] * dww_ref[j:j + 1, cs]
                prod = hbuf[off + j:off + j + tm, cs] * dcur
                wacc[j, :, cs] += prod.reshape(tm // 8, 8, 128).sum(axis=0)
            a = uc_ref[:, cs]
            sg = _sigmoid(uc_ref[:, c + 128 * cb:c + 128 * (cb + 1)])
            du_ref[:, cs] = acc * sg
            du_ref[:, c + 128 * cb:c + 128 * (cb + 1)] = acc * a * sg * (1.0 - sg)

        @pl.when(i == nsteps - 1)
        def _():
            for j in range(CONV_WIDTH):
                ddw_ref[j:j + 1, :] = jnp.sum(wacc[j], axis=0, keepdims=True)
            ddw_ref[CONV_WIDTH:, :] = jnp.zeros((HALO - CONV_WIDTH, c), F32)

    return pl.pallas_call(
        body, name=name, grid=(nsteps,),
        in_specs=[pl.BlockSpec((tm, 2 * c), lambda i: (i, 0)),
                  pl.BlockSpec((HALO, 2 * c), lambda i: (jnp.maximum(i * (tm // HALO) - 1, 0), 0)),
                  pl.BlockSpec((tm, c), lambda i: (i, 0)),
                  pl.BlockSpec((HALO, c), lambda i: (jnp.minimum((i + 1) * (tm // HALO), s_len // HALO - 1), 0)),
                  pl.BlockSpec((HALO, c), lambda i: (0, 0)),
                  pl.BlockSpec(memory_space=pl.ANY)],
        out_specs=[pl.BlockSpec((tm, 2 * c), lambda i: (i, 0)), pl.BlockSpec((HALO, c), lambda i: (0, 0))],
        out_shape=[jax.ShapeDtypeStruct(du.shape, F32), jax.ShapeDtypeStruct((HALO, c), F32)],
        scratch_shapes=[pltpu.VMEM((tm + HALO, c), F32), pltpu.VMEM((tm + HALO, c), F32),
                        pltpu.VMEM((CONV_WIDTH, 8, c), F32)],
        input_output_aliases={5: 0},
        compiler_params=_params(("arbitrary",)),
    )(u, u, dconv, dconv, dw_w, du)


COL_GQ = 1536 // GDN_W
COL_AQ = 4608 // ATT_W
SHALO = 8


def _softplus(z):
    return jnp.maximum(z, 0.0) + jnp.log1p(jnp.exp(-jnp.abs(z)))


def _short_conv(buf, cw_ref, part, rows, first):
    acc = jnp.zeros((rows, GDN_W), F32)
    for j in range(SHORT_CONV):
        acc = acc + buf[first + j:first + j + rows, :] * cw_ref[j:j + 1, GDN_W * part:GDN_W * (part + 1)]
    return acc


def _gdn_prep_fwd(u, cw, al, dtb, *, name, tm=256):
    s_len = u.shape[0]
    first = SHALO - SHORT_CONV + 1

    def body(uq, uk, uv, pq, pk, pv, uba, cw_ref, al_ref, dtb_ref, qn_ref, kn_ref, vc_ref, bg_ref, buf):
        i = pl.program_id(0)

        def conv(cur, prev, part):
            buf[SHALO:, :] = cur[...]
            buf[:SHALO, :] = jnp.where(i > 0, prev[...], 0.0)
            return _silu(_short_conv(buf, cw_ref, part, tm, first))

        for part, (cur, prev, out, scale) in enumerate(
                ((uq, pq, qn_ref, GDN_D ** -0.5), (uk, pk, kn_ref, 1.0))):
            y = conv(cur, prev, part)
            for h in range(GDN_HEADS):
                hs = slice(GDN_D * h, GDN_D * (h + 1))
                yh = y[:, hs]
                out[:, hs] = yh * (lax.rsqrt(jnp.sum(yh * yh, axis=-1, keepdims=True) + 1e-6) * scale)
        vc_ref[...] = conv(uv, pv, 2)
        ba = uba[...]
        lane = lax.broadcasted_iota(jnp.int32, ba.shape, 1)
        g = -jnp.exp(al_ref[...]) * _softplus(ba + dtb_ref[...])
        bg_ref[...] = jnp.where(lane < GDN_HEADS, _sigmoid(ba), jnp.where(lane < 2 * GDN_HEADS, g, 0.0))

    def cur(col):
        return pl.BlockSpec((tm, GDN_W), lambda i: (i, col))

    def prev(col):
        return pl.BlockSpec((SHALO, GDN_W), lambda i: (jnp.maximum(i * (tm // SHALO) - 1, 0), col))

    vec = pl.BlockSpec((1, 128), lambda i: (0, 0))
    row = pl.BlockSpec((tm, GDN_W), lambda i: (i, 0))
    wide = jax.ShapeDtypeStruct((s_len, GDN_W), F32)
    return pl.pallas_call(
        body, name=name, grid=(s_len // tm,),
        in_specs=[cur(COL_GQ), cur(COL_GQ + 1), cur(COL_GQ + 2), prev(COL_GQ), prev(COL_GQ + 1), prev(COL_GQ + 2),
                  pl.BlockSpec((tm, 128), lambda i: (i, COL_BA // 128)),
                  pl.BlockSpec((SHALO, 3 * GDN_W), lambda i: (0, 0)), vec, vec],
        out_specs=[row, row, row, pl.BlockSpec((tm, 128), lambda i: (i, 0))],
        out_shape=[wide, wide, wide, jax.ShapeDtypeStruct((s_len, 128), F32)],
        scratch_shapes=[pltpu.VMEM((tm + SHALO, GDN_W), F32)],
        compiler_params=_params(("parallel",)),
    )(u, u, u, u, u, u, u, cw, al, dtb)


def _chunk_masks():
    c = GDN_CHUNK
    row = lax.broadcasted_iota(jnp.int32, (c, c), 0)
    col = lax.broadcasted_iota(jnp.int32, (c, c), 1)
    return row >= col, row > col


def _cum_decay(bg):
    c = GDN_CHUNK
    causal, _ = _chunk_masks()
    g_cum = _nn(causal.astype(F32), bg, HIGHEST)
    sel = (lax.broadcasted_iota(jnp.int32, (8, 128), 0) + GDN_HEADS
           == lax.broadcasted_iota(jnp.int32, (8, 128), 1)).astype(F32)
    return g_cum, _nt(sel, g_cum, HIGHEST)


def _head_chunk(g_cum, g_rows, bg, h):
    causal, _ = _chunk_masks()
    gc = g_cum[:, GDN_HEADS + h:GDN_HEADS + h + 1]
    diff = gc - g_rows[h:h + 1, :]
    dec = jnp.where(causal, jnp.exp(jnp.where(causal, diff, 0.0)), 0.0)
    return gc, bg[:, h:h + 1], dec


def _gdn_intra_fwd(qn, kn, vc, bg, *, name):
    s_len = qn.shape[0]
    c = GDN_CHUNK
    nchunks = s_len // c

    def body(q_ref, k_ref, v_ref, bg_ref, wk_ref, wv_ref, qd_ref, kd_ref, p_ref, t_ref, g_ref):
        causal, strict = _chunk_masks()
        eye = (lax.broadcasted_iota(jnp.int32, (c, c), 0) == lax.broadcasted_iota(jnp.int32, (c, c), 1)).astype(F32)
        bg = bg_ref[...]
        g_cum, g_rows = _cum_decay(bg)
        g_ref[...] = g_cum
        for h in range(GDN_HEADS):
            hs = slice(GDN_D * h, GDN_D * (h + 1))
            gc, bc, dec = _head_chunk(g_cum, g_rows, bg, h)
            q, k, v = q_ref[:, hs], k_ref[:, hs], v_ref[:, hs]
            k16 = k.astype(BF16)
            low = jnp.where(strict, bc * _nt(k16, k16) * dec, 0.0)
            pw = -low
            t = eye + pw
            for _ in range(5):
                pw = _nn(pw, pw, HIGHEST)
                t = t + _nn(t, pw, HIGHEST)
            t_ref[h] = t
            t16 = t.astype(BF16)
            eg = jnp.exp(gc)
            wk_ref[:, hs] = _nn(t16, (k * (bc * eg)).astype(BF16)).astype(BF16)
            wv_ref[:, hs] = _nn(t16, (v * bc).astype(BF16))
            p_ref[h] = jnp.where(causal, _nt(q.astype(BF16), k16) * dec, 0.0).astype(BF16)
            g_last = g_cum[c - 1:c, GDN_HEADS + h:GDN_HEADS + h + 1]
            qd_ref[:, hs] = (q * eg).astype(BF16)
            kd_ref[:, hs] = (k * jnp.exp(g_last - gc)).astype(BF16)

    row = pl.BlockSpec((c, GDN_W), lambda n: (n, 0))
    sq = pl.BlockSpec((GDN_HEADS, c, c), lambda n: (0, n, 0))
    narrow = pl.BlockSpec((c, 128), lambda n: (n, 0))
    w16 = jax.ShapeDtypeStruct((s_len, GDN_W), BF16)
    return pl.pallas_call(
        body, name=name, grid=(nchunks,), in_specs=[row, row, row, narrow],
        out_specs=[row, row, row, row, sq, sq, narrow],
        out_shape=[w16, jax.ShapeDtypeStruct((s_len, GDN_W), F32), w16, w16,
                   jax.ShapeDtypeStruct((GDN_HEADS, s_len, c), BF16),
                   jax.ShapeDtypeStruct((GDN_HEADS, s_len, c), F32),
                   jax.ShapeDtypeStruct((s_len, 128), F32)],
        compiler_params=_params(("parallel",)),
    )(qn, kn, vc, bg)


def _gdn_scan_fwd(wk, wv, qd, kd, p, g_cum, *, name):
    s_len = wk.shape[0]
    c = GDN_CHUNK
    nchunks = s_len // c

    def body(wk_ref, wv_ref, qd_ref, kd_ref, p_ref, g_ref, o_ref, vn_ref, sp_ref, st):
        @pl.when(pl.program_id(0) == 0)
        def _():
            st[...] = jnp.zeros_like(st)

        for h in range(GDN_HEADS):
            hs = slice(GDN_D * h, GDN_D * (h + 1))
            s = st[h]
            sp_ref[0, h] = s
            s16 = s.astype(BF16)
            vn16 = (wv_ref[:, hs] - _nn(wk_ref[:, hs], s16)).astype(BF16)
            vn_ref[:, hs] = vn16
            o_ref[:, hs] = _nn(qd_ref[:, hs], s16) + _nn(p_ref[h], vn16)
            gl = jnp.exp(g_ref[c - 1:c, GDN_HEADS + h:GDN_HEADS + h + 1])
            st[h] = s * gl + _tn(kd_ref[:, hs], vn16)

    row = pl.BlockSpec((c, GDN_W), lambda n: (n, 0))
    return pl.pallas_call(
        body, name=name, grid=(nchunks,),
        in_specs=[row, row, row, row, pl.BlockSpec((GDN_HEADS, c, c), lambda n: (0, n, 0)),
                  pl.BlockSpec((c, 128), lambda n: (n, 0))],
        out_specs=[row, row, pl.BlockSpec((1, GDN_HEADS, GDN_D, GDN_D), lambda n: (n, 0, 0, 0))],
        out_shape=[jax.ShapeDtypeStruct((s_len, GDN_W), F32), jax.ShapeDtypeStruct((s_len, GDN_W), BF16),
                   jax.ShapeDtypeStruct((nchunks, GDN_HEADS, GDN_D, GDN_D), F32)],
        scratch_shapes=[pltpu.VMEM((GDN_HEADS, GDN_D, GDN_D), F32)],
        compiler_params=_params(("arbitrary",)),
    )(wk, wv, qd, kd, p, g_cum)


def _gdn_scan_bwd(do, wk, qd, kd, p, g_cum, *, name):
    s_len = wk.shape[0]
    c = GDN_CHUNK
    nchunks = s_len // c

    def body(do_ref, wk_ref, qd_ref, kd_ref, p_ref, g_ref, dvn_ref, ds_ref, dst):
        @pl.when(pl.program_id(0) == 0)
        def _():
            dst[...] = jnp.zeros_like(dst)

        for h in range(GDN_HEADS):
            hs = slice(GDN_D * h, GDN_D * (h + 1))
            ds = dst[h]
            ds_ref[0, h] = ds
            do16 = do_ref[:, hs].astype(BF16)
            dvn16 = (_tn(p_ref[h], do16) + _nn(kd_ref[:, hs], ds.astype(BF16))).astype(BF16)
            dvn_ref[:, hs] = dvn16
            gl = jnp.exp(g_ref[c - 1:c, GDN_HEADS + h:GDN_HEADS + h + 1])
            dst[h] = _tn(qd_ref[:, hs], do16) + ds * gl - _tn(wk_ref[:, hs], dvn16)

    row = pl.BlockSpec((c, GDN_W), lambda n: (nchunks - 1 - n, 0))
    return pl.pallas_call(
        body, name=name, grid=(nchunks,),
        in_specs=[row, row, row, row, pl.BlockSpec((GDN_HEADS, c, c), lambda n: (0, nchunks - 1 - n, 0)),
                  pl.BlockSpec((c, 128), lambda n: (nchunks - 1 - n, 0))],
        out_specs=[row, pl.BlockSpec((1, GDN_HEADS, GDN_D, GDN_D), lambda n: (nchunks - 1 - n, 0, 0, 0))],
        out_shape=[jax.ShapeDtypeStruct((s_len, GDN_W), BF16),
                   jax.ShapeDtypeStruct((nchunks, GDN_HEADS, GDN_D, GDN_D), F32)],
        scratch_shapes=[pltpu.VMEM((GDN_HEADS, GDN_D, GDN_D), F32)],
        compiler_params=_params(("arbitrary",)),
    )(do, wk, qd, kd, p, g_cum)


def _gdn_intra_bwd(qn, kn, vc, bg, g_cum, t, do, dvn, vn, sprev, ds_all, *, name):
    s_len = qn.shape[0]
    c = GDN_CHUNK
    nchunks = s_len // c

    def body(q_ref, k_ref, v_ref, bg_ref, g_ref, t_ref, do_ref, dvn_ref, vn_ref, sp_ref, ds_ref,
             dqkv_ref, dbg_ref):
        causal, strict = _chunk_masks()
        bg = bg_ref[...]
        g_cum = g_ref[...]
        _, g_rows = _cum_decay(bg)
        lane = lax.broadcasted_iota(jnp.int32, (c, 128), 1)
        rowi = lax.broadcasted_iota(jnp.int32, (c, 128), 0)
        ones = jnp.ones((c, 128), F32)
        dbeta_all = jnp.zeros((c, 128), F32)
        dg_all = jnp.zeros((c, 128), F32)
        for h in range(GDN_HEADS):
            hs = slice(GDN_D * h, GDN_D * (h + 1))
            gc, bc, dec = _head_chunk(g_cum, g_rows, bg, h)
            q, k, v = q_ref[:, hs], k_ref[:, hs], v_ref[:, hs]
            q16, k16 = q.astype(BF16), k.astype(BF16)
            kk = _nt(k16, k16)
            low = jnp.where(strict, bc * kk * dec, 0.0)
            eg = jnp.exp(gc)
            g_last = g_cum[c - 1:c, GDN_HEADS + h:GDN_HEADS + h + 1]
            kdec = jnp.exp(g_last - gc)
            kb, vb, qd, kd = k * (bc * eg), v * bc, q * eg, k * kdec
            pm = jnp.where(causal, _nt(q16, k16) * dec, 0.0)
            s = sp_ref[0, h]
            ds = ds_ref[0, h]
            s16, ds16 = s.astype(BF16), ds.astype(BF16)
            do16 = do_ref[:, hs].astype(BF16)
            dvn16, vn16 = dvn_ref[:, hs], vn_ref[:, hs]
            tm = t_ref[h]
            t16 = tm.astype(BF16)

            dqd = _nt(do16, s16)
            dp = jnp.where(causal, _nt(do16, vn16), 0.0)
            dkd = _nt(vn16, ds16)
            dgl = jnp.sum(jnp.sum(s * ds, axis=1, keepdims=True), axis=0, keepdims=True) * jnp.exp(g_last)
            dwk16 = (-_nt(dvn16, s16)).astype(BF16)
            dt = _nt(dwk16, kb.astype(BF16)) + _nt(dvn16, vb.astype(BF16))
            dkb = _tn(t16, dwk16)
            dvb = _tn(t16, dvn16)
            dlow = jnp.where(strict, -_nt(_tn(tm, dt, HIGHEST), tm, HIGHEST), 0.0)
            dkk16 = (dlow * bc * dec).astype(BF16)
            dqk16 = (dp * dec).astype(BF16)

            dqkv_ref[0, :, hs] = _nn(dqk16, k16) + dqd * eg
            dqkv_ref[1, :, hs] = (_tn(dqk16, q16) + _nn(dkk16, k16) + _tn(dkk16, k16)
                                  + dkb * (bc * eg) + dkd * kdec)
            dqkv_ref[2, :, hs] = dvb * bc

            dbeta = (jnp.sum(dlow * kk * dec, axis=1, keepdims=True)
                     + jnp.sum(dkb * k, axis=1, keepdims=True) * eg + jnp.sum(dvb * v, axis=1, keepdims=True))
            mm = dlow * low + dp * pm
            mm_pad = jnp.concatenate([mm, jnp.zeros((c, 128 - c), F32)], axis=1)
            col_sum = _tn(mm_pad, ones, HIGHEST)[:c, 0:1]
            dkd_sum = jnp.sum(dkd * kd, axis=1, keepdims=True)
            dg = (jnp.sum(mm, axis=1, keepdims=True) - col_sum + jnp.sum(dkb * kb, axis=1, keepdims=True)
                  + jnp.sum(dqd * qd, axis=1, keepdims=True) - dkd_sum)
            tail = jnp.sum(dkd_sum, axis=0, keepdims=True) + dgl
            dbeta_all = dbeta_all + jnp.where(lane == h, dbeta, 0.0)
            dg_all = dg_all + jnp.where(lane == GDN_HEADS + h, dg + jnp.where(rowi == c - 1, tail, 0.0), 0.0)
        upper = (lax.broadcasted_iota(jnp.int32, (c, c), 0) <= lax.broadcasted_iota(jnp.int32, (c, c), 1)).astype(F32)
        dbg_ref[...] = dbeta_all + _nn(upper, dg_all, HIGHEST)

    row = pl.BlockSpec((c, GDN_W), lambda n: (n, 0))
    narrow = pl.BlockSpec((c, 128), lambda n: (n, 0))
    state = pl.BlockSpec((1, GDN_HEADS, GDN_D, GDN_D), lambda n: (n, 0, 0, 0))
    return pl.pallas_call(
        body, name=name, grid=(nchunks,),
        in_specs=[row, row, row, narrow, narrow, pl.BlockSpec((GDN_HEADS, c, c), lambda n: (0, n, 0)),
                  row, row, row, state, state],
        out_specs=[pl.BlockSpec((3, c, GDN_W), lambda n: (0, n, 0)), narrow],
        out_shape=[jax.ShapeDtypeStruct((3, s_len, GDN_W), F32), jax.ShapeDtypeStruct((s_len, 128), F32)],
        compiler_params=_params(("parallel",)),
    )(qn, kn, vc, bg, g_cum, t, do, dvn, vn, sprev, ds_all)


def _gdn_prep_bwd(u, dqkv, cw, du, *, name, tm=256):
    s_len = u.shape[0]
    nsteps = s_len // tm
    ext = tm + SHALO

    def body(uc, up, un, dc, dn, cw_ref, du_in_ref, du_ref, dcw_ref, xbuf, dbuf, pbuf, wacc):
        del du_in_ref
        part = pl.program_id(0)
        i = pl.program_id(1)

        @pl.when(i == 0)
        def _():
            wacc[...] = jnp.zeros_like(wacc)

        xbuf[:SHALO, :] = jnp.where(i > 0, up[...], 0.0)
        xbuf[SHALO:SHALO + tm, :] = uc[...]
        xbuf[SHALO + tm:, :] = jnp.where(i < nsteps - 1, un[...], 0.0)
        dbuf[:tm, :] = dc[...]
        dbuf[tm:, :] = jnp.where(i < nsteps - 1, dn[...], 0.0)
        first = SHALO - SHORT_CONV + 1
        w = [cw_ref[j:j + 1, :] for j in range(SHORT_CONV)]
        pre = jnp.zeros((ext, GDN_W), F32)
        for j in range(SHORT_CONV):
            pre = pre + xbuf[first + j:first + j + ext, :] * w[j]
        y = _silu(pre)
        dout = dbuf[...]
        scale = jnp.where(part == 0, GDN_D ** -0.5, 1.0)
        for h in range(GDN_HEADS):
            hs = slice(GDN_D * h, GDN_D * (h + 1))
            yh, dh = y[:, hs], dout[:, hs]
            rs = lax.rsqrt(jnp.sum(yh * yh, axis=-1, keepdims=True) + 1e-6)
            dyn = scale * rs * (dh - yh * (rs * rs) * jnp.sum(dh * yh, axis=-1, keepdims=True))
            dy = jnp.where(part < 2, dyn, dh)
            pbuf[:, hs] = dy * _dsilu(pre[:, hs])
        acc = jnp.zeros((tm, GDN_W), F32)
        dpre = pbuf[0:tm, :]
        for j in range(SHORT_CONV):
            k = SHORT_CONV - 1 - j
            acc = acc + pbuf[k:k + tm, :] * w[j]
            wacc[j] += (xbuf[first + j:first + j + tm, :] * dpre).reshape(tm // 8, 8, GDN_W).sum(axis=0)
        du_ref[...] = acc

        @pl.when(i == nsteps - 1)
        def _():
            for j in range(SHORT_CONV):
                dcw_ref[j:j + 1, :] = jnp.sum(wacc[j], axis=0, keepdims=True)
            dcw_ref[SHORT_CONV:, :] = jnp.zeros((SHALO - SHORT_CONV, GDN_W), F32)

    per = tm // SHALO
    return pl.pallas_call(
        body, name=name, grid=(3, nsteps),
        in_specs=[pl.BlockSpec((tm, GDN_W), lambda p, i: (i, COL_GQ + p)),
                  pl.BlockSpec((SHALO, GDN_W), lambda p, i: (jnp.maximum(i * per - 1, 0), COL_GQ + p)),
                  pl.BlockSpec((SHALO, GDN_W), lambda p, i: (jnp.minimum((i + 1) * per, s_len // SHALO - 1), COL_GQ + p)),
                  pl.BlockSpec((None, tm, GDN_W), lambda p, i: (p, i, 0)),
                  pl.BlockSpec((None, SHALO, GDN_W), lambda p, i: (p, jnp.minimum((i + 1) * per, s_len // SHALO - 1), 0)),
                  pl.BlockSpec((SHALO, GDN_W), lambda p, i: (0, p)),
                  pl.BlockSpec(memory_space=pl.ANY)],
        out_specs=[pl.BlockSpec((tm, GDN_W), lambda p, i: (i, COL_GQ + p)),
                   pl.BlockSpec((SHALO, GDN_W), lambda p, i: (0, p))],
        out_shape=[jax.ShapeDtypeStruct(du.shape, F32), jax.ShapeDtypeStruct((SHALO, 3 * GDN_W), F32)],
        scratch_shapes=[pltpu.VMEM((tm + 2 * SHALO, GDN_W), F32), pltpu.VMEM((ext, GDN_W), F32),
                        pltpu.VMEM((ext, GDN_W), F32), pltpu.VMEM((SHORT_CONV, 8, GDN_W), F32)],
        input_output_aliases={6: 0},
        compiler_params=_params(("arbitrary", "arbitrary")),
    )(u, u, u, dqkv, dqkv, cw, du)


def _gdn_ba_bwd(u, dbg, al, dtb, du, *, name, tm=256):
    s_len = u.shape[0]
    nsteps = s_len // tm
    wpad = IN_WP - COL_BA

    def body(uba, dbg_ref, al_ref, dtb_ref, du_in_ref, du_ref, sums_ref):
        del du_in_ref
        i = pl.program_id(0)

        @pl.when(i == 0)
        def _():
            sums_ref[...] = jnp.zeros_like(sums_ref)

        ba = uba[...]
        dbg = dbg_ref[...]
        lane = lax.broadcasted_iota(jnp.int32, ba.shape, 1)
        is_g = (lane >= GDN_HEADS) & (lane < 2 * GDN_HEADS)
        beta = _sigmoid(ba)
        z = ba + dtb_ref[...]
        ea = jnp.exp(al_ref[...])
        g = -ea * _softplus(z)
        dz = jnp.where(is_g, dbg * (-ea) * _sigmoid(z), 0.0)
        du_ref[:, :128] = jnp.where(lane < GDN_HEADS, dbg * beta * (1.0 - beta), dz)
        du_ref[:, 128:] = jnp.zeros((tm, wpad - 128), F32)
        sums_ref[0:1, :] += jnp.sum(jnp.where(is_g, dbg * g, 0.0), axis=0, keepdims=True)
        sums_ref[1:2, :] += jnp.sum(dz, axis=0, keepdims=True)

    vec = pl.BlockSpec((1, 128), lambda i: (0, 0))
    return pl.pallas_call(
        body, name=name, grid=(nsteps,),
        in_specs=[pl.BlockSpec((tm, 128), lambda i: (i, COL_BA // 128)), pl.BlockSpec((tm, 128), lambda i: (i, 0)),
                  vec, vec, pl.BlockSpec(memory_space=pl.ANY)],
        out_specs=[pl.BlockSpec((tm, wpad), lambda i: (i, COL_BA // wpad)), pl.BlockSpec((8, 128), lambda i: (0, 0))],
        out_shape=[jax.ShapeDtypeStruct(du.shape, F32), jax.ShapeDtypeStruct((8, 128), F32)],
        input_output_aliases={4: 0},
        compiler_params=_params(("arbitrary",)),
    )(u, dbg, al, dtb, du)


def _rope_tables(s_len):
    half = ROPE_DIM // 2
    inv = ROPE_THETA ** (-jnp.arange(half, dtype=F32) / half)
    ang = jnp.arange(s_len, dtype=F32)[:, None] * inv[None, :]
    cos, sin = jnp.cos(ang), jnp.sin(ang)
    one = jnp.ones((s_len, ATT_HD - ROPE_DIM), F32)
    zero = jnp.zeros((s_len, ATT_HD - ROPE_DIM), F32)
    zh = jnp.zeros((s_len, half), F32)
    c = jnp.concatenate([cos, cos, one], axis=1)
    s1 = jnp.concatenate([-sin, zh, zero], axis=1)
    s2 = jnp.concatenate([zh, sin, zero], axis=1)
    return tuple(jnp.concatenate([t, t], axis=1) for t in (c, s1, s2))


def _rope(x, c, s1, s2):
    return x * c + pltpu.roll(x, 128 - ROPE_DIM // 2, 1) * s1 + pltpu.roll(x, ROPE_DIM // 2, 1) * s2


def _rope_t(dy, c, s1, s2):
    return dy * c + pltpu.roll(dy * s1, ROPE_DIM // 2, 1) + pltpu.roll(dy * s2, 128 - ROPE_DIM // 2, 1)


def _att_prep_fwd(u, tabs, *, name, tm=256):
    s_len = u.shape[0]
    scale = ATT_HD ** -0.5

    def body(uq, uk, uv, c_ref, s1_ref, s2_ref, q_ref, k_ref, v_ref):
        c, s1, s2 = c_ref[...], s1_ref[...], s2_ref[...]
        for b in range(ATT_W // 128):
            bs = slice(128 * b, 128 * (b + 1))
            q_ref[:, bs] = (_rope(uq[:, bs], c, s1, s2) * scale).astype(BF16)
            k_ref[:, bs] = _rope(uk[:, bs], c, s1, s2).astype(BF16)
        v_ref[...] = uv[...].astype(BF16)

    tab = pl.BlockSpec((tm, 128), lambda i: (i, 0))
    row = pl.BlockSpec((tm, ATT_W), lambda i: (i, 0))
    out = jax.ShapeDtypeStruct((s_len, ATT_W), BF16)
    return pl.pallas_call(
        body, name=name, grid=(s_len // tm,),
        in_specs=[pl.BlockSpec((tm, ATT_W), lambda i, col=COL_AQ + j: (i, col)) for j in range(3)] + [tab] * 3,
        out_specs=[row] * 3, out_shape=[out] * 3,
        compiler_params=_params(("parallel",)),
    )(u, u, u, *tabs)


def _head_lanes(h):
    lane = lax.broadcasted_iota(jnp.int32, (1, 128), 1)
    return (lane < ATT_HD) if h % 2 == 0 else (lane >= ATT_HD)


def _att_fwd(qr, kr, vb, dil, *, name):
    s_len = qr.shape[0]
    lr = s_len // dil
    nb = lr // ATT_BLOCK
    blk = ATT_BLOCK

    def body(q_ref, kp_ref, kc_ref, vp_ref, vc_ref, o_ref, lse_ref):
        n = pl.program_id(1)
        qi = lax.broadcasted_iota(jnp.int32, (blk, 2 * blk), 0)
        ki = lax.broadcasted_iota(jnp.int32, (blk, 2 * blk), 1)
        dist = qi + blk - ki
        valid = (dist >= 0) & (dist <= blk) & ((ki >= blk) | (n > 0))
        lane = lax.broadcasted_iota(jnp.int32, (blk, 128), 1)
        lse_all = jnp.zeros((blk, 128), F32)
        for hp in range(ATT_HEADS // 2):
            bs = slice(128 * hp, 128 * (hp + 1))
            qb = q_ref[:, bs]
            kb = jnp.concatenate([kp_ref[:, bs], kc_ref[:, bs]], axis=0)
            vv = jnp.concatenate([vp_ref[:, bs], vc_ref[:, bs]], axis=0)
            outs = []
            for sub in range(2):
                h = 2 * hp + sub
                qm = jnp.where(_head_lanes(h), qb, jnp.zeros_like(qb))
                s = jnp.where(valid, _nt(qm, kb), NEG_INF)
                m = jnp.max(s, axis=-1, keepdims=True)
                p = jnp.exp(s - m)
                l = jnp.sum(p, axis=-1, keepdims=True)
                outs.append(_nn((p / l).astype(BF16), vv))
                lse_all = lse_all + jnp.where(lane == h, m + jnp.log(l), 0.0)
            o_ref[:, bs] = jnp.where(lane < ATT_HD, outs[0], outs[1])
        lse_ref[...] = lse_all

    cur = pl.BlockSpec((blk, ATT_W), lambda r, n: (n, r))
    prev = pl.BlockSpec((blk, ATT_W), lambda r, n: (jnp.maximum(n - 1, 0), r))
    view = (lr, dil * ATT_W)
    o, lse = pl.pallas_call(
        body, name=name, grid=(dil, nb), in_specs=[cur, prev, cur, prev, cur],
        out_specs=[cur, pl.BlockSpec((blk, 128), lambda r, n: (n, r))],
        out_shape=[jax.ShapeDtypeStruct(view, F32), jax.ShapeDtypeStruct((lr, dil * 128), F32)],
        compiler_params=_params(("parallel", "parallel")),
    )(qr.reshape(view), kr.reshape(view), kr.reshape(view), vb.reshape(view), vb.reshape(view))
    return o.reshape(s_len, ATT_W), lse.reshape(s_len, 128)


def _att_bwd_dq(qr, kr, vb, do, lse, delta, dil, *, name):
    s_len = qr.shape[0]
    lr = s_len // dil
    nb = lr // ATT_BLOCK
    blk = ATT_BLOCK

    def body(q_ref, kp_ref, kc_ref, vp_ref, vc_ref, do_ref, lse_ref, dl_ref, dq_ref):
        n = pl.program_id(1)
        qi = lax.broadcasted_iota(jnp.int32, (blk, 2 * blk), 0)
        ki = lax.broadcasted_iota(jnp.int32, (blk, 2 * blk), 1)
        dist = qi + blk - ki
        valid = (dist >= 0) & (dist <= blk) & ((ki >= blk) | (n > 0))
        lane = lax.broadcasted_iota(jnp.int32, (blk, 128), 1)
        for hp in range(ATT_HEADS // 2):
            bs = slice(128 * hp, 128 * (hp + 1))
            qb = q_ref[:, bs]
            dob = do_ref[:, bs].astype(BF16)
            kb = jnp.concatenate([kp_ref[:, bs], kc_ref[:, bs]], axis=0)
            vv = jnp.concatenate([vp_ref[:, bs], vc_ref[:, bs]], axis=0)
            outs = []
            for sub in range(2):
                h = 2 * hp + sub
                hm = _head_lanes(h)
                s = _nt(jnp.where(hm, qb, jnp.zeros_like(qb)), kb)
                p = jnp.where(valid, jnp.exp(s - lse_ref[:, h:h + 1]), 0.0)
                dp = _nt(jnp.where(hm, dob, jnp.zeros_like(dob)), vv)
                outs.append(_nn((p * (dp - dl_ref[:, h:h + 1])).astype(BF16), kb))
            dq_ref[:, bs] = jnp.where(lane < ATT_HD, outs[0], outs[1])

    cur = pl.BlockSpec((blk, ATT_W), lambda r, n: (n, r))
    prev = pl.BlockSpec((blk, ATT_W), lambda r, n: (jnp.maximum(n - 1, 0), r))
    nar = pl.BlockSpec((blk, 128), lambda r, n: (n, r))
    view = (lr, dil * ATT_W)
    nview = (lr, dil * 128)
    dq = pl.pallas_call(
        body, name=name, grid=(dil, nb), in_specs=[cur, prev, cur, prev, cur, cur, nar, nar],
        out_specs=cur, out_shape=jax.ShapeDtypeStruct(view, F32),
        compiler_params=_params(("parallel", "parallel")),
    )(qr.reshape(view), kr.reshape(view), kr.reshape(view), vb.reshape(view), vb.reshape(view),
      do.reshape(view), lse.reshape(nview), delta.reshape(nview))
    return dq.reshape(s_len, ATT_W)


def _att_bwd_dkv(qr, kr, vb, do, lse, delta, dil, *, name):
    s_len = qr.shape[0]
    lr = s_len // dil
    nb = lr // ATT_BLOCK
    blk = ATT_BLOCK

    def body(k_ref, v_ref, qa_ref, qb_ref, doa_ref, dob_ref, la_ref, lb_ref, da_ref, db_ref, dk_ref, dv_ref):
        n = pl.program_id(1)
        qi = lax.broadcasted_iota(jnp.int32, (2 * blk, blk), 0)
        ki = lax.broadcasted_iota(jnp.int32, (2 * blk, blk), 1)
        dist = qi - ki
        valid = (dist >= 0) & (dist <= blk) & ((qi < blk) | (n < nb - 1))
        lse2 = jnp.concatenate([la_ref[...], lb_ref[...]], axis=0)
        dl2 = jnp.concatenate([da_ref[...], db_ref[...]], axis=0)
        for hp in range(ATT_HEADS // 2):
            bs = slice(128 * hp, 128 * (hp + 1))
            kb, vv = k_ref[:, bs], v_ref[:, bs]
            q2 = jnp.concatenate([qa_ref[:, bs], qb_ref[:, bs]], axis=0)
            do2 = jnp.concatenate([doa_ref[:, bs], dob_ref[:, bs]], axis=0).astype(BF16)
            dk_acc = jnp.zeros((blk, 128), F32)
            dv_acc = jnp.zeros((blk, 128), F32)
            for sub in range(2):
                h = 2 * hp + sub
                hm = _head_lanes(h)
                qm = jnp.where(hm, q2, jnp.zeros_like(q2))
                dom = jnp.where(hm, do2, jnp.zeros_like(do2))
                p = jnp.where(valid, jnp.exp(_nt(qm, kb) - lse2[:, h:h + 1]), 0.0)
                dv_acc = dv_acc + _tn(p.astype(BF16), dom)
                ds = p * (_nt(dom, vv) - dl2[:, h:h + 1])
                dk_acc = dk_acc + _tn(ds.astype(BF16), qm)
            dk_ref[:, bs] = dk_acc
            dv_ref[:, bs] = dv_acc

    cur = pl.BlockSpec((blk, ATT_W), lambda r, n: (n, r))
    nxt = pl.BlockSpec((blk, ATT_W), lambda r, n: (jnp.minimum(n + 1, nb - 1), r))
    ncur = pl.BlockSpec((blk, 128), lambda r, n: (n, r))
    nnxt = pl.BlockSpec((blk, 128), lambda r, n: (jnp.minimum(n + 1, nb - 1), r))
    view = (lr, dil * ATT_W)
    nview = (lr, dil * 128)
    qv, dov, lv, dlv = qr.reshape(view), do.reshape(view), lse.reshape(nview), delta.reshape(nview)
    dk, dv = pl.pallas_call(
        body, name=name, grid=(dil, nb),
        in_specs=[cur, cur, cur, nxt, cur, nxt, ncur, nnxt, ncur, nnxt],
        out_specs=[cur, cur], out_shape=[jax.ShapeDtypeStruct(view, F32)] * 2,
        compiler_params=_params(("parallel", "parallel")),
    )(kr.reshape(view), vb.reshape(view), qv, qv, dov, dov, lv, lv, dlv, dlv)
    return dk.reshape(s_len, ATT_W), dv.reshape(s_len, ATT_W)


def _att_prep_bwd(dqs, dks, dvs, tabs, du, *, name, tm=256):
    s_len = du.shape[0]
    scale = ATT_HD ** -0.5

    def body(*refs):
        dq_refs, dk_refs, dv_refs = refs[0:3], refs[3:6], refs[6:9]
        c_ref, s1_ref, s2_ref, _, du_ref = refs[9:14]
        c, s1, s2 = c_ref[...], s1_ref[...], s2_ref[...]
        for b in range(ATT_W // 128):
            bs = slice(128 * b, 128 * (b + 1))
            dq = dq_refs[0][:, bs] + dq_refs[1][:, bs] + dq_refs[2][:, bs]
            dk = dk_refs[0][:, bs] + dk_refs[1][:, bs] + dk_refs[2][:, bs]
            du_ref[:, bs] = _rope_t(dq * scale, c, s1, s2)
            du_ref[:, ATT_W + 128 * b:ATT_W + 128 * (b + 1)] = _rope_t(dk, c, s1, s2)
        du_ref[:, 2 * ATT_W:] = dv_refs[0][...] + dv_refs[1][...] + dv_refs[2][...]

    row = pl.BlockSpec((tm, ATT_W), lambda i: (i, 0))
    tab = pl.BlockSpec((tm, 128), lambda i: (i, 0))
    return pl.pallas_call(
        body, name=name, grid=(s_len // tm,),
        in_specs=[row] * 9 + [tab] * 3 + [pl.BlockSpec(memory_space=pl.ANY)],
        out_specs=pl.BlockSpec((tm, 3 * ATT_W), lambda i: (i, COL_AQ // 3)),
        out_shape=jax.ShapeDtypeStruct(du.shape, F32),
        input_output_aliases={12: 0},
        compiler_params=_params(("parallel",)),
    )(*dqs, *dks, *dvs, *tabs, du)


def _head_weights(w, b):
    lane = lax.broadcasted_iota(jnp.int32, (1, 128), 1)
    return jnp.where(lane < ATT_HD, w[:, 2 * b:2 * b + 1], w[:, 2 * b + 1:2 * b + 2])


def _assemble_fwd(pw, u, o_gdn, gnw, o_groups, lse_groups, *, name, tm=256):
    s_len = u.shape[0]
    c = CONV_CH

    def body(pw_ref, cg_ref, z_ref, ag_ref, og_ref, gnw_ref, o1, o2, o3, l1, l2, l3, y_ref, oa_ref, lse_ref):
        y_ref[:, :c] = (pw_ref[...] * _silu(cg_ref[...])).astype(BF16)
        gw = gnw_ref[...]
        for h in range(GDN_HEADS):
            hs = slice(GDN_D * h, GDN_D * (h + 1))
            oh = og_ref[:, hs]
            yn = oh * lax.rsqrt(jnp.mean(oh * oh, axis=-1, keepdims=True) + 1e-6) * gw
            y_ref[:, c + GDN_D * h:c + GDN_D * (h + 1)] = (yn * _silu(z_ref[:, hs])).astype(BF16)
        la, lb, lc = l1[...], l2[...], l3[...]
        m = jnp.maximum(jnp.maximum(la, lb), lc)
        ea, eb, ec = jnp.exp(la - m), jnp.exp(lb - m), jnp.exp(lc - m)
        den = ea + eb + ec
        lse_ref[...] = m + jnp.log(den)
        wa, wb, wc = ea / den, eb / den, ec / den
        for b in range(ATT_W // 128):
            bs = slice(128 * b, 128 * (b + 1))
            o = (_head_weights(wa, b) * o1[:, bs] + _head_weights(wb, b) * o2[:, bs]
                 + _head_weights(wc, b) * o3[:, bs])
            oa_ref[:, bs] = o
            y_ref[:, c + GDN_W + 128 * b:c + GDN_W + 128 * (b + 1)] = (o * _silu(ag_ref[:, bs])).astype(BF16)

    wide = pl.BlockSpec((tm, 768), lambda i: (i, 0))
    nar = pl.BlockSpec((tm, 128), lambda i: (i, 0))
    return pl.pallas_call(
        body, name=name, grid=(s_len // tm,),
        in_specs=[pl.BlockSpec((tm, c), lambda i: (i, 0)), pl.BlockSpec((tm, c), lambda i: (i, 1024 // c)),
                  pl.BlockSpec((tm, 768), lambda i: (i, COL_GQ + 3)), pl.BlockSpec((tm, 768), lambda i: (i, COL_AQ + 3)),
                  wide, pl.BlockSpec((1, 128), lambda i: (0, 0)), wide, wide, wide, nar, nar, nar],
        out_specs=[pl.BlockSpec((tm, D_MODEL), lambda i: (i, 0)), wide, nar],
        out_shape=[jax.ShapeDtypeStruct((s_len, D_MODEL), BF16), jax.ShapeDtypeStruct((s_len, ATT_W), F32),
                   jax.ShapeDtypeStruct((s_len, 128), F32)],
        compiler_params=_params(("parallel",)),
    )(pw, u, u, u, o_gdn, gnw, *o_groups, *lse_groups)


def _assemble_bwd(dy, pw, u, o_gdn, gnw, o_att, *, name, tm=128):
    s_len = u.shape[0]
    c = CONV_CH
    nsteps = s_len // tm

    def body(dy_ref, pw_ref, cg_ref, z_ref, ag_ref, og_ref, gnw_ref, oa_ref,
             du_ref, dpw_ref, dog_ref, dgw_ref, doa_ref, dl_ref, acc_ref):
        i = pl.program_id(0)

        @pl.when(i == 0)
        def _():
            acc_ref[...] = jnp.zeros_like(acc_ref)

        du_ref[...] = jnp.zeros_like(du_ref)
        dyc = dy_ref[:, :c]
        cg = cg_ref[...]
        dpw_ref[...] = dyc * _silu(cg)
        du_ref[:, 1024:1024 + c] = dyc * pw_ref[...] * _dsilu(cg)
        gw = gnw_ref[...]
        dgw = jnp.zeros((8, 128), F32)
        for h in range(GDN_HEADS):
            hs = slice(GDN_D * h, GDN_D * (h + 1))
            oh = og_ref[:, hs]
            zh = z_ref[:, hs]
            dyh = dy_ref[:, c + GDN_D * h:c + GDN_D * (h + 1)]
            r = lax.rsqrt(jnp.mean(oh * oh, axis=-1, keepdims=True) + 1e-6)
            xn = oh * r
            dyn = dyh * _silu(zh)
            du_ref[:, GDN_W * (COL_GQ + 3) + GDN_D * h:GDN_W * (COL_GQ + 3) + GDN_D * (h + 1)] = dyh * xn * gw * _dsilu(zh)
            dgw = dgw + (dyn * xn).reshape(tm // 8, 8, 128).sum(axis=0)
            dxn = dyn * gw
            dog_ref[:, hs] = r * (dxn - xn * jnp.mean(dxn * xn, axis=-1, keepdims=True))
        acc_ref[...] += dgw
        lane = lax.broadcasted_iota(jnp.int32, (tm, 128), 1)
        delta = jnp.zeros((tm, 128), F32)
        for b in range(ATT_W // 128):
            bs = slice(128 * b, 128 * (b + 1))
            dya = dy_ref[:, c + GDN_W + 128 * b:c + GDN_W + 128 * (b + 1)]
            ag = ag_ref[:, bs]
            oa = oa_ref[:, bs]
            do = dya * _silu(ag)
            doa_ref[:, bs] = do
            du_ref[:, ATT_W * (COL_AQ + 3) + 128 * b:ATT_W * (COL_AQ + 3) + 128 * (b + 1)] = dya * oa * _dsilu(ag)
            prod = do * oa
            lo = jnp.sum(jnp.where(lane < ATT_HD, prod, 0.0), axis=-1, keepdims=True)
            hi = jnp.sum(jnp.where(lane >= ATT_HD, prod, 0.0), axis=-1, keepdims=True)
            delta = delta + jnp.where(lane == 2 * b, lo, 0.0) + jnp.where(lane == 2 * b + 1, hi, 0.0)
        dl_ref[...] = delta

        @pl.when(i == nsteps - 1)
        def _():
            dgw_ref[...] = jnp.sum(acc_ref[...], axis=0, keepdims=True)

    wide = pl.BlockSpec((tm, 768), lambda i: (i, 0))
    nar = pl.BlockSpec((tm, 128), lambda i: (i, 0))
    vec = pl.BlockSpec((1, 128), lambda i: (0, 0))
    return pl.pallas_call(
        body, name=name, grid=(nsteps,),
        in_specs=[pl.BlockSpec((tm, D_MODEL), lambda i: (i, 0)), pl.BlockSpec((tm, c), lambda i: (i, 0)),
                  pl.BlockSpec((tm, c), lambda i: (i, 1024 // c)), pl.BlockSpec((tm, 768), lambda i: (i, COL_GQ + 3)),
                  pl.BlockSpec((tm, 768), lambda i: (i, COL_AQ + 3)), wide, vec, wide],
        out_specs=[pl.BlockSpec((tm, IN_WP), lambda i: (i, 0)), pl.BlockSpec((tm, c), lambda i: (i, 0)),
                   wide, vec, wide, nar],
        out_shape=[jax.ShapeDtypeStruct((s_len, IN_WP), F32), jax.ShapeDtypeStruct((s_len, c), F32),
                   jax.ShapeDtypeStruct((s_len, GDN_W), F32), jax.ShapeDtypeStruct((1, 128), F32),
                   jax.ShapeDtypeStruct((s_len, ATT_W), F32), jax.ShapeDtypeStruct((s_len, 128), F32)],
        scratch_shapes=[pltpu.VMEM((8, 128), F32)],
        compiler_params=_params(("arbitrary",)),
    )(dy, pw, u, u, u, o_gdn, gnw, o_att)


def _layer_fwd(x, p, tabs):
    h = _rms_fwd(x, p["norm_w"], name="rms_fwd")
    u = _matmul(h, p["wp"], name="in_proj")
    conv, sw = _conf_fwd(u, p["dw_w"], p["dw_b"], p["ln_w"], p["ln_b"], name="conf_fwd")
    pw = _matmul(sw, p["pw_w"], name="conf_pw")
    qn, kn, vc, bg = _gdn_prep_fwd(u, p["cw"], p["al"], p["dtb"], name="gdn_prep_fwd")
    wk, wv, qd, kd, pm, t, g_cum = _gdn_intra_fwd(qn, kn, vc, bg, name="gdn_intra_fwd")
    o_gdn, vn, sprev = _gdn_scan_fwd(wk, wv, qd, kd, pm, g_cum, name="gdn_scan_fwd")
    qr, kr, vb = _att_prep_fwd(u, tabs, name="att_prep_fwd")
    groups = [_att_fwd(qr, kr, vb, dil, name=f"att_fwd_d{dil}") for _, dil in DIL_PATTERNS]
    y, o_att, lse = _assemble_fwd(pw, u, o_gdn, p["gnw"], [g[0] for g in groups], [g[1] for g in groups],
                                  name="assemble_fwd")
    x_next = _matmul(y, p["wout"], add=x, name="out_proj")
    saved = dict(x=x, h=h, u=u, conv=conv, sw=sw, pw=pw, qn=qn, kn=kn, vc=vc, bg=bg, wk=wk, qd=qd, kd=kd, pm=pm,
                 t=t, g_cum=g_cum, vn=vn, sprev=sprev, o_gdn=o_gdn, qr=qr, kr=kr, vb=vb, o_att=o_att, lse=lse, y=y)
    return x_next, saved


def _layer_bwd(dx_out, s, p, tabs):
    dy = _matmul(dx_out, p["wout"], tb=True, name="out_proj_dy")
    d_wout = _matmul(s["y"], dx_out, ta=True, name="out_proj_dw")
    du, dpw, dog, dgw, doa, delta = _assemble_bwd(dy, s["pw"], s["u"], s["o_gdn"], p["gnw"], s["o_att"],
                                                  name="assemble_bwd")
    dsw = _matmul(dpw, p["pw_w"], tb=True, name="conf_pw_dx")
    d_pw_w = _matmul(s["sw"], dpw, ta=True, name="conf_pw_dw")
    dconv, ln_sums = _conf_bwd_ln(dsw, s["conv"], p["ln_w"], p["ln_b"], name="conf_bwd_ln")
    du, d_dw_w = _conf_bwd_conv(s["u"], dconv, p["dw_w"], du, name="conf_bwd_conv")
    dvn, ds_all = _gdn_scan_bwd(dog, s["wk"], s["qd"], s["kd"], s["pm"], s["g_cum"], name="gdn_scan_bwd")
    dqkv, dbg = _gdn_intra_bwd(s["qn"], s["kn"], s["vc"], s["bg"], s["g_cum"], s["t"], dog, dvn, s["vn"],
                               s["sprev"], ds_all, name="gdn_intra_bwd")
    du, d_cw = _gdn_prep_bwd(s["u"], dqkv, p["cw"], du, name="gdn_prep_bwd")
    du, ba_sums = _gdn_ba_bwd(s["u"], dbg, p["al"], p["dtb"], du, name="gdn_ba_bwd")
    dqs, dks, dvs = [], [], []
    for _, dil in DIL_PATTERNS:
        dqs.append(_att_bwd_dq(s["qr"], s["kr"], s["vb"], doa, s["lse"], delta, dil, name=f"att_bwd_dq_d{dil}"))
        dk, dv = _att_bwd_dkv(s["qr"], s["kr"], s["vb"], doa, s["lse"], delta, dil, name=f"att_bwd_dkv_d{dil}")
        dks.append(dk)
        dvs.append(dv)
    du = _att_prep_bwd(dqs, dks, dvs, tabs, du, name="att_prep_bwd")
    dh = _matmul(du, p["wp"], tb=True, name="in_proj_dx")
    d_wp = _matmul(s["h"], du, ta=True, name="in_proj_dw")
    dx, d_norm_w = _rms_bwd(s["x"], dh, p["norm_w"], dx_out, name="rms_bwd")
    grads = dict(wp=d_wp, wout=d_wout, pw_w=d_pw_w, norm_w=d_norm_w, gnw=dgw, ln_sums=ln_sums, dw_w=d_dw_w,
                 cw=d_cw, ba_sums=ba_sums)
    return dx, grads


def _trunk(x, target, params, final_norm_w):
    tabs = _rope_tables(x.shape[0])

    def fwd(xc, p):
        return _layer_fwd(xc, p, tabs)

    x_last, saved = lax.scan(fwd, x, params)
    dx, d_final, loss = _loss_head(x_last, final_norm_w, target, name="loss_head")

    def bwd(dxc, sp):
        return _layer_bwd(dxc, sp[0], sp[1], tabs)

    dx, grads = lax.scan(bwd, dx, (saved, params), reverse=True)
    return loss[0, 0], dx, grads, d_final


ANY = pl.BlockSpec(memory_space=pl.ANY)


def _position():
    return lax.axis_index("x"), lax.axis_index("y"), lax.axis_index("c")


def _other_chips(x, y):
    return [(1 - x, y), (x, 1 - y), (1 - x, 1 - y)]


def _gather_chips(shards, *, name):
    n = len(shards)

    def body(*refs):
        ins, outs = refs[:n], refs[n:2 * n]
        send, recv, lsem = refs[2 * n:]
        x, y, c = _position()
        sib = (x, y, 1 - c)
        chips = _other_chips(x, y)
        half = pl.ds(2 * c, 2)

        def copy(k, a, chip, rows, to, src=None):
            dst = outs[a].at[2 * chip[0] + chip[1], rows]
            return pltpu.make_async_remote_copy(
                src_ref=dst if src is None else src, dst_ref=dst, send_sem=send.at[k * n + a],
                recv_sem=recv.at[k * n + a], device_id=to, device_id_type=MESH)

        mine = [pltpu.make_async_copy(ins[a], outs[a].at[2 * x + y], lsem.at[a]) for a in range(n)]
        for cp in mine:
            cp.start()
        first = [copy(j, a, (x, y), half, (*chip, c), src=ins[a].at[half])
                 for j, chip in enumerate(chips) for a in range(n)]
        for cp in first:
            cp.start()
        passed = []
        for j, chip in enumerate(chips):
            for a in range(n):
                copy(j, a, chip, half, (x, y, c)).wait_recv()
                cp = copy(3 + j, a, chip, half, sib)
                cp.start()
                passed.append(cp)
        for j, chip in enumerate(chips):
            for a in range(n):
                copy(3 + j, a, chip, pl.ds(2 * (1 - c), 2), (x, y, c)).wait_recv()
        for cp in first + passed:
            cp.wait_send()
        for cp in mine:
            cp.wait()

    return pl.pallas_call(
        body, name=name, in_specs=[ANY] * n, out_specs=[ANY] * n,
        out_shape=[jax.ShapeDtypeStruct((4,) + s.shape, s.dtype) for s in shards],
        scratch_shapes=[pltpu.SemaphoreType.DMA((6 * n,)), pltpu.SemaphoreType.DMA((6 * n,)),
                        pltpu.SemaphoreType.DMA((n,))],
    )(*shards)


def _to_sibling(arrs, *, name):
    n = len(arrs)

    def body(*refs):
        ins, outs = refs[:n], refs[n:2 * n]
        send, recv = refs[2 * n:]
        x, y, c = _position()
        cps = [pltpu.make_async_remote_copy(src_ref=ins[a], dst_ref=outs[a], send_sem=send.at[a],
                                            recv_sem=recv.at[a], device_id=(x, y, 1 - c), device_id_type=MESH)
               for a in range(n)]
        for cp in cps:
            cp.start()
        for cp in cps:
            cp.wait()

    return pl.pallas_call(
        body, name=name, in_specs=[ANY] * n, out_specs=[ANY] * n,
        out_shape=[jax.ShapeDtypeStruct(a.shape, a.dtype) for a in arrs],
        scratch_shapes=[pltpu.SemaphoreType.DMA((n,)), pltpu.SemaphoreType.DMA((n,))],
    )(*arrs)


def _to_chips(arrs, *, name):
    n = len(arrs)

    def body(*refs):
        ins, outs = refs[:n], refs[n:2 * n]
        send, recv = refs[2 * n:]
        x, y, c = _position()
        cps = [pltpu.make_async_remote_copy(
            src_ref=ins[a].at[2 * chip[0] + chip[1]], dst_ref=outs[a].at[j], send_sem=send.at[j * n + a],
            recv_sem=recv.at[j * n + a], device_id=(*chip, c), device_id_type=MESH)
            for j, chip in enumerate(_other_chips(x, y)) for a in range(n)]
        for cp in cps:
            cp.start()
        for cp in cps:
            cp.wait()

    return pl.pallas_call(
        body, name=name, in_specs=[ANY] * n, out_specs=[ANY] * n,
        out_shape=[jax.ShapeDtypeStruct((3,) + a.shape[1:], a.dtype) for a in arrs],
        scratch_shapes=[pltpu.SemaphoreType.DMA((3 * n,)), pltpu.SemaphoreType.DMA((3 * n,))],
    )(*arrs)


def _join_halves(halves, *, name):
    n = len(halves)

    def body(*refs):
        ins, outs = refs[:n], refs[n:2 * n]
        send, recv, lsem = refs[2 * n:]
        x, y, c = _position()
        mine = pl.ds(2 * c, 2)
        loc = [pltpu.make_async_copy(ins[a], outs[a].at[mine], lsem.at[a]) for a in range(n)]
        cps = [pltpu.make_async_remote_copy(src_ref=ins[a], dst_ref=outs[a].at[mine], send_sem=send.at[a],
                                            recv_sem=recv.at[a], device_id=(x, y, 1 - c), device_id_type=MESH)
               for a in range(n)]
        for cp in loc + cps:
            cp.start()
        for a in range(n):
            cps[a].wait_send()
            pltpu.make_async_remote_copy(src_ref=ins[a], dst_ref=outs[a].at[pl.ds(2 * (1 - c), 2)],
                                         send_sem=send.at[a], recv_sem=recv.at[a], device_id=(x, y, 1 - c),
                                         device_id_type=MESH).wait_recv()
            loc[a].wait()

    return pl.pallas_call(
        body, name=name, in_specs=[ANY] * n, out_specs=[ANY] * n,
        out_shape=[jax.ShapeDtypeStruct((4,) + h.shape[1:], h.dtype) for h in halves],
        scratch_shapes=[pltpu.SemaphoreType.DMA((n,)), pltpu.SemaphoreType.DMA((n,)), pltpu.SemaphoreType.DMA((n,))],
    )(*halves)


def _allreduce_small(packed, *, name):
    rows = packed.shape[0]
    ndev = 8

    def body(x_ref, sum_ref, all_ref, send, recv, lsem):
        x, y, c = _position()
        me, sib = (x, y, c), (x, y, 1 - c)
        chips = _other_chips(x, y)

        def blk(px, py, pc):
            return all_ref.at[pl.ds((4 * px + 2 * py + pc) * rows, rows), :]

        def copy(k, block, to, src=None):
            return pltpu.make_async_remote_copy(
                src_ref=blk(*block) if src is None else src, dst_ref=blk(*block), send_sem=send.at[k],
                recv_sem=recv.at[k], device_id=to, device_id_type=MESH)

        mine = pltpu.make_async_copy(x_ref, blk(*me), lsem)
        mine.start()
        first = [copy(0, me, sib, src=x_ref)] + [copy(1 + j, me, (*chip, c), src=x_ref) for j, chip in enumerate(chips)]
        for cp in first:
            cp.start()
        passed = [copy(4 + j, (*chip, c), sib) for j, chip in enumerate(chips)]
        for j, chip in enumerate(chips):
            copy(1 + j, (*chip, c), me).wait_recv()
            passed[j].start()
        copy(0, sib, me).wait_recv()
        for j, chip in enumerate(chips):
            copy(4 + j, (*chip, 1 - c), me).wait_recv()
        for cp in first + passed:
            cp.wait_send()
        mine.wait()
        acc = all_ref[0:rows, :]
        for d in range(1, ndev):
            acc = acc + all_ref[d * rows:(d + 1) * rows, :]
        sum_ref[...] = acc

    vm = pl.BlockSpec(memory_space=pltpu.VMEM)
    return pl.pallas_call(
        body, name=name, in_specs=[vm], out_specs=vm, out_shape=jax.ShapeDtypeStruct((rows, 128), F32),
        scratch_shapes=[pltpu.VMEM((ndev * rows, 128), F32), pltpu.SemaphoreType.DMA((7,)),
                        pltpu.SemaphoreType.DMA((7,)), pltpu.SemaphoreType.DMA],
        compiler_params=pltpu.CompilerParams(vmem_limit_bytes=VMEM_LIMIT),
    )(packed)


def _pack(arrs):
    flat = jnp.concatenate([a.reshape(-1) for a in arrs])
    pad = (-flat.shape[0]) % 1024
    return jnp.pad(flat, (0, pad)).reshape(-1, 128)


def _unpack(packed, shapes):
    flat = packed.reshape(-1)
    out, pos = [], 0
    for s in shapes:
        size = math.prod(s)
        out.append(flat[pos:pos + size].reshape(s))
        pos += size
    return out


def _pad_cols(w):
    zeros = jnp.zeros(w.shape[:-1] + (IN_WP - IN_W,), w.dtype)
    return jnp.concatenate([w[..., :ORIG_BA], w[..., ORIG_ATT:], w[..., ORIG_BA:ORIG_ATT], zeros], axis=-1)


def _unpad_cols(w):
    n_att = IN_W - ORIG_ATT
    return jnp.concatenate([w[..., :ORIG_BA], w[..., COL_BA:COL_BA + ORIG_ATT - ORIG_BA],
                            w[..., ORIG_BA:ORIG_BA + n_att]], axis=-1)


def _lanes(v, first):
    return jnp.pad(v, ((0, 0), (first, 128 - first - v.shape[1])))[:, None, :]


def _by_chip(g, axis):
    shape = g.shape[:axis] + (4, g.shape[axis] // 4) + g.shape[axis + 1:]
    return jnp.moveaxis(g.reshape(shape), axis, 0)


def kernel(x, norm_w, w_in, conv_qkv_w, a_log, dt_bias, gdn_norm_w, conf_dw_w, conf_dw_b, conf_ln_w, conf_ln_b, conf_pw_w, w_out, final_norm_w, loss_target, m_norm_w, m_w_in, m_conv_qkv_w, m_a_log, m_dt_bias, m_gdn_norm_w, m_conf_dw_w, m_conf_dw_b, m_conf_ln_w, m_conf_ln_b, m_conf_pw_w, m_w_out, m_final_norm_w, v_norm_w, v_w_in, v_conv_qkv_w, v_a_log, v_dt_bias, v_gdn_norm_w, v_conf_dw_w, v_conf_dw_b, v_conf_ln_w, v_conf_ln_b, v_conf_pw_w, v_w_out, v_final_norm_w):
    xi, yi, ci = _position()
    chip = 2 * xi + yi

    g_in, g_out, g_pw, g_cw, g_dw = _gather_chips(
        [w_in.astype(BF16), w_out.astype(BF16), conf_pw_w.astype(BF16), conv_qkv_w, conf_dw_w], name="gather_weights")
    w_in_full = jnp.moveaxis(g_in, 0, 2).reshape(DEPTH, D_MODEL, IN_W)
    cw_full = jnp.moveaxis(g_cw, 0, 2).reshape(DEPTH, SHORT_CONV, 3 * GDN_W)
    dw_full = jnp.moveaxis(g_dw, 0, 2).reshape(DEPTH, CONV_WIDTH, CONV_CH)
    params = dict(
        norm_w=norm_w[:, None, :],
        wp=_pad_cols(w_in_full),
        wout=jnp.moveaxis(g_out, 0, 1).reshape(DEPTH, D_MODEL, D_MODEL),
        pw_w=jnp.moveaxis(g_pw, 0, 1).reshape(DEPTH, CONV_CH, CONV_CH),
        cw=jnp.pad(cw_full, ((0, 0), (0, SHALO - SHORT_CONV), (0, 0))),
        dw_w=jnp.pad(dw_full, ((0, 0), (0, HALO - CONV_WIDTH), (0, 0))),
        al=_lanes(a_log, GDN_HEADS), dtb=_lanes(dt_bias, GDN_HEADS), gnw=gdn_norm_w[:, None, :],
        dw_b=conf_dw_b[:, None, :], ln_w=conf_ln_w[:, None, :], ln_b=conf_ln_b[:, None, :],
    )

    loss_part, grad_x, grads, d_final = _trunk(x[0], loss_target[0], params, final_norm_w[None, :])
    loss = lax.psum(loss_part, ("x", "y", "c"))

    big = [_by_chip(_unpad_cols(grads["wp"]), 2), _by_chip(grads["wout"], 1), _by_chip(grads["pw_w"], 1)]
    keep = [lax.dynamic_slice_in_dim(g, 2 * ci, 2, axis=1) for g in big]
    give = [lax.dynamic_slice_in_dim(g, 2 * (1 - ci), 2, axis=1).astype(BF16) for g in big]
    got = _to_sibling(give, name="grads_to_sibling")
    pair = [_sum_arrays([k.reshape((8,) + k.shape[2:]), r.reshape((8,) + r.shape[2:])], name=f"pair_sum_{i}",
                        out_dtype=BF16).reshape(k.shape) for i, (k, r) in enumerate(zip(keep, got))]
    arrived = _to_chips(pair, name="grads_to_chips")
    halves = []
    for i, (pr, ar) in enumerate(zip(pair, arrived)):
        own = lax.dynamic_index_in_dim(pr, chip, axis=0, keepdims=False)
        halves.append(_sum_arrays([own, ar[0], ar[1], ar[2]], name=f"chip_sum_{i}", out_dtype=F32))
    g_w_in, g_w_out, g_pw_w = _join_halves(halves, name="join_halves")

    ba = grads["ba_sums"]
    small = [grads["norm_w"], ba[:, 0:1, :], ba[:, 1:2, :], grads["gnw"], grads["ln_sums"][:, 2:3, :],
             grads["ln_sums"][:, 0:1, :], grads["ln_sums"][:, 1:2, :], d_final,
             grads["cw"][:, :SHORT_CONV, :], grads["dw_w"][:, :CONV_WIDTH, :]]
    red = _unpack(_allreduce_small(_pack(small), name="allreduce_small"), [s.shape for s in small])
    g_norm_w = red[0][:, 0, :]
    g_a_log = red[1][:, 0, GDN_HEADS:2 * GDN_HEADS]
    g_dt_bias = red[2][:, 0, GDN_HEADS:2 * GDN_HEADS]
    g_gnw, g_dw_b, g_ln_w, g_ln_b = red[3][:, 0, :], red[4][:, 0, :], red[5][:, 0, :], red[6][:, 0, :]
    g_final = red[7][0]
    g_cw = lax.dynamic_slice_in_dim(red[8], chip * (3 * GDN_W // 4), 3 * GDN_W // 4, axis=2)
    g_dw_w = lax.dynamic_slice_in_dim(red[9], chip * (CONV_CH // 4), CONV_CH // 4, axis=2)

    d_w_in, nm_w_in, nv_w_in = _adamw(w_in, g_w_in, m_w_in, v_w_in, name="adamw_w_in")
    d_w_out, nm_w_out, nv_w_out = _adamw(w_out, g_w_out, m_w_out, v_w_out, name="adamw_w_out")
    d_pw_w, nm_pw_w, nv_pw_w = _adamw(conf_pw_w, g_pw_w, m_conf_pw_w, v_conf_pw_w, name="adamw_pw")
    sw = [norm_w, a_log, dt_bias, gdn_norm_w, conf_dw_b, conf_ln_w, conf_ln_b, final_norm_w, conv_qkv_w, conf_dw_w]
    sg = [g_norm_w, g_a_log, g_dt_bias, g_gnw, g_dw_b, g_ln_w, g_ln_b, g_final, g_cw, g_dw_w]
    sm = [m_norm_w, m_a_log, m_dt_bias, m_gdn_norm_w, m_conf_dw_b, m_conf_ln_w, m_conf_ln_b, m_final_norm_w,
          m_conv_qkv_w, m_conf_dw_w]
    sv = [v_norm_w, v_a_log, v_dt_bias, v_gdn_norm_w, v_conf_dw_b, v_conf_ln_w, v_conf_ln_b, v_final_norm_w,
          v_conv_qkv_w, v_conf_dw_w]
    shapes = [a.shape for a in sw]
    packed = _adamw(_pack(sw)[None], _pack(sg)[None], _pack(sm)[None], _pack(sv)[None], name="adamw_small")
    sd, snm, snv = [_unpack(pk[0], shapes) for pk in packed]

    def order(big3, small10):
        s = small10
        return [s[0], big3[0], s[8], s[1], s[2], s[3], s[9], s[4], s[5], s[6], big3[2], big3[1], s[7]]

    return (loss, grad_x[None], *order([g_w_in, g_w_out, g_pw_w], sg),
            *order([d_w_in, d_w_out, d_pw_w], sd), *order([nm_w_in, nm_w_out, nm_pw_w], snm),
            *order([nv_w_in, nv_w_out, nv_pw_w], snv))
```

```python
import functools
import math

import jax
import jax.numpy as jnp
from jax import lax
from jax.experimental import pallas as pl
from jax.experimental.pallas import tpu as pltpu

F32, BF16 = jnp.float32, jnp.bfloat16
HIGHEST = lax.Precision.HIGHEST
MESH = pl.DeviceIdType.MESH

D_MODEL = 2048
DEPTH = 4
CONV_CH = 512
GDN_W = 768
GDN_HEADS = 6
GDN_D = 128
ATT_W = 768
ATT_HEADS = 12
ATT_HD = 64
CONV_WIDTH = 31
SHORT_CONV = 4
GDN_CHUNK = 64
ROPE_THETA = 500000.0
ROPE_DIM = ATT_HD // 4
DIL_PATTERNS = ((128, 1), (512, 4), (2048, 16))
ATT_BLOCK = 128
NEG_INF = -1e30
IN_W = 7692

IN_WP = 8192
COL_BA = 7680
ORIG_BA = 4608
ORIG_ATT = 4620

ADAM_LR = 0.001
ADAM_B1 = 0.9
ADAM_B2 = 0.999
ADAM_EPS = 1e-08
ADAM_WD = 0.01
ADAM_STEP = 10

VMEM_LIMIT = 56 * 1024 * 1024


def _params(sem=None):
    return pltpu.CompilerParams(dimension_semantics=sem, vmem_limit_bytes=VMEM_LIMIT)


def _sigmoid(x):
    return 1.0 / (1.0 + jnp.exp(-x))


def _silu(x):
    return x * _sigmoid(x)


def _dsilu(x):
    s = _sigmoid(x)
    return s * (1.0 + x * (1.0 - s))


def _dot(a, b, dims, precision=None):
    return lax.dot_general(a, b, (dims, ((), ())), precision=precision, preferred_element_type=F32)


def _nn(a, b, precision=None):
    return _dot(a, b, ((1,), (0,)), precision)


def _nt(a, b, precision=None):
    return _dot(a, b, ((1,), (1,)), precision)


def _tn(a, b, precision=None):
    return _dot(a, b, ((0,), (0,)), precision)


def _matmul(a, b, *, name, ta=False, tb=False, out_dtype=F32, add=None, stack=None, tm=1024, tn=1024, tk=1024):
    if ta:
        k_dim, m_dim = a.shape
    else:
        m_dim, k_dim = a.shape
    n_dim = b.shape[0] if tb else b.shape[1]
    tm, tn, tk = min(tm, m_dim), min(tn, n_dim), min(tk, k_dim)
    assert m_dim % tm == 0 and n_dim % tn == 0 and k_dim % tk == 0, (name, a.shape, b.shape)
    nk = k_dim // tk
    a_spec = pl.BlockSpec((tk, tm), lambda i, j, k: (k, i)) if ta else pl.BlockSpec((tm, tk), lambda i, j, k: (i, k))
    b_spec = pl.BlockSpec((tn, tk), lambda i, j, k: (j, k)) if tb else pl.BlockSpec((tk, tn), lambda i, j, k: (k, j))
    o_spec = pl.BlockSpec((tm, tn), lambda i, j, k: (i, j))
    out_shape = jax.ShapeDtypeStruct((m_dim, n_dim), out_dtype)
    dims = ((0 if ta else 1,), (1 if tb else 0,))
    has_add = add is not None
    ins = [a, b] + ([add] if has_add else [])
    specs = [a_spec, b_spec] + ([o_spec] if has_add else [])
    aliases = {}
    if stack is not None:
        buf, slab, nslabs = stack
        o_spec = pl.BlockSpec((None, tm, tn), lambda i, j, k: (slab, i, j))
        out_shape = jax.ShapeDtypeStruct((nslabs, m_dim, n_dim), out_dtype)
        if buf is not None:
            aliases = {len(ins): 0}
            ins.append(buf)
            specs.append(pl.BlockSpec(memory_space=pl.ANY))
    n_in = len(ins)

    def body(*refs):
        a_ref, b_ref = refs[0], refs[1]
        o_ref = refs[n_in]

        def finish(r):
            if has_add:
                r = r + refs[2][...]
            o_ref[...] = r.astype(out_dtype)

        prod = _dot(a_ref[...].astype(BF16), b_ref[...].astype(BF16), dims)
        if nk == 1:
            finish(prod)
            return
        acc_ref = refs[n_in + 1]
        k = pl.program_id(2)

        @pl.when(k == 0)
        def _():
            acc_ref[...] = prod

        @pl.when(k > 0)
        def _():
            acc_ref[...] += prod

        @pl.when(k == nk - 1)
        def _():
            finish(acc_ref[...])

    return pl.pallas_call(
        body, name=name, grid=(m_dim // tm, n_dim // tn, nk), in_specs=specs, out_specs=o_spec,
        out_shape=out_shape, scratch_shapes=[pltpu.VMEM((tm, tn), F32)] if nk > 1 else [],
        input_output_aliases=aliases,
        compiler_params=_params(("parallel", "parallel", "arbitrary")),
    )(*ins)


def _rms_fwd(x, w, *, name, tm=256):
    s_len, d = x.shape

    def body(x_ref, w_ref, h_ref):
        xv = x_ref[...]
        r = lax.rsqrt(jnp.mean(xv * xv, axis=-1, keepdims=True) + 1e-6)
        h_ref[...] = (xv * r * w_ref[...]).astype(BF16)

    return pl.pallas_call(
        body, name=name, grid=(s_len // tm,),
        in_specs=[pl.BlockSpec((tm, d), lambda i: (i, 0)), pl.BlockSpec((1, d), lambda i: (0, 0))],
        out_specs=pl.BlockSpec((tm, d), lambda i: (i, 0)),
        out_shape=jax.ShapeDtypeStruct((s_len, d), BF16),
        compiler_params=_params(("parallel",)),
    )(x, w)


def _rms_bwd(x, dh, w, dres, *, name, tm=256):
    s_len, d = x.shape
    nsteps = s_len // tm

    def body(x_ref, dh_ref, w_ref, dres_ref, dx_ref, dw_ref, acc_ref):
        i = pl.program_id(0)

        @pl.when(i == 0)
        def _():
            acc_ref[...] = jnp.zeros_like(acc_ref)

        xv = x_ref[...]
        r = lax.rsqrt(jnp.mean(xv * xv, axis=-1, keepdims=True) + 1e-6)
        xn = xv * r
        dy = dh_ref[...]
        dxn = dy * w_ref[...]
        dx_ref[...] = dres_ref[...] + r * (dxn - xn * jnp.mean(dxn * xn, axis=-1, keepdims=True))
        acc_ref[...] += (dy * xn).reshape(tm // 8, 8, d).sum(axis=0)

        @pl.when(i == nsteps - 1)
        def _():
            dw_ref[...] = jnp.sum(acc_ref[...], axis=0, keepdims=True)

    row = pl.BlockSpec((tm, d), lambda i: (i, 0))
    vec = pl.BlockSpec((1, d), lambda i: (0, 0))
    return pl.pallas_call(
        body, name=name, grid=(nsteps,), in_specs=[row, row, vec, row], out_specs=[row, vec],
        out_shape=[jax.ShapeDtypeStruct((s_len, d), F32), jax.ShapeDtypeStruct((1, d), F32)],
        scratch_shapes=[pltpu.VMEM((8, d), F32)],
        compiler_params=_params(("arbitrary",)),
    )(x, dh, w, dres)


def _loss_head(x, w, target, *, name, tm=256):
    s_len, d = x.shape
    nsteps = s_len // tm

    def body(x_ref, w_ref, t_ref, dx_ref, dw_ref, loss_ref, acc_ref, lacc_ref):
        i = pl.program_id(0)

        @pl.when(i == 0)
        def _():
            acc_ref[...] = jnp.zeros_like(acc_ref)
            lacc_ref[...] = jnp.zeros_like(lacc_ref)

        xv = x_ref[...]
        wv = w_ref[...]
        r = lax.rsqrt(jnp.mean(xv * xv, axis=-1, keepdims=True) + 1e-6)
        xn = xv * r
        err = xn * wv - t_ref[...]
        lacc_ref[...] += (err * err).reshape(tm // 8, 8, d).sum(axis=0)
        dy = err * (1.0 / d)
        dxn = dy * wv
        dx_ref[...] = r * (dxn - xn * jnp.mean(dxn * xn, axis=-1, keepdims=True))
        acc_ref[...] += (dy * xn).reshape(tm // 8, 8, d).sum(axis=0)

        @pl.when(i == nsteps - 1)
        def _():
            dw_ref[...] = jnp.sum(acc_ref[...], axis=0, keepdims=True)
            tot = jnp.sum(jnp.sum(lacc_ref[...], axis=0, keepdims=True), axis=1, keepdims=True)
            loss_ref[...] = jnp.broadcast_to(tot * (0.5 / d), (1, 128))

    row = pl.BlockSpec((tm, d), lambda i: (i, 0))
    vec = pl.BlockSpec((1, d), lambda i: (0, 0))
    return pl.pallas_call(
        body, name=name, grid=(nsteps,), in_specs=[row, vec, row],
        out_specs=[row, vec, pl.BlockSpec((1, 128), lambda i: (0, 0))],
        out_shape=[jax.ShapeDtypeStruct((s_len, d), F32), jax.ShapeDtypeStruct((1, d), F32),
                   jax.ShapeDtypeStruct((1, 128), F32)],
        scratch_shapes=[pltpu.VMEM((8, d), F32), pltpu.VMEM((8, d), F32)],
        compiler_params=_params(("arbitrary",)),
    )(x, w, target)


def _rows_block(shape, tr=256):
    lead, rows, cols = shape
    if rows % tr != 0:
        assert rows * cols <= 1 << 20, shape
        tr = rows
    return (lead, rows // tr), pl.BlockSpec((1, tr, cols), lambda a, i: (a, i, 0))


def _adamw(w, g, m, v, *, name):
    grid, spec = _rows_block(w.shape)
    c1 = 1.0 / (1.0 - ADAM_B1 ** ADAM_STEP)
    c2 = 1.0 / (1.0 - ADAM_B2 ** ADAM_STEP)

    def body(w_ref, g_ref, m_ref, v_ref, d_ref, nm_ref, nv_ref):
        gv = g_ref[...]
        nm = ADAM_B1 * m_ref[...] + (1.0 - ADAM_B1) * gv
        nv = ADAM_B2 * v_ref[...] + (1.0 - ADAM_B2) * (gv * gv)
        nm_ref[...] = nm
        nv_ref[...] = nv
        d_ref[...] = -ADAM_LR * ((nm * c1) / (jnp.sqrt(nv * c2) + ADAM_EPS) + ADAM_WD * w_ref[...])

    out = jax.ShapeDtypeStruct(w.shape, F32)
    return pl.pallas_call(
        body, name=name, grid=grid, in_specs=[spec] * 4, out_specs=[spec] * 3, out_shape=[out] * 3,
        compiler_params=_params(("parallel", "parallel")),
    )(w, g, m, v)


def _sum_arrays(arrs, *, name, out_dtype):
    grid, spec = _rows_block(arrs[0].shape)
    n = len(arrs)

    def body(*refs):
        acc = refs[0][...].astype(F32)
        for r in refs[1:n]:
            acc = acc + r[...].astype(F32)
        refs[n][...] = acc.astype(out_dtype)

    return pl.pallas_call(
        body, name=name, grid=grid, in_specs=[spec] * n, out_specs=spec,
        out_shape=jax.ShapeDtypeStruct(arrs[0].shape, out_dtype),
        compiler_params=_params(("parallel", "parallel")),
    )(*arrs)


HALO = 32


def _conf_fwd(u, dw_w, dw_b, ln_w, ln_b, *, name, tm=256):
    s_len = u.shape[0]
    c = CONV_CH

    def body(uc_ref, up_ref, dww_ref, dwb_ref, lnw_ref, lnb_ref, conv_ref, sw_ref, hbuf):
        i = pl.program_id(0)
        hbuf[HALO:, :] = uc_ref[:, :c] * _sigmoid(uc_ref[:, c:])
        hp = up_ref[:, :c] * _sigmoid(up_ref[:, c:])
        hbuf[:HALO, :] = jnp.where(i > 0, hp, 0.0)
        for cb in range(c // 128):
            cs = slice(128 * cb, 128 * (cb + 1))
            acc = jnp.zeros((tm, 128), F32)
            for j in range(CONV_WIDTH):
                acc = acc + hbuf[HALO - CONV_WIDTH + 1 + j:HALO - CONV_WIDTH + 1 + j + tm, cs] * dww_ref[j:j + 1, cs]
            conv_ref[:, cs] = acc + dwb_ref[:, cs]
        cv = conv_ref[...]
        mu = jnp.mean(cv, axis=-1, keepdims=True)
        xc = cv - mu
        var = jnp.mean(xc * xc, axis=-1, keepdims=True)
        ln = xc * lax.rsqrt(var + 1e-5) * lnw_ref[...] + lnb_ref[...]
        sw_ref[...] = _silu(ln).astype(BF16)

    vec = pl.BlockSpec((1, c), lambda i: (0, 0))
    return pl.pallas_call(
        body, name=name, grid=(s_len // tm,),
        in_specs=[pl.BlockSpec((tm, 2 * c), lambda i: (i, 0)),
                  pl.BlockSpec((HALO, 2 * c), lambda i: (jnp.maximum(i * (tm // HALO) - 1, 0), 0)),
                  pl.BlockSpec((HALO, c), lambda i: (0, 0)), vec, vec, vec],
        out_specs=[pl.BlockSpec((tm, c), lambda i: (i, 0))] * 2,
        out_shape=[jax.ShapeDtypeStruct((s_len, c), F32), jax.ShapeDtypeStruct((s_len, c), BF16)],
        scratch_shapes=[pltpu.VMEM((tm + HALO, c), F32)],
        compiler_params=_params(("parallel",)),
    )(u, u, dw_w, dw_b, ln_w, ln_b)


def _conf_bwd_ln(d_sw, conv, ln_w, ln_b, *, name, tm=256):
    s_len, c = conv.shape
    nsteps = s_len // tm

    def body(dsw_ref, conv_ref, lnw_ref, lnb_ref, dconv_ref, sums_ref):
        i = pl.program_id(0)

        @pl.when(i == 0)
        def _():
            sums_ref[...] = jnp.zeros_like(sums_ref)

        cv = conv_ref[...]
        mu = jnp.mean(cv, axis=-1, keepdims=True)
        xc = cv - mu
        rs = lax.rsqrt(jnp.mean(xc * xc, axis=-1, keepdims=True) + 1e-5)
        xhat = xc * rs
        lnw = lnw_ref[...]
        ln = xhat * lnw + lnb_ref[...]
        dln = dsw_ref[...] * _dsilu(ln)
        dxh = dln * lnw
        dconv = rs * (dxh - jnp.mean(dxh, axis=-1, keepdims=True)
                      - xhat * jnp.mean(dxh * xhat, axis=-1, keepdims=True))
        dconv_ref[...] = dconv
        sums_ref[0:1, :] += jnp.sum(dln * xhat, axis=0, keepdims=True)
        sums_ref[1:2, :] += jnp.sum(dln, axis=0, keepdims=True)
        sums_ref[2:3, :] += jnp.sum(dconv, axis=0, keepdims=True)

    row = pl.BlockSpec((tm, c), lambda i: (i, 0))
    vec = pl.BlockSpec((1, c), lambda i: (0, 0))
    return pl.pallas_call(
        body, name=name, grid=(nsteps,), in_specs=[row, row, vec, vec],
        out_specs=[row, pl.BlockSpec((8, c), lambda i: (0, 0))],
        out_shape=[jax.ShapeDtypeStruct((s_len, c), F32), jax.ShapeDtypeStruct((8, c), F32)],
        compiler_params=_params(("arbitrary",)),
    )(d_sw, conv, ln_w, ln_b)


def _conf_bwd_conv(u, dconv, dw_w, du, *, name, tm=256):
    s_len = u.shape[0]
    c = CONV_CH
    nsteps = s_len // tm
    off = HALO - CONV_WIDTH + 1

    def body(uc_ref, up_ref, dc_ref, dn_ref, dww_ref, du_in_ref, du_ref, ddw_ref, hbuf, dbuf, wacc):
        del du_in_ref
        i = pl.program_id(0)

        @pl.when(i == 0)
        def _():
            wacc[...] = jnp.zeros_like(wacc)

        hbuf[HALO:, :] = uc_ref[:, :c] * _sigmoid(uc_ref[:, c:])
        hp = up_ref[:, :c] * _sigmoid(up_ref[:, c:])
        hbuf[:HALO, :] = jnp.where(i > 0, hp, 0.0)
        dbuf[:tm, :] = dc_ref[...]
        dbuf[tm:, :] = jnp.where(i < nsteps - 1, dn_ref[...], 0.0)
        for cb in range(c // 128):
            cs = slice(128 * cb, 128 * (cb + 1))
            dcur = dbuf[0:tm, cs]
            acc = jnp.zeros((tm, 128), F32)
            for j in range(CONV_WIDTH):
                k = CONV_WIDTH - 1 - j
                acc = acc + dbuf[k:k + tm, cs] * dww_ref[j:j + 1, cs]
                prod = hbuf[off + j:off + j + tm, cs] * dcur
                wacc[j, :, cs] += prod.reshape(tm // 8, 8, 128).sum(axis=0)
            a = uc_ref[:, cs]
            sg = _sigmoid(uc_ref[:, c + 128 * cb:c + 128 * (cb + 1)])
            du_ref[:, cs] = acc * sg
            du_ref[:, c + 128 * cb:c + 128 * (cb + 1)] = acc * a * sg * (1.0 - sg)

        @pl.when(i == nsteps - 1)
        def _():
            for j in range(CONV_WIDTH):
                ddw_ref[j:j + 1, :] = jnp.sum(wacc[j], axis=0, keepdims=True)
            ddw_ref[CONV_WIDTH:, :] = jnp.zeros((HALO - CONV_WIDTH, c), F32)

    return pl.pallas_call(
        body, name=name, grid=(nsteps,),
        in_specs=[pl.BlockSpec((tm, 2 * c), lambda i: (i, 0)),
                  pl.BlockSpec((HALO, 2 * c), lambda i: (jnp.maximum(i * (tm // HALO) - 1, 0), 0)),
                  pl.BlockSpec((tm, c), lambda i: (i, 0)),
                  pl.BlockSpec((HALO, c), lambda i: (jnp.minimum((i + 1) * (tm // HALO), s_len // HALO - 1), 0)),
                  pl.BlockSpec((HALO, c), lambda i: (0, 0)),
                  pl.BlockSpec(memory_space=pl.ANY)],
        out_specs=[pl.BlockSpec((tm, 2 * c), lambda i: (i, 0)), pl.BlockSpec((HALO, c), lambda i: (0, 0))],
        out_shape=[jax.ShapeDtypeStruct(du.shape, F32), jax.ShapeDtypeStruct((HALO, c), F32)],
        scratch_shapes=[pltpu.VMEM((tm + HALO, c), F32), pltpu.VMEM((tm + HALO, c), F32),
                        pltpu.VMEM((CONV_WIDTH, 8, c), F32)],
        input_output_aliases={5: 0},
        compiler_params=_params(("arbitrary",)),
    )(u, u, dconv, dconv, dw_w, du)


COL_GQ = 1536 // GDN_W
COL_AQ = 4608 // ATT_W
SHALO = 8


def _softplus(z):
    return jnp.maximum(z, 0.0) + jnp.log1p(jnp.exp(-jnp.abs(z)))


def _short_conv(buf, cw_ref, part, rows, first):
    acc = jnp.zeros((rows, GDN_W), F32)
    for j in range(SHORT_CONV):
        acc = acc + buf[first + j:first + j + rows, :] * cw_ref[j:j + 1, GDN_W * part:GDN_W * (part + 1)]
    return acc


def _gdn_prep_fwd(u, cw, al, dtb, *, name, tm=256):
    s_len = u.shape[0]
    first = SHALO - SHORT_CONV + 1

    def body(uq, uk, uv, pq, pk, pv, uba, cw_ref, al_ref, dtb_ref, qn_ref, kn_ref, vc_ref, bg_ref, buf):
        i = pl.program_id(0)

        def conv(cur, prev, part):
            buf[SHALO:, :] = cur[...]
            buf[:SHALO, :] = jnp.where(i > 0, prev[...], 0.0)
            return _silu(_short_conv(buf, cw_ref, part, tm, first))

        for part, (cur, prev, out, scale) in enumerate(
                ((uq, pq, qn_ref, GDN_D ** -0.5), (uk, pk, kn_ref, 1.0))):
            y = conv(cur, prev, part)
            for h in range(GDN_HEADS):
                hs = slice(GDN_D * h, GDN_D * (h + 1))
                yh = y[:, hs]
                out[:, hs] = yh * (lax.rsqrt(jnp.sum(yh * yh, axis=-1, keepdims=True) + 1e-6) * scale)
        vc_ref[...] = conv(uv, pv, 2)
        ba = uba[...]
        lane = lax.broadcasted_iota(jnp.int32, ba.shape, 1)
        g = -jnp.exp(al_ref[...]) * _softplus(ba + dtb_ref[...])
        bg_ref[...] = jnp.where(lane < GDN_HEADS, _sigmoid(ba), jnp.where(lane < 2 * GDN_HEADS, g, 0.0))

    def cur(col):
        return pl.BlockSpec((tm, GDN_W), lambda i: (i, col))

    def prev(col):
        return pl.BlockSpec((SHALO, GDN_W), lambda i: (jnp.maximum(i * (tm // SHALO) - 1, 0), col))

    vec = pl.BlockSpec((1, 128), lambda i: (0, 0))
    row = pl.BlockSpec((tm, GDN_W), lambda i: (i, 0))
    wide = jax.ShapeDtypeStruct((s_len, GDN_W), F32)
    return pl.pallas_call(
        body, name=name, grid=(s_len // tm,),
        in_specs=[cur(COL_GQ), cur(COL_GQ + 1), cur(COL_GQ + 2), prev(COL_GQ), prev(COL_GQ + 1), prev(COL_GQ + 2),
                  pl.BlockSpec((tm, 128), lambda i: (i, COL_BA // 128)),
                  pl.BlockSpec((SHALO, 3 * GDN_W), lambda i: (0, 0)), vec, vec],
        out_specs=[row, row, row, pl.BlockSpec((tm, 128), lambda i: (i, 0))],
        out_shape=[wide, wide, wide, jax.ShapeDtypeStruct((s_len, 128), F32)],
        scratch_shapes=[pltpu.VMEM((tm + SHALO, GDN_W), F32)],
        compiler_params=_params(("parallel",)),
    )(u, u, u, u, u, u, u, cw, al, dtb)


def _chunk_masks():
    c = GDN_CHUNK
    row = lax.broadcasted_iota(jnp.int32, (c, c), 0)
    col = lax.broadcasted_iota(jnp.int32, (c, c), 1)
    return row >= col, row > col


def _cum_decay(bg):
    c = GDN_CHUNK
    causal, _ = _chunk_masks()
    g_cum = _nn(causal.astype(F32), bg, HIGHEST)
    sel = (lax.broadcasted_iota(jnp.int32, (8, 128), 0) + GDN_HEADS
           == lax.broadcasted_iota(jnp.int32, (8, 128), 1)).astype(F32)
    return g_cum, _nt(sel, g_cum, HIGHEST)


def _bdot(a, b, ca, cb):
    return lax.dot_general(a, b, (((ca,), (cb,)), ((0,), (0,))), preferred_element_type=F32)


def _bnn(a, b):
    return _bdot(a, b, 2, 1)


def _bnt(a, b):
    return _bdot(a, b, 2, 2)


def _btn(a, b):
    return _bdot(a, b, 1, 1)


def _split(a):
    hi = a.astype(BF16)
    return hi, (a - hi.astype(F32)).astype(BF16)


def _bnn3(a, b):
    ah, al = _split(a)
    bh, bl = _split(b)
    return _bnn(ah, bh) + (_bnn(al, bh) + _bnn(ah, bl))


def _heads(ref):
    return jnp.stack([ref[:, GDN_D * h:GDN_D * (h + 1)] for h in range(GDN_HEADS)])


def _head_columns(a, first):
    return jnp.stack([a[:, first + h:first + h + 1] for h in range(GDN_HEADS)])


def _chunk_decay(g_cum, g_rows, bg):
    causal, _ = _chunk_masks()
    gc = _head_columns(g_cum, GDN_HEADS)
    gr = jnp.stack([g_rows[h:h + 1, :] for h in range(GDN_HEADS)])
    dec = jnp.where(causal, jnp.exp(jnp.where(causal, gc - gr, 0.0)), 0.0)
    return gc, _head_columns(bg, 0), dec


def _gdn_intra_fwd(qn, kn, vc, bg, *, name):
    s_len = qn.shape[0]
    c = GDN_CHUNK
    nchunks = s_len // c

    def body(q_ref, k_ref, v_ref, bg_ref, wk_ref, wv_ref, qd_ref, kd_ref, p_ref, t_ref, g_ref):
        causal, strict = _chunk_masks()
        eye = (lax.broadcasted_iota(jnp.int32, (c, c), 0) == lax.broadcasted_iota(jnp.int32, (c, c), 1)).astype(F32)
        bg = bg_ref[...]
        g_cum, g_rows = _cum_decay(bg)
        g_ref[...] = g_cum
        gc, bc, dec = _chunk_decay(g_cum, g_rows, bg)
        q, k, v = _heads(q_ref), _heads(k_ref), _heads(v_ref)
        k16 = k.astype(BF16)
        low = jnp.where(strict, bc * _bnt(k16, k16) * dec, 0.0)
        pw = -low
        t = eye + pw
        for _ in range(5):
            pw = _bnn3(pw, pw)
            t = t + _bnn3(t, pw)
        t_ref[...] = t
        t16 = t.astype(BF16)
        eg = jnp.exp(gc)
        wk = _bnn(t16, (k * (bc * eg)).astype(BF16))
        wv = _bnn(t16, (v * bc).astype(BF16))
        p_ref[...] = jnp.where(causal, _bnt(q.astype(BF16), k16) * dec, 0.0).astype(BF16)
        qd = q * eg
        kd = k * jnp.exp(gc[:, c - 1:c, :] - gc)
        for h in range(GDN_HEADS):
            hs = slice(GDN_D * h, GDN_D * (h + 1))
            wk_ref[:, hs] = wk[h].astype(BF16)
            wv_ref[:, hs] = wv[h]
            qd_ref[:, hs] = qd[h].astype(BF16)
            kd_ref[:, hs] = kd[h].astype(BF16)

    row = pl.BlockSpec((c, GDN_W), lambda n: (n, 0))
    sq = pl.BlockSpec((GDN_HEADS, c, c), lambda n: (0, n, 0))
    narrow = pl.BlockSpec((c, 128), lambda n: (n, 0))
    w16 = jax.ShapeDtypeStruct((s_len, GDN_W), BF16)
    return pl.pallas_call(
        body, name=name, grid=(nchunks,), in_specs=[row, row, row, narrow],
        out_specs=[row, row, row, row, sq, sq, narrow],
        out_shape=[w16, jax.ShapeDtypeStruct((s_len, GDN_W), F32), w16, w16,
                   jax.ShapeDtypeStruct((GDN_HEADS, s_len, c), BF16),
                   jax.ShapeDtypeStruct((GDN_HEADS, s_len, c), F32),
                   jax.ShapeDtypeStruct((s_len, 128), F32)],
        compiler_params=_params(("parallel",)),
    )(qn, kn, vc, bg)


def _gdn_scan_fwd(wk, wv, qd, kd, p, g_cum, *, name):
    s_len = wk.shape[0]
    c = GDN_CHUNK
    nchunks = s_len // c

    def body(wk_ref, wv_ref, qd_ref, kd_ref, p_ref, g_ref, o_ref, vn_ref, sp_ref, st):
        @pl.when(pl.program_id(0) == 0)
        def _():
            st[...] = jnp.zeros_like(st)

        for h in range(GDN_HEADS):
            hs = slice(GDN_D * h, GDN_D * (h + 1))
            s = st[h]
            sp_ref[0, h] = s
            s16 = s.astype(BF16)
            vn16 = (wv_ref[:, hs] - _nn(wk_ref[:, hs], s16)).astype(BF16)
            vn_ref[:, hs] = vn16
            o_ref[:, hs] = _nn(qd_ref[:, hs], s16) + _nn(p_ref[h], vn16)
            gl = jnp.exp(g_ref[c - 1:c, GDN_HEADS + h:GDN_HEADS + h + 1])
            st[h] = s * gl + _tn(kd_ref[:, hs], vn16)

    row = pl.BlockSpec((c, GDN_W), lambda n: (n, 0))
    return pl.pallas_call(
        body, name=name, grid=(nchunks,),
        in_specs=[row, row, row, row, pl.BlockSpec((GDN_HEADS, c, c), lambda n: (0, n, 0)),
                  pl.BlockSpec((c, 128), lambda n: (n, 0))],
        out_specs=[row, row, pl.BlockSpec((1, GDN_HEADS, GDN_D, GDN_D), lambda n: (n, 0, 0, 0))],
        out_shape=[jax.ShapeDtypeStruct((s_len, GDN_W), F32), jax.ShapeDtypeStruct((s_len, GDN_W), BF16),
                   jax.ShapeDtypeStruct((nchunks, GDN_HEADS, GDN_D, GDN_D), F32)],
        scratch_shapes=[pltpu.VMEM((GDN_HEADS, GDN_D, GDN_D), F32)],
        compiler_params=_params(("arbitrary",)),
    )(wk, wv, qd, kd, p, g_cum)


def _gdn_scan_bwd(do, wk, qd, kd, p, g_cum, *, name):
    s_len = wk.shape[0]
    c = GDN_CHUNK
    nchunks = s_len // c

    def body(do_ref, wk_ref, qd_ref, kd_ref, p_ref, g_ref, dvn_ref, ds_ref, dst):
        @pl.when(pl.program_id(0) == 0)
        def _():
            dst[...] = jnp.zeros_like(dst)

        for h in range(GDN_HEADS):
            hs = slice(GDN_D * h, GDN_D * (h + 1))
            ds = dst[h]
            ds_ref[0, h] = ds
            do16 = do_ref[:, hs].astype(BF16)
            dvn16 = (_tn(p_ref[h], do16) + _nn(kd_ref[:, hs], ds.astype(BF16))).astype(BF16)
            dvn_ref[:, hs] = dvn16
            gl = jnp.exp(g_ref[c - 1:c, GDN_HEADS + h:GDN_HEADS + h + 1])
            dst[h] = _tn(qd_ref[:, hs], do16) + ds * gl - _tn(wk_ref[:, hs], dvn16)

    row = pl.BlockSpec((c, GDN_W), lambda n: (nchunks - 1 - n, 0))
    return pl.pallas_call(
        body, name=name, grid=(nchunks,),
        in_specs=[row, row, row, row, pl.BlockSpec((GDN_HEADS, c, c), lambda n: (0, nchunks - 1 - n, 0)),
                  pl.BlockSpec((c, 128), lambda n: (nchunks - 1 - n, 0))],
        out_specs=[row, pl.BlockSpec((1, GDN_HEADS, GDN_D, GDN_D), lambda n: (nchunks - 1 - n, 0, 0, 0))],
        out_shape=[jax.ShapeDtypeStruct((s_len, GDN_W), BF16),
                   jax.ShapeDtypeStruct((nchunks, GDN_HEADS, GDN_D, GDN_D), F32)],
        scratch_shapes=[pltpu.VMEM((GDN_HEADS, GDN_D, GDN_D), F32)],
        compiler_params=_params(("arbitrary",)),
    )(do, wk, qd, kd, p, g_cum)


def _gdn_intra_bwd(qn, kn, vc, bg, g_cum, t, do, dvn, vn, sprev, ds_all, *, name):
    s_len = qn.shape[0]
    c = GDN_CHUNK
    nchunks = s_len // c

    def body(q_ref, k_ref, v_ref, bg_ref, g_ref, t_ref, do_ref, dvn_ref, vn_ref, sp_ref, ds_ref,
             dqkv_ref, dbg_ref):
        causal, strict = _chunk_masks()
        bg = bg_ref[...]
        g_cum = g_ref[...]
        _, g_rows = _cum_decay(bg)
        lane = lax.broadcasted_iota(jnp.int32, (c, 128), 1)
        rowi = lax.broadcasted_iota(jnp.int32, (c, 128), 0)
        gc, bc, dec = _chunk_decay(g_cum, g_rows, bg)
        q, k, v = _heads(q_ref), _heads(k_ref), _heads(v_ref)
        q16, k16 = q.astype(BF16), k.astype(BF16)
        kk = _bnt(k16, k16)
        low = jnp.where(strict, bc * kk * dec, 0.0)
        eg = jnp.exp(gc)
        g_last = gc[:, c - 1:c, :]
        kdec = jnp.exp(g_last - gc)
        kb, vb, qd, kd = k * (bc * eg), v * bc, q * eg, k * kdec
        pm = jnp.where(causal, _bnt(q16, k16) * dec, 0.0)
        s = sp_ref[0]
        ds = ds_ref[0]
        s16, ds16 = s.astype(BF16), ds.astype(BF16)
        do16 = _heads(do_ref).astype(BF16)
        dvn16, vn16 = _heads(dvn_ref), _heads(vn_ref)
        tm = t_ref[...]
        t16 = tm.astype(BF16)

        dqd = _bnt(do16, s16)
        dp = jnp.where(causal, _bnt(do16, vn16), 0.0)
        dkd = _bnt(vn16, ds16)
        dgl = jnp.sum(jnp.sum(s * ds, axis=2, keepdims=True), axis=1, keepdims=True) * jnp.exp(g_last)
        dwk16 = (-_bnt(dvn16, s16)).astype(BF16)
        dt = _bnt(dwk16, kb.astype(BF16)) + _bnt(dvn16, vb.astype(BF16))
        dkb = _btn(t16, dwk16)
        dvb = _btn(t16, dvn16)
        th, tl = _split(tm)
        dth, dtl = _split(dt)
        xm = _btn(th, dth) + (_btn(tl, dth) + _btn(th, dtl))
        xh, xl = _split(xm)
        dlow = jnp.where(strict, -(_bnt(xh, th) + (_bnt(xl, th) + _bnt(xh, tl))), 0.0)
        dkk16 = (dlow * bc * dec).astype(BF16)
        dqk16 = (dp * dec).astype(BF16)

        dq = _bnn(dqk16, k16) + dqd * eg
        dk = _btn(dqk16, q16) + _bnn(dkk16, k16) + _btn(dkk16, k16) + dkb * (bc * eg) + dkd * kdec
        dv = dvb * bc
        for h in range(GDN_HEADS):
            hs = slice(GDN_D * h, GDN_D * (h + 1))
            dqkv_ref[0, :, hs] = dq[h]
            dqkv_ref[1, :, hs] = dk[h]
            dqkv_ref[2, :, hs] = dv[h]

        dbeta = (jnp.sum(dlow * kk * dec, axis=2, keepdims=True)
                 + jnp.sum(dkb * k, axis=2, keepdims=True) * eg + jnp.sum(dvb * v, axis=2, keepdims=True))
        mm = dlow * low + dp * pm
        mh, ml = _split(mm)
        ones16 = jnp.ones((GDN_HEADS, c, 128), BF16)
        col_sum = (_btn(mh, ones16) + _btn(ml, ones16))[:, :, 0:1]
        dkd_sum = jnp.sum(dkd * kd, axis=2, keepdims=True)
        dg = (jnp.sum(mm, axis=2, keepdims=True) - col_sum + jnp.sum(dkb * kb, axis=2, keepdims=True)
              + jnp.sum(dqd * qd, axis=2, keepdims=True) - dkd_sum)
        tail = jnp.sum(dkd_sum, axis=1, keepdims=True) + dgl
        dbeta_all = jnp.zeros((c, 128), F32)
        dg_all = jnp.zeros((c, 128), F32)
        for h in range(GDN_HEADS):
            dbeta_all = dbeta_all + jnp.where(lane == h, dbeta[h], 0.0)
            dg_all = dg_all + jnp.where(lane == GDN_HEADS + h, dg[h] + jnp.where(rowi == c - 1, tail[h], 0.0), 0.0)
        upper = (lax.broadcasted_iota(jnp.int32, (c, c), 0) <= lax.broadcasted_iota(jnp.int32, (c, c), 1)).astype(F32)
        dbg_ref[...] = dbeta_all + _nn(upper, dg_all, HIGHEST)

    row = pl.BlockSpec((c, GDN_W), lambda n: (n, 0))
    narrow = pl.BlockSpec((c, 128), lambda n: (n, 0))
    state = pl.BlockSpec((1, GDN_HEADS, GDN_D, GDN_D), lambda n: (n, 0, 0, 0))
    return pl.pallas_call(
        body, name=name, grid=(nchunks,),
        in_specs=[row, row, row, narrow, narrow, pl.BlockSpec((GDN_HEADS, c, c), lambda n: (0, n, 0)),
                  row, row, row, state, state],
        out_specs=[pl.BlockSpec((3, c, GDN_W), lambda n: (0, n, 0)), narrow],
        out_shape=[jax.ShapeDtypeStruct((3, s_len, GDN_W), F32), jax.ShapeDtypeStruct((s_len, 128), F32)],
        compiler_params=_params(("parallel",)),
    )(qn, kn, vc, bg, g_cum, t, do, dvn, vn, sprev, ds_all)


def _gdn_prep_bwd(u, dqkv, cw, du, *, name, tm=256):
    s_len = u.shape[0]
    nsteps = s_len // tm
    ext = tm + SHALO

    def body(uc, up, un, dc, dn, cw_ref, du_in_ref, du_ref, dcw_ref, xbuf, dbuf, pbuf, wacc):
        del du_in_ref
        part = pl.program_id(0)
        i = pl.program_id(1)

        @pl.when(i == 0)
        def _():
            wacc[...] = jnp.zeros_like(wacc)

        xbuf[:SHALO, :] = jnp.where(i > 0, up[...], 0.0)
        xbuf[SHALO:SHALO + tm, :] = uc[...]
        xbuf[SHALO + tm:, :] = jnp.where(i < nsteps - 1, un[...], 0.0)
        dbuf[:tm, :] = dc[...]
        dbuf[tm:, :] = jnp.where(i < nsteps - 1, dn[...], 0.0)
        first = SHALO - SHORT_CONV + 1
        w = [cw_ref[j:j + 1, :] for j in range(SHORT_CONV)]
        pre = jnp.zeros((ext, GDN_W), F32)
        for j in range(SHORT_CONV):
            pre = pre + xbuf[first + j:first + j + ext, :] * w[j]
        y = _silu(pre)
        dout = dbuf[...]
        scale = jnp.where(part == 0, GDN_D ** -0.5, 1.0)
        for h in range(GDN_HEADS):
            hs = slice(GDN_D * h, GDN_D * (h + 1))
            yh, dh = y[:, hs], dout[:, hs]
            rs = lax.rsqrt(jnp.sum(yh * yh, axis=-1, keepdims=True) + 1e-6)
            dyn = scale * rs * (dh - yh * (rs * rs) * jnp.sum(dh * yh, axis=-1, keepdims=True))
            dy = jnp.where(part < 2, dyn, dh)
            pbuf[:, hs] = dy * _dsilu(pre[:, hs])
        acc = jnp.zeros((tm, GDN_W), F32)
        dpre = pbuf[0:tm, :]
        for j in range(SHORT_CONV):
            k = SHORT_CONV - 1 - j
            acc = acc + pbuf[k:k + tm, :] * w[j]
            wacc[j] += (xbuf[first + j:first + j + tm, :] * dpre).reshape(tm // 8, 8, GDN_W).sum(axis=0)
        du_ref[...] = acc

        @pl.when(i == nsteps - 1)
        def _():
            for j in range(SHORT_CONV):
                dcw_ref[j:j + 1, :] = jnp.sum(wacc[j], axis=0, keepdims=True)
            dcw_ref[SHORT_CONV:, :] = jnp.zeros((SHALO - SHORT_CONV, GDN_W), F32)

    per = tm // SHALO
    return pl.pallas_call(
        body, name=name, grid=(3, nsteps),
        in_specs=[pl.BlockSpec((tm, GDN_W), lambda p, i: (i, COL_GQ + p)),
                  pl.BlockSpec((SHALO, GDN_W), lambda p, i: (jnp.maximum(i * per - 1, 0), COL_GQ + p)),
                  pl.BlockSpec((SHALO, GDN_W), lambda p, i: (jnp.minimum((i + 1) * per, s_len // SHALO - 1), COL_GQ + p)),
                  pl.BlockSpec((None, tm, GDN_W), lambda p, i: (p, i, 0)),
                  pl.BlockSpec((None, SHALO, GDN_W), lambda p, i: (p, jnp.minimum((i + 1) * per, s_len // SHALO - 1), 0)),
                  pl.BlockSpec((SHALO, GDN_W), lambda p, i: (0, p)),
                  pl.BlockSpec(memory_space=pl.ANY)],
        out_specs=[pl.BlockSpec((tm, GDN_W), lambda p, i: (i, COL_GQ + p)),
                   pl.BlockSpec((SHALO, GDN_W), lambda p, i: (0, p))],
        out_shape=[jax.ShapeDtypeStruct(du.shape, F32), jax.ShapeDtypeStruct((SHALO, 3 * GDN_W), F32)],
        scratch_shapes=[pltpu.VMEM((tm + 2 * SHALO, GDN_W), F32), pltpu.VMEM((ext, GDN_W), F32),
                        pltpu.VMEM((ext, GDN_W), F32), pltpu.VMEM((SHORT_CONV, 8, GDN_W), F32)],
        input_output_aliases={6: 0},
        compiler_params=_params(("arbitrary", "arbitrary")),
    )(u, u, u, dqkv, dqkv, cw, du)


def _gdn_ba_bwd(u, dbg, al, dtb, du, *, name, tm=256):
    s_len = u.shape[0]
    nsteps = s_len // tm
    wpad = IN_WP - COL_BA

    def body(uba, dbg_ref, al_ref, dtb_ref, du_in_ref, du_ref, sums_ref):
        del du_in_ref
        i = pl.program_id(0)

        @pl.when(i == 0)
        def _():
            sums_ref[...] = jnp.zeros_like(sums_ref)

        ba = uba[...]
        dbg = dbg_ref[...]
        lane = lax.broadcasted_iota(jnp.int32, ba.shape, 1)
        is_g = (lane >= GDN_HEADS) & (lane < 2 * GDN_HEADS)
        beta = _sigmoid(ba)
        z = ba + dtb_ref[...]
        ea = jnp.exp(al_ref[...])
        g = -ea * _softplus(z)
        dz = jnp.where(is_g, dbg * (-ea) * _sigmoid(z), 0.0)
        du_ref[:, :128] = jnp.where(lane < GDN_HEADS, dbg * beta * (1.0 - beta), dz)
        du_ref[:, 128:] = jnp.zeros((tm, wpad - 128), F32)
        sums_ref[0:1, :] += jnp.sum(jnp.where(is_g, dbg * g, 0.0), axis=0, keepdims=True)
        sums_ref[1:2, :] += jnp.sum(dz, axis=0, keepdims=True)

    vec = pl.BlockSpec((1, 128), lambda i: (0, 0))
    return pl.pallas_call(
        body, name=name, grid=(nsteps,),
        in_specs=[pl.BlockSpec((tm, 128), lambda i: (i, COL_BA // 128)), pl.BlockSpec((tm, 128), lambda i: (i, 0)),
                  vec, vec, pl.BlockSpec(memory_space=pl.ANY)],
        out_specs=[pl.BlockSpec((tm, wpad), lambda i: (i, COL_BA // wpad)), pl.BlockSpec((8, 128), lambda i: (0, 0))],
        out_shape=[jax.ShapeDtypeStruct(du.shape, F32), jax.ShapeDtypeStruct((8, 128), F32)],
        input_output_aliases={4: 0},
        compiler_params=_params(("arbitrary",)),
    )(u, dbg, al, dtb, du)


def _rope_tables(s_len):
    half = ROPE_DIM // 2
    inv = ROPE_THETA ** (-jnp.arange(half, dtype=F32) / half)
    ang = jnp.arange(s_len, dtype=F32)[:, None] * inv[None, :]
    cos, sin = jnp.cos(ang), jnp.sin(ang)
    one = jnp.ones((s_len, ATT_HD - ROPE_DIM), F32)
    zero = jnp.zeros((s_len, ATT_HD - ROPE_DIM), F32)
    zh = jnp.zeros((s_len, half), F32)
    c = jnp.concatenate([cos, cos, one], axis=1)
    s1 = jnp.concatenate([-sin, zh, zero], axis=1)
    s2 = jnp.concatenate([zh, sin, zero], axis=1)
    return tuple(jnp.concatenate([t, t], axis=1) for t in (c, s1, s2))


def _rope(x, c, s1, s2):
    return x * c + pltpu.roll(x, 128 - ROPE_DIM // 2, 1) * s1 + pltpu.roll(x, ROPE_DIM // 2, 1) * s2


def _rope_t(dy, c, s1, s2):
    return dy * c + pltpu.roll(dy * s1, ROPE_DIM // 2, 1) + pltpu.roll(dy * s2, 128 - ROPE_DIM // 2, 1)


DILATIONS = tuple(d for _, d in DIL_PATTERNS)
VIEW_ROWS = 256


def _to_view(scr, out_ref, dil, dtype):
    nblk, rows, _ = scr.shape
    width = nblk * 128
    for b in range(nblk):
        if dil == 1:
            out_ref[:, 128 * b:128 * (b + 1)] = scr[b].astype(dtype)
            continue
        for r in range(dil):
            out_ref[:, r * width + 128 * b:r * width + 128 * (b + 1)] = (
                scr.at[b][pl.ds(r, rows // dil, stride=dil), :].astype(dtype))


def _from_view(in_ref, scr, dil):
    nblk, rows, _ = scr.shape
    width = nblk * 128
    for b in range(nblk):
        for r in range(dil):
            scr.at[b][pl.ds(r, rows // dil, stride=dil), :] = in_ref[:, r * width + 128 * b:r * width + 128 * (b + 1)]


def _view_spec(dil, width, tm=VIEW_ROWS):
    return pl.BlockSpec((tm // dil, dil * width), lambda i: (i, 0))


def _view_shape(s_len, dil, width, dtype):
    return jax.ShapeDtypeStruct((s_len // dil, dil * width), dtype)


def _att_prep_fwd(u, tabs, *, name):
    s_len = u.shape[0]
    tm = VIEW_ROWS
    scale = ATT_HD ** -0.5
    nblk = ATT_W // 128

    def body(uq, uk, uv, c_ref, s1_ref, s2_ref, *rest):
        outs, scr = rest[:-1], rest[-1]
        c, s1, s2 = c_ref[...], s1_ref[...], s2_ref[...]
        for part, src in enumerate((uq, uk, uv)):
            for b in range(nblk):
                xb = src[:, 128 * b:128 * (b + 1)]
                if part == 0:
                    xb = _rope(xb, c, s1, s2) * scale
                elif part == 1:
                    xb = _rope(xb, c, s1, s2)
                scr[b] = xb
            for gi, dil in enumerate(DILATIONS):
                _to_view(scr, outs[3 * gi + part], dil, BF16)

    tab = pl.BlockSpec((tm, 128), lambda i: (i, 0))
    outs = pl.pallas_call(
        body, name=name, grid=(s_len // tm,),
        in_specs=[pl.BlockSpec((tm, ATT_W), lambda i, col=COL_AQ + j: (i, col)) for j in range(3)] + [tab] * 3,
        out_specs=[_view_spec(dil, ATT_W) for dil in DILATIONS for _ in range(3)],
        out_shape=[_view_shape(s_len, dil, ATT_W, BF16) for dil in DILATIONS for _ in range(3)],
        scratch_shapes=[pltpu.VMEM((nblk, tm, 128), F32)],
        compiler_params=_params(("parallel",)),
    )(u, u, u, *tabs)
    return [outs[3 * gi:3 * gi + 3] for gi in range(len(DILATIONS))]


def _head_lanes(h):
    lane = lax.broadcasted_iota(jnp.int32, (1, 128), 1)
    return (lane < ATT_HD) if h % 2 == 0 else (lane >= ATT_HD)


def _att_fwd(qr, kr, vb, dil, *, name):
    lr = qr.shape[0]
    nb = lr // ATT_BLOCK
    blk = ATT_BLOCK

    def body(q_ref, kp_ref, kc_ref, vp_ref, vc_ref, o_ref, lse_ref):
        n = pl.program_id(1)
        qi = lax.broadcasted_iota(jnp.int32, (blk, 2 * blk), 0)
        ki = lax.broadcasted_iota(jnp.int32, (blk, 2 * blk), 1)
        dist = qi + blk - ki
        valid = (dist >= 0) & (dist <= blk) & ((ki >= blk) | (n > 0))
        lane = lax.broadcasted_iota(jnp.int32, (blk, 128), 1)
        lse_all = jnp.zeros((blk, 128), F32)
        for hp in range(ATT_HEADS // 2):
            bs = slice(128 * hp, 128 * (hp + 1))
            qb = q_ref[:, bs]
            kb = jnp.concatenate([kp_ref[:, bs], kc_ref[:, bs]], axis=0)
            vv = jnp.concatenate([vp_ref[:, bs], vc_ref[:, bs]], axis=0)
            outs = []
            for sub in range(2):
                h = 2 * hp + sub
                qm = jnp.where(_head_lanes(h), qb, jnp.zeros_like(qb))
                s = jnp.where(valid, _nt(qm, kb), NEG_INF)
                m = jnp.max(s, axis=-1, keepdims=True)
                p = jnp.exp(s - m)
                l = jnp.sum(p, axis=-1, keepdims=True)
                outs.append(_nn((p / l).astype(BF16), vv))
                lse_all = lse_all + jnp.where(lane == h, m + jnp.log(l), 0.0)
            o_ref[:, bs] = jnp.where(lane < ATT_HD, outs[0], outs[1])
        lse_ref[...] = lse_all

    cur = pl.BlockSpec((blk, ATT_W), lambda r, n: (n, r))
    prev = pl.BlockSpec((blk, ATT_W), lambda r, n: (jnp.maximum(n - 1, 0), r))
    return pl.pallas_call(
        body, name=name, grid=(dil, nb), in_specs=[cur, prev, cur, prev, cur],
        out_specs=[cur, pl.BlockSpec((blk, 128), lambda r, n: (n, r))],
        out_shape=[jax.ShapeDtypeStruct(qr.shape, F32), jax.ShapeDtypeStruct((lr, dil * 128), F32)],
        compiler_params=_params(("parallel", "parallel")),
    )(qr, kr, kr, vb, vb)


def _att_bwd_dq(qr, kr, vb, do, lse, delta, dil, *, name):
    lr = qr.shape[0]
    nb = lr // ATT_BLOCK
    blk = ATT_BLOCK

    def body(q_ref, kp_ref, kc_ref, vp_ref, vc_ref, do_ref, lse_ref, dl_ref, dq_ref):
        n = pl.program_id(1)
        qi = lax.broadcasted_iota(jnp.int32, (blk, 2 * blk), 0)
        ki = lax.broadcasted_iota(jnp.int32, (blk, 2 * blk), 1)
        dist = qi + blk - ki
        valid = (dist >= 0) & (dist <= blk) & ((ki >= blk) | (n > 0))
        lane = lax.broadcasted_iota(jnp.int32, (blk, 128), 1)
        for hp in range(ATT_HEADS // 2):
            bs = slice(128 * hp, 128 * (hp + 1))
            qb = q_ref[:, bs]
            dob = do_ref[:, bs].astype(BF16)
            kb = jnp.concatenate([kp_ref[:, bs], kc_ref[:, bs]], axis=0)
            vv = jnp.concatenate([vp_ref[:, bs], vc_ref[:, bs]], axis=0)
            outs = []
            for sub in range(2):
                h = 2 * hp + sub
                hm = _head_lanes(h)
                s = _nt(jnp.where(hm, qb, jnp.zeros_like(qb)), kb)
                p = jnp.where(valid, jnp.exp(s - lse_ref[:, h:h + 1]), 0.0)
                dp = _nt(jnp.where(hm, dob, jnp.zeros_like(dob)), vv)
                outs.append(_nn((p * (dp - dl_ref[:, h:h + 1])).astype(BF16), kb))
            dq_ref[:, bs] = jnp.where(lane < ATT_HD, outs[0], outs[1])

    cur = pl.BlockSpec((blk, ATT_W), lambda r, n: (n, r))
    prev = pl.BlockSpec((blk, ATT_W), lambda r, n: (jnp.maximum(n - 1, 0), r))
    nar = pl.BlockSpec((blk, 128), lambda r, n: (n, r))
    return pl.pallas_call(
        body, name=name, grid=(dil, nb), in_specs=[cur, prev, cur, prev, cur, cur, nar, nar],
        out_specs=cur, out_shape=jax.ShapeDtypeStruct(qr.shape, F32),
        compiler_params=_params(("parallel", "parallel")),
    )(qr, kr, kr, vb, vb, do, lse, delta)


def _att_bwd_dkv(qr, kr, vb, do, lse, delta, dil, *, name):
    lr = qr.shape[0]
    nb = lr // ATT_BLOCK
    blk = ATT_BLOCK

    def body(k_ref, v_ref, qa_ref, qb_ref, doa_ref, dob_ref, la_ref, lb_ref, da_ref, db_ref, dk_ref, dv_ref):
        n = pl.program_id(1)
        qi = lax.broadcasted_iota(jnp.int32, (2 * blk, blk), 0)
        ki = lax.broadcasted_iota(jnp.int32, (2 * blk, blk), 1)
        dist = qi - ki
        valid = (dist >= 0) & (dist <= blk) & ((qi < blk) | (n < nb - 1))
        lse2 = jnp.concatenate([la_ref[...], lb_ref[...]], axis=0)
        dl2 = jnp.concatenate([da_ref[...], db_ref[...]], axis=0)
        for hp in range(ATT_HEADS // 2):
            bs = slice(128 * hp, 128 * (hp + 1))
            kb, vv = k_ref[:, bs], v_ref[:, bs]
            q2 = jnp.concatenate([qa_ref[:, bs], qb_ref[:, bs]], axis=0)
            do2 = jnp.concatenate([doa_ref[:, bs], dob_ref[:, bs]], axis=0).astype(BF16)
            dk_acc = jnp.zeros((blk, 128), F32)
            dv_acc = jnp.zeros((blk, 128), F32)
            for sub in range(2):
                h = 2 * hp + sub
                hm = _head_lanes(h)
                qm = jnp.where(hm, q2, jnp.zeros_like(q2))
                dom = jnp.where(hm, do2, jnp.zeros_like(do2))
                p = jnp.where(valid, jnp.exp(_nt(qm, kb) - lse2[:, h:h + 1]), 0.0)
                dv_acc = dv_acc + _tn(p.astype(BF16), dom)
                ds = p * (_nt(dom, vv) - dl2[:, h:h + 1])
                dk_acc = dk_acc + _tn(ds.astype(BF16), qm)
            dk_ref[:, bs] = dk_acc
            dv_ref[:, bs] = dv_acc

    cur = pl.BlockSpec((blk, ATT_W), lambda r, n: (n, r))
    nxt = pl.BlockSpec((blk, ATT_W), lambda r, n: (jnp.minimum(n + 1, nb - 1), r))
    ncur = pl.BlockSpec((blk, 128), lambda r, n: (n, r))
    nnxt = pl.BlockSpec((blk, 128), lambda r, n: (jnp.minimum(n + 1, nb - 1), r))
    return pl.pallas_call(
        body, name=name, grid=(dil, nb),
        in_specs=[cur, cur, cur, nxt, cur, nxt, ncur, nnxt, ncur, nnxt],
        out_specs=[cur, cur], out_shape=[jax.ShapeDtypeStruct(qr.shape, F32)] * 2,
        compiler_params=_params(("parallel", "parallel")),
    )(kr, vb, qr, qr, do, do, lse, lse, delta, delta)


def _att_prep_bwd(dgroups, tabs, du, *, name):
    s_len = du.shape[0]
    tm = VIEW_ROWS
    scale = ATT_HD ** -0.5
    nblk = ATT_W // 128
    ng = len(DILATIONS)

    def body(*refs):
        grads = refs[:3 * ng]
        c_ref, s1_ref, s2_ref, _, du_ref = refs[3 * ng:3 * ng + 5]
        scrs = refs[3 * ng + 5:]
        c, s1, s2 = c_ref[...], s1_ref[...], s2_ref[...]
        for part in range(3):
            for gi, dil in enumerate(DILATIONS):
                if dil > 1:
                    _from_view(grads[3 * gi + part], scrs[gi], dil)
            for b in range(nblk):
                tot = None
                for gi, dil in enumerate(DILATIONS):
                    term = grads[3 * gi + part][:, 128 * b:128 * (b + 1)] if dil == 1 else scrs[gi][b]
                    tot = term if tot is None else tot + term
                if part == 0:
                    tot = _rope_t(tot * scale, c, s1, s2)
                elif part == 1:
                    tot = _rope_t(tot, c, s1, s2)
                du_ref[:, ATT_W * part + 128 * b:ATT_W * part + 128 * (b + 1)] = tot

    tab = pl.BlockSpec((tm, 128), lambda i: (i, 0))
    return pl.pallas_call(
        body, name=name, grid=(s_len // tm,),
        in_specs=[_view_spec(dil, ATT_W) for dil in DILATIONS for _ in range(3)] + [tab] * 3
        + [pl.BlockSpec(memory_space=pl.ANY)],
        out_specs=pl.BlockSpec((tm, 3 * ATT_W), lambda i: (i, COL_AQ // 3)),
        out_shape=jax.ShapeDtypeStruct(du.shape, F32),
        scratch_shapes=[pltpu.VMEM((nblk, tm, 128), F32) for _ in DILATIONS],
        input_output_aliases={3 * ng + 3: 0},
        compiler_params=_params(("parallel",)),
    )(*[a for g in dgroups for a in g], *tabs, du)


def _head_weights(w, b):
    lane = lax.broadcasted_iota(jnp.int32, (1, 128), 1)
    return jnp.where(lane < ATT_HD, w[:, 2 * b:2 * b + 1], w[:, 2 * b + 1:2 * b + 2])


def _assemble_fwd(pw, u, o_gdn, gnw, o_groups, lse_groups, *, name):
    s_len = u.shape[0]
    tm = VIEW_ROWS
    c = CONV_CH
    nblk = ATT_W // 128
    ng = len(DILATIONS)

    def body(*refs):
        pw_ref, cg_ref, z_ref, ag_ref, og_ref, gnw_ref = refs[:6]
        o_refs, l_refs = refs[6:6 + ng], refs[6 + ng:6 + 2 * ng]
        y_ref, oa_ref = refs[6 + 2 * ng:8 + 2 * ng]
        lse_outs = refs[8 + 2 * ng:8 + 3 * ng]
        o_scr, l_scr = refs[8 + 3 * ng:8 + 4 * ng], refs[8 + 4 * ng:8 + 5 * ng]
        lse_scr = refs[8 + 5 * ng]
        y_ref[:, :c] = (pw_ref[...] * _silu(cg_ref[...])).astype(BF16)
        gw = gnw_ref[...]
        for h in range(GDN_HEADS):
            hs = slice(GDN_D * h, GDN_D * (h + 1))
            oh = og_ref[:, hs]
            yn = oh * lax.rsqrt(jnp.mean(oh * oh, axis=-1, keepdims=True) + 1e-6) * gw
            y_ref[:, c + GDN_D * h:c + GDN_D * (h + 1)] = (yn * _silu(z_ref[:, hs])).astype(BF16)
        for gi, dil in enumerate(DILATIONS):
            if dil > 1:
                _from_view(o_refs[gi], o_scr[gi], dil)
                _from_view(l_refs[gi], l_scr[gi], dil)
        ls = [l_refs[gi][...] if dil == 1 else l_scr[gi][0] for gi, dil in enumerate(DILATIONS)]
        m = functools.reduce(jnp.maximum, ls)
        es = [jnp.exp(l - m) for l in ls]
        den = functools.reduce(lambda a, b: a + b, es)
        lse_scr[0] = m + jnp.log(den)
        ws = [e / den for e in es]
        for b in range(nblk):
            bs = slice(128 * b, 128 * (b + 1))
            o = None
            for gi, dil in enumerate(DILATIONS):
                term = _head_weights(ws[gi], b) * (o_refs[gi][:, bs] if dil == 1 else o_scr[gi][b])
                o = term if o is None else o + term
            oa_ref[:, bs] = o
            y_ref[:, c + GDN_W + 128 * b:c + GDN_W + 128 * (b + 1)] = (o * _silu(ag_ref[:, bs])).astype(BF16)
        for gi, dil in enumerate(DILATIONS):
            _to_view(lse_scr, lse_outs[gi], dil, F32)

    wide = pl.BlockSpec((tm, 768), lambda i: (i, 0))
    return pl.pallas_call(
        body, name=name, grid=(s_len // tm,),
        in_specs=[pl.BlockSpec((tm, c), lambda i: (i, 0)), pl.BlockSpec((tm, c), lambda i: (i, 1024 // c)),
                  pl.BlockSpec((tm, 768), lambda i: (i, COL_GQ + 3)), pl.BlockSpec((tm, 768), lambda i: (i, COL_AQ + 3)),
                  wide, pl.BlockSpec((1, 128), lambda i: (0, 0))]
        + [_view_spec(dil, ATT_W) for dil in DILATIONS] + [_view_spec(dil, 128) for dil in DILATIONS],
        out_specs=[pl.BlockSpec((tm, D_MODEL), lambda i: (i, 0)), wide] + [_view_spec(dil, 128) for dil in DILATIONS],
        out_shape=[jax.ShapeDtypeStruct((s_len, D_MODEL), BF16), jax.ShapeDtypeStruct((s_len, ATT_W), F32)]
        + [_view_shape(s_len, dil, 128, F32) for dil in DILATIONS],
        scratch_shapes=[pltpu.VMEM((nblk, tm, 128), F32) for _ in DILATIONS]
        + [pltpu.VMEM((1, tm, 128), F32) for _ in DILATIONS] + [pltpu.VMEM((1, tm, 128), F32)],
        compiler_params=_params(("parallel",)),
    )(pw, u, u, u, o_gdn, gnw, *o_groups, *lse_groups)


def _assemble_bwd(dy, pw, u, o_gdn, gnw, o_att, *, name):
    s_len = u.shape[0]
    tm = VIEW_ROWS
    c = CONV_CH
    nsteps = s_len // tm
    nblk = ATT_W // 128
    ng = len(DILATIONS)

    def body(dy_ref, pw_ref, cg_ref, z_ref, ag_ref, og_ref, gnw_ref, oa_ref,
             du_ref, dpw_ref, dog_ref, dgw_ref, *rest):
        do_outs, dl_outs = rest[:ng], rest[ng:2 * ng]
        acc_ref, do_scr, dl_scr = rest[2 * ng:]
        i = pl.program_id(0)

        @pl.when(i == 0)
        def _():
            acc_ref[...] = jnp.zeros_like(acc_ref)

        du_ref[...] = jnp.zeros_like(du_ref)
        dyc = dy_ref[:, :c]
        cg = cg_ref[...]
        dpw_ref[...] = dyc * _silu(cg)
        du_ref[:, 1024:1024 + c] = dyc * pw_ref[...] * _dsilu(cg)
        gw = gnw_ref[...]
        dgw = jnp.zeros((8, 128), F32)
        for h in range(GDN_HEADS):
            hs = slice(GDN_D * h, GDN_D * (h + 1))
            oh = og_ref[:, hs]
            zh = z_ref[:, hs]
            dyh = dy_ref[:, c + GDN_D * h:c + GDN_D * (h + 1)]
            r = lax.rsqrt(jnp.mean(oh * oh, axis=-1, keepdims=True) + 1e-6)
            xn = oh * r
            dyn = dyh * _silu(zh)
            du_ref[:, GDN_W * (COL_GQ + 3) + GDN_D * h:GDN_W * (COL_GQ + 3) + GDN_D * (h + 1)] = dyh * xn * gw * _dsilu(zh)
            dgw = dgw + (dyn * xn).reshape(tm // 8, 8, 128).sum(axis=0)
            dxn = dyn * gw
            dog_ref[:, hs] = r * (dxn - xn * jnp.mean(dxn * xn, axis=-1, keepdims=True))
        acc_ref[...] += dgw
        lane = lax.broadcasted_iota(jnp.int32, (tm, 128), 1)
        delta = jnp.zeros((tm, 128), F32)
        for b in range(ATT_W // 128):
            bs = slice(128 * b, 128 * (b + 1))
            dya = dy_ref[:, c + GDN_W + 128 * b:c + GDN_W + 128 * (b + 1)]
            ag = ag_ref[:, bs]
            oa = oa_ref[:, bs]
            do = dya * _silu(ag)
            do_scr[b] = do
            du_ref[:, ATT_W * (COL_AQ + 3) + 128 * b:ATT_W * (COL_AQ + 3) + 128 * (b + 1)] = dya * oa * _dsilu(ag)
            prod = do * oa
            lo = jnp.sum(jnp.where(lane < ATT_HD, prod, 0.0), axis=-1, keepdims=True)
            hi = jnp.sum(jnp.where(lane >= ATT_HD, prod, 0.0), axis=-1, keepdims=True)
            delta = delta + jnp.where(lane == 2 * b, lo, 0.0) + jnp.where(lane == 2 * b + 1, hi, 0.0)
        dl_scr[0] = delta
        for gi, dil in enumerate(DILATIONS):
            _to_view(do_scr, do_outs[gi], dil, BF16)
            _to_view(dl_scr, dl_outs[gi], dil, F32)

        @pl.when(i == nsteps - 1)
        def _():
            dgw_ref[...] = jnp.sum(acc_ref[...], axis=0, keepdims=True)

    wide = pl.BlockSpec((tm, 768), lambda i: (i, 0))
    vec = pl.BlockSpec((1, 128), lambda i: (0, 0))
    outs = pl.pallas_call(
        body, name=name, grid=(nsteps,),
        in_specs=[pl.BlockSpec((tm, D_MODEL), lambda i: (i, 0)), pl.BlockSpec((tm, c), lambda i: (i, 0)),
                  pl.BlockSpec((tm, c), lambda i: (i, 1024 // c)), pl.BlockSpec((tm, 768), lambda i: (i, COL_GQ + 3)),
                  pl.BlockSpec((tm, 768), lambda i: (i, COL_AQ + 3)), wide, vec, wide],
        out_specs=[pl.BlockSpec((tm, IN_WP), lambda i: (i, 0)), pl.BlockSpec((tm, c), lambda i: (i, 0)), wide, vec]
        + [_view_spec(dil, ATT_W) for dil in DILATIONS] + [_view_spec(dil, 128) for dil in DILATIONS],
        out_shape=[jax.ShapeDtypeStruct((s_len, IN_WP), F32), jax.ShapeDtypeStruct((s_len, c), F32),
                   jax.ShapeDtypeStruct((s_len, GDN_W), F32), jax.ShapeDtypeStruct((1, 128), F32)]
        + [_view_shape(s_len, dil, ATT_W, BF16) for dil in DILATIONS]
        + [_view_shape(s_len, dil, 128, F32) for dil in DILATIONS],
        scratch_shapes=[pltpu.VMEM((8, 128), F32), pltpu.VMEM((nblk, tm, 128), F32), pltpu.VMEM((1, tm, 128), F32)],
        compiler_params=_params(("arbitrary",)),
    )(dy, pw, u, u, u, o_gdn, gnw, o_att)
    return outs[:4], outs[4:4 + ng], outs[4 + ng:]


def _layer_fwd(x, p, tabs):
    h = _rms_fwd(x, p["norm_w"], name="rms_fwd")
    u = _matmul(h, p["wp"], name="in_proj", tk=2048)
    conv, sw = _conf_fwd(u, p["dw_w"], p["dw_b"], p["ln_w"], p["ln_b"], name="conf_fwd")
    pw = _matmul(sw, p["pw_w"], name="conf_pw")
    qn, kn, vc, bg = _gdn_prep_fwd(u, p["cw"], p["al"], p["dtb"], name="gdn_prep_fwd")
    wk, wv, qd, kd, pm, t, g_cum = _gdn_intra_fwd(qn, kn, vc, bg, name="gdn_intra_fwd")
    o_gdn, vn, sprev = _gdn_scan_fwd(wk, wv, qd, kd, pm, g_cum, name="gdn_scan_fwd")
    qkv = _att_prep_fwd(u, tabs, name="att_prep_fwd")
    groups = [_att_fwd(*qkv[gi], dil, name=f"att_fwd_d{dil}") for gi, dil in enumerate(DILATIONS)]
    outs = _assemble_fwd(pw, u, o_gdn, p["gnw"], [g[0] for g in groups], [g[1] for g in groups],
                         name="assemble_fwd")
    y, o_att, lse = outs[0], outs[1], outs[2:]
    x_next = _matmul(y, p["wout"], add=x, name="out_proj", tk=2048)
    saved = dict(x=x, h=h, u=u, conv=conv, sw=sw, pw=pw, qn=qn, kn=kn, vc=vc, bg=bg, wk=wk, qd=qd, kd=kd, pm=pm,
                 t=t, g_cum=g_cum, vn=vn, sprev=sprev, o_gdn=o_gdn, qkv=qkv, o_att=o_att, lse=lse, y=y)
    return x_next, saved


def _layer_bwd(dx_out, s, p, tabs, layer, big):
    dy = _matmul(dx_out, p["wout"], tb=True, name="out_proj_dy", tk=2048)
    d_wout = _matmul(s["y"], dx_out, ta=True, name="out_proj_dw", tn=2048, stack=(big[1], layer, DEPTH))
    (du, dpw, dog, dgw), do_views, dl_views = _assemble_bwd(dy, s["pw"], s["u"], s["o_gdn"], p["gnw"], s["o_att"],
                                                            name="assemble_bwd")
    dsw = _matmul(dpw, p["pw_w"], tb=True, name="conf_pw_dx")
    d_pw_w = _matmul(s["sw"], dpw, ta=True, name="conf_pw_dw", stack=(big[2], layer, DEPTH))
    dconv, ln_sums = _conf_bwd_ln(dsw, s["conv"], p["ln_w"], p["ln_b"], name="conf_bwd_ln")
    du, d_dw_w = _conf_bwd_conv(s["u"], dconv, p["dw_w"], du, name="conf_bwd_conv")
    dvn, ds_all = _gdn_scan_bwd(dog, s["wk"], s["qd"], s["kd"], s["pm"], s["g_cum"], name="gdn_scan_bwd")
    dqkv, dbg = _gdn_intra_bwd(s["qn"], s["kn"], s["vc"], s["bg"], s["g_cum"], s["t"], dog, dvn, s["vn"],
                               s["sprev"], ds_all, name="gdn_intra_bwd")
    du, d_cw = _gdn_prep_bwd(s["u"], dqkv, p["cw"], du, name="gdn_prep_bwd")
    du, ba_sums = _gdn_ba_bwd(s["u"], dbg, p["al"], p["dtb"], du, name="gdn_ba_bwd")
    dgroups = []
    for gi, dil in enumerate(DILATIONS):
        args = (*s["qkv"][gi], do_views[gi], s["lse"][gi], dl_views[gi], dil)
        dq = _att_bwd_dq(*args, name=f"att_bwd_dq_d{dil}")
        dk, dv = _att_bwd_dkv(*args, name=f"att_bwd_dkv_d{dil}")
        dgroups.append([dq, dk, dv])
    du = _att_prep_bwd(dgroups, tabs, du, name="att_prep_bwd")
    dh = _matmul(du, p["wp"], tb=True, name="in_proj_dx", tn=2048)
    d_wp = _matmul(s["h"], du, ta=True, name="in_proj_dw", tn=2048, stack=(big[0], layer, DEPTH))
    dx, d_norm_w = _rms_bwd(s["x"], dh, p["norm_w"], dx_out, name="rms_bwd")
    small = dict(norm_w=d_norm_w, gnw=dgw, ln_sums=ln_sums, dw_w=d_dw_w, cw=d_cw, ba_sums=ba_sums)
    return dx, (d_wp, d_wout, d_pw_w), small


def _trunk(x, target, params, final_norm_w):
    tabs = _rope_tables(x.shape[0])
    layers = [{k: v[l] for k, v in params.items()} for l in range(DEPTH)]
    saved = []
    for p in layers:
        x, s = _layer_fwd(x, p, tabs)
        saved.append(s)
    dx, d_final, loss = _loss_head(x, final_norm_w, target, name="loss_head")
    big = (None, None, None)
    small = [None] * DEPTH
    for l in reversed(range(DEPTH)):
        dx, big, small[l] = _layer_bwd(dx, saved[l], layers[l], tabs, l, big)
    grads = {k: jnp.stack([sm[k] for sm in small]) for k in small[0]}
    grads.update(wp=big[0], wout=big[1], pw_w=big[2])
    return loss[0, 0], dx, grads, d_final


ANY = pl.BlockSpec(memory_space=pl.ANY)


def _position():
    return lax.axis_index("x"), lax.axis_index("y"), lax.axis_index("c")


def _other_chips(x, y):
    return [(1 - x, y), (x, 1 - y), (1 - x, 1 - y)]


def _gather_chips(shards, *, name):
    n = len(shards)

    def body(*refs):
        ins, outs = refs[:n], refs[n:2 * n]
        send, recv, lsem = refs[2 * n:]
        x, y, c = _position()
        sib = (x, y, 1 - c)
        chips = _other_chips(x, y)
        half = pl.ds(2 * c, 2)

        def copy(k, a, chip, rows, to, src=None):
            dst = outs[a].at[2 * chip[0] + chip[1], rows]
            return pltpu.make_async_remote_copy(
                src_ref=dst if src is None else src, dst_ref=dst, send_sem=send.at[k * n + a],
                recv_sem=recv.at[k * n + a], device_id=to, device_id_type=MESH)

        mine = [pltpu.make_async_copy(ins[a], outs[a].at[2 * x + y], lsem.at[a]) for a in range(n)]
        for cp in mine:
            cp.start()
        first = [copy(j, a, (x, y), half, (*chip, c), src=ins[a].at[half])
                 for j, chip in enumerate(chips) for a in range(n)]
        for cp in first:
            cp.start()
        passed = []
        for j, chip in enumerate(chips):
            for a in range(n):
                copy(j, a, chip, half, (x, y, c)).wait_recv()
                cp = copy(3 + j, a, chip, half, sib)
                cp.start()
                passed.append(cp)
        for j, chip in enumerate(chips):
            for a in range(n):
                copy(3 + j, a, chip, pl.ds(2 * (1 - c), 2), (x, y, c)).wait_recv()
        for cp in first + passed:
            cp.wait_send()
        for cp in mine:
            cp.wait()

    return pl.pallas_call(
        body, name=name, in_specs=[ANY] * n, out_specs=[ANY] * n,
        out_shape=[jax.ShapeDtypeStruct((4,) + s.shape, s.dtype) for s in shards],
        scratch_shapes=[pltpu.SemaphoreType.DMA((6 * n,)), pltpu.SemaphoreType.DMA((6 * n,)),
                        pltpu.SemaphoreType.DMA((n,))],
    )(*shards)


def _to_sibling(arrs, *, name):
    n = len(arrs)

    def body(*refs):
        ins, outs = refs[:n], refs[n:2 * n]
        send, recv = refs[2 * n:]
        x, y, c = _position()
        cps = [pltpu.make_async_remote_copy(src_ref=ins[a], dst_ref=outs[a], send_sem=send.at[a],
                                            recv_sem=recv.at[a], device_id=(x, y, 1 - c), device_id_type=MESH)
               for a in range(n)]
        for cp in cps:
            cp.start()
        for cp in cps:
            cp.wait()

    return pl.pallas_call(
        body, name=name, in_specs=[ANY] * n, out_specs=[ANY] * n,
        out_shape=[jax.ShapeDtypeStruct(a.shape, a.dtype) for a in arrs],
        scratch_shapes=[pltpu.SemaphoreType.DMA((n,)), pltpu.SemaphoreType.DMA((n,))],
    )(*arrs)


def _to_chips(arrs, *, name):
    n = len(arrs)

    def body(*refs):
        ins, outs = refs[:n], refs[n:2 * n]
        send, recv = refs[2 * n:]
        x, y, c = _position()
        cps = [pltpu.make_async_remote_copy(
            src_ref=ins[a].at[2 * chip[0] + chip[1]], dst_ref=outs[a].at[j], send_sem=send.at[j * n + a],
            recv_sem=recv.at[j * n + a], device_id=(*chip, c), device_id_type=MESH)
            for j, chip in enumerate(_other_chips(x, y)) for a in range(n)]
        for cp in cps:
            cp.start()
        for cp in cps:
            cp.wait()

    return pl.pallas_call(
        body, name=name, in_specs=[ANY] * n, out_specs=[ANY] * n,
        out_shape=[jax.ShapeDtypeStruct((3,) + a.shape[1:], a.dtype) for a in arrs],
        scratch_shapes=[pltpu.SemaphoreType.DMA((3 * n,)), pltpu.SemaphoreType.DMA((3 * n,))],
    )(*arrs)


def _join_halves(halves, *, name):
    n = len(halves)

    def body(*refs):
        ins, outs = refs[:n], refs[n:2 * n]
        send, recv, lsem = refs[2 * n:]
        x, y, c = _position()
        mine = pl.ds(2 * c, 2)
        loc = [pltpu.make_async_copy(ins[a], outs[a].at[mine], lsem.at[a]) for a in range(n)]
        cps = [pltpu.make_async_remote_copy(src_ref=ins[a], dst_ref=outs[a].at[mine], send_sem=send.at[a],
                                            recv_sem=recv.at[a], device_id=(x, y, 1 - c), device_id_type=MESH)
               for a in range(n)]
        for cp in loc + cps:
            cp.start()
        for a in range(n):
            cps[a].wait_send()
            pltpu.make_async_remote_copy(src_ref=ins[a], dst_ref=outs[a].at[pl.ds(2 * (1 - c), 2)],
                                         send_sem=send.at[a], recv_sem=recv.at[a], device_id=(x, y, 1 - c),
                                         device_id_type=MESH).wait_recv()
            loc[a].wait()

    return pl.pallas_call(
        body, name=name, in_specs=[ANY] * n, out_specs=[ANY] * n,
        out_shape=[jax.ShapeDtypeStruct((4,) + h.shape[1:], h.dtype) for h in halves],
        scratch_shapes=[pltpu.SemaphoreType.DMA((n,)), pltpu.SemaphoreType.DMA((n,)), pltpu.SemaphoreType.DMA((n,))],
    )(*halves)


def _allreduce_small(packed, *, name):
    rows = packed.shape[0]
    ndev = 8

    def body(x_ref, sum_ref, all_ref, send, recv, lsem):
        x, y, c = _position()
        me, sib = (x, y, c), (x, y, 1 - c)
        chips = _other_chips(x, y)

        def blk(px, py, pc):
            return all_ref.at[pl.ds((4 * px + 2 * py + pc) * rows, rows), :]

        def copy(k, block, to, src=None):
            return pltpu.make_async_remote_copy(
                src_ref=blk(*block) if src is None else src, dst_ref=blk(*block), send_sem=send.at[k],
                recv_sem=recv.at[k], device_id=to, device_id_type=MESH)

        mine = pltpu.make_async_copy(x_ref, blk(*me), lsem)
        mine.start()
        first = [copy(0, me, sib, src=x_ref)] + [copy(1 + j, me, (*chip, c), src=x_ref) for j, chip in enumerate(chips)]
        for cp in first:
            cp.start()
        passed = [copy(4 + j, (*chip, c), sib) for j, chip in enumerate(chips)]
        for j, chip in enumerate(chips):
            copy(1 + j, (*chip, c), me).wait_recv()
            passed[j].start()
        copy(0, sib, me).wait_recv()
        for j, chip in enumerate(chips):
            copy(4 + j, (*chip, 1 - c), me).wait_recv()
        for cp in first + passed:
            cp.wait_send()
        mine.wait()
        acc = all_ref[0:rows, :]
        for d in range(1, ndev):
            acc = acc + all_ref[d * rows:(d + 1) * rows, :]
        sum_ref[...] = acc

    vm = pl.BlockSpec(memory_space=pltpu.VMEM)
    return pl.pallas_call(
        body, name=name, in_specs=[vm], out_specs=vm, out_shape=jax.ShapeDtypeStruct((rows, 128), F32),
        scratch_shapes=[pltpu.VMEM((ndev * rows, 128), F32), pltpu.SemaphoreType.DMA((7,)),
                        pltpu.SemaphoreType.DMA((7,)), pltpu.SemaphoreType.DMA],
        compiler_params=pltpu.CompilerParams(vmem_limit_bytes=VMEM_LIMIT),
    )(packed)


def _pack(arrs):
    flat = jnp.concatenate([a.reshape(-1) for a in arrs])
    pad = (-flat.shape[0]) % 1024
    return jnp.pad(flat, (0, pad)).reshape(-1, 128)


def _unpack(packed, shapes):
    flat = packed.reshape(-1)
    out, pos = [], 0
    for s in shapes:
        size = math.prod(s)
        out.append(flat[pos:pos + size].reshape(s))
        pos += size
    return out


def _pad_cols(w):
    zeros = jnp.zeros(w.shape[:-1] + (IN_WP - IN_W,), w.dtype)
    return jnp.concatenate([w[..., :ORIG_BA], w[..., ORIG_ATT:], w[..., ORIG_BA:ORIG_ATT], zeros], axis=-1)


def _unpad_cols(w):
    n_att = IN_W - ORIG_ATT
    return jnp.concatenate([w[..., :ORIG_BA], w[..., COL_BA:COL_BA + ORIG_ATT - ORIG_BA],
                            w[..., ORIG_BA:ORIG_BA + n_att]], axis=-1)


def _lanes(v, first):
    return jnp.pad(v, ((0, 0), (first, 128 - first - v.shape[1])))[:, None, :]


def _by_chip(g, axis):
    shape = g.shape[:axis] + (4, g.shape[axis] // 4) + g.shape[axis + 1:]
    return jnp.moveaxis(g.reshape(shape), axis, 0)


def kernel(x, norm_w, w_in, conv_qkv_w, a_log, dt_bias, gdn_norm_w, conf_dw_w, conf_dw_b, conf_ln_w, conf_ln_b, conf_pw_w, w_out, final_norm_w, loss_target, m_norm_w, m_w_in, m_conv_qkv_w, m_a_log, m_dt_bias, m_gdn_norm_w, m_conf_dw_w, m_conf_dw_b, m_conf_ln_w, m_conf_ln_b, m_conf_pw_w, m_w_out, m_final_norm_w, v_norm_w, v_w_in, v_conv_qkv_w, v_a_log, v_dt_bias, v_gdn_norm_w, v_conf_dw_w, v_conf_dw_b, v_conf_ln_w, v_conf_ln_b, v_conf_pw_w, v_w_out, v_final_norm_w):
    xi, yi, ci = _position()
    chip = 2 * xi + yi

    g_in, g_out, g_pw, g_cw, g_dw = _gather_chips(
        [w_in.astype(BF16), w_out.astype(BF16), conf_pw_w.astype(BF16), conv_qkv_w, conf_dw_w], name="gather_weights")
    w_in_full = jnp.moveaxis(g_in, 0, 2).reshape(DEPTH, D_MODEL, IN_W)
    cw_full = jnp.moveaxis(g_cw, 0, 2).reshape(DEPTH, SHORT_CONV, 3 * GDN_W)
    dw_full = jnp.moveaxis(g_dw, 0, 2).reshape(DEPTH, CONV_WIDTH, CONV_CH)
    params = dict(
        norm_w=norm_w[:, None, :],
        wp=_pad_cols(w_in_full),
        wout=jnp.moveaxis(g_out, 0, 1).reshape(DEPTH, D_MODEL, D_MODEL),
        pw_w=jnp.moveaxis(g_pw, 0, 1).reshape(DEPTH, CONV_CH, CONV_CH),
        cw=jnp.pad(cw_full, ((0, 0), (0, SHALO - SHORT_CONV), (0, 0))),
        dw_w=jnp.pad(dw_full, ((0, 0), (0, HALO - CONV_WIDTH), (0, 0))),
        al=_lanes(a_log, GDN_HEADS), dtb=_lanes(dt_bias, GDN_HEADS), gnw=gdn_norm_w[:, None, :],
        dw_b=conf_dw_b[:, None, :], ln_w=conf_ln_w[:, None, :], ln_b=conf_ln_b[:, None, :],
    )

    loss_part, grad_x, grads, d_final = _trunk(x[0], loss_target[0], params, final_norm_w[None, :])
    loss = lax.psum(loss_part, ("x", "y", "c"))

    big = [_by_chip(_unpad_cols(grads["wp"]), 2), _by_chip(grads["wout"], 1), _by_chip(grads["pw_w"], 1)]
    keep = [lax.dynamic_slice_in_dim(g, 2 * ci, 2, axis=1) for g in big]
    give = [lax.dynamic_slice_in_dim(g, 2 * (1 - ci), 2, axis=1).astype(BF16) for g in big]
    got = _to_sibling(give, name="grads_to_sibling")
    pair = [_sum_arrays([k.reshape((8,) + k.shape[2:]), r.reshape((8,) + r.shape[2:])], name=f"pair_sum_{i}",
                        out_dtype=BF16).reshape(k.shape) for i, (k, r) in enumerate(zip(keep, got))]
    arrived = _to_chips(pair, name="grads_to_chips")
    halves = []
    for i, (pr, ar) in enumerate(zip(pair, arrived)):
        own = lax.dynamic_index_in_dim(pr, chip, axis=0, keepdims=False)
        halves.append(_sum_arrays([own, ar[0], ar[1], ar[2]], name=f"chip_sum_{i}", out_dtype=F32))
    g_w_in, g_w_out, g_pw_w = _join_halves(halves, name="join_halves")

    ba = grads["ba_sums"]
    small = [grads["norm_w"], ba[:, 0:1, :], ba[:, 1:2, :], grads["gnw"], grads["ln_sums"][:, 2:3, :],
             grads["ln_sums"][:, 0:1, :], grads["ln_sums"][:, 1:2, :], d_final,
             grads["cw"][:, :SHORT_CONV, :], grads["dw_w"][:, :CONV_WIDTH, :]]
    red = _unpack(_allreduce_small(_pack(small), name="allreduce_small"), [s.shape for s in small])
    g_norm_w = red[0][:, 0, :]
    g_a_log = red[1][:, 0, GDN_HEADS:2 * GDN_HEADS]
    g_dt_bias = red[2][:, 0, GDN_HEADS:2 * GDN_HEADS]
    g_gnw, g_dw_b, g_ln_w, g_ln_b = red[3][:, 0, :], red[4][:, 0, :], red[5][:, 0, :], red[6][:, 0, :]
    g_final = red[7][0]
    g_cw = lax.dynamic_slice_in_dim(red[8], chip * (3 * GDN_W // 4), 3 * GDN_W // 4, axis=2)
    g_dw_w = lax.dynamic_slice_in_dim(red[9], chip * (CONV_CH // 4), CONV_CH // 4, axis=2)

    d_w_in, nm_w_in, nv_w_in = _adamw(w_in, g_w_in, m_w_in, v_w_in, name="adamw_w_in")
    d_w_out, nm_w_out, nv_w_out = _adamw(w_out, g_w_out, m_w_out, v_w_out, name="adamw_w_out")
    d_pw_w, nm_pw_w, nv_pw_w = _adamw(conf_pw_w, g_pw_w, m_conf_pw_w, v_conf_pw_w, name="adamw_pw")
    sw = [norm_w, a_log, dt_bias, gdn_norm_w, conf_dw_b, conf_ln_w, conf_ln_b, final_norm_w, conv_qkv_w, conf_dw_w]
    sg = [g_norm_w, g_a_log, g_dt_bias, g_gnw, g_dw_b, g_ln_w, g_ln_b, g_final, g_cw, g_dw_w]
    sm = [m_norm_w, m_a_log, m_dt_bias, m_gdn_norm_w, m_conf_dw_b, m_conf_ln_w, m_conf_ln_b, m_final_norm_w,
          m_conv_qkv_w, m_conf_dw_w]
    sv = [v_norm_w, v_a_log, v_dt_bias, v_gdn_norm_w, v_conf_dw_b, v_conf_ln_w, v_conf_ln_b, v_final_norm_w,
          v_conv_qkv_w, v_conf_dw_w]
    shapes = [a.shape for a in sw]
    packed = _adamw(_pack(sw)[None], _pack(sg)[None], _pack(sm)[None], _pack(sv)[None], name="adamw_small")
    sd, snm, snv = [_unpack(pk[0], shapes) for pk in packed]

    def order(big3, small10):
        s = small10
        return [s[0], big3[0], s[8], s[1], s[2], s[3], s[9], s[4], s[5], s[6], big3[2], big3[1], s[7]]

    return (loss, grad_x[None], *order([g_w_in, g_w_out, g_pw_w], sg),
            *order([d_w_in, d_w_out, d_pw_w], sd), *order([nm_w_in, nm_w_out, nm_pw_w], snm),
            *order([nv_w_in, nv_w_out, nv_pw_w], snv))
```

```python
import functools
import math

import jax
import jax.numpy as jnp
from jax import lax
from jax.experimental import pallas as pl
from jax.experimental.pallas import tpu as pltpu

F32, BF16 = jnp.float32, jnp.bfloat16
HIGHEST = lax.Precision.HIGHEST
MESH = pl.DeviceIdType.MESH

D_MODEL = 2048
DEPTH = 4
CONV_CH = 512
GDN_W = 768
GDN_HEADS = 6
GDN_D = 128
ATT_W = 768
ATT_HEADS = 12
ATT_HD = 64
CONV_WIDTH = 31
SHORT_CONV = 4
GDN_CHUNK = 64
ROPE_THETA = 500000.0
ROPE_DIM = ATT_HD // 4
DIL_PATTERNS = ((128, 1), (512, 4), (2048, 16))
ATT_BLOCK = 128
NEG_INF = -1e30
IN_W = 7692

IN_WP = 8192
COL_BA = 7680
ORIG_BA = 4608
ORIG_ATT = 4620

ADAM_LR = 0.001
ADAM_B1 = 0.9
ADAM_B2 = 0.999
ADAM_EPS = 1e-08
ADAM_WD = 0.01
ADAM_STEP = 10

VMEM_LIMIT = 56 * 1024 * 1024


def _params(sem=None):
    return pltpu.CompilerParams(dimension_semantics=sem, vmem_limit_bytes=VMEM_LIMIT)


def _sigmoid(x):
    return 1.0 / (1.0 + jnp.exp(-x))


def _silu(x):
    return x * _sigmoid(x)


def _dsilu(x):
    s = _sigmoid(x)
    return s * (1.0 + x * (1.0 - s))


def _dot(a, b, dims, precision=None):
    return lax.dot_general(a, b, (dims, ((), ())), precision=precision, preferred_element_type=F32)


def _nn(a, b, precision=None):
    return _dot(a, b, ((1,), (0,)), precision)


def _nt(a, b, precision=None):
    return _dot(a, b, ((1,), (1,)), precision)


def _tn(a, b, precision=None):
    return _dot(a, b, ((0,), (0,)), precision)


def _matmul(a, b, *, name, ta=False, tb=False, out_dtype=F32, add=None, stack=None, tm=1024, tn=1024, tk=1024):
    if ta:
        k_dim, m_dim = a.shape
    else:
        m_dim, k_dim = a.shape
    n_dim = b.shape[0] if tb else b.shape[1]
    tm, tn, tk = min(tm, m_dim), min(tn, n_dim), min(tk, k_dim)
    assert m_dim % tm == 0 and n_dim % tn == 0 and k_dim % tk == 0, (name, a.shape, b.shape)
    nk = k_dim // tk
    a_spec = pl.BlockSpec((tk, tm), lambda i, j, k: (k, i)) if ta else pl.BlockSpec((tm, tk), lambda i, j, k: (i, k))
    b_spec = pl.BlockSpec((tn, tk), lambda i, j, k: (j, k)) if tb else pl.BlockSpec((tk, tn), lambda i, j, k: (k, j))
    o_spec = pl.BlockSpec((tm, tn), lambda i, j, k: (i, j))
    out_shape = jax.ShapeDtypeStruct((m_dim, n_dim), out_dtype)
    dims = ((0 if ta else 1,), (1 if tb else 0,))
    has_add = add is not None
    ins = [a, b] + ([add] if has_add else [])
    specs = [a_spec, b_spec] + ([o_spec] if has_add else [])
    aliases = {}
    if stack is not None:
        buf, slab, nslabs = stack
        o_spec = pl.BlockSpec((None, tm, tn), lambda i, j, k: (slab, i, j))
        out_shape = jax.ShapeDtypeStruct((nslabs, m_dim, n_dim), out_dtype)
        if buf is not None:
            aliases = {len(ins): 0}
            ins.append(buf)
            specs.append(pl.BlockSpec(memory_space=pl.ANY))
    n_in = len(ins)

    def body(*refs):
        a_ref, b_ref = refs[0], refs[1]
        o_ref = refs[n_in]

        def finish(r):
            if has_add:
                r = r + refs[2][...]
            o_ref[...] = r.astype(out_dtype)

        prod = _dot(a_ref[...].astype(BF16), b_ref[...].astype(BF16), dims)
        if nk == 1:
            finish(prod)
            return
        acc_ref = refs[n_in + 1]
        k = pl.program_id(2)

        @pl.when(k == 0)
        def _():
            acc_ref[...] = prod

        @pl.when(k > 0)
        def _():
            acc_ref[...] += prod

        @pl.when(k == nk - 1)
        def _():
            finish(acc_ref[...])

    return pl.pallas_call(
        body, name=name, grid=(m_dim // tm, n_dim // tn, nk), in_specs=specs, out_specs=o_spec,
        out_shape=out_shape, scratch_shapes=[pltpu.VMEM((tm, tn), F32)] if nk > 1 else [],
        input_output_aliases=aliases,
        compiler_params=_params(("parallel", "parallel", "arbitrary")),
    )(*ins)


def _rms_fwd(x, w, *, name, tm=256):
    s_len, d = x.shape

    def body(x_ref, w_ref, h_ref):
        xv = x_ref[...]
        r = lax.rsqrt(jnp.mean(xv * xv, axis=-1, keepdims=True) + 1e-6)
        h_ref[...] = (xv * r * w_ref[...]).astype(BF16)

    return pl.pallas_call(
        body, name=name, grid=(s_len // tm,),
        in_specs=[pl.BlockSpec((tm, d), lambda i: (i, 0)), pl.BlockSpec((1, d), lambda i: (0, 0))],
        out_specs=pl.BlockSpec((tm, d), lambda i: (i, 0)),
        out_shape=jax.ShapeDtypeStruct((s_len, d), BF16),
        compiler_params=_params(("parallel",)),
    )(x, w)


def _rms_bwd(x, dh, w, dres, *, name, tm=256):
    s_len, d = x.shape
    nsteps = s_len // tm

    def body(x_ref, dh_ref, w_ref, dres_ref, dx_ref, dw_ref, acc_ref):
        i = pl.program_id(0)

        @pl.when(i == 0)
        def _():
            acc_ref[...] = jnp.zeros_like(acc_ref)

        xv = x_ref[...]
        r = lax.rsqrt(jnp.mean(xv * xv, axis=-1, keepdims=True) + 1e-6)
        xn = xv * r
        dy = dh_ref[...]
        dxn = dy * w_ref[...]
        dx_ref[...] = dres_ref[...] + r * (dxn - xn * jnp.mean(dxn * xn, axis=-1, keepdims=True))
        acc_ref[...] += (dy * xn).reshape(tm // 8, 8, d).sum(axis=0)

        @pl.when(i == nsteps - 1)
        def _():
            dw_ref[...] = jnp.sum(acc_ref[...], axis=0, keepdims=True)

    row = pl.BlockSpec((tm, d), lambda i: (i, 0))
    vec = pl.BlockSpec((1, d), lambda i: (0, 0))
    return pl.pallas_call(
        body, name=name, grid=(nsteps,), in_specs=[row, row, vec, row], out_specs=[row, vec],
        out_shape=[jax.ShapeDtypeStruct((s_len, d), F32), jax.ShapeDtypeStruct((1, d), F32)],
        scratch_shapes=[pltpu.VMEM((8, d), F32)],
        compiler_params=_params(("arbitrary",)),
    )(x, dh, w, dres)


def _loss_head(x, w, target, *, name, tm=256):
    s_len, d = x.shape
    nsteps = s_len // tm

    def body(x_ref, w_ref, t_ref, dx_ref, dw_ref, loss_ref, acc_ref, lacc_ref):
        i = pl.program_id(0)

        @pl.when(i == 0)
        def _():
            acc_ref[...] = jnp.zeros_like(acc_ref)
            lacc_ref[...] = jnp.zeros_like(lacc_ref)

        xv = x_ref[...]
        wv = w_ref[...]
        r = lax.rsqrt(jnp.mean(xv * xv, axis=-1, keepdims=True) + 1e-6)
        xn = xv * r
        err = xn * wv - t_ref[...]
        lacc_ref[...] += (err * err).reshape(tm // 8, 8, d).sum(axis=0)
        dy = err * (1.0 / d)
        dxn = dy * wv
        dx_ref[...] = r * (dxn - xn * jnp.mean(dxn * xn, axis=-1, keepdims=True))
        acc_ref[...] += (dy * xn).reshape(tm // 8, 8, d).sum(axis=0)

        @pl.when(i == nsteps - 1)
        def _():
            dw_ref[...] = jnp.sum(acc_ref[...], axis=0, keepdims=True)
            tot = jnp.sum(jnp.sum(lacc_ref[...], axis=0, keepdims=True), axis=1, keepdims=True)
            loss_ref[...] = jnp.broadcast_to(tot * (0.5 / d), (1, 128))

    row = pl.BlockSpec((tm, d), lambda i: (i, 0))
    vec = pl.BlockSpec((1, d), lambda i: (0, 0))
    return pl.pallas_call(
        body, name=name, grid=(nsteps,), in_specs=[row, vec, row],
        out_specs=[row, vec, pl.BlockSpec((1, 128), lambda i: (0, 0))],
        out_shape=[jax.ShapeDtypeStruct((s_len, d), F32), jax.ShapeDtypeStruct((1, d), F32),
                   jax.ShapeDtypeStruct((1, 128), F32)],
        scratch_shapes=[pltpu.VMEM((8, d), F32), pltpu.VMEM((8, d), F32)],
        compiler_params=_params(("arbitrary",)),
    )(x, w, target)


def _rows_block(shape, tr=256):
    lead, rows, cols = shape
    if rows % tr != 0:
        assert rows * cols <= 1 << 20, shape
        tr = rows
    return (lead, rows // tr), pl.BlockSpec((1, tr, cols), lambda a, i: (a, i, 0))


def _adamw(w, g, m, v, *, name):
    grid, spec = _rows_block(w.shape)
    c1 = 1.0 / (1.0 - ADAM_B1 ** ADAM_STEP)
    c2 = 1.0 / (1.0 - ADAM_B2 ** ADAM_STEP)

    def body(w_ref, g_ref, m_ref, v_ref, d_ref, nm_ref, nv_ref):
        gv = g_ref[...]
        nm = ADAM_B1 * m_ref[...] + (1.0 - ADAM_B1) * gv
        nv = ADAM_B2 * v_ref[...] + (1.0 - ADAM_B2) * (gv * gv)
        nm_ref[...] = nm
        nv_ref[...] = nv
        d_ref[...] = -ADAM_LR * ((nm * c1) / (jnp.sqrt(nv * c2) + ADAM_EPS) + ADAM_WD * w_ref[...])

    out = jax.ShapeDtypeStruct(w.shape, F32)
    return pl.pallas_call(
        body, name=name, grid=grid, in_specs=[spec] * 4, out_specs=[spec] * 3, out_shape=[out] * 3,
        compiler_params=_params(("parallel", "parallel")),
    )(w, g, m, v)


def _sum_into_half(arrs, half, *, name):
    lead, rows, cols = arrs[0].shape
    assert lead == 2
    (_, nr), spec0 = _rows_block(arrs[0].shape)
    tr = spec0.block_shape[1]
    n = len(arrs)

    def body(half_ref, *refs):
        del half_ref
        acc = refs[0][...].astype(F32)
        for r in refs[1:n]:
            acc = acc + r[...].astype(F32)
        refs[n][...] = acc

    spec = pl.BlockSpec((1, tr, cols), lambda a, i, h: (a, i, 0))
    return pl.pallas_call(
        body, name=name,
        grid_spec=pltpu.PrefetchScalarGridSpec(
            num_scalar_prefetch=1, grid=(2, nr), in_specs=[spec] * n,
            out_specs=pl.BlockSpec((1, tr, cols), lambda a, i, h: (2 * h[0] + a, i, 0))),
        out_shape=jax.ShapeDtypeStruct((4, rows, cols), F32),
        compiler_params=_params(("parallel", "parallel")),
    )(jnp.reshape(half, (1,)).astype(jnp.int32), *arrs)


def _sum_arrays(arrs, *, name, out_dtype):
    grid, spec = _rows_block(arrs[0].shape)
    n = len(arrs)

    def body(*refs):
        acc = refs[0][...].astype(F32)
        for r in refs[1:n]:
            acc = acc + r[...].astype(F32)
        refs[n][...] = acc.astype(out_dtype)

    return pl.pallas_call(
        body, name=name, grid=grid, in_specs=[spec] * n, out_specs=spec,
        out_shape=jax.ShapeDtypeStruct(arrs[0].shape, out_dtype),
        compiler_params=_params(("parallel", "parallel")),
    )(*arrs)


HALO = 32


def _conf_fwd(u, dw_w, dw_b, ln_w, ln_b, *, name, tm=256):
    s_len = u.shape[0]
    c = CONV_CH

    def body(uc_ref, up_ref, dww_ref, dwb_ref, lnw_ref, lnb_ref, conv_ref, sw_ref, hbuf):
        i = pl.program_id(0)
        hbuf[HALO:, :] = uc_ref[:, :c] * _sigmoid(uc_ref[:, c:])
        hp = up_ref[:, :c] * _sigmoid(up_ref[:, c:])
        hbuf[:HALO, :] = jnp.where(i > 0, hp, 0.0)
        for cb in range(c // 128):
            cs = slice(128 * cb, 128 * (cb + 1))
            acc = jnp.zeros((tm, 128), F32)
            for j in range(CONV_WIDTH):
                acc = acc + hbuf[HALO - CONV_WIDTH + 1 + j:HALO - CONV_WIDTH + 1 + j + tm, cs] * dww_ref[j:j + 1, cs]
            conv_ref[:, cs] = acc + dwb_ref[:, cs]
        cv = conv_ref[...]
        mu = jnp.mean(cv, axis=-1, keepdims=True)
        xc = cv - mu
        var = jnp.mean(xc * xc, axis=-1, keepdims=True)
        ln = xc * lax.rsqrt(var + 1e-5) * lnw_ref[...] + lnb_ref[...]
        sw_ref[...] = _silu(ln).astype(BF16)

    vec = pl.BlockSpec((1, c), lambda i: (0, 0))
    return pl.pallas_call(
        body, name=name, grid=(s_len // tm,),
        in_specs=[pl.BlockSpec((tm, 2 * c), lambda i: (i, 0)),
                  pl.BlockSpec((HALO, 2 * c), lambda i: (jnp.maximum(i * (tm // HALO) - 1, 0), 0)),
                  pl.BlockSpec((HALO, c), lambda i: (0, 0)), vec, vec, vec],
        out_specs=[pl.BlockSpec((tm, c), lambda i: (i, 0))] * 2,
        out_shape=[jax.ShapeDtypeStruct((s_len, c), F32), jax.ShapeDtypeStruct((s_len, c), BF16)],
        scratch_shapes=[pltpu.VMEM((tm + HALO, c), F32)],
        compiler_params=_params(("parallel",)),
    )(u, u, dw_w, dw_b, ln_w, ln_b)


def _conf_bwd_ln(d_sw, conv, ln_w, ln_b, *, name, tm=256):
    s_len, c = conv.shape
    nsteps = s_len // tm

    def body(dsw_ref, conv_ref, lnw_ref, lnb_ref, dconv_ref, sums_ref):
        i = pl.program_id(0)

        @pl.when(i == 0)
        def _():
            sums_ref[...] = jnp.zeros_like(sums_ref)

        cv = conv_ref[...]
        mu = jnp.mean(cv, axis=-1, keepdims=True)
        xc = cv - mu
        rs = lax.rsqrt(jnp.mean(xc * xc, axis=-1, keepdims=True) + 1e-5)
        xhat = xc * rs
        lnw = lnw_ref[...]
        ln = xhat * lnw + lnb_ref[...]
        dln = dsw_ref[...] * _dsilu(ln)
        dxh = dln * lnw
        dconv = rs * (dxh - jnp.mean(dxh, axis=-1, keepdims=True)
                      - xhat * jnp.mean(dxh * xhat, axis=-1, keepdims=True))
        dconv_ref[...] = dconv
        sums_ref[0:1, :] += jnp.sum(dln * xhat, axis=0, keepdims=True)
        sums_ref[1:2, :] += jnp.sum(dln, axis=0, keepdims=True)
        sums_ref[2:3, :] += jnp.sum(dconv, axis=0, keepdims=True)

    row = pl.BlockSpec((tm, c), lambda i: (i, 0))
    vec = pl.BlockSpec((1, c), lambda i: (0, 0))
    return pl.pallas_call(
        body, name=name, grid=(nsteps,), in_specs=[row, row, vec, vec],
        out_specs=[row, pl.BlockSpec((8, c), lambda i: (0, 0))],
        out_shape=[jax.ShapeDtypeStruct((s_len, c), F32), jax.ShapeDtypeStruct((8, c), F32)],
        compiler_params=_params(("arbitrary",)),
    )(d_sw, conv, ln_w, ln_b)


def _conf_bwd_conv(u, dconv, dw_w, du, *, name, tm=256):
    s_len = u.shape[0]
    c = CONV_CH
    nsteps = s_len // tm
    off = HALO - CONV_WIDTH + 1

    def body(uc_ref, up_ref, dc_ref, dn_ref, dww_ref, du_in_ref, du_ref, ddw_ref, hbuf, dbuf, wacc):
        del du_in_ref
        i = pl.program_id(0)

        @pl.when(i == 0)
        def _():
            wacc[...] = jnp.zeros_like(wacc)

        hbuf[HALO:, :] = uc_ref[:, :c] * _sigmoid(uc_ref[:, c:])
        hp = up_ref[:, :c] * _sigmoid(up_ref[:, c:])
        hbuf[:HALO, :] = jnp.where(i > 0, hp, 0.0)
        dbuf[:tm, :] = dc_ref[...]
        dbuf[tm:, :] = jnp.where(i < nsteps - 1, dn_ref[...], 0.0)
        for cb in range(c // 128):
            cs = slice(128 * cb, 128 * (cb + 1))
            dcur = dbuf[0:tm, cs]
            acc = jnp.zeros((tm, 128), F32)
            for j in range(CONV_WIDTH):
                k = CONV_WIDTH - 1 - j
                acc = acc + dbuf[k:k + tm, cs] * dww_ref[j:j + 1, cs]
                prod = hbuf[off + j:off + j + tm, cs] * dcur
                wacc[j, :, cs] += prod.reshape(tm // 8, 8, 128).sum(axis=0)
            a = uc_ref[:, cs]
            sg = _sigmoid(uc_ref[:, c + 128 * cb:c + 128 * (cb + 1)])
            du_ref[:, cs] = acc * sg
            du_ref[:, c + 128 * cb:c + 128 * (cb + 1)] = acc * a * sg * (1.0 - sg)

        @pl.when(i == nsteps - 1)
        def _():
            for j in range(CONV_WIDTH):
                ddw_ref[j:j + 1, :] = jnp.sum(wacc[j], axis=0, keepdims=True)
            ddw_ref[CONV_WIDTH:, :] = jnp.zeros((HALO - CONV_WIDTH, c), F32)

    return pl.pallas_call(
        body, name=name, grid=(nsteps,),
        in_specs=[pl.BlockSpec((tm, 2 * c), lambda i: (i, 0)),
                  pl.BlockSpec((HALO, 2 * c), lambda i: (jnp.maximum(i * (tm // HALO) - 1, 0), 0)),
                  pl.BlockSpec((tm, c), lambda i: (i, 0)),
                  pl.BlockSpec((HALO, c), lambda i: (jnp.minimum((i + 1) * (tm // HALO), s_len // HALO - 1), 0)),
                  pl.BlockSpec((HALO, c), lambda i: (0, 0)),
                  pl.BlockSpec(memory_space=pl.ANY)],
        out_specs=[pl.BlockSpec((tm, 2 * c), lambda i: (i, 0)), pl.BlockSpec((HALO, c), lambda i: (0, 0))],
        out_shape=[jax.ShapeDtypeStruct(du.shape, F32), jax.ShapeDtypeStruct((HALO, c), F32)],
        scratch_shapes=[pltpu.VMEM((tm + HALO, c), F32), pltpu.VMEM((tm + HALO, c), F32),
                        pltpu.VMEM((CONV_WIDTH, 8, c), F32)],
        input_output_aliases={5: 0},
        compiler_params=_params(("arbitrary",)),
    )(u, u, dconv, dconv, dw_w, du)


COL_GQ = 1536 // GDN_W
COL_AQ = 4608 // ATT_W
SHALO = 8


def _softplus(z):
    return jnp.maximum(z, 0.0) + jnp.log1p(jnp.exp(-jnp.abs(z)))


def _short_conv(buf, cw_ref, part, rows, first):
    acc = jnp.zeros((rows, GDN_W), F32)
    for j in range(SHORT_CONV):
        acc = acc + buf[first + j:first + j + rows, :] * cw_ref[j:j + 1, GDN_W * part:GDN_W * (part + 1)]
    return acc


def _gdn_prep_fwd(u, cw, al, dtb, *, name, tm=256):
    s_len = u.shape[0]
    first = SHALO - SHORT_CONV + 1

    def body(uq, uk, uv, pq, pk, pv, uba, cw_ref, al_ref, dtb_ref, qn_ref, kn_ref, vc_ref, bg_ref, buf):
        i = pl.program_id(0)

        def conv(cur, prev, part):
            buf[SHALO:, :] = cur[...]
            buf[:SHALO, :] = jnp.where(i > 0, prev[...], 0.0)
            return _silu(_short_conv(buf, cw_ref, part, tm, first))

        for part, (cur, prev, out, scale) in enumerate(
                ((uq, pq, qn_ref, GDN_D ** -0.5), (uk, pk, kn_ref, 1.0))):
            y = conv(cur, prev, part)
            for h in range(GDN_HEADS):
                hs = slice(GDN_D * h, GDN_D * (h + 1))
                yh = y[:, hs]
                out[:, hs] = yh * (lax.rsqrt(jnp.sum(yh * yh, axis=-1, keepdims=True) + 1e-6) * scale)
        vc_ref[...] = conv(uv, pv, 2)
        ba = uba[...]
        lane = lax.broadcasted_iota(jnp.int32, ba.shape, 1)
        g = -jnp.exp(al_ref[...]) * _softplus(ba + dtb_ref[...])
        bg_ref[...] = jnp.where(lane < GDN_HEADS, _sigmoid(ba), jnp.where(lane < 2 * GDN_HEADS, g, 0.0))

    def cur(col):
        return pl.BlockSpec((tm, GDN_W), lambda i: (i, col))

    def prev(col):
        return pl.BlockSpec((SHALO, GDN_W), lambda i: (jnp.maximum(i * (tm // SHALO) - 1, 0), col))

    vec = pl.BlockSpec((1, 128), lambda i: (0, 0))
    row = pl.BlockSpec((tm, GDN_W), lambda i: (i, 0))
    wide = jax.ShapeDtypeStruct((s_len, GDN_W), F32)
    return pl.pallas_call(
        body, name=name, grid=(s_len // tm,),
        in_specs=[cur(COL_GQ), cur(COL_GQ + 1), cur(COL_GQ + 2), prev(COL_GQ), prev(COL_GQ + 1), prev(COL_GQ + 2),
                  pl.BlockSpec((tm, 128), lambda i: (i, COL_BA // 128)),
                  pl.BlockSpec((SHALO, 3 * GDN_W), lambda i: (0, 0)), vec, vec],
        out_specs=[row, row, row, pl.BlockSpec((tm, 128), lambda i: (i, 0))],
        out_shape=[wide, wide, wide, jax.ShapeDtypeStruct((s_len, 128), F32)],
        scratch_shapes=[pltpu.VMEM((tm + SHALO, GDN_W), F32)],
        compiler_params=_params(("parallel",)),
    )(u, u, u, u, u, u, u, cw, al, dtb)


def _chunk_masks():
    c = GDN_CHUNK
    row = lax.broadcasted_iota(jnp.int32, (c, c), 0)
    col = lax.broadcasted_iota(jnp.int32, (c, c), 1)
    return row >= col, row > col


def _cum_decay(bg):
    c = GDN_CHUNK
    causal, _ = _chunk_masks()
    g_cum = _nn(causal.astype(F32), bg, HIGHEST)
    sel = (lax.broadcasted_iota(jnp.int32, (8, 128), 0) + GDN_HEADS
           == lax.broadcasted_iota(jnp.int32, (8, 128), 1)).astype(F32)
    return g_cum, _nt(sel, g_cum, HIGHEST)


def _bdot(a, b, ca, cb):
    return lax.dot_general(a, b, (((ca,), (cb,)), ((0,), (0,))), preferred_element_type=F32)


def _bnn(a, b):
    return _bdot(a, b, 2, 1)


def _bnt(a, b):
    return _bdot(a, b, 2, 2)


def _btn(a, b):
    return _bdot(a, b, 1, 1)


def _split(a):
    hi = a.astype(BF16)
    return hi, (a - hi.astype(F32)).astype(BF16)


def _bnn3(a, b):
    ah, al = _split(a)
    bh, bl = _split(b)
    return _bnn(ah, bh) + (_bnn(al, bh) + _bnn(ah, bl))


def _heads(ref):
    return jnp.stack([ref[:, GDN_D * h:GDN_D * (h + 1)] for h in range(GDN_HEADS)])


def _head_columns(a, first):
    return jnp.stack([a[:, first + h:first + h + 1] for h in range(GDN_HEADS)])


def _chunk_decay(g_cum, g_rows, bg):
    causal, _ = _chunk_masks()
    gc = _head_columns(g_cum, GDN_HEADS)
    gr = jnp.stack([g_rows[h:h + 1, :] for h in range(GDN_HEADS)])
    dec = jnp.where(causal, jnp.exp(jnp.where(causal, gc - gr, 0.0)), 0.0)
    return gc, _head_columns(bg, 0), dec


def _gdn_intra_fwd(qn, kn, vc, bg, *, name):
    s_len = qn.shape[0]
    c = GDN_CHUNK
    nchunks = s_len // c

    def body(q_ref, k_ref, v_ref, bg_ref, wk_ref, wv_ref, qd_ref, kd_ref, p_ref, t_ref, g_ref):
        causal, strict = _chunk_masks()
        eye = (lax.broadcasted_iota(jnp.int32, (c, c), 0) == lax.broadcasted_iota(jnp.int32, (c, c), 1)).astype(F32)
        bg = bg_ref[...]
        g_cum, g_rows = _cum_decay(bg)
        g_ref[...] = g_cum
        gc, bc, dec = _chunk_decay(g_cum, g_rows, bg)
        q, k, v = _heads(q_ref), _heads(k_ref), _heads(v_ref)
        k16 = k.astype(BF16)
        low = jnp.where(strict, bc * _bnt(k16, k16) * dec, 0.0)
        pw = -low
        t = eye + pw
        for _ in range(5):
            pw = _bnn3(pw, pw)
            t = t + _bnn3(t, pw)
        t_ref[...] = t
        t16 = t.astype(BF16)
        eg = jnp.exp(gc)
        wk = _bnn(t16, (k * (bc * eg)).astype(BF16))
        wv = _bnn(t16, (v * bc).astype(BF16))
        p_ref[...] = jnp.where(causal, _bnt(q.astype(BF16), k16) * dec, 0.0).astype(BF16)
        qd = q * eg
        kd = k * jnp.exp(gc[:, c - 1:c, :] - gc)
        for h in range(GDN_HEADS):
            hs = slice(GDN_D * h, GDN_D * (h + 1))
            wk_ref[:, hs] = wk[h].astype(BF16)
            wv_ref[:, hs] = wv[h]
            qd_ref[:, hs] = qd[h].astype(BF16)
            kd_ref[:, hs] = kd[h].astype(BF16)

    row = pl.BlockSpec((c, GDN_W), lambda n: (n, 0))
    sq = pl.BlockSpec((GDN_HEADS, c, c), lambda n: (0, n, 0))
    narrow = pl.BlockSpec((c, 128), lambda n: (n, 0))
    w16 = jax.ShapeDtypeStruct((s_len, GDN_W), BF16)
    return pl.pallas_call(
        body, name=name, grid=(nchunks,), in_specs=[row, row, row, narrow],
        out_specs=[row, row, row, row, sq, sq, narrow],
        out_shape=[w16, jax.ShapeDtypeStruct((s_len, GDN_W), F32), w16, w16,
                   jax.ShapeDtypeStruct((GDN_HEADS, s_len, c), BF16),
                   jax.ShapeDtypeStruct((GDN_HEADS, s_len, c), F32),
                   jax.ShapeDtypeStruct((s_len, 128), F32)],
        compiler_params=_params(("parallel",)),
    )(qn, kn, vc, bg)


def _gdn_scan_fwd(wk, wv, qd, kd, p, g_cum, *, name):
    s_len = wk.shape[0]
    c = GDN_CHUNK
    nchunks = s_len // c

    def body(wk_ref, wv_ref, qd_ref, kd_ref, p_ref, g_ref, o_ref, vn_ref, sp_ref, st):
        @pl.when(pl.program_id(0) == 0)
        def _():
            st[...] = jnp.zeros_like(st)

        for h in range(GDN_HEADS):
            hs = slice(GDN_D * h, GDN_D * (h + 1))
            s = st[h]
            sp_ref[0, h] = s
            s16 = s.astype(BF16)
            vn16 = (wv_ref[:, hs] - _nn(wk_ref[:, hs], s16)).astype(BF16)
            vn_ref[:, hs] = vn16
            o_ref[:, hs] = _nn(qd_ref[:, hs], s16) + _nn(p_ref[h], vn16)
            gl = jnp.exp(g_ref[c - 1:c, GDN_HEADS + h:GDN_HEADS + h + 1])
            st[h] = s * gl + _tn(kd_ref[:, hs], vn16)

    row = pl.BlockSpec((c, GDN_W), lambda n: (n, 0))
    return pl.pallas_call(
        body, name=name, grid=(nchunks,),
        in_specs=[row, row, row, row, pl.BlockSpec((GDN_HEADS, c, c), lambda n: (0, n, 0)),
                  pl.BlockSpec((c, 128), lambda n: (n, 0))],
        out_specs=[row, row, pl.BlockSpec((1, GDN_HEADS, GDN_D, GDN_D), lambda n: (n, 0, 0, 0))],
        out_shape=[jax.ShapeDtypeStruct((s_len, GDN_W), F32), jax.ShapeDtypeStruct((s_len, GDN_W), BF16),
                   jax.ShapeDtypeStruct((nchunks, GDN_HEADS, GDN_D, GDN_D), F32)],
        scratch_shapes=[pltpu.VMEM((GDN_HEADS, GDN_D, GDN_D), F32)],
        compiler_params=_params(("arbitrary",)),
    )(wk, wv, qd, kd, p, g_cum)


def _gdn_scan_bwd(do, wk, qd, kd, p, g_cum, *, name):
    s_len = wk.shape[0]
    c = GDN_CHUNK
    nchunks = s_len // c

    def body(do_ref, wk_ref, qd_ref, kd_ref, p_ref, g_ref, dvn_ref, ds_ref, dst):
        @pl.when(pl.program_id(0) == 0)
        def _():
            dst[...] = jnp.zeros_like(dst)

        for h in range(GDN_HEADS):
            hs = slice(GDN_D * h, GDN_D * (h + 1))
            ds = dst[h]
            ds_ref[0, h] = ds
            do16 = do_ref[:, hs].astype(BF16)
            dvn16 = (_tn(p_ref[h], do16) + _nn(kd_ref[:, hs], ds.astype(BF16))).astype(BF16)
            dvn_ref[:, hs] = dvn16
            gl = jnp.exp(g_ref[c - 1:c, GDN_HEADS + h:GDN_HEADS + h + 1])
            dst[h] = _tn(qd_ref[:, hs], do16) + ds * gl - _tn(wk_ref[:, hs], dvn16)

    row = pl.BlockSpec((c, GDN_W), lambda n: (nchunks - 1 - n, 0))
    return pl.pallas_call(
        body, name=name, grid=(nchunks,),
        in_specs=[row, row, row, row, pl.BlockSpec((GDN_HEADS, c, c), lambda n: (0, nchunks - 1 - n, 0)),
                  pl.BlockSpec((c, 128), lambda n: (nchunks - 1 - n, 0))],
        out_specs=[row, pl.BlockSpec((1, GDN_HEADS, GDN_D, GDN_D), lambda n: (nchunks - 1 - n, 0, 0, 0))],
        out_shape=[jax.ShapeDtypeStruct((s_len, GDN_W), BF16),
                   jax.ShapeDtypeStruct((nchunks, GDN_HEADS, GDN_D, GDN_D), F32)],
        scratch_shapes=[pltpu.VMEM((GDN_HEADS, GDN_D, GDN_D), F32)],
        compiler_params=_params(("arbitrary",)),
    )(do, wk, qd, kd, p, g_cum)


def _gdn_intra_bwd(qn, kn, vc, bg, g_cum, t, do, dvn, vn, sprev, ds_all, *, name):
    s_len = qn.shape[0]
    c = GDN_CHUNK
    nchunks = s_len // c

    def body(q_ref, k_ref, v_ref, bg_ref, g_ref, t_ref, do_ref, dvn_ref, vn_ref, sp_ref, ds_ref,
             dqkv_ref, dbg_ref):
        causal, strict = _chunk_masks()
        bg = bg_ref[...]
        g_cum = g_ref[...]
        _, g_rows = _cum_decay(bg)
        lane = lax.broadcasted_iota(jnp.int32, (c, 128), 1)
        rowi = lax.broadcasted_iota(jnp.int32, (c, 128), 0)
        gc, bc, dec = _chunk_decay(g_cum, g_rows, bg)
        q, k, v = _heads(q_ref), _heads(k_ref), _heads(v_ref)
        q16, k16 = q.astype(BF16), k.astype(BF16)
        kk = _bnt(k16, k16)
        low = jnp.where(strict, bc * kk * dec, 0.0)
        eg = jnp.exp(gc)
        g_last = gc[:, c - 1:c, :]
        kdec = jnp.exp(g_last - gc)
        kb, vb, qd, kd = k * (bc * eg), v * bc, q * eg, k * kdec
        pm = jnp.where(causal, _bnt(q16, k16) * dec, 0.0)
        s = sp_ref[0]
        ds = ds_ref[0]
        s16, ds16 = s.astype(BF16), ds.astype(BF16)
        do16 = _heads(do_ref).astype(BF16)
        dvn16, vn16 = _heads(dvn_ref), _heads(vn_ref)
        tm = t_ref[...]
        t16 = tm.astype(BF16)

        dqd = _bnt(do16, s16)
        dp = jnp.where(causal, _bnt(do16, vn16), 0.0)
        dkd = _bnt(vn16, ds16)
        dgl = jnp.sum(jnp.sum(s * ds, axis=2, keepdims=True), axis=1, keepdims=True) * jnp.exp(g_last)
        dwk16 = (-_bnt(dvn16, s16)).astype(BF16)
        dt = _bnt(dwk16, kb.astype(BF16)) + _bnt(dvn16, vb.astype(BF16))
        dkb = _btn(t16, dwk16)
        dvb = _btn(t16, dvn16)
        th, tl = _split(tm)
        dth, dtl = _split(dt)
        xm = _btn(th, dth) + (_btn(tl, dth) + _btn(th, dtl))
        xh, xl = _split(xm)
        dlow = jnp.where(strict, -(_bnt(xh, th) + (_bnt(xl, th) + _bnt(xh, tl))), 0.0)
        dkk16 = (dlow * bc * dec).astype(BF16)
        dqk16 = (dp * dec).astype(BF16)

        dq = _bnn(dqk16, k16) + dqd * eg
        dk = _btn(dqk16, q16) + _bnn(dkk16, k16) + _btn(dkk16, k16) + dkb * (bc * eg) + dkd * kdec
        dv = dvb * bc
        for h in range(GDN_HEADS):
            hs = slice(GDN_D * h, GDN_D * (h + 1))
            dqkv_ref[0, :, hs] = dq[h]
            dqkv_ref[1, :, hs] = dk[h]
            dqkv_ref[2, :, hs] = dv[h]

        dbeta = (jnp.sum(dlow * kk * dec, axis=2, keepdims=True)
                 + jnp.sum(dkb * k, axis=2, keepdims=True) * eg + jnp.sum(dvb * v, axis=2, keepdims=True))
        mm = dlow * low + dp * pm
        mh, ml = _split(mm)
        ones16 = jnp.ones((GDN_HEADS, c, 128), BF16)
        col_sum = (_btn(mh, ones16) + _btn(ml, ones16))[:, :, 0:1]
        dkd_sum = jnp.sum(dkd * kd, axis=2, keepdims=True)
        dg = (jnp.sum(mm, axis=2, keepdims=True) - col_sum + jnp.sum(dkb * kb, axis=2, keepdims=True)
              + jnp.sum(dqd * qd, axis=2, keepdims=True) - dkd_sum)
        tail = jnp.sum(dkd_sum, axis=1, keepdims=True) + dgl
        dbeta_all = jnp.zeros((c, 128), F32)
        dg_all = jnp.zeros((c, 128), F32)
        for h in range(GDN_HEADS):
            dbeta_all = dbeta_all + jnp.where(lane == h, dbeta[h], 0.0)
            dg_all = dg_all + jnp.where(lane == GDN_HEADS + h, dg[h] + jnp.where(rowi == c - 1, tail[h], 0.0), 0.0)
        upper = (lax.broadcasted_iota(jnp.int32, (c, c), 0) <= lax.broadcasted_iota(jnp.int32, (c, c), 1)).astype(F32)
        dbg_ref[...] = dbeta_all + _nn(upper, dg_all, HIGHEST)

    row = pl.BlockSpec((c, GDN_W), lambda n: (n, 0))
    narrow = pl.BlockSpec((c, 128), lambda n: (n, 0))
    state = pl.BlockSpec((1, GDN_HEADS, GDN_D, GDN_D), lambda n: (n, 0, 0, 0))
    return pl.pallas_call(
        body, name=name, grid=(nchunks,),
        in_specs=[row, row, row, narrow, narrow, pl.BlockSpec((GDN_HEADS, c, c), lambda n: (0, n, 0)),
                  row, row, row, state, state],
        out_specs=[pl.BlockSpec((3, c, GDN_W), lambda n: (0, n, 0)), narrow],
        out_shape=[jax.ShapeDtypeStruct((3, s_len, GDN_W), F32), jax.ShapeDtypeStruct((s_len, 128), F32)],
        compiler_params=_params(("parallel",)),
    )(qn, kn, vc, bg, g_cum, t, do, dvn, vn, sprev, ds_all)


def _gdn_prep_bwd(u, dqkv, cw, du, *, name, tm=256):
    s_len = u.shape[0]
    nsteps = s_len // tm
    ext = tm + SHALO

    def body(uc, up, un, dc, dn, cw_ref, du_in_ref, du_ref, dcw_ref, xbuf, dbuf, pbuf, wacc):
        del du_in_ref
        part = pl.program_id(0)
        i = pl.program_id(1)

        @pl.when(i == 0)
        def _():
            wacc[...] = jnp.zeros_like(wacc)

        xbuf[:SHALO, :] = jnp.where(i > 0, up[...], 0.0)
        xbuf[SHALO:SHALO + tm, :] = uc[...]
        xbuf[SHALO + tm:, :] = jnp.where(i < nsteps - 1, un[...], 0.0)
        dbuf[:tm, :] = dc[...]
        dbuf[tm:, :] = jnp.where(i < nsteps - 1, dn[...], 0.0)
        first = SHALO - SHORT_CONV + 1
        w = [cw_ref[j:j + 1, :] for j in range(SHORT_CONV)]
        pre = jnp.zeros((ext, GDN_W), F32)
        for j in range(SHORT_CONV):
            pre = pre + xbuf[first + j:first + j + ext, :] * w[j]
        y = _silu(pre)
        dout = dbuf[...]
        scale = jnp.where(part == 0, GDN_D ** -0.5, 1.0)
        for h in range(GDN_HEADS):
            hs = slice(GDN_D * h, GDN_D * (h + 1))
            yh, dh = y[:, hs], dout[:, hs]
            rs = lax.rsqrt(jnp.sum(yh * yh, axis=-1, keepdims=True) + 1e-6)
            dyn = scale * rs * (dh - yh * (rs * rs) * jnp.sum(dh * yh, axis=-1, keepdims=True))
            dy = jnp.where(part < 2, dyn, dh)
            pbuf[:, hs] = dy * _dsilu(pre[:, hs])
        acc = jnp.zeros((tm, GDN_W), F32)
        dpre = pbuf[0:tm, :]
        for j in range(SHORT_CONV):
            k = SHORT_CONV - 1 - j
            acc = acc + pbuf[k:k + tm, :] * w[j]
            wacc[j] += (xbuf[first + j:first + j + tm, :] * dpre).reshape(tm // 8, 8, GDN_W).sum(axis=0)
        du_ref[...] = acc

        @pl.when(i == nsteps - 1)
        def _():
            for j in range(SHORT_CONV):
                dcw_ref[j:j + 1, :] = jnp.sum(wacc[j], axis=0, keepdims=True)
            dcw_ref[SHORT_CONV:, :] = jnp.zeros((SHALO - SHORT_CONV, GDN_W), F32)

    per = tm // SHALO
    return pl.pallas_call(
        body, name=name, grid=(3, nsteps),
        in_specs=[pl.BlockSpec((tm, GDN_W), lambda p, i: (i, COL_GQ + p)),
                  pl.BlockSpec((SHALO, GDN_W), lambda p, i: (jnp.maximum(i * per - 1, 0), COL_GQ + p)),
                  pl.BlockSpec((SHALO, GDN_W), lambda p, i: (jnp.minimum((i + 1) * per, s_len // SHALO - 1), COL_GQ + p)),
                  pl.BlockSpec((None, tm, GDN_W), lambda p, i: (p, i, 0)),
                  pl.BlockSpec((None, SHALO, GDN_W), lambda p, i: (p, jnp.minimum((i + 1) * per, s_len // SHALO - 1), 0)),
                  pl.BlockSpec((SHALO, GDN_W), lambda p, i: (0, p)),
                  pl.BlockSpec(memory_space=pl.ANY)],
        out_specs=[pl.BlockSpec((tm, GDN_W), lambda p, i: (i, COL_GQ + p)),
                   pl.BlockSpec((SHALO, GDN_W), lambda p, i: (0, p))],
        out_shape=[jax.ShapeDtypeStruct(du.shape, F32), jax.ShapeDtypeStruct((SHALO, 3 * GDN_W), F32)],
        scratch_shapes=[pltpu.VMEM((tm + 2 * SHALO, GDN_W), F32), pltpu.VMEM((ext, GDN_W), F32),
                        pltpu.VMEM((ext, GDN_W), F32), pltpu.VMEM((SHORT_CONV, 8, GDN_W), F32)],
        input_output_aliases={6: 0},
        compiler_params=_params(("arbitrary", "arbitrary")),
    )(u, u, u, dqkv, dqkv, cw, du)


def _gdn_ba_bwd(u, dbg, al, dtb, du, *, name, tm=256):
    s_len = u.shape[0]
    nsteps = s_len // tm
    wpad = IN_WP - COL_BA

    def body(uba, dbg_ref, al_ref, dtb_ref, du_in_ref, du_ref, sums_ref):
        del du_in_ref
        i = pl.program_id(0)

        @pl.when(i == 0)
        def _():
            sums_ref[...] = jnp.zeros_like(sums_ref)

        ba = uba[...]
        dbg = dbg_ref[...]
        lane = lax.broadcasted_iota(jnp.int32, ba.shape, 1)
        is_g = (lane >= GDN_HEADS) & (lane < 2 * GDN_HEADS)
        beta = _sigmoid(ba)
        z = ba + dtb_ref[...]
        ea = jnp.exp(al_ref[...])
        g = -ea * _softplus(z)
        dz = jnp.where(is_g, dbg * (-ea) * _sigmoid(z), 0.0)
        du_ref[:, :128] = jnp.where(lane < GDN_HEADS, dbg * beta * (1.0 - beta), dz)
        du_ref[:, 128:] = jnp.zeros((tm, wpad - 128), F32)
        sums_ref[0:1, :] += jnp.sum(jnp.where(is_g, dbg * g, 0.0), axis=0, keepdims=True)
        sums_ref[1:2, :] += jnp.sum(dz, axis=0, keepdims=True)

    vec = pl.BlockSpec((1, 128), lambda i: (0, 0))
    return pl.pallas_call(
        body, name=name, grid=(nsteps,),
        in_specs=[pl.BlockSpec((tm, 128), lambda i: (i, COL_BA // 128)), pl.BlockSpec((tm, 128), lambda i: (i, 0)),
                  vec, vec, pl.BlockSpec(memory_space=pl.ANY)],
        out_specs=[pl.BlockSpec((tm, wpad), lambda i: (i, COL_BA // wpad)), pl.BlockSpec((8, 128), lambda i: (0, 0))],
        out_shape=[jax.ShapeDtypeStruct(du.shape, F32), jax.ShapeDtypeStruct((8, 128), F32)],
        input_output_aliases={4: 0},
        compiler_params=_params(("arbitrary",)),
    )(u, dbg, al, dtb, du)


def _rope_tables(s_len):
    half = ROPE_DIM // 2
    inv = ROPE_THETA ** (-jnp.arange(half, dtype=F32) / half)
    ang = jnp.arange(s_len, dtype=F32)[:, None] * inv[None, :]
    cos, sin = jnp.cos(ang), jnp.sin(ang)
    one = jnp.ones((s_len, ATT_HD - ROPE_DIM), F32)
    zero = jnp.zeros((s_len, ATT_HD - ROPE_DIM), F32)
    zh = jnp.zeros((s_len, half), F32)
    c = jnp.concatenate([cos, cos, one], axis=1)
    s1 = jnp.concatenate([-sin, zh, zero], axis=1)
    s2 = jnp.concatenate([zh, sin, zero], axis=1)
    return tuple(jnp.concatenate([t, t], axis=1) for t in (c, s1, s2))


def _rope(x, c, s1, s2):
    return x * c + pltpu.roll(x, 128 - ROPE_DIM // 2, 1) * s1 + pltpu.roll(x, ROPE_DIM // 2, 1) * s2


def _rope_t(dy, c, s1, s2):
    return dy * c + pltpu.roll(dy * s1, ROPE_DIM // 2, 1) + pltpu.roll(dy * s2, 128 - ROPE_DIM // 2, 1)


DILATIONS = tuple(d for _, d in DIL_PATTERNS)
VIEW_ROWS = 256


def _to_view(scr, out_ref, dil, dtype):
    nblk, rows, _ = scr.shape
    width = nblk * 128
    for b in range(nblk):
        if dil == 1:
            out_ref[:, 128 * b:128 * (b + 1)] = scr[b].astype(dtype)
            continue
        for r in range(dil):
            out_ref[:, r * width + 128 * b:r * width + 128 * (b + 1)] = (
                scr.at[b][pl.ds(r, rows // dil, stride=dil), :].astype(dtype))


def _from_view(in_ref, scr, dil):
    nblk, rows, _ = scr.shape
    width = nblk * 128
    for b in range(nblk):
        for r in range(dil):
            scr.at[b][pl.ds(r, rows // dil, stride=dil), :] = in_ref[:, r * width + 128 * b:r * width + 128 * (b + 1)]


def _view_spec(dil, width, tm=VIEW_ROWS):
    return pl.BlockSpec((tm // dil, dil * width), lambda i: (i, 0))


def _view_shape(s_len, dil, width, dtype):
    return jax.ShapeDtypeStruct((s_len // dil, dil * width), dtype)


def _att_prep_fwd(u, tabs, *, name):
    s_len = u.shape[0]
    tm = VIEW_ROWS
    scale = ATT_HD ** -0.5
    nblk = ATT_W // 128

    def body(uq, uk, uv, c_ref, s1_ref, s2_ref, *rest):
        outs, scr = rest[:-1], rest[-1]
        c, s1, s2 = c_ref[...], s1_ref[...], s2_ref[...]
        for part, src in enumerate((uq, uk, uv)):
            for b in range(nblk):
                xb = src[:, 128 * b:128 * (b + 1)]
                if part == 0:
                    xb = _rope(xb, c, s1, s2) * scale
                elif part == 1:
                    xb = _rope(xb, c, s1, s2)
                scr[b] = xb
            for gi, dil in enumerate(DILATIONS):
                _to_view(scr, outs[3 * gi + part], dil, BF16)

    tab = pl.BlockSpec((tm, 128), lambda i: (i, 0))
    outs = pl.pallas_call(
        body, name=name, grid=(s_len // tm,),
        in_specs=[pl.BlockSpec((tm, ATT_W), lambda i, col=COL_AQ + j: (i, col)) for j in range(3)] + [tab] * 3,
        out_specs=[_view_spec(dil, ATT_W) for dil in DILATIONS for _ in range(3)],
        out_shape=[_view_shape(s_len, dil, ATT_W, BF16) for dil in DILATIONS for _ in range(3)],
        scratch_shapes=[pltpu.VMEM((nblk, tm, 128), F32)],
        compiler_params=_params(("parallel",)),
    )(u, u, u, *tabs)
    return [outs[3 * gi:3 * gi + 3] for gi in range(len(DILATIONS))]


def _head_lanes(h):
    lane = lax.broadcasted_iota(jnp.int32, (1, 128), 1)
    return (lane < ATT_HD) if h % 2 == 0 else (lane >= ATT_HD)


def _att_fwd(qr, kr, vb, dil, *, name):
    lr = qr.shape[0]
    nb = lr // ATT_BLOCK
    blk = ATT_BLOCK

    def body(q_ref, kp_ref, kc_ref, vp_ref, vc_ref, o_ref, lse_ref):
        n = pl.program_id(1)
        qi = lax.broadcasted_iota(jnp.int32, (blk, 2 * blk), 0)
        ki = lax.broadcasted_iota(jnp.int32, (blk, 2 * blk), 1)
        dist = qi + blk - ki
        valid = (dist >= 0) & (dist <= blk) & ((ki >= blk) | (n > 0))
        lane = lax.broadcasted_iota(jnp.int32, (blk, 128), 1)
        lse_all = jnp.zeros((blk, 128), F32)
        for hp in range(ATT_HEADS // 2):
            bs = slice(128 * hp, 128 * (hp + 1))
            qb = q_ref[:, bs]
            kb = jnp.concatenate([kp_ref[:, bs], kc_ref[:, bs]], axis=0)
            vv = jnp.concatenate([vp_ref[:, bs], vc_ref[:, bs]], axis=0)
            outs = []
            for sub in range(2):
                h = 2 * hp + sub
                qm = jnp.where(_head_lanes(h), qb, jnp.zeros_like(qb))
                s = jnp.where(valid, _nt(qm, kb), NEG_INF)
                m = jnp.max(s, axis=-1, keepdims=True)
                p = jnp.exp(s - m)
                l = jnp.sum(p, axis=-1, keepdims=True)
                outs.append(_nn((p * (1.0 / l)).astype(BF16), vv))
                lse_all = lse_all + jnp.where(lane == h, m + jnp.log(l), 0.0)
            o_ref[:, bs] = jnp.where(lane < ATT_HD, outs[0], outs[1])
        lse_ref[...] = lse_all

    cur = pl.BlockSpec((blk, ATT_W), lambda r, n: (n, r))
    prev = pl.BlockSpec((blk, ATT_W), lambda r, n: (jnp.maximum(n - 1, 0), r))
    return pl.pallas_call(
        body, name=name, grid=(dil, nb), in_specs=[cur, prev, cur, prev, cur],
        out_specs=[cur, pl.BlockSpec((blk, 128), lambda r, n: (n, r))],
        out_shape=[jax.ShapeDtypeStruct(qr.shape, F32), jax.ShapeDtypeStruct((lr, dil * 128), F32)],
        compiler_params=_params(("parallel", "parallel")),
    )(qr, kr, kr, vb, vb)


def _att_bwd(qr, kr, vb, do, lse, delta, dil, *, name):
    lr = qr.shape[0]
    nb = lr // ATT_BLOCK
    blk = ATT_BLOCK

    def body(q_ref, kp_ref, kc_ref, vp_ref, vc_ref, do_ref, lse_ref, dl_ref, dq_ref, dk_ref, dv_ref, carry):
        n = pl.program_id(1)

        @pl.when(n == 0)
        def _():
            carry[...] = jnp.zeros_like(carry)

        @pl.when(n == nb)
        def _():
            dk_ref[...] = carry[0]
            dv_ref[...] = carry[1]

        @pl.when(n < nb)
        def _():
            qi = lax.broadcasted_iota(jnp.int32, (blk, 2 * blk), 0)
            ki = lax.broadcasted_iota(jnp.int32, (blk, 2 * blk), 1)
            dist = qi + blk - ki
            valid = (dist >= 0) & (dist <= blk) & ((ki >= blk) | (n > 0))
            lane = lax.broadcasted_iota(jnp.int32, (blk, 128), 1)
            for hp in range(ATT_HEADS // 2):
                bs = slice(128 * hp, 128 * (hp + 1))
                qb = q_ref[:, bs]
                dob = do_ref[:, bs]
                kb = jnp.concatenate([kp_ref[:, bs], kc_ref[:, bs]], axis=0)
                vv = jnp.concatenate([vp_ref[:, bs], vc_ref[:, bs]], axis=0)
                dqs = []
                dk_acc = jnp.zeros((2 * blk, 128), F32)
                dv_acc = jnp.zeros((2 * blk, 128), F32)
                for sub in range(2):
                    h = 2 * hp + sub
                    hm = _head_lanes(h)
                    qm = jnp.where(hm, qb, jnp.zeros_like(qb))
                    dom = jnp.where(hm, dob, jnp.zeros_like(dob))
                    p = jnp.where(valid, jnp.exp(_nt(qm, kb) - lse_ref[:, h:h + 1]), 0.0)
                    ds16 = (p * (_nt(dom, vv) - dl_ref[:, h:h + 1])).astype(BF16)
                    dqs.append(_nn(ds16, kb))
                    dv_acc = dv_acc + _tn(p.astype(BF16), dom)
                    dk_acc = dk_acc + _tn(ds16, qm)
                dq_ref[:, bs] = jnp.where(lane < ATT_HD, dqs[0], dqs[1])
                dk_ref[:, bs] = carry[0, :, bs] + dk_acc[:blk]
                dv_ref[:, bs] = carry[1, :, bs] + dv_acc[:blk]
                carry[0, :, bs] = dk_acc[blk:]
                carry[1, :, bs] = dv_acc[blk:]

    def at(n):
        return jnp.minimum(n, nb - 1)

    cur = pl.BlockSpec((blk, ATT_W), lambda r, n: (at(n), r))
    prev = pl.BlockSpec((blk, ATT_W), lambda r, n: (jnp.maximum(at(n) - 1, 0), r))
    nar = pl.BlockSpec((blk, 128), lambda r, n: (at(n), r))
    late = pl.BlockSpec((blk, ATT_W), lambda r, n: (jnp.maximum(n - 1, 0), r))
    out = jax.ShapeDtypeStruct(qr.shape, F32)
    return pl.pallas_call(
        body, name=name, grid=(dil, nb + 1), in_specs=[cur, prev, cur, prev, cur, cur, nar, nar],
        out_specs=[cur, late, late], out_shape=[out, out, out],
        scratch_shapes=[pltpu.VMEM((2, blk, ATT_W), F32)],
        compiler_params=_params(("parallel", "arbitrary")),
    )(qr, kr, kr, vb, vb, do, lse, delta)


def _att_prep_bwd(dgroups, tabs, du, *, name):
    s_len = du.shape[0]
    tm = VIEW_ROWS
    scale = ATT_HD ** -0.5
    nblk = ATT_W // 128
    ng = len(DILATIONS)

    def body(*refs):
        grads = refs[:3 * ng]
        c_ref, s1_ref, s2_ref, _, du_ref = refs[3 * ng:3 * ng + 5]
        scrs = refs[3 * ng + 5:]
        c, s1, s2 = c_ref[...], s1_ref[...], s2_ref[...]
        for part in range(3):
            for gi, dil in enumerate(DILATIONS):
                if dil > 1:
                    _from_view(grads[3 * gi + part], scrs[gi], dil)
            for b in range(nblk):
                tot = None
                for gi, dil in enumerate(DILATIONS):
                    term = grads[3 * gi + part][:, 128 * b:128 * (b + 1)] if dil == 1 else scrs[gi][b]
                    tot = term if tot is None else tot + term
                if part == 0:
                    tot = _rope_t(tot * scale, c, s1, s2)
                elif part == 1:
                    tot = _rope_t(tot, c, s1, s2)
                du_ref[:, ATT_W * part + 128 * b:ATT_W * part + 128 * (b + 1)] = tot

    tab = pl.BlockSpec((tm, 128), lambda i: (i, 0))
    return pl.pallas_call(
        body, name=name, grid=(s_len // tm,),
        in_specs=[_view_spec(dil, ATT_W) for dil in DILATIONS for _ in range(3)] + [tab] * 3
        + [pl.BlockSpec(memory_space=pl.ANY)],
        out_specs=pl.BlockSpec((tm, 3 * ATT_W), lambda i: (i, COL_AQ // 3)),
        out_shape=jax.ShapeDtypeStruct(du.shape, F32),
        scratch_shapes=[pltpu.VMEM((nblk, tm, 128), F32) for _ in DILATIONS],
        input_output_aliases={3 * ng + 3: 0},
        compiler_params=_params(("parallel",)),
    )(*[a for g in dgroups for a in g], *tabs, du)


def _head_weights(w, b):
    lane = lax.broadcasted_iota(jnp.int32, (1, 128), 1)
    return jnp.where(lane < ATT_HD, w[:, 2 * b:2 * b + 1], w[:, 2 * b + 1:2 * b + 2])


def _assemble_fwd(pw, u, o_gdn, gnw, o_groups, lse_groups, *, name):
    s_len = u.shape[0]
    tm = VIEW_ROWS
    c = CONV_CH
    nblk = ATT_W // 128
    ng = len(DILATIONS)

    def body(*refs):
        pw_ref, cg_ref, z_ref, ag_ref, og_ref, gnw_ref = refs[:6]
        o_refs, l_refs = refs[6:6 + ng], refs[6 + ng:6 + 2 * ng]
        y_ref, oa_ref = refs[6 + 2 * ng:8 + 2 * ng]
        lse_outs = refs[8 + 2 * ng:8 + 3 * ng]
        o_scr, l_scr = refs[8 + 3 * ng:8 + 4 * ng], refs[8 + 4 * ng:8 + 5 * ng]
        lse_scr = refs[8 + 5 * ng]
        y_ref[:, :c] = (pw_ref[...] * _silu(cg_ref[...])).astype(BF16)
        gw = gnw_ref[...]
        for h in range(GDN_HEADS):
            hs = slice(GDN_D * h, GDN_D * (h + 1))
            oh = og_ref[:, hs]
            yn = oh * lax.rsqrt(jnp.mean(oh * oh, axis=-1, keepdims=True) + 1e-6) * gw
            y_ref[:, c + GDN_D * h:c + GDN_D * (h + 1)] = (yn * _silu(z_ref[:, hs])).astype(BF16)
        for gi, dil in enumerate(DILATIONS):
            if dil > 1:
                _from_view(o_refs[gi], o_scr[gi], dil)
                _from_view(l_refs[gi], l_scr[gi], dil)
        ls = [l_refs[gi][...] if dil == 1 else l_scr[gi][0] for gi, dil in enumerate(DILATIONS)]
        m = functools.reduce(jnp.maximum, ls)
        es = [jnp.exp(l - m) for l in ls]
        den = functools.reduce(lambda a, b: a + b, es)
        lse_scr[0] = m + jnp.log(den)
        ws = [e / den for e in es]
        for b in range(nblk):
            bs = slice(128 * b, 128 * (b + 1))
            o = None
            for gi, dil in enumerate(DILATIONS):
                term = _head_weights(ws[gi], b) * (o_refs[gi][:, bs] if dil == 1 else o_scr[gi][b])
                o = term if o is None else o + term
            oa_ref[:, bs] = o
            y_ref[:, c + GDN_W + 128 * b:c + GDN_W + 128 * (b + 1)] = (o * _silu(ag_ref[:, bs])).astype(BF16)
        for gi, dil in enumerate(DILATIONS):
            _to_view(lse_scr, lse_outs[gi], dil, F32)

    wide = pl.BlockSpec((tm, 768), lambda i: (i, 0))
    return pl.pallas_call(
        body, name=name, grid=(s_len // tm,),
        in_specs=[pl.BlockSpec((tm, c), lambda i: (i, 0)), pl.BlockSpec((tm, c), lambda i: (i, 1024 // c)),
                  pl.BlockSpec((tm, 768), lambda i: (i, COL_GQ + 3)), pl.BlockSpec((tm, 768), lambda i: (i, COL_AQ + 3)),
                  wide, pl.BlockSpec((1, 128), lambda i: (0, 0))]
        + [_view_spec(dil, ATT_W) for dil in DILATIONS] + [_view_spec(dil, 128) for dil in DILATIONS],
        out_specs=[pl.BlockSpec((tm, D_MODEL), lambda i: (i, 0)), wide] + [_view_spec(dil, 128) for dil in DILATIONS],
        out_shape=[jax.ShapeDtypeStruct((s_len, D_MODEL), BF16), jax.ShapeDtypeStruct((s_len, ATT_W), F32)]
        + [_view_shape(s_len, dil, 128, F32) for dil in DILATIONS],
        scratch_shapes=[pltpu.VMEM((nblk, tm, 128), F32) for _ in DILATIONS]
        + [pltpu.VMEM((1, tm, 128), F32) for _ in DILATIONS] + [pltpu.VMEM((1, tm, 128), F32)],
        compiler_params=_params(("parallel",)),
    )(pw, u, u, u, o_gdn, gnw, *o_groups, *lse_groups)


def _assemble_bwd(dy, pw, u, o_gdn, gnw, o_att, *, name):
    s_len = u.shape[0]
    tm = VIEW_ROWS
    c = CONV_CH
    nsteps = s_len // tm
    nblk = ATT_W // 128
    ng = len(DILATIONS)

    def body(dy_ref, pw_ref, cg_ref, z_ref, ag_ref, og_ref, gnw_ref, oa_ref,
             du_ref, dpw_ref, dog_ref, dgw_ref, *rest):
        do_outs, dl_outs = rest[:ng], rest[ng:2 * ng]
        acc_ref, do_scr, dl_scr = rest[2 * ng:]
        i = pl.program_id(0)

        @pl.when(i == 0)
        def _():
            acc_ref[...] = jnp.zeros_like(acc_ref)

        du_ref[...] = jnp.zeros_like(du_ref)
        dyc = dy_ref[:, :c]
        cg = cg_ref[...]
        dpw_ref[...] = dyc * _silu(cg)
        du_ref[:, 1024:1024 + c] = dyc * pw_ref[...] * _dsilu(cg)
        gw = gnw_ref[...]
        dgw = jnp.zeros((8, 128), F32)
        for h in range(GDN_HEADS):
            hs = slice(GDN_D * h, GDN_D * (h + 1))
            oh = og_ref[:, hs]
            zh = z_ref[:, hs]
            dyh = dy_ref[:, c + GDN_D * h:c + GDN_D * (h + 1)]
            r = lax.rsqrt(jnp.mean(oh * oh, axis=-1, keepdims=True) + 1e-6)
            xn = oh * r
            dyn = dyh * _silu(zh)
            du_ref[:, GDN_W * (COL_GQ + 3) + GDN_D * h:GDN_W * (COL_GQ + 3) + GDN_D * (h + 1)] = dyh * xn * gw * _dsilu(zh)
            dgw = dgw + (dyn * xn).reshape(tm // 8, 8, 128).sum(axis=0)
            dxn = dyn * gw
            dog_ref[:, hs] = r * (dxn - xn * jnp.mean(dxn * xn, axis=-1, keepdims=True))
        acc_ref[...] += dgw
        lane = lax.broadcasted_iota(jnp.int32, (tm, 128), 1)
        delta = jnp.zeros((tm, 128), F32)
        for b in range(ATT_W // 128):
            bs = slice(128 * b, 128 * (b + 1))
            dya = dy_ref[:, c + GDN_W + 128 * b:c + GDN_W + 128 * (b + 1)]
            ag = ag_ref[:, bs]
            oa = oa_ref[:, bs]
            do = dya * _silu(ag)
            do_scr[b] = do
            du_ref[:, ATT_W * (COL_AQ + 3) + 128 * b:ATT_W * (COL_AQ + 3) + 128 * (b + 1)] = dya * oa * _dsilu(ag)
            prod = do * oa
            lo = jnp.sum(jnp.where(lane < ATT_HD, prod, 0.0), axis=-1, keepdims=True)
            hi = jnp.sum(jnp.where(lane >= ATT_HD, prod, 0.0), axis=-1, keepdims=True)
            delta = delta + jnp.where(lane == 2 * b, lo, 0.0) + jnp.where(lane == 2 * b + 1, hi, 0.0)
        dl_scr[0] = delta
        for gi, dil in enumerate(DILATIONS):
            _to_view(do_scr, do_outs[gi], dil, BF16)
            _to_view(dl_scr, dl_outs[gi], dil, F32)

        @pl.when(i == nsteps - 1)
        def _():
            dgw_ref[...] = jnp.sum(acc_ref[...], axis=0, keepdims=True)

    wide = pl.BlockSpec((tm, 768), lambda i: (i, 0))
    vec = pl.BlockSpec((1, 128), lambda i: (0, 0))
    outs = pl.pallas_call(
        body, name=name, grid=(nsteps,),
        in_specs=[pl.BlockSpec((tm, D_MODEL), lambda i: (i, 0)), pl.BlockSpec((tm, c), lambda i: (i, 0)),
                  pl.BlockSpec((tm, c), lambda i: (i, 1024 // c)), pl.BlockSpec((tm, 768), lambda i: (i, COL_GQ + 3)),
                  pl.BlockSpec((tm, 768), lambda i: (i, COL_AQ + 3)), wide, vec, wide],
        out_specs=[pl.BlockSpec((tm, IN_WP), lambda i: (i, 0)), pl.BlockSpec((tm, c), lambda i: (i, 0)), wide, vec]
        + [_view_spec(dil, ATT_W) for dil in DILATIONS] + [_view_spec(dil, 128) for dil in DILATIONS],
        out_shape=[jax.ShapeDtypeStruct((s_len, IN_WP), F32), jax.ShapeDtypeStruct((s_len, c), F32),
                   jax.ShapeDtypeStruct((s_len, GDN_W), F32), jax.ShapeDtypeStruct((1, 128), F32)]
        + [_view_shape(s_len, dil, ATT_W, BF16) for dil in DILATIONS]
        + [_view_shape(s_len, dil, 128, F32) for dil in DILATIONS],
        scratch_shapes=[pltpu.VMEM((8, 128), F32), pltpu.VMEM((nblk, tm, 128), F32), pltpu.VMEM((1, tm, 128), F32)],
        compiler_params=_params(("arbitrary",)),
    )(dy, pw, u, u, u, o_gdn, gnw, o_att)
    return outs[:4], outs[4:4 + ng], outs[4 + ng:]


def _layer_fwd(x, p, tabs):
    h = _rms_fwd(x, p["norm_w"], name="rms_fwd")
    u = _matmul(h, p["wp"], name="in_proj", tk=2048)
    conv, sw = _conf_fwd(u, p["dw_w"], p["dw_b"], p["ln_w"], p["ln_b"], name="conf_fwd")
    pw = _matmul(sw, p["pw_w"], name="conf_pw")
    qn, kn, vc, bg = _gdn_prep_fwd(u, p["cw"], p["al"], p["dtb"], name="gdn_prep_fwd")
    wk, wv, qd, kd, pm, t, g_cum = _gdn_intra_fwd(qn, kn, vc, bg, name="gdn_intra_fwd")
    o_gdn, vn, sprev = _gdn_scan_fwd(wk, wv, qd, kd, pm, g_cum, name="gdn_scan_fwd")
    qkv = _att_prep_fwd(u, tabs, name="att_prep_fwd")
    groups = [_att_fwd(*qkv[gi], dil, name=f"att_fwd_d{dil}") for gi, dil in enumerate(DILATIONS)]
    outs = _assemble_fwd(pw, u, o_gdn, p["gnw"], [g[0] for g in groups], [g[1] for g in groups],
                         name="assemble_fwd")
    y, o_att, lse = outs[0], outs[1], outs[2:]
    x_next = _matmul(y, p["wout"], add=x, name="out_proj", tk=2048)
    saved = dict(x=x, h=h, u=u, conv=conv, sw=sw, pw=pw, qn=qn, kn=kn, vc=vc, bg=bg, wk=wk, qd=qd, kd=kd, pm=pm,
                 t=t, g_cum=g_cum, vn=vn, sprev=sprev, o_gdn=o_gdn, qkv=qkv, o_att=o_att, lse=lse, y=y)
    return x_next, saved


def _layer_bwd(dx_out, s, p, tabs, layer, big):
    dy = _matmul(dx_out, p["wout"], tb=True, name="out_proj_dy", tk=2048)
    d_wout = _matmul(s["y"], dx_out, ta=True, name="out_proj_dw", tn=2048, stack=(big[1], layer, DEPTH))
    (du, dpw, dog, dgw), do_views, dl_views = _assemble_bwd(dy, s["pw"], s["u"], s["o_gdn"], p["gnw"], s["o_att"],
                                                            name="assemble_bwd")
    dsw = _matmul(dpw, p["pw_w"], tb=True, name="conf_pw_dx")
    d_pw_w = _matmul(s["sw"], dpw, ta=True, name="conf_pw_dw", stack=(big[2], layer, DEPTH))
    dconv, ln_sums = _conf_bwd_ln(dsw, s["conv"], p["ln_w"], p["ln_b"], name="conf_bwd_ln")
    du, d_dw_w = _conf_bwd_conv(s["u"], dconv, p["dw_w"], du, name="conf_bwd_conv")
    dvn, ds_all = _gdn_scan_bwd(dog, s["wk"], s["qd"], s["kd"], s["pm"], s["g_cum"], name="gdn_scan_bwd")
    dqkv, dbg = _gdn_intra_bwd(s["qn"], s["kn"], s["vc"], s["bg"], s["g_cum"], s["t"], dog, dvn, s["vn"],
                               s["sprev"], ds_all, name="gdn_intra_bwd")
    du, d_cw = _gdn_prep_bwd(s["u"], dqkv, p["cw"], du, name="gdn_prep_bwd")
    du, ba_sums = _gdn_ba_bwd(s["u"], dbg, p["al"], p["dtb"], du, name="gdn_ba_bwd")
    dgroups = []
    for gi, dil in enumerate(DILATIONS):
        args = (*s["qkv"][gi], do_views[gi], s["lse"][gi], dl_views[gi], dil)
        dgroups.append(_att_bwd(*args, name=f"att_bwd_d{dil}"))
    du = _att_prep_bwd(dgroups, tabs, du, name="att_prep_bwd")
    dh = _matmul(du, p["wp"], tb=True, name="in_proj_dx", tn=2048)
    d_wp = _matmul(s["h"], du, ta=True, name="in_proj_dw", tn=2048, stack=(big[0], layer, DEPTH))
    dx, d_norm_w = _rms_bwd(s["x"], dh, p["norm_w"], dx_out, name="rms_bwd")
    small = dict(norm_w=d_norm_w, gnw=dgw, ln_sums=ln_sums, dw_w=d_dw_w, cw=d_cw, ba_sums=ba_sums)
    return dx, (d_wp, d_wout, d_pw_w), small


def _trunk(x, target, params, final_norm_w):
    tabs = _rope_tables(x.shape[0])
    layers = [{k: v[l] for k, v in params.items()} for l in range(DEPTH)]
    saved = []
    for p in layers:
        x, s = _layer_fwd(x, p, tabs)
        saved.append(s)
    dx, d_final, loss = _loss_head(x, final_norm_w, target, name="loss_head")
    big = (None, None, None)
    small = [None] * DEPTH
    for l in reversed(range(DEPTH)):
        dx, big, small[l] = _layer_bwd(dx, saved[l], layers[l], tabs, l, big)
    grads = {k: jnp.stack([sm[k] for sm in small]) for k in small[0]}
    grads.update(wp=big[0], wout=big[1], pw_w=big[2])
    return loss[0, 0], dx, grads, d_final


ANY = pl.BlockSpec(memory_space=pl.ANY)


def _position():
    return lax.axis_index("x"), lax.axis_index("y"), lax.axis_index("c")


def _other_chips(x, y):
    return [(1 - x, y), (x, 1 - y), (1 - x, 1 - y)]


def _gather_chips(shards, *, name):
    n = len(shards)
    kinds = 12

    def body(*refs):
        ins, outs = refs[:n], refs[n:2 * n]
        send, recv = refs[2 * n:]
        x, y, c = _position()
        me, sib = (x, y, c), (x, y, 1 - c)
        xn, yn, dg = (1 - x, y), (x, 1 - y), (1 - x, 1 - y)
        pa, pb = 2 * c, 2 * c + 1

        def copy(k, a, chip, layer, to, src=None):
            dst = outs[a].at[2 * chip[0] + chip[1], pl.ds(layer, 1)]
            return pltpu.make_async_remote_copy(
                src_ref=dst if src is None else src, dst_ref=dst, send_sem=send.at[k * n + a],
                recv_sem=recv.at[k * n + a], device_id=to, device_id_type=MESH)

        def own(k, a, layer, chip):
            return copy(k, a, (x, y), layer, (*chip, c), src=ins[a].at[pl.ds(layer, 1)])

        sends = []
        for a in range(n):
            sends += [own(0, a, pa, xn), own(1, a, pb, yn), own(2, a, pb, xn), own(3, a, pa, yn)]
        for cp in sends:
            cp.start()
        arrivals = [(1, yn, pb, (4, xn)), (0, xn, pa, (5, yn)), (2, xn, pb, None), (3, yn, pa, None),
                    (4, dg, pb, None), (5, dg, pa, None)]
        for a in range(n):
            for j, (k, chip, layer, onward) in enumerate(arrivals):
                copy(k, a, chip, layer, me).wait_recv()
                if onward is not None:
                    cp = copy(onward[0], a, chip, layer, (*onward[1], c))
                    cp.start()
                    sends.append(cp)
                cp = copy(6 + j, a, chip, layer, sib)
                cp.start()
                sends.append(cp)
        for a in range(n):
            for j, (k, chip, layer, onward) in enumerate(arrivals):
                copy(6 + j, a, chip, layer + 2 - 4 * c, me).wait_recv()
        for cp in sends:
            cp.wait_send()

    return pl.pallas_call(
        body, name=name, in_specs=[ANY] * n, out_specs=[ANY] * n,
        out_shape=[jax.ShapeDtypeStruct((4,) + s.shape, s.dtype) for s in shards],
        scratch_shapes=[pltpu.SemaphoreType.DMA((kinds * n,)), pltpu.SemaphoreType.DMA((kinds * n,))],
    )(*shards)


def _to_sibling(arrs, *, name):
    n = len(arrs)

    def body(*refs):
        ins, outs = refs[:n], refs[n:2 * n]
        send, recv = refs[2 * n:]
        x, y, c = _position()
        cps = [pltpu.make_async_remote_copy(src_ref=ins[a], dst_ref=outs[a], send_sem=send.at[a],
                                            recv_sem=recv.at[a], device_id=(x, y, 1 - c), device_id_type=MESH)
               for a in range(n)]
        for cp in cps:
            cp.start()
        for cp in cps:
            cp.wait()

    return pl.pallas_call(
        body, name=name, in_specs=[ANY] * n, out_specs=[ANY] * n,
        out_shape=[jax.ShapeDtypeStruct(a.shape, a.dtype) for a in arrs],
        scratch_shapes=[pltpu.SemaphoreType.DMA((n,)), pltpu.SemaphoreType.DMA((n,))],
    )(*arrs)


def _to_chips(arrs, *, name):
    n = len(arrs)

    def body(*refs):
        ins, outs = refs[:n], refs[n:2 * n]
        send, recv = refs[2 * n:]
        x, y, c = _position()
        cps = [pltpu.make_async_remote_copy(
            src_ref=ins[a].at[2 * chip[0] + chip[1]], dst_ref=outs[a].at[j], send_sem=send.at[j * n + a],
            recv_sem=recv.at[j * n + a], device_id=(*chip, c), device_id_type=MESH)
            for j, chip in enumerate(_other_chips(x, y)) for a in range(n)]
        for cp in cps:
            cp.start()
        for cp in cps:
            cp.wait()

    return pl.pallas_call(
        body, name=name, in_specs=[ANY] * n, out_specs=[ANY] * n,
        out_shape=[jax.ShapeDtypeStruct((3,) + a.shape[1:], a.dtype) for a in arrs],
        scratch_shapes=[pltpu.SemaphoreType.DMA((3 * n,)), pltpu.SemaphoreType.DMA((3 * n,))],
    )(*arrs)


def _join_halves(fulls, *, name):
    n = len(fulls)

    def body(*refs):
        ins, outs = refs[:n], refs[n:2 * n]
        send, recv = refs[2 * n:]
        x, y, c = _position()

        def copy(a, rows):
            return pltpu.make_async_remote_copy(
                src_ref=ins[a].at[rows], dst_ref=outs[a].at[rows], send_sem=send.at[a], recv_sem=recv.at[a],
                device_id=(x, y, 1 - c), device_id_type=MESH)

        cps = [copy(a, pl.ds(2 * c, 2)) for a in range(n)]
        for cp in cps:
            cp.start()
        for a in range(n):
            cps[a].wait_send()
            copy(a, pl.ds(2 * (1 - c), 2)).wait_recv()

    return pl.pallas_call(
        body, name=name, in_specs=[ANY] * n, out_specs=[ANY] * n,
        out_shape=[jax.ShapeDtypeStruct(f.shape, f.dtype) for f in fulls],
        scratch_shapes=[pltpu.SemaphoreType.DMA((n,)), pltpu.SemaphoreType.DMA((n,))],
        input_output_aliases={a: a for a in range(n)},
    )(*fulls)


def _allreduce_small(packed, *, name):
    rows = packed.shape[0]
    ndev = 8

    def body(x_ref, sum_ref, all_ref, send, recv, lsem):
        x, y, c = _position()
        me, sib = (x, y, c), (x, y, 1 - c)
        chips = _other_chips(x, y)

        def blk(px, py, pc):
            return all_ref.at[pl.ds((4 * px + 2 * py + pc) * rows, rows), :]

        def copy(k, block, to, src=None):
            return pltpu.make_async_remote_copy(
                src_ref=blk(*block) if src is None else src, dst_ref=blk(*block), send_sem=send.at[k],
                recv_sem=recv.at[k], device_id=to, device_id_type=MESH)

        mine = pltpu.make_async_copy(x_ref, blk(*me), lsem)
        mine.start()
        first = [copy(0, me, sib, src=x_ref)] + [copy(1 + j, me, (*chip, c), src=x_ref) for j, chip in enumerate(chips)]
        for cp in first:
            cp.start()
        passed = [copy(4 + j, (*chip, c), sib) for j, chip in enumerate(chips)]
        for j, chip in enumerate(chips):
            copy(1 + j, (*chip, c), me).wait_recv()
            passed[j].start()
        copy(0, sib, me).wait_recv()
        for j, chip in enumerate(chips):
            copy(4 + j, (*chip, 1 - c), me).wait_recv()
        for cp in first + passed:
            cp.wait_send()
        mine.wait()
        acc = all_ref[0:rows, :]
        for d in range(1, ndev):
            acc = acc + all_ref[d * rows:(d + 1) * rows, :]
        sum_ref[...] = acc

    vm = pl.BlockSpec(memory_space=pltpu.VMEM)
    return pl.pallas_call(
        body, name=name, in_specs=[vm], out_specs=vm, out_shape=jax.ShapeDtypeStruct((rows, 128), F32),
        scratch_shapes=[pltpu.VMEM((ndev * rows, 128), F32), pltpu.SemaphoreType.DMA((7,)),
                        pltpu.SemaphoreType.DMA((7,)), pltpu.SemaphoreType.DMA],
        compiler_params=pltpu.CompilerParams(vmem_limit_bytes=VMEM_LIMIT),
    )(packed)


def _pack(arrs):
    flat = jnp.concatenate([a.reshape(-1) for a in arrs])
    pad = (-flat.shape[0]) % 1024
    return jnp.pad(flat, (0, pad)).reshape(-1, 128)


def _unpack(packed, shapes):
    flat = packed.reshape(-1)
    out, pos = [], 0
    for s in shapes:
        size = math.prod(s)
        out.append(flat[pos:pos + size].reshape(s))
        pos += size
    return out


def _pad_cols(w):
    zeros = jnp.zeros(w.shape[:-1] + (IN_WP - IN_W,), w.dtype)
    return jnp.concatenate([w[..., :ORIG_BA], w[..., ORIG_ATT:], w[..., ORIG_BA:ORIG_ATT], zeros], axis=-1)


def _chip_cols(j):
    per = IN_W // 4
    lo, hi = j * per, (j + 1) * per
    out = []
    for o0, o1, p0 in ((0, ORIG_BA, 0), (ORIG_BA, ORIG_ATT, COL_BA), (ORIG_ATT, IN_W, ORIG_BA)):
        a, b = max(lo, o0), min(hi, o1)
        if a < b:
            out.append((p0 + a - o0, p0 + b - o0))
    return out


def _shards_to_padded(g):
    pieces = []
    for j in range(4):
        loc = 0
        for p0, p1 in _chip_cols(j):
            pieces.append((p0, g[j][..., loc:loc + p1 - p0]))
            loc += p1 - p0
    pieces.sort(key=lambda t: t[0])
    zeros = jnp.zeros(g.shape[1:-1] + (IN_WP - IN_W,), g.dtype)
    return jnp.concatenate([p for _, p in pieces] + [zeros], axis=-1)


def _padded_to_shards(g, dtype):
    return jnp.stack([jnp.concatenate([g[..., p0:p1] for p0, p1 in _chip_cols(j)], axis=-1).astype(dtype)
                      for j in range(4)])


def _unpad_cols(w):
    n_att = IN_W - ORIG_ATT
    return jnp.concatenate([w[..., :ORIG_BA], w[..., COL_BA:COL_BA + ORIG_ATT - ORIG_BA],
                            w[..., ORIG_BA:ORIG_BA + n_att]], axis=-1)


def _lanes(v, first):
    return jnp.pad(v, ((0, 0), (first, 128 - first - v.shape[1])))[:, None, :]


def _by_chip(g, axis):
    shape = g.shape[:axis] + (4, g.shape[axis] // 4) + g.shape[axis + 1:]
    return jnp.moveaxis(g.reshape(shape), axis, 0)


def kernel(x, norm_w, w_in, conv_qkv_w, a_log, dt_bias, gdn_norm_w, conf_dw_w, conf_dw_b, conf_ln_w, conf_ln_b, conf_pw_w, w_out, final_norm_w, loss_target, m_norm_w, m_w_in, m_conv_qkv_w, m_a_log, m_dt_bias, m_gdn_norm_w, m_conf_dw_w, m_conf_dw_b, m_conf_ln_w, m_conf_ln_b, m_conf_pw_w, m_w_out, m_final_norm_w, v_norm_w, v_w_in, v_conv_qkv_w, v_a_log, v_dt_bias, v_gdn_norm_w, v_conf_dw_w, v_conf_dw_b, v_conf_ln_w, v_conf_ln_b, v_conf_pw_w, v_w_out, v_final_norm_w):
    xi, yi, ci = _position()
    chip = 2 * xi + yi

    shards = [w_in.astype(BF16), w_out.astype(BF16), conf_pw_w.astype(BF16), conv_qkv_w, conf_dw_w]
    g_in, g_out, g_pw, g_cw, g_dw = [
        lax.dynamic_update_slice_in_dim(g, s[None], chip, axis=0)
        for g, s in zip(_gather_chips(shards, name="gather_weights"), shards)]
    cw_full = jnp.moveaxis(g_cw, 0, 2).reshape(DEPTH, SHORT_CONV, 3 * GDN_W)
    dw_full = jnp.moveaxis(g_dw, 0, 2).reshape(DEPTH, CONV_WIDTH, CONV_CH)
    params = dict(
        norm_w=norm_w[:, None, :],
        wp=_shards_to_padded(g_in),
        wout=jnp.moveaxis(g_out, 0, 1).reshape(DEPTH, D_MODEL, D_MODEL),
        pw_w=jnp.moveaxis(g_pw, 0, 1).reshape(DEPTH, CONV_CH, CONV_CH),
        cw=jnp.pad(cw_full, ((0, 0), (0, SHALO - SHORT_CONV), (0, 0))),
        dw_w=jnp.pad(dw_full, ((0, 0), (0, HALO - CONV_WIDTH), (0, 0))),
        al=_lanes(a_log, GDN_HEADS), dtb=_lanes(dt_bias, GDN_HEADS), gnw=gdn_norm_w[:, None, :],
        dw_b=conf_dw_b[:, None, :], ln_w=conf_ln_w[:, None, :], ln_b=conf_ln_b[:, None, :],
    )

    loss_part, grad_x, grads, d_final = _trunk(x[0], loss_target[0], params, final_norm_w[None, :])
    loss = lax.psum(loss_part, ("x", "y", "c"))

    def half_by_chip(first, dtype):
        wp, wout, pw = [lax.dynamic_slice_in_dim(grads[k], first, 2, axis=0) for k in ("wp", "wout", "pw_w")]
        return [_padded_to_shards(wp, dtype), _by_chip(wout, 1).astype(dtype), _by_chip(pw, 1).astype(dtype)]

    keep = half_by_chip(2 * ci, F32)
    give = half_by_chip(2 * (1 - ci), BF16)
    got = _to_sibling(give, name="grads_to_sibling")
    pair = [_sum_arrays([k.reshape((8,) + k.shape[2:]), r.reshape((8,) + r.shape[2:])], name=f"pair_sum_{i}",
                        out_dtype=BF16).reshape(k.shape) for i, (k, r) in enumerate(zip(keep, got))]
    arrived = _to_chips(pair, name="grads_to_chips")
    halves = []
    for i, (pr, ar) in enumerate(zip(pair, arrived)):
        own = lax.dynamic_index_in_dim(pr, chip, axis=0, keepdims=False)
        halves.append(_sum_into_half([own, ar[0], ar[1], ar[2]], ci, name=f"chip_sum_{i}"))
    g_w_in, g_w_out, g_pw_w = _join_halves(halves, name="join_halves")

    ba = grads["ba_sums"]
    small = [grads["norm_w"], ba[:, 0:1, :], ba[:, 1:2, :], grads["gnw"], grads["ln_sums"][:, 2:3, :],
             grads["ln_sums"][:, 0:1, :], grads["ln_sums"][:, 1:2, :], d_final,
             grads["cw"][:, :SHORT_CONV, :], grads["dw_w"][:, :CONV_WIDTH, :]]
    red = _unpack(_allreduce_small(_pack(small), name="allreduce_small"), [s.shape for s in small])
    g_norm_w = red[0][:, 0, :]
    g_a_log = red[1][:, 0, GDN_HEADS:2 * GDN_HEADS]
    g_dt_bias = red[2][:, 0, GDN_HEADS:2 * GDN_HEADS]
    g_gnw, g_dw_b, g_ln_w, g_ln_b = red[3][:, 0, :], red[4][:, 0, :], red[5][:, 0, :], red[6][:, 0, :]
    g_final = red[7][0]
    g_cw = lax.dynamic_slice_in_dim(red[8], chip * (3 * GDN_W // 4), 3 * GDN_W // 4, axis=2)
    g_dw_w = lax.dynamic_slice_in_dim(red[9], chip * (CONV_CH // 4), CONV_CH // 4, axis=2)

    d_w_in, nm_w_in, nv_w_in = _adamw(w_in, g_w_in, m_w_in, v_w_in, name="adamw_w_in")
    d_w_out, nm_w_out, nv_w_out = _adamw(w_out, g_w_out, m_w_out, v_w_out, name="adamw_w_out")
    d_pw_w, nm_pw_w, nv_pw_w = _adamw(conf_pw_w, g_pw_w, m_conf_pw_w, v_conf_pw_w, name="adamw_pw")
    sw = [norm_w, a_log, dt_bias, gdn_norm_w, conf_dw_b, conf_ln_w, conf_ln_b, final_norm_w, conv_qkv_w, conf_dw_w]
    sg = [g_norm_w, g_a_log, g_dt_bias, g_gnw, g_dw_b, g_ln_w, g_ln_b, g_final, g_cw, g_dw_w]
    sm = [m_norm_w, m_a_log, m_dt_bias, m_gdn_norm_w, m_conf_dw_b, m_conf_ln_w, m_conf_ln_b, m_final_norm_w,
          m_conv_qkv_w, m_conf_dw_w]
    sv = [v_norm_w, v_a_log, v_dt_bias, v_gdn_norm_w, v_conf_dw_b, v_conf_ln_w, v_conf_ln_b, v_final_norm_w,
          v_conv_qkv_w, v_conf_dw_w]
    shapes = [a.shape for a in sw]
    packed = _adamw(_pack(sw)[None], _pack(sg)[None], _pack(sm)[None], _pack(sv)[None], name="adamw_small")
    sd, snm, snv = [_unpack(pk[0], shapes) for pk in packed]

    def order(big3, small10):
        s = small10
        return [s[0], big3[0], s[8], s[1], s[2], s[3], s[9], s[4], s[5], s[6], big3[2], big3[1], s[7]]

    return (loss, grad_x[None], *order([g_w_in, g_w_out, g_pw_w], sg),
            *order([d_w_in, d_w_out, d_pw_w], sd), *order([nm_w_in, nm_w_out, nm_pw_w], snm),
            *order([nv_w_in, nv_w_out, nv_pw_w], snv))
```

```python
import functools
import math

import jax
import jax.numpy as jnp
from jax import lax
from jax.experimental import pallas as pl
from jax.experimental.pallas import tpu as pltpu

F32, BF16 = jnp.float32, jnp.bfloat16
HIGHEST = lax.Precision.HIGHEST
MESH = pl.DeviceIdType.MESH

D_MODEL = 2048
DEPTH = 4
CONV_CH = 512
GDN_W = 768
GDN_HEADS = 6
GDN_D = 128
ATT_W = 768
ATT_HEADS = 12
ATT_HD = 64
CONV_WIDTH = 31
SHORT_CONV = 4
GDN_CHUNK = 64
ROPE_THETA = 500000.0
ROPE_DIM = ATT_HD // 4
DIL_PATTERNS = ((128, 1), (512, 4), (2048, 16))
ATT_BLOCK = 128
NEG_INF = -1e30
IN_W = 7692

IN_WP = 8192
COL_BA = 7680
ORIG_BA = 4608
ORIG_ATT = 4620

ADAM_LR = 0.001
ADAM_B1 = 0.9
ADAM_B2 = 0.999
ADAM_EPS = 1e-08
ADAM_WD = 0.01
ADAM_STEP = 10

VMEM_LIMIT = 56 * 1024 * 1024


def _params(sem=None):
    return pltpu.CompilerParams(dimension_semantics=sem, vmem_limit_bytes=VMEM_LIMIT)


def _sigmoid(x):
    return 0.5 * jnp.tanh(0.5 * x) + 0.5


def _silu(x):
    return x * _sigmoid(x)


def _dsilu(x):
    s = _sigmoid(x)
    return s * (1.0 + x * (1.0 - s))


def _dot(a, b, dims, precision=None):
    return lax.dot_general(a, b, (dims, ((), ())), precision=precision, preferred_element_type=F32)


def _nn(a, b, precision=None):
    return _dot(a, b, ((1,), (0,)), precision)


def _nt(a, b, precision=None):
    return _dot(a, b, ((1,), (1,)), precision)


def _tn(a, b, precision=None):
    return _dot(a, b, ((0,), (0,)), precision)


def _matmul(a, b, *, name, ta=False, tb=False, out_dtype=F32, add=None, stack=None, tm=1024, tn=1024, tk=1024):
    if ta:
        k_dim, m_dim = a.shape
    else:
        m_dim, k_dim = a.shape
    n_dim = b.shape[0] if tb else b.shape[1]
    tm, tn, tk = min(tm, m_dim), min(tn, n_dim), min(tk, k_dim)
    assert m_dim % tm == 0 and n_dim % tn == 0 and k_dim % tk == 0, (name, a.shape, b.shape)
    nk = k_dim // tk
    a_spec = pl.BlockSpec((tk, tm), lambda i, j, k: (k, i)) if ta else pl.BlockSpec((tm, tk), lambda i, j, k: (i, k))
    b_spec = pl.BlockSpec((tn, tk), lambda i, j, k: (j, k)) if tb else pl.BlockSpec((tk, tn), lambda i, j, k: (k, j))
    o_spec = pl.BlockSpec((tm, tn), lambda i, j, k: (i, j))
    out_shape = jax.ShapeDtypeStruct((m_dim, n_dim), out_dtype)
    dims = ((0 if ta else 1,), (1 if tb else 0,))
    has_add = add is not None
    ins = [a, b] + ([add] if has_add else [])
    specs = [a_spec, b_spec] + ([o_spec] if has_add else [])
    aliases = {}
    if stack is not None:
        buf, slab, nslabs = stack
        o_spec = pl.BlockSpec((None, tm, tn), lambda i, j, k: (slab, i, j))
        out_shape = jax.ShapeDtypeStruct((nslabs, m_dim, n_dim), out_dtype)
        if buf is not None:
            aliases = {len(ins): 0}
            ins.append(buf)
            specs.append(pl.BlockSpec(memory_space=pl.ANY))
    n_in = len(ins)

    def body(*refs):
        a_ref, b_ref = refs[0], refs[1]
        o_ref = refs[n_in]

        def finish(r):
            if has_add:
                r = r + refs[2][...]
            o_ref[...] = r.astype(out_dtype)

        prod = _dot(a_ref[...].astype(BF16), b_ref[...].astype(BF16), dims)
        if nk == 1:
            finish(prod)
            return
        acc_ref = refs[n_in + 1]
        k = pl.program_id(2)

        @pl.when(k == 0)
        def _():
            acc_ref[...] = prod

        @pl.when(k > 0)
        def _():
            acc_ref[...] += prod

        @pl.when(k == nk - 1)
        def _():
            finish(acc_ref[...])

    return pl.pallas_call(
        body, name=name, grid=(m_dim // tm, n_dim // tn, nk), in_specs=specs, out_specs=o_spec,
        out_shape=out_shape, scratch_shapes=[pltpu.VMEM((tm, tn), F32)] if nk > 1 else [],
        input_output_aliases=aliases,
        compiler_params=_params(("parallel", "parallel", "arbitrary")),
    )(*ins)


def _rms_fwd(x, w, *, name, tm=256):
    s_len, d = x.shape

    def body(x_ref, w_ref, h_ref):
        xv = x_ref[...]
        r = lax.rsqrt(jnp.mean(xv * xv, axis=-1, keepdims=True) + 1e-6)
        h_ref[...] = (xv * r * w_ref[...]).astype(BF16)

    return pl.pallas_call(
        body, name=name, grid=(s_len // tm,),
        in_specs=[pl.BlockSpec((tm, d), lambda i: (i, 0)), pl.BlockSpec((1, d), lambda i: (0, 0))],
        out_specs=pl.BlockSpec((tm, d), lambda i: (i, 0)),
        out_shape=jax.ShapeDtypeStruct((s_len, d), BF16),
        compiler_params=_params(("parallel",)),
    )(x, w)


def _rms_bwd(x, dh, w, dres, *, name, tm=256):
    s_len, d = x.shape
    nsteps = s_len // tm

    def body(x_ref, dh_ref, w_ref, dres_ref, dx_ref, dw_ref, acc_ref):
        i = pl.program_id(0)

        @pl.when(i == 0)
        def _():
            acc_ref[...] = jnp.zeros_like(acc_ref)

        xv = x_ref[...]
        r = lax.rsqrt(jnp.mean(xv * xv, axis=-1, keepdims=True) + 1e-6)
        xn = xv * r
        dy = dh_ref[...]
        dxn = dy * w_ref[...]
        dx_ref[...] = dres_ref[...] + r * (dxn - xn * jnp.mean(dxn * xn, axis=-1, keepdims=True))
        acc_ref[...] += (dy * xn).reshape(tm // 8, 8, d).sum(axis=0)

        @pl.when(i == nsteps - 1)
        def _():
            dw_ref[...] = jnp.sum(acc_ref[...], axis=0, keepdims=True)

    row = pl.BlockSpec((tm, d), lambda i: (i, 0))
    vec = pl.BlockSpec((1, d), lambda i: (0, 0))
    return pl.pallas_call(
        body, name=name, grid=(nsteps,), in_specs=[row, row, vec, row], out_specs=[row, vec],
        out_shape=[jax.ShapeDtypeStruct((s_len, d), F32), jax.ShapeDtypeStruct((1, d), F32)],
        scratch_shapes=[pltpu.VMEM((8, d), F32)],
        compiler_params=_params(("arbitrary",)),
    )(x, dh, w, dres)


def _loss_head(x, w, target, *, name, tm=256):
    s_len, d = x.shape
    nsteps = s_len // tm

    def body(x_ref, w_ref, t_ref, dx_ref, dw_ref, loss_ref, acc_ref, lacc_ref):
        i = pl.program_id(0)

        @pl.when(i == 0)
        def _():
            acc_ref[...] = jnp.zeros_like(acc_ref)
            lacc_ref[...] = jnp.zeros_like(lacc_ref)

        xv = x_ref[...]
        wv = w_ref[...]
        r = lax.rsqrt(jnp.mean(xv * xv, axis=-1, keepdims=True) + 1e-6)
        xn = xv * r
        err = xn * wv - t_ref[...]
        lacc_ref[...] += (err * err).reshape(tm // 8, 8, d).sum(axis=0)
        dy = err * (1.0 / d)
        dxn = dy * wv
        dx_ref[...] = r * (dxn - xn * jnp.mean(dxn * xn, axis=-1, keepdims=True))
        acc_ref[...] += (dy * xn).reshape(tm // 8, 8, d).sum(axis=0)

        @pl.when(i == nsteps - 1)
        def _():
            dw_ref[...] = jnp.sum(acc_ref[...], axis=0, keepdims=True)
            tot = jnp.sum(jnp.sum(lacc_ref[...], axis=0, keepdims=True), axis=1, keepdims=True)
            loss_ref[...] = jnp.broadcast_to(tot * (0.5 / d), (1, 128))

    row = pl.BlockSpec((tm, d), lambda i: (i, 0))
    vec = pl.BlockSpec((1, d), lambda i: (0, 0))
    return pl.pallas_call(
        body, name=name, grid=(nsteps,), in_specs=[row, vec, row],
        out_specs=[row, vec, pl.BlockSpec((1, 128), lambda i: (0, 0))],
        out_shape=[jax.ShapeDtypeStruct((s_len, d), F32), jax.ShapeDtypeStruct((1, d), F32),
                   jax.ShapeDtypeStruct((1, 128), F32)],
        scratch_shapes=[pltpu.VMEM((8, d), F32), pltpu.VMEM((8, d), F32)],
        compiler_params=_params(("arbitrary",)),
    )(x, w, target)


def _rows_block(shape, tr=256):
    lead, rows, cols = shape
    if rows % tr != 0:
        assert rows * cols <= 1 << 20, shape
        tr = rows
    return (lead, rows // tr), pl.BlockSpec((1, tr, cols), lambda a, i: (a, i, 0))


def _adamw(w, g, m, v, *, name):
    grid, spec = _rows_block(w.shape)
    c1 = 1.0 / (1.0 - ADAM_B1 ** ADAM_STEP)
    c2 = 1.0 / (1.0 - ADAM_B2 ** ADAM_STEP)

    def body(w_ref, g_ref, m_ref, v_ref, d_ref, nm_ref, nv_ref):
        gv = g_ref[...]
        nm = ADAM_B1 * m_ref[...] + (1.0 - ADAM_B1) * gv
        nv = ADAM_B2 * v_ref[...] + (1.0 - ADAM_B2) * (gv * gv)
        nm_ref[...] = nm
        nv_ref[...] = nv
        d_ref[...] = -ADAM_LR * ((nm * c1) / (jnp.sqrt(nv * c2) + ADAM_EPS) + ADAM_WD * w_ref[...])

    out = jax.ShapeDtypeStruct(w.shape, F32)
    return pl.pallas_call(
        body, name=name, grid=grid, in_specs=[spec] * 4, out_specs=[spec] * 3, out_shape=[out] * 3,
        compiler_params=_params(("parallel", "parallel")),
    )(w, g, m, v)


def _sum_into_half(arrs, half, *, name):
    lead, rows, cols = arrs[0].shape
    assert lead == 2
    (_, nr), spec0 = _rows_block(arrs[0].shape)
    tr = spec0.block_shape[1]
    n = len(arrs)

    def body(half_ref, *refs):
        del half_ref
        acc = refs[0][...].astype(F32)
        for r in refs[1:n]:
            acc = acc + r[...].astype(F32)
        refs[n][...] = acc

    spec = pl.BlockSpec((1, tr, cols), lambda a, i, h: (a, i, 0))
    return pl.pallas_call(
        body, name=name,
        grid_spec=pltpu.PrefetchScalarGridSpec(
            num_scalar_prefetch=1, grid=(2, nr), in_specs=[spec] * n,
            out_specs=pl.BlockSpec((1, tr, cols), lambda a, i, h: (2 * h[0] + a, i, 0))),
        out_shape=jax.ShapeDtypeStruct((4, rows, cols), F32),
        compiler_params=_params(("parallel", "parallel")),
    )(jnp.reshape(half, (1,)).astype(jnp.int32), *arrs)


def _sum_arrays(arrs, *, name, out_dtype):
    grid, spec = _rows_block(arrs[0].shape)
    n = len(arrs)

    def body(*refs):
        acc = refs[0][...].astype(F32)
        for r in refs[1:n]:
            acc = acc + r[...].astype(F32)
        refs[n][...] = acc.astype(out_dtype)

    return pl.pallas_call(
        body, name=name, grid=grid, in_specs=[spec] * n, out_specs=spec,
        out_shape=jax.ShapeDtypeStruct(arrs[0].shape, out_dtype),
        compiler_params=_params(("parallel", "parallel")),
    )(*arrs)


HALO = 32


def _conf_fwd(u, dw_w, dw_b, ln_w, ln_b, *, name, tm=256):
    s_len = u.shape[0]
    c = CONV_CH

    def body(uc_ref, up_ref, dww_ref, dwb_ref, lnw_ref, lnb_ref, conv_ref, sw_ref, hbuf):
        i = pl.program_id(0)
        hbuf[HALO:, :] = uc_ref[:, :c] * _sigmoid(uc_ref[:, c:])
        hp = up_ref[:, :c] * _sigmoid(up_ref[:, c:])
        hbuf[:HALO, :] = jnp.where(i > 0, hp, 0.0)
        for cb in range(c // 128):
            cs = slice(128 * cb, 128 * (cb + 1))
            acc = jnp.zeros((tm, 128), F32)
            for j in range(CONV_WIDTH):
                acc = acc + hbuf[HALO - CONV_WIDTH + 1 + j:HALO - CONV_WIDTH + 1 + j + tm, cs] * dww_ref[j:j + 1, cs]
            conv_ref[:, cs] = acc + dwb_ref[:, cs]
        cv = conv_ref[...]
        mu = jnp.mean(cv, axis=-1, keepdims=True)
        xc = cv - mu
        var = jnp.mean(xc * xc, axis=-1, keepdims=True)
        ln = xc * lax.rsqrt(var + 1e-5) * lnw_ref[...] + lnb_ref[...]
        sw_ref[...] = _silu(ln).astype(BF16)

    vec = pl.BlockSpec((1, c), lambda i: (0, 0))
    return pl.pallas_call(
        body, name=name, grid=(s_len // tm,),
        in_specs=[pl.BlockSpec((tm, 2 * c), lambda i: (i, 0)),
                  pl.BlockSpec((HALO, 2 * c), lambda i: (jnp.maximum(i * (tm // HALO) - 1, 0), 0)),
                  pl.BlockSpec((HALO, c), lambda i: (0, 0)), vec, vec, vec],
        out_specs=[pl.BlockSpec((tm, c), lambda i: (i, 0))] * 2,
        out_shape=[jax.ShapeDtypeStruct((s_len, c), F32), jax.ShapeDtypeStruct((s_len, c), BF16)],
        scratch_shapes=[pltpu.VMEM((tm + HALO, c), F32)],
        compiler_params=_params(("parallel",)),
    )(u, u, dw_w, dw_b, ln_w, ln_b)


def _conf_bwd_ln(d_sw, conv, ln_w, ln_b, *, name, tm=256):
    s_len, c = conv.shape
    nsteps = s_len // tm

    def body(dsw_ref, conv_ref, lnw_ref, lnb_ref, dconv_ref, sums_ref):
        i = pl.program_id(0)

        @pl.when(i == 0)
        def _():
            sums_ref[...] = jnp.zeros_like(sums_ref)

        cv = conv_ref[...]
        mu = jnp.mean(cv, axis=-1, keepdims=True)
        xc = cv - mu
        rs = lax.rsqrt(jnp.mean(xc * xc, axis=-1, keepdims=True) + 1e-5)
        xhat = xc * rs
        lnw = lnw_ref[...]
        ln = xhat * lnw + lnb_ref[...]
        dln = dsw_ref[...] * _dsilu(ln)
        dxh = dln * lnw
        dconv = rs * (dxh - jnp.mean(dxh, axis=-1, keepdims=True)
                      - xhat * jnp.mean(dxh * xhat, axis=-1, keepdims=True))
        dconv_ref[...] = dconv
        sums_ref[0:1, :] += jnp.sum(dln * xhat, axis=0, keepdims=True)
        sums_ref[1:2, :] += jnp.sum(dln, axis=0, keepdims=True)
        sums_ref[2:3, :] += jnp.sum(dconv, axis=0, keepdims=True)

    row = pl.BlockSpec((tm, c), lambda i: (i, 0))
    vec = pl.BlockSpec((1, c), lambda i: (0, 0))
    return pl.pallas_call(
        body, name=name, grid=(nsteps,), in_specs=[row, row, vec, vec],
        out_specs=[row, pl.BlockSpec((8, c), lambda i: (0, 0))],
        out_shape=[jax.ShapeDtypeStruct((s_len, c), F32), jax.ShapeDtypeStruct((8, c), F32)],
        compiler_params=_params(("arbitrary",)),
    )(d_sw, conv, ln_w, ln_b)


def _conf_bwd_conv(u, dconv, dw_w, du, *, name, tm=256):
    s_len = u.shape[0]
    c = CONV_CH
    nsteps = s_len // tm
    off = HALO - CONV_WIDTH + 1

    def body(uc_ref, up_ref, dc_ref, dn_ref, dww_ref, du_in_ref, du_ref, ddw_ref, hbuf, dbuf, wacc):
        del du_in_ref
        i = pl.program_id(0)

        @pl.when(i == 0)
        def _():
            wacc[...] = jnp.zeros_like(wacc)

        hbuf[HALO:, :] = uc_ref[:, :c] * _sigmoid(uc_ref[:, c:])
        hp = up_ref[:, :c] * _sigmoid(up_ref[:, c:])
        hbuf[:HALO, :] = jnp.where(i > 0, hp, 0.0)
        dbuf[:tm, :] = dc_ref[...]
        dbuf[tm:, :] = jnp.where(i < nsteps - 1, dn_ref[...], 0.0)
        for cb in range(c // 128):
            cs = slice(128 * cb, 128 * (cb + 1))
            dcur = dbuf[0:tm, cs]
            acc = jnp.zeros((tm, 128), F32)
            for j in range(CONV_WIDTH):
                k = CONV_WIDTH - 1 - j
                acc = acc + dbuf[k:k + tm, cs] * dww_ref[j:j + 1, cs]
                prod = hbuf[off + j:off + j + tm, cs] * dcur
                wacc[j, :, cs] += prod.reshape(tm // 8, 8, 128).sum(axis=0)
            a = uc_ref[:, cs]
            sg = _sigmoid(uc_ref[:, c + 128 * cb:c + 128 * (cb + 1)])
            du_ref[:, cs] = (acc * sg).astype(du_ref.dtype)
            du_ref[:, c + 128 * cb:c + 128 * (cb + 1)] = (acc * a * sg * (1.0 - sg)).astype(du_ref.dtype)

        @pl.when(i == nsteps - 1)
        def _():
            for j in range(CONV_WIDTH):
                ddw_ref[j:j + 1, :] = jnp.sum(wacc[j], axis=0, keepdims=True)
            ddw_ref[CONV_WIDTH:, :] = jnp.zeros((HALO - CONV_WIDTH, c), F32)

    return pl.pallas_call(
        body, name=name, grid=(nsteps,),
        in_specs=[pl.BlockSpec((tm, 2 * c), lambda i: (i, 0)),
                  pl.BlockSpec((HALO, 2 * c), lambda i: (jnp.maximum(i * (tm // HALO) - 1, 0), 0)),
                  pl.BlockSpec((tm, c), lambda i: (i, 0)),
                  pl.BlockSpec((HALO, c), lambda i: (jnp.minimum((i + 1) * (tm // HALO), s_len // HALO - 1), 0)),
                  pl.BlockSpec((HALO, c), lambda i: (0, 0)),
                  pl.BlockSpec(memory_space=pl.ANY)],
        out_specs=[pl.BlockSpec((tm, 2 * c), lambda i: (i, 0)), pl.BlockSpec((HALO, c), lambda i: (0, 0))],
        out_shape=[jax.ShapeDtypeStruct(du.shape, du.dtype), jax.ShapeDtypeStruct((HALO, c), F32)],
        scratch_shapes=[pltpu.VMEM((tm + HALO, c), F32), pltpu.VMEM((tm + HALO, c), F32),
                        pltpu.VMEM((CONV_WIDTH, 8, c), F32)],
        input_output_aliases={5: 0},
        compiler_params=_params(("arbitrary",)),
    )(u, u, dconv, dconv, dw_w, du)


COL_GQ = 1536 // GDN_W
COL_AQ = 4608 // ATT_W
SHALO = 8


def _softplus(z):
    return jnp.maximum(z, 0.0) + jnp.log1p(jnp.exp(-jnp.abs(z)))


def _short_conv(buf, cw_ref, part, rows, first):
    acc = jnp.zeros((rows, GDN_W), F32)
    for j in range(SHORT_CONV):
        acc = acc + buf[first + j:first + j + rows, :] * cw_ref[j:j + 1, GDN_W * part:GDN_W * (part + 1)]
    return acc


def _gdn_prep_fwd(u, cw, al, dtb, *, name, tm=256):
    s_len = u.shape[0]
    first = SHALO - SHORT_CONV + 1

    def body(uq, uk, uv, pq, pk, pv, uba, cw_ref, al_ref, dtb_ref, qn_ref, kn_ref, vc_ref, bg_ref, buf):
        i = pl.program_id(0)

        def conv(cur, prev, part):
            buf[SHALO:, :] = cur[...]
            buf[:SHALO, :] = jnp.where(i > 0, prev[...], 0.0)
            return _silu(_short_conv(buf, cw_ref, part, tm, first))

        for part, (cur, prev, out, scale) in enumerate(
                ((uq, pq, qn_ref, GDN_D ** -0.5), (uk, pk, kn_ref, 1.0))):
            y = conv(cur, prev, part)
            for h in range(GDN_HEADS):
                hs = slice(GDN_D * h, GDN_D * (h + 1))
                yh = y[:, hs]
                out[:, hs] = yh * (lax.rsqrt(jnp.sum(yh * yh, axis=-1, keepdims=True) + 1e-6) * scale)
        vc_ref[...] = conv(uv, pv, 2)
        ba = uba[...]
        lane = lax.broadcasted_iota(jnp.int32, ba.shape, 1)
        g = -jnp.exp(al_ref[...]) * _softplus(ba + dtb_ref[...])
        bg_ref[...] = jnp.where(lane < GDN_HEADS, _sigmoid(ba), jnp.where(lane < 2 * GDN_HEADS, g, 0.0))

    def cur(col):
        return pl.BlockSpec((tm, GDN_W), lambda i: (i, col))

    def prev(col):
        return pl.BlockSpec((SHALO, GDN_W), lambda i: (jnp.maximum(i * (tm // SHALO) - 1, 0), col))

    vec = pl.BlockSpec((1, 128), lambda i: (0, 0))
    row = pl.BlockSpec((tm, GDN_W), lambda i: (i, 0))
    wide = jax.ShapeDtypeStruct((s_len, GDN_W), F32)
    return pl.pallas_call(
        body, name=name, grid=(s_len // tm,),
        in_specs=[cur(COL_GQ), cur(COL_GQ + 1), cur(COL_GQ + 2), prev(COL_GQ), prev(COL_GQ + 1), prev(COL_GQ + 2),
                  pl.BlockSpec((tm, 128), lambda i: (i, COL_BA // 128)),
                  pl.BlockSpec((SHALO, 3 * GDN_W), lambda i: (0, 0)), vec, vec],
        out_specs=[row, row, row, pl.BlockSpec((tm, 128), lambda i: (i, 0))],
        out_shape=[wide, wide, wide, jax.ShapeDtypeStruct((s_len, 128), F32)],
        scratch_shapes=[pltpu.VMEM((tm + SHALO, GDN_W), F32)],
        compiler_params=_params(("parallel",)),
    )(u, u, u, u, u, u, u, cw, al, dtb)


def _chunk_masks():
    c = GDN_CHUNK
    row = lax.broadcasted_iota(jnp.int32, (c, c), 0)
    col = lax.broadcasted_iota(jnp.int32, (c, c), 1)
    return row >= col, row > col


def _cum_decay(bg):
    c = GDN_CHUNK
    causal, _ = _chunk_masks()
    g_cum = _nn(causal.astype(F32), bg, HIGHEST)
    sel = (lax.broadcasted_iota(jnp.int32, (8, 128), 0) + GDN_HEADS
           == lax.broadcasted_iota(jnp.int32, (8, 128), 1)).astype(F32)
    return g_cum, _nt(sel, g_cum, HIGHEST)


def _bdot(a, b, ca, cb):
    return lax.dot_general(a, b, (((ca,), (cb,)), ((0,), (0,))), preferred_element_type=F32)


def _bnn(a, b):
    return _bdot(a, b, 2, 1)


def _bnt(a, b):
    return _bdot(a, b, 2, 2)


def _btn(a, b):
    return _bdot(a, b, 1, 1)


def _split(a):
    hi = a.astype(BF16)
    return hi, (a - hi.astype(F32)).astype(BF16)


def _bnn3(a, b):
    ah, al = _split(a)
    bh, bl = _split(b)
    return _bnn(ah, bh) + (_bnn(al, bh) + _bnn(ah, bl))


def _heads(ref):
    return jnp.stack([ref[:, GDN_D * h:GDN_D * (h + 1)] for h in range(GDN_HEADS)])


def _head_columns(a, first):
    return jnp.stack([a[:, first + h:first + h + 1] for h in range(GDN_HEADS)])


def _chunk_decay(g_cum, g_rows, bg):
    causal, _ = _chunk_masks()
    gc = _head_columns(g_cum, GDN_HEADS)
    gr = jnp.stack([g_rows[h:h + 1, :] for h in range(GDN_HEADS)])
    dec = jnp.where(causal, jnp.exp(jnp.where(causal, gc - gr, 0.0)), 0.0)
    return gc, _head_columns(bg, 0), dec


def _gdn_intra_fwd(qn, kn, vc, bg, *, name):
    s_len = qn.shape[0]
    c = GDN_CHUNK
    nchunks = s_len // c

    def body(q_ref, k_ref, v_ref, bg_ref, wk_ref, wv_ref, qd_ref, kd_ref, p_ref, t_ref, g_ref):
        causal, strict = _chunk_masks()
        eye = (lax.broadcasted_iota(jnp.int32, (c, c), 0) == lax.broadcasted_iota(jnp.int32, (c, c), 1)).astype(F32)
        bg = bg_ref[...]
        g_cum, g_rows = _cum_decay(bg)
        g_ref[...] = g_cum
        gc, bc, dec = _chunk_decay(g_cum, g_rows, bg)
        q, k, v = _heads(q_ref), _heads(k_ref), _heads(v_ref)
        k16 = k.astype(BF16)
        low = jnp.where(strict, bc * _bnt(k16, k16) * dec, 0.0)
        pw = -low
        t = eye + pw
        for _ in range(5):
            pw = _bnn3(pw, pw)
            t = t + _bnn3(t, pw)
        t_ref[...] = t
        t16 = t.astype(BF16)
        eg = jnp.exp(gc)
        wk = _bnn(t16, (k * (bc * eg)).astype(BF16))
        wv = _bnn(t16, (v * bc).astype(BF16))
        p_ref[...] = jnp.where(causal, _bnt(q.astype(BF16), k16) * dec, 0.0).astype(BF16)
        qd = q * eg
        kd = k * jnp.exp(gc[:, c - 1:c, :] - gc)
        for h in range(GDN_HEADS):
            hs = slice(GDN_D * h, GDN_D * (h + 1))
            wk_ref[:, hs] = wk[h].astype(BF16)
            wv_ref[:, hs] = wv[h]
            qd_ref[:, hs] = qd[h].astype(BF16)
            kd_ref[:, hs] = kd[h].astype(BF16)

    row = pl.BlockSpec((c, GDN_W), lambda n: (n, 0))
    sq = pl.BlockSpec((GDN_HEADS, c, c), lambda n: (0, n, 0))
    narrow = pl.BlockSpec((c, 128), lambda n: (n, 0))
    w16 = jax.ShapeDtypeStruct((s_len, GDN_W), BF16)
    return pl.pallas_call(
        body, name=name, grid=(nchunks,), in_specs=[row, row, row, narrow],
        out_specs=[row, row, row, row, sq, sq, narrow],
        out_shape=[w16, jax.ShapeDtypeStruct((s_len, GDN_W), F32), w16, w16,
                   jax.ShapeDtypeStruct((GDN_HEADS, s_len, c), BF16),
                   jax.ShapeDtypeStruct((GDN_HEADS, s_len, c), F32),
                   jax.ShapeDtypeStruct((s_len, 128), F32)],
        compiler_params=_params(("parallel",)),
    )(qn, kn, vc, bg)


def _gdn_scan_fwd(wk, wv, qd, kd, p, g_cum, *, name):
    s_len = wk.shape[0]
    c = GDN_CHUNK
    nchunks = s_len // c

    def body(wk_ref, wv_ref, qd_ref, kd_ref, p_ref, g_ref, o_ref, vn_ref, sp_ref, st):
        @pl.when(pl.program_id(0) == 0)
        def _():
            st[...] = jnp.zeros_like(st)

        s = st[...]
        sp_ref[0] = s
        s16 = s.astype(BF16)
        vn16 = (_heads(wv_ref) - _bnn(_heads(wk_ref), s16)).astype(BF16)
        o = _bnn(_heads(qd_ref), s16) + _bnn(p_ref[...], vn16)
        gl = jnp.exp(_head_columns(g_ref[c - 1:c, :], GDN_HEADS))
        st[...] = s * gl + _btn(_heads(kd_ref), vn16)
        for h in range(GDN_HEADS):
            hs = slice(GDN_D * h, GDN_D * (h + 1))
            vn_ref[:, hs] = vn16[h]
            o_ref[:, hs] = o[h]

    row = pl.BlockSpec((c, GDN_W), lambda n: (n, 0))
    return pl.pallas_call(
        body, name=name, grid=(nchunks,),
        in_specs=[row, row, row, row, pl.BlockSpec((GDN_HEADS, c, c), lambda n: (0, n, 0)),
                  pl.BlockSpec((c, 128), lambda n: (n, 0))],
        out_specs=[row, row, pl.BlockSpec((1, GDN_HEADS, GDN_D, GDN_D), lambda n: (n, 0, 0, 0))],
        out_shape=[jax.ShapeDtypeStruct((s_len, GDN_W), F32), jax.ShapeDtypeStruct((s_len, GDN_W), BF16),
                   jax.ShapeDtypeStruct((nchunks, GDN_HEADS, GDN_D, GDN_D), F32)],
        scratch_shapes=[pltpu.VMEM((GDN_HEADS, GDN_D, GDN_D), F32)],
        compiler_params=_params(("arbitrary",)),
    )(wk, wv, qd, kd, p, g_cum)


def _gdn_scan_bwd(do, wk, qd, kd, p, g_cum, *, name):
    s_len = wk.shape[0]
    c = GDN_CHUNK
    nchunks = s_len // c

    def body(do_ref, wk_ref, qd_ref, kd_ref, p_ref, g_ref, dvn_ref, ds_ref, dst):
        @pl.when(pl.program_id(0) == 0)
        def _():
            dst[...] = jnp.zeros_like(dst)

        ds = dst[...]
        ds_ref[0] = ds
        do16 = _heads(do_ref).astype(BF16)
        dvn16 = (_btn(p_ref[...], do16) + _bnn(_heads(kd_ref), ds.astype(BF16))).astype(BF16)
        gl = jnp.exp(_head_columns(g_ref[c - 1:c, :], GDN_HEADS))
        dst[...] = _btn(_heads(qd_ref), do16) + ds * gl - _btn(_heads(wk_ref), dvn16)
        for h in range(GDN_HEADS):
            dvn_ref[:, GDN_D * h:GDN_D * (h + 1)] = dvn16[h]

    row = pl.BlockSpec((c, GDN_W), lambda n: (nchunks - 1 - n, 0))
    return pl.pallas_call(
        body, name=name, grid=(nchunks,),
        in_specs=[row, row, row, row, pl.BlockSpec((GDN_HEADS, c, c), lambda n: (0, nchunks - 1 - n, 0)),
                  pl.BlockSpec((c, 128), lambda n: (nchunks - 1 - n, 0))],
        out_specs=[row, pl.BlockSpec((1, GDN_HEADS, GDN_D, GDN_D), lambda n: (nchunks - 1 - n, 0, 0, 0))],
        out_shape=[jax.ShapeDtypeStruct((s_len, GDN_W), BF16),
                   jax.ShapeDtypeStruct((nchunks, GDN_HEADS, GDN_D, GDN_D), F32)],
        scratch_shapes=[pltpu.VMEM((GDN_HEADS, GDN_D, GDN_D), F32)],
        compiler_params=_params(("arbitrary",)),
    )(do, wk, qd, kd, p, g_cum)


def _gdn_intra_bwd(qn, kn, vc, bg, g_cum, t, do, dvn, vn, sprev, ds_all, *, name):
    s_len = qn.shape[0]
    c = GDN_CHUNK
    nchunks = s_len // c

    def body(q_ref, k_ref, v_ref, bg_ref, g_ref, t_ref, do_ref, dvn_ref, vn_ref, sp_ref, ds_ref,
             dqkv_ref, dbg_ref):
        causal, strict = _chunk_masks()
        bg = bg_ref[...]
        g_cum = g_ref[...]
        _, g_rows = _cum_decay(bg)
        lane = lax.broadcasted_iota(jnp.int32, (c, 128), 1)
        rowi = lax.broadcasted_iota(jnp.int32, (c, 128), 0)
        gc, bc, dec = _chunk_decay(g_cum, g_rows, bg)
        q, k, v = _heads(q_ref), _heads(k_ref), _heads(v_ref)
        q16, k16 = q.astype(BF16), k.astype(BF16)
        kk = _bnt(k16, k16)
        low = jnp.where(strict, bc * kk * dec, 0.0)
        eg = jnp.exp(gc)
        g_last = gc[:, c - 1:c, :]
        kdec = jnp.exp(g_last - gc)
        kb, vb, qd, kd = k * (bc * eg), v * bc, q * eg, k * kdec
        pm = jnp.where(causal, _bnt(q16, k16) * dec, 0.0)
        s = sp_ref[0]
        ds = ds_ref[0]
        s16, ds16 = s.astype(BF16), ds.astype(BF16)
        do16 = _heads(do_ref).astype(BF16)
        dvn16, vn16 = _heads(dvn_ref), _heads(vn_ref)
        tm = t_ref[...]
        t16 = tm.astype(BF16)

        dqd = _bnt(do16, s16)
        dp = jnp.where(causal, _bnt(do16, vn16), 0.0)
        dkd = _bnt(vn16, ds16)
        dgl = jnp.sum(jnp.sum(s * ds, axis=2, keepdims=True), axis=1, keepdims=True) * jnp.exp(g_last)
        dwk16 = (-_bnt(dvn16, s16)).astype(BF16)
        dt = _bnt(dwk16, kb.astype(BF16)) + _bnt(dvn16, vb.astype(BF16))
        dkb = _btn(t16, dwk16)
        dvb = _btn(t16, dvn16)
        th, tl = _split(tm)
        dth, dtl = _split(dt)
        xm = _btn(th, dth) + (_btn(tl, dth) + _btn(th, dtl))
        xh, xl = _split(xm)
        dlow = jnp.where(strict, -(_bnt(xh, th) + (_bnt(xl, th) + _bnt(xh, tl))), 0.0)
        dkk16 = (dlow * bc * dec).astype(BF16)
        dqk16 = (dp * dec).astype(BF16)

        dq = _bnn(dqk16, k16) + dqd * eg
        dk = _btn(dqk16, q16) + _bnn(dkk16, k16) + _btn(dkk16, k16) + dkb * (bc * eg) + dkd * kdec
        dv = dvb * bc
        for h in range(GDN_HEADS):
            hs = slice(GDN_D * h, GDN_D * (h + 1))
            dqkv_ref[0, :, hs] = dq[h]
            dqkv_ref[1, :, hs] = dk[h]
            dqkv_ref[2, :, hs] = dv[h]

        dbeta = (jnp.sum(dlow * kk * dec, axis=2, keepdims=True)
                 + jnp.sum(dkb * k, axis=2, keepdims=True) * eg + jnp.sum(dvb * v, axis=2, keepdims=True))
        mm = dlow * low + dp * pm
        mh, ml = _split(mm)
        ones16 = jnp.ones((GDN_HEADS, c, 128), BF16)
        col_sum = (_btn(mh, ones16) + _btn(ml, ones16))[:, :, 0:1]
        dkd_sum = jnp.sum(dkd * kd, axis=2, keepdims=True)
        dg = (jnp.sum(mm, axis=2, keepdims=True) - col_sum + jnp.sum(dkb * kb, axis=2, keepdims=True)
              + jnp.sum(dqd * qd, axis=2, keepdims=True) - dkd_sum)
        tail = jnp.sum(dkd_sum, axis=1, keepdims=True) + dgl
        dbeta_all = jnp.zeros((c, 128), F32)
        dg_all = jnp.zeros((c, 128), F32)
        for h in range(GDN_HEADS):
            dbeta_all = dbeta_all + jnp.where(lane == h, dbeta[h], 0.0)
            dg_all = dg_all + jnp.where(lane == GDN_HEADS + h, dg[h] + jnp.where(rowi == c - 1, tail[h], 0.0), 0.0)
        upper = (lax.broadcasted_iota(jnp.int32, (c, c), 0) <= lax.broadcasted_iota(jnp.int32, (c, c), 1)).astype(F32)
        dbg_ref[...] = dbeta_all + _nn(upper, dg_all, HIGHEST)

    row = pl.BlockSpec((c, GDN_W), lambda n: (n, 0))
    narrow = pl.BlockSpec((c, 128), lambda n: (n, 0))
    state = pl.BlockSpec((1, GDN_HEADS, GDN_D, GDN_D), lambda n: (n, 0, 0, 0))
    return pl.pallas_call(
        body, name=name, grid=(nchunks,),
        in_specs=[row, row, row, narrow, narrow, pl.BlockSpec((GDN_HEADS, c, c), lambda n: (0, n, 0)),
                  row, row, row, state, state],
        out_specs=[pl.BlockSpec((3, c, GDN_W), lambda n: (0, n, 0)), narrow],
        out_shape=[jax.ShapeDtypeStruct((3, s_len, GDN_W), F32), jax.ShapeDtypeStruct((s_len, 128), F32)],
        compiler_params=_params(("parallel",)),
    )(qn, kn, vc, bg, g_cum, t, do, dvn, vn, sprev, ds_all)


def _gdn_prep_bwd(u, dqkv, cw, du, *, name, tm=256):
    s_len = u.shape[0]
    nsteps = s_len // tm
    ext = tm + SHALO

    def body(uc, up, un, dc, dn, cw_ref, du_in_ref, du_ref, dcw_ref, xbuf, dbuf, pbuf, wacc):
        del du_in_ref
        part = pl.program_id(0)
        i = pl.program_id(1)

        @pl.when(i == 0)
        def _():
            wacc[...] = jnp.zeros_like(wacc)

        xbuf[:SHALO, :] = jnp.where(i > 0, up[...], 0.0)
        xbuf[SHALO:SHALO + tm, :] = uc[...]
        xbuf[SHALO + tm:, :] = jnp.where(i < nsteps - 1, un[...], 0.0)
        dbuf[:tm, :] = dc[...]
        dbuf[tm:, :] = jnp.where(i < nsteps - 1, dn[...], 0.0)
        first = SHALO - SHORT_CONV + 1
        w = [cw_ref[j:j + 1, :] for j in range(SHORT_CONV)]
        pre = jnp.zeros((ext, GDN_W), F32)
        for j in range(SHORT_CONV):
            pre = pre + xbuf[first + j:first + j + ext, :] * w[j]
        y = _silu(pre)
        dout = dbuf[...]
        scale = jnp.where(part == 0, GDN_D ** -0.5, 1.0)
        for h in range(GDN_HEADS):
            hs = slice(GDN_D * h, GDN_D * (h + 1))
            yh, dh = y[:, hs], dout[:, hs]
            rs = lax.rsqrt(jnp.sum(yh * yh, axis=-1, keepdims=True) + 1e-6)
            dyn = scale * rs * (dh - yh * (rs * rs) * jnp.sum(dh * yh, axis=-1, keepdims=True))
            dy = jnp.where(part < 2, dyn, dh)
            pbuf[:, hs] = dy * _dsilu(pre[:, hs])
        acc = jnp.zeros((tm, GDN_W), F32)
        dpre = pbuf[0:tm, :]
        for j in range(SHORT_CONV):
            k = SHORT_CONV - 1 - j
            acc = acc + pbuf[k:k + tm, :] * w[j]
            wacc[j] += (xbuf[first + j:first + j + tm, :] * dpre).reshape(tm // 8, 8, GDN_W).sum(axis=0)
        du_ref[...] = acc.astype(du_ref.dtype)

        @pl.when(i == nsteps - 1)
        def _():
            for j in range(SHORT_CONV):
                dcw_ref[j:j + 1, :] = jnp.sum(wacc[j], axis=0, keepdims=True)
            dcw_ref[SHORT_CONV:, :] = jnp.zeros((SHALO - SHORT_CONV, GDN_W), F32)

    per = tm // SHALO
    return pl.pallas_call(
        body, name=name, grid=(3, nsteps),
        in_specs=[pl.BlockSpec((tm, GDN_W), lambda p, i: (i, COL_GQ + p)),
                  pl.BlockSpec((SHALO, GDN_W), lambda p, i: (jnp.maximum(i * per - 1, 0), COL_GQ + p)),
                  pl.BlockSpec((SHALO, GDN_W), lambda p, i: (jnp.minimum((i + 1) * per, s_len // SHALO - 1), COL_GQ + p)),
                  pl.BlockSpec((None, tm, GDN_W), lambda p, i: (p, i, 0)),
                  pl.BlockSpec((None, SHALO, GDN_W), lambda p, i: (p, jnp.minimum((i + 1) * per, s_len // SHALO - 1), 0)),
                  pl.BlockSpec((SHALO, GDN_W), lambda p, i: (0, p)),
                  pl.BlockSpec(memory_space=pl.ANY)],
        out_specs=[pl.BlockSpec((tm, GDN_W), lambda p, i: (i, COL_GQ + p)),
                   pl.BlockSpec((SHALO, GDN_W), lambda p, i: (0, p))],
        out_shape=[jax.ShapeDtypeStruct(du.shape, du.dtype), jax.ShapeDtypeStruct((SHALO, 3 * GDN_W), F32)],
        scratch_shapes=[pltpu.VMEM((tm + 2 * SHALO, GDN_W), F32), pltpu.VMEM((ext, GDN_W), F32),
                        pltpu.VMEM((ext, GDN_W), F32), pltpu.VMEM((SHORT_CONV, 8, GDN_W), F32)],
        input_output_aliases={6: 0},
        compiler_params=_params(("arbitrary", "arbitrary")),
    )(u, u, u, dqkv, dqkv, cw, du)


def _gdn_ba_bwd(u, dbg, al, dtb, du, *, name, tm=256):
    s_len = u.shape[0]
    nsteps = s_len // tm
    wpad = IN_WP - COL_BA

    def body(uba, dbg_ref, al_ref, dtb_ref, du_in_ref, du_ref, sums_ref):
        del du_in_ref
        i = pl.program_id(0)

        @pl.when(i == 0)
        def _():
            sums_ref[...] = jnp.zeros_like(sums_ref)

        ba = uba[...]
        dbg = dbg_ref[...]
        lane = lax.broadcasted_iota(jnp.int32, ba.shape, 1)
        is_g = (lane >= GDN_HEADS) & (lane < 2 * GDN_HEADS)
        beta = _sigmoid(ba)
        z = ba + dtb_ref[...]
        ea = jnp.exp(al_ref[...])
        g = -ea * _softplus(z)
        dz = jnp.where(is_g, dbg * (-ea) * _sigmoid(z), 0.0)
        du_ref[:, :128] = jnp.where(lane < GDN_HEADS, dbg * beta * (1.0 - beta), dz).astype(du_ref.dtype)
        du_ref[:, 128:] = jnp.zeros((tm, wpad - 128), du_ref.dtype)
        sums_ref[0:1, :] += jnp.sum(jnp.where(is_g, dbg * g, 0.0), axis=0, keepdims=True)
        sums_ref[1:2, :] += jnp.sum(dz, axis=0, keepdims=True)

    vec = pl.BlockSpec((1, 128), lambda i: (0, 0))
    return pl.pallas_call(
        body, name=name, grid=(nsteps,),
        in_specs=[pl.BlockSpec((tm, 128), lambda i: (i, COL_BA // 128)), pl.BlockSpec((tm, 128), lambda i: (i, 0)),
                  vec, vec, pl.BlockSpec(memory_space=pl.ANY)],
        out_specs=[pl.BlockSpec((tm, wpad), lambda i: (i, COL_BA // wpad)), pl.BlockSpec((8, 128), lambda i: (0, 0))],
        out_shape=[jax.ShapeDtypeStruct(du.shape, du.dtype), jax.ShapeDtypeStruct((8, 128), F32)],
        input_output_aliases={4: 0},
        compiler_params=_params(("arbitrary",)),
    )(u, dbg, al, dtb, du)


def _rope_tables(s_len):
    half = ROPE_DIM // 2
    inv = ROPE_THETA ** (-jnp.arange(half, dtype=F32) / half)
    ang = jnp.arange(s_len, dtype=F32)[:, None] * inv[None, :]
    cos, sin = jnp.cos(ang), jnp.sin(ang)
    one = jnp.ones((s_len, ATT_HD - ROPE_DIM), F32)
    zero = jnp.zeros((s_len, ATT_HD - ROPE_DIM), F32)
    zh = jnp.zeros((s_len, half), F32)
    c = jnp.concatenate([cos, cos, one], axis=1)
    s1 = jnp.concatenate([-sin, zh, zero], axis=1)
    s2 = jnp.concatenate([zh, sin, zero], axis=1)
    return tuple(jnp.concatenate([t, t], axis=1) for t in (c, s1, s2))


def _rope(x, c, s1, s2):
    return x * c + pltpu.roll(x, 128 - ROPE_DIM // 2, 1) * s1 + pltpu.roll(x, ROPE_DIM // 2, 1) * s2


def _rope_t(dy, c, s1, s2):
    return dy * c + pltpu.roll(dy * s1, ROPE_DIM // 2, 1) + pltpu.roll(dy * s2, 128 - ROPE_DIM // 2, 1)


DILATIONS = tuple(d for _, d in DIL_PATTERNS)
VIEW_ROWS = 256


def _to_view(scr, out_ref, dil, dtype):
    nblk, rows, _ = scr.shape
    width = nblk * 128
    for b in range(nblk):
        if dil == 1:
            out_ref[:, 128 * b:128 * (b + 1)] = scr[b].astype(dtype)
            continue
        for r in range(dil):
            out_ref[:, r * width + 128 * b:r * width + 128 * (b + 1)] = (
                scr.at[b][pl.ds(r, rows // dil, stride=dil), :].astype(dtype))


def _from_view(in_ref, scr, dil):
    nblk, rows, _ = scr.shape
    width = nblk * 128
    for b in range(nblk):
        for r in range(dil):
            scr.at[b][pl.ds(r, rows // dil, stride=dil), :] = in_ref[:, r * width + 128 * b:r * width + 128 * (b + 1)]


def _view_spec(dil, width, tm=VIEW_ROWS):
    return pl.BlockSpec((tm // dil, dil * width), lambda i: (i, 0))


def _view_shape(s_len, dil, width, dtype):
    return jax.ShapeDtypeStruct((s_len // dil, dil * width), dtype)


def _att_prep_fwd(u, tabs, *, name):
    s_len = u.shape[0]
    tm = VIEW_ROWS
    scale = ATT_HD ** -0.5
    nblk = ATT_W // 128

    def body(uq, uk, uv, c_ref, s1_ref, s2_ref, *rest):
        outs, scr = rest[:-1], rest[-1]
        c, s1, s2 = c_ref[...], s1_ref[...], s2_ref[...]
        for part, src in enumerate((uq, uk, uv)):
            for b in range(nblk):
                xb = src[:, 128 * b:128 * (b + 1)]
                if part == 0:
                    xb = _rope(xb, c, s1, s2) * scale
                elif part == 1:
                    xb = _rope(xb, c, s1, s2)
                scr[b] = xb
            for gi, dil in enumerate(DILATIONS):
                _to_view(scr, outs[3 * gi + part], dil, BF16)

    tab = pl.BlockSpec((tm, 128), lambda i: (i, 0))
    outs = pl.pallas_call(
        body, name=name, grid=(s_len // tm,),
        in_specs=[pl.BlockSpec((tm, ATT_W), lambda i, col=COL_AQ + j: (i, col)) for j in range(3)] + [tab] * 3,
        out_specs=[_view_spec(dil, ATT_W) for dil in DILATIONS for _ in range(3)],
        out_shape=[_view_shape(s_len, dil, ATT_W, BF16) for dil in DILATIONS for _ in range(3)],
        scratch_shapes=[pltpu.VMEM((nblk, tm, 128), F32)],
        compiler_params=_params(("parallel",)),
    )(u, u, u, *tabs)
    return [outs[3 * gi:3 * gi + 3] for gi in range(len(DILATIONS))]


def _stack_heads(x):
    lane = lax.broadcasted_iota(jnp.int32, (1, 128), 1)
    zero = jnp.zeros_like(x)
    return jnp.concatenate([jnp.where(lane < ATT_HD, x, zero), jnp.where(lane >= ATT_HD, x, zero)], axis=0)


def _att_fwd(qr, kr, vb, dil, *, name):
    lr = qr.shape[0]
    nb = lr // ATT_BLOCK
    blk = ATT_BLOCK

    def body(q_ref, kp_ref, kc_ref, vp_ref, vc_ref, o_ref, lse_ref):
        n = pl.program_id(1)
        qi = lax.broadcasted_iota(jnp.int32, (blk, 2 * blk), 0)
        ki = lax.broadcasted_iota(jnp.int32, (blk, 2 * blk), 1)
        dist = qi + blk - ki
        valid = (dist >= 0) & (dist <= blk) & ((ki >= blk) | (n > 0))
        valid = jnp.concatenate([valid, valid], axis=0)
        lane = lax.broadcasted_iota(jnp.int32, (blk, 128), 1)
        lse_all = jnp.zeros((blk, 128), F32)
        for hp in range(ATT_HEADS // 2):
            bs = slice(128 * hp, 128 * (hp + 1))
            kb = jnp.concatenate([kp_ref[:, bs], kc_ref[:, bs]], axis=0)
            vv = jnp.concatenate([vp_ref[:, bs], vc_ref[:, bs]], axis=0)
            s = jnp.where(valid, _nt(_stack_heads(q_ref[:, bs]), kb), NEG_INF)
            m = jnp.max(s, axis=-1, keepdims=True)
            p = jnp.exp(s - m)
            l = jnp.sum(p, axis=-1, keepdims=True)
            o = _nn((p * (1.0 / l)).astype(BF16), vv)
            o_ref[:, bs] = jnp.where(lane < ATT_HD, o[:blk], o[blk:])
            lse = m + jnp.log(l)
            lse_all = (lse_all + jnp.where(lane == 2 * hp, lse[:blk], 0.0)
                       + jnp.where(lane == 2 * hp + 1, lse[blk:], 0.0))
        lse_ref[...] = lse_all

    cur = pl.BlockSpec((blk, ATT_W), lambda r, n: (n, r))
    prev = pl.BlockSpec((blk, ATT_W), lambda r, n: (jnp.maximum(n - 1, 0), r))
    return pl.pallas_call(
        body, name=name, grid=(dil, nb), in_specs=[cur, prev, cur, prev, cur],
        out_specs=[cur, pl.BlockSpec((blk, 128), lambda r, n: (n, r))],
        out_shape=[jax.ShapeDtypeStruct(qr.shape, F32), jax.ShapeDtypeStruct((lr, dil * 128), F32)],
        compiler_params=_params(("parallel", "parallel")),
    )(qr, kr, kr, vb, vb)


def _att_bwd(qr, kr, vb, do, lse, delta, dil, *, name):
    lr = qr.shape[0]
    nb = lr // ATT_BLOCK
    blk = ATT_BLOCK

    def body(q_ref, kp_ref, kc_ref, vp_ref, vc_ref, do_ref, lse_ref, dl_ref, dq_ref, dk_ref, dv_ref, carry):
        n = pl.program_id(1)

        @pl.when(n == 0)
        def _():
            carry[...] = jnp.zeros_like(carry)

        @pl.when(n == nb)
        def _():
            dk_ref[...] = carry[0]
            dv_ref[...] = carry[1]

        @pl.when(n < nb)
        def _():
            qi = lax.broadcasted_iota(jnp.int32, (blk, 2 * blk), 0)
            ki = lax.broadcasted_iota(jnp.int32, (blk, 2 * blk), 1)
            dist = qi + blk - ki
            valid = (dist >= 0) & (dist <= blk) & ((ki >= blk) | (n > 0))
            valid = jnp.concatenate([valid, valid], axis=0)
            lane = lax.broadcasted_iota(jnp.int32, (blk, 128), 1)
            for hp in range(ATT_HEADS // 2):
                bs = slice(128 * hp, 128 * (hp + 1))
                kb = jnp.concatenate([kp_ref[:, bs], kc_ref[:, bs]], axis=0)
                vv = jnp.concatenate([vp_ref[:, bs], vc_ref[:, bs]], axis=0)
                q2 = _stack_heads(q_ref[:, bs])
                do2 = _stack_heads(do_ref[:, bs])
                lse2 = jnp.concatenate([lse_ref[:, 2 * hp:2 * hp + 1], lse_ref[:, 2 * hp + 1:2 * hp + 2]], axis=0)
                dl2 = jnp.concatenate([dl_ref[:, 2 * hp:2 * hp + 1], dl_ref[:, 2 * hp + 1:2 * hp + 2]], axis=0)
                p = jnp.where(valid, jnp.exp(_nt(q2, kb) - lse2), 0.0)
                ds16 = (p * (_nt(do2, vv) - dl2)).astype(BF16)
                dq2 = _nn(ds16, kb)
                dv_acc = _tn(p.astype(BF16), do2)
                dk_acc = _tn(ds16, q2)
                dq_ref[:, bs] = jnp.where(lane < ATT_HD, dq2[:blk], dq2[blk:])
                dk_ref[:, bs] = carry[0, :, bs] + dk_acc[:blk]
                dv_ref[:, bs] = carry[1, :, bs] + dv_acc[:blk]
                carry[0, :, bs] = dk_acc[blk:]
                carry[1, :, bs] = dv_acc[blk:]

    def at(n):
        return jnp.minimum(n, nb - 1)

    cur = pl.BlockSpec((blk, ATT_W), lambda r, n: (at(n), r))
    prev = pl.BlockSpec((blk, ATT_W), lambda r, n: (jnp.maximum(at(n) - 1, 0), r))
    nar = pl.BlockSpec((blk, 128), lambda r, n: (at(n), r))
    late = pl.BlockSpec((blk, ATT_W), lambda r, n: (jnp.maximum(n - 1, 0), r))
    out = jax.ShapeDtypeStruct(qr.shape, F32)
    return pl.pallas_call(
        body, name=name, grid=(dil, nb + 1), in_specs=[cur, prev, cur, prev, cur, cur, nar, nar],
        out_specs=[cur, late, late], out_shape=[out, out, out],
        scratch_shapes=[pltpu.VMEM((2, blk, ATT_W), F32)],
        compiler_params=_params(("parallel", "arbitrary")),
    )(qr, kr, kr, vb, vb, do, lse, delta)


def _att_prep_bwd(dgroups, tabs, du, *, name):
    s_len = du.shape[0]
    tm = VIEW_ROWS
    scale = ATT_HD ** -0.5
    nblk = ATT_W // 128
    ng = len(DILATIONS)

    def body(*refs):
        grads = refs[:3 * ng]
        c_ref, s1_ref, s2_ref, _, du_ref = refs[3 * ng:3 * ng + 5]
        scrs = refs[3 * ng + 5:]
        c, s1, s2 = c_ref[...], s1_ref[...], s2_ref[...]
        for part in range(3):
            for gi, dil in enumerate(DILATIONS):
                if dil > 1:
                    _from_view(grads[3 * gi + part], scrs[gi], dil)
            for b in range(nblk):
                tot = None
                for gi, dil in enumerate(DILATIONS):
                    term = grads[3 * gi + part][:, 128 * b:128 * (b + 1)] if dil == 1 else scrs[gi][b]
                    tot = term if tot is None else tot + term
                if part == 0:
                    tot = _rope_t(tot * scale, c, s1, s2)
                elif part == 1:
                    tot = _rope_t(tot, c, s1, s2)
                du_ref[:, ATT_W * part + 128 * b:ATT_W * part + 128 * (b + 1)] = tot.astype(du_ref.dtype)

    tab = pl.BlockSpec((tm, 128), lambda i: (i, 0))
    return pl.pallas_call(
        body, name=name, grid=(s_len // tm,),
        in_specs=[_view_spec(dil, ATT_W) for dil in DILATIONS for _ in range(3)] + [tab] * 3
        + [pl.BlockSpec(memory_space=pl.ANY)],
        out_specs=pl.BlockSpec((tm, 3 * ATT_W), lambda i: (i, COL_AQ // 3)),
        out_shape=jax.ShapeDtypeStruct(du.shape, du.dtype),
        scratch_shapes=[pltpu.VMEM((nblk, tm, 128), F32) for _ in DILATIONS],
        input_output_aliases={3 * ng + 3: 0},
        compiler_params=_params(("parallel",)),
    )(*[a for g in dgroups for a in g], *tabs, du)


def _head_weights(w, b):
    lane = lax.broadcasted_iota(jnp.int32, (1, 128), 1)
    return jnp.where(lane < ATT_HD, w[:, 2 * b:2 * b + 1], w[:, 2 * b + 1:2 * b + 2])


def _assemble_fwd(pw, u, o_gdn, gnw, o_groups, lse_groups, *, name):
    s_len = u.shape[0]
    tm = VIEW_ROWS
    c = CONV_CH
    nblk = ATT_W // 128
    ng = len(DILATIONS)

    def body(*refs):
        pw_ref, cg_ref, z_ref, ag_ref, og_ref, gnw_ref = refs[:6]
        o_refs, l_refs = refs[6:6 + ng], refs[6 + ng:6 + 2 * ng]
        y_ref, oa_ref = refs[6 + 2 * ng:8 + 2 * ng]
        lse_outs = refs[8 + 2 * ng:8 + 3 * ng]
        o_scr, l_scr = refs[8 + 3 * ng:8 + 4 * ng], refs[8 + 4 * ng:8 + 5 * ng]
        lse_scr = refs[8 + 5 * ng]
        y_ref[:, :c] = (pw_ref[...] * _silu(cg_ref[...])).astype(BF16)
        gw = gnw_ref[...]
        for h in range(GDN_HEADS):
            hs = slice(GDN_D * h, GDN_D * (h + 1))
            oh = og_ref[:, hs]
            yn = oh * lax.rsqrt(jnp.mean(oh * oh, axis=-1, keepdims=True) + 1e-6) * gw
            y_ref[:, c + GDN_D * h:c + GDN_D * (h + 1)] = (yn * _silu(z_ref[:, hs])).astype(BF16)
        for gi, dil in enumerate(DILATIONS):
            if dil > 1:
                _from_view(o_refs[gi], o_scr[gi], dil)
                _from_view(l_refs[gi], l_scr[gi], dil)
        ls = [l_refs[gi][...] if dil == 1 else l_scr[gi][0] for gi, dil in enumerate(DILATIONS)]
        m = functools.reduce(jnp.maximum, ls)
        es = [jnp.exp(l - m) for l in ls]
        den = functools.reduce(lambda a, b: a + b, es)
        lse_scr[0] = m + jnp.log(den)
        ws = [e / den for e in es]
        for b in range(nblk):
            bs = slice(128 * b, 128 * (b + 1))
            o = None
            for gi, dil in enumerate(DILATIONS):
                term = _head_weights(ws[gi], b) * (o_refs[gi][:, bs] if dil == 1 else o_scr[gi][b])
                o = term if o is None else o + term
            oa_ref[:, bs] = o
            y_ref[:, c + GDN_W + 128 * b:c + GDN_W + 128 * (b + 1)] = (o * _silu(ag_ref[:, bs])).astype(BF16)
        for gi, dil in enumerate(DILATIONS):
            _to_view(lse_scr, lse_outs[gi], dil, F32)

    wide = pl.BlockSpec((tm, 768), lambda i: (i, 0))
    return pl.pallas_call(
        body, name=name, grid=(s_len // tm,),
        in_specs=[pl.BlockSpec((tm, c), lambda i: (i, 0)), pl.BlockSpec((tm, c), lambda i: (i, 1024 // c)),
                  pl.BlockSpec((tm, 768), lambda i: (i, COL_GQ + 3)), pl.BlockSpec((tm, 768), lambda i: (i, COL_AQ + 3)),
                  wide, pl.BlockSpec((1, 128), lambda i: (0, 0))]
        + [_view_spec(dil, ATT_W) for dil in DILATIONS] + [_view_spec(dil, 128) for dil in DILATIONS],
        out_specs=[pl.BlockSpec((tm, D_MODEL), lambda i: (i, 0)), wide] + [_view_spec(dil, 128) for dil in DILATIONS],
        out_shape=[jax.ShapeDtypeStruct((s_len, D_MODEL), BF16), jax.ShapeDtypeStruct((s_len, ATT_W), F32)]
        + [_view_shape(s_len, dil, 128, F32) for dil in DILATIONS],
        scratch_shapes=[pltpu.VMEM((nblk, tm, 128), F32) for _ in DILATIONS]
        + [pltpu.VMEM((1, tm, 128), F32) for _ in DILATIONS] + [pltpu.VMEM((1, tm, 128), F32)],
        compiler_params=_params(("parallel",)),
    )(pw, u, u, u, o_gdn, gnw, *o_groups, *lse_groups)


def _assemble_bwd(dy, pw, u, o_gdn, gnw, o_att, *, name):
    s_len = u.shape[0]
    tm = VIEW_ROWS
    c = CONV_CH
    nsteps = s_len // tm
    nblk = ATT_W // 128
    ng = len(DILATIONS)

    def body(dy_ref, pw_ref, cg_ref, z_ref, ag_ref, og_ref, gnw_ref, oa_ref,
             du_ref, dpw_ref, dog_ref, dgw_ref, *rest):
        do_outs, dl_outs = rest[:ng], rest[ng:2 * ng]
        acc_ref, do_scr, dl_scr = rest[2 * ng:]
        i = pl.program_id(0)

        @pl.when(i == 0)
        def _():
            acc_ref[...] = jnp.zeros_like(acc_ref)

        du_ref[...] = jnp.zeros_like(du_ref)
        dyc = dy_ref[:, :c]
        cg = cg_ref[...]
        dpw_ref[...] = dyc * _silu(cg)
        du_ref[:, 1024:1024 + c] = (dyc * pw_ref[...] * _dsilu(cg)).astype(BF16)
        gw = gnw_ref[...]
        dgw = jnp.zeros((8, 128), F32)
        for h in range(GDN_HEADS):
            hs = slice(GDN_D * h, GDN_D * (h + 1))
            oh = og_ref[:, hs]
            zh = z_ref[:, hs]
            dyh = dy_ref[:, c + GDN_D * h:c + GDN_D * (h + 1)]
            r = lax.rsqrt(jnp.mean(oh * oh, axis=-1, keepdims=True) + 1e-6)
            xn = oh * r
            dyn = dyh * _silu(zh)
            du_ref[:, GDN_W * (COL_GQ + 3) + GDN_D * h:GDN_W * (COL_GQ + 3) + GDN_D * (h + 1)] = (
                dyh * xn * gw * _dsilu(zh)).astype(BF16)
            dgw = dgw + (dyn * xn).reshape(tm // 8, 8, 128).sum(axis=0)
            dxn = dyn * gw
            dog_ref[:, hs] = r * (dxn - xn * jnp.mean(dxn * xn, axis=-1, keepdims=True))
        acc_ref[...] += dgw
        lane = lax.broadcasted_iota(jnp.int32, (tm, 128), 1)
        delta = jnp.zeros((tm, 128), F32)
        for b in range(ATT_W // 128):
            bs = slice(128 * b, 128 * (b + 1))
            dya = dy_ref[:, c + GDN_W + 128 * b:c + GDN_W + 128 * (b + 1)]
            ag = ag_ref[:, bs]
            oa = oa_ref[:, bs]
            do = dya * _silu(ag)
            do_scr[b] = do
            du_ref[:, ATT_W * (COL_AQ + 3) + 128 * b:ATT_W * (COL_AQ + 3) + 128 * (b + 1)] = (
                dya * oa * _dsilu(ag)).astype(BF16)
            prod = do * oa
            lo = jnp.sum(jnp.where(lane < ATT_HD, prod, 0.0), axis=-1, keepdims=True)
            hi = jnp.sum(jnp.where(lane >= ATT_HD, prod, 0.0), axis=-1, keepdims=True)
            delta = delta + jnp.where(lane == 2 * b, lo, 0.0) + jnp.where(lane == 2 * b + 1, hi, 0.0)
        dl_scr[0] = delta
        for gi, dil in enumerate(DILATIONS):
            _to_view(do_scr, do_outs[gi], dil, BF16)
            _to_view(dl_scr, dl_outs[gi], dil, F32)

        @pl.when(i == nsteps - 1)
        def _():
            dgw_ref[...] = jnp.sum(acc_ref[...], axis=0, keepdims=True)

    wide = pl.BlockSpec((tm, 768), lambda i: (i, 0))
    vec = pl.BlockSpec((1, 128), lambda i: (0, 0))
    outs = pl.pallas_call(
        body, name=name, grid=(nsteps,),
        in_specs=[pl.BlockSpec((tm, D_MODEL), lambda i: (i, 0)), pl.BlockSpec((tm, c), lambda i: (i, 0)),
                  pl.BlockSpec((tm, c), lambda i: (i, 1024 // c)), pl.BlockSpec((tm, 768), lambda i: (i, COL_GQ + 3)),
                  pl.BlockSpec((tm, 768), lambda i: (i, COL_AQ + 3)), wide, vec, wide],
        out_specs=[pl.BlockSpec((tm, IN_WP), lambda i: (i, 0)), pl.BlockSpec((tm, c), lambda i: (i, 0)), wide, vec]
        + [_view_spec(dil, ATT_W) for dil in DILATIONS] + [_view_spec(dil, 128) for dil in DILATIONS],
        out_shape=[jax.ShapeDtypeStruct((s_len, IN_WP), BF16), jax.ShapeDtypeStruct((s_len, c), F32),
                   jax.ShapeDtypeStruct((s_len, GDN_W), F32), jax.ShapeDtypeStruct((1, 128), F32)]
        + [_view_shape(s_len, dil, ATT_W, BF16) for dil in DILATIONS]
        + [_view_shape(s_len, dil, 128, F32) for dil in DILATIONS],
        scratch_shapes=[pltpu.VMEM((8, 128), F32), pltpu.VMEM((nblk, tm, 128), F32), pltpu.VMEM((1, tm, 128), F32)],
        compiler_params=_params(("arbitrary",)),
    )(dy, pw, u, u, u, o_gdn, gnw, o_att)
    return outs[:4], outs[4:4 + ng], outs[4 + ng:]


def _layer_fwd(x, p, tabs):
    h = _rms_fwd(x, p["norm_w"], name="rms_fwd")
    u = _matmul(h, p["wp"], name="in_proj", tk=2048)
    conv, sw = _conf_fwd(u, p["dw_w"], p["dw_b"], p["ln_w"], p["ln_b"], name="conf_fwd")
    pw = _matmul(sw, p["pw_w"], name="conf_pw")
    qn, kn, vc, bg = _gdn_prep_fwd(u, p["cw"], p["al"], p["dtb"], name="gdn_prep_fwd")
    wk, wv, qd, kd, pm, t, g_cum = _gdn_intra_fwd(qn, kn, vc, bg, name="gdn_intra_fwd")
    o_gdn, vn, sprev = _gdn_scan_fwd(wk, wv, qd, kd, pm, g_cum, name="gdn_scan_fwd")
    qkv = _att_prep_fwd(u, tabs, name="att_prep_fwd")
    groups = [_att_fwd(*qkv[gi], dil, name=f"att_fwd_d{dil}") for gi, dil in enumerate(DILATIONS)]
    outs = _assemble_fwd(pw, u, o_gdn, p["gnw"], [g[0] for g in groups], [g[1] for g in groups],
                         name="assemble_fwd")
    y, o_att, lse = outs[0], outs[1], outs[2:]
    x_next = _matmul(y, p["wout"], add=x, name="out_proj", tk=2048)
    saved = dict(x=x, h=h, u=u, conv=conv, sw=sw, pw=pw, qn=qn, kn=kn, vc=vc, bg=bg, wk=wk, qd=qd, kd=kd, pm=pm,
                 t=t, g_cum=g_cum, vn=vn, sprev=sprev, o_gdn=o_gdn, qkv=qkv, o_att=o_att, lse=lse, y=y)
    return x_next, saved


def _layer_bwd(dx_out, s, p, tabs, layer, big):
    dy = _matmul(dx_out, p["wout"], tb=True, name="out_proj_dy", tk=2048)
    d_wout = _matmul(s["y"], dx_out, ta=True, name="out_proj_dw", tn=2048, stack=(big[1], layer, DEPTH))
    (du, dpw, dog, dgw), do_views, dl_views = _assemble_bwd(dy, s["pw"], s["u"], s["o_gdn"], p["gnw"], s["o_att"],
                                                            name="assemble_bwd")
    dsw = _matmul(dpw, p["pw_w"], tb=True, name="conf_pw_dx")
    d_pw_w = _matmul(s["sw"], dpw, ta=True, name="conf_pw_dw", stack=(big[2], layer, DEPTH))
    dconv, ln_sums = _conf_bwd_ln(dsw, s["conv"], p["ln_w"], p["ln_b"], name="conf_bwd_ln")
    du, d_dw_w = _conf_bwd_conv(s["u"], dconv, p["dw_w"], du, name="conf_bwd_conv")
    dvn, ds_all = _gdn_scan_bwd(dog, s["wk"], s["qd"], s["kd"], s["pm"], s["g_cum"], name="gdn_scan_bwd")
    dqkv, dbg = _gdn_intra_bwd(s["qn"], s["kn"], s["vc"], s["bg"], s["g_cum"], s["t"], dog, dvn, s["vn"],
                               s["sprev"], ds_all, name="gdn_intra_bwd")
    du, d_cw = _gdn_prep_bwd(s["u"], dqkv, p["cw"], du, name="gdn_prep_bwd")
    du, ba_sums = _gdn_ba_bwd(s["u"], dbg, p["al"], p["dtb"], du, name="gdn_ba_bwd")
    dgroups = []
    for gi, dil in enumerate(DILATIONS):
        args = (*s["qkv"][gi], do_views[gi], s["lse"][gi], dl_views[gi], dil)
        dgroups.append(_att_bwd(*args, name=f"att_bwd_d{dil}"))
    du = _att_prep_bwd(dgroups, tabs, du, name="att_prep_bwd")
    dh = _matmul(du, p["wp"], tb=True, name="in_proj_dx", tn=2048)
    d_wp = _matmul(s["h"], du, ta=True, name="in_proj_dw", tn=2048, stack=(big[0], layer, DEPTH))
    dx, d_norm_w = _rms_bwd(s["x"], dh, p["norm_w"], dx_out, name="rms_bwd")
    small = dict(norm_w=d_norm_w, gnw=dgw, ln_sums=ln_sums, dw_w=d_dw_w, cw=d_cw, ba_sums=ba_sums)
    return dx, (d_wp, d_wout, d_pw_w), small


def _trunk(x, target, params, final_norm_w):
    tabs = _rope_tables(x.shape[0])
    layers = [{k: v[l] for k, v in params.items()} for l in range(DEPTH)]
    saved = []
    for p in layers:
        x, s = _layer_fwd(x, p, tabs)
        saved.append(s)
    dx, d_final, loss = _loss_head(x, final_norm_w, target, name="loss_head")
    big = (None, None, None)
    small = [None] * DEPTH
    for l in reversed(range(DEPTH)):
        dx, big, small[l] = _layer_bwd(dx, saved[l], layers[l], tabs, l, big)
    grads = {k: jnp.stack([sm[k] for sm in small]) for k in small[0]}
    grads.update(wp=big[0], wout=big[1], pw_w=big[2])
    return loss[0, 0], dx, grads, d_final


ANY = pl.BlockSpec(memory_space=pl.ANY)


def _position():
    return lax.axis_index("x"), lax.axis_index("y"), lax.axis_index("c")


def _other_chips(x, y):
    return [(1 - x, y), (x, 1 - y), (1 - x, 1 - y)]


def _gather_chips(shards, *, name):
    n = len(shards)
    kinds = 12

    def body(*refs):
        ins, outs = refs[:n], refs[n:2 * n]
        send, recv = refs[2 * n:]
        x, y, c = _position()
        me, sib = (x, y, c), (x, y, 1 - c)
        xn, yn, dg = (1 - x, y), (x, 1 - y), (1 - x, 1 - y)
        pa, pb = 2 * c, 2 * c + 1

        def copy(k, a, chip, layer, to, src=None):
            dst = outs[a].at[2 * chip[0] + chip[1], pl.ds(layer, 1)]
            return pltpu.make_async_remote_copy(
                src_ref=dst if src is None else src, dst_ref=dst, send_sem=send.at[k * n + a],
                recv_sem=recv.at[k * n + a], device_id=to, device_id_type=MESH)

        def own(k, a, layer, chip):
            return copy(k, a, (x, y), layer, (*chip, c), src=ins[a].at[pl.ds(layer, 1)])

        sends = []
        for a in range(n):
            sends += [own(0, a, pa, xn), own(1, a, pb, yn), own(2, a, pb, xn), own(3, a, pa, yn)]
        for cp in sends:
            cp.start()
        arrivals = [(1, yn, pb, (4, xn)), (0, xn, pa, (5, yn)), (2, xn, pb, None), (3, yn, pa, None),
                    (4, dg, pb, None), (5, dg, pa, None)]
        for a in range(n):
            for j, (k, chip, layer, onward) in enumerate(arrivals):
                copy(k, a, chip, layer, me).wait_recv()
                if onward is not None:
                    cp = copy(onward[0], a, chip, layer, (*onward[1], c))
                    cp.start()
                    sends.append(cp)
                cp = copy(6 + j, a, chip, layer, sib)
                cp.start()
                sends.append(cp)
        for a in range(n):
            for j, (k, chip, layer, onward) in enumerate(arrivals):
                copy(6 + j, a, chip, layer + 2 - 4 * c, me).wait_recv()
        for cp in sends:
            cp.wait_send()

    return pl.pallas_call(
        body, name=name, in_specs=[ANY] * n, out_specs=[ANY] * n,
        out_shape=[jax.ShapeDtypeStruct((4,) + s.shape, s.dtype) for s in shards],
        scratch_shapes=[pltpu.SemaphoreType.DMA((kinds * n,)), pltpu.SemaphoreType.DMA((kinds * n,))],
    )(*shards)


def _to_sibling(arrs, *, name):
    n = len(arrs)

    def body(*refs):
        ins, outs = refs[:n], refs[n:2 * n]
        send, recv = refs[2 * n:]
        x, y, c = _position()
        cps = [pltpu.make_async_remote_copy(src_ref=ins[a], dst_ref=outs[a], send_sem=send.at[a],
                                            recv_sem=recv.at[a], device_id=(x, y, 1 - c), device_id_type=MESH)
               for a in range(n)]
        for cp in cps:
            cp.start()
        for cp in cps:
            cp.wait()

    return pl.pallas_call(
        body, name=name, in_specs=[ANY] * n, out_specs=[ANY] * n,
        out_shape=[jax.ShapeDtypeStruct(a.shape, a.dtype) for a in arrs],
        scratch_shapes=[pltpu.SemaphoreType.DMA((n,)), pltpu.SemaphoreType.DMA((n,))],
    )(*arrs)


def _to_chips(arrs, *, name):
    n = len(arrs)

    def body(*refs):
        ins, outs = refs[:n], refs[n:2 * n]
        send, recv = refs[2 * n:]
        x, y, c = _position()
        cps = [pltpu.make_async_remote_copy(
            src_ref=ins[a].at[2 * chip[0] + chip[1]], dst_ref=outs[a].at[j], send_sem=send.at[j * n + a],
            recv_sem=recv.at[j * n + a], device_id=(*chip, c), device_id_type=MESH)
            for j, chip in enumerate(_other_chips(x, y)) for a in range(n)]
        for cp in cps:
            cp.start()
        for cp in cps:
            cp.wait()

    return pl.pallas_call(
        body, name=name, in_specs=[ANY] * n, out_specs=[ANY] * n,
        out_shape=[jax.ShapeDtypeStruct((3,) + a.shape[1:], a.dtype) for a in arrs],
        scratch_shapes=[pltpu.SemaphoreType.DMA((3 * n,)), pltpu.SemaphoreType.DMA((3 * n,))],
    )(*arrs)


def _join_halves(fulls, *, name):
    n = len(fulls)

    def body(*refs):
        ins, outs = refs[:n], refs[n:2 * n]
        send, recv = refs[2 * n:]
        x, y, c = _position()

        def copy(a, rows):
            return pltpu.make_async_remote_copy(
                src_ref=ins[a].at[rows], dst_ref=outs[a].at[rows], send_sem=send.at[a], recv_sem=recv.at[a],
                device_id=(x, y, 1 - c), device_id_type=MESH)

        cps = [copy(a, pl.ds(2 * c, 2)) for a in range(n)]
        for cp in cps:
            cp.start()
        for a in range(n):
            cps[a].wait_send()
            copy(a, pl.ds(2 * (1 - c), 2)).wait_recv()

    return pl.pallas_call(
        body, name=name, in_specs=[ANY] * n, out_specs=[ANY] * n,
        out_shape=[jax.ShapeDtypeStruct(f.shape, f.dtype) for f in fulls],
        scratch_shapes=[pltpu.SemaphoreType.DMA((n,)), pltpu.SemaphoreType.DMA((n,))],
        input_output_aliases={a: a for a in range(n)},
    )(*fulls)


def _allreduce_small(packed, *, name):
    rows = packed.shape[0]
    ndev = 8

    def body(x_ref, sum_ref, all_ref, send, recv, lsem):
        x, y, c = _position()
        me, sib = (x, y, c), (x, y, 1 - c)
        chips = _other_chips(x, y)

        def blk(px, py, pc):
            return all_ref.at[pl.ds((4 * px + 2 * py + pc) * rows, rows), :]

        def copy(k, block, to, src=None):
            return pltpu.make_async_remote_copy(
                src_ref=blk(*block) if src is None else src, dst_ref=blk(*block), send_sem=send.at[k],
                recv_sem=recv.at[k], device_id=to, device_id_type=MESH)

        mine = pltpu.make_async_copy(x_ref, blk(*me), lsem)
        mine.start()
        first = [copy(0, me, sib, src=x_ref)] + [copy(1 + j, me, (*chip, c), src=x_ref) for j, chip in enumerate(chips)]
        for cp in first:
            cp.start()
        passed = [copy(4 + j, (*chip, c), sib) for j, chip in enumerate(chips)]
        for j, chip in enumerate(chips):
            copy(1 + j, (*chip, c), me).wait_recv()
            passed[j].start()
        copy(0, sib, me).wait_recv()
        for j, chip in enumerate(chips):
            copy(4 + j, (*chip, 1 - c), me).wait_recv()
        for cp in first + passed:
            cp.wait_send()
        mine.wait()
        acc = all_ref[0:rows, :]
        for d in range(1, ndev):
            acc = acc + all_ref[d * rows:(d + 1) * rows, :]
        sum_ref[...] = acc

    vm = pl.BlockSpec(memory_space=pltpu.VMEM)
    return pl.pallas_call(
        body, name=name, in_specs=[vm], out_specs=vm, out_shape=jax.ShapeDtypeStruct((rows, 128), F32),
        scratch_shapes=[pltpu.VMEM((ndev * rows, 128), F32), pltpu.SemaphoreType.DMA((7,)),
                        pltpu.SemaphoreType.DMA((7,)), pltpu.SemaphoreType.DMA],
        compiler_params=pltpu.CompilerParams(vmem_limit_bytes=VMEM_LIMIT),
    )(packed)


def _pack(arrs):
    flat = jnp.concatenate([a.reshape(-1) for a in arrs])
    pad = (-flat.shape[0]) % 1024
    return jnp.pad(flat, (0, pad)).reshape(-1, 128)


def _unpack(packed, shapes):
    flat = packed.reshape(-1)
    out, pos = [], 0
    for s in shapes:
        size = math.prod(s)
        out.append(flat[pos:pos + size].reshape(s))
        pos += size
    return out


def _pad_cols(w):
    zeros = jnp.zeros(w.shape[:-1] + (IN_WP - IN_W,), w.dtype)
    return jnp.concatenate([w[..., :ORIG_BA], w[..., ORIG_ATT:], w[..., ORIG_BA:ORIG_ATT], zeros], axis=-1)


def _chip_cols(j):
    per = IN_W // 4
    lo, hi = j * per, (j + 1) * per
    out = []
    for o0, o1, p0 in ((0, ORIG_BA, 0), (ORIG_BA, ORIG_ATT, COL_BA), (ORIG_ATT, IN_W, ORIG_BA)):
        a, b = max(lo, o0), min(hi, o1)
        if a < b:
            out.append((p0 + a - o0, p0 + b - o0))
    return out


def _shards_to_padded(g):
    pieces = []
    for j in range(4):
        loc = 0
        for p0, p1 in _chip_cols(j):
            pieces.append((p0, g[j][..., loc:loc + p1 - p0]))
            loc += p1 - p0
    pieces.sort(key=lambda t: t[0])
    zeros = jnp.zeros(g.shape[1:-1] + (IN_WP - IN_W,), g.dtype)
    return jnp.concatenate([p for _, p in pieces] + [zeros], axis=-1)


def _padded_to_shards(g, dtype):
    return jnp.stack([jnp.concatenate([g[..., p0:p1] for p0, p1 in _chip_cols(j)], axis=-1).astype(dtype)
                      for j in range(4)])


def _unpad_cols(w):
    n_att = IN_W - ORIG_ATT
    return jnp.concatenate([w[..., :ORIG_BA], w[..., COL_BA:COL_BA + ORIG_ATT - ORIG_BA],
                            w[..., ORIG_BA:ORIG_BA + n_att]], axis=-1)


def _lanes(v, first):
    return jnp.pad(v, ((0, 0), (first, 128 - first - v.shape[1])))[:, None, :]


def _by_chip(g, axis):
    shape = g.shape[:axis] + (4, g.shape[axis] // 4) + g.shape[axis + 1:]
    return jnp.moveaxis(g.reshape(shape), axis, 0)


def kernel(x, norm_w, w_in, conv_qkv_w, a_log, dt_bias, gdn_norm_w, conf_dw_w, conf_dw_b, conf_ln_w, conf_ln_b, conf_pw_w, w_out, final_norm_w, loss_target, m_norm_w, m_w_in, m_conv_qkv_w, m_a_log, m_dt_bias, m_gdn_norm_w, m_conf_dw_w, m_conf_dw_b, m_conf_ln_w, m_conf_ln_b, m_conf_pw_w, m_w_out, m_final_norm_w, v_norm_w, v_w_in, v_conv_qkv_w, v_a_log, v_dt_bias, v_gdn_norm_w, v_conf_dw_w, v_conf_dw_b, v_conf_ln_w, v_conf_ln_b, v_conf_pw_w, v_w_out, v_final_norm_w):
    xi, yi, ci = _position()
    chip = 2 * xi + yi

    shards = [w_in.astype(BF16), w_out.astype(BF16), conf_pw_w.astype(BF16), conv_qkv_w, conf_dw_w]
    g_in, g_out, g_pw, g_cw, g_dw = [
        lax.dynamic_update_slice_in_dim(g, s[None], chip, axis=0)
        for g, s in zip(_gather_chips(shards, name="gather_weights"), shards)]
    cw_full = jnp.moveaxis(g_cw, 0, 2).reshape(DEPTH, SHORT_CONV, 3 * GDN_W)
    dw_full = jnp.moveaxis(g_dw, 0, 2).reshape(DEPTH, CONV_WIDTH, CONV_CH)
    params = dict(
        norm_w=norm_w[:, None, :],
        wp=_shards_to_padded(g_in),
        wout=jnp.moveaxis(g_out, 0, 1).reshape(DEPTH, D_MODEL, D_MODEL),
        pw_w=jnp.moveaxis(g_pw, 0, 1).reshape(DEPTH, CONV_CH, CONV_CH),
        cw=jnp.pad(cw_full, ((0, 0), (0, SHALO - SHORT_CONV), (0, 0))),
        dw_w=jnp.pad(dw_full, ((0, 0), (0, HALO - CONV_WIDTH), (0, 0))),
        al=_lanes(a_log, GDN_HEADS), dtb=_lanes(dt_bias, GDN_HEADS), gnw=gdn_norm_w[:, None, :],
        dw_b=conf_dw_b[:, None, :], ln_w=conf_ln_w[:, None, :], ln_b=conf_ln_b[:, None, :],
    )

    loss_part, grad_x, grads, d_final = _trunk(x[0], loss_target[0], params, final_norm_w[None, :])
    loss = lax.psum(loss_part, ("x", "y", "c"))

    def half_by_chip(first, dtype):
        wp, wout, pw = [lax.dynamic_slice_in_dim(grads[k], first, 2, axis=0) for k in ("wp", "wout", "pw_w")]
        return [_padded_to_shards(wp, dtype), _by_chip(wout, 1).astype(dtype), _by_chip(pw, 1).astype(dtype)]

    keep = half_by_chip(2 * ci, F32)
    give = half_by_chip(2 * (1 - ci), BF16)
    got = _to_sibling(give, name="grads_to_sibling")
    pair = [_sum_arrays([k.reshape((8,) + k.shape[2:]), r.reshape((8,) + r.shape[2:])], name=f"pair_sum_{i}",
                        out_dtype=BF16).reshape(k.shape) for i, (k, r) in enumerate(zip(keep, got))]
    arrived = _to_chips(pair, name="grads_to_chips")
    halves = []
    for i, (pr, ar) in enumerate(zip(pair, arrived)):
        own = lax.dynamic_index_in_dim(pr, chip, axis=0, keepdims=False)
        halves.append(_sum_into_half([own, ar[0], ar[1], ar[2]], ci, name=f"chip_sum_{i}"))
    g_w_in, g_w_out, g_pw_w = _join_halves(halves, name="join_halves")

    ba = grads["ba_sums"]
    small = [grads["norm_w"], ba[:, 0:1, :], ba[:, 1:2, :], grads["gnw"], grads["ln_sums"][:, 2:3, :],
             grads["ln_sums"][:, 0:1, :], grads["ln_sums"][:, 1:2, :], d_final,
             grads["cw"][:, :SHORT_CONV, :], grads["dw_w"][:, :CONV_WIDTH, :]]
    red = _unpack(_allreduce_small(_pack(small), name="allreduce_small"), [s.shape for s in small])
    g_norm_w = red[0][:, 0, :]
    g_a_log = red[1][:, 0, GDN_HEADS:2 * GDN_HEADS]
    g_dt_bias = red[2][:, 0, GDN_HEADS:2 * GDN_HEADS]
    g_gnw, g_dw_b, g_ln_w, g_ln_b = red[3][:, 0, :], red[4][:, 0, :], red[5][:, 0, :], red[6][:, 0, :]
    g_final = red[7][0]
    g_cw = lax.dynamic_slice_in_dim(red[8], chip * (3 * GDN_W // 4), 3 * GDN_W // 4, axis=2)
    g_dw_w = lax.dynamic_slice_in_dim(red[9], chip * (CONV_CH // 4), CONV_CH // 4, axis=2)

    d_w_in, nm_w_in, nv_w_in = _adamw(w_in, g_w_in, m_w_in, v_w_in, name="adamw_w_in")
    d_w_out, nm_w_out, nv_w_out = _adamw(w_out, g_w_out, m_w_out, v_w_out, name="adamw_w_out")
    d_pw_w, nm_pw_w, nv_pw_w = _adamw(conf_pw_w, g_pw_w, m_conf_pw_w, v_conf_pw_w, name="adamw_pw")
    sw = [norm_w, a_log, dt_bias, gdn_norm_w, conf_dw_b, conf_ln_w, conf_ln_b, final_norm_w, conv_qkv_w, conf_dw_w]
    sg = [g_norm_w, g_a_log, g_dt_bias, g_gnw, g_dw_b, g_ln_w, g_ln_b, g_final, g_cw, g_dw_w]
    sm = [m_norm_w, m_a_log, m_dt_bias, m_gdn_norm_w, m_conf_dw_b, m_conf_ln_w, m_conf_ln_b, m_final_norm_w,
          m_conv_qkv_w, m_conf_dw_w]
    sv = [v_norm_w, v_a_log, v_dt_bias, v_gdn_norm_w, v_conf_dw_b, v_conf_ln_w, v_conf_ln_b, v_final_norm_w,
          v_conv_qkv_w, v_conf_dw_w]
    shapes = [a.shape for a in sw]
    packed = _adamw(_pack(sw)[None], _pack(sg)[None], _pack(sm)[None], _pack(sv)[None], name="adamw_small")
    sd, snm, snv = [_unpack(pk[0], shapes) for pk in packed]

    def order(big3, small10):
        s = small10
        return [s[0], big3[0], s[8], s[1], s[2], s[3], s[9], s[4], s[5], s[6], big3[2], big3[1], s[7]]

    return (loss, grad_x[None], *order([g_w_in, g_w_out, g_pw_w], sg),
            *order([d_w_in, d_w_out, d_pw_w], sd), *order([nm_w_in, nm_w_out, nm_pw_w], snm),
            *order([nv_w_in, nv_w_out, nv_pw_w], snv))
```

```python
import functools
import math

import jax
import jax.numpy as jnp
from jax import lax
from jax.experimental import pallas as pl
from jax.experimental.pallas import tpu as pltpu

F32, BF16 = jnp.float32, jnp.bfloat16
HIGHEST = lax.Precision.HIGHEST
MESH = pl.DeviceIdType.MESH

D_MODEL = 2048
DEPTH = 4
CONV_CH = 512
GDN_W = 768
GDN_HEADS = 6
GDN_D = 128
ATT_W = 768
ATT_HEADS = 12
ATT_HD = 64
CONV_WIDTH = 31
SHORT_CONV = 4
GDN_CHUNK = 64
ROPE_THETA = 500000.0
ROPE_DIM = ATT_HD // 4
DIL_PATTERNS = ((128, 1), (512, 4), (2048, 16))
ATT_BLOCK = 128
NEG_INF = -1e30
IN_W = 7692

IN_WP = 8192
COL_BA = 7680
ORIG_BA = 4608
ORIG_ATT = 4620

ADAM_LR = 0.001
ADAM_B1 = 0.9
ADAM_B2 = 0.999
ADAM_EPS = 1e-08
ADAM_WD = 0.01
ADAM_STEP = 10

VMEM_LIMIT = 56 * 1024 * 1024


def _params(sem=None):
    return pltpu.CompilerParams(dimension_semantics=sem, vmem_limit_bytes=VMEM_LIMIT)


def _sigmoid(x):
    return 0.5 * jnp.tanh(0.5 * x) + 0.5


def _silu(x):
    return x * _sigmoid(x)


def _dsilu(x):
    s = _sigmoid(x)
    return s * (1.0 + x * (1.0 - s))


def _dot(a, b, dims, precision=None):
    return lax.dot_general(a, b, (dims, ((), ())), precision=precision, preferred_element_type=F32)


def _nn(a, b, precision=None):
    return _dot(a, b, ((1,), (0,)), precision)


def _nt(a, b, precision=None):
    return _dot(a, b, ((1,), (1,)), precision)


def _tn(a, b, precision=None):
    return _dot(a, b, ((0,), (0,)), precision)


def _matmul(a, b, *, name, ta=False, tb=False, out_dtype=F32, add=None, stack=None, tm=1024, tn=1024, tk=1024):
    if ta:
        k_dim, m_dim = a.shape
    else:
        m_dim, k_dim = a.shape
    n_dim = b.shape[0] if tb else b.shape[1]
    tm, tn, tk = min(tm, m_dim), min(tn, n_dim), min(tk, k_dim)
    assert m_dim % tm == 0 and n_dim % tn == 0 and k_dim % tk == 0, (name, a.shape, b.shape)
    nk = k_dim // tk
    a_spec = pl.BlockSpec((tk, tm), lambda i, j, k: (k, i)) if ta else pl.BlockSpec((tm, tk), lambda i, j, k: (i, k))
    b_spec = pl.BlockSpec((tn, tk), lambda i, j, k: (j, k)) if tb else pl.BlockSpec((tk, tn), lambda i, j, k: (k, j))
    o_spec = pl.BlockSpec((tm, tn), lambda i, j, k: (i, j))
    out_shape = jax.ShapeDtypeStruct((m_dim, n_dim), out_dtype)
    dims = ((0 if ta else 1,), (1 if tb else 0,))
    has_add = add is not None
    ins = [a, b] + ([add] if has_add else [])
    specs = [a_spec, b_spec] + ([o_spec] if has_add else [])
    aliases = {}
    if stack is not None:
        buf, slab, nslabs = stack
        o_spec = pl.BlockSpec((None, tm, tn), lambda i, j, k: (slab, i, j))
        out_shape = jax.ShapeDtypeStruct((nslabs, m_dim, n_dim), out_dtype)
        if buf is not None:
            aliases = {len(ins): 0}
            ins.append(buf)
            specs.append(pl.BlockSpec(memory_space=pl.ANY))
    n_in = len(ins)

    def body(*refs):
        a_ref, b_ref = refs[0], refs[1]
        o_ref = refs[n_in]

        def finish(r):
            if has_add:
                r = r + refs[2][...]
            o_ref[...] = r.astype(out_dtype)

        prod = _dot(a_ref[...].astype(BF16), b_ref[...].astype(BF16), dims)
        if nk == 1:
            finish(prod)
            return
        acc_ref = refs[n_in + 1]
        k = pl.program_id(2)

        @pl.when(k == 0)
        def _():
            acc_ref[...] = prod

        @pl.when(k > 0)
        def _():
            acc_ref[...] += prod

        @pl.when(k == nk - 1)
        def _():
            finish(acc_ref[...])

    return pl.pallas_call(
        body, name=name, grid=(m_dim // tm, n_dim // tn, nk), in_specs=specs, out_specs=o_spec,
        out_shape=out_shape, scratch_shapes=[pltpu.VMEM((tm, tn), F32)] if nk > 1 else [],
        input_output_aliases=aliases,
        compiler_params=_params(("parallel", "parallel", "arbitrary")),
    )(*ins)


def _rms_fwd(x, w, *, name, tm=256):
    s_len, d = x.shape

    def body(x_ref, w_ref, h_ref):
        xv = x_ref[...]
        r = lax.rsqrt(jnp.mean(xv * xv, axis=-1, keepdims=True) + 1e-6)
        h_ref[...] = (xv * r * w_ref[...]).astype(BF16)

    return pl.pallas_call(
        body, name=name, grid=(s_len // tm,),
        in_specs=[pl.BlockSpec((tm, d), lambda i: (i, 0)), pl.BlockSpec((1, d), lambda i: (0, 0))],
        out_specs=pl.BlockSpec((tm, d), lambda i: (i, 0)),
        out_shape=jax.ShapeDtypeStruct((s_len, d), BF16),
        compiler_params=_params(("parallel",)),
    )(x, w)


def _rms_bwd(x, dh, w, dres, *, name, tm=256):
    s_len, d = x.shape
    nsteps = s_len // tm

    def body(x_ref, dh_ref, w_ref, dres_ref, dx_ref, dw_ref, acc_ref):
        i = pl.program_id(0)

        @pl.when(i == 0)
        def _():
            acc_ref[...] = jnp.zeros_like(acc_ref)

        xv = x_ref[...]
        r = lax.rsqrt(jnp.mean(xv * xv, axis=-1, keepdims=True) + 1e-6)
        xn = xv * r
        dy = dh_ref[...]
        dxn = dy * w_ref[...]
        dx_ref[...] = dres_ref[...] + r * (dxn - xn * jnp.mean(dxn * xn, axis=-1, keepdims=True))
        acc_ref[...] += (dy * xn).reshape(tm // 8, 8, d).sum(axis=0)

        @pl.when(i == nsteps - 1)
        def _():
            dw_ref[...] = jnp.sum(acc_ref[...], axis=0, keepdims=True)

    row = pl.BlockSpec((tm, d), lambda i: (i, 0))
    vec = pl.BlockSpec((1, d), lambda i: (0, 0))
    return pl.pallas_call(
        body, name=name, grid=(nsteps,), in_specs=[row, row, vec, row], out_specs=[row, vec],
        out_shape=[jax.ShapeDtypeStruct((s_len, d), F32), jax.ShapeDtypeStruct((1, d), F32)],
        scratch_shapes=[pltpu.VMEM((8, d), F32)],
        compiler_params=_params(("arbitrary",)),
    )(x, dh, w, dres)


def _loss_head(x, w, target, *, name, tm=256):
    s_len, d = x.shape
    nsteps = s_len // tm

    def body(x_ref, w_ref, t_ref, dx_ref, dw_ref, loss_ref, acc_ref, lacc_ref):
        i = pl.program_id(0)

        @pl.when(i == 0)
        def _():
            acc_ref[...] = jnp.zeros_like(acc_ref)
            lacc_ref[...] = jnp.zeros_like(lacc_ref)

        xv = x_ref[...]
        wv = w_ref[...]
        r = lax.rsqrt(jnp.mean(xv * xv, axis=-1, keepdims=True) + 1e-6)
        xn = xv * r
        err = xn * wv - t_ref[...]
        lacc_ref[...] += (err * err).reshape(tm // 8, 8, d).sum(axis=0)
        dy = err * (1.0 / d)
        dxn = dy * wv
        dx_ref[...] = r * (dxn - xn * jnp.mean(dxn * xn, axis=-1, keepdims=True))
        acc_ref[...] += (dy * xn).reshape(tm // 8, 8, d).sum(axis=0)

        @pl.when(i == nsteps - 1)
        def _():
            dw_ref[...] = jnp.sum(acc_ref[...], axis=0, keepdims=True)
            tot = jnp.sum(jnp.sum(lacc_ref[...], axis=0, keepdims=True), axis=1, keepdims=True)
            loss_ref[...] = jnp.broadcast_to(tot * (0.5 / d), (1, 128))

    row = pl.BlockSpec((tm, d), lambda i: (i, 0))
    vec = pl.BlockSpec((1, d), lambda i: (0, 0))
    return pl.pallas_call(
        body, name=name, grid=(nsteps,), in_specs=[row, vec, row],
        out_specs=[row, vec, pl.BlockSpec((1, 128), lambda i: (0, 0))],
        out_shape=[jax.ShapeDtypeStruct((s_len, d), F32), jax.ShapeDtypeStruct((1, d), F32),
                   jax.ShapeDtypeStruct((1, 128), F32)],
        scratch_shapes=[pltpu.VMEM((8, d), F32), pltpu.VMEM((8, d), F32)],
        compiler_params=_params(("arbitrary",)),
    )(x, w, target)


def _rows_block(shape, tr=256):
    lead, rows, cols = shape
    if rows % tr != 0:
        assert rows * cols <= 1 << 20, shape
        tr = rows
    return (lead, rows // tr), pl.BlockSpec((1, tr, cols), lambda a, i: (a, i, 0))


def _adamw(w, g, m, v, *, name):
    grid, spec = _rows_block(w.shape)
    c1 = 1.0 / (1.0 - ADAM_B1 ** ADAM_STEP)
    c2 = 1.0 / (1.0 - ADAM_B2 ** ADAM_STEP)

    def body(w_ref, g_ref, m_ref, v_ref, d_ref, nm_ref, nv_ref):
        gv = g_ref[...]
        nm = ADAM_B1 * m_ref[...] + (1.0 - ADAM_B1) * gv
        nv = ADAM_B2 * v_ref[...] + (1.0 - ADAM_B2) * (gv * gv)
        nm_ref[...] = nm
        nv_ref[...] = nv
        d_ref[...] = -ADAM_LR * ((nm * c1) / (jnp.sqrt(nv * c2) + ADAM_EPS) + ADAM_WD * w_ref[...])

    out = jax.ShapeDtypeStruct(w.shape, F32)
    return pl.pallas_call(
        body, name=name, grid=grid, in_specs=[spec] * 4, out_specs=[spec] * 3, out_shape=[out] * 3,
        compiler_params=_params(("parallel", "parallel")),
    )(w, g, m, v)


def _sum_into_half(arrs, half, *, name):
    lead, rows, cols = arrs[0].shape
    assert lead == 2
    (_, nr), spec0 = _rows_block(arrs[0].shape)
    tr = spec0.block_shape[1]
    n = len(arrs)

    def body(half_ref, *refs):
        del half_ref
        acc = refs[0][...].astype(F32)
        for r in refs[1:n]:
            acc = acc + r[...].astype(F32)
        refs[n][...] = acc

    spec = pl.BlockSpec((1, tr, cols), lambda a, i, h: (a, i, 0))
    return pl.pallas_call(
        body, name=name,
        grid_spec=pltpu.PrefetchScalarGridSpec(
            num_scalar_prefetch=1, grid=(2, nr), in_specs=[spec] * n,
            out_specs=pl.BlockSpec((1, tr, cols), lambda a, i, h: (2 * h[0] + a, i, 0))),
        out_shape=jax.ShapeDtypeStruct((4, rows, cols), F32),
        compiler_params=_params(("parallel", "parallel")),
    )(jnp.reshape(half, (1,)).astype(jnp.int32), *arrs)


def _sum_arrays(arrs, *, name, out_dtype):
    grid, spec = _rows_block(arrs[0].shape)
    n = len(arrs)

    def body(*refs):
        acc = refs[0][...].astype(F32)
        for r in refs[1:n]:
            acc = acc + r[...].astype(F32)
        refs[n][...] = acc.astype(out_dtype)

    return pl.pallas_call(
        body, name=name, grid=grid, in_specs=[spec] * n, out_specs=spec,
        out_shape=jax.ShapeDtypeStruct(arrs[0].shape, out_dtype),
        compiler_params=_params(("parallel", "parallel")),
    )(*arrs)


HALO = 32


def _shifted_windows(buf, tm, offsets):
    rows = buf.shape[0]
    for b in range(8):
        group = [(k, s) for k, s in enumerate(offsets) if s % 8 == b]
        if not group:
            continue
        rb = buf if b == 0 else pltpu.roll(buf, rows - b, 0)
        for k, s in group:
            yield k, rb[s - b:s - b + tm, :]


def _conf_fwd(u, dw_w, dw_b, ln_w, ln_b, *, name, tm=256):
    s_len = u.shape[0]
    c = CONV_CH

    def body(uc_ref, up_ref, dww_ref, dwb_ref, lnw_ref, lnb_ref, conv_ref, sw_ref, hbuf):
        i = pl.program_id(0)
        hbuf[HALO:, :] = uc_ref[:, :c] * _sigmoid(uc_ref[:, c:])
        hp = up_ref[:, :c] * _sigmoid(up_ref[:, c:])
        hbuf[:HALO, :] = jnp.where(i > 0, hp, 0.0)
        for cb in range(c // 128):
            cs = slice(128 * cb, 128 * (cb + 1))
            acc = jnp.zeros((tm, 128), F32)
            taps = [HALO - CONV_WIDTH + 1 + j for j in range(CONV_WIDTH)]
            for j, win in _shifted_windows(hbuf[:, cs], tm, taps):
                acc = acc + win * dww_ref[j:j + 1, cs]
            conv_ref[:, cs] = acc + dwb_ref[:, cs]
        cv = conv_ref[...]
        mu = jnp.mean(cv, axis=-1, keepdims=True)
        xc = cv - mu
        var = jnp.mean(xc * xc, axis=-1, keepdims=True)
        ln = xc * lax.rsqrt(var + 1e-5) * lnw_ref[...] + lnb_ref[...]
        sw_ref[...] = _silu(ln).astype(BF16)

    vec = pl.BlockSpec((1, c), lambda i: (0, 0))
    return pl.pallas_call(
        body, name=name, grid=(s_len // tm,),
        in_specs=[pl.BlockSpec((tm, 2 * c), lambda i: (i, 0)),
                  pl.BlockSpec((HALO, 2 * c), lambda i: (jnp.maximum(i * (tm // HALO) - 1, 0), 0)),
                  pl.BlockSpec((HALO, c), lambda i: (0, 0)), vec, vec, vec],
        out_specs=[pl.BlockSpec((tm, c), lambda i: (i, 0))] * 2,
        out_shape=[jax.ShapeDtypeStruct((s_len, c), F32), jax.ShapeDtypeStruct((s_len, c), BF16)],
        scratch_shapes=[pltpu.VMEM((tm + HALO, c), F32)],
        compiler_params=_params(("parallel",)),
    )(u, u, dw_w, dw_b, ln_w, ln_b)


def _conf_bwd_ln(d_sw, conv, ln_w, ln_b, *, name, tm=256):
    s_len, c = conv.shape
    nsteps = s_len // tm

    def body(dsw_ref, conv_ref, lnw_ref, lnb_ref, dconv_ref, sums_ref):
        i = pl.program_id(0)

        @pl.when(i == 0)
        def _():
            sums_ref[...] = jnp.zeros_like(sums_ref)

        cv = conv_ref[...]
        mu = jnp.mean(cv, axis=-1, keepdims=True)
        xc = cv - mu
        rs = lax.rsqrt(jnp.mean(xc * xc, axis=-1, keepdims=True) + 1e-5)
        xhat = xc * rs
        lnw = lnw_ref[...]
        ln = xhat * lnw + lnb_ref[...]
        dln = dsw_ref[...] * _dsilu(ln)
        dxh = dln * lnw
        dconv = rs * (dxh - jnp.mean(dxh, axis=-1, keepdims=True)
                      - xhat * jnp.mean(dxh * xhat, axis=-1, keepdims=True))
        dconv_ref[...] = dconv
        sums_ref[0:1, :] += jnp.sum(dln * xhat, axis=0, keepdims=True)
        sums_ref[1:2, :] += jnp.sum(dln, axis=0, keepdims=True)
        sums_ref[2:3, :] += jnp.sum(dconv, axis=0, keepdims=True)

    row = pl.BlockSpec((tm, c), lambda i: (i, 0))
    vec = pl.BlockSpec((1, c), lambda i: (0, 0))
    return pl.pallas_call(
        body, name=name, grid=(nsteps,), in_specs=[row, row, vec, vec],
        out_specs=[row, pl.BlockSpec((8, c), lambda i: (0, 0))],
        out_shape=[jax.ShapeDtypeStruct((s_len, c), F32), jax.ShapeDtypeStruct((8, c), F32)],
        compiler_params=_params(("arbitrary",)),
    )(d_sw, conv, ln_w, ln_b)


def _conf_bwd_conv(u, dconv, dw_w, du, *, name, tm=256):
    s_len = u.shape[0]
    c = CONV_CH
    nsteps = s_len // tm
    off = HALO - CONV_WIDTH + 1

    def body(uc_ref, up_ref, dc_ref, dn_ref, dww_ref, du_in_ref, du_ref, ddw_ref, hbuf, dbuf, wacc):
        del du_in_ref
        i = pl.program_id(0)

        @pl.when(i == 0)
        def _():
            wacc[...] = jnp.zeros_like(wacc)

        hbuf[HALO:, :] = uc_ref[:, :c] * _sigmoid(uc_ref[:, c:])
        hp = up_ref[:, :c] * _sigmoid(up_ref[:, c:])
        hbuf[:HALO, :] = jnp.where(i > 0, hp, 0.0)
        dbuf[:tm, :] = dc_ref[...]
        dbuf[tm:, :] = jnp.where(i < nsteps - 1, dn_ref[...], 0.0)
        for cb in range(c // 128):
            cs = slice(128 * cb, 128 * (cb + 1))
            dcur = dbuf[0:tm, cs]
            acc = jnp.zeros((tm, 128), F32)
            for k, win in _shifted_windows(dbuf[:, cs], tm, list(range(CONV_WIDTH))):
                j = CONV_WIDTH - 1 - k
                acc = acc + win * dww_ref[j:j + 1, cs]
            for j, win in _shifted_windows(hbuf[:, cs], tm, [off + j for j in range(CONV_WIDTH)]):
                wacc[j, :, cs] += (win * dcur).reshape(tm // 8, 8, 128).sum(axis=0)
            a = uc_ref[:, cs]
            sg = _sigmoid(uc_ref[:, c + 128 * cb:c + 128 * (cb + 1)])
            du_ref[:, cs] = (acc * sg).astype(du_ref.dtype)
            du_ref[:, c + 128 * cb:c + 128 * (cb + 1)] = (acc * a * sg * (1.0 - sg)).astype(du_ref.dtype)

        @pl.when(i == nsteps - 1)
        def _():
            for j in range(CONV_WIDTH):
                ddw_ref[j:j + 1, :] = jnp.sum(wacc[j], axis=0, keepdims=True)
            ddw_ref[CONV_WIDTH:, :] = jnp.zeros((HALO - CONV_WIDTH, c), F32)

    return pl.pallas_call(
        body, name=name, grid=(nsteps,),
        in_specs=[pl.BlockSpec((tm, 2 * c), lambda i: (i, 0)),
                  pl.BlockSpec((HALO, 2 * c), lambda i: (jnp.maximum(i * (tm // HALO) - 1, 0), 0)),
                  pl.BlockSpec((tm, c), lambda i: (i, 0)),
                  pl.BlockSpec((HALO, c), lambda i: (jnp.minimum((i + 1) * (tm // HALO), s_len // HALO - 1), 0)),
                  pl.BlockSpec((HALO, c), lambda i: (0, 0)),
                  pl.BlockSpec(memory_space=pl.ANY)],
        out_specs=[pl.BlockSpec((tm, 2 * c), lambda i: (i, 0)), pl.BlockSpec((HALO, c), lambda i: (0, 0))],
        out_shape=[jax.ShapeDtypeStruct(du.shape, du.dtype), jax.ShapeDtypeStruct((HALO, c), F32)],
        scratch_shapes=[pltpu.VMEM((tm + HALO, c), F32), pltpu.VMEM((tm + HALO, c), F32),
                        pltpu.VMEM((CONV_WIDTH, 8, c), F32)],
        input_output_aliases={5: 0},
        compiler_params=_params(("arbitrary",)),
    )(u, u, dconv, dconv, dw_w, du)


COL_GQ = 1536 // GDN_W
COL_AQ = 4608 // ATT_W
SHALO = 8


def _softplus(z):
    return jnp.maximum(z, 0.0) + jnp.log1p(jnp.exp(-jnp.abs(z)))


def _short_conv(buf, cw_ref, part, rows, first):
    acc = jnp.zeros((rows, GDN_W), F32)
    for j, win in _shifted_windows(buf[...], rows, [first + j for j in range(SHORT_CONV)]):
        acc = acc + win * cw_ref[j:j + 1, GDN_W * part:GDN_W * (part + 1)]
    return acc


def _gdn_prep_fwd(u, cw, al, dtb, *, name, tm=256):
    s_len = u.shape[0]
    first = SHALO - SHORT_CONV + 1

    def body(uq, uk, uv, pq, pk, pv, uba, cw_ref, al_ref, dtb_ref, qn_ref, kn_ref, vc_ref, bg_ref, buf):
        i = pl.program_id(0)

        def conv(cur, prev, part):
            buf[SHALO:, :] = cur[...]
            buf[:SHALO, :] = jnp.where(i > 0, prev[...], 0.0)
            return _silu(_short_conv(buf, cw_ref, part, tm, first))

        for part, (cur, prev, out, scale) in enumerate(
                ((uq, pq, qn_ref, GDN_D ** -0.5), (uk, pk, kn_ref, 1.0))):
            y = conv(cur, prev, part)
            for h in range(GDN_HEADS):
                hs = slice(GDN_D * h, GDN_D * (h + 1))
                yh = y[:, hs]
                out[:, hs] = yh * (lax.rsqrt(jnp.sum(yh * yh, axis=-1, keepdims=True) + 1e-6) * scale)
        vc_ref[...] = conv(uv, pv, 2)
        ba = uba[...]
        lane = lax.broadcasted_iota(jnp.int32, ba.shape, 1)
        g = -jnp.exp(al_ref[...]) * _softplus(ba + dtb_ref[...])
        bg_ref[...] = jnp.where(lane < GDN_HEADS, _sigmoid(ba), jnp.where(lane < 2 * GDN_HEADS, g, 0.0))

    def cur(col):
        return pl.BlockSpec((tm, GDN_W), lambda i: (i, col))

    def prev(col):
        return pl.BlockSpec((SHALO, GDN_W), lambda i: (jnp.maximum(i * (tm // SHALO) - 1, 0), col))

    vec = pl.BlockSpec((1, 128), lambda i: (0, 0))
    row = pl.BlockSpec((tm, GDN_W), lambda i: (i, 0))
    wide = jax.ShapeDtypeStruct((s_len, GDN_W), F32)
    return pl.pallas_call(
        body, name=name, grid=(s_len // tm,),
        in_specs=[cur(COL_GQ), cur(COL_GQ + 1), cur(COL_GQ + 2), prev(COL_GQ), prev(COL_GQ + 1), prev(COL_GQ + 2),
                  pl.BlockSpec((tm, 128), lambda i: (i, COL_BA // 128)),
                  pl.BlockSpec((SHALO, 3 * GDN_W), lambda i: (0, 0)), vec, vec],
        out_specs=[row, row, row, pl.BlockSpec((tm, 128), lambda i: (i, 0))],
        out_shape=[wide, wide, wide, jax.ShapeDtypeStruct((s_len, 128), F32)],
        scratch_shapes=[pltpu.VMEM((tm + SHALO, GDN_W), F32)],
        compiler_params=_params(("parallel",)),
    )(u, u, u, u, u, u, u, cw, al, dtb)


def _chunk_masks():
    c = GDN_CHUNK
    row = lax.broadcasted_iota(jnp.int32, (c, c), 0)
    col = lax.broadcasted_iota(jnp.int32, (c, c), 1)
    return row >= col, row > col


def _cum_decay(bg):
    c = GDN_CHUNK
    causal, _ = _chunk_masks()
    g_cum = _nn(causal.astype(F32), bg, HIGHEST)
    sel = (lax.broadcasted_iota(jnp.int32, (8, 128), 0) + GDN_HEADS
           == lax.broadcasted_iota(jnp.int32, (8, 128), 1)).astype(F32)
    return g_cum, _nt(sel, g_cum, HIGHEST)


def _bdot(a, b, ca, cb):
    return lax.dot_general(a, b, (((ca,), (cb,)), ((0,), (0,))), preferred_element_type=F32)


def _bnn(a, b):
    return _bdot(a, b, 2, 1)


def _bnt(a, b):
    return _bdot(a, b, 2, 2)


def _btn(a, b):
    return _bdot(a, b, 1, 1)


def _split(a):
    hi = a.astype(BF16)
    return hi, (a - hi.astype(F32)).astype(BF16)


def _bnn3(a, b):
    ah, al = _split(a)
    bh, bl = _split(b)
    return _bnn(ah, bh) + (_bnn(al, bh) + _bnn(ah, bl))


def _heads(ref):
    return jnp.stack([ref[:, GDN_D * h:GDN_D * (h + 1)] for h in range(GDN_HEADS)])


def _head_columns(a, first):
    return jnp.stack([a[:, first + h:first + h + 1] for h in range(GDN_HEADS)])


def _chunk_decay(g_cum, g_rows, bg):
    causal, _ = _chunk_masks()
    gc = _head_columns(g_cum, GDN_HEADS)
    gr = jnp.stack([g_rows[h:h + 1, :] for h in range(GDN_HEADS)])
    dec = jnp.where(causal, jnp.exp(jnp.where(causal, gc - gr, 0.0)), 0.0)
    return gc, _head_columns(bg, 0), dec


def _gdn_intra_fwd(qn, kn, vc, bg, *, name):
    s_len = qn.shape[0]
    c = GDN_CHUNK
    nchunks = s_len // c

    def body(q_ref, k_ref, v_ref, bg_ref, wk_ref, wv_ref, qd_ref, kd_ref, p_ref, t_ref, g_ref):
        causal, strict = _chunk_masks()
        eye = (lax.broadcasted_iota(jnp.int32, (c, c), 0) == lax.broadcasted_iota(jnp.int32, (c, c), 1)).astype(F32)
        bg = bg_ref[...]
        g_cum, g_rows = _cum_decay(bg)
        g_ref[...] = g_cum
        gc, bc, dec = _chunk_decay(g_cum, g_rows, bg)
        q, k, v = _heads(q_ref), _heads(k_ref), _heads(v_ref)
        k16 = k.astype(BF16)
        low = jnp.where(strict, bc * _bnt(k16, k16) * dec, 0.0)
        pw = -low
        t = eye + pw
        for _ in range(5):
            pw = _bnn3(pw, pw)
            t = t + _bnn3(t, pw)
        t_ref[...] = t
        t16 = t.astype(BF16)
        eg = jnp.exp(gc)
        wk = _bnn(t16, (k * (bc * eg)).astype(BF16))
        wv = _bnn(t16, (v * bc).astype(BF16))
        p_ref[...] = jnp.where(causal, _bnt(q.astype(BF16), k16) * dec, 0.0).astype(BF16)
        qd = q * eg
        kd = k * jnp.exp(gc[:, c - 1:c, :] - gc)
        for h in range(GDN_HEADS):
            hs = slice(GDN_D * h, GDN_D * (h + 1))
            wk_ref[:, hs] = wk[h].astype(BF16)
            wv_ref[:, hs] = wv[h]
            qd_ref[:, hs] = qd[h].astype(BF16)
            kd_ref[:, hs] = kd[h].astype(BF16)

    row = pl.BlockSpec((c, GDN_W), lambda n: (n, 0))
    sq = pl.BlockSpec((GDN_HEADS, c, c), lambda n: (0, n, 0))
    narrow = pl.BlockSpec((c, 128), lambda n: (n, 0))
    w16 = jax.ShapeDtypeStruct((s_len, GDN_W), BF16)
    return pl.pallas_call(
        body, name=name, grid=(nchunks,), in_specs=[row, row, row, narrow],
        out_specs=[row, row, row, row, sq, sq, narrow],
        out_shape=[w16, jax.ShapeDtypeStruct((s_len, GDN_W), F32), w16, w16,
                   jax.ShapeDtypeStruct((GDN_HEADS, s_len, c), BF16),
                   jax.ShapeDtypeStruct((GDN_HEADS, s_len, c), F32),
                   jax.ShapeDtypeStruct((s_len, 128), F32)],
        compiler_params=_params(("parallel",)),
    )(qn, kn, vc, bg)


def _gdn_scan_fwd(wk, wv, qd, kd, p, g_cum, *, name):
    s_len = wk.shape[0]
    c = GDN_CHUNK
    nchunks = s_len // c

    def body(wk_ref, wv_ref, qd_ref, kd_ref, p_ref, g_ref, o_ref, vn_ref, sp_ref, st):
        @pl.when(pl.program_id(0) == 0)
        def _():
            st[...] = jnp.zeros_like(st)

        s = st[...]
        sp_ref[0] = s
        s16 = s.astype(BF16)
        vn16 = (_heads(wv_ref) - _bnn(_heads(wk_ref), s16)).astype(BF16)
        o = _bnn(_heads(qd_ref), s16) + _bnn(p_ref[...], vn16)
        gl = jnp.exp(_head_columns(g_ref[c - 1:c, :], GDN_HEADS))
        st[...] = s * gl + _btn(_heads(kd_ref), vn16)
        for h in range(GDN_HEADS):
            hs = slice(GDN_D * h, GDN_D * (h + 1))
            vn_ref[:, hs] = vn16[h]
            o_ref[:, hs] = o[h]

    row = pl.BlockSpec((c, GDN_W), lambda n: (n, 0))
    return pl.pallas_call(
        body, name=name, grid=(nchunks,),
        in_specs=[row, row, row, row, pl.BlockSpec((GDN_HEADS, c, c), lambda n: (0, n, 0)),
                  pl.BlockSpec((c, 128), lambda n: (n, 0))],
        out_specs=[row, row, pl.BlockSpec((1, GDN_HEADS, GDN_D, GDN_D), lambda n: (n, 0, 0, 0))],
        out_shape=[jax.ShapeDtypeStruct((s_len, GDN_W), F32), jax.ShapeDtypeStruct((s_len, GDN_W), BF16),
                   jax.ShapeDtypeStruct((nchunks, GDN_HEADS, GDN_D, GDN_D), F32)],
        scratch_shapes=[pltpu.VMEM((GDN_HEADS, GDN_D, GDN_D), F32)],
        compiler_params=_params(("arbitrary",)),
    )(wk, wv, qd, kd, p, g_cum)


def _gdn_scan_bwd(do, wk, qd, kd, p, g_cum, *, name):
    s_len = wk.shape[0]
    c = GDN_CHUNK
    nchunks = s_len // c

    def body(do_ref, wk_ref, qd_ref, kd_ref, p_ref, g_ref, dvn_ref, ds_ref, dst):
        @pl.when(pl.program_id(0) == 0)
        def _():
            dst[...] = jnp.zeros_like(dst)

        ds = dst[...]
        ds_ref[0] = ds
        do16 = _heads(do_ref).astype(BF16)
        dvn16 = (_btn(p_ref[...], do16) + _bnn(_heads(kd_ref), ds.astype(BF16))).astype(BF16)
        gl = jnp.exp(_head_columns(g_ref[c - 1:c, :], GDN_HEADS))
        dst[...] = _btn(_heads(qd_ref), do16) + ds * gl - _btn(_heads(wk_ref), dvn16)
        for h in range(GDN_HEADS):
            dvn_ref[:, GDN_D * h:GDN_D * (h + 1)] = dvn16[h]

    row = pl.BlockSpec((c, GDN_W), lambda n: (nchunks - 1 - n, 0))
    return pl.pallas_call(
        body, name=name, grid=(nchunks,),
        in_specs=[row, row, row, row, pl.BlockSpec((GDN_HEADS, c, c), lambda n: (0, nchunks - 1 - n, 0)),
                  pl.BlockSpec((c, 128), lambda n: (nchunks - 1 - n, 0))],
        out_specs=[row, pl.BlockSpec((1, GDN_HEADS, GDN_D, GDN_D), lambda n: (nchunks - 1 - n, 0, 0, 0))],
        out_shape=[jax.ShapeDtypeStruct((s_len, GDN_W), BF16),
                   jax.ShapeDtypeStruct((nchunks, GDN_HEADS, GDN_D, GDN_D), F32)],
        scratch_shapes=[pltpu.VMEM((GDN_HEADS, GDN_D, GDN_D), F32)],
        compiler_params=_params(("arbitrary",)),
    )(do, wk, qd, kd, p, g_cum)


def _gdn_intra_bwd(qn, kn, vc, bg, g_cum, t, do, dvn, vn, sprev, ds_all, *, name):
    s_len = qn.shape[0]
    c = GDN_CHUNK
    nchunks = s_len // c

    def body(q_ref, k_ref, v_ref, bg_ref, g_ref, t_ref, do_ref, dvn_ref, vn_ref, sp_ref, ds_ref,
             dqkv_ref, dbg_ref):
        causal, strict = _chunk_masks()
        bg = bg_ref[...]
        g_cum = g_ref[...]
        _, g_rows = _cum_decay(bg)
        lane = lax.broadcasted_iota(jnp.int32, (c, 128), 1)
        rowi = lax.broadcasted_iota(jnp.int32, (c, 128), 0)
        gc, bc, dec = _chunk_decay(g_cum, g_rows, bg)
        q, k, v = _heads(q_ref), _heads(k_ref), _heads(v_ref)
        q16, k16 = q.astype(BF16), k.astype(BF16)
        kk = _bnt(k16, k16)
        low = jnp.where(strict, bc * kk * dec, 0.0)
        eg = jnp.exp(gc)
        g_last = gc[:, c - 1:c, :]
        kdec = jnp.exp(g_last - gc)
        kb, vb, qd, kd = k * (bc * eg), v * bc, q * eg, k * kdec
        pm = jnp.where(causal, _bnt(q16, k16) * dec, 0.0)
        s = sp_ref[0]
        ds = ds_ref[0]
        s16, ds16 = s.astype(BF16), ds.astype(BF16)
        do16 = _heads(do_ref).astype(BF16)
        dvn16, vn16 = _heads(dvn_ref), _heads(vn_ref)
        tm = t_ref[...]
        t16 = tm.astype(BF16)

        dqd = _bnt(do16, s16)
        dp = jnp.where(causal, _bnt(do16, vn16), 0.0)
        dkd = _bnt(vn16, ds16)
        dgl = jnp.sum(jnp.sum(s * ds, axis=2, keepdims=True), axis=1, keepdims=True) * jnp.exp(g_last)
        dwk16 = (-_bnt(dvn16, s16)).astype(BF16)
        dt = _bnt(dwk16, kb.astype(BF16)) + _bnt(dvn16, vb.astype(BF16))
        dkb = _btn(t16, dwk16)
        dvb = _btn(t16, dvn16)
        th, tl = _split(tm)
        dth, dtl = _split(dt)
        xm = _btn(th, dth) + (_btn(tl, dth) + _btn(th, dtl))
        xh, xl = _split(xm)
        dlow = jnp.where(strict, -(_bnt(xh, th) + (_bnt(xl, th) + _bnt(xh, tl))), 0.0)
        dkk16 = (dlow * bc * dec).astype(BF16)
        dqk16 = (dp * dec).astype(BF16)

        dq = _bnn(dqk16, k16) + dqd * eg
        dk = _btn(dqk16, q16) + _bnn(dkk16, k16) + _btn(dkk16, k16) + dkb * (bc * eg) + dkd * kdec
        dv = dvb * bc
        for h in range(GDN_HEADS):
            hs = slice(GDN_D * h, GDN_D * (h + 1))
            dqkv_ref[0, :, hs] = dq[h]
            dqkv_ref[1, :, hs] = dk[h]
            dqkv_ref[2, :, hs] = dv[h]

        dbeta = (jnp.sum(dlow * kk * dec, axis=2, keepdims=True)
                 + jnp.sum(dkb * k, axis=2, keepdims=True) * eg + jnp.sum(dvb * v, axis=2, keepdims=True))
        mm = dlow * low + dp * pm
        mh, ml = _split(mm)
        ones16 = jnp.ones((GDN_HEADS, c, 128), BF16)
        col_sum = (_btn(mh, ones16) + _btn(ml, ones16))[:, :, 0:1]
        dkd_sum = jnp.sum(dkd * kd, axis=2, keepdims=True)
        dg = (jnp.sum(mm, axis=2, keepdims=True) - col_sum + jnp.sum(dkb * kb, axis=2, keepdims=True)
              + jnp.sum(dqd * qd, axis=2, keepdims=True) - dkd_sum)
        tail = jnp.sum(dkd_sum, axis=1, keepdims=True) + dgl
        dbeta_all = jnp.zeros((c, 128), F32)
        dg_all = jnp.zeros((c, 128), F32)
        for h in range(GDN_HEADS):
            dbeta_all = dbeta_all + jnp.where(lane == h, dbeta[h], 0.0)
            dg_all = dg_all + jnp.where(lane == GDN_HEADS + h, dg[h] + jnp.where(rowi == c - 1, tail[h], 0.0), 0.0)
        upper = (lax.broadcasted_iota(jnp.int32, (c, c), 0) <= lax.broadcasted_iota(jnp.int32, (c, c), 1)).astype(F32)
        dbg_ref[...] = dbeta_all + _nn(upper, dg_all, HIGHEST)

    row = pl.BlockSpec((c, GDN_W), lambda n: (n, 0))
    narrow = pl.BlockSpec((c, 128), lambda n: (n, 0))
    state = pl.BlockSpec((1, GDN_HEADS, GDN_D, GDN_D), lambda n: (n, 0, 0, 0))
    return pl.pallas_call(
        body, name=name, grid=(nchunks,),
        in_specs=[row, row, row, narrow, narrow, pl.BlockSpec((GDN_HEADS, c, c), lambda n: (0, n, 0)),
                  row, row, row, state, state],
        out_specs=[pl.BlockSpec((3, c, GDN_W), lambda n: (0, n, 0)), narrow],
        out_shape=[jax.ShapeDtypeStruct((3, s_len, GDN_W), F32), jax.ShapeDtypeStruct((s_len, 128), F32)],
        compiler_params=_params(("parallel",)),
    )(qn, kn, vc, bg, g_cum, t, do, dvn, vn, sprev, ds_all)


def _gdn_prep_bwd(u, dqkv, cw, du, *, name, tm=256):
    s_len = u.shape[0]
    nsteps = s_len // tm
    ext = tm + SHALO

    def body(uc, up, un, dc, dn, cw_ref, du_in_ref, du_ref, dcw_ref, xbuf, dbuf, pbuf, wacc):
        del du_in_ref
        part = pl.program_id(0)
        i = pl.program_id(1)

        @pl.when(i == 0)
        def _():
            wacc[...] = jnp.zeros_like(wacc)

        xbuf[:SHALO, :] = jnp.where(i > 0, up[...], 0.0)
        xbuf[SHALO:SHALO + tm, :] = uc[...]
        xbuf[SHALO + tm:, :] = jnp.where(i < nsteps - 1, un[...], 0.0)
        dbuf[:tm, :] = dc[...]
        dbuf[tm:, :] = jnp.where(i < nsteps - 1, dn[...], 0.0)
        first = SHALO - SHORT_CONV + 1
        w = [cw_ref[j:j + 1, :] for j in range(SHORT_CONV)]
        taps = [first + j for j in range(SHORT_CONV)]
        xv = xbuf[...]
        pre = jnp.zeros((ext, GDN_W), F32)
        for j, win in _shifted_windows(xv, ext, taps):
            pre = pre + win * w[j]
        y = _silu(pre)
        dout = dbuf[...]
        scale = jnp.where(part == 0, GDN_D ** -0.5, 1.0)
        for h in range(GDN_HEADS):
            hs = slice(GDN_D * h, GDN_D * (h + 1))
            yh, dh = y[:, hs], dout[:, hs]
            rs = lax.rsqrt(jnp.sum(yh * yh, axis=-1, keepdims=True) + 1e-6)
            dyn = scale * rs * (dh - yh * (rs * rs) * jnp.sum(dh * yh, axis=-1, keepdims=True))
            dy = jnp.where(part < 2, dyn, dh)
            pbuf[:, hs] = dy * _dsilu(pre[:, hs])
        acc = jnp.zeros((tm, GDN_W), F32)
        dpre = pbuf[0:tm, :]
        for k, win in _shifted_windows(pbuf[...], tm, list(range(SHORT_CONV))):
            acc = acc + win * w[SHORT_CONV - 1 - k]
        for j, win in _shifted_windows(xv, tm, taps):
            wacc[j] += (win * dpre).reshape(tm // 8, 8, GDN_W).sum(axis=0)
        du_ref[...] = acc.astype(du_ref.dtype)

        @pl.when(i == nsteps - 1)
        def _():
            for j in range(SHORT_CONV):
                dcw_ref[j:j + 1, :] = jnp.sum(wacc[j], axis=0, keepdims=True)
            dcw_ref[SHORT_CONV:, :] = jnp.zeros((SHALO - SHORT_CONV, GDN_W), F32)

    per = tm // SHALO
    return pl.pallas_call(
        body, name=name, grid=(3, nsteps),
        in_specs=[pl.BlockSpec((tm, GDN_W), lambda p, i: (i, COL_GQ + p)),
                  pl.BlockSpec((SHALO, GDN_W), lambda p, i: (jnp.maximum(i * per - 1, 0), COL_GQ + p)),
                  pl.BlockSpec((SHALO, GDN_W), lambda p, i: (jnp.minimum((i + 1) * per, s_len // SHALO - 1), COL_GQ + p)),
                  pl.BlockSpec((None, tm, GDN_W), lambda p, i: (p, i, 0)),
                  pl.BlockSpec((None, SHALO, GDN_W), lambda p, i: (p, jnp.minimum((i + 1) * per, s_len // SHALO - 1), 0)),
                  pl.BlockSpec((SHALO, GDN_W), lambda p, i: (0, p)),
                  pl.BlockSpec(memory_space=pl.ANY)],
        out_specs=[pl.BlockSpec((tm, GDN_W), lambda p, i: (i, COL_GQ + p)),
                   pl.BlockSpec((SHALO, GDN_W), lambda p, i: (0, p))],
        out_shape=[jax.ShapeDtypeStruct(du.shape, du.dtype), jax.ShapeDtypeStruct((SHALO, 3 * GDN_W), F32)],
        scratch_shapes=[pltpu.VMEM((tm + 2 * SHALO, GDN_W), F32), pltpu.VMEM((ext, GDN_W), F32),
                        pltpu.VMEM((ext, GDN_W), F32), pltpu.VMEM((SHORT_CONV, 8, GDN_W), F32)],
        input_output_aliases={6: 0},
        compiler_params=_params(("arbitrary", "arbitrary")),
    )(u, u, u, dqkv, dqkv, cw, du)


def _gdn_ba_bwd(u, dbg, al, dtb, du, *, name, tm=256):
    s_len = u.shape[0]
    nsteps = s_len // tm
    wpad = IN_WP - COL_BA

    def body(uba, dbg_ref, al_ref, dtb_ref, du_in_ref, du_ref, sums_ref):
        del du_in_ref
        i = pl.program_id(0)

        @pl.when(i == 0)
        def _():
            sums_ref[...] = jnp.zeros_like(sums_ref)

        ba = uba[...]
        dbg = dbg_ref[...]
        lane = lax.broadcasted_iota(jnp.int32, ba.shape, 1)
        is_g = (lane >= GDN_HEADS) & (lane < 2 * GDN_HEADS)
        beta = _sigmoid(ba)
        z = ba + dtb_ref[...]
        ea = jnp.exp(al_ref[...])
        g = -ea * _softplus(z)
        dz = jnp.where(is_g, dbg * (-ea) * _sigmoid(z), 0.0)
        du_ref[:, :128] = jnp.where(lane < GDN_HEADS, dbg * beta * (1.0 - beta), dz).astype(du_ref.dtype)
        du_ref[:, 128:] = jnp.zeros((tm, wpad - 128), du_ref.dtype)
        sums_ref[0:1, :] += jnp.sum(jnp.where(is_g, dbg * g, 0.0), axis=0, keepdims=True)
        sums_ref[1:2, :] += jnp.sum(dz, axis=0, keepdims=True)

    vec = pl.BlockSpec((1, 128), lambda i: (0, 0))
    return pl.pallas_call(
        body, name=name, grid=(nsteps,),
        in_specs=[pl.BlockSpec((tm, 128), lambda i: (i, COL_BA // 128)), pl.BlockSpec((tm, 128), lambda i: (i, 0)),
                  vec, vec, pl.BlockSpec(memory_space=pl.ANY)],
        out_specs=[pl.BlockSpec((tm, wpad), lambda i: (i, COL_BA // wpad)), pl.BlockSpec((8, 128), lambda i: (0, 0))],
        out_shape=[jax.ShapeDtypeStruct(du.shape, du.dtype), jax.ShapeDtypeStruct((8, 128), F32)],
        input_output_aliases={4: 0},
        compiler_params=_params(("arbitrary",)),
    )(u, dbg, al, dtb, du)


def _rope_tables(s_len):
    half = ROPE_DIM // 2
    inv = ROPE_THETA ** (-jnp.arange(half, dtype=F32) / half)
    ang = jnp.arange(s_len, dtype=F32)[:, None] * inv[None, :]
    cos, sin = jnp.cos(ang), jnp.sin(ang)
    one = jnp.ones((s_len, ATT_HD - ROPE_DIM), F32)
    zero = jnp.zeros((s_len, ATT_HD - ROPE_DIM), F32)
    zh = jnp.zeros((s_len, half), F32)
    c = jnp.concatenate([cos, cos, one], axis=1)
    s1 = jnp.concatenate([-sin, zh, zero], axis=1)
    s2 = jnp.concatenate([zh, sin, zero], axis=1)
    return tuple(jnp.concatenate([t, t], axis=1) for t in (c, s1, s2))


def _rope(x, c, s1, s2):
    return x * c + pltpu.roll(x, 128 - ROPE_DIM // 2, 1) * s1 + pltpu.roll(x, ROPE_DIM // 2, 1) * s2


def _rope_t(dy, c, s1, s2):
    return dy * c + pltpu.roll(dy * s1, ROPE_DIM // 2, 1) + pltpu.roll(dy * s2, 128 - ROPE_DIM // 2, 1)


DILATIONS = tuple(d for _, d in DIL_PATTERNS)
VIEW_ROWS = 256


def _to_view(scr, out_ref, dil, dtype):
    nblk, rows, _ = scr.shape
    width = nblk * 128
    for b in range(nblk):
        if dil == 1:
            out_ref[:, 128 * b:128 * (b + 1)] = scr[b].astype(dtype)
            continue
        for r in range(dil):
            out_ref[:, r * width + 128 * b:r * width + 128 * (b + 1)] = (
                scr.at[b][pl.ds(r, rows // dil, stride=dil), :].astype(dtype))


def _from_view(in_ref, scr, dil):
    nblk, rows, _ = scr.shape
    width = nblk * 128
    for b in range(nblk):
        for r in range(dil):
            scr.at[b][pl.ds(r, rows // dil, stride=dil), :] = in_ref[:, r * width + 128 * b:r * width + 128 * (b + 1)]


def _view_spec(dil, width, tm=VIEW_ROWS):
    return pl.BlockSpec((tm // dil, dil * width), lambda i: (i, 0))


def _view_shape(s_len, dil, width, dtype):
    return jax.ShapeDtypeStruct((s_len // dil, dil * width), dtype)


def _att_prep_fwd(u, tabs, *, name):
    s_len = u.shape[0]
    tm = VIEW_ROWS
    scale = ATT_HD ** -0.5
    nblk = ATT_W // 128

    def body(uq, uk, uv, c_ref, s1_ref, s2_ref, *rest):
        outs, scr = rest[:-1], rest[-1]
        c, s1, s2 = c_ref[...], s1_ref[...], s2_ref[...]
        for part, src in enumerate((uq, uk, uv)):
            for b in range(nblk):
                xb = src[:, 128 * b:128 * (b + 1)]
                if part == 0:
                    xb = _rope(xb, c, s1, s2) * scale
                elif part == 1:
                    xb = _rope(xb, c, s1, s2)
                scr[b] = xb
            for gi, dil in enumerate(DILATIONS):
                _to_view(scr, outs[3 * gi + part], dil, BF16)

    tab = pl.BlockSpec((tm, 128), lambda i: (i, 0))
    outs = pl.pallas_call(
        body, name=name, grid=(s_len // tm,),
        in_specs=[pl.BlockSpec((tm, ATT_W), lambda i, col=COL_AQ + j: (i, col)) for j in range(3)] + [tab] * 3,
        out_specs=[_view_spec(dil, ATT_W) for dil in DILATIONS for _ in range(3)],
        out_shape=[_view_shape(s_len, dil, ATT_W, BF16) for dil in DILATIONS for _ in range(3)],
        scratch_shapes=[pltpu.VMEM((nblk, tm, 128), F32)],
        compiler_params=_params(("parallel",)),
    )(u, u, u, *tabs)
    return [outs[3 * gi:3 * gi + 3] for gi in range(len(DILATIONS))]


def _stack_heads(x):
    lane = lax.broadcasted_iota(jnp.int32, (1, 128), 1)
    zero = jnp.zeros_like(x)
    return jnp.concatenate([jnp.where(lane < ATT_HD, x, zero), jnp.where(lane >= ATT_HD, x, zero)], axis=0)


def _att_fwd(qr, kr, vb, dil, *, name):
    lr = qr.shape[0]
    nb = lr // ATT_BLOCK
    blk = ATT_BLOCK

    def body(q_ref, kp_ref, kc_ref, vp_ref, vc_ref, o_ref, lse_ref):
        n = pl.program_id(1)
        qi = lax.broadcasted_iota(jnp.int32, (blk, 2 * blk), 0)
        ki = lax.broadcasted_iota(jnp.int32, (blk, 2 * blk), 1)
        dist = qi + blk - ki
        valid = (dist >= 0) & (dist <= blk) & ((ki >= blk) | (n > 0))
        valid = jnp.concatenate([valid, valid], axis=0)
        lane = lax.broadcasted_iota(jnp.int32, (blk, 128), 1)
        lse_all = jnp.zeros((blk, 128), F32)
        for hp in range(ATT_HEADS // 2):
            bs = slice(128 * hp, 128 * (hp + 1))
            kb = jnp.concatenate([kp_ref[:, bs], kc_ref[:, bs]], axis=0)
            vv = jnp.concatenate([vp_ref[:, bs], vc_ref[:, bs]], axis=0)
            s = jnp.where(valid, _nt(_stack_heads(q_ref[:, bs]), kb), NEG_INF)
            m = jnp.max(s, axis=-1, keepdims=True)
            p = jnp.exp(s - m)
            l = jnp.sum(p, axis=-1, keepdims=True)
            o = _nn((p * (1.0 / l)).astype(BF16), vv)
            o_ref[:, bs] = jnp.where(lane < ATT_HD, o[:blk], o[blk:])
            lse = m + jnp.log(l)
            lse_all = (lse_all + jnp.where(lane == 2 * hp, lse[:blk], 0.0)
                       + jnp.where(lane == 2 * hp + 1, lse[blk:], 0.0))
        lse_ref[...] = lse_all

    cur = pl.BlockSpec((blk, ATT_W), lambda r, n: (n, r))
    prev = pl.BlockSpec((blk, ATT_W), lambda r, n: (jnp.maximum(n - 1, 0), r))
    return pl.pallas_call(
        body, name=name, grid=(dil, nb), in_specs=[cur, prev, cur, prev, cur],
        out_specs=[cur, pl.BlockSpec((blk, 128), lambda r, n: (n, r))],
        out_shape=[jax.ShapeDtypeStruct(qr.shape, F32), jax.ShapeDtypeStruct((lr, dil * 128), F32)],
        compiler_params=_params(("parallel", "parallel")),
    )(qr, kr, kr, vb, vb)


def _att_bwd(qr, kr, vb, do, lse, delta, dil, *, name):
    lr = qr.shape[0]
    nb = lr // ATT_BLOCK
    blk = ATT_BLOCK

    def body(q_ref, kp_ref, kc_ref, vp_ref, vc_ref, do_ref, lse_ref, dl_ref, dq_ref, dk_ref, dv_ref, carry):
        n = pl.program_id(1)

        @pl.when(n == 0)
        def _():
            carry[...] = jnp.zeros_like(carry)

        @pl.when(n == nb)
        def _():
            dk_ref[...] = carry[0]
            dv_ref[...] = carry[1]

        @pl.when(n < nb)
        def _():
            qi = lax.broadcasted_iota(jnp.int32, (blk, 2 * blk), 0)
            ki = lax.broadcasted_iota(jnp.int32, (blk, 2 * blk), 1)
            dist = qi + blk - ki
            valid = (dist >= 0) & (dist <= blk) & ((ki >= blk) | (n > 0))
            valid = jnp.concatenate([valid, valid], axis=0)
            lane = lax.broadcasted_iota(jnp.int32, (blk, 128), 1)
            for hp in range(ATT_HEADS // 2):
                bs = slice(128 * hp, 128 * (hp + 1))
                kb = jnp.concatenate([kp_ref[:, bs], kc_ref[:, bs]], axis=0)
                vv = jnp.concatenate([vp_ref[:, bs], vc_ref[:, bs]], axis=0)
                q2 = _stack_heads(q_ref[:, bs])
                do2 = _stack_heads(do_ref[:, bs])
                lse2 = jnp.concatenate([lse_ref[:, 2 * hp:2 * hp + 1], lse_ref[:, 2 * hp + 1:2 * hp + 2]], axis=0)
                dl2 = jnp.concatenate([dl_ref[:, 2 * hp:2 * hp + 1], dl_ref[:, 2 * hp + 1:2 * hp + 2]], axis=0)
                p = jnp.where(valid, jnp.exp(_nt(q2, kb) - lse2), 0.0)
                ds16 = (p * (_nt(do2, vv) - dl2)).astype(BF16)
                dq2 = _nn(ds16, kb)
                dv_acc = _tn(p.astype(BF16), do2)
                dk_acc = _tn(ds16, q2)
                dq_ref[:, bs] = jnp.where(lane < ATT_HD, dq2[:blk], dq2[blk:])
                dk_ref[:, bs] = carry[0, :, bs] + dk_acc[:blk]
                dv_ref[:, bs] = carry[1, :, bs] + dv_acc[:blk]
                carry[0, :, bs] = dk_acc[blk:]
                carry[1, :, bs] = dv_acc[blk:]

    def at(n):
        return jnp.minimum(n, nb - 1)

    cur = pl.BlockSpec((blk, ATT_W), lambda r, n: (at(n), r))
    prev = pl.BlockSpec((blk, ATT_W), lambda r, n: (jnp.maximum(at(n) - 1, 0), r))
    nar = pl.BlockSpec((blk, 128), lambda r, n: (at(n), r))
    late = pl.BlockSpec((blk, ATT_W), lambda r, n: (jnp.maximum(n - 1, 0), r))
    out = jax.ShapeDtypeStruct(qr.shape, F32)
    return pl.pallas_call(
        body, name=name, grid=(dil, nb + 1), in_specs=[cur, prev, cur, prev, cur, cur, nar, nar],
        out_specs=[cur, late, late], out_shape=[out, out, out],
        scratch_shapes=[pltpu.VMEM((2, blk, ATT_W), F32)],
        compiler_params=_params(("parallel", "arbitrary")),
    )(qr, kr, kr, vb, vb, do, lse, delta)


def _att_prep_bwd(dgroups, tabs, du, *, name):
    s_len = du.shape[0]
    tm = VIEW_ROWS
    scale = ATT_HD ** -0.5
    nblk = ATT_W // 128
    ng = len(DILATIONS)

    def body(*refs):
        grads = refs[:3 * ng]
        c_ref, s1_ref, s2_ref, _, du_ref = refs[3 * ng:3 * ng + 5]
        scrs = refs[3 * ng + 5:]
        c, s1, s2 = c_ref[...], s1_ref[...], s2_ref[...]
        for part in range(3):
            for gi, dil in enumerate(DILATIONS):
                if dil > 1:
                    _from_view(grads[3 * gi + part], scrs[gi], dil)
            for b in range(nblk):
                tot = None
                for gi, dil in enumerate(DILATIONS):
                    term = grads[3 * gi + part][:, 128 * b:128 * (b + 1)] if dil == 1 else scrs[gi][b]
                    tot = term if tot is None else tot + term
                if part == 0:
                    tot = _rope_t(tot * scale, c, s1, s2)
                elif part == 1:
                    tot = _rope_t(tot, c, s1, s2)
                du_ref[:, ATT_W * part + 128 * b:ATT_W * part + 128 * (b + 1)] = tot.astype(du_ref.dtype)

    tab = pl.BlockSpec((tm, 128), lambda i: (i, 0))
    return pl.pallas_call(
        body, name=name, grid=(s_len // tm,),
        in_specs=[_view_spec(dil, ATT_W) for dil in DILATIONS for _ in range(3)] + [tab] * 3
        + [pl.BlockSpec(memory_space=pl.ANY)],
        out_specs=pl.BlockSpec((tm, 3 * ATT_W), lambda i: (i, COL_AQ // 3)),
        out_shape=jax.ShapeDtypeStruct(du.shape, du.dtype),
        scratch_shapes=[pltpu.VMEM((nblk, tm, 128), F32) for _ in DILATIONS],
        input_output_aliases={3 * ng + 3: 0},
        compiler_params=_params(("parallel",)),
    )(*[a for g in dgroups for a in g], *tabs, du)


def _head_weights(w, b):
    lane = lax.broadcasted_iota(jnp.int32, (1, 128), 1)
    return jnp.where(lane < ATT_HD, w[:, 2 * b:2 * b + 1], w[:, 2 * b + 1:2 * b + 2])


def _assemble_fwd(pw, u, o_gdn, gnw, o_groups, lse_groups, *, name):
    s_len = u.shape[0]
    tm = VIEW_ROWS
    c = CONV_CH
    nblk = ATT_W // 128
    ng = len(DILATIONS)

    def body(*refs):
        pw_ref, cg_ref, z_ref, ag_ref, og_ref, gnw_ref = refs[:6]
        o_refs, l_refs = refs[6:6 + ng], refs[6 + ng:6 + 2 * ng]
        y_ref, oa_ref = refs[6 + 2 * ng:8 + 2 * ng]
        lse_outs = refs[8 + 2 * ng:8 + 3 * ng]
        o_scr, l_scr = refs[8 + 3 * ng:8 + 4 * ng], refs[8 + 4 * ng:8 + 5 * ng]
        lse_scr = refs[8 + 5 * ng]
        y_ref[:, :c] = (pw_ref[...] * _silu(cg_ref[...])).astype(BF16)
        gw = gnw_ref[...]
        for h in range(GDN_HEADS):
            hs = slice(GDN_D * h, GDN_D * (h + 1))
            oh = og_ref[:, hs]
            yn = oh * lax.rsqrt(jnp.mean(oh * oh, axis=-1, keepdims=True) + 1e-6) * gw
            y_ref[:, c + GDN_D * h:c + GDN_D * (h + 1)] = (yn * _silu(z_ref[:, hs])).astype(BF16)
        for gi, dil in enumerate(DILATIONS):
            if dil > 1:
                _from_view(o_refs[gi], o_scr[gi], dil)
                _from_view(l_refs[gi], l_scr[gi], dil)
        ls = [l_refs[gi][...] if dil == 1 else l_scr[gi][0] for gi, dil in enumerate(DILATIONS)]
        m = functools.reduce(jnp.maximum, ls)
        es = [jnp.exp(l - m) for l in ls]
        den = functools.reduce(lambda a, b: a + b, es)
        lse_scr[0] = m + jnp.log(den)
        ws = [e / den for e in es]
        for b in range(nblk):
            bs = slice(128 * b, 128 * (b + 1))
            o = None
            for gi, dil in enumerate(DILATIONS):
                term = _head_weights(ws[gi], b) * (o_refs[gi][:, bs] if dil == 1 else o_scr[gi][b])
                o = term if o is None else o + term
            oa_ref[:, bs] = o
            y_ref[:, c + GDN_W + 128 * b:c + GDN_W + 128 * (b + 1)] = (o * _silu(ag_ref[:, bs])).astype(BF16)
        for gi, dil in enumerate(DILATIONS):
            _to_view(lse_scr, lse_outs[gi], dil, F32)

    wide = pl.BlockSpec((tm, 768), lambda i: (i, 0))
    return pl.pallas_call(
        body, name=name, grid=(s_len // tm,),
        in_specs=[pl.BlockSpec((tm, c), lambda i: (i, 0)), pl.BlockSpec((tm, c), lambda i: (i, 1024 // c)),
                  pl.BlockSpec((tm, 768), lambda i: (i, COL_GQ + 3)), pl.BlockSpec((tm, 768), lambda i: (i, COL_AQ + 3)),
                  wide, pl.BlockSpec((1, 128), lambda i: (0, 0))]
        + [_view_spec(dil, ATT_W) for dil in DILATIONS] + [_view_spec(dil, 128) for dil in DILATIONS],
        out_specs=[pl.BlockSpec((tm, D_MODEL), lambda i: (i, 0)), wide] + [_view_spec(dil, 128) for dil in DILATIONS],
        out_shape=[jax.ShapeDtypeStruct((s_len, D_MODEL), BF16), jax.ShapeDtypeStruct((s_len, ATT_W), F32)]
        + [_view_shape(s_len, dil, 128, F32) for dil in DILATIONS],
        scratch_shapes=[pltpu.VMEM((nblk, tm, 128), F32) for _ in DILATIONS]
        + [pltpu.VMEM((1, tm, 128), F32) for _ in DILATIONS] + [pltpu.VMEM((1, tm, 128), F32)],
        compiler_params=_params(("parallel",)),
    )(pw, u, u, u, o_gdn, gnw, *o_groups, *lse_groups)


def _assemble_bwd(dy, pw, u, o_gdn, gnw, o_att, *, name):
    s_len = u.shape[0]
    tm = VIEW_ROWS
    c = CONV_CH
    nsteps = s_len // tm
    nblk = ATT_W // 128
    ng = len(DILATIONS)

    def body(dy_ref, pw_ref, cg_ref, z_ref, ag_ref, og_ref, gnw_ref, oa_ref,
             du_ref, dpw_ref, dog_ref, dgw_ref, *rest):
        do_outs, dl_outs = rest[:ng], rest[ng:2 * ng]
        acc_ref, do_scr, dl_scr = rest[2 * ng:]
        i = pl.program_id(0)

        @pl.when(i == 0)
        def _():
            acc_ref[...] = jnp.zeros_like(acc_ref)

        du_ref[...] = jnp.zeros_like(du_ref)
        dyc = dy_ref[:, :c]
        cg = cg_ref[...]
        dpw_ref[...] = dyc * _silu(cg)
        du_ref[:, 1024:1024 + c] = (dyc * pw_ref[...] * _dsilu(cg)).astype(BF16)
        gw = gnw_ref[...]
        dgw = jnp.zeros((8, 128), F32)
        for h in range(GDN_HEADS):
            hs = slice(GDN_D * h, GDN_D * (h + 1))
            oh = og_ref[:, hs]
            zh = z_ref[:, hs]
            dyh = dy_ref[:, c + GDN_D * h:c + GDN_D * (h + 1)]
            r = lax.rsqrt(jnp.mean(oh * oh, axis=-1, keepdims=True) + 1e-6)
            xn = oh * r
            dyn = dyh * _silu(zh)
            du_ref[:, GDN_W * (COL_GQ + 3) + GDN_D * h:GDN_W * (COL_GQ + 3) + GDN_D * (h + 1)] = (
                dyh * xn * gw * _dsilu(zh)).astype(BF16)
            dgw = dgw + (dyn * xn).reshape(tm // 8, 8, 128).sum(axis=0)
            dxn = dyn * gw
            dog_ref[:, hs] = r * (dxn - xn * jnp.mean(dxn * xn, axis=-1, keepdims=True))
        acc_ref[...] += dgw
        lane = lax.broadcasted_iota(jnp.int32, (tm, 128), 1)
        delta = jnp.zeros((tm, 128), F32)
        for b in range(ATT_W // 128):
            bs = slice(128 * b, 128 * (b + 1))
            dya = dy_ref[:, c + GDN_W + 128 * b:c + GDN_W + 128 * (b + 1)]
            ag = ag_ref[:, bs]
            oa = oa_ref[:, bs]
            do = dya * _silu(ag)
            do_scr[b] = do
            du_ref[:, ATT_W * (COL_AQ + 3) + 128 * b:ATT_W * (COL_AQ + 3) + 128 * (b + 1)] = (
                dya * oa * _dsilu(ag)).astype(BF16)
            prod = do * oa
            lo = jnp.sum(jnp.where(lane < ATT_HD, prod, 0.0), axis=-1, keepdims=True)
            hi = jnp.sum(jnp.where(lane >= ATT_HD, prod, 0.0), axis=-1, keepdims=True)
            delta = delta + jnp.where(lane == 2 * b, lo, 0.0) + jnp.where(lane == 2 * b + 1, hi, 0.0)
        dl_scr[0] = delta
        for gi, dil in enumerate(DILATIONS):
            _to_view(do_scr, do_outs[gi], dil, BF16)
            _to_view(dl_scr, dl_outs[gi], dil, F32)

        @pl.when(i == nsteps - 1)
        def _():
            dgw_ref[...] = jnp.sum(acc_ref[...], axis=0, keepdims=True)

    wide = pl.BlockSpec((tm, 768), lambda i: (i, 0))
    vec = pl.BlockSpec((1, 128), lambda i: (0, 0))
    outs = pl.pallas_call(
        body, name=name, grid=(nsteps,),
        in_specs=[pl.BlockSpec((tm, D_MODEL), lambda i: (i, 0)), pl.BlockSpec((tm, c), lambda i: (i, 0)),
                  pl.BlockSpec((tm, c), lambda i: (i, 1024 // c)), pl.BlockSpec((tm, 768), lambda i: (i, COL_GQ + 3)),
                  pl.BlockSpec((tm, 768), lambda i: (i, COL_AQ + 3)), wide, vec, wide],
        out_specs=[pl.BlockSpec((tm, IN_WP), lambda i: (i, 0)), pl.BlockSpec((tm, c), lambda i: (i, 0)), wide, vec]
        + [_view_spec(dil, ATT_W) for dil in DILATIONS] + [_view_spec(dil, 128) for dil in DILATIONS],
        out_shape=[jax.ShapeDtypeStruct((s_len, IN_WP), BF16), jax.ShapeDtypeStruct((s_len, c), F32),
                   jax.ShapeDtypeStruct((s_len, GDN_W), F32), jax.ShapeDtypeStruct((1, 128), F32)]
        + [_view_shape(s_len, dil, ATT_W, BF16) for dil in DILATIONS]
        + [_view_shape(s_len, dil, 128, F32) for dil in DILATIONS],
        scratch_shapes=[pltpu.VMEM((8, 128), F32), pltpu.VMEM((nblk, tm, 128), F32), pltpu.VMEM((1, tm, 128), F32)],
        compiler_params=_params(("arbitrary",)),
    )(dy, pw, u, u, u, o_gdn, gnw, o_att)
    return outs[:4], outs[4:4 + ng], outs[4 + ng:]


def _layer_fwd(x, p, tabs):
    h = _rms_fwd(x, p["norm_w"], name="rms_fwd")
    u = _matmul(h, p["wp"], name="in_proj", tk=2048)
    conv, sw = _conf_fwd(u, p["dw_w"], p["dw_b"], p["ln_w"], p["ln_b"], name="conf_fwd")
    pw = _matmul(sw, p["pw_w"], name="conf_pw")
    qn, kn, vc, bg = _gdn_prep_fwd(u, p["cw"], p["al"], p["dtb"], name="gdn_prep_fwd")
    wk, wv, qd, kd, pm, t, g_cum = _gdn_intra_fwd(qn, kn, vc, bg, name="gdn_intra_fwd")
    o_gdn, vn, sprev = _gdn_scan_fwd(wk, wv, qd, kd, pm, g_cum, name="gdn_scan_fwd")
    qkv = _att_prep_fwd(u, tabs, name="att_prep_fwd")
    groups = [_att_fwd(*qkv[gi], dil, name=f"att_fwd_d{dil}") for gi, dil in enumerate(DILATIONS)]
    outs = _assemble_fwd(pw, u, o_gdn, p["gnw"], [g[0] for g in groups], [g[1] for g in groups],
                         name="assemble_fwd")
    y, o_att, lse = outs[0], outs[1], outs[2:]
    x_next = _matmul(y, p["wout"], add=x, name="out_proj", tk=2048)
    saved = dict(x=x, h=h, u=u, conv=conv, sw=sw, pw=pw, qn=qn, kn=kn, vc=vc, bg=bg, wk=wk, qd=qd, kd=kd, pm=pm,
                 t=t, g_cum=g_cum, vn=vn, sprev=sprev, o_gdn=o_gdn, qkv=qkv, o_att=o_att, lse=lse, y=y)
    return x_next, saved


def _layer_bwd(dx_out, s, p, tabs, layer, big):
    dy = _matmul(dx_out, p["wout"], tb=True, name="out_proj_dy", tk=2048)
    d_wout = _matmul(s["y"], dx_out, ta=True, name="out_proj_dw", tk=2048, stack=(big[1], layer, DEPTH))
    (du, dpw, dog, dgw), do_views, dl_views = _assemble_bwd(dy, s["pw"], s["u"], s["o_gdn"], p["gnw"], s["o_att"],
                                                            name="assemble_bwd")
    dsw = _matmul(dpw, p["pw_w"], tb=True, name="conf_pw_dx")
    d_pw_w = _matmul(s["sw"], dpw, ta=True, name="conf_pw_dw", stack=(big[2], layer, DEPTH))
    dconv, ln_sums = _conf_bwd_ln(dsw, s["conv"], p["ln_w"], p["ln_b"], name="conf_bwd_ln")
    du, d_dw_w = _conf_bwd_conv(s["u"], dconv, p["dw_w"], du, name="conf_bwd_conv")
    dvn, ds_all = _gdn_scan_bwd(dog, s["wk"], s["qd"], s["kd"], s["pm"], s["g_cum"], name="gdn_scan_bwd")
    dqkv, dbg = _gdn_intra_bwd(s["qn"], s["kn"], s["vc"], s["bg"], s["g_cum"], s["t"], dog, dvn, s["vn"],
                               s["sprev"], ds_all, name="gdn_intra_bwd")
    du, d_cw = _gdn_prep_bwd(s["u"], dqkv, p["cw"], du, name="gdn_prep_bwd")
    du, ba_sums = _gdn_ba_bwd(s["u"], dbg, p["al"], p["dtb"], du, name="gdn_ba_bwd")
    dgroups = []
    for gi, dil in enumerate(DILATIONS):
        args = (*s["qkv"][gi], do_views[gi], s["lse"][gi], dl_views[gi], dil)
        dgroups.append(_att_bwd(*args, name=f"att_bwd_d{dil}"))
    du = _att_prep_bwd(dgroups, tabs, du, name="att_prep_bwd")
    dh = _matmul(du, p["wp"], tb=True, name="in_proj_dx", tk=4096)
    d_wp = _matmul(s["h"], du, ta=True, name="in_proj_dw", tk=4096, stack=(big[0], layer, DEPTH))
    dx, d_norm_w = _rms_bwd(s["x"], dh, p["norm_w"], dx_out, name="rms_bwd")
    small = dict(norm_w=d_norm_w, gnw=dgw, ln_sums=ln_sums, dw_w=d_dw_w, cw=d_cw, ba_sums=ba_sums)
    return dx, (d_wp, d_wout, d_pw_w), small


def _trunk(x, target, params, final_norm_w):
    tabs = _rope_tables(x.shape[0])
    layers = [{k: v[l] for k, v in params.items()} for l in range(DEPTH)]
    saved = []
    for p in layers:
        x, s = _layer_fwd(x, p, tabs)
        saved.append(s)
    dx, d_final, loss = _loss_head(x, final_norm_w, target, name="loss_head")
    big = (None, None, None)
    small = [None] * DEPTH
    for l in reversed(range(DEPTH)):
        dx, big, small[l] = _layer_bwd(dx, saved[l], layers[l], tabs, l, big)
    grads = {k: jnp.stack([sm[k] for sm in small]) for k in small[0]}
    grads.update(wp=big[0], wout=big[1], pw_w=big[2])
    return loss[0, 0], dx, grads, d_final


ANY = pl.BlockSpec(memory_space=pl.ANY)


def _position():
    return lax.axis_index("x"), lax.axis_index("y"), lax.axis_index("c")


def _other_chips(x, y):
    return [(1 - x, y), (x, 1 - y), (1 - x, 1 - y)]


def _gather_chips(shards, *, name):
    n = len(shards)
    kinds = 12

    def body(*refs):
        ins, outs = refs[:n], refs[n:2 * n]
        send, recv = refs[2 * n:]
        x, y, c = _position()
        me, sib = (x, y, c), (x, y, 1 - c)
        xn, yn, dg = (1 - x, y), (x, 1 - y), (1 - x, 1 - y)
        pa, pb = 2 * c, 2 * c + 1

        def copy(k, a, chip, layer, to, src=None):
            dst = outs[a].at[2 * chip[0] + chip[1], pl.ds(layer, 1)]
            return pltpu.make_async_remote_copy(
                src_ref=dst if src is None else src, dst_ref=dst, send_sem=send.at[k * n + a],
                recv_sem=recv.at[k * n + a], device_id=to, device_id_type=MESH)

        def own(k, a, layer, chip):
            return copy(k, a, (x, y), layer, (*chip, c), src=ins[a].at[pl.ds(layer, 1)])

        sends = []
        for a in range(n):
            sends += [own(0, a, pa, xn), own(1, a, pb, yn), own(2, a, pb, xn), own(3, a, pa, yn)]
        for cp in sends:
            cp.start()
        arrivals = [(1, yn, pb, (4, xn)), (0, xn, pa, (5, yn)), (2, xn, pb, None), (3, yn, pa, None),
                    (4, dg, pb, None), (5, dg, pa, None)]
        for a in range(n):
            for j, (k, chip, layer, onward) in enumerate(arrivals):
                copy(k, a, chip, layer, me).wait_recv()
                if onward is not None:
                    cp = copy(onward[0], a, chip, layer, (*onward[1], c))
                    cp.start()
                    sends.append(cp)
                cp = copy(6 + j, a, chip, layer, sib)
                cp.start()
                sends.append(cp)
        for a in range(n):
            for j, (k, chip, layer, onward) in enumerate(arrivals):
                copy(6 + j, a, chip, layer + 2 - 4 * c, me).wait_recv()
        for cp in sends:
            cp.wait_send()

    return pl.pallas_call(
        body, name=name, in_specs=[ANY] * n, out_specs=[ANY] * n,
        out_shape=[jax.ShapeDtypeStruct((4,) + s.shape, s.dtype) for s in shards],
        scratch_shapes=[pltpu.SemaphoreType.DMA((kinds * n,)), pltpu.SemaphoreType.DMA((kinds * n,))],
    )(*shards)


def _to_sibling(arrs, *, name):
    n = len(arrs)

    def body(*refs):
        ins, outs = refs[:n], refs[n:2 * n]
        send, recv = refs[2 * n:]
        x, y, c = _position()
        cps = [pltpu.make_async_remote_copy(src_ref=ins[a], dst_ref=outs[a], send_sem=send.at[a],
                                            recv_sem=recv.at[a], device_id=(x, y, 1 - c), device_id_type=MESH)
               for a in range(n)]
        for cp in cps:
            cp.start()
        for cp in cps:
            cp.wait()

    return pl.pallas_call(
        body, name=name, in_specs=[ANY] * n, out_specs=[ANY] * n,
        out_shape=[jax.ShapeDtypeStruct(a.shape, a.dtype) for a in arrs],
        scratch_shapes=[pltpu.SemaphoreType.DMA((n,)), pltpu.SemaphoreType.DMA((n,))],
    )(*arrs)


def _to_chips(arrs, *, name):
    n = len(arrs)

    def body(*refs):
        ins, outs = refs[:n], refs[n:2 * n]
        send, recv = refs[2 * n:]
        x, y, c = _position()
        cps = [pltpu.make_async_remote_copy(
            src_ref=ins[a].at[2 * chip[0] + chip[1]], dst_ref=outs[a].at[j], send_sem=send.at[j * n + a],
            recv_sem=recv.at[j * n + a], device_id=(*chip, c), device_id_type=MESH)
            for j, chip in enumerate(_other_chips(x, y)) for a in range(n)]
        for cp in cps:
            cp.start()
        for cp in cps:
            cp.wait()

    return pl.pallas_call(
        body, name=name, in_specs=[ANY] * n, out_specs=[ANY] * n,
        out_shape=[jax.ShapeDtypeStruct((3,) + a.shape[1:], a.dtype) for a in arrs],
        scratch_shapes=[pltpu.SemaphoreType.DMA((3 * n,)), pltpu.SemaphoreType.DMA((3 * n,))],
    )(*arrs)


def _join_halves(fulls, *, name):
    n = len(fulls)

    def body(*refs):
        ins, outs = refs[:n], refs[n:2 * n]
        send, recv = refs[2 * n:]
        x, y, c = _position()

        def copy(a, rows):
            return pltpu.make_async_remote_copy(
                src_ref=ins[a].at[rows], dst_ref=outs[a].at[rows], send_sem=send.at[a], recv_sem=recv.at[a],
                device_id=(x, y, 1 - c), device_id_type=MESH)

        cps = [copy(a, pl.ds(2 * c, 2)) for a in range(n)]
        for cp in cps:
            cp.start()
        for a in range(n):
            cps[a].wait_send()
            copy(a, pl.ds(2 * (1 - c), 2)).wait_recv()

    return pl.pallas_call(
        body, name=name, in_specs=[ANY] * n, out_specs=[ANY] * n,
        out_shape=[jax.ShapeDtypeStruct(f.shape, f.dtype) for f in fulls],
        scratch_shapes=[pltpu.SemaphoreType.DMA((n,)), pltpu.SemaphoreType.DMA((n,))],
        input_output_aliases={a: a for a in range(n)},
    )(*fulls)


def _allreduce_small(packed, *, name):
    rows = packed.shape[0]
    ndev = 8

    def body(x_ref, sum_ref, all_ref, send, recv, lsem):
        x, y, c = _position()
        me, sib = (x, y, c), (x, y, 1 - c)
        chips = _other_chips(x, y)

        def blk(px, py, pc):
            return all_ref.at[pl.ds((4 * px + 2 * py + pc) * rows, rows), :]

        def copy(k, block, to, src=None):
            return pltpu.make_async_remote_copy(
                src_ref=blk(*block) if src is None else src, dst_ref=blk(*block), send_sem=send.at[k],
                recv_sem=recv.at[k], device_id=to, device_id_type=MESH)

        mine = pltpu.make_async_copy(x_ref, blk(*me), lsem)
        mine.start()
        first = [copy(0, me, sib, src=x_ref)] + [copy(1 + j, me, (*chip, c), src=x_ref) for j, chip in enumerate(chips)]
        for cp in first:
            cp.start()
        passed = [copy(4 + j, (*chip, c), sib) for j, chip in enumerate(chips)]
        for j, chip in enumerate(chips):
            copy(1 + j, (*chip, c), me).wait_recv()
            passed[j].start()
        copy(0, sib, me).wait_recv()
        for j, chip in enumerate(chips):
            copy(4 + j, (*chip, 1 - c), me).wait_recv()
        for cp in first + passed:
            cp.wait_send()
        mine.wait()
        acc = all_ref[0:rows, :]
        for d in range(1, ndev):
            acc = acc + all_ref[d * rows:(d + 1) * rows, :]
        sum_ref[...] = acc

    vm = pl.BlockSpec(memory_space=pltpu.VMEM)
    return pl.pallas_call(
        body, name=name, in_specs=[vm], out_specs=vm, out_shape=jax.ShapeDtypeStruct((rows, 128), F32),
        scratch_shapes=[pltpu.VMEM((ndev * rows, 128), F32), pltpu.SemaphoreType.DMA((7,)),
                        pltpu.SemaphoreType.DMA((7,)), pltpu.SemaphoreType.DMA],
        compiler_params=pltpu.CompilerParams(vmem_limit_bytes=VMEM_LIMIT),
    )(packed)


def _pack(arrs):
    flat = jnp.concatenate([a.reshape(-1) for a in arrs])
    pad = (-flat.shape[0]) % 1024
    return jnp.pad(flat, (0, pad)).reshape(-1, 128)


def _unpack(packed, shapes):
    flat = packed.reshape(-1)
    out, pos = [], 0
    for s in shapes:
        size = math.prod(s)
        out.append(flat[pos:pos + size].reshape(s))
        pos += size
    return out


def _pad_cols(w):
    zeros = jnp.zeros(w.shape[:-1] + (IN_WP - IN_W,), w.dtype)
    return jnp.concatenate([w[..., :ORIG_BA], w[..., ORIG_ATT:], w[..., ORIG_BA:ORIG_ATT], zeros], axis=-1)


def _chip_cols(j):
    per = IN_W // 4
    lo, hi = j * per, (j + 1) * per
    out = []
    for o0, o1, p0 in ((0, ORIG_BA, 0), (ORIG_BA, ORIG_ATT, COL_BA), (ORIG_ATT, IN_W, ORIG_BA)):
        a, b = max(lo, o0), min(hi, o1)
        if a < b:
            out.append((p0 + a - o0, p0 + b - o0))
    return out


def _shards_to_padded(g):
    pieces = []
    for j in range(4):
        loc = 0
        for p0, p1 in _chip_cols(j):
            pieces.append((p0, g[j][..., loc:loc + p1 - p0]))
            loc += p1 - p0
    pieces.sort(key=lambda t: t[0])
    zeros = jnp.zeros(g.shape[1:-1] + (IN_WP - IN_W,), g.dtype)
    return jnp.concatenate([p for _, p in pieces] + [zeros], axis=-1)


def _padded_to_shards(g, dtype):
    return jnp.stack([jnp.concatenate([g[..., p0:p1] for p0, p1 in _chip_cols(j)], axis=-1).astype(dtype)
                      for j in range(4)])


def _unpad_cols(w):
    n_att = IN_W - ORIG_ATT
    return jnp.concatenate([w[..., :ORIG_BA], w[..., COL_BA:COL_BA + ORIG_ATT - ORIG_BA],
                            w[..., ORIG_BA:ORIG_BA + n_att]], axis=-1)


def _lanes(v, first):
    return jnp.pad(v, ((0, 0), (first, 128 - first - v.shape[1])))[:, None, :]


def _by_chip(g, axis):
    shape = g.shape[:axis] + (4, g.shape[axis] // 4) + g.shape[axis + 1:]
    return jnp.moveaxis(g.reshape(shape), axis, 0)


def kernel(x, norm_w, w_in, conv_qkv_w, a_log, dt_bias, gdn_norm_w, conf_dw_w, conf_dw_b, conf_ln_w, conf_ln_b, conf_pw_w, w_out, final_norm_w, loss_target, m_norm_w, m_w_in, m_conv_qkv_w, m_a_log, m_dt_bias, m_gdn_norm_w, m_conf_dw_w, m_conf_dw_b, m_conf_ln_w, m_conf_ln_b, m_conf_pw_w, m_w_out, m_final_norm_w, v_norm_w, v_w_in, v_conv_qkv_w, v_a_log, v_dt_bias, v_gdn_norm_w, v_conf_dw_w, v_conf_dw_b, v_conf_ln_w, v_conf_ln_b, v_conf_pw_w, v_w_out, v_final_norm_w):
    xi, yi, ci = _position()
    chip = 2 * xi + yi

    shards = [w_in.astype(BF16), w_out.astype(BF16), conf_pw_w.astype(BF16), conv_qkv_w, conf_dw_w]
    g_in, g_out, g_pw, g_cw, g_dw = [
        lax.dynamic_update_slice_in_dim(g, s[None], chip, axis=0)
        for g, s in zip(_gather_chips(shards, name="gather_weights"), shards)]
    cw_full = jnp.moveaxis(g_cw, 0, 2).reshape(DEPTH, SHORT_CONV, 3 * GDN_W)
    dw_full = jnp.moveaxis(g_dw, 0, 2).reshape(DEPTH, CONV_WIDTH, CONV_CH)
    params = dict(
        norm_w=norm_w[:, None, :],
        wp=_shards_to_padded(g_in),
        wout=jnp.moveaxis(g_out, 0, 1).reshape(DEPTH, D_MODEL, D_MODEL),
        pw_w=jnp.moveaxis(g_pw, 0, 1).reshape(DEPTH, CONV_CH, CONV_CH),
        cw=jnp.pad(cw_full, ((0, 0), (0, SHALO - SHORT_CONV), (0, 0))),
        dw_w=jnp.pad(dw_full, ((0, 0), (0, HALO - CONV_WIDTH), (0, 0))),
        al=_lanes(a_log, GDN_HEADS), dtb=_lanes(dt_bias, GDN_HEADS), gnw=gdn_norm_w[:, None, :],
        dw_b=conf_dw_b[:, None, :], ln_w=conf_ln_w[:, None, :], ln_b=conf_ln_b[:, None, :],
    )

    loss_part, grad_x, grads, d_final = _trunk(x[0], loss_target[0], params, final_norm_w[None, :])
    loss = lax.psum(loss_part, ("x", "y", "c"))

    def half_by_chip(first, dtype):
        wp, wout, pw = [lax.dynamic_slice_in_dim(grads[k], first, 2, axis=0) for k in ("wp", "wout", "pw_w")]
        return [_padded_to_shards(wp, dtype), _by_chip(wout, 1).astype(dtype), _by_chip(pw, 1).astype(dtype)]

    keep = half_by_chip(2 * ci, F32)
    give = half_by_chip(2 * (1 - ci), BF16)
    got = _to_sibling(give, name="grads_to_sibling")
    pair = [_sum_arrays([k.reshape((8,) + k.shape[2:]), r.reshape((8,) + r.shape[2:])], name=f"pair_sum_{i}",
                        out_dtype=BF16).reshape(k.shape) for i, (k, r) in enumerate(zip(keep, got))]
    arrived = _to_chips(pair, name="grads_to_chips")
    halves = []
    for i, (pr, ar) in enumerate(zip(pair, arrived)):
        own = lax.dynamic_index_in_dim(pr, chip, axis=0, keepdims=False)
        halves.append(_sum_into_half([own, ar[0], ar[1], ar[2]], ci, name=f"chip_sum_{i}"))
    g_w_in, g_w_out, g_pw_w = _join_halves(halves, name="join_halves")

    ba = grads["ba_sums"]
    small = [grads["norm_w"], ba[:, 0:1, :], ba[:, 1:2, :], grads["gnw"], grads["ln_sums"][:, 2:3, :],
             grads["ln_sums"][:, 0:1, :], grads["ln_sums"][:, 1:2, :], d_final,
             grads["cw"][:, :SHORT_CONV, :], grads["dw_w"][:, :CONV_WIDTH, :]]
    red = _unpack(_allreduce_small(_pack(small), name="allreduce_small"), [s.shape for s in small])
    g_norm_w = red[0][:, 0, :]
    g_a_log = red[1][:, 0, GDN_HEADS:2 * GDN_HEADS]
    g_dt_bias = red[2][:, 0, GDN_HEADS:2 * GDN_HEADS]
    g_gnw, g_dw_b, g_ln_w, g_ln_b = red[3][:, 0, :], red[4][:, 0, :], red[5][:, 0, :], red[6][:, 0, :]
    g_final = red[7][0]
    g_cw = lax.dynamic_slice_in_dim(red[8], chip * (3 * GDN_W // 4), 3 * GDN_W // 4, axis=2)
    g_dw_w = lax.dynamic_slice_in_dim(red[9], chip * (CONV_CH // 4), CONV_CH // 4, axis=2)

    d_w_in, nm_w_in, nv_w_in = _adamw(w_in, g_w_in, m_w_in, v_w_in, name="adamw_w_in")
    d_w_out, nm_w_out, nv_w_out = _adamw(w_out, g_w_out, m_w_out, v_w_out, name="adamw_w_out")
    d_pw_w, nm_pw_w, nv_pw_w = _adamw(conf_pw_w, g_pw_w, m_conf_pw_w, v_conf_pw_w, name="adamw_pw")
    sw = [norm_w, a_log, dt_bias, gdn_norm_w, conf_dw_b, conf_ln_w, conf_ln_b, final_norm_w, conv_qkv_w, conf_dw_w]
    sg = [g_norm_w, g_a_log, g_dt_bias, g_gnw, g_dw_b, g_ln_w, g_ln_b, g_final, g_cw, g_dw_w]
    sm = [m_norm_w, m_a_log, m_dt_bias, m_gdn_norm_w, m_conf_dw_b, m_conf_ln_w, m_conf_ln_b, m_final_norm_w,
          m_conv_qkv_w, m_conf_dw_w]
    sv = [v_norm_w, v_a_log, v_dt_bias, v_gdn_norm_w, v_conf_dw_b, v_conf_ln_w, v_conf_ln_b, v_final_norm_w,
          v_conv_qkv_w, v_conf_dw_w]
    shapes = [a.shape for a in sw]
    packed = _adamw(_pack(sw)[None], _pack(sg)[None], _pack(sm)[None], _pack(sv)[None], name="adamw_small")
    sd, snm, snv = [_unpack(pk[0], shapes) for pk in packed]

    def order(big3, small10):
        s = small10
        return [s[0], big3[0], s[8], s[1], s[2], s[3], s[9], s[4], s[5], s[6], big3[2], big3[1], s[7]]

    return (loss, grad_x[None], *order([g_w_in, g_w_out, g_pw_w], sg),
            *order([d_w_in, d_w_out, d_pw_w], sd), *order([nm_w_in, nm_w_out, nm_pw_w], snm),
            *order([nv_w_in, nv_w_out, nv_pw_w], snv))
```

```python
import functools
import math

import jax
import jax.numpy as jnp
from jax import lax
from jax.experimental import pallas as pl
from jax.experimental.pallas import tpu as pltpu

F32, BF16 = jnp.float32, jnp.bfloat16
HIGHEST = lax.Precision.HIGHEST
MESH = pl.DeviceIdType.MESH

D_MODEL = 2048
DEPTH = 4
CONV_CH = 512
GDN_W = 768
GDN_HEADS = 6
GDN_D = 128
ATT_W = 768
ATT_HEADS = 12
ATT_HD = 64
CONV_WIDTH = 31
SHORT_CONV = 4
GDN_CHUNK = 64
ROPE_THETA = 500000.0
ROPE_DIM = ATT_HD // 4
DIL_PATTERNS = ((128, 1), (512, 4), (2048, 16))
ATT_BLOCK = 128
NEG_INF = -1e30
IN_W = 7692

IN_WP = 8192
COL_BA = 7680
ORIG_BA = 4608
ORIG_ATT = 4620

ADAM_LR = 0.001
ADAM_B1 = 0.9
ADAM_B2 = 0.999
ADAM_EPS = 1e-08
ADAM_WD = 0.01
ADAM_STEP = 10

VMEM_LIMIT = 56 * 1024 * 1024


def _params(sem=None):
    return pltpu.CompilerParams(dimension_semantics=sem, vmem_limit_bytes=VMEM_LIMIT)


def _sigmoid(x):
    return 0.5 * jnp.tanh(0.5 * x) + 0.5


def _silu(x):
    return x * _sigmoid(x)


def _dsilu(x):
    s = _sigmoid(x)
    return s * (1.0 + x * (1.0 - s))


def _dot(a, b, dims, precision=None):
    return lax.dot_general(a, b, (dims, ((), ())), precision=precision, preferred_element_type=F32)


def _nn(a, b, precision=None):
    return _dot(a, b, ((1,), (0,)), precision)


def _nt(a, b, precision=None):
    return _dot(a, b, ((1,), (1,)), precision)


def _tn(a, b, precision=None):
    return _dot(a, b, ((0,), (0,)), precision)


def _matmul(a, b, *, name, ta=False, tb=False, out_dtype=F32, add=None, stack=None, tm=1024, tn=1024, tk=1024):
    if ta:
        k_dim, m_dim = a.shape
    else:
        m_dim, k_dim = a.shape
    n_dim = b.shape[0] if tb else b.shape[1]
    tm, tn, tk = min(tm, m_dim), min(tn, n_dim), min(tk, k_dim)
    assert m_dim % tm == 0 and n_dim % tn == 0 and k_dim % tk == 0, (name, a.shape, b.shape)
    nk = k_dim // tk
    a_spec = pl.BlockSpec((tk, tm), lambda i, j, k: (k, i)) if ta else pl.BlockSpec((tm, tk), lambda i, j, k: (i, k))
    b_spec = pl.BlockSpec((tn, tk), lambda i, j, k: (j, k)) if tb else pl.BlockSpec((tk, tn), lambda i, j, k: (k, j))
    o_spec = pl.BlockSpec((tm, tn), lambda i, j, k: (i, j))
    out_shape = jax.ShapeDtypeStruct((m_dim, n_dim), out_dtype)
    dims = ((0 if ta else 1,), (1 if tb else 0,))
    has_add = add is not None
    ins = [a, b] + ([add] if has_add else [])
    specs = [a_spec, b_spec] + ([o_spec] if has_add else [])
    aliases = {}
    if stack is not None:
        buf, slab, nslabs = stack
        o_spec = pl.BlockSpec((None, tm, tn), lambda i, j, k: (slab, i, j))
        out_shape = jax.ShapeDtypeStruct((nslabs, m_dim, n_dim), out_dtype)
        if buf is not None:
            aliases = {len(ins): 0}
            ins.append(buf)
            specs.append(pl.BlockSpec(memory_space=pl.ANY))
    n_in = len(ins)

    def body(*refs):
        a_ref, b_ref = refs[0], refs[1]
        o_ref = refs[n_in]

        def finish(r):
            if has_add:
                r = r + refs[2][...]
            o_ref[...] = r.astype(out_dtype)

        prod = _dot(a_ref[...].astype(BF16), b_ref[...].astype(BF16), dims)
        if nk == 1:
            finish(prod)
            return
        acc_ref = refs[n_in + 1]
        k = pl.program_id(2)

        @pl.when(k == 0)
        def _():
            acc_ref[...] = prod

        @pl.when(k > 0)
        def _():
            acc_ref[...] += prod

        @pl.when(k == nk - 1)
        def _():
            finish(acc_ref[...])

    return pl.pallas_call(
        body, name=name, grid=(m_dim // tm, n_dim // tn, nk), in_specs=specs, out_specs=o_spec,
        out_shape=out_shape, scratch_shapes=[pltpu.VMEM((tm, tn), F32)] if nk > 1 else [],
        input_output_aliases=aliases,
        compiler_params=_params(("parallel", "parallel", "arbitrary")),
    )(*ins)


def _rms_fwd(x, w, *, name, tm=256):
    s_len, d = x.shape

    def body(x_ref, w_ref, h_ref):
        xv = x_ref[...]
        r = lax.rsqrt(jnp.mean(xv * xv, axis=-1, keepdims=True) + 1e-6)
        h_ref[...] = (xv * r * w_ref[...]).astype(BF16)

    return pl.pallas_call(
        body, name=name, grid=(s_len // tm,),
        in_specs=[pl.BlockSpec((tm, d), lambda i: (i, 0)), pl.BlockSpec((1, d), lambda i: (0, 0))],
        out_specs=pl.BlockSpec((tm, d), lambda i: (i, 0)),
        out_shape=jax.ShapeDtypeStruct((s_len, d), BF16),
        compiler_params=_params(("parallel",)),
    )(x, w)


def _rms_bwd(x, dh, w, dres, *, name, tm=256):
    s_len, d = x.shape
    nsteps = s_len // tm

    def body(x_ref, dh_ref, w_ref, dres_ref, dx_ref, dw_ref, acc_ref):
        i = pl.program_id(0)

        @pl.when(i == 0)
        def _():
            acc_ref[...] = jnp.zeros_like(acc_ref)

        xv = x_ref[...]
        r = lax.rsqrt(jnp.mean(xv * xv, axis=-1, keepdims=True) + 1e-6)
        xn = xv * r
        dy = dh_ref[...]
        dxn = dy * w_ref[...]
        dx_ref[...] = dres_ref[...] + r * (dxn - xn * jnp.mean(dxn * xn, axis=-1, keepdims=True))
        acc_ref[...] += (dy * xn).reshape(tm // 8, 8, d).sum(axis=0)

        @pl.when(i == nsteps - 1)
        def _():
            dw_ref[...] = jnp.sum(acc_ref[...], axis=0, keepdims=True)

    row = pl.BlockSpec((tm, d), lambda i: (i, 0))
    vec = pl.BlockSpec((1, d), lambda i: (0, 0))
    return pl.pallas_call(
        body, name=name, grid=(nsteps,), in_specs=[row, row, vec, row], out_specs=[row, vec],
        out_shape=[jax.ShapeDtypeStruct((s_len, d), F32), jax.ShapeDtypeStruct((1, d), F32)],
        scratch_shapes=[pltpu.VMEM((8, d), F32)],
        compiler_params=_params(("arbitrary",)),
    )(x, dh, w, dres)


def _loss_head(x, w, target, *, name, tm=256):
    s_len, d = x.shape
    nsteps = s_len // tm

    def body(x_ref, w_ref, t_ref, dx_ref, dw_ref, loss_ref, acc_ref, lacc_ref):
        i = pl.program_id(0)

        @pl.when(i == 0)
        def _():
            acc_ref[...] = jnp.zeros_like(acc_ref)
            lacc_ref[...] = jnp.zeros_like(lacc_ref)

        xv = x_ref[...]
        wv = w_ref[...]
        r = lax.rsqrt(jnp.mean(xv * xv, axis=-1, keepdims=True) + 1e-6)
        xn = xv * r
        err = xn * wv - t_ref[...]
        lacc_ref[...] += (err * err).reshape(tm // 8, 8, d).sum(axis=0)
        dy = err * (1.0 / d)
        dxn = dy * wv
        dx_ref[...] = r * (dxn - xn * jnp.mean(dxn * xn, axis=-1, keepdims=True))
        acc_ref[...] += (dy * xn).reshape(tm // 8, 8, d).sum(axis=0)

        @pl.when(i == nsteps - 1)
        def _():
            dw_ref[...] = jnp.sum(acc_ref[...], axis=0, keepdims=True)
            tot = jnp.sum(jnp.sum(lacc_ref[...], axis=0, keepdims=True), axis=1, keepdims=True)
            loss_ref[...] = jnp.broadcast_to(tot * (0.5 / d), (1, 128))

    row = pl.BlockSpec((tm, d), lambda i: (i, 0))
    vec = pl.BlockSpec((1, d), lambda i: (0, 0))
    return pl.pallas_call(
        body, name=name, grid=(nsteps,), in_specs=[row, vec, row],
        out_specs=[row, vec, pl.BlockSpec((1, 128), lambda i: (0, 0))],
        out_shape=[jax.ShapeDtypeStruct((s_len, d), F32), jax.ShapeDtypeStruct((1, d), F32),
                   jax.ShapeDtypeStruct((1, 128), F32)],
        scratch_shapes=[pltpu.VMEM((8, d), F32), pltpu.VMEM((8, d), F32)],
        compiler_params=_params(("arbitrary",)),
    )(x, w, target)


def _rows_block(shape, tr=256):
    lead, rows, cols = shape
    if rows % tr != 0:
        assert rows * cols <= 1 << 20, shape
        tr = rows
    return (lead, rows // tr), pl.BlockSpec((1, tr, cols), lambda a, i: (a, i, 0))


LEAD_BLOCK = 64


def _adamw(w, g, m, v, *, name, by_lead=False):
    if by_lead:
        lead, rows, cols = w.shape
        grid = (pl.cdiv(lead, LEAD_BLOCK), 1)
        spec = pl.BlockSpec((LEAD_BLOCK, rows, cols), lambda a, i: (a, 0, 0))
    else:
        grid, spec = _rows_block(w.shape)
    c1 = 1.0 / (1.0 - ADAM_B1 ** ADAM_STEP)
    c2 = 1.0 / (1.0 - ADAM_B2 ** ADAM_STEP)

    def body(w_ref, g_ref, m_ref, v_ref, d_ref, nm_ref, nv_ref):
        gv = g_ref[...]
        nm = ADAM_B1 * m_ref[...] + (1.0 - ADAM_B1) * gv
        nv = ADAM_B2 * v_ref[...] + (1.0 - ADAM_B2) * (gv * gv)
        nm_ref[...] = nm
        nv_ref[...] = nv
        d_ref[...] = -ADAM_LR * ((nm * c1) / (jnp.sqrt(nv * c2) + ADAM_EPS) + ADAM_WD * w_ref[...])

    out = jax.ShapeDtypeStruct(w.shape, F32)
    return pl.pallas_call(
        body, name=name, grid=grid, in_specs=[spec] * 4, out_specs=[spec] * 3, out_shape=[out] * 3,
        compiler_params=_params(("parallel", "parallel")),
    )(w, g, m, v)


def _sum_into_half(arrs, half, *, name):
    lead, rows, cols = arrs[0].shape
    assert lead == 2
    (_, nr), spec0 = _rows_block(arrs[0].shape)
    tr = spec0.block_shape[1]
    n = len(arrs)

    def body(half_ref, *refs):
        del half_ref
        acc = refs[0][...].astype(F32)
        for r in refs[1:n]:
            acc = acc + r[...].astype(F32)
        refs[n][...] = acc

    spec = pl.BlockSpec((1, tr, cols), lambda a, i, h: (a, i, 0))
    return pl.pallas_call(
        body, name=name,
        grid_spec=pltpu.PrefetchScalarGridSpec(
            num_scalar_prefetch=1, grid=(2, nr), in_specs=[spec] * n,
            out_specs=pl.BlockSpec((1, tr, cols), lambda a, i, h: (2 * h[0] + a, i, 0))),
        out_shape=jax.ShapeDtypeStruct((4, rows, cols), F32),
        compiler_params=_params(("parallel", "parallel")),
    )(jnp.reshape(half, (1,)).astype(jnp.int32), *arrs)


def _sum_arrays(arrs, *, name, out_dtype):
    grid, spec = _rows_block(arrs[0].shape)
    n = len(arrs)

    def body(*refs):
        acc = refs[0][...].astype(F32)
        for r in refs[1:n]:
            acc = acc + r[...].astype(F32)
        refs[n][...] = acc.astype(out_dtype)

    return pl.pallas_call(
        body, name=name, grid=grid, in_specs=[spec] * n, out_specs=spec,
        out_shape=jax.ShapeDtypeStruct(arrs[0].shape, out_dtype),
        compiler_params=_params(("parallel", "parallel")),
    )(*arrs)


HALO = 32


def _shifted_windows(buf, tm, offsets):
    rows = buf.shape[0]
    for b in range(8):
        group = [(k, s) for k, s in enumerate(offsets) if s % 8 == b]
        if not group:
            continue
        rb = buf if b == 0 else pltpu.roll(buf, rows - b, 0)
        for k, s in group:
            yield k, rb[s - b:s - b + tm, :]


def _conf_fwd(u, dw_w, dw_b, ln_w, ln_b, *, name, tm=256):
    s_len = u.shape[0]
    c = CONV_CH

    def body(uc_ref, up_ref, dww_ref, dwb_ref, lnw_ref, lnb_ref, conv_ref, sw_ref, hbuf):
        i = pl.program_id(0)
        hbuf[HALO:, :] = uc_ref[:, :c] * _sigmoid(uc_ref[:, c:])
        hp = up_ref[:, :c] * _sigmoid(up_ref[:, c:])
        hbuf[:HALO, :] = jnp.where(i > 0, hp, 0.0)
        for cb in range(c // 128):
            cs = slice(128 * cb, 128 * (cb + 1))
            acc = jnp.zeros((tm, 128), F32)
            taps = [HALO - CONV_WIDTH + 1 + j for j in range(CONV_WIDTH)]
            for j, win in _shifted_windows(hbuf[:, cs], tm, taps):
                acc = acc + win * dww_ref[j:j + 1, cs]
            conv_ref[:, cs] = acc + dwb_ref[:, cs]
        cv = conv_ref[...]
        mu = jnp.mean(cv, axis=-1, keepdims=True)
        xc = cv - mu
        var = jnp.mean(xc * xc, axis=-1, keepdims=True)
        ln = xc * lax.rsqrt(var + 1e-5) * lnw_ref[...] + lnb_ref[...]
        sw_ref[...] = _silu(ln).astype(BF16)

    vec = pl.BlockSpec((1, c), lambda i: (0, 0))
    return pl.pallas_call(
        body, name=name, grid=(s_len // tm,),
        in_specs=[pl.BlockSpec((tm, 2 * c), lambda i: (i, 0)),
                  pl.BlockSpec((HALO, 2 * c), lambda i: (jnp.maximum(i * (tm // HALO) - 1, 0), 0)),
                  pl.BlockSpec((HALO, c), lambda i: (0, 0)), vec, vec, vec],
        out_specs=[pl.BlockSpec((tm, c), lambda i: (i, 0))] * 2,
        out_shape=[jax.ShapeDtypeStruct((s_len, c), F32), jax.ShapeDtypeStruct((s_len, c), BF16)],
        scratch_shapes=[pltpu.VMEM((tm + HALO, c), F32)],
        compiler_params=_params(("parallel",)),
    )(u, u, dw_w, dw_b, ln_w, ln_b)


def _conf_bwd_ln(d_sw, conv, ln_w, ln_b, *, name, tm=256):
    s_len, c = conv.shape
    nsteps = s_len // tm

    def body(dsw_ref, conv_ref, lnw_ref, lnb_ref, dconv_ref, sums_ref):
        i = pl.program_id(0)

        @pl.when(i == 0)
        def _():
            sums_ref[...] = jnp.zeros_like(sums_ref)

        cv = conv_ref[...]
        mu = jnp.mean(cv, axis=-1, keepdims=True)
        xc = cv - mu
        rs = lax.rsqrt(jnp.mean(xc * xc, axis=-1, keepdims=True) + 1e-5)
        xhat = xc * rs
        lnw = lnw_ref[...]
        ln = xhat * lnw + lnb_ref[...]
        dln = dsw_ref[...] * _dsilu(ln)
        dxh = dln * lnw
        dconv = rs * (dxh - jnp.mean(dxh, axis=-1, keepdims=True)
                      - xhat * jnp.mean(dxh * xhat, axis=-1, keepdims=True))
        dconv_ref[...] = dconv
        sums_ref[0:1, :] += jnp.sum(dln * xhat, axis=0, keepdims=True)
        sums_ref[1:2, :] += jnp.sum(dln, axis=0, keepdims=True)
        sums_ref[2:3, :] += jnp.sum(dconv, axis=0, keepdims=True)

    row = pl.BlockSpec((tm, c), lambda i: (i, 0))
    vec = pl.BlockSpec((1, c), lambda i: (0, 0))
    return pl.pallas_call(
        body, name=name, grid=(nsteps,), in_specs=[row, row, vec, vec],
        out_specs=[row, pl.BlockSpec((8, c), lambda i: (0, 0))],
        out_shape=[jax.ShapeDtypeStruct((s_len, c), F32), jax.ShapeDtypeStruct((8, c), F32)],
        compiler_params=_params(("arbitrary",)),
    )(d_sw, conv, ln_w, ln_b)


def _conf_bwd_conv(u, dconv, dw_w, du, *, name, tm=256):
    s_len = u.shape[0]
    c = CONV_CH
    nsteps = s_len // tm
    off = HALO - CONV_WIDTH + 1

    def body(uc_ref, up_ref, dc_ref, dn_ref, dww_ref, du_in_ref, du_ref, ddw_ref, hbuf, dbuf, wacc):
        del du_in_ref
        i = pl.program_id(0)

        @pl.when(i == 0)
        def _():
            wacc[...] = jnp.zeros_like(wacc)

        hbuf[HALO:, :] = uc_ref[:, :c] * _sigmoid(uc_ref[:, c:])
        hp = up_ref[:, :c] * _sigmoid(up_ref[:, c:])
        hbuf[:HALO, :] = jnp.where(i > 0, hp, 0.0)
        dbuf[:tm, :] = dc_ref[...]
        dbuf[tm:, :] = jnp.where(i < nsteps - 1, dn_ref[...], 0.0)
        for cb in range(c // 128):
            cs = slice(128 * cb, 128 * (cb + 1))
            dcur = dbuf[0:tm, cs]
            acc = jnp.zeros((tm, 128), F32)
            for k, win in _shifted_windows(dbuf[:, cs], tm, list(range(CONV_WIDTH))):
                j = CONV_WIDTH - 1 - k
                acc = acc + win * dww_ref[j:j + 1, cs]
            for j, win in _shifted_windows(hbuf[:, cs], tm, [off + j for j in range(CONV_WIDTH)]):
                wacc[j, :, cs] += (win * dcur).reshape(tm // 8, 8, 128).sum(axis=0)
            a = uc_ref[:, cs]
            sg = _sigmoid(uc_ref[:, c + 128 * cb:c + 128 * (cb + 1)])
            du_ref[:, cs] = (acc * sg).astype(du_ref.dtype)
            du_ref[:, c + 128 * cb:c + 128 * (cb + 1)] = (acc * a * sg * (1.0 - sg)).astype(du_ref.dtype)

        @pl.when(i == nsteps - 1)
        def _():
            for j in range(CONV_WIDTH):
                ddw_ref[j:j + 1, :] = jnp.sum(wacc[j], axis=0, keepdims=True)
            ddw_ref[CONV_WIDTH:, :] = jnp.zeros((HALO - CONV_WIDTH, c), F32)

    return pl.pallas_call(
        body, name=name, grid=(nsteps,),
        in_specs=[pl.BlockSpec((tm, 2 * c), lambda i: (i, 0)),
                  pl.BlockSpec((HALO, 2 * c), lambda i: (jnp.maximum(i * (tm // HALO) - 1, 0), 0)),
                  pl.BlockSpec((tm, c), lambda i: (i, 0)),
                  pl.BlockSpec((HALO, c), lambda i: (jnp.minimum((i + 1) * (tm // HALO), s_len // HALO - 1), 0)),
                  pl.BlockSpec((HALO, c), lambda i: (0, 0)),
                  pl.BlockSpec(memory_space=pl.ANY)],
        out_specs=[pl.BlockSpec((tm, 2 * c), lambda i: (i, 0)), pl.BlockSpec((HALO, c), lambda i: (0, 0))],
        out_shape=[jax.ShapeDtypeStruct(du.shape, du.dtype), jax.ShapeDtypeStruct((HALO, c), F32)],
        scratch_shapes=[pltpu.VMEM((tm + HALO, c), F32), pltpu.VMEM((tm + HALO, c), F32),
                        pltpu.VMEM((CONV_WIDTH, 8, c), F32)],
        input_output_aliases={5: 0},
        compiler_params=_params(("arbitrary",)),
    )(u, u, dconv, dconv, dw_w, du)


COL_GQ = 1536 // GDN_W
COL_AQ = 4608 // ATT_W
SHALO = 8
INTRA_CHUNKS = 2


def _softplus(z):
    return jnp.maximum(z, 0.0) + jnp.log1p(jnp.exp(-jnp.abs(z)))


def _short_conv(buf, cw_ref, part, rows, first):
    acc = jnp.zeros((rows, GDN_W), F32)
    for j, win in _shifted_windows(buf[...], rows, [first + j for j in range(SHORT_CONV)]):
        acc = acc + win * cw_ref[j:j + 1, GDN_W * part:GDN_W * (part + 1)]
    return acc


def _gdn_prep_fwd(u, cw, al, dtb, *, name, tm=256):
    s_len = u.shape[0]
    first = SHALO - SHORT_CONV + 1

    def body(uq, uk, uv, pq, pk, pv, uba, cw_ref, al_ref, dtb_ref, qn_ref, kn_ref, vc_ref, bg_ref, buf):
        i = pl.program_id(0)

        def conv(cur, prev, part):
            buf[SHALO:, :] = cur[...]
            buf[:SHALO, :] = jnp.where(i > 0, prev[...], 0.0)
            return _silu(_short_conv(buf, cw_ref, part, tm, first))

        for part, (cur, prev, out, scale) in enumerate(
                ((uq, pq, qn_ref, GDN_D ** -0.5), (uk, pk, kn_ref, 1.0))):
            y = conv(cur, prev, part)
            for h in range(GDN_HEADS):
                hs = slice(GDN_D * h, GDN_D * (h + 1))
                yh = y[:, hs]
                out[:, hs] = yh * (lax.rsqrt(jnp.sum(yh * yh, axis=-1, keepdims=True) + 1e-6) * scale)
        vc_ref[...] = conv(uv, pv, 2)
        ba = uba[...]
        lane = lax.broadcasted_iota(jnp.int32, ba.shape, 1)
        g = -jnp.exp(al_ref[...]) * _softplus(ba + dtb_ref[...])
        bg_ref[...] = jnp.where(lane < GDN_HEADS, _sigmoid(ba), jnp.where(lane < 2 * GDN_HEADS, g, 0.0))

    def cur(col):
        return pl.BlockSpec((tm, GDN_W), lambda i: (i, col))

    def prev(col):
        return pl.BlockSpec((SHALO, GDN_W), lambda i: (jnp.maximum(i * (tm // SHALO) - 1, 0), col))

    vec = pl.BlockSpec((1, 128), lambda i: (0, 0))
    row = pl.BlockSpec((tm, GDN_W), lambda i: (i, 0))
    wide = jax.ShapeDtypeStruct((s_len, GDN_W), F32)
    return pl.pallas_call(
        body, name=name, grid=(s_len // tm,),
        in_specs=[cur(COL_GQ), cur(COL_GQ + 1), cur(COL_GQ + 2), prev(COL_GQ), prev(COL_GQ + 1), prev(COL_GQ + 2),
                  pl.BlockSpec((tm, 128), lambda i: (i, COL_BA // 128)),
                  pl.BlockSpec((SHALO, 3 * GDN_W), lambda i: (0, 0)), vec, vec],
        out_specs=[row, row, row, pl.BlockSpec((tm, 128), lambda i: (i, 0))],
        out_shape=[wide, wide, wide, jax.ShapeDtypeStruct((s_len, 128), F32)],
        scratch_shapes=[pltpu.VMEM((tm + SHALO, GDN_W), F32)],
        compiler_params=_params(("parallel",)),
    )(u, u, u, u, u, u, u, cw, al, dtb)


def _chunk_masks():
    c = GDN_CHUNK
    row = lax.broadcasted_iota(jnp.int32, (c, c), 0)
    col = lax.broadcasted_iota(jnp.int32, (c, c), 1)
    return row >= col, row > col


def _cum_decay(bg):
    c = GDN_CHUNK
    causal, _ = _chunk_masks()
    g_cum = _nn(causal.astype(F32), bg, HIGHEST)
    sel = (lax.broadcasted_iota(jnp.int32, (8, 128), 0) + GDN_HEADS
           == lax.broadcasted_iota(jnp.int32, (8, 128), 1)).astype(F32)
    return g_cum, _nt(sel, g_cum, HIGHEST)


def _bdot(a, b, ca, cb):
    return lax.dot_general(a, b, (((ca,), (cb,)), ((0,), (0,))), preferred_element_type=F32)


def _bnn(a, b):
    return _bdot(a, b, 2, 1)


def _bnt(a, b):
    return _bdot(a, b, 2, 2)


def _btn(a, b):
    return _bdot(a, b, 1, 1)


def _split(a):
    hi = a.astype(BF16)
    return hi, (a - hi.astype(F32)).astype(BF16)


def _bnn3(a, b):
    ah, al = _split(a)
    bh, bl = _split(b)
    return _bnn(ah, bh) + (_bnn(al, bh) + _bnn(ah, bl))


def _heads(ref, rows=slice(None)):
    return jnp.stack([ref[rows, GDN_D * h:GDN_D * (h + 1)] for h in range(GDN_HEADS)])


def _head_columns(a, first):
    return jnp.stack([a[:, first + h:first + h + 1] for h in range(GDN_HEADS)])


def _chunk_decay(g_cum, g_rows, bg):
    causal, _ = _chunk_masks()
    gc = _head_columns(g_cum, GDN_HEADS)
    gr = jnp.stack([g_rows[h:h + 1, :] for h in range(GDN_HEADS)])
    dec = jnp.where(causal, jnp.exp(jnp.where(causal, gc - gr, 0.0)), 0.0)
    return gc, _head_columns(bg, 0), dec


def _gdn_intra_fwd(qn, kn, vc, bg, *, name):
    s_len = qn.shape[0]
    c = GDN_CHUNK
    nch = INTRA_CHUNKS
    nsteps = s_len // (c * nch)

    def body(q_ref, k_ref, v_ref, bg_ref, wk_ref, wv_ref, qd_ref, kd_ref, p_ref, t_ref, g_ref):
        causal, strict = _chunk_masks()
        eye = (lax.broadcasted_iota(jnp.int32, (c, c), 0) == lax.broadcasted_iota(jnp.int32, (c, c), 1)).astype(F32)
        parts = []
        for ch in range(nch):
            rs = slice(c * ch, c * (ch + 1))
            bg = bg_ref[rs, :]
            g_cum, g_rows = _cum_decay(bg)
            g_ref[rs, :] = g_cum
            parts.append(_chunk_decay(g_cum, g_rows, bg) + (_heads(q_ref, rs), _heads(k_ref, rs), _heads(v_ref, rs)))
        gc, bc, dec, q, k, v = [jnp.concatenate([p[i] for p in parts], axis=0) for i in range(6)]
        k16 = k.astype(BF16)
        low = jnp.where(strict, bc * _bnt(k16, k16) * dec, 0.0)
        pw = -low
        t = eye + pw
        for _ in range(5):
            pw = _bnn3(pw, pw)
            t = t + _bnn3(t, pw)
        t16 = t.astype(BF16)
        eg = jnp.exp(gc)
        wk = _bnn(t16, (k * (bc * eg)).astype(BF16))
        wv = _bnn(t16, (v * bc).astype(BF16))
        pm = jnp.where(causal, _bnt(q.astype(BF16), k16) * dec, 0.0).astype(BF16)
        qd = q * eg
        kd = k * jnp.exp(gc[:, c - 1:c, :] - gc)
        for idx in range(nch * GDN_HEADS):
            ch, h = divmod(idx, GDN_HEADS)
            rs = slice(c * ch, c * (ch + 1))
            hs = slice(GDN_D * h, GDN_D * (h + 1))
            t_ref[h, rs, :] = t[idx]
            p_ref[h, rs, :] = pm[idx]
            wk_ref[rs, hs] = wk[idx].astype(BF16)
            wv_ref[rs, hs] = wv[idx]
            qd_ref[rs, hs] = qd[idx].astype(BF16)
            kd_ref[rs, hs] = kd[idx].astype(BF16)

    row = pl.BlockSpec((c * nch, GDN_W), lambda n: (n, 0))
    sq = pl.BlockSpec((GDN_HEADS, c * nch, c), lambda n: (0, n, 0))
    narrow = pl.BlockSpec((c * nch, 128), lambda n: (n, 0))
    w16 = jax.ShapeDtypeStruct((s_len, GDN_W), BF16)
    return pl.pallas_call(
        body, name=name, grid=(nsteps,), in_specs=[row, row, row, narrow],
        out_specs=[row, row, row, row, sq, sq, narrow],
        out_shape=[w16, jax.ShapeDtypeStruct((s_len, GDN_W), F32), w16, w16,
                   jax.ShapeDtypeStruct((GDN_HEADS, s_len, c), BF16),
                   jax.ShapeDtypeStruct((GDN_HEADS, s_len, c), F32),
                   jax.ShapeDtypeStruct((s_len, 128), F32)],
        compiler_params=_params(("parallel",)),
    )(qn, kn, vc, bg)


def _gdn_scan_fwd(wk, wv, qd, kd, p, g_cum, *, name):
    s_len = wk.shape[0]
    c = GDN_CHUNK
    nchunks = s_len // c

    def body(wk_ref, wv_ref, qd_ref, kd_ref, p_ref, g_ref, o_ref, vn_ref, sp_ref, st):
        @pl.when(pl.program_id(0) == 0)
        def _():
            st[...] = jnp.zeros_like(st)

        s = st[...]
        sp_ref[0] = s
        s16 = s.astype(BF16)
        vn16 = (_heads(wv_ref) - _bnn(_heads(wk_ref), s16)).astype(BF16)
        o = _bnn(_heads(qd_ref), s16) + _bnn(p_ref[...], vn16)
        gl = jnp.exp(_head_columns(g_ref[c - 1:c, :], GDN_HEADS))
        st[...] = s * gl + _btn(_heads(kd_ref), vn16)
        for h in range(GDN_HEADS):
            hs = slice(GDN_D * h, GDN_D * (h + 1))
            vn_ref[:, hs] = vn16[h]
            o_ref[:, hs] = o[h]

    row = pl.BlockSpec((c, GDN_W), lambda n: (n, 0))
    return pl.pallas_call(
        body, name=name, grid=(nchunks,),
        in_specs=[row, row, row, row, pl.BlockSpec((GDN_HEADS, c, c), lambda n: (0, n, 0)),
                  pl.BlockSpec((c, 128), lambda n: (n, 0))],
        out_specs=[row, row, pl.BlockSpec((1, GDN_HEADS, GDN_D, GDN_D), lambda n: (n, 0, 0, 0))],
        out_shape=[jax.ShapeDtypeStruct((s_len, GDN_W), F32), jax.ShapeDtypeStruct((s_len, GDN_W), BF16),
                   jax.ShapeDtypeStruct((nchunks, GDN_HEADS, GDN_D, GDN_D), F32)],
        scratch_shapes=[pltpu.VMEM((GDN_HEADS, GDN_D, GDN_D), F32)],
        compiler_params=_params(("arbitrary",)),
    )(wk, wv, qd, kd, p, g_cum)


def _gdn_scan_bwd(do, wk, qd, kd, p, g_cum, *, name):
    s_len = wk.shape[0]
    c = GDN_CHUNK
    nchunks = s_len // c

    def body(do_ref, wk_ref, qd_ref, kd_ref, p_ref, g_ref, dvn_ref, ds_ref, dst):
        @pl.when(pl.program_id(0) == 0)
        def _():
            dst[...] = jnp.zeros_like(dst)

        ds = dst[...]
        ds_ref[0] = ds
        do16 = _heads(do_ref).astype(BF16)
        dvn16 = (_btn(p_ref[...], do16) + _bnn(_heads(kd_ref), ds.astype(BF16))).astype(BF16)
        gl = jnp.exp(_head_columns(g_ref[c - 1:c, :], GDN_HEADS))
        dst[...] = _btn(_heads(qd_ref), do16) + ds * gl - _btn(_heads(wk_ref), dvn16)
        for h in range(GDN_HEADS):
            dvn_ref[:, GDN_D * h:GDN_D * (h + 1)] = dvn16[h]

    row = pl.BlockSpec((c, GDN_W), lambda n: (nchunks - 1 - n, 0))
    return pl.pallas_call(
        body, name=name, grid=(nchunks,),
        in_specs=[row, row, row, row, pl.BlockSpec((GDN_HEADS, c, c), lambda n: (0, nchunks - 1 - n, 0)),
                  pl.BlockSpec((c, 128), lambda n: (nchunks - 1 - n, 0))],
        out_specs=[row, pl.BlockSpec((1, GDN_HEADS, GDN_D, GDN_D), lambda n: (nchunks - 1 - n, 0, 0, 0))],
        out_shape=[jax.ShapeDtypeStruct((s_len, GDN_W), BF16),
                   jax.ShapeDtypeStruct((nchunks, GDN_HEADS, GDN_D, GDN_D), F32)],
        scratch_shapes=[pltpu.VMEM((GDN_HEADS, GDN_D, GDN_D), F32)],
        compiler_params=_params(("arbitrary",)),
    )(do, wk, qd, kd, p, g_cum)


def _gdn_intra_bwd(qn, kn, vc, bg, g_cum, t, do, dvn, vn, sprev, ds_all, *, name):
    s_len = qn.shape[0]
    c = GDN_CHUNK
    nch = INTRA_CHUNKS
    nsteps = s_len // (c * nch)
    nb = nch * GDN_HEADS

    def body(q_ref, k_ref, v_ref, bg_ref, g_ref, t_ref, do_ref, dvn_ref, vn_ref, sp_ref, ds_ref,
             dqkv_ref, dbg_ref):
        causal, strict = _chunk_masks()
        lane = lax.broadcasted_iota(jnp.int32, (c, 128), 1)
        rowi = lax.broadcasted_iota(jnp.int32, (c, 128), 0)
        parts = []
        for ch in range(nch):
            rs = slice(c * ch, c * (ch + 1))
            bg = bg_ref[rs, :]
            _, g_rows = _cum_decay(bg)
            parts.append(_chunk_decay(g_ref[rs, :], g_rows, bg) + tuple(
                _heads(r, rs) for r in (q_ref, k_ref, v_ref, do_ref, dvn_ref, vn_ref)) + (t_ref[:, rs, :],))
        gc, bc, dec, q, k, v, do, dvn16, vn16, tm = [jnp.concatenate([p[i] for p in parts], axis=0)
                                                     for i in range(10)]
        q16, k16 = q.astype(BF16), k.astype(BF16)
        kk = _bnt(k16, k16)
        low = jnp.where(strict, bc * kk * dec, 0.0)
        eg = jnp.exp(gc)
        g_last = gc[:, c - 1:c, :]
        kdec = jnp.exp(g_last - gc)
        kb, vb, qd, kd = k * (bc * eg), v * bc, q * eg, k * kdec
        pm = jnp.where(causal, _bnt(q16, k16) * dec, 0.0)
        s = sp_ref[...].reshape(nb, GDN_D, GDN_D)
        ds = ds_ref[...].reshape(nb, GDN_D, GDN_D)
        s16, ds16 = s.astype(BF16), ds.astype(BF16)
        do16 = do.astype(BF16)
        t16 = tm.astype(BF16)

        dqd = _bnt(do16, s16)
        dp = jnp.where(causal, _bnt(do16, vn16), 0.0)
        dkd = _bnt(vn16, ds16)
        dgl = jnp.sum(jnp.sum(s * ds, axis=2, keepdims=True), axis=1, keepdims=True) * jnp.exp(g_last)
        dwk16 = (-_bnt(dvn16, s16)).astype(BF16)
        dt = _bnt(dwk16, kb.astype(BF16)) + _bnt(dvn16, vb.astype(BF16))
        dkb = _btn(t16, dwk16)
        dvb = _btn(t16, dvn16)
        th, tl = _split(tm)
        dth, dtl = _split(dt)
        xm = _btn(th, dth) + (_btn(tl, dth) + _btn(th, dtl))
        xh, xl = _split(xm)
        dlow = jnp.where(strict, -(_bnt(xh, th) + (_bnt(xl, th) + _bnt(xh, tl))), 0.0)
        dkk16 = (dlow * bc * dec).astype(BF16)
        dqk16 = (dp * dec).astype(BF16)

        dq = _bnn(dqk16, k16) + dqd * eg
        dk = _btn(dqk16, q16) + _bnn(dkk16, k16) + _btn(dkk16, k16) + dkb * (bc * eg) + dkd * kdec
        dv = dvb * bc
        for idx in range(nb):
            ch, h = divmod(idx, GDN_HEADS)
            rs = slice(c * ch, c * (ch + 1))
            hs = slice(GDN_D * h, GDN_D * (h + 1))
            dqkv_ref[0, rs, hs] = dq[idx]
            dqkv_ref[1, rs, hs] = dk[idx]
            dqkv_ref[2, rs, hs] = dv[idx]

        dbeta = (jnp.sum(dlow * kk * dec, axis=2, keepdims=True)
                 + jnp.sum(dkb * k, axis=2, keepdims=True) * eg + jnp.sum(dvb * v, axis=2, keepdims=True))
        mm = dlow * low + dp * pm
        mh, ml = _split(mm)
        ones16 = jnp.ones((nb, c, 128), BF16)
        col_sum = (_btn(mh, ones16) + _btn(ml, ones16))[:, :, 0:1]
        dkd_sum = jnp.sum(dkd * kd, axis=2, keepdims=True)
        dg = (jnp.sum(mm, axis=2, keepdims=True) - col_sum + jnp.sum(dkb * kb, axis=2, keepdims=True)
              + jnp.sum(dqd * qd, axis=2, keepdims=True) - dkd_sum)
        tail = jnp.sum(dkd_sum, axis=1, keepdims=True) + dgl
        upper = (lax.broadcasted_iota(jnp.int32, (c, c), 0) <= lax.broadcasted_iota(jnp.int32, (c, c), 1)).astype(F32)
        for ch in range(nch):
            dbeta_all = jnp.zeros((c, 128), F32)
            dg_all = jnp.zeros((c, 128), F32)
            for h in range(GDN_HEADS):
                idx = ch * GDN_HEADS + h
                dbeta_all = dbeta_all + jnp.where(lane == h, dbeta[idx], 0.0)
                dg_all = dg_all + jnp.where(lane == GDN_HEADS + h,
                                            dg[idx] + jnp.where(rowi == c - 1, tail[idx], 0.0), 0.0)
            dbg_ref[c * ch:c * (ch + 1), :] = dbeta_all + _nn(upper, dg_all, HIGHEST)

    row = pl.BlockSpec((c * nch, GDN_W), lambda n: (n, 0))
    narrow = pl.BlockSpec((c * nch, 128), lambda n: (n, 0))
    state = pl.BlockSpec((nch, GDN_HEADS, GDN_D, GDN_D), lambda n: (n, 0, 0, 0))
    return pl.pallas_call(
        body, name=name, grid=(nsteps,),
        in_specs=[row, row, row, narrow, narrow, pl.BlockSpec((GDN_HEADS, c * nch, c), lambda n: (0, n, 0)),
                  row, row, row, state, state],
        out_specs=[pl.BlockSpec((3, c * nch, GDN_W), lambda n: (0, n, 0)), narrow],
        out_shape=[jax.ShapeDtypeStruct((3, s_len, GDN_W), F32), jax.ShapeDtypeStruct((s_len, 128), F32)],
        compiler_params=_params(("parallel",)),
    )(qn, kn, vc, bg, g_cum, t, do, dvn, vn, sprev, ds_all)


def _gdn_prep_bwd(u, dqkv, cw, du, *, name, tm=256):
    s_len = u.shape[0]
    nsteps = s_len // tm
    ext = tm + SHALO

    def body(uc, up, un, dc, dn, cw_ref, du_in_ref, du_ref, dcw_ref, xbuf, dbuf, pbuf, wacc):
        del du_in_ref
        part = pl.program_id(0)
        i = pl.program_id(1)

        @pl.when(i == 0)
        def _():
            wacc[...] = jnp.zeros_like(wacc)

        xbuf[:SHALO, :] = jnp.where(i > 0, up[...], 0.0)
        xbuf[SHALO:SHALO + tm, :] = uc[...]
        xbuf[SHALO + tm:, :] = jnp.where(i < nsteps - 1, un[...], 0.0)
        dbuf[:tm, :] = dc[...]
        dbuf[tm:, :] = jnp.where(i < nsteps - 1, dn[...], 0.0)
        first = SHALO - SHORT_CONV + 1
        w = [cw_ref[j:j + 1, :] for j in range(SHORT_CONV)]
        taps = [first + j for j in range(SHORT_CONV)]
        xv = xbuf[...]
        pre = jnp.zeros((ext, GDN_W), F32)
        for j, win in _shifted_windows(xv, ext, taps):
            pre = pre + win * w[j]
        y = _silu(pre)
        dout = dbuf[...]
        scale = jnp.where(part == 0, GDN_D ** -0.5, 1.0)
        for h in range(GDN_HEADS):
            hs = slice(GDN_D * h, GDN_D * (h + 1))
            yh, dh = y[:, hs], dout[:, hs]
            rs = lax.rsqrt(jnp.sum(yh * yh, axis=-1, keepdims=True) + 1e-6)
            dyn = scale * rs * (dh - yh * (rs * rs) * jnp.sum(dh * yh, axis=-1, keepdims=True))
            dy = jnp.where(part < 2, dyn, dh)
            pbuf[:, hs] = dy * _dsilu(pre[:, hs])
        acc = jnp.zeros((tm, GDN_W), F32)
        dpre = pbuf[0:tm, :]
        for k, win in _shifted_windows(pbuf[...], tm, list(range(SHORT_CONV))):
            acc = acc + win * w[SHORT_CONV - 1 - k]
        for j, win in _shifted_windows(xv, tm, taps):
            wacc[j] += (win * dpre).reshape(tm // 8, 8, GDN_W).sum(axis=0)
        du_ref[...] = acc.astype(du_ref.dtype)

        @pl.when(i == nsteps - 1)
        def _():
            for j in range(SHORT_CONV):
                dcw_ref[j:j + 1, :] = jnp.sum(wacc[j], axis=0, keepdims=True)
            dcw_ref[SHORT_CONV:, :] = jnp.zeros((SHALO - SHORT_CONV, GDN_W), F32)

    per = tm // SHALO
    return pl.pallas_call(
        body, name=name, grid=(3, nsteps),
        in_specs=[pl.BlockSpec((tm, GDN_W), lambda p, i: (i, COL_GQ + p)),
                  pl.BlockSpec((SHALO, GDN_W), lambda p, i: (jnp.maximum(i * per - 1, 0), COL_GQ + p)),
                  pl.BlockSpec((SHALO, GDN_W), lambda p, i: (jnp.minimum((i + 1) * per, s_len // SHALO - 1), COL_GQ + p)),
                  pl.BlockSpec((None, tm, GDN_W), lambda p, i: (p, i, 0)),
                  pl.BlockSpec((None, SHALO, GDN_W), lambda p, i: (p, jnp.minimum((i + 1) * per, s_len // SHALO - 1), 0)),
                  pl.BlockSpec((SHALO, GDN_W), lambda p, i: (0, p)),
                  pl.BlockSpec(memory_space=pl.ANY)],
        out_specs=[pl.BlockSpec((tm, GDN_W), lambda p, i: (i, COL_GQ + p)),
                   pl.BlockSpec((SHALO, GDN_W), lambda p, i: (0, p))],
        out_shape=[jax.ShapeDtypeStruct(du.shape, du.dtype), jax.ShapeDtypeStruct((SHALO, 3 * GDN_W), F32)],
        scratch_shapes=[pltpu.VMEM((tm + 2 * SHALO, GDN_W), F32), pltpu.VMEM((ext, GDN_W), F32),
                        pltpu.VMEM((ext, GDN_W), F32), pltpu.VMEM((SHORT_CONV, 8, GDN_W), F32)],
        input_output_aliases={6: 0},
        compiler_params=_params(("arbitrary", "arbitrary")),
    )(u, u, u, dqkv, dqkv, cw, du)


def _gdn_ba_bwd(u, dbg, al, dtb, du, *, name, tm=256):
    s_len = u.shape[0]
    nsteps = s_len // tm
    wpad = IN_WP - COL_BA

    def body(uba, dbg_ref, al_ref, dtb_ref, du_in_ref, du_ref, sums_ref):
        del du_in_ref
        i = pl.program_id(0)

        @pl.when(i == 0)
        def _():
            sums_ref[...] = jnp.zeros_like(sums_ref)

        ba = uba[...]
        dbg = dbg_ref[...]
        lane = lax.broadcasted_iota(jnp.int32, ba.shape, 1)
        is_g = (lane >= GDN_HEADS) & (lane < 2 * GDN_HEADS)
        beta = _sigmoid(ba)
        z = ba + dtb_ref[...]
        ea = jnp.exp(al_ref[...])
        g = -ea * _softplus(z)
        dz = jnp.where(is_g, dbg * (-ea) * _sigmoid(z), 0.0)
        du_ref[:, :128] = jnp.where(lane < GDN_HEADS, dbg * beta * (1.0 - beta), dz).astype(du_ref.dtype)
        du_ref[:, 128:] = jnp.zeros((tm, wpad - 128), du_ref.dtype)
        sums_ref[0:1, :] += jnp.sum(jnp.where(is_g, dbg * g, 0.0), axis=0, keepdims=True)
        sums_ref[1:2, :] += jnp.sum(dz, axis=0, keepdims=True)

    vec = pl.BlockSpec((1, 128), lambda i: (0, 0))
    return pl.pallas_call(
        body, name=name, grid=(nsteps,),
        in_specs=[pl.BlockSpec((tm, 128), lambda i: (i, COL_BA // 128)), pl.BlockSpec((tm, 128), lambda i: (i, 0)),
                  vec, vec, pl.BlockSpec(memory_space=pl.ANY)],
        out_specs=[pl.BlockSpec((tm, wpad), lambda i: (i, COL_BA // wpad)), pl.BlockSpec((8, 128), lambda i: (0, 0))],
        out_shape=[jax.ShapeDtypeStruct(du.shape, du.dtype), jax.ShapeDtypeStruct((8, 128), F32)],
        input_output_aliases={4: 0},
        compiler_params=_params(("arbitrary",)),
    )(u, dbg, al, dtb, du)


def _rope_tables(s_len):
    half = ROPE_DIM // 2
    inv = ROPE_THETA ** (-jnp.arange(half, dtype=F32) / half)
    ang = jnp.arange(s_len, dtype=F32)[:, None] * inv[None, :]
    cos, sin = jnp.cos(ang), jnp.sin(ang)
    one = jnp.ones((s_len, ATT_HD - ROPE_DIM), F32)
    zero = jnp.zeros((s_len, ATT_HD - ROPE_DIM), F32)
    zh = jnp.zeros((s_len, half), F32)
    c = jnp.concatenate([cos, cos, one], axis=1)
    s1 = jnp.concatenate([-sin, zh, zero], axis=1)
    s2 = jnp.concatenate([zh, sin, zero], axis=1)
    return tuple(jnp.concatenate([t, t], axis=1) for t in (c, s1, s2))


def _rope(x, c, s1, s2):
    return x * c + pltpu.roll(x, 128 - ROPE_DIM // 2, 1) * s1 + pltpu.roll(x, ROPE_DIM // 2, 1) * s2


def _rope_t(dy, c, s1, s2):
    return dy * c + pltpu.roll(dy * s1, ROPE_DIM // 2, 1) + pltpu.roll(dy * s2, 128 - ROPE_DIM // 2, 1)


DILATIONS = tuple(d for _, d in DIL_PATTERNS)
VIEW_ROWS = 256


def _to_view(scr, out_ref, dil, dtype):
    nblk, rows, _ = scr.shape
    width = nblk * 128
    for b in range(nblk):
        if dil == 1:
            out_ref[:, 128 * b:128 * (b + 1)] = scr[b].astype(dtype)
            continue
        for r in range(dil):
            out_ref[:, r * width + 128 * b:r * width + 128 * (b + 1)] = (
                scr.at[b][pl.ds(r, rows // dil, stride=dil), :].astype(dtype))


def _from_view(in_ref, scr, dil):
    nblk, rows, _ = scr.shape
    width = nblk * 128
    for b in range(nblk):
        for r in range(dil):
            scr.at[b][pl.ds(r, rows // dil, stride=dil), :] = in_ref[:, r * width + 128 * b:r * width + 128 * (b + 1)]


def _view_spec(dil, width, tm=VIEW_ROWS):
    return pl.BlockSpec((tm // dil, dil * width), lambda i: (i, 0))


def _view_shape(s_len, dil, width, dtype):
    return jax.ShapeDtypeStruct((s_len // dil, dil * width), dtype)


def _att_prep_fwd(u, tabs, *, name):
    s_len = u.shape[0]
    tm = VIEW_ROWS
    scale = ATT_HD ** -0.5
    nblk = ATT_W // 128

    def body(uq, uk, uv, c_ref, s1_ref, s2_ref, *rest):
        outs, scr = rest[:-1], rest[-1]
        c, s1, s2 = c_ref[...], s1_ref[...], s2_ref[...]
        for part, src in enumerate((uq, uk, uv)):
            for b in range(nblk):
                xb = src[:, 128 * b:128 * (b + 1)]
                if part == 0:
                    xb = _rope(xb, c, s1, s2) * scale
                elif part == 1:
                    xb = _rope(xb, c, s1, s2)
                scr[b] = xb
            for gi, dil in enumerate(DILATIONS):
                _to_view(scr, outs[3 * gi + part], dil, BF16)

    tab = pl.BlockSpec((tm, 128), lambda i: (i, 0))
    outs = pl.pallas_call(
        body, name=name, grid=(s_len // tm,),
        in_specs=[pl.BlockSpec((tm, ATT_W), lambda i, col=COL_AQ + j: (i, col)) for j in range(3)] + [tab] * 3,
        out_specs=[_view_spec(dil, ATT_W) for dil in DILATIONS for _ in range(3)],
        out_shape=[_view_shape(s_len, dil, ATT_W, BF16) for dil in DILATIONS for _ in range(3)],
        scratch_shapes=[pltpu.VMEM((nblk, tm, 128), F32)],
        compiler_params=_params(("parallel",)),
    )(u, u, u, *tabs)
    return [outs[3 * gi:3 * gi + 3] for gi in range(len(DILATIONS))]


def _stack_heads(x):
    lane = lax.broadcasted_iota(jnp.int32, (1, 128), 1)
    zero = jnp.zeros_like(x)
    return jnp.concatenate([jnp.where(lane < ATT_HD, x, zero), jnp.where(lane >= ATT_HD, x, zero)], axis=0)


def _att_fwd(qr, kr, vb, dil, *, name):
    lr = qr.shape[0]
    nb = lr // ATT_BLOCK
    blk = ATT_BLOCK

    def body(q_ref, kp_ref, kc_ref, vp_ref, vc_ref, o_ref, lse_ref):
        n = pl.program_id(1)
        qi = lax.broadcasted_iota(jnp.int32, (blk, 2 * blk), 0)
        ki = lax.broadcasted_iota(jnp.int32, (blk, 2 * blk), 1)
        dist = qi + blk - ki
        valid = (dist >= 0) & (dist <= blk) & ((ki >= blk) | (n > 0))
        valid = jnp.concatenate([valid, valid], axis=0)
        lane = lax.broadcasted_iota(jnp.int32, (blk, 128), 1)
        lse_all = jnp.zeros((blk, 128), F32)
        for hp in range(ATT_HEADS // 2):
            bs = slice(128 * hp, 128 * (hp + 1))
            kb = jnp.concatenate([kp_ref[:, bs], kc_ref[:, bs]], axis=0)
            vv = jnp.concatenate([vp_ref[:, bs], vc_ref[:, bs]], axis=0)
            s = jnp.where(valid, _nt(_stack_heads(q_ref[:, bs]), kb), NEG_INF)
            m = jnp.max(s, axis=-1, keepdims=True)
            p = jnp.exp(s - m)
            l = jnp.sum(p, axis=-1, keepdims=True)
            o = _nn((p * (1.0 / l)).astype(BF16), vv)
            o_ref[:, bs] = jnp.where(lane < ATT_HD, o[:blk], o[blk:])
            lse = m + jnp.log(l)
            lse_all = (lse_all + jnp.where(lane == 2 * hp, lse[:blk], 0.0)
                       + jnp.where(lane == 2 * hp + 1, lse[blk:], 0.0))
        lse_ref[...] = lse_all

    cur = pl.BlockSpec((blk, ATT_W), lambda r, n: (n, r))
    prev = pl.BlockSpec((blk, ATT_W), lambda r, n: (jnp.maximum(n - 1, 0), r))
    return pl.pallas_call(
        body, name=name, grid=(dil, nb), in_specs=[cur, prev, cur, prev, cur],
        out_specs=[cur, pl.BlockSpec((blk, 128), lambda r, n: (n, r))],
        out_shape=[jax.ShapeDtypeStruct(qr.shape, F32), jax.ShapeDtypeStruct((lr, dil * 128), F32)],
        compiler_params=_params(("parallel", "parallel")),
    )(qr, kr, kr, vb, vb)


def _att_bwd(qr, kr, vb, do, lse, delta, dil, *, name):
    lr = qr.shape[0]
    nb = lr // ATT_BLOCK
    blk = ATT_BLOCK

    def body(q_ref, kp_ref, kc_ref, vp_ref, vc_ref, do_ref, lse_ref, dl_ref, dq_ref, dk_ref, dv_ref, carry):
        n = pl.program_id(1)

        @pl.when(n == 0)
        def _():
            carry[...] = jnp.zeros_like(carry)

        @pl.when(n == nb)
        def _():
            dk_ref[...] = carry[0]
            dv_ref[...] = carry[1]

        @pl.when(n < nb)
        def _():
            qi = lax.broadcasted_iota(jnp.int32, (blk, 2 * blk), 0)
            ki = lax.broadcasted_iota(jnp.int32, (blk, 2 * blk), 1)
            dist = qi + blk - ki
            valid = (dist >= 0) & (dist <= blk) & ((ki >= blk) | (n > 0))
            valid = jnp.concatenate([valid, valid], axis=0)
            lane = lax.broadcasted_iota(jnp.int32, (blk, 128), 1)
            for hp in range(ATT_HEADS // 2):
                bs = slice(128 * hp, 128 * (hp + 1))
                kb = jnp.concatenate([kp_ref[:, bs], kc_ref[:, bs]], axis=0)
                vv = jnp.concatenate([vp_ref[:, bs], vc_ref[:, bs]], axis=0)
                q2 = _stack_heads(q_ref[:, bs])
                do2 = _stack_heads(do_ref[:, bs])
                lse2 = jnp.concatenate([lse_ref[:, 2 * hp:2 * hp + 1], lse_ref[:, 2 * hp + 1:2 * hp + 2]], axis=0)
                dl2 = jnp.concatenate([dl_ref[:, 2 * hp:2 * hp + 1], dl_ref[:, 2 * hp + 1:2 * hp + 2]], axis=0)
                p = jnp.where(valid, jnp.exp(_nt(q2, kb) - lse2), 0.0)
                ds16 = (p * (_nt(do2, vv) - dl2)).astype(BF16)
                dq2 = _nn(ds16, kb)
                dv_acc = _tn(p.astype(BF16), do2)
                dk_acc = _tn(ds16, q2)
                dq_ref[:, bs] = jnp.where(lane < ATT_HD, dq2[:blk], dq2[blk:])
                dk_ref[:, bs] = carry[0, :, bs] + dk_acc[:blk]
                dv_ref[:, bs] = carry[1, :, bs] + dv_acc[:blk]
                carry[0, :, bs] = dk_acc[blk:]
                carry[1, :, bs] = dv_acc[blk:]

    def at(n):
        return jnp.minimum(n, nb - 1)

    cur = pl.BlockSpec((blk, ATT_W), lambda r, n: (at(n), r))
    prev = pl.BlockSpec((blk, ATT_W), lambda r, n: (jnp.maximum(at(n) - 1, 0), r))
    nar = pl.BlockSpec((blk, 128), lambda r, n: (at(n), r))
    late = pl.BlockSpec((blk, ATT_W), lambda r, n: (jnp.maximum(n - 1, 0), r))
    out = jax.ShapeDtypeStruct(qr.shape, F32)
    return pl.pallas_call(
        body, name=name, grid=(dil, nb + 1), in_specs=[cur, prev, cur, prev, cur, cur, nar, nar],
        out_specs=[cur, late, late], out_shape=[out, out, out],
        scratch_shapes=[pltpu.VMEM((2, blk, ATT_W), F32)],
        compiler_params=_params(("parallel", "arbitrary")),
    )(qr, kr, kr, vb, vb, do, lse, delta)


def _att_prep_bwd(dgroups, tabs, du, *, name):
    s_len = du.shape[0]
    tm = VIEW_ROWS
    scale = ATT_HD ** -0.5
    nblk = ATT_W // 128
    ng = len(DILATIONS)

    def body(*refs):
        grads = refs[:3 * ng]
        c_ref, s1_ref, s2_ref, _, du_ref = refs[3 * ng:3 * ng + 5]
        scrs = refs[3 * ng + 5:]
        c, s1, s2 = c_ref[...], s1_ref[...], s2_ref[...]
        for part in range(3):
            for gi, dil in enumerate(DILATIONS):
                if dil > 1:
                    _from_view(grads[3 * gi + part], scrs[gi], dil)
            for b in range(nblk):
                tot = None
                for gi, dil in enumerate(DILATIONS):
                    term = grads[3 * gi + part][:, 128 * b:128 * (b + 1)] if dil == 1 else scrs[gi][b]
                    tot = term if tot is None else tot + term
                if part == 0:
                    tot = _rope_t(tot * scale, c, s1, s2)
                elif part == 1:
                    tot = _rope_t(tot, c, s1, s2)
                du_ref[:, ATT_W * part + 128 * b:ATT_W * part + 128 * (b + 1)] = tot.astype(du_ref.dtype)

    tab = pl.BlockSpec((tm, 128), lambda i: (i, 0))
    return pl.pallas_call(
        body, name=name, grid=(s_len // tm,),
        in_specs=[_view_spec(dil, ATT_W) for dil in DILATIONS for _ in range(3)] + [tab] * 3
        + [pl.BlockSpec(memory_space=pl.ANY)],
        out_specs=pl.BlockSpec((tm, 3 * ATT_W), lambda i: (i, COL_AQ // 3)),
        out_shape=jax.ShapeDtypeStruct(du.shape, du.dtype),
        scratch_shapes=[pltpu.VMEM((nblk, tm, 128), F32) for _ in DILATIONS],
        input_output_aliases={3 * ng + 3: 0},
        compiler_params=_params(("parallel",)),
    )(*[a for g in dgroups for a in g], *tabs, du)


def _head_weights(w, b):
    lane = lax.broadcasted_iota(jnp.int32, (1, 128), 1)
    return jnp.where(lane < ATT_HD, w[:, 2 * b:2 * b + 1], w[:, 2 * b + 1:2 * b + 2])


def _assemble_fwd(pw, u, o_gdn, gnw, o_groups, lse_groups, *, name):
    s_len = u.shape[0]
    tm = VIEW_ROWS
    c = CONV_CH
    nblk = ATT_W // 128
    ng = len(DILATIONS)

    def body(*refs):
        pw_ref, cg_ref, z_ref, ag_ref, og_ref, gnw_ref = refs[:6]
        o_refs, l_refs = refs[6:6 + ng], refs[6 + ng:6 + 2 * ng]
        y_ref, oa_ref = refs[6 + 2 * ng:8 + 2 * ng]
        lse_outs = refs[8 + 2 * ng:8 + 3 * ng]
        o_scr, l_scr = refs[8 + 3 * ng:8 + 4 * ng], refs[8 + 4 * ng:8 + 5 * ng]
        lse_scr = refs[8 + 5 * ng]
        y_ref[:, :c] = (pw_ref[...] * _silu(cg_ref[...])).astype(BF16)
        gw = gnw_ref[...]
        for h in range(GDN_HEADS):
            hs = slice(GDN_D * h, GDN_D * (h + 1))
            oh = og_ref[:, hs]
            yn = oh * lax.rsqrt(jnp.mean(oh * oh, axis=-1, keepdims=True) + 1e-6) * gw
            y_ref[:, c + GDN_D * h:c + GDN_D * (h + 1)] = (yn * _silu(z_ref[:, hs])).astype(BF16)
        for gi, dil in enumerate(DILATIONS):
            if dil > 1:
                _from_view(o_refs[gi], o_scr[gi], dil)
                _from_view(l_refs[gi], l_scr[gi], dil)
        ls = [l_refs[gi][...] if dil == 1 else l_scr[gi][0] for gi, dil in enumerate(DILATIONS)]
        m = functools.reduce(jnp.maximum, ls)
        es = [jnp.exp(l - m) for l in ls]
        den = functools.reduce(lambda a, b: a + b, es)
        lse_scr[0] = m + jnp.log(den)
        ws = [e / den for e in es]
        for b in range(nblk):
            bs = slice(128 * b, 128 * (b + 1))
            o = None
            for gi, dil in enumerate(DILATIONS):
                term = _head_weights(ws[gi], b) * (o_refs[gi][:, bs] if dil == 1 else o_scr[gi][b])
                o = term if o is None else o + term
            oa_ref[:, bs] = o
            y_ref[:, c + GDN_W + 128 * b:c + GDN_W + 128 * (b + 1)] = (o * _silu(ag_ref[:, bs])).astype(BF16)
        for gi, dil in enumerate(DILATIONS):
            _to_view(lse_scr, lse_outs[gi], dil, F32)

    wide = pl.BlockSpec((tm, 768), lambda i: (i, 0))
    return pl.pallas_call(
        body, name=name, grid=(s_len // tm,),
        in_specs=[pl.BlockSpec((tm, c), lambda i: (i, 0)), pl.BlockSpec((tm, c), lambda i: (i, 1024 // c)),
                  pl.BlockSpec((tm, 768), lambda i: (i, COL_GQ + 3)), pl.BlockSpec((tm, 768), lambda i: (i, COL_AQ + 3)),
                  wide, pl.BlockSpec((1, 128), lambda i: (0, 0))]
        + [_view_spec(dil, ATT_W) for dil in DILATIONS] + [_view_spec(dil, 128) for dil in DILATIONS],
        out_specs=[pl.BlockSpec((tm, D_MODEL), lambda i: (i, 0)), wide] + [_view_spec(dil, 128) for dil in DILATIONS],
        out_shape=[jax.ShapeDtypeStruct((s_len, D_MODEL), BF16), jax.ShapeDtypeStruct((s_len, ATT_W), F32)]
        + [_view_shape(s_len, dil, 128, F32) for dil in DILATIONS],
        scratch_shapes=[pltpu.VMEM((nblk, tm, 128), F32) for _ in DILATIONS]
        + [pltpu.VMEM((1, tm, 128), F32) for _ in DILATIONS] + [pltpu.VMEM((1, tm, 128), F32)],
        compiler_params=_params(("parallel",)),
    )(pw, u, u, u, o_gdn, gnw, *o_groups, *lse_groups)


def _assemble_bwd(dy, pw, u, o_gdn, gnw, o_att, *, name):
    s_len = u.shape[0]
    tm = VIEW_ROWS
    c = CONV_CH
    nsteps = s_len // tm
    nblk = ATT_W // 128
    ng = len(DILATIONS)

    def body(dy_ref, pw_ref, cg_ref, z_ref, ag_ref, og_ref, gnw_ref, oa_ref,
             du_ref, dpw_ref, dog_ref, dgw_ref, *rest):
        do_outs, dl_outs = rest[:ng], rest[ng:2 * ng]
        acc_ref, do_scr, dl_scr = rest[2 * ng:]
        i = pl.program_id(0)

        @pl.when(i == 0)
        def _():
            acc_ref[...] = jnp.zeros_like(acc_ref)

        du_ref[...] = jnp.zeros_like(du_ref)
        dyc = dy_ref[:, :c]
        cg = cg_ref[...]
        dpw_ref[...] = dyc * _silu(cg)
        du_ref[:, 1024:1024 + c] = (dyc * pw_ref[...] * _dsilu(cg)).astype(BF16)
        gw = gnw_ref[...]
        dgw = jnp.zeros((8, 128), F32)
        for h in range(GDN_HEADS):
            hs = slice(GDN_D * h, GDN_D * (h + 1))
            oh = og_ref[:, hs]
            zh = z_ref[:, hs]
            dyh = dy_ref[:, c + GDN_D * h:c + GDN_D * (h + 1)]
            r = lax.rsqrt(jnp.mean(oh * oh, axis=-1, keepdims=True) + 1e-6)
            xn = oh * r
            dyn = dyh * _silu(zh)
            du_ref[:, GDN_W * (COL_GQ + 3) + GDN_D * h:GDN_W * (COL_GQ + 3) + GDN_D * (h + 1)] = (
                dyh * xn * gw * _dsilu(zh)).astype(BF16)
            dgw = dgw + (dyn * xn).reshape(tm // 8, 8, 128).sum(axis=0)
            dxn = dyn * gw
            dog_ref[:, hs] = r * (dxn - xn * jnp.mean(dxn * xn, axis=-1, keepdims=True))
        acc_ref[...] += dgw
        lane = lax.broadcasted_iota(jnp.int32, (tm, 128), 1)
        delta = jnp.zeros((tm, 128), F32)
        for b in range(ATT_W // 128):
            bs = slice(128 * b, 128 * (b + 1))
            dya = dy_ref[:, c + GDN_W + 128 * b:c + GDN_W + 128 * (b + 1)]
            ag = ag_ref[:, bs]
            oa = oa_ref[:, bs]
            do = dya * _silu(ag)
            do_scr[b] = do
            du_ref[:, ATT_W * (COL_AQ + 3) + 128 * b:ATT_W * (COL_AQ + 3) + 128 * (b + 1)] = (
                dya * oa * _dsilu(ag)).astype(BF16)
            prod = do * oa
            lo = jnp.sum(jnp.where(lane < ATT_HD, prod, 0.0), axis=-1, keepdims=True)
            hi = jnp.sum(jnp.where(lane >= ATT_HD, prod, 0.0), axis=-1, keepdims=True)
            delta = delta + jnp.where(lane == 2 * b, lo, 0.0) + jnp.where(lane == 2 * b + 1, hi, 0.0)
        dl_scr[0] = delta
        for gi, dil in enumerate(DILATIONS):
            _to_view(do_scr, do_outs[gi], dil, BF16)
            _to_view(dl_scr, dl_outs[gi], dil, F32)

        @pl.when(i == nsteps - 1)
        def _():
            dgw_ref[...] = jnp.sum(acc_ref[...], axis=0, keepdims=True)

    wide = pl.BlockSpec((tm, 768), lambda i: (i, 0))
    vec = pl.BlockSpec((1, 128), lambda i: (0, 0))
    outs = pl.pallas_call(
        body, name=name, grid=(nsteps,),
        in_specs=[pl.BlockSpec((tm, D_MODEL), lambda i: (i, 0)), pl.BlockSpec((tm, c), lambda i: (i, 0)),
                  pl.BlockSpec((tm, c), lambda i: (i, 1024 // c)), pl.BlockSpec((tm, 768), lambda i: (i, COL_GQ + 3)),
                  pl.BlockSpec((tm, 768), lambda i: (i, COL_AQ + 3)), wide, vec, wide],
        out_specs=[pl.BlockSpec((tm, IN_WP), lambda i: (i, 0)), pl.BlockSpec((tm, c), lambda i: (i, 0)), wide, vec]
        + [_view_spec(dil, ATT_W) for dil in DILATIONS] + [_view_spec(dil, 128) for dil in DILATIONS],
        out_shape=[jax.ShapeDtypeStruct((s_len, IN_WP), BF16), jax.ShapeDtypeStruct((s_len, c), F32),
                   jax.ShapeDtypeStruct((s_len, GDN_W), F32), jax.ShapeDtypeStruct((1, 128), F32)]
        + [_view_shape(s_len, dil, ATT_W, BF16) for dil in DILATIONS]
        + [_view_shape(s_len, dil, 128, F32) for dil in DILATIONS],
        scratch_shapes=[pltpu.VMEM((8, 128), F32), pltpu.VMEM((nblk, tm, 128), F32), pltpu.VMEM((1, tm, 128), F32)],
        compiler_params=_params(("arbitrary",)),
    )(dy, pw, u, u, u, o_gdn, gnw, o_att)
    return outs[:4], outs[4:4 + ng], outs[4 + ng:]


def _layer_fwd(x, p, tabs):
    h = _rms_fwd(x, p["norm_w"], name="rms_fwd")
    u = _matmul(h, p["wp"], name="in_proj", tk=2048)
    conv, sw = _conf_fwd(u, p["dw_w"], p["dw_b"], p["ln_w"], p["ln_b"], name="conf_fwd")
    pw = _matmul(sw, p["pw_w"], name="conf_pw")
    qn, kn, vc, bg = _gdn_prep_fwd(u, p["cw"], p["al"], p["dtb"], name="gdn_prep_fwd")
    wk, wv, qd, kd, pm, t, g_cum = _gdn_intra_fwd(qn, kn, vc, bg, name="gdn_intra_fwd")
    o_gdn, vn, sprev = _gdn_scan_fwd(wk, wv, qd, kd, pm, g_cum, name="gdn_scan_fwd")
    qkv = _att_prep_fwd(u, tabs, name="att_prep_fwd")
    groups = [_att_fwd(*qkv[gi], dil, name=f"att_fwd_d{dil}") for gi, dil in enumerate(DILATIONS)]
    outs = _assemble_fwd(pw, u, o_gdn, p["gnw"], [g[0] for g in groups], [g[1] for g in groups],
                         name="assemble_fwd")
    y, o_att, lse = outs[0], outs[1], outs[2:]
    x_next = _matmul(y, p["wout"], add=x, name="out_proj", tk=2048)
    saved = dict(x=x, h=h, u=u, conv=conv, sw=sw, pw=pw, qn=qn, kn=kn, vc=vc, bg=bg, wk=wk, qd=qd, kd=kd, pm=pm,
                 t=t, g_cum=g_cum, vn=vn, sprev=sprev, o_gdn=o_gdn, qkv=qkv, o_att=o_att, lse=lse, y=y)
    return x_next, saved


def _layer_bwd(dx_out, s, p, tabs, layer, big):
    dy = _matmul(dx_out, p["wout"], tb=True, name="out_proj_dy", tk=2048)
    d_wout = _matmul(s["y"], dx_out, ta=True, name="out_proj_dw", tk=2048, stack=(big[1], layer, DEPTH))
    (du, dpw, dog, dgw), do_views, dl_views = _assemble_bwd(dy, s["pw"], s["u"], s["o_gdn"], p["gnw"], s["o_att"],
                                                            name="assemble_bwd")
    dsw = _matmul(dpw, p["pw_w"], tb=True, name="conf_pw_dx")
    d_pw_w = _matmul(s["sw"], dpw, ta=True, name="conf_pw_dw", stack=(big[2], layer, DEPTH))
    dconv, ln_sums = _conf_bwd_ln(dsw, s["conv"], p["ln_w"], p["ln_b"], name="conf_bwd_ln")
    du, d_dw_w = _conf_bwd_conv(s["u"], dconv, p["dw_w"], du, name="conf_bwd_conv")
    dvn, ds_all = _gdn_scan_bwd(dog, s["wk"], s["qd"], s["kd"], s["pm"], s["g_cum"], name="gdn_scan_bwd")
    dqkv, dbg = _gdn_intra_bwd(s["qn"], s["kn"], s["vc"], s["bg"], s["g_cum"], s["t"], dog, dvn, s["vn"],
                               s["sprev"], ds_all, name="gdn_intra_bwd")
    du, d_cw = _gdn_prep_bwd(s["u"], dqkv, p["cw"], du, name="gdn_prep_bwd")
    du, ba_sums = _gdn_ba_bwd(s["u"], dbg, p["al"], p["dtb"], du, name="gdn_ba_bwd")
    dgroups = []
    for gi, dil in enumerate(DILATIONS):
        args = (*s["qkv"][gi], do_views[gi], s["lse"][gi], dl_views[gi], dil)
        dgroups.append(_att_bwd(*args, name=f"att_bwd_d{dil}"))
    du = _att_prep_bwd(dgroups, tabs, du, name="att_prep_bwd")
    dh = _matmul(du, p["wp"], tb=True, name="in_proj_dx", tk=4096)
    d_wp = _matmul(s["h"], du, ta=True, name="in_proj_dw", tk=4096, stack=(big[0], layer, DEPTH))
    dx, d_norm_w = _rms_bwd(s["x"], dh, p["norm_w"], dx_out, name="rms_bwd")
    small = dict(norm_w=d_norm_w, gnw=dgw, ln_sums=ln_sums, dw_w=d_dw_w, cw=d_cw, ba_sums=ba_sums)
    return dx, (d_wp, d_wout, d_pw_w), small


def _trunk(x, target, params, final_norm_w):
    tabs = _rope_tables(x.shape[0])
    layers = [{k: v[l] for k, v in params.items()} for l in range(DEPTH)]
    saved = []
    for p in layers:
        x, s = _layer_fwd(x, p, tabs)
        saved.append(s)
    dx, d_final, loss = _loss_head(x, final_norm_w, target, name="loss_head")
    big = (None, None, None)
    small = [None] * DEPTH
    for l in reversed(range(DEPTH)):
        dx, big, small[l] = _layer_bwd(dx, saved[l], layers[l], tabs, l, big)
    grads = {k: jnp.stack([sm[k] for sm in small]) for k in small[0]}
    grads.update(wp=big[0], wout=big[1], pw_w=big[2])
    return loss[0, 0], dx, grads, d_final


ANY = pl.BlockSpec(memory_space=pl.ANY)


def _position():
    return lax.axis_index("x"), lax.axis_index("y"), lax.axis_index("c")


def _other_chips(x, y):
    return [(1 - x, y), (x, 1 - y), (1 - x, 1 - y)]


def _gather_chips(shards, *, name):
    n = len(shards)
    kinds = 12

    def body(*refs):
        ins, outs = refs[:n], refs[n:2 * n]
        send, recv = refs[2 * n:]
        x, y, c = _position()
        me, sib = (x, y, c), (x, y, 1 - c)
        xn, yn, dg = (1 - x, y), (x, 1 - y), (1 - x, 1 - y)
        pa, pb = 2 * c, 2 * c + 1

        def copy(k, a, chip, layer, to, src=None):
            dst = outs[a].at[2 * chip[0] + chip[1], pl.ds(layer, 1)]
            return pltpu.make_async_remote_copy(
                src_ref=dst if src is None else src, dst_ref=dst, send_sem=send.at[k * n + a],
                recv_sem=recv.at[k * n + a], device_id=to, device_id_type=MESH)

        def own(k, a, layer, chip):
            return copy(k, a, (x, y), layer, (*chip, c), src=ins[a].at[pl.ds(layer, 1)])

        sends = []
        for a in range(n):
            sends += [own(0, a, pa, xn), own(1, a, pb, yn), own(2, a, pb, xn), own(3, a, pa, yn)]
        for cp in sends:
            cp.start()
        arrivals = [(1, yn, pb, (4, xn)), (0, xn, pa, (5, yn)), (2, xn, pb, None), (3, yn, pa, None),
                    (4, dg, pb, None), (5, dg, pa, None)]
        for a in range(n):
            for j, (k, chip, layer, onward) in enumerate(arrivals):
                copy(k, a, chip, layer, me).wait_recv()
                if onward is not None:
                    cp = copy(onward[0], a, chip, layer, (*onward[1], c))
                    cp.start()
                    sends.append(cp)
                cp = copy(6 + j, a, chip, layer, sib)
                cp.start()
                sends.append(cp)
        for a in range(n):
            for j, (k, chip, layer, onward) in enumerate(arrivals):
                copy(6 + j, a, chip, layer + 2 - 4 * c, me).wait_recv()
        for cp in sends:
            cp.wait_send()

    return pl.pallas_call(
        body, name=name, in_specs=[ANY] * n, out_specs=[ANY] * n,
        out_shape=[jax.ShapeDtypeStruct((4,) + s.shape, s.dtype) for s in shards],
        scratch_shapes=[pltpu.SemaphoreType.DMA((kinds * n,)), pltpu.SemaphoreType.DMA((kinds * n,))],
    )(*shards)


def _to_sibling(arrs, *, name):
    n = len(arrs)

    def body(*refs):
        ins, outs = refs[:n], refs[n:2 * n]
        send, recv = refs[2 * n:]
        x, y, c = _position()
        cps = [pltpu.make_async_remote_copy(src_ref=ins[a], dst_ref=outs[a], send_sem=send.at[a],
                                            recv_sem=recv.at[a], device_id=(x, y, 1 - c), device_id_type=MESH)
               for a in range(n)]
        for cp in cps:
            cp.start()
        for cp in cps:
            cp.wait()

    return pl.pallas_call(
        body, name=name, in_specs=[ANY] * n, out_specs=[ANY] * n,
        out_shape=[jax.ShapeDtypeStruct(a.shape, a.dtype) for a in arrs],
        scratch_shapes=[pltpu.SemaphoreType.DMA((n,)), pltpu.SemaphoreType.DMA((n,))],
    )(*arrs)


def _to_chips(arrs, *, name):
    n = len(arrs)

    def body(*refs):
        ins, outs = refs[:n], refs[n:2 * n]
        send, recv = refs[2 * n:]
        x, y, c = _position()
        cps = [pltpu.make_async_remote_copy(
            src_ref=ins[a].at[2 * chip[0] + chip[1]], dst_ref=outs[a].at[j], send_sem=send.at[j * n + a],
            recv_sem=recv.at[j * n + a], device_id=(*chip, c), device_id_type=MESH)
            for j, chip in enumerate(_other_chips(x, y)) for a in range(n)]
        for cp in cps:
            cp.start()
        for cp in cps:
            cp.wait()

    return pl.pallas_call(
        body, name=name, in_specs=[ANY] * n, out_specs=[ANY] * n,
        out_shape=[jax.ShapeDtypeStruct((3,) + a.shape[1:], a.dtype) for a in arrs],
        scratch_shapes=[pltpu.SemaphoreType.DMA((3 * n,)), pltpu.SemaphoreType.DMA((3 * n,))],
    )(*arrs)


def _join_halves(fulls, *, name):
    n = len(fulls)

    def body(*refs):
        ins, outs = refs[:n], refs[n:2 * n]
        send, recv = refs[2 * n:]
        x, y, c = _position()

        def copy(a, rows):
            return pltpu.make_async_remote_copy(
                src_ref=ins[a].at[rows], dst_ref=outs[a].at[rows], send_sem=send.at[a], recv_sem=recv.at[a],
                device_id=(x, y, 1 - c), device_id_type=MESH)

        cps = [copy(a, pl.ds(2 * c, 2)) for a in range(n)]
        for cp in cps:
            cp.start()
        for a in range(n):
            cps[a].wait_send()
            copy(a, pl.ds(2 * (1 - c), 2)).wait_recv()

    return pl.pallas_call(
        body, name=name, in_specs=[ANY] * n, out_specs=[ANY] * n,
        out_shape=[jax.ShapeDtypeStruct(f.shape, f.dtype) for f in fulls],
        scratch_shapes=[pltpu.SemaphoreType.DMA((n,)), pltpu.SemaphoreType.DMA((n,))],
        input_output_aliases={a: a for a in range(n)},
    )(*fulls)


def _allreduce_small(packed, *, name):
    rows = packed.shape[0]
    ndev = 8

    def body(x_ref, sum_ref, all_ref, send, recv, lsem):
        x, y, c = _position()
        me, sib = (x, y, c), (x, y, 1 - c)
        chips = _other_chips(x, y)

        def blk(px, py, pc):
            return all_ref.at[pl.ds((4 * px + 2 * py + pc) * rows, rows), :]

        def copy(k, block, to, src=None):
            return pltpu.make_async_remote_copy(
                src_ref=blk(*block) if src is None else src, dst_ref=blk(*block), send_sem=send.at[k],
                recv_sem=recv.at[k], device_id=to, device_id_type=MESH)

        mine = pltpu.make_async_copy(x_ref, blk(*me), lsem)
        mine.start()
        first = [copy(0, me, sib, src=x_ref)] + [copy(1 + j, me, (*chip, c), src=x_ref) for j, chip in enumerate(chips)]
        for cp in first:
            cp.start()
        passed = [copy(4 + j, (*chip, c), sib) for j, chip in enumerate(chips)]
        for j, chip in enumerate(chips):
            copy(1 + j, (*chip, c), me).wait_recv()
            passed[j].start()
        copy(0, sib, me).wait_recv()
        for j, chip in enumerate(chips):
            copy(4 + j, (*chip, 1 - c), me).wait_recv()
        for cp in first + passed:
            cp.wait_send()
        mine.wait()
        acc = all_ref[0:rows, :]
        for d in range(1, ndev):
            acc = acc + all_ref[d * rows:(d + 1) * rows, :]
        sum_ref[...] = acc

    vm = pl.BlockSpec(memory_space=pltpu.VMEM)
    return pl.pallas_call(
        body, name=name, in_specs=[vm], out_specs=vm, out_shape=jax.ShapeDtypeStruct((rows, 128), F32),
        scratch_shapes=[pltpu.VMEM((ndev * rows, 128), F32), pltpu.SemaphoreType.DMA((7,)),
                        pltpu.SemaphoreType.DMA((7,)), pltpu.SemaphoreType.DMA],
        compiler_params=pltpu.CompilerParams(vmem_limit_bytes=VMEM_LIMIT),
    )(packed)


def _pack(arrs):
    flat = jnp.concatenate([a.reshape(-1) for a in arrs])
    pad = (-flat.shape[0]) % 1024
    return jnp.pad(flat, (0, pad)).reshape(-1, 128)


def _unpack(packed, shapes):
    flat = packed.reshape(-1)
    out, pos = [], 0
    for s in shapes:
        size = math.prod(s)
        out.append(flat[pos:pos + size].reshape(s))
        pos += size
    return out


def _pad_cols(w):
    zeros = jnp.zeros(w.shape[:-1] + (IN_WP - IN_W,), w.dtype)
    return jnp.concatenate([w[..., :ORIG_BA], w[..., ORIG_ATT:], w[..., ORIG_BA:ORIG_ATT], zeros], axis=-1)


def _chip_cols(j):
    per = IN_W // 4
    lo, hi = j * per, (j + 1) * per
    out = []
    for o0, o1, p0 in ((0, ORIG_BA, 0), (ORIG_BA, ORIG_ATT, COL_BA), (ORIG_ATT, IN_W, ORIG_BA)):
        a, b = max(lo, o0), min(hi, o1)
        if a < b:
            out.append((p0 + a - o0, p0 + b - o0))
    return out


def _shards_to_padded(g):
    pieces = []
    for j in range(4):
        loc = 0
        for p0, p1 in _chip_cols(j):
            pieces.append((p0, g[j][..., loc:loc + p1 - p0]))
            loc += p1 - p0
    pieces.sort(key=lambda t: t[0])
    zeros = jnp.zeros(g.shape[1:-1] + (IN_WP - IN_W,), g.dtype)
    return jnp.concatenate([p for _, p in pieces] + [zeros], axis=-1)


def _padded_to_shards(g, dtype):
    return jnp.stack([jnp.concatenate([g[..., p0:p1] for p0, p1 in _chip_cols(j)], axis=-1).astype(dtype)
                      for j in range(4)])


def _unpad_cols(w):
    n_att = IN_W - ORIG_ATT
    return jnp.concatenate([w[..., :ORIG_BA], w[..., COL_BA:COL_BA + ORIG_ATT - ORIG_BA],
                            w[..., ORIG_BA:ORIG_BA + n_att]], axis=-1)


def _lanes(v, first):
    return jnp.pad(v, ((0, 0), (first, 128 - first - v.shape[1])))[:, None, :]


def _by_chip(g, axis):
    shape = g.shape[:axis] + (4, g.shape[axis] // 4) + g.shape[axis + 1:]
    return jnp.moveaxis(g.reshape(shape), axis, 0)


def kernel(x, norm_w, w_in, conv_qkv_w, a_log, dt_bias, gdn_norm_w, conf_dw_w, conf_dw_b, conf_ln_w, conf_ln_b, conf_pw_w, w_out, final_norm_w, loss_target, m_norm_w, m_w_in, m_conv_qkv_w, m_a_log, m_dt_bias, m_gdn_norm_w, m_conf_dw_w, m_conf_dw_b, m_conf_ln_w, m_conf_ln_b, m_conf_pw_w, m_w_out, m_final_norm_w, v_norm_w, v_w_in, v_conv_qkv_w, v_a_log, v_dt_bias, v_gdn_norm_w, v_conf_dw_w, v_conf_dw_b, v_conf_ln_w, v_conf_ln_b, v_conf_pw_w, v_w_out, v_final_norm_w):
    xi, yi, ci = _position()
    chip = 2 * xi + yi

    shards = [w_in.astype(BF16), w_out.astype(BF16), conf_pw_w.astype(BF16), conv_qkv_w, conf_dw_w]
    g_in, g_out, g_pw, g_cw, g_dw = [
        lax.dynamic_update_slice_in_dim(g, s[None], chip, axis=0)
        for g, s in zip(_gather_chips(shards, name="gather_weights"), shards)]
    cw_full = jnp.moveaxis(g_cw, 0, 2).reshape(DEPTH, SHORT_CONV, 3 * GDN_W)
    dw_full = jnp.moveaxis(g_dw, 0, 2).reshape(DEPTH, CONV_WIDTH, CONV_CH)
    params = dict(
        norm_w=norm_w[:, None, :],
        wp=_shards_to_padded(g_in),
        wout=jnp.moveaxis(g_out, 0, 1).reshape(DEPTH, D_MODEL, D_MODEL),
        pw_w=jnp.moveaxis(g_pw, 0, 1).reshape(DEPTH, CONV_CH, CONV_CH),
        cw=jnp.pad(cw_full, ((0, 0), (0, SHALO - SHORT_CONV), (0, 0))),
        dw_w=jnp.pad(dw_full, ((0, 0), (0, HALO - CONV_WIDTH), (0, 0))),
        al=_lanes(a_log, GDN_HEADS), dtb=_lanes(dt_bias, GDN_HEADS), gnw=gdn_norm_w[:, None, :],
        dw_b=conf_dw_b[:, None, :], ln_w=conf_ln_w[:, None, :], ln_b=conf_ln_b[:, None, :],
    )

    loss_part, grad_x, grads, d_final = _trunk(x[0], loss_target[0], params, final_norm_w[None, :])
    loss = lax.psum(loss_part, ("x", "y", "c"))

    def half_by_chip(first, dtype):
        wp, wout, pw = [lax.dynamic_slice_in_dim(grads[k], first, 2, axis=0) for k in ("wp", "wout", "pw_w")]
        return [_padded_to_shards(wp, dtype), _by_chip(wout, 1).astype(dtype), _by_chip(pw, 1).astype(dtype)]

    keep = half_by_chip(2 * ci, F32)
    give = half_by_chip(2 * (1 - ci), BF16)
    got = _to_sibling(give, name="grads_to_sibling")
    pair = [_sum_arrays([k.reshape((8,) + k.shape[2:]), r.reshape((8,) + r.shape[2:])], name=f"pair_sum_{i}",
                        out_dtype=BF16).reshape(k.shape) for i, (k, r) in enumerate(zip(keep, got))]
    arrived = _to_chips(pair, name="grads_to_chips")
    halves = []
    for i, (pr, ar) in enumerate(zip(pair, arrived)):
        own = lax.dynamic_index_in_dim(pr, chip, axis=0, keepdims=False)
        halves.append(_sum_into_half([own, ar[0], ar[1], ar[2]], ci, name=f"chip_sum_{i}"))
    g_w_in, g_w_out, g_pw_w = _join_halves(halves, name="join_halves")

    ba = grads["ba_sums"]
    small = [grads["norm_w"], ba[:, 0:1, :], ba[:, 1:2, :], grads["gnw"], grads["ln_sums"][:, 2:3, :],
             grads["ln_sums"][:, 0:1, :], grads["ln_sums"][:, 1:2, :], d_final,
             grads["cw"][:, :SHORT_CONV, :], grads["dw_w"][:, :CONV_WIDTH, :]]
    red = _unpack(_allreduce_small(_pack(small), name="allreduce_small"), [s.shape for s in small])
    g_norm_w = red[0][:, 0, :]
    g_a_log = red[1][:, 0, GDN_HEADS:2 * GDN_HEADS]
    g_dt_bias = red[2][:, 0, GDN_HEADS:2 * GDN_HEADS]
    g_gnw, g_dw_b, g_ln_w, g_ln_b = red[3][:, 0, :], red[4][:, 0, :], red[5][:, 0, :], red[6][:, 0, :]
    g_final = red[7][0]
    g_cw = lax.dynamic_slice_in_dim(red[8], chip * (3 * GDN_W // 4), 3 * GDN_W // 4, axis=2)
    g_dw_w = lax.dynamic_slice_in_dim(red[9], chip * (CONV_CH // 4), CONV_CH // 4, axis=2)

    def cols_first(a):
        return jnp.transpose(a, (2, 0, 1))

    def cols_last(a):
        return jnp.transpose(a, (1, 2, 0))

    g_t = cols_first(g_w_in)
    g_w_in = cols_last(g_t)
    d_w_in, nm_w_in, nv_w_in = [cols_last(a) for a in _adamw(
        cols_first(w_in), g_t, cols_first(m_w_in), cols_first(v_w_in), name="adamw_w_in", by_lead=True)]
    d_w_out, nm_w_out, nv_w_out = _adamw(w_out, g_w_out, m_w_out, v_w_out, name="adamw_w_out")
    d_pw_w, nm_pw_w, nv_pw_w = _adamw(conf_pw_w, g_pw_w, m_conf_pw_w, v_conf_pw_w, name="adamw_pw")
    sw = [norm_w, a_log, dt_bias, gdn_norm_w, conf_dw_b, conf_ln_w, conf_ln_b, final_norm_w, conv_qkv_w, conf_dw_w]
    sg = [g_norm_w, g_a_log, g_dt_bias, g_gnw, g_dw_b, g_ln_w, g_ln_b, g_final, g_cw, g_dw_w]
    sm = [m_norm_w, m_a_log, m_dt_bias, m_gdn_norm_w, m_conf_dw_b, m_conf_ln_w, m_conf_ln_b, m_final_norm_w,
          m_conv_qkv_w, m_conf_dw_w]
    sv = [v_norm_w, v_a_log, v_dt_bias, v_gdn_norm_w, v_conf_dw_b, v_conf_ln_w, v_conf_ln_b, v_final_norm_w,
          v_conv_qkv_w, v_conf_dw_w]
    shapes = [a.shape for a in sw]
    packed = _adamw(_pack(sw)[None], _pack(sg)[None], _pack(sm)[None], _pack(sv)[None], name="adamw_small")
    sd, snm, snv = [_unpack(pk[0], shapes) for pk in packed]

    def order(big3, small10):
        s = small10
        return [s[0], big3[0], s[8], s[1], s[2], s[3], s[9], s[4], s[5], s[6], big3[2], big3[1], s[7]]

    return (loss, grad_x[None], *order([g_w_in, g_w_out, g_pw_w], sg),
            *order([d_w_in, d_w_out, d_pw_w], sd), *order([nm_w_in, nm_w_out, nm_pw_w], snm),
            *order([nv_w_in, nv_w_out, nv_pw_w], snv))
```

```python
import functools
import math

import jax
import jax.numpy as jnp
from jax import lax
from jax.experimental import pallas as pl
from jax.experimental.pallas import tpu as pltpu

F32, BF16 = jnp.float32, jnp.bfloat16
HIGHEST = lax.Precision.HIGHEST
MESH = pl.DeviceIdType.MESH

D_MODEL = 2048
DEPTH = 4
CONV_CH = 512
GDN_W = 768
GDN_HEADS = 6
GDN_D = 128
ATT_W = 768
ATT_HEADS = 12
ATT_HD = 64
CONV_WIDTH = 31
SHORT_CONV = 4
GDN_CHUNK = 64
ROPE_THETA = 500000.0
ROPE_DIM = ATT_HD // 4
DIL_PATTERNS = ((128, 1), (512, 4), (2048, 16))
ATT_BLOCK = 128
NEG_INF = -1e30
IN_W = 7692

IN_WP = 8192
COL_BA = 7680
ORIG_BA = 4608
ORIG_ATT = 4620

ADAM_LR = 0.001
ADAM_B1 = 0.9
ADAM_B2 = 0.999
ADAM_EPS = 1e-08
ADAM_WD = 0.01
ADAM_STEP = 10

VMEM_LIMIT = 56 * 1024 * 1024


def _params(sem=None):
    return pltpu.CompilerParams(dimension_semantics=sem, vmem_limit_bytes=VMEM_LIMIT)


def _sigmoid(x):
    return 0.5 * jnp.tanh(0.5 * x) + 0.5


def _silu(x):
    return x * _sigmoid(x)


def _dsilu(x):
    s = _sigmoid(x)
    return s * (1.0 + x * (1.0 - s))


def _dot(a, b, dims, precision=None):
    return lax.dot_general(a, b, (dims, ((), ())), precision=precision, preferred_element_type=F32)


def _nn(a, b, precision=None):
    return _dot(a, b, ((1,), (0,)), precision)


def _nt(a, b, precision=None):
    return _dot(a, b, ((1,), (1,)), precision)


def _tn(a, b, precision=None):
    return _dot(a, b, ((0,), (0,)), precision)


def _matmul(a, b, *, name, ta=False, tb=False, out_dtype=F32, add=None, stack=None, tm=1024, tn=1024, tk=1024):
    if ta:
        k_dim, m_dim = a.shape
    else:
        m_dim, k_dim = a.shape
    n_dim = b.shape[0] if tb else b.shape[1]
    tm, tn, tk = min(tm, m_dim), min(tn, n_dim), min(tk, k_dim)
    assert m_dim % tm == 0 and n_dim % tn == 0 and k_dim % tk == 0, (name, a.shape, b.shape)
    nk = k_dim // tk
    a_spec = pl.BlockSpec((tk, tm), lambda i, j, k: (k, i)) if ta else pl.BlockSpec((tm, tk), lambda i, j, k: (i, k))
    b_spec = pl.BlockSpec((tn, tk), lambda i, j, k: (j, k)) if tb else pl.BlockSpec((tk, tn), lambda i, j, k: (k, j))
    o_spec = pl.BlockSpec((tm, tn), lambda i, j, k: (i, j))
    out_shape = jax.ShapeDtypeStruct((m_dim, n_dim), out_dtype)
    dims = ((0 if ta else 1,), (1 if tb else 0,))
    has_add = add is not None
    ins = [a, b] + ([add] if has_add else [])
    specs = [a_spec, b_spec] + ([o_spec] if has_add else [])
    aliases = {}
    if stack is not None:
        buf, slab, nslabs = stack
        o_spec = pl.BlockSpec((None, tm, tn), lambda i, j, k: (slab, i, j))
        out_shape = jax.ShapeDtypeStruct((nslabs, m_dim, n_dim), out_dtype)
        if buf is not None:
            aliases = {len(ins): 0}
            ins.append(buf)
            specs.append(pl.BlockSpec(memory_space=pl.ANY))
    n_in = len(ins)

    def body(*refs):
        a_ref, b_ref = refs[0], refs[1]
        o_ref = refs[n_in]

        def finish(r):
            if has_add:
                r = r + refs[2][...]
            o_ref[...] = r.astype(out_dtype)

        prod = _dot(a_ref[...].astype(BF16), b_ref[...].astype(BF16), dims)
        if nk == 1:
            finish(prod)
            return
        acc_ref = refs[n_in + 1]
        k = pl.program_id(2)

        @pl.when(k == 0)
        def _():
            acc_ref[...] = prod

        @pl.when(k > 0)
        def _():
            acc_ref[...] += prod

        @pl.when(k == nk - 1)
        def _():
            finish(acc_ref[...])

    return pl.pallas_call(
        body, name=name, grid=(m_dim // tm, n_dim // tn, nk), in_specs=specs, out_specs=o_spec,
        out_shape=out_shape, scratch_shapes=[pltpu.VMEM((tm, tn), F32)] if nk > 1 else [],
        input_output_aliases=aliases,
        compiler_params=_params(("parallel", "parallel", "arbitrary")),
    )(*ins)


def _rms_fwd(x, w, *, name, tm=256):
    s_len, d = x.shape

    def body(x_ref, w_ref, h_ref):
        xv = x_ref[...]
        r = lax.rsqrt(jnp.mean(xv * xv, axis=-1, keepdims=True) + 1e-6)
        h_ref[...] = (xv * r * w_ref[...]).astype(BF16)

    return pl.pallas_call(
        body, name=name, grid=(s_len // tm,),
        in_specs=[pl.BlockSpec((tm, d), lambda i: (i, 0)), pl.BlockSpec((1, d), lambda i: (0, 0))],
        out_specs=pl.BlockSpec((tm, d), lambda i: (i, 0)),
        out_shape=jax.ShapeDtypeStruct((s_len, d), BF16),
        compiler_params=_params(("parallel",)),
    )(x, w)


def _rms_bwd(x, dh, w, dres, *, name, tm=256):
    s_len, d = x.shape
    nsteps = s_len // tm

    def body(x_ref, dh_ref, w_ref, dres_ref, dx_ref, dw_ref, acc_ref):
        i = pl.program_id(0)

        @pl.when(i == 0)
        def _():
            acc_ref[...] = jnp.zeros_like(acc_ref)

        xv = x_ref[...]
        r = lax.rsqrt(jnp.mean(xv * xv, axis=-1, keepdims=True) + 1e-6)
        xn = xv * r
        dy = dh_ref[...]
        dxn = dy * w_ref[...]
        dx_ref[...] = dres_ref[...] + r * (dxn - xn * jnp.mean(dxn * xn, axis=-1, keepdims=True))
        acc_ref[...] += (dy * xn).reshape(tm // 8, 8, d).sum(axis=0)

        @pl.when(i == nsteps - 1)
        def _():
            dw_ref[...] = jnp.sum(acc_ref[...], axis=0, keepdims=True)

    row = pl.BlockSpec((tm, d), lambda i: (i, 0))
    vec = pl.BlockSpec((1, d), lambda i: (0, 0))
    return pl.pallas_call(
        body, name=name, grid=(nsteps,), in_specs=[row, row, vec, row], out_specs=[row, vec],
        out_shape=[jax.ShapeDtypeStruct((s_len, d), F32), jax.ShapeDtypeStruct((1, d), F32)],
        scratch_shapes=[pltpu.VMEM((8, d), F32)],
        compiler_params=_params(("arbitrary",)),
    )(x, dh, w, dres)


def _loss_head(x, w, target, *, name, tm=256):
    s_len, d = x.shape
    nsteps = s_len // tm

    def body(x_ref, w_ref, t_ref, dx_ref, dw_ref, loss_ref, acc_ref, lacc_ref):
        i = pl.program_id(0)

        @pl.when(i == 0)
        def _():
            acc_ref[...] = jnp.zeros_like(acc_ref)
            lacc_ref[...] = jnp.zeros_like(lacc_ref)

        xv = x_ref[...]
        wv = w_ref[...]
        r = lax.rsqrt(jnp.mean(xv * xv, axis=-1, keepdims=True) + 1e-6)
        xn = xv * r
        err = xn * wv - t_ref[...]
        lacc_ref[...] += (err * err).reshape(tm // 8, 8, d).sum(axis=0)
        dy = err * (1.0 / d)
        dxn = dy * wv
        dx_ref[...] = r * (dxn - xn * jnp.mean(dxn * xn, axis=-1, keepdims=True))
        acc_ref[...] += (dy * xn).reshape(tm // 8, 8, d).sum(axis=0)

        @pl.when(i == nsteps - 1)
        def _():
            dw_ref[...] = jnp.sum(acc_ref[...], axis=0, keepdims=True)
            tot = jnp.sum(jnp.sum(lacc_ref[...], axis=0, keepdims=True), axis=1, keepdims=True)
            loss_ref[...] = jnp.broadcast_to(tot * (0.5 / d), (1, 128))

    row = pl.BlockSpec((tm, d), lambda i: (i, 0))
    vec = pl.BlockSpec((1, d), lambda i: (0, 0))
    return pl.pallas_call(
        body, name=name, grid=(nsteps,), in_specs=[row, vec, row],
        out_specs=[row, vec, pl.BlockSpec((1, 128), lambda i: (0, 0))],
        out_shape=[jax.ShapeDtypeStruct((s_len, d), F32), jax.ShapeDtypeStruct((1, d), F32),
                   jax.ShapeDtypeStruct((1, 128), F32)],
        scratch_shapes=[pltpu.VMEM((8, d), F32), pltpu.VMEM((8, d), F32)],
        compiler_params=_params(("arbitrary",)),
    )(x, w, target)


def _rows_block(shape, tr=256):
    lead, rows, cols = shape
    if rows % tr != 0:
        assert rows * cols <= 1 << 20, shape
        tr = rows
    return (lead, rows // tr), pl.BlockSpec((1, tr, cols), lambda a, i: (a, i, 0))


LEAD_BLOCK = 64


def _adamw(w, g, m, v, *, name, by_lead=False):
    if by_lead:
        lead, rows, cols = w.shape
        grid = (pl.cdiv(lead, LEAD_BLOCK), 1)
        spec = pl.BlockSpec((LEAD_BLOCK, rows, cols), lambda a, i: (a, 0, 0))
    else:
        grid, spec = _rows_block(w.shape)
    c1 = 1.0 / (1.0 - ADAM_B1 ** ADAM_STEP)
    c2 = 1.0 / (1.0 - ADAM_B2 ** ADAM_STEP)

    def body(w_ref, g_ref, m_ref, v_ref, d_ref, nm_ref, nv_ref):
        gv = g_ref[...]
        nm = ADAM_B1 * m_ref[...] + (1.0 - ADAM_B1) * gv
        nv = ADAM_B2 * v_ref[...] + (1.0 - ADAM_B2) * (gv * gv)
        nm_ref[...] = nm
        nv_ref[...] = nv
        d_ref[...] = -ADAM_LR * ((nm * c1) / (jnp.sqrt(nv * c2) + ADAM_EPS) + ADAM_WD * w_ref[...])

    out = jax.ShapeDtypeStruct(w.shape, F32)
    return pl.pallas_call(
        body, name=name, grid=grid, in_specs=[spec] * 4, out_specs=[spec] * 3, out_shape=[out] * 3,
        compiler_params=_params(("parallel", "parallel")),
    )(w, g, m, v)


def _sum_into_half(arrs, half, *, name):
    lead, rows, cols = arrs[0].shape
    assert lead == 2
    (_, nr), spec0 = _rows_block(arrs[0].shape)
    tr = spec0.block_shape[1]
    n = len(arrs)

    def body(half_ref, *refs):
        del half_ref
        acc = refs[0][...].astype(F32)
        for r in refs[1:n]:
            acc = acc + r[...].astype(F32)
        refs[n][...] = acc

    spec = pl.BlockSpec((1, tr, cols), lambda a, i, h: (a, i, 0))
    return pl.pallas_call(
        body, name=name,
        grid_spec=pltpu.PrefetchScalarGridSpec(
            num_scalar_prefetch=1, grid=(2, nr), in_specs=[spec] * n,
            out_specs=pl.BlockSpec((1, tr, cols), lambda a, i, h: (2 * h[0] + a, i, 0))),
        out_shape=jax.ShapeDtypeStruct((4, rows, cols), F32),
        compiler_params=_params(("parallel", "parallel")),
    )(jnp.reshape(half, (1,)).astype(jnp.int32), *arrs)


def _sum_arrays(arrs, *, name, out_dtype):
    grid, spec = _rows_block(arrs[0].shape)
    n = len(arrs)

    def body(*refs):
        acc = refs[0][...].astype(F32)
        for r in refs[1:n]:
            acc = acc + r[...].astype(F32)
        refs[n][...] = acc.astype(out_dtype)

    return pl.pallas_call(
        body, name=name, grid=grid, in_specs=[spec] * n, out_specs=spec,
        out_shape=jax.ShapeDtypeStruct(arrs[0].shape, out_dtype),
        compiler_params=_params(("parallel", "parallel")),
    )(*arrs)


HALO = 32


def _shifted_windows(buf, tm, offsets):
    rows = buf.shape[0]
    for b in range(8):
        group = [(k, s) for k, s in enumerate(offsets) if s % 8 == b]
        if not group:
            continue
        rb = buf if b == 0 else pltpu.roll(buf, rows - b, 0)
        for k, s in group:
            yield k, rb[s - b:s - b + tm, :]


def _conf_fwd(u, dw_w, dw_b, ln_w, ln_b, *, name, tm=256):
    s_len = u.shape[0]
    c = CONV_CH

    def body(uc_ref, up_ref, dww_ref, dwb_ref, lnw_ref, lnb_ref, conv_ref, sw_ref, hbuf):
        i = pl.program_id(0)
        hbuf[HALO:, :] = uc_ref[:, :c] * _sigmoid(uc_ref[:, c:])
        hp = up_ref[:, :c] * _sigmoid(up_ref[:, c:])
        hbuf[:HALO, :] = jnp.where(i > 0, hp, 0.0)
        for cb in range(c // 128):
            cs = slice(128 * cb, 128 * (cb + 1))
            acc = jnp.zeros((tm, 128), F32)
            taps = [HALO - CONV_WIDTH + 1 + j for j in range(CONV_WIDTH)]
            for j, win in _shifted_windows(hbuf[:, cs], tm, taps):
                acc = acc + win * dww_ref[j:j + 1, cs]
            conv_ref[:, cs] = acc + dwb_ref[:, cs]
        cv = conv_ref[...]
        mu = jnp.mean(cv, axis=-1, keepdims=True)
        xc = cv - mu
        var = jnp.mean(xc * xc, axis=-1, keepdims=True)
        ln = xc * lax.rsqrt(var + 1e-5) * lnw_ref[...] + lnb_ref[...]
        sw_ref[...] = _silu(ln).astype(BF16)

    vec = pl.BlockSpec((1, c), lambda i: (0, 0))
    return pl.pallas_call(
        body, name=name, grid=(s_len // tm,),
        in_specs=[pl.BlockSpec((tm, 2 * c), lambda i: (i, 0)),
                  pl.BlockSpec((HALO, 2 * c), lambda i: (jnp.maximum(i * (tm // HALO) - 1, 0), 0)),
                  pl.BlockSpec((HALO, c), lambda i: (0, 0)), vec, vec, vec],
        out_specs=[pl.BlockSpec((tm, c), lambda i: (i, 0))] * 2,
        out_shape=[jax.ShapeDtypeStruct((s_len, c), F32), jax.ShapeDtypeStruct((s_len, c), BF16)],
        scratch_shapes=[pltpu.VMEM((tm + HALO, c), F32)],
        compiler_params=_params(("parallel",)),
    )(u, u, dw_w, dw_b, ln_w, ln_b)


def _conf_bwd_ln(d_sw, conv, ln_w, ln_b, *, name, tm=256):
    s_len, c = conv.shape
    nsteps = s_len // tm

    def body(dsw_ref, conv_ref, lnw_ref, lnb_ref, dconv_ref, sums_ref):
        i = pl.program_id(0)

        @pl.when(i == 0)
        def _():
            sums_ref[...] = jnp.zeros_like(sums_ref)

        cv = conv_ref[...]
        mu = jnp.mean(cv, axis=-1, keepdims=True)
        xc = cv - mu
        rs = lax.rsqrt(jnp.mean(xc * xc, axis=-1, keepdims=True) + 1e-5)
        xhat = xc * rs
        lnw = lnw_ref[...]
        ln = xhat * lnw + lnb_ref[...]
        dln = dsw_ref[...] * _dsilu(ln)
        dxh = dln * lnw
        dconv = rs * (dxh - jnp.mean(dxh, axis=-1, keepdims=True)
                      - xhat * jnp.mean(dxh * xhat, axis=-1, keepdims=True))
        dconv_ref[...] = dconv
        sums_ref[0:1, :] += jnp.sum(dln * xhat, axis=0, keepdims=True)
        sums_ref[1:2, :] += jnp.sum(dln, axis=0, keepdims=True)
        sums_ref[2:3, :] += jnp.sum(dconv, axis=0, keepdims=True)

    row = pl.BlockSpec((tm, c), lambda i: (i, 0))
    vec = pl.BlockSpec((1, c), lambda i: (0, 0))
    return pl.pallas_call(
        body, name=name, grid=(nsteps,), in_specs=[row, row, vec, vec],
        out_specs=[row, pl.BlockSpec((8, c), lambda i: (0, 0))],
        out_shape=[jax.ShapeDtypeStruct((s_len, c), F32), jax.ShapeDtypeStruct((8, c), F32)],
        compiler_params=_params(("arbitrary",)),
    )(d_sw, conv, ln_w, ln_b)


def _conf_bwd_conv(u, dconv, dw_w, du, *, name, tm=256):
    s_len = u.shape[0]
    c = CONV_CH
    nsteps = s_len // tm
    off = HALO - CONV_WIDTH + 1

    def body(uc_ref, up_ref, dc_ref, dn_ref, dww_ref, du_in_ref, du_ref, ddw_ref, hbuf, dbuf, wacc):
        del du_in_ref
        i = pl.program_id(0)

        @pl.when(i == 0)
        def _():
            wacc[...] = jnp.zeros_like(wacc)

        hbuf[HALO:, :] = uc_ref[:, :c] * _sigmoid(uc_ref[:, c:])
        hp = up_ref[:, :c] * _sigmoid(up_ref[:, c:])
        hbuf[:HALO, :] = jnp.where(i > 0, hp, 0.0)
        dbuf[:tm, :] = dc_ref[...]
        dbuf[tm:, :] = jnp.where(i < nsteps - 1, dn_ref[...], 0.0)
        for cb in range(c // 128):
            cs = slice(128 * cb, 128 * (cb + 1))
            dcur = dbuf[0:tm, cs]
            acc = jnp.zeros((tm, 128), F32)
            for k, win in _shifted_windows(dbuf[:, cs], tm, list(range(CONV_WIDTH))):
                j = CONV_WIDTH - 1 - k
                acc = acc + win * dww_ref[j:j + 1, cs]
            for j, win in _shifted_windows(hbuf[:, cs], tm, [off + j for j in range(CONV_WIDTH)]):
                wacc[j, :, cs] += (win * dcur).reshape(tm // 8, 8, 128).sum(axis=0)
            a = uc_ref[:, cs]
            sg = _sigmoid(uc_ref[:, c + 128 * cb:c + 128 * (cb + 1)])
            du_ref[:, cs] = (acc * sg).astype(du_ref.dtype)
            du_ref[:, c + 128 * cb:c + 128 * (cb + 1)] = (acc * a * sg * (1.0 - sg)).astype(du_ref.dtype)

        @pl.when(i == nsteps - 1)
        def _():
            for j in range(CONV_WIDTH):
                ddw_ref[j:j + 1, :] = jnp.sum(wacc[j], axis=0, keepdims=True)
            ddw_ref[CONV_WIDTH:, :] = jnp.zeros((HALO - CONV_WIDTH, c), F32)

    return pl.pallas_call(
        body, name=name, grid=(nsteps,),
        in_specs=[pl.BlockSpec((tm, 2 * c), lambda i: (i, 0)),
                  pl.BlockSpec((HALO, 2 * c), lambda i: (jnp.maximum(i * (tm // HALO) - 1, 0), 0)),
                  pl.BlockSpec((tm, c), lambda i: (i, 0)),
                  pl.BlockSpec((HALO, c), lambda i: (jnp.minimum((i + 1) * (tm // HALO), s_len // HALO - 1), 0)),
                  pl.BlockSpec((HALO, c), lambda i: (0, 0)),
                  pl.BlockSpec(memory_space=pl.ANY)],
        out_specs=[pl.BlockSpec((tm, 2 * c), lambda i: (i, 0)), pl.BlockSpec((HALO, c), lambda i: (0, 0))],
        out_shape=[jax.ShapeDtypeStruct(du.shape, du.dtype), jax.ShapeDtypeStruct((HALO, c), F32)],
        scratch_shapes=[pltpu.VMEM((tm + HALO, c), F32), pltpu.VMEM((tm + HALO, c), F32),
                        pltpu.VMEM((CONV_WIDTH, 8, c), F32)],
        input_output_aliases={5: 0},
        compiler_params=_params(("arbitrary",)),
    )(u, u, dconv, dconv, dw_w, du)


COL_GQ = 1536 // GDN_W
COL_AQ = 4608 // ATT_W
SHALO = 8
INTRA_CHUNKS = 4


def _softplus(z):
    return jnp.maximum(z, 0.0) + jnp.log1p(jnp.exp(-jnp.abs(z)))


def _short_conv(buf, cw_ref, part, rows, first):
    acc = jnp.zeros((rows, GDN_W), F32)
    for j, win in _shifted_windows(buf[...], rows, [first + j for j in range(SHORT_CONV)]):
        acc = acc + win * cw_ref[j:j + 1, GDN_W * part:GDN_W * (part + 1)]
    return acc


def _gdn_prep_fwd(u, cw, al, dtb, *, name, tm=256):
    s_len = u.shape[0]
    first = SHALO - SHORT_CONV + 1

    def body(uq, uk, uv, pq, pk, pv, uba, cw_ref, al_ref, dtb_ref, qn_ref, kn_ref, vc_ref, bg_ref, buf):
        i = pl.program_id(0)

        def conv(cur, prev, part):
            buf[SHALO:, :] = cur[...]
            buf[:SHALO, :] = jnp.where(i > 0, prev[...], 0.0)
            return _silu(_short_conv(buf, cw_ref, part, tm, first))

        for part, (cur, prev, out, scale) in enumerate(
                ((uq, pq, qn_ref, GDN_D ** -0.5), (uk, pk, kn_ref, 1.0))):
            y = conv(cur, prev, part)
            for h in range(GDN_HEADS):
                hs = slice(GDN_D * h, GDN_D * (h + 1))
                yh = y[:, hs]
                out[:, hs] = yh * (lax.rsqrt(jnp.sum(yh * yh, axis=-1, keepdims=True) + 1e-6) * scale)
        vc_ref[...] = conv(uv, pv, 2)
        ba = uba[...]
        lane = lax.broadcasted_iota(jnp.int32, ba.shape, 1)
        g = -jnp.exp(al_ref[...]) * _softplus(ba + dtb_ref[...])
        bg_ref[...] = jnp.where(lane < GDN_HEADS, _sigmoid(ba), jnp.where(lane < 2 * GDN_HEADS, g, 0.0))

    def cur(col):
        return pl.BlockSpec((tm, GDN_W), lambda i: (i, col))

    def prev(col):
        return pl.BlockSpec((SHALO, GDN_W), lambda i: (jnp.maximum(i * (tm // SHALO) - 1, 0), col))

    vec = pl.BlockSpec((1, 128), lambda i: (0, 0))
    row = pl.BlockSpec((tm, GDN_W), lambda i: (i, 0))
    wide = jax.ShapeDtypeStruct((s_len, GDN_W), F32)
    return pl.pallas_call(
        body, name=name, grid=(s_len // tm,),
        in_specs=[cur(COL_GQ), cur(COL_GQ + 1), cur(COL_GQ + 2), prev(COL_GQ), prev(COL_GQ + 1), prev(COL_GQ + 2),
                  pl.BlockSpec((tm, 128), lambda i: (i, COL_BA // 128)),
                  pl.BlockSpec((SHALO, 3 * GDN_W), lambda i: (0, 0)), vec, vec],
        out_specs=[row, row, row, pl.BlockSpec((tm, 128), lambda i: (i, 0))],
        out_shape=[wide, wide, wide, jax.ShapeDtypeStruct((s_len, 128), F32)],
        scratch_shapes=[pltpu.VMEM((tm + SHALO, GDN_W), F32)],
        compiler_params=_params(("parallel",)),
    )(u, u, u, u, u, u, u, cw, al, dtb)


def _chunk_masks():
    c = GDN_CHUNK
    row = lax.broadcasted_iota(jnp.int32, (c, c), 0)
    col = lax.broadcasted_iota(jnp.int32, (c, c), 1)
    return row >= col, row > col


def _cum_decay(bg):
    c = GDN_CHUNK
    causal, _ = _chunk_masks()
    g_cum = _nn(causal.astype(F32), bg, HIGHEST)
    sel = (lax.broadcasted_iota(jnp.int32, (8, 128), 0) + GDN_HEADS
           == lax.broadcasted_iota(jnp.int32, (8, 128), 1)).astype(F32)
    return g_cum, _nt(sel, g_cum, HIGHEST)


def _bdot(a, b, ca, cb):
    return lax.dot_general(a, b, (((ca,), (cb,)), ((0,), (0,))), preferred_element_type=F32)


def _bnn(a, b):
    return _bdot(a, b, 2, 1)


def _bnt(a, b):
    return _bdot(a, b, 2, 2)


def _btn(a, b):
    return _bdot(a, b, 1, 1)


def _split(a):
    hi = a.astype(BF16)
    return hi, (a - hi.astype(F32)).astype(BF16)


def _bnn3(a, b):
    ah, al = _split(a)
    bh, bl = _split(b)
    return _bnn(ah, bh) + (_bnn(al, bh) + _bnn(ah, bl))


def _heads(ref, rows=slice(None)):
    return jnp.stack([ref[rows, GDN_D * h:GDN_D * (h + 1)] for h in range(GDN_HEADS)])


def _head_columns(a, first):
    return jnp.stack([a[:, first + h:first + h + 1] for h in range(GDN_HEADS)])


def _chunk_decay(g_cum, g_rows, bg):
    causal, _ = _chunk_masks()
    gc = _head_columns(g_cum, GDN_HEADS)
    gr = jnp.stack([g_rows[h:h + 1, :] for h in range(GDN_HEADS)])
    dec = jnp.where(causal, jnp.exp(jnp.where(causal, gc - gr, 0.0)), 0.0)
    return gc, _head_columns(bg, 0), dec


def _gdn_intra_fwd(qn, kn, vc, bg, *, name):
    s_len = qn.shape[0]
    c = GDN_CHUNK
    nch = INTRA_CHUNKS
    nsteps = s_len // (c * nch)

    def body(q_ref, k_ref, v_ref, bg_ref, wk_ref, wv_ref, qd_ref, kd_ref, p_ref, t_ref, g_ref):
        causal, strict = _chunk_masks()
        eye = (lax.broadcasted_iota(jnp.int32, (c, c), 0) == lax.broadcasted_iota(jnp.int32, (c, c), 1)).astype(F32)
        parts = []
        for ch in range(nch):
            rs = slice(c * ch, c * (ch + 1))
            bg = bg_ref[rs, :]
            g_cum, g_rows = _cum_decay(bg)
            g_ref[rs, :] = g_cum
            parts.append(_chunk_decay(g_cum, g_rows, bg) + (_heads(q_ref, rs), _heads(k_ref, rs), _heads(v_ref, rs)))
        gc, bc, dec, q, k, v = [jnp.concatenate([p[i] for p in parts], axis=0) for i in range(6)]
        k16 = k.astype(BF16)
        low = jnp.where(strict, bc * _bnt(k16, k16) * dec, 0.0)
        pw = -low
        t = eye + pw
        for _ in range(5):
            pw = _bnn3(pw, pw)
            t = t + _bnn3(t, pw)
        t16 = t.astype(BF16)
        eg = jnp.exp(gc)
        wk = _bnn(t16, (k * (bc * eg)).astype(BF16))
        wv = _bnn(t16, (v * bc).astype(BF16))
        pm = jnp.where(causal, _bnt(q.astype(BF16), k16) * dec, 0.0).astype(BF16)
        qd = q * eg
        kd = k * jnp.exp(gc[:, c - 1:c, :] - gc)
        for idx in range(nch * GDN_HEADS):
            ch, h = divmod(idx, GDN_HEADS)
            rs = slice(c * ch, c * (ch + 1))
            hs = slice(GDN_D * h, GDN_D * (h + 1))
            t_ref[h, rs, :] = t[idx]
            p_ref[h, rs, :] = pm[idx]
            wk_ref[rs, hs] = wk[idx].astype(BF16)
            wv_ref[rs, hs] = wv[idx]
            qd_ref[rs, hs] = qd[idx].astype(BF16)
            kd_ref[rs, hs] = kd[idx].astype(BF16)

    row = pl.BlockSpec((c * nch, GDN_W), lambda n: (n, 0))
    sq = pl.BlockSpec((GDN_HEADS, c * nch, c), lambda n: (0, n, 0))
    narrow = pl.BlockSpec((c * nch, 128), lambda n: (n, 0))
    w16 = jax.ShapeDtypeStruct((s_len, GDN_W), BF16)
    return pl.pallas_call(
        body, name=name, grid=(nsteps,), in_specs=[row, row, row, narrow],
        out_specs=[row, row, row, row, sq, sq, narrow],
        out_shape=[w16, jax.ShapeDtypeStruct((s_len, GDN_W), F32), w16, w16,
                   jax.ShapeDtypeStruct((GDN_HEADS, s_len, c), BF16),
                   jax.ShapeDtypeStruct((GDN_HEADS, s_len, c), F32),
                   jax.ShapeDtypeStruct((s_len, 128), F32)],
        compiler_params=_params(("parallel",)),
    )(qn, kn, vc, bg)


def _gdn_scan_fwd(wk, wv, qd, kd, p, g_cum, *, name):
    s_len = wk.shape[0]
    c = GDN_CHUNK
    nchunks = s_len // c

    def body(wk_ref, wv_ref, qd_ref, kd_ref, p_ref, g_ref, o_ref, vn_ref, sp_ref, st):
        @pl.when(pl.program_id(0) == 0)
        def _():
            st[...] = jnp.zeros_like(st)

        s = st[...]
        sp_ref[0] = s
        s16 = s.astype(BF16)
        vn16 = (_heads(wv_ref) - _bnn(_heads(wk_ref), s16)).astype(BF16)
        o = _bnn(_heads(qd_ref), s16) + _bnn(p_ref[...], vn16)
        gl = jnp.exp(_head_columns(g_ref[c - 1:c, :], GDN_HEADS))
        st[...] = s * gl + _btn(_heads(kd_ref), vn16)
        for h in range(GDN_HEADS):
            hs = slice(GDN_D * h, GDN_D * (h + 1))
            vn_ref[:, hs] = vn16[h]
            o_ref[:, hs] = o[h]

    row = pl.BlockSpec((c, GDN_W), lambda n: (n, 0))
    return pl.pallas_call(
        body, name=name, grid=(nchunks,),
        in_specs=[row, row, row, row, pl.BlockSpec((GDN_HEADS, c, c), lambda n: (0, n, 0)),
                  pl.BlockSpec((c, 128), lambda n: (n, 0))],
        out_specs=[row, row, pl.BlockSpec((1, GDN_HEADS, GDN_D, GDN_D), lambda n: (n, 0, 0, 0))],
        out_shape=[jax.ShapeDtypeStruct((s_len, GDN_W), F32), jax.ShapeDtypeStruct((s_len, GDN_W), BF16),
                   jax.ShapeDtypeStruct((nchunks, GDN_HEADS, GDN_D, GDN_D), F32)],
        scratch_shapes=[pltpu.VMEM((GDN_HEADS, GDN_D, GDN_D), F32)],
        compiler_params=_params(("arbitrary",)),
    )(wk, wv, qd, kd, p, g_cum)


def _gdn_scan_bwd(do, wk, qd, kd, p, g_cum, *, name):
    s_len = wk.shape[0]
    c = GDN_CHUNK
    nchunks = s_len // c

    def body(do_ref, wk_ref, qd_ref, kd_ref, p_ref, g_ref, dvn_ref, ds_ref, dst):
        @pl.when(pl.program_id(0) == 0)
        def _():
            dst[...] = jnp.zeros_like(dst)

        ds = dst[...]
        ds_ref[0] = ds
        do16 = _heads(do_ref).astype(BF16)
        dvn16 = (_btn(p_ref[...], do16) + _bnn(_heads(kd_ref), ds.astype(BF16))).astype(BF16)
        gl = jnp.exp(_head_columns(g_ref[c - 1:c, :], GDN_HEADS))
        dst[...] = _btn(_heads(qd_ref), do16) + ds * gl - _btn(_heads(wk_ref), dvn16)
        for h in range(GDN_HEADS):
            dvn_ref[:, GDN_D * h:GDN_D * (h + 1)] = dvn16[h]

    row = pl.BlockSpec((c, GDN_W), lambda n: (nchunks - 1 - n, 0))
    return pl.pallas_call(
        body, name=name, grid=(nchunks,),
        in_specs=[row, row, row, row, pl.BlockSpec((GDN_HEADS, c, c), lambda n: (0, nchunks - 1 - n, 0)),
                  pl.BlockSpec((c, 128), lambda n: (nchunks - 1 - n, 0))],
        out_specs=[row, pl.BlockSpec((1, GDN_HEADS, GDN_D, GDN_D), lambda n: (nchunks - 1 - n, 0, 0, 0))],
        out_shape=[jax.ShapeDtypeStruct((s_len, GDN_W), BF16),
                   jax.ShapeDtypeStruct((nchunks, GDN_HEADS, GDN_D, GDN_D), F32)],
        scratch_shapes=[pltpu.VMEM((GDN_HEADS, GDN_D, GDN_D), F32)],
        compiler_params=_params(("arbitrary",)),
    )(do, wk, qd, kd, p, g_cum)


def _gdn_intra_bwd(qn, kn, vc, bg, g_cum, t, do, dvn, vn, sprev, ds_all, *, name):
    s_len = qn.shape[0]
    c = GDN_CHUNK
    nch = INTRA_CHUNKS
    nsteps = s_len // (c * nch)
    nb = nch * GDN_HEADS

    def body(q_ref, k_ref, v_ref, bg_ref, g_ref, t_ref, do_ref, dvn_ref, vn_ref, sp_ref, ds_ref,
             dqkv_ref, dbg_ref):
        causal, strict = _chunk_masks()
        lane = lax.broadcasted_iota(jnp.int32, (c, 128), 1)
        rowi = lax.broadcasted_iota(jnp.int32, (c, 128), 0)
        parts = []
        for ch in range(nch):
            rs = slice(c * ch, c * (ch + 1))
            bg = bg_ref[rs, :]
            _, g_rows = _cum_decay(bg)
            parts.append(_chunk_decay(g_ref[rs, :], g_rows, bg) + tuple(
                _heads(r, rs) for r in (q_ref, k_ref, v_ref, do_ref, dvn_ref, vn_ref)) + (t_ref[:, rs, :],))
        gc, bc, dec, q, k, v, do, dvn16, vn16, tm = [jnp.concatenate([p[i] for p in parts], axis=0)
                                                     for i in range(10)]
        q16, k16 = q.astype(BF16), k.astype(BF16)
        kk = _bnt(k16, k16)
        low = jnp.where(strict, bc * kk * dec, 0.0)
        eg = jnp.exp(gc)
        g_last = gc[:, c - 1:c, :]
        kdec = jnp.exp(g_last - gc)
        kb, vb, qd, kd = k * (bc * eg), v * bc, q * eg, k * kdec
        pm = jnp.where(causal, _bnt(q16, k16) * dec, 0.0)
        s = sp_ref[...].reshape(nb, GDN_D, GDN_D)
        ds = ds_ref[...].reshape(nb, GDN_D, GDN_D)
        s16, ds16 = s.astype(BF16), ds.astype(BF16)
        do16 = do.astype(BF16)
        t16 = tm.astype(BF16)

        dqd = _bnt(do16, s16)
        dp = jnp.where(causal, _bnt(do16, vn16), 0.0)
        dkd = _bnt(vn16, ds16)
        dgl = jnp.sum(jnp.sum(s * ds, axis=2, keepdims=True), axis=1, keepdims=True) * jnp.exp(g_last)
        dwk16 = (-_bnt(dvn16, s16)).astype(BF16)
        dt = _bnt(dwk16, kb.astype(BF16)) + _bnt(dvn16, vb.astype(BF16))
        dkb = _btn(t16, dwk16)
        dvb = _btn(t16, dvn16)
        th, tl = _split(tm)
        dth, dtl = _split(dt)
        xm = _btn(th, dth) + (_btn(tl, dth) + _btn(th, dtl))
        xh, xl = _split(xm)
        dlow = jnp.where(strict, -(_bnt(xh, th) + (_bnt(xl, th) + _bnt(xh, tl))), 0.0)
        dkk16 = (dlow * bc * dec).astype(BF16)
        dqk16 = (dp * dec).astype(BF16)

        dq = _bnn(dqk16, k16) + dqd * eg
        dk = _btn(dqk16, q16) + _bnn(dkk16, k16) + _btn(dkk16, k16) + dkb * (bc * eg) + dkd * kdec
        dv = dvb * bc
        for idx in range(nb):
            ch, h = divmod(idx, GDN_HEADS)
            rs = slice(c * ch, c * (ch + 1))
            hs = slice(GDN_D * h, GDN_D * (h + 1))
            dqkv_ref[0, rs, hs] = dq[idx]
            dqkv_ref[1, rs, hs] = dk[idx]
            dqkv_ref[2, rs, hs] = dv[idx]

        dbeta = (jnp.sum(dlow * kk * dec, axis=2, keepdims=True)
                 + jnp.sum(dkb * k, axis=2, keepdims=True) * eg + jnp.sum(dvb * v, axis=2, keepdims=True))
        mm = dlow * low + dp * pm
        mh, ml = _split(mm)
        ones16 = jnp.ones((nb, c, 128), BF16)
        col_sum = (_btn(mh, ones16) + _btn(ml, ones16))[:, :, 0:1]
        dkd_sum = jnp.sum(dkd * kd, axis=2, keepdims=True)
        dg = (jnp.sum(mm, axis=2, keepdims=True) - col_sum + jnp.sum(dkb * kb, axis=2, keepdims=True)
              + jnp.sum(dqd * qd, axis=2, keepdims=True) - dkd_sum)
        tail = jnp.sum(dkd_sum, axis=1, keepdims=True) + dgl
        upper = (lax.broadcasted_iota(jnp.int32, (c, c), 0) <= lax.broadcasted_iota(jnp.int32, (c, c), 1)).astype(F32)
        for ch in range(nch):
            dbeta_all = jnp.zeros((c, 128), F32)
            dg_all = jnp.zeros((c, 128), F32)
            for h in range(GDN_HEADS):
                idx = ch * GDN_HEADS + h
                dbeta_all = dbeta_all + jnp.where(lane == h, dbeta[idx], 0.0)
                dg_all = dg_all + jnp.where(lane == GDN_HEADS + h,
                                            dg[idx] + jnp.where(rowi == c - 1, tail[idx], 0.0), 0.0)
            dbg_ref[c * ch:c * (ch + 1), :] = dbeta_all + _nn(upper, dg_all, HIGHEST)

    row = pl.BlockSpec((c * nch, GDN_W), lambda n: (n, 0))
    narrow = pl.BlockSpec((c * nch, 128), lambda n: (n, 0))
    state = pl.BlockSpec((nch, GDN_HEADS, GDN_D, GDN_D), lambda n: (n, 0, 0, 0))
    return pl.pallas_call(
        body, name=name, grid=(nsteps,),
        in_specs=[row, row, row, narrow, narrow, pl.BlockSpec((GDN_HEADS, c * nch, c), lambda n: (0, n, 0)),
                  row, row, row, state, state],
        out_specs=[pl.BlockSpec((3, c * nch, GDN_W), lambda n: (0, n, 0)), narrow],
        out_shape=[jax.ShapeDtypeStruct((3, s_len, GDN_W), F32), jax.ShapeDtypeStruct((s_len, 128), F32)],
        compiler_params=_params(("parallel",)),
    )(qn, kn, vc, bg, g_cum, t, do, dvn, vn, sprev, ds_all)


def _gdn_prep_bwd(u, dqkv, cw, du, *, name, tm=256):
    s_len = u.shape[0]
    nsteps = s_len // tm
    ext = tm + SHALO

    def body(uc, up, un, dc, dn, cw_ref, du_in_ref, du_ref, dcw_ref, xbuf, dbuf, pbuf, wacc):
        del du_in_ref
        part = pl.program_id(0)
        i = pl.program_id(1)

        @pl.when(i == 0)
        def _():
            wacc[...] = jnp.zeros_like(wacc)

        xbuf[:SHALO, :] = jnp.where(i > 0, up[...], 0.0)
        xbuf[SHALO:SHALO + tm, :] = uc[...]
        xbuf[SHALO + tm:, :] = jnp.where(i < nsteps - 1, un[...], 0.0)
        dbuf[:tm, :] = dc[...]
        dbuf[tm:, :] = jnp.where(i < nsteps - 1, dn[...], 0.0)
        first = SHALO - SHORT_CONV + 1
        w = [cw_ref[j:j + 1, :] for j in range(SHORT_CONV)]
        taps = [first + j for j in range(SHORT_CONV)]
        xv = xbuf[...]
        pre = jnp.zeros((ext, GDN_W), F32)
        for j, win in _shifted_windows(xv, ext, taps):
            pre = pre + win * w[j]
        y = _silu(pre)
        dout = dbuf[...]
        scale = jnp.where(part == 0, GDN_D ** -0.5, 1.0)
        for h in range(GDN_HEADS):
            hs = slice(GDN_D * h, GDN_D * (h + 1))
            yh, dh = y[:, hs], dout[:, hs]
            rs = lax.rsqrt(jnp.sum(yh * yh, axis=-1, keepdims=True) + 1e-6)
            dyn = scale * rs * (dh - yh * (rs * rs) * jnp.sum(dh * yh, axis=-1, keepdims=True))
            dy = jnp.where(part < 2, dyn, dh)
            pbuf[:, hs] = dy * _dsilu(pre[:, hs])
        acc = jnp.zeros((tm, GDN_W), F32)
        dpre = pbuf[0:tm, :]
        for k, win in _shifted_windows(pbuf[...], tm, list(range(SHORT_CONV))):
            acc = acc + win * w[SHORT_CONV - 1 - k]
        for j, win in _shifted_windows(xv, tm, taps):
            wacc[j] += (win * dpre).reshape(tm // 8, 8, GDN_W).sum(axis=0)
        du_ref[...] = acc.astype(du_ref.dtype)

        @pl.when(i == nsteps - 1)
        def _():
            for j in range(SHORT_CONV):
                dcw_ref[j:j + 1, :] = jnp.sum(wacc[j], axis=0, keepdims=True)
            dcw_ref[SHORT_CONV:, :] = jnp.zeros((SHALO - SHORT_CONV, GDN_W), F32)

    per = tm // SHALO
    return pl.pallas_call(
        body, name=name, grid=(3, nsteps),
        in_specs=[pl.BlockSpec((tm, GDN_W), lambda p, i: (i, COL_GQ + p)),
                  pl.BlockSpec((SHALO, GDN_W), lambda p, i: (jnp.maximum(i * per - 1, 0), COL_GQ + p)),
                  pl.BlockSpec((SHALO, GDN_W), lambda p, i: (jnp.minimum((i + 1) * per, s_len // SHALO - 1), COL_GQ + p)),
                  pl.BlockSpec((None, tm, GDN_W), lambda p, i: (p, i, 0)),
                  pl.BlockSpec((None, SHALO, GDN_W), lambda p, i: (p, jnp.minimum((i + 1) * per, s_len // SHALO - 1), 0)),
                  pl.BlockSpec((SHALO, GDN_W), lambda p, i: (0, p)),
                  pl.BlockSpec(memory_space=pl.ANY)],
        out_specs=[pl.BlockSpec((tm, GDN_W), lambda p, i: (i, COL_GQ + p)),
                   pl.BlockSpec((SHALO, GDN_W), lambda p, i: (0, p))],
        out_shape=[jax.ShapeDtypeStruct(du.shape, du.dtype), jax.ShapeDtypeStruct((SHALO, 3 * GDN_W), F32)],
        scratch_shapes=[pltpu.VMEM((tm + 2 * SHALO, GDN_W), F32), pltpu.VMEM((ext, GDN_W), F32),
                        pltpu.VMEM((ext, GDN_W), F32), pltpu.VMEM((SHORT_CONV, 8, GDN_W), F32)],
        input_output_aliases={6: 0},
        compiler_params=_params(("arbitrary", "arbitrary")),
    )(u, u, u, dqkv, dqkv, cw, du)


def _gdn_ba_bwd(u, dbg, al, dtb, du, *, name, tm=256):
    s_len = u.shape[0]
    nsteps = s_len // tm
    wpad = IN_WP - COL_BA

    def body(uba, dbg_ref, al_ref, dtb_ref, du_in_ref, du_ref, sums_ref):
        del du_in_ref
        i = pl.program_id(0)

        @pl.when(i == 0)
        def _():
            sums_ref[...] = jnp.zeros_like(sums_ref)

        ba = uba[...]
        dbg = dbg_ref[...]
        lane = lax.broadcasted_iota(jnp.int32, ba.shape, 1)
        is_g = (lane >= GDN_HEADS) & (lane < 2 * GDN_HEADS)
        beta = _sigmoid(ba)
        z = ba + dtb_ref[...]
        ea = jnp.exp(al_ref[...])
        g = -ea * _softplus(z)
        dz = jnp.where(is_g, dbg * (-ea) * _sigmoid(z), 0.0)
        du_ref[:, :128] = jnp.where(lane < GDN_HEADS, dbg * beta * (1.0 - beta), dz).astype(du_ref.dtype)
        du_ref[:, 128:] = jnp.zeros((tm, wpad - 128), du_ref.dtype)
        sums_ref[0:1, :] += jnp.sum(jnp.where(is_g, dbg * g, 0.0), axis=0, keepdims=True)
        sums_ref[1:2, :] += jnp.sum(dz, axis=0, keepdims=True)

    vec = pl.BlockSpec((1, 128), lambda i: (0, 0))
    return pl.pallas_call(
        body, name=name, grid=(nsteps,),
        in_specs=[pl.BlockSpec((tm, 128), lambda i: (i, COL_BA // 128)), pl.BlockSpec((tm, 128), lambda i: (i, 0)),
                  vec, vec, pl.BlockSpec(memory_space=pl.ANY)],
        out_specs=[pl.BlockSpec((tm, wpad), lambda i: (i, COL_BA // wpad)), pl.BlockSpec((8, 128), lambda i: (0, 0))],
        out_shape=[jax.ShapeDtypeStruct(du.shape, du.dtype), jax.ShapeDtypeStruct((8, 128), F32)],
        input_output_aliases={4: 0},
        compiler_params=_params(("arbitrary",)),
    )(u, dbg, al, dtb, du)


def _rope_tables(s_len):
    half = ROPE_DIM // 2
    inv = ROPE_THETA ** (-jnp.arange(half, dtype=F32) / half)
    ang = jnp.arange(s_len, dtype=F32)[:, None] * inv[None, :]
    cos, sin = jnp.cos(ang), jnp.sin(ang)
    one = jnp.ones((s_len, ATT_HD - ROPE_DIM), F32)
    zero = jnp.zeros((s_len, ATT_HD - ROPE_DIM), F32)
    zh = jnp.zeros((s_len, half), F32)
    c = jnp.concatenate([cos, cos, one], axis=1)
    s1 = jnp.concatenate([-sin, zh, zero], axis=1)
    s2 = jnp.concatenate([zh, sin, zero], axis=1)
    return tuple(jnp.concatenate([t, t], axis=1) for t in (c, s1, s2))


def _rope(x, c, s1, s2):
    return x * c + pltpu.roll(x, 128 - ROPE_DIM // 2, 1) * s1 + pltpu.roll(x, ROPE_DIM // 2, 1) * s2


def _rope_t(dy, c, s1, s2):
    return dy * c + pltpu.roll(dy * s1, ROPE_DIM // 2, 1) + pltpu.roll(dy * s2, 128 - ROPE_DIM // 2, 1)


DILATIONS = tuple(d for _, d in DIL_PATTERNS)
ATT_QBLOCKS = 2
VIEW_ROWS = 256


def _to_view(scr, out_ref, dil, dtype):
    nblk, rows, _ = scr.shape
    width = nblk * 128
    for b in range(nblk):
        if dil == 1:
            out_ref[:, 128 * b:128 * (b + 1)] = scr[b].astype(dtype)
            continue
        for r in range(dil):
            out_ref[:, r * width + 128 * b:r * width + 128 * (b + 1)] = (
                scr.at[b][pl.ds(r, rows // dil, stride=dil), :].astype(dtype))


def _from_view(in_ref, scr, dil):
    nblk, rows, _ = scr.shape
    width = nblk * 128
    for b in range(nblk):
        for r in range(dil):
            scr.at[b][pl.ds(r, rows // dil, stride=dil), :] = in_ref[:, r * width + 128 * b:r * width + 128 * (b + 1)]


def _view_spec(dil, width, tm=VIEW_ROWS):
    return pl.BlockSpec((tm // dil, dil * width), lambda i: (i, 0))


def _view_shape(s_len, dil, width, dtype):
    return jax.ShapeDtypeStruct((s_len // dil, dil * width), dtype)


def _att_prep_fwd(u, tabs, *, name):
    s_len = u.shape[0]
    tm = VIEW_ROWS
    scale = ATT_HD ** -0.5
    nblk = ATT_W // 128

    def body(uq, uk, uv, c_ref, s1_ref, s2_ref, *rest):
        outs, scr = rest[:-1], rest[-1]
        c, s1, s2 = c_ref[...], s1_ref[...], s2_ref[...]
        for part, src in enumerate((uq, uk, uv)):
            for b in range(nblk):
                xb = src[:, 128 * b:128 * (b + 1)]
                if part == 0:
                    xb = _rope(xb, c, s1, s2) * scale
                elif part == 1:
                    xb = _rope(xb, c, s1, s2)
                scr[b] = xb
            for gi, dil in enumerate(DILATIONS):
                _to_view(scr, outs[3 * gi + part], dil, BF16)

    tab = pl.BlockSpec((tm, 128), lambda i: (i, 0))
    outs = pl.pallas_call(
        body, name=name, grid=(s_len // tm,),
        in_specs=[pl.BlockSpec((tm, ATT_W), lambda i, col=COL_AQ + j: (i, col)) for j in range(3)] + [tab] * 3,
        out_specs=[_view_spec(dil, ATT_W) for dil in DILATIONS for _ in range(3)],
        out_shape=[_view_shape(s_len, dil, ATT_W, BF16) for dil in DILATIONS for _ in range(3)],
        scratch_shapes=[pltpu.VMEM((nblk, tm, 128), F32)],
        compiler_params=_params(("parallel",)),
    )(u, u, u, *tabs)
    return [outs[3 * gi:3 * gi + 3] for gi in range(len(DILATIONS))]


def _stack_heads(x):
    lane = lax.broadcasted_iota(jnp.int32, (1, 128), 1)
    zero = jnp.zeros_like(x)
    return jnp.concatenate([jnp.where(lane < ATT_HD, x, zero), jnp.where(lane >= ATT_HD, x, zero)], axis=0)


def _att_fwd(qr, kr, vb, dil, *, name):
    lr = qr.shape[0]
    blk = ATT_BLOCK
    qb = ATT_QBLOCKS
    nsteps = lr // (blk * qb)

    def body(q_ref, kp_ref, kc_ref, vp_ref, vc_ref, o_ref, lse_ref):
        n = pl.program_id(1)
        qi = lax.broadcasted_iota(jnp.int32, (blk, 2 * blk), 0)
        ki = lax.broadcasted_iota(jnp.int32, (blk, 2 * blk), 1)
        dist = qi + blk - ki
        band = (dist >= 0) & (dist <= blk)
        lane = lax.broadcasted_iota(jnp.int32, (blk, 128), 1)
        for sub in range(qb):
            rs = slice(blk * sub, blk * (sub + 1))
            valid = band if sub > 0 else band & ((ki >= blk) | (n > 0))
            valid = jnp.concatenate([valid, valid], axis=0)
            lse_all = jnp.zeros((blk, 128), F32)
            for hp in range(ATT_HEADS // 2):
                bs = slice(128 * hp, 128 * (hp + 1))
                if sub == 0:
                    kb = jnp.concatenate([kp_ref[:, bs], kc_ref[0:blk, bs]], axis=0)
                    vv = jnp.concatenate([vp_ref[:, bs], vc_ref[0:blk, bs]], axis=0)
                else:
                    kb = kc_ref[blk * (sub - 1):blk * (sub + 1), bs]
                    vv = vc_ref[blk * (sub - 1):blk * (sub + 1), bs]
                s = jnp.where(valid, _nt(_stack_heads(q_ref[rs, bs]), kb), NEG_INF)
                m = jnp.max(s, axis=-1, keepdims=True)
                p = jnp.exp(s - m)
                l = jnp.sum(p, axis=-1, keepdims=True)
                o = _nn((p * (1.0 / l)).astype(BF16), vv)
                o_ref[rs, bs] = jnp.where(lane < ATT_HD, o[:blk], o[blk:])
                lse = m + jnp.log(l)
                lse_all = (lse_all + jnp.where(lane == 2 * hp, lse[:blk], 0.0)
                           + jnp.where(lane == 2 * hp + 1, lse[blk:], 0.0))
            lse_ref[rs, :] = lse_all

    cur = pl.BlockSpec((blk * qb, ATT_W), lambda r, n: (n, r))
    prev = pl.BlockSpec((blk, ATT_W), lambda r, n: (jnp.maximum(qb * n - 1, 0), r))
    return pl.pallas_call(
        body, name=name, grid=(dil, nsteps), in_specs=[cur, prev, cur, prev, cur],
        out_specs=[cur, pl.BlockSpec((blk * qb, 128), lambda r, n: (n, r))],
        out_shape=[jax.ShapeDtypeStruct(qr.shape, F32), jax.ShapeDtypeStruct((lr, dil * 128), F32)],
        compiler_params=_params(("parallel", "parallel")),
    )(qr, kr, kr, vb, vb)


def _att_bwd(qr, kr, vb, do, lse, delta, dil, *, name):
    lr = qr.shape[0]
    blk = ATT_BLOCK
    nsteps = lr // (2 * blk)

    def body(q_ref, kp_ref, kc_ref, vp_ref, vc_ref, do_ref, lse_ref, dl_ref, dq_ref, dk_ref, dv_ref, carry):
        n = pl.program_id(1)

        @pl.when(n == 0)
        def _():
            carry[...] = jnp.zeros_like(carry)

        @pl.when(n == nsteps)
        def _():
            for t, ref in enumerate((dk_ref, dv_ref)):
                ref[:blk, :] = carry[0, t]
                ref[blk:, :] = carry[1, t]

        @pl.when(n < nsteps)
        def _():
            qi = lax.broadcasted_iota(jnp.int32, (blk, 2 * blk), 0)
            ki = lax.broadcasted_iota(jnp.int32, (blk, 2 * blk), 1)
            dist = qi + blk - ki
            band = (dist >= 0) & (dist <= blk)
            lane = lax.broadcasted_iota(jnp.int32, (blk, 128), 1)
            for hp in range(ATT_HEADS // 2):
                bs = slice(128 * hp, 128 * (hp + 1))
                accs = []
                for sub in range(2):
                    rs = slice(blk * sub, blk * (sub + 1))
                    valid = band if sub > 0 else band & ((ki >= blk) | (n > 0))
                    valid = jnp.concatenate([valid, valid], axis=0)
                    if sub == 0:
                        kb = jnp.concatenate([kp_ref[:, bs], kc_ref[0:blk, bs]], axis=0)
                        vv = jnp.concatenate([vp_ref[:, bs], vc_ref[0:blk, bs]], axis=0)
                    else:
                        kb, vv = kc_ref[:, bs], vc_ref[:, bs]
                    q2 = _stack_heads(q_ref[rs, bs])
                    do2 = _stack_heads(do_ref[rs, bs])
                    lse2 = jnp.concatenate([lse_ref[rs, 2 * hp:2 * hp + 1], lse_ref[rs, 2 * hp + 1:2 * hp + 2]], axis=0)
                    dl2 = jnp.concatenate([dl_ref[rs, 2 * hp:2 * hp + 1], dl_ref[rs, 2 * hp + 1:2 * hp + 2]], axis=0)
                    p = jnp.where(valid, jnp.exp(_nt(q2, kb) - lse2), 0.0)
                    ds16 = (p * (_nt(do2, vv) - dl2)).astype(BF16)
                    dq2 = _nn(ds16, kb)
                    dq_ref[rs, bs] = jnp.where(lane < ATT_HD, dq2[:blk], dq2[blk:])
                    accs.append((_tn(ds16, q2), _tn(p.astype(BF16), do2)))
                for t, ref in enumerate((dk_ref, dv_ref)):
                    first, second = accs[0][t], accs[1][t]
                    ref[:blk, bs] = carry[0, t, :, bs]
                    ref[blk:, bs] = carry[1, t, :, bs] + first[:blk]
                    carry[0, t, :, bs] = first[blk:] + second[:blk]
                    carry[1, t, :, bs] = second[blk:]

    def at(n):
        return jnp.minimum(n, nsteps - 1)

    cur = pl.BlockSpec((2 * blk, ATT_W), lambda r, n: (at(n), r))
    prev = pl.BlockSpec((blk, ATT_W), lambda r, n: (jnp.maximum(2 * at(n) - 1, 0), r))
    nar = pl.BlockSpec((2 * blk, 128), lambda r, n: (at(n), r))
    late = pl.BlockSpec((2 * blk, ATT_W), lambda r, n: (jnp.maximum(n - 1, 0), r))
    out = jax.ShapeDtypeStruct(qr.shape, F32)
    return pl.pallas_call(
        body, name=name, grid=(dil, nsteps + 1), in_specs=[cur, prev, cur, prev, cur, cur, nar, nar],
        out_specs=[cur, late, late], out_shape=[out, out, out],
        scratch_shapes=[pltpu.VMEM((2, 2, blk, ATT_W), F32)],
        compiler_params=_params(("parallel", "arbitrary")),
    )(qr, kr, kr, vb, vb, do, lse, delta)


def _att_prep_bwd(dgroups, tabs, du, *, name):
    s_len = du.shape[0]
    tm = VIEW_ROWS
    scale = ATT_HD ** -0.5
    nblk = ATT_W // 128
    ng = len(DILATIONS)

    def body(*refs):
        grads = refs[:3 * ng]
        c_ref, s1_ref, s2_ref, _, du_ref = refs[3 * ng:3 * ng + 5]
        scrs = refs[3 * ng + 5:]
        c, s1, s2 = c_ref[...], s1_ref[...], s2_ref[...]
        for part in range(3):
            for gi, dil in enumerate(DILATIONS):
                if dil > 1:
                    _from_view(grads[3 * gi + part], scrs[gi], dil)
            for b in range(nblk):
                tot = None
                for gi, dil in enumerate(DILATIONS):
                    term = grads[3 * gi + part][:, 128 * b:128 * (b + 1)] if dil == 1 else scrs[gi][b]
                    tot = term if tot is None else tot + term
                if part == 0:
                    tot = _rope_t(tot * scale, c, s1, s2)
                elif part == 1:
                    tot = _rope_t(tot, c, s1, s2)
                du_ref[:, ATT_W * part + 128 * b:ATT_W * part + 128 * (b + 1)] = tot.astype(du_ref.dtype)

    tab = pl.BlockSpec((tm, 128), lambda i: (i, 0))
    return pl.pallas_call(
        body, name=name, grid=(s_len // tm,),
        in_specs=[_view_spec(dil, ATT_W) for dil in DILATIONS for _ in range(3)] + [tab] * 3
        + [pl.BlockSpec(memory_space=pl.ANY)],
        out_specs=pl.BlockSpec((tm, 3 * ATT_W), lambda i: (i, COL_AQ // 3)),
        out_shape=jax.ShapeDtypeStruct(du.shape, du.dtype),
        scratch_shapes=[pltpu.VMEM((nblk, tm, 128), F32) for _ in DILATIONS],
        input_output_aliases={3 * ng + 3: 0},
        compiler_params=_params(("parallel",)),
    )(*[a for g in dgroups for a in g], *tabs, du)


def _head_weights(w, b):
    lane = lax.broadcasted_iota(jnp.int32, (1, 128), 1)
    return jnp.where(lane < ATT_HD, w[:, 2 * b:2 * b + 1], w[:, 2 * b + 1:2 * b + 2])


def _assemble_fwd(pw, u, o_gdn, gnw, o_groups, lse_groups, *, name):
    s_len = u.shape[0]
    tm = VIEW_ROWS
    c = CONV_CH
    nblk = ATT_W // 128
    ng = len(DILATIONS)

    def body(*refs):
        pw_ref, cg_ref, z_ref, ag_ref, og_ref, gnw_ref = refs[:6]
        o_refs, l_refs = refs[6:6 + ng], refs[6 + ng:6 + 2 * ng]
        y_ref, oa_ref = refs[6 + 2 * ng:8 + 2 * ng]
        lse_outs = refs[8 + 2 * ng:8 + 3 * ng]
        o_scr, l_scr = refs[8 + 3 * ng:8 + 4 * ng], refs[8 + 4 * ng:8 + 5 * ng]
        lse_scr = refs[8 + 5 * ng]
        y_ref[:, :c] = (pw_ref[...] * _silu(cg_ref[...])).astype(BF16)
        gw = gnw_ref[...]
        for h in range(GDN_HEADS):
            hs = slice(GDN_D * h, GDN_D * (h + 1))
            oh = og_ref[:, hs]
            yn = oh * lax.rsqrt(jnp.mean(oh * oh, axis=-1, keepdims=True) + 1e-6) * gw
            y_ref[:, c + GDN_D * h:c + GDN_D * (h + 1)] = (yn * _silu(z_ref[:, hs])).astype(BF16)
        for gi, dil in enumerate(DILATIONS):
            if dil > 1:
                _from_view(o_refs[gi], o_scr[gi], dil)
                _from_view(l_refs[gi], l_scr[gi], dil)
        ls = [l_refs[gi][...] if dil == 1 else l_scr[gi][0] for gi, dil in enumerate(DILATIONS)]
        m = functools.reduce(jnp.maximum, ls)
        es = [jnp.exp(l - m) for l in ls]
        den = functools.reduce(lambda a, b: a + b, es)
        lse_scr[0] = m + jnp.log(den)
        ws = [e / den for e in es]
        for b in range(nblk):
            bs = slice(128 * b, 128 * (b + 1))
            o = None
            for gi, dil in enumerate(DILATIONS):
                term = _head_weights(ws[gi], b) * (o_refs[gi][:, bs] if dil == 1 else o_scr[gi][b])
                o = term if o is None else o + term
            oa_ref[:, bs] = o
            y_ref[:, c + GDN_W + 128 * b:c + GDN_W + 128 * (b + 1)] = (o * _silu(ag_ref[:, bs])).astype(BF16)
        for gi, dil in enumerate(DILATIONS):
            _to_view(lse_scr, lse_outs[gi], dil, F32)

    wide = pl.BlockSpec((tm, 768), lambda i: (i, 0))
    return pl.pallas_call(
        body, name=name, grid=(s_len // tm,),
        in_specs=[pl.BlockSpec((tm, c), lambda i: (i, 0)), pl.BlockSpec((tm, c), lambda i: (i, 1024 // c)),
                  pl.BlockSpec((tm, 768), lambda i: (i, COL_GQ + 3)), pl.BlockSpec((tm, 768), lambda i: (i, COL_AQ + 3)),
                  wide, pl.BlockSpec((1, 128), lambda i: (0, 0))]
        + [_view_spec(dil, ATT_W) for dil in DILATIONS] + [_view_spec(dil, 128) for dil in DILATIONS],
        out_specs=[pl.BlockSpec((tm, D_MODEL), lambda i: (i, 0)), wide] + [_view_spec(dil, 128) for dil in DILATIONS],
        out_shape=[jax.ShapeDtypeStruct((s_len, D_MODEL), BF16), jax.ShapeDtypeStruct((s_len, ATT_W), F32)]
        + [_view_shape(s_len, dil, 128, F32) for dil in DILATIONS],
        scratch_shapes=[pltpu.VMEM((nblk, tm, 128), F32) for _ in DILATIONS]
        + [pltpu.VMEM((1, tm, 128), F32) for _ in DILATIONS] + [pltpu.VMEM((1, tm, 128), F32)],
        compiler_params=_params(("parallel",)),
    )(pw, u, u, u, o_gdn, gnw, *o_groups, *lse_groups)


def _assemble_bwd(dy, pw, u, o_gdn, gnw, o_att, *, name):
    s_len = u.shape[0]
    tm = VIEW_ROWS
    c = CONV_CH
    nsteps = s_len // tm
    nblk = ATT_W // 128
    ng = len(DILATIONS)

    def body(dy_ref, pw_ref, cg_ref, z_ref, ag_ref, og_ref, gnw_ref, oa_ref,
             du_ref, dpw_ref, dog_ref, dgw_ref, *rest):
        do_outs, dl_outs = rest[:ng], rest[ng:2 * ng]
        acc_ref, do_scr, dl_scr = rest[2 * ng:]
        i = pl.program_id(0)

        @pl.when(i == 0)
        def _():
            acc_ref[...] = jnp.zeros_like(acc_ref)

        du_ref[...] = jnp.zeros_like(du_ref)
        dyc = dy_ref[:, :c]
        cg = cg_ref[...]
        dpw_ref[...] = dyc * _silu(cg)
        du_ref[:, 1024:1024 + c] = (dyc * pw_ref[...] * _dsilu(cg)).astype(BF16)
        gw = gnw_ref[...]
        dgw = jnp.zeros((8, 128), F32)
        for h in range(GDN_HEADS):
            hs = slice(GDN_D * h, GDN_D * (h + 1))
            oh = og_ref[:, hs]
            zh = z_ref[:, hs]
            dyh = dy_ref[:, c + GDN_D * h:c + GDN_D * (h + 1)]
            r = lax.rsqrt(jnp.mean(oh * oh, axis=-1, keepdims=True) + 1e-6)
            xn = oh * r
            dyn = dyh * _silu(zh)
            du_ref[:, GDN_W * (COL_GQ + 3) + GDN_D * h:GDN_W * (COL_GQ + 3) + GDN_D * (h + 1)] = (
                dyh * xn * gw * _dsilu(zh)).astype(BF16)
            dgw = dgw + (dyn * xn).reshape(tm // 8, 8, 128).sum(axis=0)
            dxn = dyn * gw
            dog_ref[:, hs] = r * (dxn - xn * jnp.mean(dxn * xn, axis=-1, keepdims=True))
        acc_ref[...] += dgw
        lane = lax.broadcasted_iota(jnp.int32, (tm, 128), 1)
        delta = jnp.zeros((tm, 128), F32)
        for b in range(ATT_W // 128):
            bs = slice(128 * b, 128 * (b + 1))
            dya = dy_ref[:, c + GDN_W + 128 * b:c + GDN_W + 128 * (b + 1)]
            ag = ag_ref[:, bs]
            oa = oa_ref[:, bs]
            do = dya * _silu(ag)
            do_scr[b] = do
            du_ref[:, ATT_W * (COL_AQ + 3) + 128 * b:ATT_W * (COL_AQ + 3) + 128 * (b + 1)] = (
                dya * oa * _dsilu(ag)).astype(BF16)
            prod = do * oa
            lo = jnp.sum(jnp.where(lane < ATT_HD, prod, 0.0), axis=-1, keepdims=True)
            hi = jnp.sum(jnp.where(lane >= ATT_HD, prod, 0.0), axis=-1, keepdims=True)
            delta = delta + jnp.where(lane == 2 * b, lo, 0.0) + jnp.where(lane == 2 * b + 1, hi, 0.0)
        dl_scr[0] = delta
        for gi, dil in enumerate(DILATIONS):
            _to_view(do_scr, do_outs[gi], dil, BF16)
            _to_view(dl_scr, dl_outs[gi], dil, F32)

        @pl.when(i == nsteps - 1)
        def _():
            dgw_ref[...] = jnp.sum(acc_ref[...], axis=0, keepdims=True)

    wide = pl.BlockSpec((tm, 768), lambda i: (i, 0))
    vec = pl.BlockSpec((1, 128), lambda i: (0, 0))
    outs = pl.pallas_call(
        body, name=name, grid=(nsteps,),
        in_specs=[pl.BlockSpec((tm, D_MODEL), lambda i: (i, 0)), pl.BlockSpec((tm, c), lambda i: (i, 0)),
                  pl.BlockSpec((tm, c), lambda i: (i, 1024 // c)), pl.BlockSpec((tm, 768), lambda i: (i, COL_GQ + 3)),
                  pl.BlockSpec((tm, 768), lambda i: (i, COL_AQ + 3)), wide, vec, wide],
        out_specs=[pl.BlockSpec((tm, IN_WP), lambda i: (i, 0)), pl.BlockSpec((tm, c), lambda i: (i, 0)), wide, vec]
        + [_view_spec(dil, ATT_W) for dil in DILATIONS] + [_view_spec(dil, 128) for dil in DILATIONS],
        out_shape=[jax.ShapeDtypeStruct((s_len, IN_WP), BF16), jax.ShapeDtypeStruct((s_len, c), F32),
                   jax.ShapeDtypeStruct((s_len, GDN_W), F32), jax.ShapeDtypeStruct((1, 128), F32)]
        + [_view_shape(s_len, dil, ATT_W, BF16) for dil in DILATIONS]
        + [_view_shape(s_len, dil, 128, F32) for dil in DILATIONS],
        scratch_shapes=[pltpu.VMEM((8, 128), F32), pltpu.VMEM((nblk, tm, 128), F32), pltpu.VMEM((1, tm, 128), F32)],
        compiler_params=_params(("arbitrary",)),
    )(dy, pw, u, u, u, o_gdn, gnw, o_att)
    return outs[:4], outs[4:4 + ng], outs[4 + ng:]


def _layer_fwd(x, p, tabs):
    h = _rms_fwd(x, p["norm_w"], name="rms_fwd")
    u = _matmul(h, p["wp"], name="in_proj", tk=2048)
    conv, sw = _conf_fwd(u, p["dw_w"], p["dw_b"], p["ln_w"], p["ln_b"], name="conf_fwd")
    pw = _matmul(sw, p["pw_w"], name="conf_pw")
    qn, kn, vc, bg = _gdn_prep_fwd(u, p["cw"], p["al"], p["dtb"], name="gdn_prep_fwd")
    wk, wv, qd, kd, pm, t, g_cum = _gdn_intra_fwd(qn, kn, vc, bg, name="gdn_intra_fwd")
    o_gdn, vn, sprev = _gdn_scan_fwd(wk, wv, qd, kd, pm, g_cum, name="gdn_scan_fwd")
    qkv = _att_prep_fwd(u, tabs, name="att_prep_fwd")
    groups = [_att_fwd(*qkv[gi], dil, name=f"att_fwd_d{dil}") for gi, dil in enumerate(DILATIONS)]
    outs = _assemble_fwd(pw, u, o_gdn, p["gnw"], [g[0] for g in groups], [g[1] for g in groups],
                         name="assemble_fwd")
    y, o_att, lse = outs[0], outs[1], outs[2:]
    x_next = _matmul(y, p["wout"], add=x, name="out_proj", tk=2048)
    saved = dict(x=x, h=h, u=u, conv=conv, sw=sw, pw=pw, qn=qn, kn=kn, vc=vc, bg=bg, wk=wk, qd=qd, kd=kd, pm=pm,
                 t=t, g_cum=g_cum, vn=vn, sprev=sprev, o_gdn=o_gdn, qkv=qkv, o_att=o_att, lse=lse, y=y)
    return x_next, saved


def _layer_bwd(dx_out, s, p, tabs, layer, big):
    dy = _matmul(dx_out, p["wout"], tb=True, name="out_proj_dy", tk=2048)
    d_wout = _matmul(s["y"], dx_out, ta=True, name="out_proj_dw", tk=2048, stack=(big[1], layer, DEPTH))
    (du, dpw, dog, dgw), do_views, dl_views = _assemble_bwd(dy, s["pw"], s["u"], s["o_gdn"], p["gnw"], s["o_att"],
                                                            name="assemble_bwd")
    dsw = _matmul(dpw, p["pw_w"], tb=True, name="conf_pw_dx")
    d_pw_w = _matmul(s["sw"], dpw, ta=True, name="conf_pw_dw", stack=(big[2], layer, DEPTH))
    dconv, ln_sums = _conf_bwd_ln(dsw, s["conv"], p["ln_w"], p["ln_b"], name="conf_bwd_ln")
    du, d_dw_w = _conf_bwd_conv(s["u"], dconv, p["dw_w"], du, name="conf_bwd_conv")
    dvn, ds_all = _gdn_scan_bwd(dog, s["wk"], s["qd"], s["kd"], s["pm"], s["g_cum"], name="gdn_scan_bwd")
    dqkv, dbg = _gdn_intra_bwd(s["qn"], s["kn"], s["vc"], s["bg"], s["g_cum"], s["t"], dog, dvn, s["vn"],
                               s["sprev"], ds_all, name="gdn_intra_bwd")
    du, d_cw = _gdn_prep_bwd(s["u"], dqkv, p["cw"], du, name="gdn_prep_bwd")
    du, ba_sums = _gdn_ba_bwd(s["u"], dbg, p["al"], p["dtb"], du, name="gdn_ba_bwd")
    dgroups = []
    for gi, dil in enumerate(DILATIONS):
        args = (*s["qkv"][gi], do_views[gi], s["lse"][gi], dl_views[gi], dil)
        dgroups.append(_att_bwd(*args, name=f"att_bwd_d{dil}"))
    du = _att_prep_bwd(dgroups, tabs, du, name="att_prep_bwd")
    dh = _matmul(du, p["wp"], tb=True, name="in_proj_dx", tk=4096)
    d_wp = _matmul(s["h"], du, ta=True, name="in_proj_dw", tk=4096, stack=(big[0], layer, DEPTH))
    dx, d_norm_w = _rms_bwd(s["x"], dh, p["norm_w"], dx_out, name="rms_bwd")
    small = dict(norm_w=d_norm_w, gnw=dgw, ln_sums=ln_sums, dw_w=d_dw_w, cw=d_cw, ba_sums=ba_sums)
    return dx, (d_wp, d_wout, d_pw_w), small


def _trunk(x, target, params, final_norm_w):
    tabs = _rope_tables(x.shape[0])
    layers = [{k: v[l] for k, v in params.items()} for l in range(DEPTH)]
    saved = []
    for p in layers:
        x, s = _layer_fwd(x, p, tabs)
        saved.append(s)
    dx, d_final, loss = _loss_head(x, final_norm_w, target, name="loss_head")
    big = (None, None, None)
    small = [None] * DEPTH
    for l in reversed(range(DEPTH)):
        dx, big, small[l] = _layer_bwd(dx, saved[l], layers[l], tabs, l, big)
    grads = {k: jnp.stack([sm[k] for sm in small]) for k in small[0]}
    grads.update(wp=big[0], wout=big[1], pw_w=big[2])
    return loss[0, 0], dx, grads, d_final


ANY = pl.BlockSpec(memory_space=pl.ANY)


def _position():
    return lax.axis_index("x"), lax.axis_index("y"), lax.axis_index("c")


def _other_chips(x, y):
    return [(1 - x, y), (x, 1 - y), (1 - x, 1 - y)]


def _gather_chips(shards, *, name):
    n = len(shards)
    kinds = 12

    def body(*refs):
        ins, outs = refs[:n], refs[n:2 * n]
        send, recv = refs[2 * n:]
        x, y, c = _position()
        me, sib = (x, y, c), (x, y, 1 - c)
        xn, yn, dg = (1 - x, y), (x, 1 - y), (1 - x, 1 - y)
        pa, pb = 2 * c, 2 * c + 1

        def copy(k, a, chip, layer, to, src=None):
            dst = outs[a].at[2 * chip[0] + chip[1], pl.ds(layer, 1)]
            return pltpu.make_async_remote_copy(
                src_ref=dst if src is None else src, dst_ref=dst, send_sem=send.at[k * n + a],
                recv_sem=recv.at[k * n + a], device_id=to, device_id_type=MESH)

        def own(k, a, layer, chip):
            return copy(k, a, (x, y), layer, (*chip, c), src=ins[a].at[pl.ds(layer, 1)])

        sends = []
        for a in range(n):
            sends += [own(0, a, pa, xn), own(1, a, pb, yn), own(2, a, pb, xn), own(3, a, pa, yn)]
        for cp in sends:
            cp.start()
        arrivals = [(1, yn, pb, (4, xn)), (0, xn, pa, (5, yn)), (2, xn, pb, None), (3, yn, pa, None),
                    (4, dg, pb, None), (5, dg, pa, None)]
        for a in range(n):
            for j, (k, chip, layer, onward) in enumerate(arrivals):
                copy(k, a, chip, layer, me).wait_recv()
                if onward is not None:
                    cp = copy(onward[0], a, chip, layer, (*onward[1], c))
                    cp.start()
                    sends.append(cp)
                cp = copy(6 + j, a, chip, layer, sib)
                cp.start()
                sends.append(cp)
        for a in range(n):
            for j, (k, chip, layer, onward) in enumerate(arrivals):
                copy(6 + j, a, chip, layer + 2 - 4 * c, me).wait_recv()
        for cp in sends:
            cp.wait_send()

    return pl.pallas_call(
        body, name=name, in_specs=[ANY] * n, out_specs=[ANY] * n,
        out_shape=[jax.ShapeDtypeStruct((4,) + s.shape, s.dtype) for s in shards],
        scratch_shapes=[pltpu.SemaphoreType.DMA((kinds * n,)), pltpu.SemaphoreType.DMA((kinds * n,))],
    )(*shards)


def _to_sibling(arrs, *, name):
    n = len(arrs)

    def body(*refs):
        ins, outs = refs[:n], refs[n:2 * n]
        send, recv = refs[2 * n:]
        x, y, c = _position()
        cps = [pltpu.make_async_remote_copy(src_ref=ins[a], dst_ref=outs[a], send_sem=send.at[a],
                                            recv_sem=recv.at[a], device_id=(x, y, 1 - c), device_id_type=MESH)
               for a in range(n)]
        for cp in cps:
            cp.start()
        for cp in cps:
            cp.wait()

    return pl.pallas_call(
        body, name=name, in_specs=[ANY] * n, out_specs=[ANY] * n,
        out_shape=[jax.ShapeDtypeStruct(a.shape, a.dtype) for a in arrs],
        scratch_shapes=[pltpu.SemaphoreType.DMA((n,)), pltpu.SemaphoreType.DMA((n,))],
    )(*arrs)


def _to_chips(arrs, *, name):
    n = len(arrs)

    def body(*refs):
        ins, outs = refs[:n], refs[n:2 * n]
        send, recv = refs[2 * n:]
        x, y, c = _position()
        cps = [pltpu.make_async_remote_copy(
            src_ref=ins[a].at[2 * chip[0] + chip[1]], dst_ref=outs[a].at[j], send_sem=send.at[j * n + a],
            recv_sem=recv.at[j * n + a], device_id=(*chip, c), device_id_type=MESH)
            for j, chip in enumerate(_other_chips(x, y)) for a in range(n)]
        for cp in cps:
            cp.start()
        for cp in cps:
            cp.wait()

    return pl.pallas_call(
        body, name=name, in_specs=[ANY] * n, out_specs=[ANY] * n,
        out_shape=[jax.ShapeDtypeStruct((3,) + a.shape[1:], a.dtype) for a in arrs],
        scratch_shapes=[pltpu.SemaphoreType.DMA((3 * n,)), pltpu.SemaphoreType.DMA((3 * n,))],
    )(*arrs)


def _join_halves(fulls, *, name):
    n = len(fulls)

    def body(*refs):
        ins, outs = refs[:n], refs[n:2 * n]
        send, recv = refs[2 * n:]
        x, y, c = _position()

        def copy(a, rows):
            return pltpu.make_async_remote_copy(
                src_ref=ins[a].at[rows], dst_ref=outs[a].at[rows], send_sem=send.at[a], recv_sem=recv.at[a],
                device_id=(x, y, 1 - c), device_id_type=MESH)

        cps = [copy(a, pl.ds(2 * c, 2)) for a in range(n)]
        for cp in cps:
            cp.start()
        for a in range(n):
            cps[a].wait_send()
            copy(a, pl.ds(2 * (1 - c), 2)).wait_recv()

    return pl.pallas_call(
        body, name=name, in_specs=[ANY] * n, out_specs=[ANY] * n,
        out_shape=[jax.ShapeDtypeStruct(f.shape, f.dtype) for f in fulls],
        scratch_shapes=[pltpu.SemaphoreType.DMA((n,)), pltpu.SemaphoreType.DMA((n,))],
        input_output_aliases={a: a for a in range(n)},
    )(*fulls)


def _allreduce_small(packed, *, name):
    rows = packed.shape[0]
    ndev = 8

    def body(x_ref, sum_ref, all_ref, send, recv, lsem):
        x, y, c = _position()
        me, sib = (x, y, c), (x, y, 1 - c)
        chips = _other_chips(x, y)

        def blk(px, py, pc):
            return all_ref.at[pl.ds((4 * px + 2 * py + pc) * rows, rows), :]

        def copy(k, block, to, src=None):
            return pltpu.make_async_remote_copy(
                src_ref=blk(*block) if src is None else src, dst_ref=blk(*block), send_sem=send.at[k],
                recv_sem=recv.at[k], device_id=to, device_id_type=MESH)

        mine = pltpu.make_async_copy(x_ref, blk(*me), lsem)
        mine.start()
        first = [copy(0, me, sib, src=x_ref)] + [copy(1 + j, me, (*chip, c), src=x_ref) for j, chip in enumerate(chips)]
        for cp in first:
            cp.start()
        passed = [copy(4 + j, (*chip, c), sib) for j, chip in enumerate(chips)]
        for j, chip in enumerate(chips):
            copy(1 + j, (*chip, c), me).wait_recv()
            passed[j].start()
        copy(0, sib, me).wait_recv()
        for j, chip in enumerate(chips):
            copy(4 + j, (*chip, 1 - c), me).wait_recv()
        for cp in first + passed:
            cp.wait_send()
        mine.wait()
        acc = all_ref[0:rows, :]
        for d in range(1, ndev):
            acc = acc + all_ref[d * rows:(d + 1) * rows, :]
        sum_ref[...] = acc

    vm = pl.BlockSpec(memory_space=pltpu.VMEM)
    return pl.pallas_call(
        body, name=name, in_specs=[vm], out_specs=vm, out_shape=jax.ShapeDtypeStruct((rows, 128), F32),
        scratch_shapes=[pltpu.VMEM((ndev * rows, 128), F32), pltpu.SemaphoreType.DMA((7,)),
                        pltpu.SemaphoreType.DMA((7,)), pltpu.SemaphoreType.DMA],
        compiler_params=pltpu.CompilerParams(vmem_limit_bytes=VMEM_LIMIT),
    )(packed)


def _pack(arrs):
    flat = jnp.concatenate([a.reshape(-1) for a in arrs])
    pad = (-flat.shape[0]) % 1024
    return jnp.pad(flat, (0, pad)).reshape(-1, 128)


def _unpack(packed, shapes):
    flat = packed.reshape(-1)
    out, pos = [], 0
    for s in shapes:
        size = math.prod(s)
        out.append(flat[pos:pos + size].reshape(s))
        pos += size
    return out


def _pad_cols(w):
    zeros = jnp.zeros(w.shape[:-1] + (IN_WP - IN_W,), w.dtype)
    return jnp.concatenate([w[..., :ORIG_BA], w[..., ORIG_ATT:], w[..., ORIG_BA:ORIG_ATT], zeros], axis=-1)


def _chip_cols(j):
    per = IN_W // 4
    lo, hi = j * per, (j + 1) * per
    out = []
    for o0, o1, p0 in ((0, ORIG_BA, 0), (ORIG_BA, ORIG_ATT, COL_BA), (ORIG_ATT, IN_W, ORIG_BA)):
        a, b = max(lo, o0), min(hi, o1)
        if a < b:
            out.append((p0 + a - o0, p0 + b - o0))
    return out


def _shards_to_padded(g):
    pieces = []
    for j in range(4):
        loc = 0
        for p0, p1 in _chip_cols(j):
            pieces.append((p0, g[j][..., loc:loc + p1 - p0]))
            loc += p1 - p0
    pieces.sort(key=lambda t: t[0])
    zeros = jnp.zeros(g.shape[1:-1] + (IN_WP - IN_W,), g.dtype)
    return jnp.concatenate([p for _, p in pieces] + [zeros], axis=-1)


def _padded_to_shards(g, dtype):
    return jnp.stack([jnp.concatenate([g[..., p0:p1] for p0, p1 in _chip_cols(j)], axis=-1).astype(dtype)
                      for j in range(4)])


def _unpad_cols(w):
    n_att = IN_W - ORIG_ATT
    return jnp.concatenate([w[..., :ORIG_BA], w[..., COL_BA:COL_BA + ORIG_ATT - ORIG_BA],
                            w[..., ORIG_BA:ORIG_BA + n_att]], axis=-1)


def _lanes(v, first):
    return jnp.pad(v, ((0, 0), (first, 128 - first - v.shape[1])))[:, None, :]


def _by_chip(g, axis):
    shape = g.shape[:axis] + (4, g.shape[axis] // 4) + g.shape[axis + 1:]
    return jnp.moveaxis(g.reshape(shape), axis, 0)


def kernel(x, norm_w, w_in, conv_qkv_w, a_log, dt_bias, gdn_norm_w, conf_dw_w, conf_dw_b, conf_ln_w, conf_ln_b, conf_pw_w, w_out, final_norm_w, loss_target, m_norm_w, m_w_in, m_conv_qkv_w, m_a_log, m_dt_bias, m_gdn_norm_w, m_conf_dw_w, m_conf_dw_b, m_conf_ln_w, m_conf_ln_b, m_conf_pw_w, m_w_out, m_final_norm_w, v_norm_w, v_w_in, v_conv_qkv_w, v_a_log, v_dt_bias, v_gdn_norm_w, v_conf_dw_w, v_conf_dw_b, v_conf_ln_w, v_conf_ln_b, v_conf_pw_w, v_w_out, v_final_norm_w):
    xi, yi, ci = _position()
    chip = 2 * xi + yi

    shards = [w_in.astype(BF16), w_out.astype(BF16), conf_pw_w.astype(BF16), conv_qkv_w, conf_dw_w]
    g_in, g_out, g_pw, g_cw, g_dw = [
        lax.dynamic_update_slice_in_dim(g, s[None], chip, axis=0)
        for g, s in zip(_gather_chips(shards, name="gather_weights"), shards)]
    cw_full = jnp.moveaxis(g_cw, 0, 2).reshape(DEPTH, SHORT_CONV, 3 * GDN_W)
    dw_full = jnp.moveaxis(g_dw, 0, 2).reshape(DEPTH, CONV_WIDTH, CONV_CH)
    params = dict(
        norm_w=norm_w[:, None, :],
        wp=_shards_to_padded(g_in),
        wout=jnp.moveaxis(g_out, 0, 1).reshape(DEPTH, D_MODEL, D_MODEL),
        pw_w=jnp.moveaxis(g_pw, 0, 1).reshape(DEPTH, CONV_CH, CONV_CH),
        cw=jnp.pad(cw_full, ((0, 0), (0, SHALO - SHORT_CONV), (0, 0))),
        dw_w=jnp.pad(dw_full, ((0, 0), (0, HALO - CONV_WIDTH), (0, 0))),
        al=_lanes(a_log, GDN_HEADS), dtb=_lanes(dt_bias, GDN_HEADS), gnw=gdn_norm_w[:, None, :],
        dw_b=conf_dw_b[:, None, :], ln_w=conf_ln_w[:, None, :], ln_b=conf_ln_b[:, None, :],
    )

    loss_part, grad_x, grads, d_final = _trunk(x[0], loss_target[0], params, final_norm_w[None, :])
    loss = lax.psum(loss_part, ("x", "y", "c"))

    def half_by_chip(first, dtype):
        wp, wout, pw = [lax.dynamic_slice_in_dim(grads[k], first, 2, axis=0) for k in ("wp", "wout", "pw_w")]
        return [_padded_to_shards(wp, dtype), _by_chip(wout, 1).astype(dtype), _by_chip(pw, 1).astype(dtype)]

    keep = half_by_chip(2 * ci, F32)
    give = half_by_chip(2 * (1 - ci), BF16)
    got = _to_sibling(give, name="grads_to_sibling")
    pair = [_sum_arrays([k.reshape((8,) + k.shape[2:]), r.reshape((8,) + r.shape[2:])], name=f"pair_sum_{i}",
                        out_dtype=BF16).reshape(k.shape) for i, (k, r) in enumerate(zip(keep, got))]
    arrived = _to_chips(pair, name="grads_to_chips")
    halves = []
    for i, (pr, ar) in enumerate(zip(pair, arrived)):
        own = lax.dynamic_index_in_dim(pr, chip, axis=0, keepdims=False)
        halves.append(_sum_into_half([own, ar[0], ar[1], ar[2]], ci, name=f"chip_sum_{i}"))
    g_w_in, g_w_out, g_pw_w = _join_halves(halves, name="join_halves")

    ba = grads["ba_sums"]
    small = [grads["norm_w"], ba[:, 0:1, :], ba[:, 1:2, :], grads["gnw"], grads["ln_sums"][:, 2:3, :],
             grads["ln_sums"][:, 0:1, :], grads["ln_sums"][:, 1:2, :], d_final,
             grads["cw"][:, :SHORT_CONV, :], grads["dw_w"][:, :CONV_WIDTH, :]]
    red = _unpack(_allreduce_small(_pack(small), name="allreduce_small"), [s.shape for s in small])
    g_norm_w = red[0][:, 0, :]
    g_a_log = red[1][:, 0, GDN_HEADS:2 * GDN_HEADS]
    g_dt_bias = red[2][:, 0, GDN_HEADS:2 * GDN_HEADS]
    g_gnw, g_dw_b, g_ln_w, g_ln_b = red[3][:, 0, :], red[4][:, 0, :], red[5][:, 0, :], red[6][:, 0, :]
    g_final = red[7][0]
    g_cw = lax.dynamic_slice_in_dim(red[8], chip * (3 * GDN_W // 4), 3 * GDN_W // 4, axis=2)
    g_dw_w = lax.dynamic_slice_in_dim(red[9], chip * (CONV_CH // 4), CONV_CH // 4, axis=2)

    def cols_first(a):
        return jnp.transpose(a, (2, 0, 1))

    def cols_last(a):
        return jnp.transpose(a, (1, 2, 0))

    g_t = cols_first(g_w_in)
    g_w_in = cols_last(g_t)
    d_w_in, nm_w_in, nv_w_in = [cols_last(a) for a in _adamw(
        cols_first(w_in), g_t, cols_first(m_w_in), cols_first(v_w_in), name="adamw_w_in", by_lead=True)]
    d_w_out, nm_w_out, nv_w_out = _adamw(w_out, g_w_out, m_w_out, v_w_out, name="adamw_w_out")
    d_pw_w, nm_pw_w, nv_pw_w = _adamw(conf_pw_w, g_pw_w, m_conf_pw_w, v_conf_pw_w, name="adamw_pw")
    sw = [norm_w, a_log, dt_bias, gdn_norm_w, conf_dw_b, conf_ln_w, conf_ln_b, final_norm_w, conv_qkv_w, conf_dw_w]
    sg = [g_norm_w, g_a_log, g_dt_bias, g_gnw, g_dw_b, g_ln_w, g_ln_b, g_final, g_cw, g_dw_w]
    sm = [m_norm_w, m_a_log, m_dt_bias, m_gdn_norm_w, m_conf_dw_b, m_conf_ln_w, m_conf_ln_b, m_final_norm_w,
          m_conv_qkv_w, m_conf_dw_w]
    sv = [v_norm_w, v_a_log, v_dt_bias, v_gdn_norm_w, v_conf_dw_b, v_conf_ln_w, v_conf_ln_b, v_final_norm_w,
          v_conv_qkv_w, v_conf_dw_w]
    shapes = [a.shape for a in sw]
    packed = _adamw(_pack(sw)[None], _pack(sg)[None], _pack(sm)[None], _pack(sv)[None], name="adamw_small")
    sd, snm, snv = [_unpack(pk[0], shapes) for pk in packed]

    def order(big3, small10):
        s = small10
        return [s[0], big3[0], s[8], s[1], s[2], s[3], s[9], s[4], s[5], s[6], big3[2], big3[1], s[7]]

    return (loss, grad_x[None], *order([g_w_in, g_w_out, g_pw_w], sg),
            *order([d_w_in, d_w_out, d_pw_w], sd), *order([nm_w_in, nm_w_out, nm_pw_w], snm),
            *order([nv_w_in, nv_w_out, nv_pw_w], snv))
```

```python
import functools
import math

import jax
import jax.numpy as jnp
from jax import lax
from jax.experimental import pallas as pl
from jax.experimental.pallas import tpu as pltpu

F32, BF16 = jnp.float32, jnp.bfloat16
HIGHEST = lax.Precision.HIGHEST
MESH = pl.DeviceIdType.MESH

D_MODEL = 2048
DEPTH = 4
CONV_CH = 512
GDN_W = 768
GDN_HEADS = 6
GDN_D = 128
ATT_W = 768
ATT_HEADS = 12
ATT_HD = 64
CONV_WIDTH = 31
SHORT_CONV = 4
GDN_CHUNK = 64
ROPE_THETA = 500000.0
ROPE_DIM = ATT_HD // 4
DIL_PATTERNS = ((128, 1), (512, 4), (2048, 16))
ATT_BLOCK = 128
NEG_INF = -1e30
IN_W = 7692

IN_WP = 8192
COL_BA = 7680
ORIG_BA = 4608
ORIG_ATT = 4620

ADAM_LR = 0.001
ADAM_B1 = 0.9
ADAM_B2 = 0.999
ADAM_EPS = 1e-08
ADAM_WD = 0.01
ADAM_STEP = 10

VMEM_LIMIT = 56 * 1024 * 1024


def _params(sem=None):
    return pltpu.CompilerParams(dimension_semantics=sem, vmem_limit_bytes=VMEM_LIMIT)


def _sigmoid(x):
    return 0.5 * jnp.tanh(0.5 * x) + 0.5


def _silu(x):
    return x * _sigmoid(x)


def _dsilu(x):
    s = _sigmoid(x)
    return s * (1.0 + x * (1.0 - s))


def _dot(a, b, dims, precision=None):
    return lax.dot_general(a, b, (dims, ((), ())), precision=precision, preferred_element_type=F32)


def _nn(a, b, precision=None):
    return _dot(a, b, ((1,), (0,)), precision)


def _nt(a, b, precision=None):
    return _dot(a, b, ((1,), (1,)), precision)


def _tn(a, b, precision=None):
    return _dot(a, b, ((0,), (0,)), precision)


def _matmul(a, b, *, name, ta=False, tb=False, out_dtype=F32, add=None, stack=None, tm=1024, tn=1024, tk=1024):
    if ta:
        k_dim, m_dim = a.shape
    else:
        m_dim, k_dim = a.shape
    n_dim = b.shape[0] if tb else b.shape[1]
    tm, tn, tk = min(tm, m_dim), min(tn, n_dim), min(tk, k_dim)
    assert m_dim % tm == 0 and n_dim % tn == 0 and k_dim % tk == 0, (name, a.shape, b.shape)
    nk = k_dim // tk
    a_spec = pl.BlockSpec((tk, tm), lambda i, j, k: (k, i)) if ta else pl.BlockSpec((tm, tk), lambda i, j, k: (i, k))
    b_spec = pl.BlockSpec((tn, tk), lambda i, j, k: (j, k)) if tb else pl.BlockSpec((tk, tn), lambda i, j, k: (k, j))
    o_spec = pl.BlockSpec((tm, tn), lambda i, j, k: (i, j))
    out_shape = jax.ShapeDtypeStruct((m_dim, n_dim), out_dtype)
    dims = ((0 if ta else 1,), (1 if tb else 0,))
    has_add = add is not None
    ins = [a, b] + ([add] if has_add else [])
    specs = [a_spec, b_spec] + ([o_spec] if has_add else [])
    aliases = {}
    if stack is not None:
        buf, slab, nslabs = stack
        o_spec = pl.BlockSpec((None, tm, tn), lambda i, j, k: (slab, i, j))
        out_shape = jax.ShapeDtypeStruct((nslabs, m_dim, n_dim), out_dtype)
        if buf is not None:
            aliases = {len(ins): 0}
            ins.append(buf)
            specs.append(pl.BlockSpec(memory_space=pl.ANY))
    n_in = len(ins)

    def body(*refs):
        a_ref, b_ref = refs[0], refs[1]
        o_ref = refs[n_in]

        def finish(r):
            if has_add:
                r = r + refs[2][...]
            o_ref[...] = r.astype(out_dtype)

        prod = _dot(a_ref[...].astype(BF16), b_ref[...].astype(BF16), dims)
        if nk == 1:
            finish(prod)
            return
        acc_ref = refs[n_in + 1]
        k = pl.program_id(2)

        @pl.when(k == 0)
        def _():
            acc_ref[...] = prod

        @pl.when(k > 0)
        def _():
            acc_ref[...] += prod

        @pl.when(k == nk - 1)
        def _():
            finish(acc_ref[...])

    return pl.pallas_call(
        body, name=name, grid=(m_dim // tm, n_dim // tn, nk), in_specs=specs, out_specs=o_spec,
        out_shape=out_shape, scratch_shapes=[pltpu.VMEM((tm, tn), F32)] if nk > 1 else [],
        input_output_aliases=aliases,
        compiler_params=_params(("parallel", "parallel", "arbitrary")),
    )(*ins)


def _rms_fwd(x, w, *, name, tm=256):
    s_len, d = x.shape

    def body(x_ref, w_ref, h_ref):
        xv = x_ref[...]
        r = lax.rsqrt(jnp.mean(xv * xv, axis=-1, keepdims=True) + 1e-6)
        h_ref[...] = (xv * r * w_ref[...]).astype(BF16)

    return pl.pallas_call(
        body, name=name, grid=(s_len // tm,),
        in_specs=[pl.BlockSpec((tm, d), lambda i: (i, 0)), pl.BlockSpec((1, d), lambda i: (0, 0))],
        out_specs=pl.BlockSpec((tm, d), lambda i: (i, 0)),
        out_shape=jax.ShapeDtypeStruct((s_len, d), BF16),
        compiler_params=_params(("parallel",)),
    )(x, w)


def _rms_bwd(x, dh, w, dres, *, name, tm=256):
    s_len, d = x.shape
    nsteps = s_len // tm

    def body(x_ref, dh_ref, w_ref, dres_ref, dx_ref, dw_ref, acc_ref):
        i = pl.program_id(0)

        @pl.when(i == 0)
        def _():
            acc_ref[...] = jnp.zeros_like(acc_ref)

        xv = x_ref[...]
        r = lax.rsqrt(jnp.mean(xv * xv, axis=-1, keepdims=True) + 1e-6)
        xn = xv * r
        dy = dh_ref[...]
        dxn = dy * w_ref[...]
        dx_ref[...] = dres_ref[...] + r * (dxn - xn * jnp.mean(dxn * xn, axis=-1, keepdims=True))
        acc_ref[...] += (dy * xn).reshape(tm // 8, 8, d).sum(axis=0)

        @pl.when(i == nsteps - 1)
        def _():
            dw_ref[...] = jnp.sum(acc_ref[...], axis=0, keepdims=True)

    row = pl.BlockSpec((tm, d), lambda i: (i, 0))
    vec = pl.BlockSpec((1, d), lambda i: (0, 0))
    return pl.pallas_call(
        body, name=name, grid=(nsteps,), in_specs=[row, row, vec, row], out_specs=[row, vec],
        out_shape=[jax.ShapeDtypeStruct((s_len, d), F32), jax.ShapeDtypeStruct((1, d), F32)],
        scratch_shapes=[pltpu.VMEM((8, d), F32)],
        compiler_params=_params(("arbitrary",)),
    )(x, dh, w, dres)


def _loss_head(x, w, target, *, name, tm=256):
    s_len, d = x.shape
    nsteps = s_len // tm

    def body(x_ref, w_ref, t_ref, dx_ref, dw_ref, loss_ref, acc_ref, lacc_ref):
        i = pl.program_id(0)

        @pl.when(i == 0)
        def _():
            acc_ref[...] = jnp.zeros_like(acc_ref)
            lacc_ref[...] = jnp.zeros_like(lacc_ref)

        xv = x_ref[...]
        wv = w_ref[...]
        r = lax.rsqrt(jnp.mean(xv * xv, axis=-1, keepdims=True) + 1e-6)
        xn = xv * r
        err = xn * wv - t_ref[...]
        lacc_ref[...] += (err * err).reshape(tm // 8, 8, d).sum(axis=0)
        dy = err * (1.0 / d)
        dxn = dy * wv
        dx_ref[...] = r * (dxn - xn * jnp.mean(dxn * xn, axis=-1, keepdims=True))
        acc_ref[...] += (dy * xn).reshape(tm // 8, 8, d).sum(axis=0)

        @pl.when(i == nsteps - 1)
        def _():
            dw_ref[...] = jnp.sum(acc_ref[...], axis=0, keepdims=True)
            tot = jnp.sum(jnp.sum(lacc_ref[...], axis=0, keepdims=True), axis=1, keepdims=True)
            loss_ref[...] = jnp.broadcast_to(tot * (0.5 / d), (1, 128))

    row = pl.BlockSpec((tm, d), lambda i: (i, 0))
    vec = pl.BlockSpec((1, d), lambda i: (0, 0))
    return pl.pallas_call(
        body, name=name, grid=(nsteps,), in_specs=[row, vec, row],
        out_specs=[row, vec, pl.BlockSpec((1, 128), lambda i: (0, 0))],
        out_shape=[jax.ShapeDtypeStruct((s_len, d), F32), jax.ShapeDtypeStruct((1, d), F32),
                   jax.ShapeDtypeStruct((1, 128), F32)],
        scratch_shapes=[pltpu.VMEM((8, d), F32), pltpu.VMEM((8, d), F32)],
        compiler_params=_params(("arbitrary",)),
    )(x, w, target)


def _rows_block(shape, tr=256):
    lead, rows, cols = shape
    if rows % tr != 0:
        assert rows * cols <= 1 << 20, shape
        tr = rows
    return (lead, rows // tr), pl.BlockSpec((1, tr, cols), lambda a, i: (a, i, 0))


LEAD_BLOCK = 64


def _adamw(w, g, m, v, *, name, by_lead=False):
    if by_lead:
        lead, rows, cols = w.shape
        grid = (pl.cdiv(lead, LEAD_BLOCK), 1)
        spec = pl.BlockSpec((LEAD_BLOCK, rows, cols), lambda a, i: (a, 0, 0))
    else:
        grid, spec = _rows_block(w.shape)
    c1 = 1.0 / (1.0 - ADAM_B1 ** ADAM_STEP)
    c2 = 1.0 / (1.0 - ADAM_B2 ** ADAM_STEP)

    def body(w_ref, g_ref, m_ref, v_ref, d_ref, nm_ref, nv_ref):
        gv = g_ref[...]
        nm = ADAM_B1 * m_ref[...] + (1.0 - ADAM_B1) * gv
        nv = ADAM_B2 * v_ref[...] + (1.0 - ADAM_B2) * (gv * gv)
        nm_ref[...] = nm
        nv_ref[...] = nv
        d_ref[...] = -ADAM_LR * ((nm * c1) / (jnp.sqrt(nv * c2) + ADAM_EPS) + ADAM_WD * w_ref[...])

    out = jax.ShapeDtypeStruct(w.shape, F32)
    return pl.pallas_call(
        body, name=name, grid=grid, in_specs=[spec] * 4, out_specs=[spec] * 3, out_shape=[out] * 3,
        compiler_params=_params(("parallel", "parallel")),
    )(w, g, m, v)


def _sum_into_half(arrs, half, *, name):
    lead, rows, cols = arrs[0].shape
    assert lead == 2
    (_, nr), spec0 = _rows_block(arrs[0].shape)
    tr = spec0.block_shape[1]
    n = len(arrs)

    def body(half_ref, *refs):
        del half_ref
        acc = refs[0][...].astype(F32)
        for r in refs[1:n]:
            acc = acc + r[...].astype(F32)
        refs[n][...] = acc

    spec = pl.BlockSpec((1, tr, cols), lambda a, i, h: (a, i, 0))
    return pl.pallas_call(
        body, name=name,
        grid_spec=pltpu.PrefetchScalarGridSpec(
            num_scalar_prefetch=1, grid=(2, nr), in_specs=[spec] * n,
            out_specs=pl.BlockSpec((1, tr, cols), lambda a, i, h: (2 * h[0] + a, i, 0))),
        out_shape=jax.ShapeDtypeStruct((4, rows, cols), F32),
        compiler_params=_params(("parallel", "parallel")),
    )(jnp.reshape(half, (1,)).astype(jnp.int32), *arrs)


def _sum_arrays(arrs, *, name, out_dtype):
    grid, spec = _rows_block(arrs[0].shape)
    n = len(arrs)

    def body(*refs):
        acc = refs[0][...].astype(F32)
        for r in refs[1:n]:
            acc = acc + r[...].astype(F32)
        refs[n][...] = acc.astype(out_dtype)

    return pl.pallas_call(
        body, name=name, grid=grid, in_specs=[spec] * n, out_specs=spec,
        out_shape=jax.ShapeDtypeStruct(arrs[0].shape, out_dtype),
        compiler_params=_params(("parallel", "parallel")),
    )(*arrs)


HALO = 32


def _shifted_windows(buf, tm, offsets):
    rows = buf.shape[0]
    for b in range(8):
        group = [(k, s) for k, s in enumerate(offsets) if s % 8 == b]
        if not group:
            continue
        rb = buf if b == 0 else pltpu.roll(buf, rows - b, 0)
        for k, s in group:
            yield k, rb[s - b:s - b + tm, :]


def _conf_fwd(u, dw_w, dw_b, ln_w, ln_b, *, name, tm=256):
    s_len = u.shape[0]
    c = CONV_CH

    def body(uc_ref, up_ref, dww_ref, dwb_ref, lnw_ref, lnb_ref, conv_ref, sw_ref, hbuf):
        i = pl.program_id(0)
        hbuf[HALO:, :] = uc_ref[:, :c] * _sigmoid(uc_ref[:, c:])
        hp = up_ref[:, :c] * _sigmoid(up_ref[:, c:])
        hbuf[:HALO, :] = jnp.where(i > 0, hp, 0.0)
        for cb in range(c // 128):
            cs = slice(128 * cb, 128 * (cb + 1))
            acc = jnp.zeros((tm, 128), F32)
            taps = [HALO - CONV_WIDTH + 1 + j for j in range(CONV_WIDTH)]
            for j, win in _shifted_windows(hbuf[:, cs], tm, taps):
                acc = acc + win * dww_ref[j:j + 1, cs]
            conv_ref[:, cs] = acc + dwb_ref[:, cs]
        cv = conv_ref[...]
        mu = jnp.mean(cv, axis=-1, keepdims=True)
        xc = cv - mu
        var = jnp.mean(xc * xc, axis=-1, keepdims=True)
        ln = xc * lax.rsqrt(var + 1e-5) * lnw_ref[...] + lnb_ref[...]
        sw_ref[...] = _silu(ln).astype(BF16)

    vec = pl.BlockSpec((1, c), lambda i: (0, 0))
    return pl.pallas_call(
        body, name=name, grid=(s_len // tm,),
        in_specs=[pl.BlockSpec((tm, 2 * c), lambda i: (i, 0)),
                  pl.BlockSpec((HALO, 2 * c), lambda i: (jnp.maximum(i * (tm // HALO) - 1, 0), 0)),
                  pl.BlockSpec((HALO, c), lambda i: (0, 0)), vec, vec, vec],
        out_specs=[pl.BlockSpec((tm, c), lambda i: (i, 0))] * 2,
        out_shape=[jax.ShapeDtypeStruct((s_len, c), F32), jax.ShapeDtypeStruct((s_len, c), BF16)],
        scratch_shapes=[pltpu.VMEM((tm + HALO, c), F32)],
        compiler_params=_params(("parallel",)),
    )(u, u, dw_w, dw_b, ln_w, ln_b)


def _conf_bwd_ln(d_sw, conv, ln_w, ln_b, *, name, tm=256):
    s_len, c = conv.shape
    nsteps = s_len // tm

    def body(dsw_ref, conv_ref, lnw_ref, lnb_ref, dconv_ref, sums_ref):
        i = pl.program_id(0)

        @pl.when(i == 0)
        def _():
            sums_ref[...] = jnp.zeros_like(sums_ref)

        cv = conv_ref[...]
        mu = jnp.mean(cv, axis=-1, keepdims=True)
        xc = cv - mu
        rs = lax.rsqrt(jnp.mean(xc * xc, axis=-1, keepdims=True) + 1e-5)
        xhat = xc * rs
        lnw = lnw_ref[...]
        ln = xhat * lnw + lnb_ref[...]
        dln = dsw_ref[...] * _dsilu(ln)
        dxh = dln * lnw
        dconv = rs * (dxh - jnp.mean(dxh, axis=-1, keepdims=True)
                      - xhat * jnp.mean(dxh * xhat, axis=-1, keepdims=True))
        dconv_ref[...] = dconv
        sums_ref[0:1, :] += jnp.sum(dln * xhat, axis=0, keepdims=True)
        sums_ref[1:2, :] += jnp.sum(dln, axis=0, keepdims=True)
        sums_ref[2:3, :] += jnp.sum(dconv, axis=0, keepdims=True)

    row = pl.BlockSpec((tm, c), lambda i: (i, 0))
    vec = pl.BlockSpec((1, c), lambda i: (0, 0))
    return pl.pallas_call(
        body, name=name, grid=(nsteps,), in_specs=[row, row, vec, vec],
        out_specs=[row, pl.BlockSpec((8, c), lambda i: (0, 0))],
        out_shape=[jax.ShapeDtypeStruct((s_len, c), F32), jax.ShapeDtypeStruct((8, c), F32)],
        compiler_params=_params(("arbitrary",)),
    )(d_sw, conv, ln_w, ln_b)


def _conf_bwd_conv(u, dconv, dw_w, du, *, name, tm=256):
    s_len = u.shape[0]
    c = CONV_CH
    nsteps = s_len // tm
    off = HALO - CONV_WIDTH + 1

    def body(uc_ref, up_ref, dc_ref, dn_ref, dww_ref, du_in_ref, du_ref, ddw_ref, hbuf, dbuf, wacc):
        del du_in_ref
        i = pl.program_id(0)

        @pl.when(i == 0)
        def _():
            wacc[...] = jnp.zeros_like(wacc)

        hbuf[HALO:, :] = uc_ref[:, :c] * _sigmoid(uc_ref[:, c:])
        hp = up_ref[:, :c] * _sigmoid(up_ref[:, c:])
        hbuf[:HALO, :] = jnp.where(i > 0, hp, 0.0)
        dbuf[:tm, :] = dc_ref[...]
        dbuf[tm:, :] = jnp.where(i < nsteps - 1, dn_ref[...], 0.0)
        for cb in range(c // 128):
            cs = slice(128 * cb, 128 * (cb + 1))
            dcur = dbuf[0:tm, cs]
            acc = jnp.zeros((tm, 128), F32)
            for k, win in _shifted_windows(dbuf[:, cs], tm, list(range(CONV_WIDTH))):
                j = CONV_WIDTH - 1 - k
                acc = acc + win * dww_ref[j:j + 1, cs]
            for j, win in _shifted_windows(hbuf[:, cs], tm, [off + j for j in range(CONV_WIDTH)]):
                wacc[j, :, cs] += (win * dcur).reshape(tm // 8, 8, 128).sum(axis=0)
            a = uc_ref[:, cs]
            sg = _sigmoid(uc_ref[:, c + 128 * cb:c + 128 * (cb + 1)])
            du_ref[:, cs] = (acc * sg).astype(du_ref.dtype)
            du_ref[:, c + 128 * cb:c + 128 * (cb + 1)] = (acc * a * sg * (1.0 - sg)).astype(du_ref.dtype)

        @pl.when(i == nsteps - 1)
        def _():
            for j in range(CONV_WIDTH):
                ddw_ref[j:j + 1, :] = jnp.sum(wacc[j], axis=0, keepdims=True)
            ddw_ref[CONV_WIDTH:, :] = jnp.zeros((HALO - CONV_WIDTH, c), F32)

    return pl.pallas_call(
        body, name=name, grid=(nsteps,),
        in_specs=[pl.BlockSpec((tm, 2 * c), lambda i: (i, 0)),
                  pl.BlockSpec((HALO, 2 * c), lambda i: (jnp.maximum(i * (tm // HALO) - 1, 0), 0)),
                  pl.BlockSpec((tm, c), lambda i: (i, 0)),
                  pl.BlockSpec((HALO, c), lambda i: (jnp.minimum((i + 1) * (tm // HALO), s_len // HALO - 1), 0)),
                  pl.BlockSpec((HALO, c), lambda i: (0, 0)),
                  pl.BlockSpec(memory_space=pl.ANY)],
        out_specs=[pl.BlockSpec((tm, 2 * c), lambda i: (i, 0)), pl.BlockSpec((HALO, c), lambda i: (0, 0))],
        out_shape=[jax.ShapeDtypeStruct(du.shape, du.dtype), jax.ShapeDtypeStruct((HALO, c), F32)],
        scratch_shapes=[pltpu.VMEM((tm + HALO, c), F32), pltpu.VMEM((tm + HALO, c), F32),
                        pltpu.VMEM((CONV_WIDTH, 8, c), F32)],
        input_output_aliases={5: 0},
        compiler_params=_params(("arbitrary",)),
    )(u, u, dconv, dconv, dw_w, du)


COL_GQ = 1536 // GDN_W
COL_AQ = 4608 // ATT_W
SHALO = 8
SCAN_CHUNKS = 4
INTRA_CHUNKS = 4


def _softplus(z):
    return jnp.maximum(z, 0.0) + jnp.log1p(jnp.exp(-jnp.abs(z)))


def _short_conv(buf, cw_ref, part, rows, first):
    acc = jnp.zeros((rows, GDN_W), F32)
    for j, win in _shifted_windows(buf[...], rows, [first + j for j in range(SHORT_CONV)]):
        acc = acc + win * cw_ref[j:j + 1, GDN_W * part:GDN_W * (part + 1)]
    return acc


def _gdn_prep_fwd(u, cw, al, dtb, *, name, tm=256):
    s_len = u.shape[0]
    first = SHALO - SHORT_CONV + 1

    def body(uq, uk, uv, pq, pk, pv, uba, cw_ref, al_ref, dtb_ref, qn_ref, kn_ref, vc_ref, bg_ref, buf):
        i = pl.program_id(0)

        def conv(cur, prev, part):
            buf[SHALO:, :] = cur[...]
            buf[:SHALO, :] = jnp.where(i > 0, prev[...], 0.0)
            return _silu(_short_conv(buf, cw_ref, part, tm, first))

        for part, (cur, prev, out, scale) in enumerate(
                ((uq, pq, qn_ref, GDN_D ** -0.5), (uk, pk, kn_ref, 1.0))):
            y = conv(cur, prev, part)
            for h in range(GDN_HEADS):
                hs = slice(GDN_D * h, GDN_D * (h + 1))
                yh = y[:, hs]
                out[:, hs] = yh * (lax.rsqrt(jnp.sum(yh * yh, axis=-1, keepdims=True) + 1e-6) * scale)
        vc_ref[...] = conv(uv, pv, 2)
        ba = uba[...]
        lane = lax.broadcasted_iota(jnp.int32, ba.shape, 1)
        g = -jnp.exp(al_ref[...]) * _softplus(ba + dtb_ref[...])
        bg_ref[...] = jnp.where(lane < GDN_HEADS, _sigmoid(ba), jnp.where(lane < 2 * GDN_HEADS, g, 0.0))

    def cur(col):
        return pl.BlockSpec((tm, GDN_W), lambda i: (i, col))

    def prev(col):
        return pl.BlockSpec((SHALO, GDN_W), lambda i: (jnp.maximum(i * (tm // SHALO) - 1, 0), col))

    vec = pl.BlockSpec((1, 128), lambda i: (0, 0))
    row = pl.BlockSpec((tm, GDN_W), lambda i: (i, 0))
    wide = jax.ShapeDtypeStruct((s_len, GDN_W), F32)
    return pl.pallas_call(
        body, name=name, grid=(s_len // tm,),
        in_specs=[cur(COL_GQ), cur(COL_GQ + 1), cur(COL_GQ + 2), prev(COL_GQ), prev(COL_GQ + 1), prev(COL_GQ + 2),
                  pl.BlockSpec((tm, 128), lambda i: (i, COL_BA // 128)),
                  pl.BlockSpec((SHALO, 3 * GDN_W), lambda i: (0, 0)), vec, vec],
        out_specs=[row, row, row, pl.BlockSpec((tm, 128), lambda i: (i, 0))],
        out_shape=[wide, wide, wide, jax.ShapeDtypeStruct((s_len, 128), F32)],
        scratch_shapes=[pltpu.VMEM((tm + SHALO, GDN_W), F32)],
        compiler_params=_params(("parallel",)),
    )(u, u, u, u, u, u, u, cw, al, dtb)


def _chunk_masks():
    c = GDN_CHUNK
    row = lax.broadcasted_iota(jnp.int32, (c, c), 0)
    col = lax.broadcasted_iota(jnp.int32, (c, c), 1)
    return row >= col, row > col


def _cum_decay(bg):
    c = GDN_CHUNK
    causal, _ = _chunk_masks()
    g_cum = _nn(causal.astype(F32), bg, HIGHEST)
    sel = (lax.broadcasted_iota(jnp.int32, (8, 128), 0) + GDN_HEADS
           == lax.broadcasted_iota(jnp.int32, (8, 128), 1)).astype(F32)
    return g_cum, _nt(sel, g_cum, HIGHEST)


def _bdot(a, b, ca, cb):
    return lax.dot_general(a, b, (((ca,), (cb,)), ((0,), (0,))), preferred_element_type=F32)


def _bnn(a, b):
    return _bdot(a, b, 2, 1)


def _bnt(a, b):
    return _bdot(a, b, 2, 2)


def _btn(a, b):
    return _bdot(a, b, 1, 1)


def _split(a):
    hi = a.astype(BF16)
    return hi, (a - hi.astype(F32)).astype(BF16)


def _bnn3(a, b):
    ah, al = _split(a)
    bh, bl = _split(b)
    return _bnn(ah, bh) + (_bnn(al, bh) + _bnn(ah, bl))


def _heads(ref, rows=slice(None)):
    return jnp.stack([ref[rows, GDN_D * h:GDN_D * (h + 1)] for h in range(GDN_HEADS)])


def _head_columns(a, first):
    return jnp.stack([a[:, first + h:first + h + 1] for h in range(GDN_HEADS)])


def _chunk_decay(g_cum, g_rows, bg):
    causal, _ = _chunk_masks()
    gc = _head_columns(g_cum, GDN_HEADS)
    gr = jnp.stack([g_rows[h:h + 1, :] for h in range(GDN_HEADS)])
    dec = jnp.where(causal, jnp.exp(jnp.where(causal, gc - gr, 0.0)), 0.0)
    return gc, _head_columns(bg, 0), dec


def _gdn_intra_fwd(qn, kn, vc, bg, *, name):
    s_len = qn.shape[0]
    c = GDN_CHUNK
    nch = INTRA_CHUNKS
    nsteps = s_len // (c * nch)

    def body(q_ref, k_ref, v_ref, bg_ref, wk_ref, wv_ref, qd_ref, kd_ref, p_ref, t_ref, g_ref):
        causal, strict = _chunk_masks()
        eye = (lax.broadcasted_iota(jnp.int32, (c, c), 0) == lax.broadcasted_iota(jnp.int32, (c, c), 1)).astype(F32)
        parts = []
        for ch in range(nch):
            rs = slice(c * ch, c * (ch + 1))
            bg = bg_ref[rs, :]
            g_cum, g_rows = _cum_decay(bg)
            g_ref[rs, :] = g_cum
            parts.append(_chunk_decay(g_cum, g_rows, bg) + (_heads(q_ref, rs), _heads(k_ref, rs), _heads(v_ref, rs)))
        gc, bc, dec, q, k, v = [jnp.concatenate([p[i] for p in parts], axis=0) for i in range(6)]
        k16 = k.astype(BF16)
        low = jnp.where(strict, bc * _bnt(k16, k16) * dec, 0.0)
        pw = -low
        t = eye + pw
        for _ in range(5):
            pw = _bnn3(pw, pw)
            t = t + _bnn3(t, pw)
        t16 = t.astype(BF16)
        eg = jnp.exp(gc)
        wk = _bnn(t16, (k * (bc * eg)).astype(BF16))
        wv = _bnn(t16, (v * bc).astype(BF16))
        pm = jnp.where(causal, _bnt(q.astype(BF16), k16) * dec, 0.0).astype(BF16)
        qd = q * eg
        kd = k * jnp.exp(gc[:, c - 1:c, :] - gc)
        for idx in range(nch * GDN_HEADS):
            ch, h = divmod(idx, GDN_HEADS)
            rs = slice(c * ch, c * (ch + 1))
            hs = slice(GDN_D * h, GDN_D * (h + 1))
            t_ref[h, rs, :] = t[idx]
            p_ref[h, rs, :] = pm[idx]
            wk_ref[rs, hs] = wk[idx].astype(BF16)
            wv_ref[rs, hs] = wv[idx]
            qd_ref[rs, hs] = qd[idx].astype(BF16)
            kd_ref[rs, hs] = kd[idx].astype(BF16)

    row = pl.BlockSpec((c * nch, GDN_W), lambda n: (n, 0))
    sq = pl.BlockSpec((GDN_HEADS, c * nch, c), lambda n: (0, n, 0))
    narrow = pl.BlockSpec((c * nch, 128), lambda n: (n, 0))
    w16 = jax.ShapeDtypeStruct((s_len, GDN_W), BF16)
    return pl.pallas_call(
        body, name=name, grid=(nsteps,), in_specs=[row, row, row, narrow],
        out_specs=[row, row, row, row, sq, sq, narrow],
        out_shape=[w16, jax.ShapeDtypeStruct((s_len, GDN_W), F32), w16, w16,
                   jax.ShapeDtypeStruct((GDN_HEADS, s_len, c), BF16),
                   jax.ShapeDtypeStruct((GDN_HEADS, s_len, c), F32),
                   jax.ShapeDtypeStruct((s_len, 128), F32)],
        compiler_params=_params(("parallel",)),
    )(qn, kn, vc, bg)


def _gdn_scan_fwd(wk, wv, qd, kd, p, g_cum, *, name):
    s_len = wk.shape[0]
    c = GDN_CHUNK
    nchunks = s_len // c
    nch = SCAN_CHUNKS

    def body(wk_ref, wv_ref, qd_ref, kd_ref, p_ref, g_ref, o_ref, vn_ref, sp_ref, st):
        @pl.when(pl.program_id(0) == 0)
        def _():
            st[...] = jnp.zeros_like(st)

        s = st[...]
        for ch in range(nch):
            rs = slice(c * ch, c * (ch + 1))
            sp_ref[ch] = s
            s16 = s.astype(BF16)
            vn16 = (_heads(wv_ref, rs) - _bnn(_heads(wk_ref, rs), s16)).astype(BF16)
            o = _bnn(_heads(qd_ref, rs), s16) + _bnn(p_ref[:, rs, :], vn16)
            gl = jnp.exp(_head_columns(g_ref[c * ch + c - 1:c * ch + c, :], GDN_HEADS))
            s = s * gl + _btn(_heads(kd_ref, rs), vn16)
            for h in range(GDN_HEADS):
                hs = slice(GDN_D * h, GDN_D * (h + 1))
                vn_ref[rs, hs] = vn16[h]
                o_ref[rs, hs] = o[h]
        st[...] = s

    row = pl.BlockSpec((c * nch, GDN_W), lambda n: (n, 0))
    return pl.pallas_call(
        body, name=name, grid=(nchunks // nch,),
        in_specs=[row, row, row, row, pl.BlockSpec((GDN_HEADS, c * nch, c), lambda n: (0, n, 0)),
                  pl.BlockSpec((c * nch, 128), lambda n: (n, 0))],
        out_specs=[row, row, pl.BlockSpec((nch, GDN_HEADS, GDN_D, GDN_D), lambda n: (n, 0, 0, 0))],
        out_shape=[jax.ShapeDtypeStruct((s_len, GDN_W), F32), jax.ShapeDtypeStruct((s_len, GDN_W), BF16),
                   jax.ShapeDtypeStruct((nchunks, GDN_HEADS, GDN_D, GDN_D), F32)],
        scratch_shapes=[pltpu.VMEM((GDN_HEADS, GDN_D, GDN_D), F32)],
        compiler_params=_params(("arbitrary",)),
    )(wk, wv, qd, kd, p, g_cum)


def _gdn_scan_bwd(do, wk, qd, kd, p, g_cum, *, name):
    s_len = wk.shape[0]
    c = GDN_CHUNK
    nchunks = s_len // c
    nch = SCAN_CHUNKS

    def body(do_ref, wk_ref, qd_ref, kd_ref, p_ref, g_ref, dvn_ref, ds_ref, dst):
        @pl.when(pl.program_id(0) == 0)
        def _():
            dst[...] = jnp.zeros_like(dst)

        ds = dst[...]
        for ch in reversed(range(nch)):
            rs = slice(c * ch, c * (ch + 1))
            ds_ref[ch] = ds
            do16 = _heads(do_ref, rs).astype(BF16)
            dvn16 = (_btn(p_ref[:, rs, :], do16) + _bnn(_heads(kd_ref, rs), ds.astype(BF16))).astype(BF16)
            gl = jnp.exp(_head_columns(g_ref[c * ch + c - 1:c * ch + c, :], GDN_HEADS))
            ds = _btn(_heads(qd_ref, rs), do16) + ds * gl - _btn(_heads(wk_ref, rs), dvn16)
            for h in range(GDN_HEADS):
                dvn_ref[rs, GDN_D * h:GDN_D * (h + 1)] = dvn16[h]
        dst[...] = ds

    last = nchunks // nch - 1
    row = pl.BlockSpec((c * nch, GDN_W), lambda n: (last - n, 0))
    return pl.pallas_call(
        body, name=name, grid=(nchunks // nch,),
        in_specs=[row, row, row, row, pl.BlockSpec((GDN_HEADS, c * nch, c), lambda n: (0, last - n, 0)),
                  pl.BlockSpec((c * nch, 128), lambda n: (last - n, 0))],
        out_specs=[row, pl.BlockSpec((nch, GDN_HEADS, GDN_D, GDN_D), lambda n: (last - n, 0, 0, 0))],
        out_shape=[jax.ShapeDtypeStruct((s_len, GDN_W), BF16),
                   jax.ShapeDtypeStruct((nchunks, GDN_HEADS, GDN_D, GDN_D), F32)],
        scratch_shapes=[pltpu.VMEM((GDN_HEADS, GDN_D, GDN_D), F32)],
        compiler_params=_params(("arbitrary",)),
    )(do, wk, qd, kd, p, g_cum)


def _gdn_intra_bwd(qn, kn, vc, bg, g_cum, t, do, dvn, vn, sprev, ds_all, *, name):
    s_len = qn.shape[0]
    c = GDN_CHUNK
    nch = INTRA_CHUNKS
    nsteps = s_len // (c * nch)
    nb = nch * GDN_HEADS

    def body(q_ref, k_ref, v_ref, bg_ref, g_ref, t_ref, do_ref, dvn_ref, vn_ref, sp_ref, ds_ref,
             dqkv_ref, dbg_ref):
        causal, strict = _chunk_masks()
        lane = lax.broadcasted_iota(jnp.int32, (c, 128), 1)
        rowi = lax.broadcasted_iota(jnp.int32, (c, 128), 0)
        parts = []
        for ch in range(nch):
            rs = slice(c * ch, c * (ch + 1))
            bg = bg_ref[rs, :]
            _, g_rows = _cum_decay(bg)
            parts.append(_chunk_decay(g_ref[rs, :], g_rows, bg) + tuple(
                _heads(r, rs) for r in (q_ref, k_ref, v_ref, do_ref, dvn_ref, vn_ref)) + (t_ref[:, rs, :],))
        gc, bc, dec, q, k, v, do, dvn16, vn16, tm = [jnp.concatenate([p[i] for p in parts], axis=0)
                                                     for i in range(10)]
        q16, k16 = q.astype(BF16), k.astype(BF16)
        kk = _bnt(k16, k16)
        low = jnp.where(strict, bc * kk * dec, 0.0)
        eg = jnp.exp(gc)
        g_last = gc[:, c - 1:c, :]
        kdec = jnp.exp(g_last - gc)
        kb, vb, qd, kd = k * (bc * eg), v * bc, q * eg, k * kdec
        pm = jnp.where(causal, _bnt(q16, k16) * dec, 0.0)
        s = sp_ref[...].reshape(nb, GDN_D, GDN_D)
        ds = ds_ref[...].reshape(nb, GDN_D, GDN_D)
        s16, ds16 = s.astype(BF16), ds.astype(BF16)
        do16 = do.astype(BF16)
        t16 = tm.astype(BF16)

        dqd = _bnt(do16, s16)
        dp = jnp.where(causal, _bnt(do16, vn16), 0.0)
        dkd = _bnt(vn16, ds16)
        dgl = jnp.sum(jnp.sum(s * ds, axis=2, keepdims=True), axis=1, keepdims=True) * jnp.exp(g_last)
        dwk16 = (-_bnt(dvn16, s16)).astype(BF16)
        dt = _bnt(dwk16, kb.astype(BF16)) + _bnt(dvn16, vb.astype(BF16))
        dkb = _btn(t16, dwk16)
        dvb = _btn(t16, dvn16)
        th, tl = _split(tm)
        dth, dtl = _split(dt)
        xm = _btn(th, dth) + (_btn(tl, dth) + _btn(th, dtl))
        xh, xl = _split(xm)
        dlow = jnp.where(strict, -(_bnt(xh, th) + (_bnt(xl, th) + _bnt(xh, tl))), 0.0)
        dkk16 = (dlow * bc * dec).astype(BF16)
        dqk16 = (dp * dec).astype(BF16)

        dq = _bnn(dqk16, k16) + dqd * eg
        dk = _btn(dqk16, q16) + _bnn(dkk16, k16) + _btn(dkk16, k16) + dkb * (bc * eg) + dkd * kdec
        dv = dvb * bc
        for idx in range(nb):
            ch, h = divmod(idx, GDN_HEADS)
            rs = slice(c * ch, c * (ch + 1))
            hs = slice(GDN_D * h, GDN_D * (h + 1))
            dqkv_ref[0, rs, hs] = dq[idx]
            dqkv_ref[1, rs, hs] = dk[idx]
            dqkv_ref[2, rs, hs] = dv[idx]

        dbeta = (jnp.sum(dlow * kk * dec, axis=2, keepdims=True)
                 + jnp.sum(dkb * k, axis=2, keepdims=True) * eg + jnp.sum(dvb * v, axis=2, keepdims=True))
        mm = dlow * low + dp * pm
        mh, ml = _split(mm)
        ones16 = jnp.ones((nb, c, 128), BF16)
        col_sum = (_btn(mh, ones16) + _btn(ml, ones16))[:, :, 0:1]
        dkd_sum = jnp.sum(dkd * kd, axis=2, keepdims=True)
        dg = (jnp.sum(mm, axis=2, keepdims=True) - col_sum + jnp.sum(dkb * kb, axis=2, keepdims=True)
              + jnp.sum(dqd * qd, axis=2, keepdims=True) - dkd_sum)
        tail = jnp.sum(dkd_sum, axis=1, keepdims=True) + dgl
        upper = (lax.broadcasted_iota(jnp.int32, (c, c), 0) <= lax.broadcasted_iota(jnp.int32, (c, c), 1)).astype(F32)
        for ch in range(nch):
            dbeta_all = jnp.zeros((c, 128), F32)
            dg_all = jnp.zeros((c, 128), F32)
            for h in range(GDN_HEADS):
                idx = ch * GDN_HEADS + h
                dbeta_all = dbeta_all + jnp.where(lane == h, dbeta[idx], 0.0)
                dg_all = dg_all + jnp.where(lane == GDN_HEADS + h,
                                            dg[idx] + jnp.where(rowi == c - 1, tail[idx], 0.0), 0.0)
            dbg_ref[c * ch:c * (ch + 1), :] = dbeta_all + _nn(upper, dg_all, HIGHEST)

    row = pl.BlockSpec((c * nch, GDN_W), lambda n: (n, 0))
    narrow = pl.BlockSpec((c * nch, 128), lambda n: (n, 0))
    state = pl.BlockSpec((nch, GDN_HEADS, GDN_D, GDN_D), lambda n: (n, 0, 0, 0))
    return pl.pallas_call(
        body, name=name, grid=(nsteps,),
        in_specs=[row, row, row, narrow, narrow, pl.BlockSpec((GDN_HEADS, c * nch, c), lambda n: (0, n, 0)),
                  row, row, row, state, state],
        out_specs=[pl.BlockSpec((3, c * nch, GDN_W), lambda n: (0, n, 0)), narrow],
        out_shape=[jax.ShapeDtypeStruct((3, s_len, GDN_W), F32), jax.ShapeDtypeStruct((s_len, 128), F32)],
        compiler_params=_params(("parallel",)),
    )(qn, kn, vc, bg, g_cum, t, do, dvn, vn, sprev, ds_all)


def _gdn_prep_bwd(u, dqkv, cw, du, *, name, tm=256):
    s_len = u.shape[0]
    nsteps = s_len // tm
    ext = tm + SHALO

    def body(uc, up, un, dc, dn, cw_ref, du_in_ref, du_ref, dcw_ref, xbuf, dbuf, pbuf, wacc):
        del du_in_ref
        part = pl.program_id(0)
        i = pl.program_id(1)

        @pl.when(i == 0)
        def _():
            wacc[...] = jnp.zeros_like(wacc)

        xbuf[:SHALO, :] = jnp.where(i > 0, up[...], 0.0)
        xbuf[SHALO:SHALO + tm, :] = uc[...]
        xbuf[SHALO + tm:, :] = jnp.where(i < nsteps - 1, un[...], 0.0)
        dbuf[:tm, :] = dc[...]
        dbuf[tm:, :] = jnp.where(i < nsteps - 1, dn[...], 0.0)
        first = SHALO - SHORT_CONV + 1
        w = [cw_ref[j:j + 1, :] for j in range(SHORT_CONV)]
        taps = [first + j for j in range(SHORT_CONV)]
        xv = xbuf[...]
        pre = jnp.zeros((ext, GDN_W), F32)
        for j, win in _shifted_windows(xv, ext, taps):
            pre = pre + win * w[j]
        y = _silu(pre)
        dout = dbuf[...]
        scale = jnp.where(part == 0, GDN_D ** -0.5, 1.0)
        for h in range(GDN_HEADS):
            hs = slice(GDN_D * h, GDN_D * (h + 1))
            yh, dh = y[:, hs], dout[:, hs]
            rs = lax.rsqrt(jnp.sum(yh * yh, axis=-1, keepdims=True) + 1e-6)
            dyn = scale * rs * (dh - yh * (rs * rs) * jnp.sum(dh * yh, axis=-1, keepdims=True))
            dy = jnp.where(part < 2, dyn, dh)
            pbuf[:, hs] = dy * _dsilu(pre[:, hs])
        acc = jnp.zeros((tm, GDN_W), F32)
        dpre = pbuf[0:tm, :]
        for k, win in _shifted_windows(pbuf[...], tm, list(range(SHORT_CONV))):
            acc = acc + win * w[SHORT_CONV - 1 - k]
        for j, win in _shifted_windows(xv, tm, taps):
            wacc[j] += (win * dpre).reshape(tm // 8, 8, GDN_W).sum(axis=0)
        du_ref[...] = acc.astype(du_ref.dtype)

        @pl.when(i == nsteps - 1)
        def _():
            for j in range(SHORT_CONV):
                dcw_ref[j:j + 1, :] = jnp.sum(wacc[j], axis=0, keepdims=True)
            dcw_ref[SHORT_CONV:, :] = jnp.zeros((SHALO - SHORT_CONV, GDN_W), F32)

    per = tm // SHALO
    return pl.pallas_call(
        body, name=name, grid=(3, nsteps),
        in_specs=[pl.BlockSpec((tm, GDN_W), lambda p, i: (i, COL_GQ + p)),
                  pl.BlockSpec((SHALO, GDN_W), lambda p, i: (jnp.maximum(i * per - 1, 0), COL_GQ + p)),
                  pl.BlockSpec((SHALO, GDN_W), lambda p, i: (jnp.minimum((i + 1) * per, s_len // SHALO - 1), COL_GQ + p)),
                  pl.BlockSpec((None, tm, GDN_W), lambda p, i: (p, i, 0)),
                  pl.BlockSpec((None, SHALO, GDN_W), lambda p, i: (p, jnp.minimum((i + 1) * per, s_len // SHALO - 1), 0)),
                  pl.BlockSpec((SHALO, GDN_W), lambda p, i: (0, p)),
                  pl.BlockSpec(memory_space=pl.ANY)],
        out_specs=[pl.BlockSpec((tm, GDN_W), lambda p, i: (i, COL_GQ + p)),
                   pl.BlockSpec((SHALO, GDN_W), lambda p, i: (0, p))],
        out_shape=[jax.ShapeDtypeStruct(du.shape, du.dtype), jax.ShapeDtypeStruct((SHALO, 3 * GDN_W), F32)],
        scratch_shapes=[pltpu.VMEM((tm + 2 * SHALO, GDN_W), F32), pltpu.VMEM((ext, GDN_W), F32),
                        pltpu.VMEM((ext, GDN_W), F32), pltpu.VMEM((SHORT_CONV, 8, GDN_W), F32)],
        input_output_aliases={6: 0},
        compiler_params=_params(("arbitrary", "arbitrary")),
    )(u, u, u, dqkv, dqkv, cw, du)


def _gdn_ba_bwd(u, dbg, al, dtb, du, *, name, tm=256):
    s_len = u.shape[0]
    nsteps = s_len // tm
    wpad = IN_WP - COL_BA

    def body(uba, dbg_ref, al_ref, dtb_ref, du_in_ref, du_ref, sums_ref):
        del du_in_ref
        i = pl.program_id(0)

        @pl.when(i == 0)
        def _():
            sums_ref[...] = jnp.zeros_like(sums_ref)

        ba = uba[...]
        dbg = dbg_ref[...]
        lane = lax.broadcasted_iota(jnp.int32, ba.shape, 1)
        is_g = (lane >= GDN_HEADS) & (lane < 2 * GDN_HEADS)
        beta = _sigmoid(ba)
        z = ba + dtb_ref[...]
        ea = jnp.exp(al_ref[...])
        g = -ea * _softplus(z)
        dz = jnp.where(is_g, dbg * (-ea) * _sigmoid(z), 0.0)
        du_ref[:, :128] = jnp.where(lane < GDN_HEADS, dbg * beta * (1.0 - beta), dz).astype(du_ref.dtype)
        du_ref[:, 128:] = jnp.zeros((tm, wpad - 128), du_ref.dtype)
        sums_ref[0:1, :] += jnp.sum(jnp.where(is_g, dbg * g, 0.0), axis=0, keepdims=True)
        sums_ref[1:2, :] += jnp.sum(dz, axis=0, keepdims=True)

    vec = pl.BlockSpec((1, 128), lambda i: (0, 0))
    return pl.pallas_call(
        body, name=name, grid=(nsteps,),
        in_specs=[pl.BlockSpec((tm, 128), lambda i: (i, COL_BA // 128)), pl.BlockSpec((tm, 128), lambda i: (i, 0)),
                  vec, vec, pl.BlockSpec(memory_space=pl.ANY)],
        out_specs=[pl.BlockSpec((tm, wpad), lambda i: (i, COL_BA // wpad)), pl.BlockSpec((8, 128), lambda i: (0, 0))],
        out_shape=[jax.ShapeDtypeStruct(du.shape, du.dtype), jax.ShapeDtypeStruct((8, 128), F32)],
        input_output_aliases={4: 0},
        compiler_params=_params(("arbitrary",)),
    )(u, dbg, al, dtb, du)


def _rope_tables(s_len):
    half = ROPE_DIM // 2
    inv = ROPE_THETA ** (-jnp.arange(half, dtype=F32) / half)
    ang = jnp.arange(s_len, dtype=F32)[:, None] * inv[None, :]
    cos, sin = jnp.cos(ang), jnp.sin(ang)
    one = jnp.ones((s_len, ATT_HD - ROPE_DIM), F32)
    zero = jnp.zeros((s_len, ATT_HD - ROPE_DIM), F32)
    zh = jnp.zeros((s_len, half), F32)
    c = jnp.concatenate([cos, cos, one], axis=1)
    s1 = jnp.concatenate([-sin, zh, zero], axis=1)
    s2 = jnp.concatenate([zh, sin, zero], axis=1)
    return tuple(jnp.concatenate([t, t], axis=1) for t in (c, s1, s2))


def _rope(x, c, s1, s2):
    return x * c + pltpu.roll(x, 128 - ROPE_DIM // 2, 1) * s1 + pltpu.roll(x, ROPE_DIM // 2, 1) * s2


def _rope_t(dy, c, s1, s2):
    return dy * c + pltpu.roll(dy * s1, ROPE_DIM // 2, 1) + pltpu.roll(dy * s2, 128 - ROPE_DIM // 2, 1)


DILATIONS = tuple(d for _, d in DIL_PATTERNS)
ATT_QBLOCKS = 2
VIEW_ROWS = 256


def _to_view(scr, out_ref, dil, dtype):
    nblk, rows, _ = scr.shape
    width = nblk * 128
    for b in range(nblk):
        if dil == 1:
            out_ref[:, 128 * b:128 * (b + 1)] = scr[b].astype(dtype)
            continue
        for r in range(dil):
            out_ref[:, r * width + 128 * b:r * width + 128 * (b + 1)] = (
                scr.at[b][pl.ds(r, rows // dil, stride=dil), :].astype(dtype))


def _from_view(in_ref, scr, dil):
    nblk, rows, _ = scr.shape
    width = nblk * 128
    for b in range(nblk):
        for r in range(dil):
            scr.at[b][pl.ds(r, rows // dil, stride=dil), :] = in_ref[:, r * width + 128 * b:r * width + 128 * (b + 1)]


def _view_spec(dil, width, tm=VIEW_ROWS):
    return pl.BlockSpec((tm // dil, dil * width), lambda i: (i, 0))


def _view_shape(s_len, dil, width, dtype):
    return jax.ShapeDtypeStruct((s_len // dil, dil * width), dtype)


def _att_prep_fwd(u, tabs, *, name):
    s_len = u.shape[0]
    tm = VIEW_ROWS
    scale = ATT_HD ** -0.5
    nblk = ATT_W // 128

    def body(uq, uk, uv, c_ref, s1_ref, s2_ref, *rest):
        outs, scr = rest[:-1], rest[-1]
        c, s1, s2 = c_ref[...], s1_ref[...], s2_ref[...]
        for part, src in enumerate((uq, uk, uv)):
            for b in range(nblk):
                xb = src[:, 128 * b:128 * (b + 1)]
                if part == 0:
                    xb = _rope(xb, c, s1, s2) * scale
                elif part == 1:
                    xb = _rope(xb, c, s1, s2)
                scr[b] = xb
            for gi, dil in enumerate(DILATIONS):
                _to_view(scr, outs[3 * gi + part], dil, BF16)

    tab = pl.BlockSpec((tm, 128), lambda i: (i, 0))
    outs = pl.pallas_call(
        body, name=name, grid=(s_len // tm,),
        in_specs=[pl.BlockSpec((tm, ATT_W), lambda i, col=COL_AQ + j: (i, col)) for j in range(3)] + [tab] * 3,
        out_specs=[_view_spec(dil, ATT_W) for dil in DILATIONS for _ in range(3)],
        out_shape=[_view_shape(s_len, dil, ATT_W, BF16) for dil in DILATIONS for _ in range(3)],
        scratch_shapes=[pltpu.VMEM((nblk, tm, 128), F32)],
        compiler_params=_params(("parallel",)),
    )(u, u, u, *tabs)
    return [outs[3 * gi:3 * gi + 3] for gi in range(len(DILATIONS))]


def _stack_heads(x):
    lane = lax.broadcasted_iota(jnp.int32, (1, 128), 1)
    zero = jnp.zeros_like(x)
    return jnp.concatenate([jnp.where(lane < ATT_HD, x, zero), jnp.where(lane >= ATT_HD, x, zero)], axis=0)


def _att_fwd(qr, kr, vb, dil, *, name):
    lr = qr.shape[0]
    blk = ATT_BLOCK
    qb = ATT_QBLOCKS
    nsteps = lr // (blk * qb)

    def body(q_ref, kp_ref, kc_ref, vp_ref, vc_ref, o_ref, lse_ref):
        n = pl.program_id(1)
        qi = lax.broadcasted_iota(jnp.int32, (blk, 2 * blk), 0)
        ki = lax.broadcasted_iota(jnp.int32, (blk, 2 * blk), 1)
        dist = qi + blk - ki
        band = (dist >= 0) & (dist <= blk)
        lane = lax.broadcasted_iota(jnp.int32, (blk, 128), 1)
        for sub in range(qb):
            rs = slice(blk * sub, blk * (sub + 1))
            valid = band if sub > 0 else band & ((ki >= blk) | (n > 0))
            valid = jnp.concatenate([valid, valid], axis=0)
            lse_all = jnp.zeros((blk, 128), F32)
            for hp in range(ATT_HEADS // 2):
                bs = slice(128 * hp, 128 * (hp + 1))
                if sub == 0:
                    kb = jnp.concatenate([kp_ref[:, bs], kc_ref[0:blk, bs]], axis=0)
                    vv = jnp.concatenate([vp_ref[:, bs], vc_ref[0:blk, bs]], axis=0)
                else:
                    kb = kc_ref[blk * (sub - 1):blk * (sub + 1), bs]
                    vv = vc_ref[blk * (sub - 1):blk * (sub + 1), bs]
                s = jnp.where(valid, _nt(_stack_heads(q_ref[rs, bs]), kb), NEG_INF)
                m = jnp.max(s, axis=-1, keepdims=True)
                p = jnp.exp(s - m)
                l = jnp.sum(p, axis=-1, keepdims=True)
                o = _nn((p * (1.0 / l)).astype(BF16), vv)
                o_ref[rs, bs] = jnp.where(lane < ATT_HD, o[:blk], o[blk:])
                lse = m + jnp.log(l)
                lse_all = (lse_all + jnp.where(lane == 2 * hp, lse[:blk], 0.0)
                           + jnp.where(lane == 2 * hp + 1, lse[blk:], 0.0))
            lse_ref[rs, :] = lse_all

    cur = pl.BlockSpec((blk * qb, ATT_W), lambda r, n: (n, r))
    prev = pl.BlockSpec((blk, ATT_W), lambda r, n: (jnp.maximum(qb * n - 1, 0), r))
    return pl.pallas_call(
        body, name=name, grid=(dil, nsteps), in_specs=[cur, prev, cur, prev, cur],
        out_specs=[cur, pl.BlockSpec((blk * qb, 128), lambda r, n: (n, r))],
        out_shape=[jax.ShapeDtypeStruct(qr.shape, F32), jax.ShapeDtypeStruct((lr, dil * 128), F32)],
        compiler_params=_params(("parallel", "parallel")),
    )(qr, kr, kr, vb, vb)


def _att_bwd(qr, kr, vb, do, lse, delta, dil, *, name):
    lr = qr.shape[0]
    blk = ATT_BLOCK
    nsteps = lr // (2 * blk)

    def body(q_ref, kp_ref, kc_ref, vp_ref, vc_ref, do_ref, lse_ref, dl_ref, dq_ref, dk_ref, dv_ref, carry):
        n = pl.program_id(1)

        @pl.when(n == 0)
        def _():
            carry[...] = jnp.zeros_like(carry)

        @pl.when(n == nsteps)
        def _():
            for t, ref in enumerate((dk_ref, dv_ref)):
                ref[:blk, :] = carry[0, t]
                ref[blk:, :] = carry[1, t]

        @pl.when(n < nsteps)
        def _():
            qi = lax.broadcasted_iota(jnp.int32, (blk, 2 * blk), 0)
            ki = lax.broadcasted_iota(jnp.int32, (blk, 2 * blk), 1)
            dist = qi + blk - ki
            band = (dist >= 0) & (dist <= blk)
            lane = lax.broadcasted_iota(jnp.int32, (blk, 128), 1)
            for hp in range(ATT_HEADS // 2):
                bs = slice(128 * hp, 128 * (hp + 1))
                accs = []
                for sub in range(2):
                    rs = slice(blk * sub, blk * (sub + 1))
                    valid = band if sub > 0 else band & ((ki >= blk) | (n > 0))
                    valid = jnp.concatenate([valid, valid], axis=0)
                    if sub == 0:
                        kb = jnp.concatenate([kp_ref[:, bs], kc_ref[0:blk, bs]], axis=0)
                        vv = jnp.concatenate([vp_ref[:, bs], vc_ref[0:blk, bs]], axis=0)
                    else:
                        kb, vv = kc_ref[:, bs], vc_ref[:, bs]
                    q2 = _stack_heads(q_ref[rs, bs])
                    do2 = _stack_heads(do_ref[rs, bs])
                    lse2 = jnp.concatenate([lse_ref[rs, 2 * hp:2 * hp + 1], lse_ref[rs, 2 * hp + 1:2 * hp + 2]], axis=0)
                    dl2 = jnp.concatenate([dl_ref[rs, 2 * hp:2 * hp + 1], dl_ref[rs, 2 * hp + 1:2 * hp + 2]], axis=0)
                    p = jnp.where(valid, jnp.exp(_nt(q2, kb) - lse2), 0.0)
                    ds16 = (p * (_nt(do2, vv) - dl2)).astype(BF16)
                    dq2 = _nn(ds16, kb)
                    dq_ref[rs, bs] = jnp.where(lane < ATT_HD, dq2[:blk], dq2[blk:])
                    accs.append((_tn(ds16, q2), _tn(p.astype(BF16), do2)))
                for t, ref in enumerate((dk_ref, dv_ref)):
                    first, second = accs[0][t], accs[1][t]
                    ref[:blk, bs] = carry[0, t, :, bs]
                    ref[blk:, bs] = carry[1, t, :, bs] + first[:blk]
                    carry[0, t, :, bs] = first[blk:] + second[:blk]
                    carry[1, t, :, bs] = second[blk:]

    def at(n):
        return jnp.minimum(n, nsteps - 1)

    cur = pl.BlockSpec((2 * blk, ATT_W), lambda r, n: (at(n), r))
    prev = pl.BlockSpec((blk, ATT_W), lambda r, n: (jnp.maximum(2 * at(n) - 1, 0), r))
    nar = pl.BlockSpec((2 * blk, 128), lambda r, n: (at(n), r))
    late = pl.BlockSpec((2 * blk, ATT_W), lambda r, n: (jnp.maximum(n - 1, 0), r))
    out = jax.ShapeDtypeStruct(qr.shape, F32)
    return pl.pallas_call(
        body, name=name, grid=(dil, nsteps + 1), in_specs=[cur, prev, cur, prev, cur, cur, nar, nar],
        out_specs=[cur, late, late], out_shape=[out, out, out],
        scratch_shapes=[pltpu.VMEM((2, 2, blk, ATT_W), F32)],
        compiler_params=_params(("parallel", "arbitrary")),
    )(qr, kr, kr, vb, vb, do, lse, delta)


def _att_prep_bwd(dgroups, tabs, du, *, name):
    s_len = du.shape[0]
    tm = VIEW_ROWS
    scale = ATT_HD ** -0.5
    nblk = ATT_W // 128
    ng = len(DILATIONS)

    def body(*refs):
        grads = refs[:3 * ng]
        c_ref, s1_ref, s2_ref, _, du_ref = refs[3 * ng:3 * ng + 5]
        scrs = refs[3 * ng + 5:]
        c, s1, s2 = c_ref[...], s1_ref[...], s2_ref[...]
        for part in range(3):
            for gi, dil in enumerate(DILATIONS):
                if dil > 1:
                    _from_view(grads[3 * gi + part], scrs[gi], dil)
            for b in range(nblk):
                tot = None
                for gi, dil in enumerate(DILATIONS):
                    term = grads[3 * gi + part][:, 128 * b:128 * (b + 1)] if dil == 1 else scrs[gi][b]
                    tot = term if tot is None else tot + term
                if part == 0:
                    tot = _rope_t(tot * scale, c, s1, s2)
                elif part == 1:
                    tot = _rope_t(tot, c, s1, s2)
                du_ref[:, ATT_W * part + 128 * b:ATT_W * part + 128 * (b + 1)] = tot.astype(du_ref.dtype)

    tab = pl.BlockSpec((tm, 128), lambda i: (i, 0))
    return pl.pallas_call(
        body, name=name, grid=(s_len // tm,),
        in_specs=[_view_spec(dil, ATT_W) for dil in DILATIONS for _ in range(3)] + [tab] * 3
        + [pl.BlockSpec(memory_space=pl.ANY)],
        out_specs=pl.BlockSpec((tm, 3 * ATT_W), lambda i: (i, COL_AQ // 3)),
        out_shape=jax.ShapeDtypeStruct(du.shape, du.dtype),
        scratch_shapes=[pltpu.VMEM((nblk, tm, 128), F32) for _ in DILATIONS],
        input_output_aliases={3 * ng + 3: 0},
        compiler_params=_params(("parallel",)),
    )(*[a for g in dgroups for a in g], *tabs, du)


def _head_weights(w, b):
    lane = lax.broadcasted_iota(jnp.int32, (1, 128), 1)
    return jnp.where(lane < ATT_HD, w[:, 2 * b:2 * b + 1], w[:, 2 * b + 1:2 * b + 2])


def _assemble_fwd(pw, u, o_gdn, gnw, o_groups, lse_groups, *, name):
    s_len = u.shape[0]
    tm = VIEW_ROWS
    c = CONV_CH
    nblk = ATT_W // 128
    ng = len(DILATIONS)

    def body(*refs):
        pw_ref, cg_ref, z_ref, ag_ref, og_ref, gnw_ref = refs[:6]
        o_refs, l_refs = refs[6:6 + ng], refs[6 + ng:6 + 2 * ng]
        y_ref, oa_ref = refs[6 + 2 * ng:8 + 2 * ng]
        lse_outs = refs[8 + 2 * ng:8 + 3 * ng]
        o_scr, l_scr = refs[8 + 3 * ng:8 + 4 * ng], refs[8 + 4 * ng:8 + 5 * ng]
        lse_scr = refs[8 + 5 * ng]
        y_ref[:, :c] = (pw_ref[...] * _silu(cg_ref[...])).astype(BF16)
        gw = gnw_ref[...]
        for h in range(GDN_HEADS):
            hs = slice(GDN_D * h, GDN_D * (h + 1))
            oh = og_ref[:, hs]
            yn = oh * lax.rsqrt(jnp.mean(oh * oh, axis=-1, keepdims=True) + 1e-6) * gw
            y_ref[:, c + GDN_D * h:c + GDN_D * (h + 1)] = (yn * _silu(z_ref[:, hs])).astype(BF16)
        for gi, dil in enumerate(DILATIONS):
            if dil > 1:
                _from_view(o_refs[gi], o_scr[gi], dil)
                _from_view(l_refs[gi], l_scr[gi], dil)
        ls = [l_refs[gi][...] if dil == 1 else l_scr[gi][0] for gi, dil in enumerate(DILATIONS)]
        m = functools.reduce(jnp.maximum, ls)
        es = [jnp.exp(l - m) for l in ls]
        den = functools.reduce(lambda a, b: a + b, es)
        lse_scr[0] = m + jnp.log(den)
        ws = [e / den for e in es]
        for b in range(nblk):
            bs = slice(128 * b, 128 * (b + 1))
            o = None
            for gi, dil in enumerate(DILATIONS):
                term = _head_weights(ws[gi], b) * (o_refs[gi][:, bs] if dil == 1 else o_scr[gi][b])
                o = term if o is None else o + term
            oa_ref[:, bs] = o
            y_ref[:, c + GDN_W + 128 * b:c + GDN_W + 128 * (b + 1)] = (o * _silu(ag_ref[:, bs])).astype(BF16)
        for gi, dil in enumerate(DILATIONS):
            _to_view(lse_scr, lse_outs[gi], dil, F32)

    wide = pl.BlockSpec((tm, 768), lambda i: (i, 0))
    return pl.pallas_call(
        body, name=name, grid=(s_len // tm,),
        in_specs=[pl.BlockSpec((tm, c), lambda i: (i, 0)), pl.BlockSpec((tm, c), lambda i: (i, 1024 // c)),
                  pl.BlockSpec((tm, 768), lambda i: (i, COL_GQ + 3)), pl.BlockSpec((tm, 768), lambda i: (i, COL_AQ + 3)),
                  wide, pl.BlockSpec((1, 128), lambda i: (0, 0))]
        + [_view_spec(dil, ATT_W) for dil in DILATIONS] + [_view_spec(dil, 128) for dil in DILATIONS],
        out_specs=[pl.BlockSpec((tm, D_MODEL), lambda i: (i, 0)), wide] + [_view_spec(dil, 128) for dil in DILATIONS],
        out_shape=[jax.ShapeDtypeStruct((s_len, D_MODEL), BF16), jax.ShapeDtypeStruct((s_len, ATT_W), F32)]
        + [_view_shape(s_len, dil, 128, F32) for dil in DILATIONS],
        scratch_shapes=[pltpu.VMEM((nblk, tm, 128), F32) for _ in DILATIONS]
        + [pltpu.VMEM((1, tm, 128), F32) for _ in DILATIONS] + [pltpu.VMEM((1, tm, 128), F32)],
        compiler_params=_params(("parallel",)),
    )(pw, u, u, u, o_gdn, gnw, *o_groups, *lse_groups)


def _assemble_bwd(dy, pw, u, o_gdn, gnw, o_att, *, name):
    s_len = u.shape[0]
    tm = VIEW_ROWS
    c = CONV_CH
    nsteps = s_len // tm
    nblk = ATT_W // 128
    ng = len(DILATIONS)

    def body(dy_ref, pw_ref, cg_ref, z_ref, ag_ref, og_ref, gnw_ref, oa_ref,
             du_ref, dpw_ref, dog_ref, dgw_ref, *rest):
        do_outs, dl_outs = rest[:ng], rest[ng:2 * ng]
        acc_ref, do_scr, dl_scr = rest[2 * ng:]
        i = pl.program_id(0)

        @pl.when(i == 0)
        def _():
            acc_ref[...] = jnp.zeros_like(acc_ref)

        du_ref[...] = jnp.zeros_like(du_ref)
        dyc = dy_ref[:, :c]
        cg = cg_ref[...]
        dpw_ref[...] = dyc * _silu(cg)
        du_ref[:, 1024:1024 + c] = (dyc * pw_ref[...] * _dsilu(cg)).astype(BF16)
        gw = gnw_ref[...]
        dgw = jnp.zeros((8, 128), F32)
        for h in range(GDN_HEADS):
            hs = slice(GDN_D * h, GDN_D * (h + 1))
            oh = og_ref[:, hs]
            zh = z_ref[:, hs]
            dyh = dy_ref[:, c + GDN_D * h:c + GDN_D * (h + 1)]
            r = lax.rsqrt(jnp.mean(oh * oh, axis=-1, keepdims=True) + 1e-6)
            xn = oh * r
            dyn = dyh * _silu(zh)
            du_ref[:, GDN_W * (COL_GQ + 3) + GDN_D * h:GDN_W * (COL_GQ + 3) + GDN_D * (h + 1)] = (
                dyh * xn * gw * _dsilu(zh)).astype(BF16)
            dgw = dgw + (dyn * xn).reshape(tm // 8, 8, 128).sum(axis=0)
            dxn = dyn * gw
            dog_ref[:, hs] = r * (dxn - xn * jnp.mean(dxn * xn, axis=-1, keepdims=True))
        acc_ref[...] += dgw
        lane = lax.broadcasted_iota(jnp.int32, (tm, 128), 1)
        delta = jnp.zeros((tm, 128), F32)
        for b in range(ATT_W // 128):
            bs = slice(128 * b, 128 * (b + 1))
            dya = dy_ref[:, c + GDN_W + 128 * b:c + GDN_W + 128 * (b + 1)]
            ag = ag_ref[:, bs]
            oa = oa_ref[:, bs]
            do = dya * _silu(ag)
            do_scr[b] = do
            du_ref[:, ATT_W * (COL_AQ + 3) + 128 * b:ATT_W * (COL_AQ + 3) + 128 * (b + 1)] = (
                dya * oa * _dsilu(ag)).astype(BF16)
            prod = do * oa
            lo = jnp.sum(jnp.where(lane < ATT_HD, prod, 0.0), axis=-1, keepdims=True)
            hi = jnp.sum(jnp.where(lane >= ATT_HD, prod, 0.0), axis=-1, keepdims=True)
            delta = delta + jnp.where(lane == 2 * b, lo, 0.0) + jnp.where(lane == 2 * b + 1, hi, 0.0)
        dl_scr[0] = delta
        for gi, dil in enumerate(DILATIONS):
            _to_view(do_scr, do_outs[gi], dil, BF16)
            _to_view(dl_scr, dl_outs[gi], dil, F32)

        @pl.when(i == nsteps - 1)
        def _():
            dgw_ref[...] = jnp.sum(acc_ref[...], axis=0, keepdims=True)

    wide = pl.BlockSpec((tm, 768), lambda i: (i, 0))
    vec = pl.BlockSpec((1, 128), lambda i: (0, 0))
    outs = pl.pallas_call(
        body, name=name, grid=(nsteps,),
        in_specs=[pl.BlockSpec((tm, D_MODEL), lambda i: (i, 0)), pl.BlockSpec((tm, c), lambda i: (i, 0)),
                  pl.BlockSpec((tm, c), lambda i: (i, 1024 // c)), pl.BlockSpec((tm, 768), lambda i: (i, COL_GQ + 3)),
                  pl.BlockSpec((tm, 768), lambda i: (i, COL_AQ + 3)), wide, vec, wide],
        out_specs=[pl.BlockSpec((tm, IN_WP), lambda i: (i, 0)), pl.BlockSpec((tm, c), lambda i: (i, 0)), wide, vec]
        + [_view_spec(dil, ATT_W) for dil in DILATIONS] + [_view_spec(dil, 128) for dil in DILATIONS],
        out_shape=[jax.ShapeDtypeStruct((s_len, IN_WP), BF16), jax.ShapeDtypeStruct((s_len, c), F32),
                   jax.ShapeDtypeStruct((s_len, GDN_W), F32), jax.ShapeDtypeStruct((1, 128), F32)]
        + [_view_shape(s_len, dil, ATT_W, BF16) for dil in DILATIONS]
        + [_view_shape(s_len, dil, 128, F32) for dil in DILATIONS],
        scratch_shapes=[pltpu.VMEM((8, 128), F32), pltpu.VMEM((nblk, tm, 128), F32), pltpu.VMEM((1, tm, 128), F32)],
        compiler_params=_params(("arbitrary",)),
    )(dy, pw, u, u, u, o_gdn, gnw, o_att)
    return outs[:4], outs[4:4 + ng], outs[4 + ng:]


def _layer_fwd(x, p, tabs):
    h = _rms_fwd(x, p["norm_w"], name="rms_fwd")
    u = _matmul(h, p["wp"], name="in_proj", tk=2048)
    conv, sw = _conf_fwd(u, p["dw_w"], p["dw_b"], p["ln_w"], p["ln_b"], name="conf_fwd")
    pw = _matmul(sw, p["pw_w"], name="conf_pw")
    qn, kn, vc, bg = _gdn_prep_fwd(u, p["cw"], p["al"], p["dtb"], name="gdn_prep_fwd")
    wk, wv, qd, kd, pm, t, g_cum = _gdn_intra_fwd(qn, kn, vc, bg, name="gdn_intra_fwd")
    o_gdn, vn, sprev = _gdn_scan_fwd(wk, wv, qd, kd, pm, g_cum, name="gdn_scan_fwd")
    qkv = _att_prep_fwd(u, tabs, name="att_prep_fwd")
    groups = [_att_fwd(*qkv[gi], dil, name=f"att_fwd_d{dil}") for gi, dil in enumerate(DILATIONS)]
    outs = _assemble_fwd(pw, u, o_gdn, p["gnw"], [g[0] for g in groups], [g[1] for g in groups],
                         name="assemble_fwd")
    y, o_att, lse = outs[0], outs[1], outs[2:]
    x_next = _matmul(y, p["wout"], add=x, name="out_proj", tk=2048)
    saved = dict(x=x, h=h, u=u, conv=conv, sw=sw, pw=pw, qn=qn, kn=kn, vc=vc, bg=bg, wk=wk, qd=qd, kd=kd, pm=pm,
                 t=t, g_cum=g_cum, vn=vn, sprev=sprev, o_gdn=o_gdn, qkv=qkv, o_att=o_att, lse=lse, y=y)
    return x_next, saved


def _layer_bwd(dx_out, s, p, tabs, layer, big):
    dy = _matmul(dx_out, p["wout"], tb=True, name="out_proj_dy", tk=2048)
    d_wout = _matmul(s["y"], dx_out, ta=True, name="out_proj_dw", tk=2048, stack=(big[1], layer, DEPTH))
    (du, dpw, dog, dgw), do_views, dl_views = _assemble_bwd(dy, s["pw"], s["u"], s["o_gdn"], p["gnw"], s["o_att"],
                                                            name="assemble_bwd")
    dsw = _matmul(dpw, p["pw_w"], tb=True, name="conf_pw_dx")
    d_pw_w = _matmul(s["sw"], dpw, ta=True, name="conf_pw_dw", stack=(big[2], layer, DEPTH))
    dconv, ln_sums = _conf_bwd_ln(dsw, s["conv"], p["ln_w"], p["ln_b"], name="conf_bwd_ln")
    du, d_dw_w = _conf_bwd_conv(s["u"], dconv, p["dw_w"], du, name="conf_bwd_conv")
    dvn, ds_all = _gdn_scan_bwd(dog, s["wk"], s["qd"], s["kd"], s["pm"], s["g_cum"], name="gdn_scan_bwd")
    dqkv, dbg = _gdn_intra_bwd(s["qn"], s["kn"], s["vc"], s["bg"], s["g_cum"], s["t"], dog, dvn, s["vn"],
                               s["sprev"], ds_all, name="gdn_intra_bwd")
    du, d_cw = _gdn_prep_bwd(s["u"], dqkv, p["cw"], du, name="gdn_prep_bwd")
    du, ba_sums = _gdn_ba_bwd(s["u"], dbg, p["al"], p["dtb"], du, name="gdn_ba_bwd")
    dgroups = []
    for gi, dil in enumerate(DILATIONS):
        args = (*s["qkv"][gi], do_views[gi], s["lse"][gi], dl_views[gi], dil)
        dgroups.append(_att_bwd(*args, name=f"att_bwd_d{dil}"))
    du = _att_prep_bwd(dgroups, tabs, du, name="att_prep_bwd")
    dh = _matmul(du, p["wp"], tb=True, name="in_proj_dx", tk=4096)
    d_wp = _matmul(s["h"], du, ta=True, name="in_proj_dw", tk=4096, stack=(big[0], layer, DEPTH))
    dx, d_norm_w = _rms_bwd(s["x"], dh, p["norm_w"], dx_out, name="rms_bwd")
    small = dict(norm_w=d_norm_w, gnw=dgw, ln_sums=ln_sums, dw_w=d_dw_w, cw=d_cw, ba_sums=ba_sums)
    return dx, (d_wp, d_wout, d_pw_w), small


def _trunk(x, target, params, final_norm_w):
    tabs = _rope_tables(x.shape[0])
    layers = [{k: v[l] for k, v in params.items()} for l in range(DEPTH)]
    saved = []
    for p in layers:
        x, s = _layer_fwd(x, p, tabs)
        saved.append(s)
    dx, d_final, loss = _loss_head(x, final_norm_w, target, name="loss_head")
    big = (None, None, None)
    small = [None] * DEPTH
    for l in reversed(range(DEPTH)):
        dx, big, small[l] = _layer_bwd(dx, saved[l], layers[l], tabs, l, big)
    grads = {k: jnp.stack([sm[k] for sm in small]) for k in small[0]}
    grads.update(wp=big[0], wout=big[1], pw_w=big[2])
    return loss[0, 0], dx, grads, d_final


ANY = pl.BlockSpec(memory_space=pl.ANY)


def _position():
    return lax.axis_index("x"), lax.axis_index("y"), lax.axis_index("c")


def _other_chips(x, y):
    return [(1 - x, y), (x, 1 - y), (1 - x, 1 - y)]


def _gather_chips(shards, *, name):
    n = len(shards)
    kinds = 12

    def body(*refs):
        ins, outs = refs[:n], refs[n:2 * n]
        send, recv = refs[2 * n:]
        x, y, c = _position()
        me, sib = (x, y, c), (x, y, 1 - c)
        xn, yn, dg = (1 - x, y), (x, 1 - y), (1 - x, 1 - y)
        pa, pb = 2 * c, 2 * c + 1

        def copy(k, a, chip, layer, to, src=None):
            dst = outs[a].at[2 * chip[0] + chip[1], pl.ds(layer, 1)]
            return pltpu.make_async_remote_copy(
                src_ref=dst if src is None else src, dst_ref=dst, send_sem=send.at[k * n + a],
                recv_sem=recv.at[k * n + a], device_id=to, device_id_type=MESH)

        def own(k, a, layer, chip):
            return copy(k, a, (x, y), layer, (*chip, c), src=ins[a].at[pl.ds(layer, 1)])

        sends = []
        for a in range(n):
            sends += [own(0, a, pa, xn), own(1, a, pb, yn), own(2, a, pb, xn), own(3, a, pa, yn)]
        for cp in sends:
            cp.start()
        arrivals = [(1, yn, pb, (4, xn)), (0, xn, pa, (5, yn)), (2, xn, pb, None), (3, yn, pa, None),
                    (4, dg, pb, None), (5, dg, pa, None)]
        for a in range(n):
            for j, (k, chip, layer, onward) in enumerate(arrivals):
                copy(k, a, chip, layer, me).wait_recv()
                if onward is not None:
                    cp = copy(onward[0], a, chip, layer, (*onward[1], c))
                    cp.start()
                    sends.append(cp)
                cp = copy(6 + j, a, chip, layer, sib)
                cp.start()
                sends.append(cp)
        for a in range(n):
            for j, (k, chip, layer, onward) in enumerate(arrivals):
                copy(6 + j, a, chip, layer + 2 - 4 * c, me).wait_recv()
        for cp in sends:
            cp.wait_send()

    return pl.pallas_call(
        body, name=name, in_specs=[ANY] * n, out_specs=[ANY] * n,
        out_shape=[jax.ShapeDtypeStruct((4,) + s.shape, s.dtype) for s in shards],
        scratch_shapes=[pltpu.SemaphoreType.DMA((kinds * n,)), pltpu.SemaphoreType.DMA((kinds * n,))],
    )(*shards)


def _to_sibling(arrs, *, name):
    n = len(arrs)

    def body(*refs):
        ins, outs = refs[:n], refs[n:2 * n]
        send, recv = refs[2 * n:]
        x, y, c = _position()
        cps = [pltpu.make_async_remote_copy(src_ref=ins[a], dst_ref=outs[a], send_sem=send.at[a],
                                            recv_sem=recv.at[a], device_id=(x, y, 1 - c), device_id_type=MESH)
               for a in range(n)]
        for cp in cps:
            cp.start()
        for cp in cps:
            cp.wait()

    return pl.pallas_call(
        body, name=name, in_specs=[ANY] * n, out_specs=[ANY] * n,
        out_shape=[jax.ShapeDtypeStruct(a.shape, a.dtype) for a in arrs],
        scratch_shapes=[pltpu.SemaphoreType.DMA((n,)), pltpu.SemaphoreType.DMA((n,))],
    )(*arrs)


def _to_chips(arrs, *, name):
    n = len(arrs)

    def body(*refs):
        ins, outs = refs[:n], refs[n:2 * n]
        send, recv = refs[2 * n:]
        x, y, c = _position()
        cps = [pltpu.make_async_remote_copy(
            src_ref=ins[a].at[2 * chip[0] + chip[1]], dst_ref=outs[a].at[j], send_sem=send.at[j * n + a],
            recv_sem=recv.at[j * n + a], device_id=(*chip, c), device_id_type=MESH)
            for j, chip in enumerate(_other_chips(x, y)) for a in range(n)]
        for cp in cps:
            cp.start()
        for cp in cps:
            cp.wait()

    return pl.pallas_call(
        body, name=name, in_specs=[ANY] * n, out_specs=[ANY] * n,
        out_shape=[jax.ShapeDtypeStruct((3,) + a.shape[1:], a.dtype) for a in arrs],
        scratch_shapes=[pltpu.SemaphoreType.DMA((3 * n,)), pltpu.SemaphoreType.DMA((3 * n,))],
    )(*arrs)


def _join_halves(fulls, *, name):
    n = len(fulls)

    def body(*refs):
        ins, outs = refs[:n], refs[n:2 * n]
        send, recv = refs[2 * n:]
        x, y, c = _position()

        def copy(a, rows):
            return pltpu.make_async_remote_copy(
                src_ref=ins[a].at[rows], dst_ref=outs[a].at[rows], send_sem=send.at[a], recv_sem=recv.at[a],
                device_id=(x, y, 1 - c), device_id_type=MESH)

        cps = [copy(a, pl.ds(2 * c, 2)) for a in range(n)]
        for cp in cps:
            cp.start()
        for a in range(n):
            cps[a].wait_send()
            copy(a, pl.ds(2 * (1 - c), 2)).wait_recv()

    return pl.pallas_call(
        body, name=name, in_specs=[ANY] * n, out_specs=[ANY] * n,
        out_shape=[jax.ShapeDtypeStruct(f.shape, f.dtype) for f in fulls],
        scratch_shapes=[pltpu.SemaphoreType.DMA((n,)), pltpu.SemaphoreType.DMA((n,))],
        input_output_aliases={a: a for a in range(n)},
    )(*fulls)


def _allreduce_small(packed, *, name):
    rows = packed.shape[0]
    ndev = 8

    def body(x_ref, sum_ref, all_ref, send, recv, lsem):
        x, y, c = _position()
        me, sib = (x, y, c), (x, y, 1 - c)
        chips = _other_chips(x, y)

        def blk(px, py, pc):
            return all_ref.at[pl.ds((4 * px + 2 * py + pc) * rows, rows), :]

        def copy(k, block, to, src=None):
            return pltpu.make_async_remote_copy(
                src_ref=blk(*block) if src is None else src, dst_ref=blk(*block), send_sem=send.at[k],
                recv_sem=recv.at[k], device_id=to, device_id_type=MESH)

        mine = pltpu.make_async_copy(x_ref, blk(*me), lsem)
        mine.start()
        first = [copy(0, me, sib, src=x_ref)] + [copy(1 + j, me, (*chip, c), src=x_ref) for j, chip in enumerate(chips)]
        for cp in first:
            cp.start()
        passed = [copy(4 + j, (*chip, c), sib) for j, chip in enumerate(chips)]
        for j, chip in enumerate(chips):
            copy(1 + j, (*chip, c), me).wait_recv()
            passed[j].start()
        copy(0, sib, me).wait_recv()
        for j, chip in enumerate(chips):
            copy(4 + j, (*chip, 1 - c), me).wait_recv()
        for cp in first + passed:
            cp.wait_send()
        mine.wait()
        acc = all_ref[0:rows, :]
        for d in range(1, ndev):
            acc = acc + all_ref[d * rows:(d + 1) * rows, :]
        sum_ref[...] = acc

    vm = pl.BlockSpec(memory_space=pltpu.VMEM)
    return pl.pallas_call(
        body, name=name, in_specs=[vm], out_specs=vm, out_shape=jax.ShapeDtypeStruct((rows, 128), F32),
        scratch_shapes=[pltpu.VMEM((ndev * rows, 128), F32), pltpu.SemaphoreType.DMA((7,)),
                        pltpu.SemaphoreType.DMA((7,)), pltpu.SemaphoreType.DMA],
        compiler_params=pltpu.CompilerParams(vmem_limit_bytes=VMEM_LIMIT),
    )(packed)


def _pack(arrs):
    flat = jnp.concatenate([a.reshape(-1) for a in arrs])
    pad = (-flat.shape[0]) % 1024
    return jnp.pad(flat, (0, pad)).reshape(-1, 128)


def _unpack(packed, shapes):
    flat = packed.reshape(-1)
    out, pos = [], 0
    for s in shapes:
        size = math.prod(s)
        out.append(flat[pos:pos + size].reshape(s))
        pos += size
    return out


def _pad_cols(w):
    zeros = jnp.zeros(w.shape[:-1] + (IN_WP - IN_W,), w.dtype)
    return jnp.concatenate([w[..., :ORIG_BA], w[..., ORIG_ATT:], w[..., ORIG_BA:ORIG_ATT], zeros], axis=-1)


def _chip_cols(j):
    per = IN_W // 4
    lo, hi = j * per, (j + 1) * per
    out = []
    for o0, o1, p0 in ((0, ORIG_BA, 0), (ORIG_BA, ORIG_ATT, COL_BA), (ORIG_ATT, IN_W, ORIG_BA)):
        a, b = max(lo, o0), min(hi, o1)
        if a < b:
            out.append((p0 + a - o0, p0 + b - o0))
    return out


def _shards_to_padded(g):
    pieces = []
    for j in range(4):
        loc = 0
        for p0, p1 in _chip_cols(j):
            pieces.append((p0, g[j][..., loc:loc + p1 - p0]))
            loc += p1 - p0
    pieces.sort(key=lambda t: t[0])
    zeros = jnp.zeros(g.shape[1:-1] + (IN_WP - IN_W,), g.dtype)
    return jnp.concatenate([p for _, p in pieces] + [zeros], axis=-1)


def _padded_to_shards(g, dtype):
    return jnp.stack([jnp.concatenate([g[..., p0:p1] for p0, p1 in _chip_cols(j)], axis=-1).astype(dtype)
                      for j in range(4)])


def _unpad_cols(w):
    n_att = IN_W - ORIG_ATT
    return jnp.concatenate([w[..., :ORIG_BA], w[..., COL_BA:COL_BA + ORIG_ATT - ORIG_BA],
                            w[..., ORIG_BA:ORIG_BA + n_att]], axis=-1)


def _lanes(v, first):
    return jnp.pad(v, ((0, 0), (first, 128 - first - v.shape[1])))[:, None, :]


def _by_chip(g, axis):
    shape = g.shape[:axis] + (4, g.shape[axis] // 4) + g.shape[axis + 1:]
    return jnp.moveaxis(g.reshape(shape), axis, 0)


def kernel(x, norm_w, w_in, conv_qkv_w, a_log, dt_bias, gdn_norm_w, conf_dw_w, conf_dw_b, conf_ln_w, conf_ln_b, conf_pw_w, w_out, final_norm_w, loss_target, m_norm_w, m_w_in, m_conv_qkv_w, m_a_log, m_dt_bias, m_gdn_norm_w, m_conf_dw_w, m_conf_dw_b, m_conf_ln_w, m_conf_ln_b, m_conf_pw_w, m_w_out, m_final_norm_w, v_norm_w, v_w_in, v_conv_qkv_w, v_a_log, v_dt_bias, v_gdn_norm_w, v_conf_dw_w, v_conf_dw_b, v_conf_ln_w, v_conf_ln_b, v_conf_pw_w, v_w_out, v_final_norm_w):
    xi, yi, ci = _position()
    chip = 2 * xi + yi

    shards = [w_in.astype(BF16), w_out.astype(BF16), conf_pw_w.astype(BF16), conv_qkv_w, conf_dw_w]
    g_in, g_out, g_pw, g_cw, g_dw = [
        lax.dynamic_update_slice_in_dim(g, s[None], chip, axis=0)
        for g, s in zip(_gather_chips(shards, name="gather_weights"), shards)]
    cw_full = jnp.moveaxis(g_cw, 0, 2).reshape(DEPTH, SHORT_CONV, 3 * GDN_W)
    dw_full = jnp.moveaxis(g_dw, 0, 2).reshape(DEPTH, CONV_WIDTH, CONV_CH)
    params = dict(
        norm_w=norm_w[:, None, :],
        wp=_shards_to_padded(g_in),
        wout=jnp.moveaxis(g_out, 0, 1).reshape(DEPTH, D_MODEL, D_MODEL),
        pw_w=jnp.moveaxis(g_pw, 0, 1).reshape(DEPTH, CONV_CH, CONV_CH),
        cw=jnp.pad(cw_full, ((0, 0), (0, SHALO - SHORT_CONV), (0, 0))),
        dw_w=jnp.pad(dw_full, ((0, 0), (0, HALO - CONV_WIDTH), (0, 0))),
        al=_lanes(a_log, GDN_HEADS), dtb=_lanes(dt_bias, GDN_HEADS), gnw=gdn_norm_w[:, None, :],
        dw_b=conf_dw_b[:, None, :], ln_w=conf_ln_w[:, None, :], ln_b=conf_ln_b[:, None, :],
    )

    loss_part, grad_x, grads, d_final = _trunk(x[0], loss_target[0], params, final_norm_w[None, :])

    def half_by_chip(first, dtype):
        wp, wout, pw = [lax.dynamic_slice_in_dim(grads[k], first, 2, axis=0) for k in ("wp", "wout", "pw_w")]
        return [_padded_to_shards(wp, dtype), _by_chip(wout, 1).astype(dtype), _by_chip(pw, 1).astype(dtype)]

    keep = half_by_chip(2 * ci, F32)
    give = half_by_chip(2 * (1 - ci), BF16)
    got = _to_sibling(give, name="grads_to_sibling")
    pair = [_sum_arrays([k.reshape((8,) + k.shape[2:]), r.reshape((8,) + r.shape[2:])], name=f"pair_sum_{i}",
                        out_dtype=BF16).reshape(k.shape) for i, (k, r) in enumerate(zip(keep, got))]
    arrived = _to_chips(pair, name="grads_to_chips")
    halves = []
    for i, (pr, ar) in enumerate(zip(pair, arrived)):
        own = lax.dynamic_index_in_dim(pr, chip, axis=0, keepdims=False)
        halves.append(_sum_into_half([own, ar[0], ar[1], ar[2]], ci, name=f"chip_sum_{i}"))
    g_w_in, g_w_out, g_pw_w = _join_halves(halves, name="join_halves")

    ba = grads["ba_sums"]
    small = [grads["norm_w"], ba[:, 0:1, :], ba[:, 1:2, :], grads["gnw"], grads["ln_sums"][:, 2:3, :],
             grads["ln_sums"][:, 0:1, :], grads["ln_sums"][:, 1:2, :], d_final,
             grads["cw"][:, :SHORT_CONV, :], grads["dw_w"][:, :CONV_WIDTH, :], loss_part.reshape(1)]
    red = _unpack(_allreduce_small(_pack(small), name="allreduce_small"), [s.shape for s in small])
    loss = red[10][0]
    g_norm_w = red[0][:, 0, :]
    g_a_log = red[1][:, 0, GDN_HEADS:2 * GDN_HEADS]
    g_dt_bias = red[2][:, 0, GDN_HEADS:2 * GDN_HEADS]
    g_gnw, g_dw_b, g_ln_w, g_ln_b = red[3][:, 0, :], red[4][:, 0, :], red[5][:, 0, :], red[6][:, 0, :]
    g_final = red[7][0]
    g_cw = lax.dynamic_slice_in_dim(red[8], chip * (3 * GDN_W // 4), 3 * GDN_W // 4, axis=2)
    g_dw_w = lax.dynamic_slice_in_dim(red[9], chip * (CONV_CH // 4), CONV_CH // 4, axis=2)

    def cols_first(a):
        return jnp.transpose(a, (2, 0, 1))

    def cols_last(a):
        return jnp.transpose(a, (1, 2, 0))

    g_t = cols_first(g_w_in)
    g_w_in = cols_last(g_t)
    d_w_in, nm_w_in, nv_w_in = [cols_last(a) for a in _adamw(
        cols_first(w_in), g_t, cols_first(m_w_in), cols_first(v_w_in), name="adamw_w_in", by_lead=True)]
    d_w_out, nm_w_out, nv_w_out = _adamw(w_out, g_w_out, m_w_out, v_w_out, name="adamw_w_out")
    d_pw_w, nm_pw_w, nv_pw_w = _adamw(conf_pw_w, g_pw_w, m_conf_pw_w, v_conf_pw_w, name="adamw_pw")
    sw = [norm_w, a_log, dt_bias, gdn_norm_w, conf_dw_b, conf_ln_w, conf_ln_b, final_norm_w, conv_qkv_w, conf_dw_w]
    sg = [g_norm_w, g_a_log, g_dt_bias, g_gnw, g_dw_b, g_ln_w, g_ln_b, g_final, g_cw, g_dw_w]
    sm = [m_norm_w, m_a_log, m_dt_bias, m_gdn_norm_w, m_conf_dw_b, m_conf_ln_w, m_conf_ln_b, m_final_norm_w,
          m_conv_qkv_w, m_conf_dw_w]
    sv = [v_norm_w, v_a_log, v_dt_bias, v_gdn_norm_w, v_conf_dw_b, v_conf_ln_w, v_conf_ln_b, v_final_norm_w,
          v_conv_qkv_w, v_conf_dw_w]
    shapes = [a.shape for a in sw]
    packed = _adamw(_pack(sw)[None], _pack(sg)[None], _pack(sm)[None], _pack(sv)[None], name="adamw_small")
    sd, snm, snv = [_unpack(pk[0], shapes) for pk in packed]

    def order(big3, small10):
        s = small10
        return [s[0], big3[0], s[8], s[1], s[2], s[3], s[9], s[4], s[5], s[6], big3[2], big3[1], s[7]]

    return (loss, grad_x[None], *order([g_w_in, g_w_out, g_pw_w], sg),
            *order([d_w_in, d_w_out, d_pw_w], sd), *order([nm_w_in, nm_w_out, nm_pw_w], snm),
            *order([nv_w_in, nv_w_out, nv_pw_w], snv))
```

```python
import functools
import math

import jax
import jax.numpy as jnp
from jax import lax
from jax.experimental import pallas as pl
from jax.experimental.pallas import tpu as pltpu

F32, BF16 = jnp.float32, jnp.bfloat16
HIGHEST = lax.Precision.HIGHEST
MESH = pl.DeviceIdType.MESH

D_MODEL = 2048
DEPTH = 4
CONV_CH = 512
GDN_W = 768
GDN_HEADS = 6
GDN_D = 128
ATT_W = 768
ATT_HEADS = 12
ATT_HD = 64
CONV_WIDTH = 31
SHORT_CONV = 4
GDN_CHUNK = 64
ROPE_THETA = 500000.0
ROPE_DIM = ATT_HD // 4
DIL_PATTERNS = ((128, 1), (512, 4), (2048, 16))
ATT_BLOCK = 128
NEG_INF = -1e30
IN_W = 7692

IN_WP = 8192
COL_BA = 7680
ORIG_BA = 4608
ORIG_ATT = 4620

ADAM_LR = 0.001
ADAM_B1 = 0.9
ADAM_B2 = 0.999
ADAM_EPS = 1e-08
ADAM_WD = 0.01
ADAM_STEP = 10

VMEM_LIMIT = 56 * 1024 * 1024


def _params(sem=None):
    return pltpu.CompilerParams(dimension_semantics=sem, vmem_limit_bytes=VMEM_LIMIT)


def _sigmoid(x):
    return 0.5 * jnp.tanh(0.5 * x) + 0.5


def _silu(x):
    return x * _sigmoid(x)


def _dsilu(x):
    s = _sigmoid(x)
    return s * (1.0 + x * (1.0 - s))


def _dot(a, b, dims, precision=None):
    return lax.dot_general(a, b, (dims, ((), ())), precision=precision, preferred_element_type=F32)


def _nn(a, b, precision=None):
    return _dot(a, b, ((1,), (0,)), precision)


def _nt(a, b, precision=None):
    return _dot(a, b, ((1,), (1,)), precision)


def _tn(a, b, precision=None):
    return _dot(a, b, ((0,), (0,)), precision)


def _matmul(a, b, *, name, ta=False, tb=False, out_dtype=F32, add=None, stack=None, tm=1024, tn=1024, tk=1024):
    if ta:
        k_dim, m_dim = a.shape
    else:
        m_dim, k_dim = a.shape
    n_dim = b.shape[0] if tb else b.shape[1]
    tm, tn, tk = min(tm, m_dim), min(tn, n_dim), min(tk, k_dim)
    assert m_dim % tm == 0 and n_dim % tn == 0 and k_dim % tk == 0, (name, a.shape, b.shape)
    nk = k_dim // tk
    a_spec = pl.BlockSpec((tk, tm), lambda i, j, k: (k, i)) if ta else pl.BlockSpec((tm, tk), lambda i, j, k: (i, k))
    b_spec = pl.BlockSpec((tn, tk), lambda i, j, k: (j, k)) if tb else pl.BlockSpec((tk, tn), lambda i, j, k: (k, j))
    o_spec = pl.BlockSpec((tm, tn), lambda i, j, k: (i, j))
    out_shape = jax.ShapeDtypeStruct((m_dim, n_dim), out_dtype)
    dims = ((0 if ta else 1,), (1 if tb else 0,))
    has_add = add is not None
    ins = [a, b] + ([add] if has_add else [])
    specs = [a_spec, b_spec] + ([o_spec] if has_add else [])
    aliases = {}
    if stack is not None:
        buf, slab, nslabs = stack
        o_spec = pl.BlockSpec((None, tm, tn), lambda i, j, k: (slab, i, j))
        out_shape = jax.ShapeDtypeStruct((nslabs, m_dim, n_dim), out_dtype)
        if buf is not None:
            aliases = {len(ins): 0}
            ins.append(buf)
            specs.append(pl.BlockSpec(memory_space=pl.ANY))
    n_in = len(ins)

    def body(*refs):
        a_ref, b_ref = refs[0], refs[1]
        o_ref = refs[n_in]

        def finish(r):
            if has_add:
                r = r + refs[2][...]
            o_ref[...] = r.astype(out_dtype)

        prod = _dot(a_ref[...].astype(BF16), b_ref[...].astype(BF16), dims)
        if nk == 1:
            finish(prod)
            return
        acc_ref = refs[n_in + 1]
        k = pl.program_id(2)

        @pl.when(k == 0)
        def _():
            acc_ref[...] = prod

        @pl.when(k > 0)
        def _():
            acc_ref[...] += prod

        @pl.when(k == nk - 1)
        def _():
            finish(acc_ref[...])

    return pl.pallas_call(
        body, name=name, grid=(m_dim // tm, n_dim // tn, nk), in_specs=specs, out_specs=o_spec,
        out_shape=out_shape, scratch_shapes=[pltpu.VMEM((tm, tn), F32)] if nk > 1 else [],
        input_output_aliases=aliases,
        compiler_params=_params(("parallel", "parallel", "arbitrary")),
    )(*ins)


def _rms_fwd(x, w, *, name, tm=512):
    s_len, d = x.shape

    def body(x_ref, w_ref, h_ref):
        xv = x_ref[...]
        r = lax.rsqrt(jnp.mean(xv * xv, axis=-1, keepdims=True) + 1e-6)
        h_ref[...] = (xv * r * w_ref[...]).astype(BF16)

    return pl.pallas_call(
        body, name=name, grid=(s_len // tm,),
        in_specs=[pl.BlockSpec((tm, d), lambda i: (i, 0)), pl.BlockSpec((1, d), lambda i: (0, 0))],
        out_specs=pl.BlockSpec((tm, d), lambda i: (i, 0)),
        out_shape=jax.ShapeDtypeStruct((s_len, d), BF16),
        compiler_params=_params(("parallel",)),
    )(x, w)


def _rms_bwd(x, dh, w, dres, *, name, tm=512):
    s_len, d = x.shape
    nsteps = s_len // tm

    def body(x_ref, dh_ref, w_ref, dres_ref, dx_ref, dw_ref, acc_ref):
        i = pl.program_id(0)

        @pl.when(i == 0)
        def _():
            acc_ref[...] = jnp.zeros_like(acc_ref)

        xv = x_ref[...]
        r = lax.rsqrt(jnp.mean(xv * xv, axis=-1, keepdims=True) + 1e-6)
        xn = xv * r
        dy = dh_ref[...]
        dxn = dy * w_ref[...]
        dx_ref[...] = dres_ref[...] + r * (dxn - xn * jnp.mean(dxn * xn, axis=-1, keepdims=True))
        acc_ref[...] += (dy * xn).reshape(tm // 8, 8, d).sum(axis=0)

        @pl.when(i == nsteps - 1)
        def _():
            dw_ref[...] = jnp.sum(acc_ref[...], axis=0, keepdims=True)

    row = pl.BlockSpec((tm, d), lambda i: (i, 0))
    vec = pl.BlockSpec((1, d), lambda i: (0, 0))
    return pl.pallas_call(
        body, name=name, grid=(nsteps,), in_specs=[row, row, vec, row], out_specs=[row, vec],
        out_shape=[jax.ShapeDtypeStruct((s_len, d), F32), jax.ShapeDtypeStruct((1, d), F32)],
        scratch_shapes=[pltpu.VMEM((8, d), F32)],
        compiler_params=_params(("arbitrary",)),
    )(x, dh, w, dres)


def _loss_head(x, w, target, *, name, tm=256):
    s_len, d = x.shape
    nsteps = s_len // tm

    def body(x_ref, w_ref, t_ref, dx_ref, dw_ref, loss_ref, acc_ref, lacc_ref):
        i = pl.program_id(0)

        @pl.when(i == 0)
        def _():
            acc_ref[...] = jnp.zeros_like(acc_ref)
            lacc_ref[...] = jnp.zeros_like(lacc_ref)

        xv = x_ref[...]
        wv = w_ref[...]
        r = lax.rsqrt(jnp.mean(xv * xv, axis=-1, keepdims=True) + 1e-6)
        xn = xv * r
        err = xn * wv - t_ref[...]
        lacc_ref[...] += (err * err).reshape(tm // 8, 8, d).sum(axis=0)
        dy = err * (1.0 / d)
        dxn = dy * wv
        dx_ref[...] = r * (dxn - xn * jnp.mean(dxn * xn, axis=-1, keepdims=True))
        acc_ref[...] += (dy * xn).reshape(tm // 8, 8, d).sum(axis=0)

        @pl.when(i == nsteps - 1)
        def _():
            dw_ref[...] = jnp.sum(acc_ref[...], axis=0, keepdims=True)
            tot = jnp.sum(jnp.sum(lacc_ref[...], axis=0, keepdims=True), axis=1, keepdims=True)
            loss_ref[...] = jnp.broadcast_to(tot * (0.5 / d), (1, 128))

    row = pl.BlockSpec((tm, d), lambda i: (i, 0))
    vec = pl.BlockSpec((1, d), lambda i: (0, 0))
    return pl.pallas_call(
        body, name=name, grid=(nsteps,), in_specs=[row, vec, row],
        out_specs=[row, vec, pl.BlockSpec((1, 128), lambda i: (0, 0))],
        out_shape=[jax.ShapeDtypeStruct((s_len, d), F32), jax.ShapeDtypeStruct((1, d), F32),
                   jax.ShapeDtypeStruct((1, 128), F32)],
        scratch_shapes=[pltpu.VMEM((8, d), F32), pltpu.VMEM((8, d), F32)],
        compiler_params=_params(("arbitrary",)),
    )(x, w, target)


def _rows_block(shape, tr=256):
    lead, rows, cols = shape
    if rows % tr != 0:
        assert rows * cols <= 1 << 20, shape
        tr = rows
    return (lead, rows // tr), pl.BlockSpec((1, tr, cols), lambda a, i: (a, i, 0))


LEAD_BLOCK = 64


def _adamw(w, g, m, v, *, name, by_lead=False):
    if by_lead:
        lead, rows, cols = w.shape
        grid = (pl.cdiv(lead, LEAD_BLOCK), 1)
        spec = pl.BlockSpec((LEAD_BLOCK, rows, cols), lambda a, i: (a, 0, 0))
    else:
        grid, spec = _rows_block(w.shape)
    c1 = 1.0 / (1.0 - ADAM_B1 ** ADAM_STEP)
    c2 = 1.0 / (1.0 - ADAM_B2 ** ADAM_STEP)

    def body(w_ref, g_ref, m_ref, v_ref, d_ref, nm_ref, nv_ref):
        gv = g_ref[...]
        nm = ADAM_B1 * m_ref[...] + (1.0 - ADAM_B1) * gv
        nv = ADAM_B2 * v_ref[...] + (1.0 - ADAM_B2) * (gv * gv)
        nm_ref[...] = nm
        nv_ref[...] = nv
        d_ref[...] = -ADAM_LR * ((nm * c1) / (jnp.sqrt(nv * c2) + ADAM_EPS) + ADAM_WD * w_ref[...])

    out = jax.ShapeDtypeStruct(w.shape, F32)
    return pl.pallas_call(
        body, name=name, grid=grid, in_specs=[spec] * 4, out_specs=[spec] * 3, out_shape=[out] * 3,
        compiler_params=_params(("parallel", "parallel")),
    )(w, g, m, v)


def _sum_into_half(arrs, half, *, name):
    lead, rows, cols = arrs[0].shape
    assert lead == 2
    (_, nr), spec0 = _rows_block(arrs[0].shape)
    tr = spec0.block_shape[1]
    n = len(arrs)

    def body(half_ref, *refs):
        del half_ref
        acc = refs[0][...].astype(F32)
        for r in refs[1:n]:
            acc = acc + r[...].astype(F32)
        refs[n][...] = acc

    spec = pl.BlockSpec((1, tr, cols), lambda a, i, h: (a, i, 0))
    return pl.pallas_call(
        body, name=name,
        grid_spec=pltpu.PrefetchScalarGridSpec(
            num_scalar_prefetch=1, grid=(2, nr), in_specs=[spec] * n,
            out_specs=pl.BlockSpec((1, tr, cols), lambda a, i, h: (2 * h[0] + a, i, 0))),
        out_shape=jax.ShapeDtypeStruct((4, rows, cols), F32),
        compiler_params=_params(("parallel", "parallel")),
    )(jnp.reshape(half, (1,)).astype(jnp.int32), *arrs)


def _sum_arrays(arrs, *, name, out_dtype):
    grid, spec = _rows_block(arrs[0].shape)
    n = len(arrs)

    def body(*refs):
        acc = refs[0][...].astype(F32)
        for r in refs[1:n]:
            acc = acc + r[...].astype(F32)
        refs[n][...] = acc.astype(out_dtype)

    return pl.pallas_call(
        body, name=name, grid=grid, in_specs=[spec] * n, out_specs=spec,
        out_shape=jax.ShapeDtypeStruct(arrs[0].shape, out_dtype),
        compiler_params=_params(("parallel", "parallel")),
    )(*arrs)


HALO = 32


def _shifted_windows(buf, tm, offsets):
    rows = buf.shape[0]
    for b in range(8):
        group = [(k, s) for k, s in enumerate(offsets) if s % 8 == b]
        if not group:
            continue
        rb = buf if b == 0 else pltpu.roll(buf, rows - b, 0)
        for k, s in group:
            yield k, rb[s - b:s - b + tm, :]


def _conf_fwd(u, dw_w, dw_b, ln_w, ln_b, *, name, tm=256):
    s_len = u.shape[0]
    c = CONV_CH

    def body(uc_ref, up_ref, dww_ref, dwb_ref, lnw_ref, lnb_ref, conv_ref, sw_ref, hbuf):
        i = pl.program_id(0)
        hbuf[HALO:, :] = uc_ref[:, :c] * _sigmoid(uc_ref[:, c:])
        hp = up_ref[:, :c] * _sigmoid(up_ref[:, c:])
        hbuf[:HALO, :] = jnp.where(i > 0, hp, 0.0)
        for cb in range(c // 128):
            cs = slice(128 * cb, 128 * (cb + 1))
            acc = jnp.zeros((tm, 128), F32)
            taps = [HALO - CONV_WIDTH + 1 + j for j in range(CONV_WIDTH)]
            for j, win in _shifted_windows(hbuf[:, cs], tm, taps):
                acc = acc + win * dww_ref[j:j + 1, cs]
            conv_ref[:, cs] = acc + dwb_ref[:, cs]
        cv = conv_ref[...]
        mu = jnp.mean(cv, axis=-1, keepdims=True)
        xc = cv - mu
        var = jnp.mean(xc * xc, axis=-1, keepdims=True)
        ln = xc * lax.rsqrt(var + 1e-5) * lnw_ref[...] + lnb_ref[...]
        sw_ref[...] = _silu(ln).astype(BF16)

    vec = pl.BlockSpec((1, c), lambda i: (0, 0))
    return pl.pallas_call(
        body, name=name, grid=(s_len // tm,),
        in_specs=[pl.BlockSpec((tm, 2 * c), lambda i: (i, 0)),
                  pl.BlockSpec((HALO, 2 * c), lambda i: (jnp.maximum(i * (tm // HALO) - 1, 0), 0)),
                  pl.BlockSpec((HALO, c), lambda i: (0, 0)), vec, vec, vec],
        out_specs=[pl.BlockSpec((tm, c), lambda i: (i, 0))] * 2,
        out_shape=[jax.ShapeDtypeStruct((s_len, c), F32), jax.ShapeDtypeStruct((s_len, c), BF16)],
        scratch_shapes=[pltpu.VMEM((tm + HALO, c), F32)],
        compiler_params=_params(("parallel",)),
    )(u, u, dw_w, dw_b, ln_w, ln_b)


def _conf_bwd_ln(d_sw, conv, ln_w, ln_b, *, name, tm=256):
    s_len, c = conv.shape
    nsteps = s_len // tm

    def body(dsw_ref, conv_ref, lnw_ref, lnb_ref, dconv_ref, sums_ref):
        i = pl.program_id(0)

        @pl.when(i == 0)
        def _():
            sums_ref[...] = jnp.zeros_like(sums_ref)

        cv = conv_ref[...]
        mu = jnp.mean(cv, axis=-1, keepdims=True)
        xc = cv - mu
        rs = lax.rsqrt(jnp.mean(xc * xc, axis=-1, keepdims=True) + 1e-5)
        xhat = xc * rs
        lnw = lnw_ref[...]
        ln = xhat * lnw + lnb_ref[...]
        dln = dsw_ref[...] * _dsilu(ln)
        dxh = dln * lnw
        dconv = rs * (dxh - jnp.mean(dxh, axis=-1, keepdims=True)
                      - xhat * jnp.mean(dxh * xhat, axis=-1, keepdims=True))
        dconv_ref[...] = dconv
        sums_ref[0:1, :] += jnp.sum(dln * xhat, axis=0, keepdims=True)
        sums_ref[1:2, :] += jnp.sum(dln, axis=0, keepdims=True)
        sums_ref[2:3, :] += jnp.sum(dconv, axis=0, keepdims=True)

    row = pl.BlockSpec((tm, c), lambda i: (i, 0))
    vec = pl.BlockSpec((1, c), lambda i: (0, 0))
    return pl.pallas_call(
        body, name=name, grid=(nsteps,), in_specs=[row, row, vec, vec],
        out_specs=[row, pl.BlockSpec((8, c), lambda i: (0, 0))],
        out_shape=[jax.ShapeDtypeStruct((s_len, c), F32), jax.ShapeDtypeStruct((8, c), F32)],
        compiler_params=_params(("arbitrary",)),
    )(d_sw, conv, ln_w, ln_b)


def _conf_bwd_conv(u, dconv, dw_w, du, *, name, tm=256):
    s_len = u.shape[0]
    c = CONV_CH
    nsteps = s_len // tm
    off = HALO - CONV_WIDTH + 1

    def body(uc_ref, up_ref, dc_ref, dn_ref, dww_ref, du_in_ref, du_ref, ddw_ref, hbuf, dbuf, wacc):
        del du_in_ref
        i = pl.program_id(0)

        @pl.when(i == 0)
        def _():
            wacc[...] = jnp.zeros_like(wacc)

        hbuf[HALO:, :] = uc_ref[:, :c] * _sigmoid(uc_ref[:, c:])
        hp = up_ref[:, :c] * _sigmoid(up_ref[:, c:])
        hbuf[:HALO, :] = jnp.where(i > 0, hp, 0.0)
        dbuf[:tm, :] = dc_ref[...]
        dbuf[tm:, :] = jnp.where(i < nsteps - 1, dn_ref[...], 0.0)
        for cb in range(c // 128):
            cs = slice(128 * cb, 128 * (cb + 1))
            dcur = dbuf[0:tm, cs]
            acc = jnp.zeros((tm, 128), F32)
            for k, win in _shifted_windows(dbuf[:, cs], tm, list(range(CONV_WIDTH))):
                j = CONV_WIDTH - 1 - k
                acc = acc + win * dww_ref[j:j + 1, cs]
            for j, win in _shifted_windows(hbuf[:, cs], tm, [off + j for j in range(CONV_WIDTH)]):
                wacc[j, :, cs] += (win * dcur).reshape(tm // 8, 8, 128).sum(axis=0)
            a = uc_ref[:, cs]
            sg = _sigmoid(uc_ref[:, c + 128 * cb:c + 128 * (cb + 1)])
            du_ref[:, cs] = (acc * sg).astype(du_ref.dtype)
            du_ref[:, c + 128 * cb:c + 128 * (cb + 1)] = (acc * a * sg * (1.0 - sg)).astype(du_ref.dtype)

        @pl.when(i == nsteps - 1)
        def _():
            for j in range(CONV_WIDTH):
                ddw_ref[j:j + 1, :] = jnp.sum(wacc[j], axis=0, keepdims=True)
            ddw_ref[CONV_WIDTH:, :] = jnp.zeros((HALO - CONV_WIDTH, c), F32)

    return pl.pallas_call(
        body, name=name, grid=(nsteps,),
        in_specs=[pl.BlockSpec((tm, 2 * c), lambda i: (i, 0)),
                  pl.BlockSpec((HALO, 2 * c), lambda i: (jnp.maximum(i * (tm // HALO) - 1, 0), 0)),
                  pl.BlockSpec((tm, c), lambda i: (i, 0)),
                  pl.BlockSpec((HALO, c), lambda i: (jnp.minimum((i + 1) * (tm // HALO), s_len // HALO - 1), 0)),
                  pl.BlockSpec((HALO, c), lambda i: (0, 0)),
                  pl.BlockSpec(memory_space=pl.ANY)],
        out_specs=[pl.BlockSpec((tm, 2 * c), lambda i: (i, 0)), pl.BlockSpec((HALO, c), lambda i: (0, 0))],
        out_shape=[jax.ShapeDtypeStruct(du.shape, du.dtype), jax.ShapeDtypeStruct((HALO, c), F32)],
        scratch_shapes=[pltpu.VMEM((tm + HALO, c), F32), pltpu.VMEM((tm + HALO, c), F32),
                        pltpu.VMEM((CONV_WIDTH, 8, c), F32)],
        input_output_aliases={5: 0},
        compiler_params=_params(("arbitrary",)),
    )(u, u, dconv, dconv, dw_w, du)


COL_GQ = 1536 // GDN_W
COL_AQ = 4608 // ATT_W
SHALO = 8
SCAN_CHUNKS = 4
INTRA_CHUNKS = 4


def _softplus(z):
    return jnp.maximum(z, 0.0) + jnp.log1p(jnp.exp(-jnp.abs(z)))


def _short_conv(buf, cw_ref, part, rows, first):
    acc = jnp.zeros((rows, GDN_W), F32)
    for j, win in _shifted_windows(buf[...], rows, [first + j for j in range(SHORT_CONV)]):
        acc = acc + win * cw_ref[j:j + 1, GDN_W * part:GDN_W * (part + 1)]
    return acc


def _gdn_prep_fwd(u, cw, al, dtb, *, name, tm=256):
    s_len = u.shape[0]
    first = SHALO - SHORT_CONV + 1

    def body(uq, uk, uv, pq, pk, pv, uba, cw_ref, al_ref, dtb_ref, qn_ref, kn_ref, vc_ref, bg_ref, buf):
        i = pl.program_id(0)

        def conv(cur, prev, part):
            buf[SHALO:, :] = cur[...]
            buf[:SHALO, :] = jnp.where(i > 0, prev[...], 0.0)
            return _silu(_short_conv(buf, cw_ref, part, tm, first))

        for part, (cur, prev, out, scale) in enumerate(
                ((uq, pq, qn_ref, GDN_D ** -0.5), (uk, pk, kn_ref, 1.0))):
            y = conv(cur, prev, part)
            for h in range(GDN_HEADS):
                hs = slice(GDN_D * h, GDN_D * (h + 1))
                yh = y[:, hs]
                out[:, hs] = yh * (lax.rsqrt(jnp.sum(yh * yh, axis=-1, keepdims=True) + 1e-6) * scale)
        vc_ref[...] = conv(uv, pv, 2)
        ba = uba[...]
        lane = lax.broadcasted_iota(jnp.int32, ba.shape, 1)
        g = -jnp.exp(al_ref[...]) * _softplus(ba + dtb_ref[...])
        bg_ref[...] = jnp.where(lane < GDN_HEADS, _sigmoid(ba), jnp.where(lane < 2 * GDN_HEADS, g, 0.0))

    def cur(col):
        return pl.BlockSpec((tm, GDN_W), lambda i: (i, col))

    def prev(col):
        return pl.BlockSpec((SHALO, GDN_W), lambda i: (jnp.maximum(i * (tm // SHALO) - 1, 0), col))

    vec = pl.BlockSpec((1, 128), lambda i: (0, 0))
    row = pl.BlockSpec((tm, GDN_W), lambda i: (i, 0))
    wide = jax.ShapeDtypeStruct((s_len, GDN_W), F32)
    return pl.pallas_call(
        body, name=name, grid=(s_len // tm,),
        in_specs=[cur(COL_GQ), cur(COL_GQ + 1), cur(COL_GQ + 2), prev(COL_GQ), prev(COL_GQ + 1), prev(COL_GQ + 2),
                  pl.BlockSpec((tm, 128), lambda i: (i, COL_BA // 128)),
                  pl.BlockSpec((SHALO, 3 * GDN_W), lambda i: (0, 0)), vec, vec],
        out_specs=[row, row, row, pl.BlockSpec((tm, 128), lambda i: (i, 0))],
        out_shape=[wide, wide, wide, jax.ShapeDtypeStruct((s_len, 128), F32)],
        scratch_shapes=[pltpu.VMEM((tm + SHALO, GDN_W), F32)],
        compiler_params=_params(("parallel",)),
    )(u, u, u, u, u, u, u, cw, al, dtb)


def _chunk_masks():
    c = GDN_CHUNK
    row = lax.broadcasted_iota(jnp.int32, (c, c), 0)
    col = lax.broadcasted_iota(jnp.int32, (c, c), 1)
    return row >= col, row > col


def _cum_decay(bg):
    c = GDN_CHUNK
    causal, _ = _chunk_masks()
    g_cum = _nn(causal.astype(F32), bg, HIGHEST)
    sel = (lax.broadcasted_iota(jnp.int32, (8, 128), 0) + GDN_HEADS
           == lax.broadcasted_iota(jnp.int32, (8, 128), 1)).astype(F32)
    return g_cum, _nt(sel, g_cum, HIGHEST)


def _bdot(a, b, ca, cb):
    return lax.dot_general(a, b, (((ca,), (cb,)), ((0,), (0,))), preferred_element_type=F32)


def _bnn(a, b):
    return _bdot(a, b, 2, 1)


def _bnt(a, b):
    return _bdot(a, b, 2, 2)


def _btn(a, b):
    return _bdot(a, b, 1, 1)


def _split(a):
    hi = a.astype(BF16)
    return hi, (a - hi.astype(F32)).astype(BF16)


def _bnn3(a, b):
    ah, al = _split(a)
    bh, bl = _split(b)
    return _bnn(ah, bh) + (_bnn(al, bh) + _bnn(ah, bl))


def _heads(ref, rows=slice(None)):
    return jnp.stack([ref[rows, GDN_D * h:GDN_D * (h + 1)] for h in range(GDN_HEADS)])


def _head_columns(a, first):
    return jnp.stack([a[:, first + h:first + h + 1] for h in range(GDN_HEADS)])


def _chunk_decay(g_cum, g_rows, bg):
    causal, _ = _chunk_masks()
    gc = _head_columns(g_cum, GDN_HEADS)
    gr = jnp.stack([g_rows[h:h + 1, :] for h in range(GDN_HEADS)])
    dec = jnp.where(causal, jnp.exp(jnp.where(causal, gc - gr, 0.0)), 0.0)
    return gc, _head_columns(bg, 0), dec


def _gdn_intra_fwd(qn, kn, vc, bg, *, name):
    s_len = qn.shape[0]
    c = GDN_CHUNK
    nch = INTRA_CHUNKS
    nsteps = s_len // (c * nch)

    def body(q_ref, k_ref, v_ref, bg_ref, wk_ref, wv_ref, qd_ref, kd_ref, p_ref, t_ref, g_ref):
        causal, strict = _chunk_masks()
        eye = (lax.broadcasted_iota(jnp.int32, (c, c), 0) == lax.broadcasted_iota(jnp.int32, (c, c), 1)).astype(F32)
        parts = []
        for ch in range(nch):
            rs = slice(c * ch, c * (ch + 1))
            bg = bg_ref[rs, :]
            g_cum, g_rows = _cum_decay(bg)
            g_ref[rs, :] = g_cum
            parts.append(_chunk_decay(g_cum, g_rows, bg) + (_heads(q_ref, rs), _heads(k_ref, rs), _heads(v_ref, rs)))
        gc, bc, dec, q, k, v = [jnp.concatenate([p[i] for p in parts], axis=0) for i in range(6)]
        k16 = k.astype(BF16)
        low = jnp.where(strict, bc * _bnt(k16, k16) * dec, 0.0)
        pw = -low
        t = eye + pw
        for _ in range(5):
            pw = _bnn3(pw, pw)
            t = t + _bnn3(t, pw)
        t16 = t.astype(BF16)
        eg = jnp.exp(gc)
        wk = _bnn(t16, (k * (bc * eg)).astype(BF16))
        wv = _bnn(t16, (v * bc).astype(BF16))
        pm = jnp.where(causal, _bnt(q.astype(BF16), k16) * dec, 0.0).astype(BF16)
        qd = q * eg
        kd = k * jnp.exp(gc[:, c - 1:c, :] - gc)
        for idx in range(nch * GDN_HEADS):
            ch, h = divmod(idx, GDN_HEADS)
            rs = slice(c * ch, c * (ch + 1))
            hs = slice(GDN_D * h, GDN_D * (h + 1))
            t_ref[h, rs, :] = t[idx]
            p_ref[h, rs, :] = pm[idx]
            wk_ref[rs, hs] = wk[idx].astype(BF16)
            wv_ref[rs, hs] = wv[idx]
            qd_ref[rs, hs] = qd[idx].astype(BF16)
            kd_ref[rs, hs] = kd[idx].astype(BF16)

    row = pl.BlockSpec((c * nch, GDN_W), lambda n: (n, 0))
    sq = pl.BlockSpec((GDN_HEADS, c * nch, c), lambda n: (0, n, 0))
    narrow = pl.BlockSpec((c * nch, 128), lambda n: (n, 0))
    w16 = jax.ShapeDtypeStruct((s_len, GDN_W), BF16)
    return pl.pallas_call(
        body, name=name, grid=(nsteps,), in_specs=[row, row, row, narrow],
        out_specs=[row, row, row, row, sq, sq, narrow],
        out_shape=[w16, jax.ShapeDtypeStruct((s_len, GDN_W), F32), w16, w16,
                   jax.ShapeDtypeStruct((GDN_HEADS, s_len, c), BF16),
                   jax.ShapeDtypeStruct((GDN_HEADS, s_len, c), F32),
                   jax.ShapeDtypeStruct((s_len, 128), F32)],
        compiler_params=_params(("parallel",)),
    )(qn, kn, vc, bg)


def _gdn_scan_fwd(wk, wv, qd, kd, p, g_cum, *, name):
    s_len = wk.shape[0]
    c = GDN_CHUNK
    nchunks = s_len // c
    nch = SCAN_CHUNKS

    def body(wk_ref, wv_ref, qd_ref, kd_ref, p_ref, g_ref, o_ref, vn_ref, sp_ref, st):
        @pl.when(pl.program_id(0) == 0)
        def _():
            st[...] = jnp.zeros_like(st)

        s = st[...]
        for ch in range(nch):
            rs = slice(c * ch, c * (ch + 1))
            sp_ref[ch] = s
            s16 = s.astype(BF16)
            vn16 = (_heads(wv_ref, rs) - _bnn(_heads(wk_ref, rs), s16)).astype(BF16)
            o = _bnn(_heads(qd_ref, rs), s16) + _bnn(p_ref[:, rs, :], vn16)
            gl = jnp.exp(_head_columns(g_ref[c * ch + c - 1:c * ch + c, :], GDN_HEADS))
            s = s * gl + _btn(_heads(kd_ref, rs), vn16)
            for h in range(GDN_HEADS):
                hs = slice(GDN_D * h, GDN_D * (h + 1))
                vn_ref[rs, hs] = vn16[h]
                o_ref[rs, hs] = o[h]
        st[...] = s

    row = pl.BlockSpec((c * nch, GDN_W), lambda n: (n, 0))
    return pl.pallas_call(
        body, name=name, grid=(nchunks // nch,),
        in_specs=[row, row, row, row, pl.BlockSpec((GDN_HEADS, c * nch, c), lambda n: (0, n, 0)),
                  pl.BlockSpec((c * nch, 128), lambda n: (n, 0))],
        out_specs=[row, row, pl.BlockSpec((nch, GDN_HEADS, GDN_D, GDN_D), lambda n: (n, 0, 0, 0))],
        out_shape=[jax.ShapeDtypeStruct((s_len, GDN_W), F32), jax.ShapeDtypeStruct((s_len, GDN_W), BF16),
                   jax.ShapeDtypeStruct((nchunks, GDN_HEADS, GDN_D, GDN_D), F32)],
        scratch_shapes=[pltpu.VMEM((GDN_HEADS, GDN_D, GDN_D), F32)],
        compiler_params=_params(("arbitrary",)),
    )(wk, wv, qd, kd, p, g_cum)


def _gdn_scan_bwd(do, wk, qd, kd, p, g_cum, *, name):
    s_len = wk.shape[0]
    c = GDN_CHUNK
    nchunks = s_len // c
    nch = SCAN_CHUNKS

    def body(do_ref, wk_ref, qd_ref, kd_ref, p_ref, g_ref, dvn_ref, ds_ref, dst):
        @pl.when(pl.program_id(0) == 0)
        def _():
            dst[...] = jnp.zeros_like(dst)

        ds = dst[...]
        for ch in reversed(range(nch)):
            rs = slice(c * ch, c * (ch + 1))
            ds_ref[ch] = ds
            do16 = _heads(do_ref, rs).astype(BF16)
            dvn16 = (_btn(p_ref[:, rs, :], do16) + _bnn(_heads(kd_ref, rs), ds.astype(BF16))).astype(BF16)
            gl = jnp.exp(_head_columns(g_ref[c * ch + c - 1:c * ch + c, :], GDN_HEADS))
            ds = _btn(_heads(qd_ref, rs), do16) + ds * gl - _btn(_heads(wk_ref, rs), dvn16)
            for h in range(GDN_HEADS):
                dvn_ref[rs, GDN_D * h:GDN_D * (h + 1)] = dvn16[h]
        dst[...] = ds

    last = nchunks // nch - 1
    row = pl.BlockSpec((c * nch, GDN_W), lambda n: (last - n, 0))
    return pl.pallas_call(
        body, name=name, grid=(nchunks // nch,),
        in_specs=[row, row, row, row, pl.BlockSpec((GDN_HEADS, c * nch, c), lambda n: (0, last - n, 0)),
                  pl.BlockSpec((c * nch, 128), lambda n: (last - n, 0))],
        out_specs=[row, pl.BlockSpec((nch, GDN_HEADS, GDN_D, GDN_D), lambda n: (last - n, 0, 0, 0))],
        out_shape=[jax.ShapeDtypeStruct((s_len, GDN_W), BF16),
                   jax.ShapeDtypeStruct((nchunks, GDN_HEADS, GDN_D, GDN_D), F32)],
        scratch_shapes=[pltpu.VMEM((GDN_HEADS, GDN_D, GDN_D), F32)],
        compiler_params=_params(("arbitrary",)),
    )(do, wk, qd, kd, p, g_cum)


def _gdn_intra_bwd(qn, kn, vc, bg, g_cum, t, do, dvn, vn, sprev, ds_all, *, name):
    s_len = qn.shape[0]
    c = GDN_CHUNK
    nch = INTRA_CHUNKS
    nsteps = s_len // (c * nch)
    nb = nch * GDN_HEADS

    def body(q_ref, k_ref, v_ref, bg_ref, g_ref, t_ref, do_ref, dvn_ref, vn_ref, sp_ref, ds_ref,
             dqkv_ref, dbg_ref):
        causal, strict = _chunk_masks()
        lane = lax.broadcasted_iota(jnp.int32, (c, 128), 1)
        rowi = lax.broadcasted_iota(jnp.int32, (c, 128), 0)
        parts = []
        for ch in range(nch):
            rs = slice(c * ch, c * (ch + 1))
            bg = bg_ref[rs, :]
            _, g_rows = _cum_decay(bg)
            parts.append(_chunk_decay(g_ref[rs, :], g_rows, bg) + tuple(
                _heads(r, rs) for r in (q_ref, k_ref, v_ref, do_ref, dvn_ref, vn_ref)) + (t_ref[:, rs, :],))
        gc, bc, dec, q, k, v, do, dvn16, vn16, tm = [jnp.concatenate([p[i] for p in parts], axis=0)
                                                     for i in range(10)]
        q16, k16 = q.astype(BF16), k.astype(BF16)
        kk = _bnt(k16, k16)
        low = jnp.where(strict, bc * kk * dec, 0.0)
        eg = jnp.exp(gc)
        g_last = gc[:, c - 1:c, :]
        kdec = jnp.exp(g_last - gc)
        kb, vb, qd, kd = k * (bc * eg), v * bc, q * eg, k * kdec
        pm = jnp.where(causal, _bnt(q16, k16) * dec, 0.0)
        s = sp_ref[...].reshape(nb, GDN_D, GDN_D)
        ds = ds_ref[...].reshape(nb, GDN_D, GDN_D)
        s16, ds16 = s.astype(BF16), ds.astype(BF16)
        do16 = do.astype(BF16)
        t16 = tm.astype(BF16)

        dqd = _bnt(do16, s16)
        dp = jnp.where(causal, _bnt(do16, vn16), 0.0)
        dkd = _bnt(vn16, ds16)
        dgl = jnp.sum(jnp.sum(s * ds, axis=2, keepdims=True), axis=1, keepdims=True) * jnp.exp(g_last)
        dwk16 = (-_bnt(dvn16, s16)).astype(BF16)
        dt = _bnt(dwk16, kb.astype(BF16)) + _bnt(dvn16, vb.astype(BF16))
        dkb = _btn(t16, dwk16)
        dvb = _btn(t16, dvn16)
        th, tl = _split(tm)
        dth, dtl = _split(dt)
        xm = _btn(th, dth) + (_btn(tl, dth) + _btn(th, dtl))
        xh, xl = _split(xm)
        dlow = jnp.where(strict, -(_bnt(xh, th) + (_bnt(xl, th) + _bnt(xh, tl))), 0.0)
        dkk16 = (dlow * bc * dec).astype(BF16)
        dqk16 = (dp * dec).astype(BF16)

        dq = _bnn(dqk16, k16) + dqd * eg
        dk = _btn(dqk16, q16) + _bnn(dkk16, k16) + _btn(dkk16, k16) + dkb * (bc * eg) + dkd * kdec
        dv = dvb * bc
        for idx in range(nb):
            ch, h = divmod(idx, GDN_HEADS)
            rs = slice(c * ch, c * (ch + 1))
            hs = slice(GDN_D * h, GDN_D * (h + 1))
            dqkv_ref[0, rs, hs] = dq[idx]
            dqkv_ref[1, rs, hs] = dk[idx]
            dqkv_ref[2, rs, hs] = dv[idx]

        dbeta = (jnp.sum(dlow * kk * dec, axis=2, keepdims=True)
                 + jnp.sum(dkb * k, axis=2, keepdims=True) * eg + jnp.sum(dvb * v, axis=2, keepdims=True))
        mm = dlow * low + dp * pm
        mh, ml = _split(mm)
        ones16 = jnp.ones((nb, c, 128), BF16)
        col_sum = (_btn(mh, ones16) + _btn(ml, ones16))[:, :, 0:1]
        dkd_sum = jnp.sum(dkd * kd, axis=2, keepdims=True)
        dg = (jnp.sum(mm, axis=2, keepdims=True) - col_sum + jnp.sum(dkb * kb, axis=2, keepdims=True)
              + jnp.sum(dqd * qd, axis=2, keepdims=True) - dkd_sum)
        tail = jnp.sum(dkd_sum, axis=1, keepdims=True) + dgl
        upper = (lax.broadcasted_iota(jnp.int32, (c, c), 0) <= lax.broadcasted_iota(jnp.int32, (c, c), 1)).astype(F32)
        for ch in range(nch):
            dbeta_all = jnp.zeros((c, 128), F32)
            dg_all = jnp.zeros((c, 128), F32)
            for h in range(GDN_HEADS):
                idx = ch * GDN_HEADS + h
                dbeta_all = dbeta_all + jnp.where(lane == h, dbeta[idx], 0.0)
                dg_all = dg_all + jnp.where(lane == GDN_HEADS + h,
                                            dg[idx] + jnp.where(rowi == c - 1, tail[idx], 0.0), 0.0)
            dbg_ref[c * ch:c * (ch + 1), :] = dbeta_all + _nn(upper, dg_all, HIGHEST)

    row = pl.BlockSpec((c * nch, GDN_W), lambda n: (n, 0))
    narrow = pl.BlockSpec((c * nch, 128), lambda n: (n, 0))
    state = pl.BlockSpec((nch, GDN_HEADS, GDN_D, GDN_D), lambda n: (n, 0, 0, 0))
    return pl.pallas_call(
        body, name=name, grid=(nsteps,),
        in_specs=[row, row, row, narrow, narrow, pl.BlockSpec((GDN_HEADS, c * nch, c), lambda n: (0, n, 0)),
                  row, row, row, state, state],
        out_specs=[pl.BlockSpec((3, c * nch, GDN_W), lambda n: (0, n, 0)), narrow],
        out_shape=[jax.ShapeDtypeStruct((3, s_len, GDN_W), F32), jax.ShapeDtypeStruct((s_len, 128), F32)],
        compiler_params=_params(("parallel",)),
    )(qn, kn, vc, bg, g_cum, t, do, dvn, vn, sprev, ds_all)


def _gdn_prep_bwd(u, dqkv, cw, du, *, name, tm=256):
    s_len = u.shape[0]
    nsteps = s_len // tm
    ext = tm + SHALO

    def body(uc, up, un, dc, dn, cw_ref, du_in_ref, du_ref, dcw_ref, xbuf, dbuf, pbuf, wacc):
        del du_in_ref
        part = pl.program_id(0)
        i = pl.program_id(1)

        @pl.when(i == 0)
        def _():
            wacc[...] = jnp.zeros_like(wacc)

        xbuf[:SHALO, :] = jnp.where(i > 0, up[...], 0.0)
        xbuf[SHALO:SHALO + tm, :] = uc[...]
        xbuf[SHALO + tm:, :] = jnp.where(i < nsteps - 1, un[...], 0.0)
        dbuf[:tm, :] = dc[...]
        dbuf[tm:, :] = jnp.where(i < nsteps - 1, dn[...], 0.0)
        first = SHALO - SHORT_CONV + 1
        w = [cw_ref[j:j + 1, :] for j in range(SHORT_CONV)]
        taps = [first + j for j in range(SHORT_CONV)]
        xv = xbuf[...]
        pre = jnp.zeros((ext, GDN_W), F32)
        for j, win in _shifted_windows(xv, ext, taps):
            pre = pre + win * w[j]
        y = _silu(pre)
        dout = dbuf[...]
        scale = jnp.where(part == 0, GDN_D ** -0.5, 1.0)
        for h in range(GDN_HEADS):
            hs = slice(GDN_D * h, GDN_D * (h + 1))
            yh, dh = y[:, hs], dout[:, hs]
            rs = lax.rsqrt(jnp.sum(yh * yh, axis=-1, keepdims=True) + 1e-6)
            dyn = scale * rs * (dh - yh * (rs * rs) * jnp.sum(dh * yh, axis=-1, keepdims=True))
            dy = jnp.where(part < 2, dyn, dh)
            pbuf[:, hs] = dy * _dsilu(pre[:, hs])
        acc = jnp.zeros((tm, GDN_W), F32)
        dpre = pbuf[0:tm, :]
        for k, win in _shifted_windows(pbuf[...], tm, list(range(SHORT_CONV))):
            acc = acc + win * w[SHORT_CONV - 1 - k]
        for j, win in _shifted_windows(xv, tm, taps):
            wacc[j] += (win * dpre).reshape(tm // 8, 8, GDN_W).sum(axis=0)
        du_ref[...] = acc.astype(du_ref.dtype)

        @pl.when(i == nsteps - 1)
        def _():
            for j in range(SHORT_CONV):
                dcw_ref[j:j + 1, :] = jnp.sum(wacc[j], axis=0, keepdims=True)
            dcw_ref[SHORT_CONV:, :] = jnp.zeros((SHALO - SHORT_CONV, GDN_W), F32)

    per = tm // SHALO
    return pl.pallas_call(
        body, name=name, grid=(3, nsteps),
        in_specs=[pl.BlockSpec((tm, GDN_W), lambda p, i: (i, COL_GQ + p)),
                  pl.BlockSpec((SHALO, GDN_W), lambda p, i: (jnp.maximum(i * per - 1, 0), COL_GQ + p)),
                  pl.BlockSpec((SHALO, GDN_W), lambda p, i: (jnp.minimum((i + 1) * per, s_len // SHALO - 1), COL_GQ + p)),
                  pl.BlockSpec((None, tm, GDN_W), lambda p, i: (p, i, 0)),
                  pl.BlockSpec((None, SHALO, GDN_W), lambda p, i: (p, jnp.minimum((i + 1) * per, s_len // SHALO - 1), 0)),
                  pl.BlockSpec((SHALO, GDN_W), lambda p, i: (0, p)),
                  pl.BlockSpec(memory_space=pl.ANY)],
        out_specs=[pl.BlockSpec((tm, GDN_W), lambda p, i: (i, COL_GQ + p)),
                   pl.BlockSpec((SHALO, GDN_W), lambda p, i: (0, p))],
        out_shape=[jax.ShapeDtypeStruct(du.shape, du.dtype), jax.ShapeDtypeStruct((SHALO, 3 * GDN_W), F32)],
        scratch_shapes=[pltpu.VMEM((tm + 2 * SHALO, GDN_W), F32), pltpu.VMEM((ext, GDN_W), F32),
                        pltpu.VMEM((ext, GDN_W), F32), pltpu.VMEM((SHORT_CONV, 8, GDN_W), F32)],
        input_output_aliases={6: 0},
        compiler_params=_params(("arbitrary", "arbitrary")),
    )(u, u, u, dqkv, dqkv, cw, du)


def _gdn_ba_bwd(u, dbg, al, dtb, du, *, name, tm=256):
    s_len = u.shape[0]
    nsteps = s_len // tm
    wpad = IN_WP - COL_BA

    def body(uba, dbg_ref, al_ref, dtb_ref, du_in_ref, du_ref, sums_ref):
        del du_in_ref
        i = pl.program_id(0)

        @pl.when(i == 0)
        def _():
            sums_ref[...] = jnp.zeros_like(sums_ref)

        ba = uba[...]
        dbg = dbg_ref[...]
        lane = lax.broadcasted_iota(jnp.int32, ba.shape, 1)
        is_g = (lane >= GDN_HEADS) & (lane < 2 * GDN_HEADS)
        beta = _sigmoid(ba)
        z = ba + dtb_ref[...]
        ea = jnp.exp(al_ref[...])
        g = -ea * _softplus(z)
        dz = jnp.where(is_g, dbg * (-ea) * _sigmoid(z), 0.0)
        du_ref[:, :128] = jnp.where(lane < GDN_HEADS, dbg * beta * (1.0 - beta), dz).astype(du_ref.dtype)
        du_ref[:, 128:] = jnp.zeros((tm, wpad - 128), du_ref.dtype)
        sums_ref[0:1, :] += jnp.sum(jnp.where(is_g, dbg * g, 0.0), axis=0, keepdims=True)
        sums_ref[1:2, :] += jnp.sum(dz, axis=0, keepdims=True)

    vec = pl.BlockSpec((1, 128), lambda i: (0, 0))
    return pl.pallas_call(
        body, name=name, grid=(nsteps,),
        in_specs=[pl.BlockSpec((tm, 128), lambda i: (i, COL_BA // 128)), pl.BlockSpec((tm, 128), lambda i: (i, 0)),
                  vec, vec, pl.BlockSpec(memory_space=pl.ANY)],
        out_specs=[pl.BlockSpec((tm, wpad), lambda i: (i, COL_BA // wpad)), pl.BlockSpec((8, 128), lambda i: (0, 0))],
        out_shape=[jax.ShapeDtypeStruct(du.shape, du.dtype), jax.ShapeDtypeStruct((8, 128), F32)],
        input_output_aliases={4: 0},
        compiler_params=_params(("arbitrary",)),
    )(u, dbg, al, dtb, du)


def _rope_tables(s_len):
    half = ROPE_DIM // 2
    inv = ROPE_THETA ** (-jnp.arange(half, dtype=F32) / half)
    ang = jnp.arange(s_len, dtype=F32)[:, None] * inv[None, :]
    cos, sin = jnp.cos(ang), jnp.sin(ang)
    one = jnp.ones((s_len, ATT_HD - ROPE_DIM), F32)
    zero = jnp.zeros((s_len, ATT_HD - ROPE_DIM), F32)
    zh = jnp.zeros((s_len, half), F32)
    c = jnp.concatenate([cos, cos, one], axis=1)
    s1 = jnp.concatenate([-sin, zh, zero], axis=1)
    s2 = jnp.concatenate([zh, sin, zero], axis=1)
    return tuple(jnp.concatenate([t, t], axis=1) for t in (c, s1, s2))


def _rope(x, c, s1, s2):
    return x * c + pltpu.roll(x, 128 - ROPE_DIM // 2, 1) * s1 + pltpu.roll(x, ROPE_DIM // 2, 1) * s2


def _rope_t(dy, c, s1, s2):
    return dy * c + pltpu.roll(dy * s1, ROPE_DIM // 2, 1) + pltpu.roll(dy * s2, 128 - ROPE_DIM // 2, 1)


DILATIONS = tuple(d for _, d in DIL_PATTERNS)
ATT_QBLOCKS = 2
VIEW_ROWS = 256


def _to_view(scr, out_ref, dil, dtype):
    nblk, rows, _ = scr.shape
    width = nblk * 128
    for b in range(nblk):
        if dil == 1:
            out_ref[:, 128 * b:128 * (b + 1)] = scr[b].astype(dtype)
            continue
        for r in range(dil):
            out_ref[:, r * width + 128 * b:r * width + 128 * (b + 1)] = (
                scr.at[b][pl.ds(r, rows // dil, stride=dil), :].astype(dtype))


def _from_view(in_ref, scr, dil):
    nblk, rows, _ = scr.shape
    width = nblk * 128
    for b in range(nblk):
        for r in range(dil):
            scr.at[b][pl.ds(r, rows // dil, stride=dil), :] = in_ref[:, r * width + 128 * b:r * width + 128 * (b + 1)]


def _view_spec(dil, width, tm=VIEW_ROWS):
    return pl.BlockSpec((tm // dil, dil * width), lambda i: (i, 0))


def _view_shape(s_len, dil, width, dtype):
    return jax.ShapeDtypeStruct((s_len // dil, dil * width), dtype)


def _att_prep_fwd(u, tabs, *, name):
    s_len = u.shape[0]
    tm = VIEW_ROWS
    scale = ATT_HD ** -0.5
    nblk = ATT_W // 128

    def body(uq, uk, uv, c_ref, s1_ref, s2_ref, *rest):
        outs, scr = rest[:-1], rest[-1]
        c, s1, s2 = c_ref[...], s1_ref[...], s2_ref[...]
        for part, src in enumerate((uq, uk, uv)):
            for b in range(nblk):
                xb = src[:, 128 * b:128 * (b + 1)]
                if part == 0:
                    xb = _rope(xb, c, s1, s2) * scale
                elif part == 1:
                    xb = _rope(xb, c, s1, s2)
                scr[b] = xb
            for gi, dil in enumerate(DILATIONS):
                _to_view(scr, outs[3 * gi + part], dil, BF16)

    tab = pl.BlockSpec((tm, 128), lambda i: (i, 0))
    outs = pl.pallas_call(
        body, name=name, grid=(s_len // tm,),
        in_specs=[pl.BlockSpec((tm, ATT_W), lambda i, col=COL_AQ + j: (i, col)) for j in range(3)] + [tab] * 3,
        out_specs=[_view_spec(dil, ATT_W) for dil in DILATIONS for _ in range(3)],
        out_shape=[_view_shape(s_len, dil, ATT_W, BF16) for dil in DILATIONS for _ in range(3)],
        scratch_shapes=[pltpu.VMEM((nblk, tm, 128), F32)],
        compiler_params=_params(("parallel",)),
    )(u, u, u, *tabs)
    return [outs[3 * gi:3 * gi + 3] for gi in range(len(DILATIONS))]


def _stack_heads(x):
    lane = lax.broadcasted_iota(jnp.int32, (1, 128), 1)
    zero = jnp.zeros_like(x)
    return jnp.concatenate([jnp.where(lane < ATT_HD, x, zero), jnp.where(lane >= ATT_HD, x, zero)], axis=0)


def _att_fwd(qr, kr, vb, dil, *, name):
    lr = qr.shape[0]
    blk = ATT_BLOCK
    qb = ATT_QBLOCKS
    nsteps = lr // (blk * qb)

    def body(q_ref, kp_ref, kc_ref, vp_ref, vc_ref, o_ref, lse_ref):
        n = pl.program_id(1)
        qi = lax.broadcasted_iota(jnp.int32, (blk, 2 * blk), 0)
        ki = lax.broadcasted_iota(jnp.int32, (blk, 2 * blk), 1)
        dist = qi + blk - ki
        band = (dist >= 0) & (dist <= blk)
        lane = lax.broadcasted_iota(jnp.int32, (blk, 128), 1)
        for sub in range(qb):
            rs = slice(blk * sub, blk * (sub + 1))
            valid = band if sub > 0 else band & ((ki >= blk) | (n > 0))
            valid = jnp.concatenate([valid, valid], axis=0)
            lse_all = jnp.zeros((blk, 128), F32)
            for hp in range(ATT_HEADS // 2):
                bs = slice(128 * hp, 128 * (hp + 1))
                if sub == 0:
                    kb = jnp.concatenate([kp_ref[:, bs], kc_ref[0:blk, bs]], axis=0)
                    vv = jnp.concatenate([vp_ref[:, bs], vc_ref[0:blk, bs]], axis=0)
                else:
                    kb = kc_ref[blk * (sub - 1):blk * (sub + 1), bs]
                    vv = vc_ref[blk * (sub - 1):blk * (sub + 1), bs]
                s = jnp.where(valid, _nt(_stack_heads(q_ref[rs, bs]), kb), NEG_INF)
                m = jnp.max(s, axis=-1, keepdims=True)
                p = jnp.exp(s - m)
                l = jnp.sum(p, axis=-1, keepdims=True)
                o = _nn((p * (1.0 / l)).astype(BF16), vv)
                o_ref[rs, bs] = jnp.where(lane < ATT_HD, o[:blk], o[blk:])
                lse = m + jnp.log(l)
                lse_all = (lse_all + jnp.where(lane == 2 * hp, lse[:blk], 0.0)
                           + jnp.where(lane == 2 * hp + 1, lse[blk:], 0.0))
            lse_ref[rs, :] = lse_all

    cur = pl.BlockSpec((blk * qb, ATT_W), lambda r, n: (n, r))
    prev = pl.BlockSpec((blk, ATT_W), lambda r, n: (jnp.maximum(qb * n - 1, 0), r))
    return pl.pallas_call(
        body, name=name, grid=(dil, nsteps), in_specs=[cur, prev, cur, prev, cur],
        out_specs=[cur, pl.BlockSpec((blk * qb, 128), lambda r, n: (n, r))],
        out_shape=[jax.ShapeDtypeStruct(qr.shape, F32), jax.ShapeDtypeStruct((lr, dil * 128), F32)],
        compiler_params=_params(("parallel", "parallel")),
    )(qr, kr, kr, vb, vb)


def _att_bwd(qr, kr, vb, do, lse, delta, dil, *, name):
    lr = qr.shape[0]
    blk = ATT_BLOCK
    nsteps = lr // (2 * blk)

    def body(q_ref, kp_ref, kc_ref, vp_ref, vc_ref, do_ref, lse_ref, dl_ref, dq_ref, dk_ref, dv_ref, carry):
        n = pl.program_id(1)

        @pl.when(n == 0)
        def _():
            carry[...] = jnp.zeros_like(carry)

        @pl.when(n == nsteps)
        def _():
            for t, ref in enumerate((dk_ref, dv_ref)):
                ref[:blk, :] = carry[0, t]
                ref[blk:, :] = carry[1, t]

        @pl.when(n < nsteps)
        def _():
            qi = lax.broadcasted_iota(jnp.int32, (blk, 2 * blk), 0)
            ki = lax.broadcasted_iota(jnp.int32, (blk, 2 * blk), 1)
            dist = qi + blk - ki
            band = (dist >= 0) & (dist <= blk)
            lane = lax.broadcasted_iota(jnp.int32, (blk, 128), 1)
            for hp in range(ATT_HEADS // 2):
                bs = slice(128 * hp, 128 * (hp + 1))
                accs = []
                for sub in range(2):
                    rs = slice(blk * sub, blk * (sub + 1))
                    valid = band if sub > 0 else band & ((ki >= blk) | (n > 0))
                    valid = jnp.concatenate([valid, valid], axis=0)
                    if sub == 0:
                        kb = jnp.concatenate([kp_ref[:, bs], kc_ref[0:blk, bs]], axis=0)
                        vv = jnp.concatenate([vp_ref[:, bs], vc_ref[0:blk, bs]], axis=0)
                    else:
                        kb, vv = kc_ref[:, bs], vc_ref[:, bs]
                    q2 = _stack_heads(q_ref[rs, bs])
                    do2 = _stack_heads(do_ref[rs, bs])
                    lse2 = jnp.concatenate([lse_ref[rs, 2 * hp:2 * hp + 1], lse_ref[rs, 2 * hp + 1:2 * hp + 2]], axis=0)
                    dl2 = jnp.concatenate([dl_ref[rs, 2 * hp:2 * hp + 1], dl_ref[rs, 2 * hp + 1:2 * hp + 2]], axis=0)
                    p = jnp.where(valid, jnp.exp(_nt(q2, kb) - lse2), 0.0)
                    ds16 = (p * (_nt(do2, vv) - dl2)).astype(BF16)
                    dq2 = _nn(ds16, kb)
                    dq_ref[rs, bs] = jnp.where(lane < ATT_HD, dq2[:blk], dq2[blk:])
                    accs.append((_tn(ds16, q2), _tn(p.astype(BF16), do2)))
                for t, ref in enumerate((dk_ref, dv_ref)):
                    first, second = accs[0][t], accs[1][t]
                    ref[:blk, bs] = carry[0, t, :, bs]
                    ref[blk:, bs] = carry[1, t, :, bs] + first[:blk]
                    carry[0, t, :, bs] = first[blk:] + second[:blk]
                    carry[1, t, :, bs] = second[blk:]

    def at(n):
        return jnp.minimum(n, nsteps - 1)

    cur = pl.BlockSpec((2 * blk, ATT_W), lambda r, n: (at(n), r))
    prev = pl.BlockSpec((blk, ATT_W), lambda r, n: (jnp.maximum(2 * at(n) - 1, 0), r))
    nar = pl.BlockSpec((2 * blk, 128), lambda r, n: (at(n), r))
    late = pl.BlockSpec((2 * blk, ATT_W), lambda r, n: (jnp.maximum(n - 1, 0), r))
    out = jax.ShapeDtypeStruct(qr.shape, F32)
    return pl.pallas_call(
        body, name=name, grid=(dil, nsteps + 1), in_specs=[cur, prev, cur, prev, cur, cur, nar, nar],
        out_specs=[cur, late, late], out_shape=[out, out, out],
        scratch_shapes=[pltpu.VMEM((2, 2, blk, ATT_W), F32)],
        compiler_params=_params(("parallel", "arbitrary")),
    )(qr, kr, kr, vb, vb, do, lse, delta)


def _att_prep_bwd(dgroups, tabs, du, *, name):
    s_len = du.shape[0]
    tm = VIEW_ROWS
    scale = ATT_HD ** -0.5
    nblk = ATT_W // 128
    ng = len(DILATIONS)

    def body(*refs):
        grads = refs[:3 * ng]
        c_ref, s1_ref, s2_ref, _, du_ref = refs[3 * ng:3 * ng + 5]
        scrs = refs[3 * ng + 5:]
        c, s1, s2 = c_ref[...], s1_ref[...], s2_ref[...]
        for part in range(3):
            for gi, dil in enumerate(DILATIONS):
                if dil > 1:
                    _from_view(grads[3 * gi + part], scrs[gi], dil)
            for b in range(nblk):
                tot = None
                for gi, dil in enumerate(DILATIONS):
                    term = grads[3 * gi + part][:, 128 * b:128 * (b + 1)] if dil == 1 else scrs[gi][b]
                    tot = term if tot is None else tot + term
                if part == 0:
                    tot = _rope_t(tot * scale, c, s1, s2)
                elif part == 1:
                    tot = _rope_t(tot, c, s1, s2)
                du_ref[:, ATT_W * part + 128 * b:ATT_W * part + 128 * (b + 1)] = tot.astype(du_ref.dtype)

    tab = pl.BlockSpec((tm, 128), lambda i: (i, 0))
    return pl.pallas_call(
        body, name=name, grid=(s_len // tm,),
        in_specs=[_view_spec(dil, ATT_W) for dil in DILATIONS for _ in range(3)] + [tab] * 3
        + [pl.BlockSpec(memory_space=pl.ANY)],
        out_specs=pl.BlockSpec((tm, 3 * ATT_W), lambda i: (i, COL_AQ // 3)),
        out_shape=jax.ShapeDtypeStruct(du.shape, du.dtype),
        scratch_shapes=[pltpu.VMEM((nblk, tm, 128), F32) for _ in DILATIONS],
        input_output_aliases={3 * ng + 3: 0},
        compiler_params=_params(("parallel",)),
    )(*[a for g in dgroups for a in g], *tabs, du)


def _head_weights(w, b):
    lane = lax.broadcasted_iota(jnp.int32, (1, 128), 1)
    return jnp.where(lane < ATT_HD, w[:, 2 * b:2 * b + 1], w[:, 2 * b + 1:2 * b + 2])


def _assemble_fwd(pw, u, o_gdn, gnw, o_groups, lse_groups, *, name):
    s_len = u.shape[0]
    tm = VIEW_ROWS
    c = CONV_CH
    nblk = ATT_W // 128
    ng = len(DILATIONS)

    def body(*refs):
        pw_ref, cg_ref, z_ref, ag_ref, og_ref, gnw_ref = refs[:6]
        o_refs, l_refs = refs[6:6 + ng], refs[6 + ng:6 + 2 * ng]
        y_ref, oa_ref = refs[6 + 2 * ng:8 + 2 * ng]
        lse_outs = refs[8 + 2 * ng:8 + 3 * ng]
        o_scr, l_scr = refs[8 + 3 * ng:8 + 4 * ng], refs[8 + 4 * ng:8 + 5 * ng]
        lse_scr = refs[8 + 5 * ng]
        y_ref[:, :c] = (pw_ref[...] * _silu(cg_ref[...])).astype(BF16)
        gw = gnw_ref[...]
        for h in range(GDN_HEADS):
            hs = slice(GDN_D * h, GDN_D * (h + 1))
            oh = og_ref[:, hs]
            yn = oh * lax.rsqrt(jnp.mean(oh * oh, axis=-1, keepdims=True) + 1e-6) * gw
            y_ref[:, c + GDN_D * h:c + GDN_D * (h + 1)] = (yn * _silu(z_ref[:, hs])).astype(BF16)
        for gi, dil in enumerate(DILATIONS):
            if dil > 1:
                _from_view(o_refs[gi], o_scr[gi], dil)
                _from_view(l_refs[gi], l_scr[gi], dil)
        ls = [l_refs[gi][...] if dil == 1 else l_scr[gi][0] for gi, dil in enumerate(DILATIONS)]
        m = functools.reduce(jnp.maximum, ls)
        es = [jnp.exp(l - m) for l in ls]
        den = functools.reduce(lambda a, b: a + b, es)
        lse_scr[0] = m + jnp.log(den)
        ws = [e / den for e in es]
        for b in range(nblk):
            bs = slice(128 * b, 128 * (b + 1))
            o = None
            for gi, dil in enumerate(DILATIONS):
                term = _head_weights(ws[gi], b) * (o_refs[gi][:, bs] if dil == 1 else o_scr[gi][b])
                o = term if o is None else o + term
            oa_ref[:, bs] = o
            y_ref[:, c + GDN_W + 128 * b:c + GDN_W + 128 * (b + 1)] = (o * _silu(ag_ref[:, bs])).astype(BF16)
        for gi, dil in enumerate(DILATIONS):
            _to_view(lse_scr, lse_outs[gi], dil, F32)

    wide = pl.BlockSpec((tm, 768), lambda i: (i, 0))
    return pl.pallas_call(
        body, name=name, grid=(s_len // tm,),
        in_specs=[pl.BlockSpec((tm, c), lambda i: (i, 0)), pl.BlockSpec((tm, c), lambda i: (i, 1024 // c)),
                  pl.BlockSpec((tm, 768), lambda i: (i, COL_GQ + 3)), pl.BlockSpec((tm, 768), lambda i: (i, COL_AQ + 3)),
                  wide, pl.BlockSpec((1, 128), lambda i: (0, 0))]
        + [_view_spec(dil, ATT_W) for dil in DILATIONS] + [_view_spec(dil, 128) for dil in DILATIONS],
        out_specs=[pl.BlockSpec((tm, D_MODEL), lambda i: (i, 0)), wide] + [_view_spec(dil, 128) for dil in DILATIONS],
        out_shape=[jax.ShapeDtypeStruct((s_len, D_MODEL), BF16), jax.ShapeDtypeStruct((s_len, ATT_W), F32)]
        + [_view_shape(s_len, dil, 128, F32) for dil in DILATIONS],
        scratch_shapes=[pltpu.VMEM((nblk, tm, 128), F32) for _ in DILATIONS]
        + [pltpu.VMEM((1, tm, 128), F32) for _ in DILATIONS] + [pltpu.VMEM((1, tm, 128), F32)],
        compiler_params=_params(("parallel",)),
    )(pw, u, u, u, o_gdn, gnw, *o_groups, *lse_groups)


def _assemble_bwd(dy, pw, u, o_gdn, gnw, o_att, *, name):
    s_len = u.shape[0]
    tm = VIEW_ROWS
    c = CONV_CH
    nsteps = s_len // tm
    nblk = ATT_W // 128
    ng = len(DILATIONS)

    def body(dy_ref, pw_ref, cg_ref, z_ref, ag_ref, og_ref, gnw_ref, oa_ref,
             du_ref, dpw_ref, dog_ref, dgw_ref, *rest):
        do_outs, dl_outs = rest[:ng], rest[ng:2 * ng]
        acc_ref, do_scr, dl_scr = rest[2 * ng:]
        i = pl.program_id(0)

        @pl.when(i == 0)
        def _():
            acc_ref[...] = jnp.zeros_like(acc_ref)

        du_ref[...] = jnp.zeros_like(du_ref)
        dyc = dy_ref[:, :c]
        cg = cg_ref[...]
        dpw_ref[...] = dyc * _silu(cg)
        du_ref[:, 1024:1024 + c] = (dyc * pw_ref[...] * _dsilu(cg)).astype(BF16)
        gw = gnw_ref[...]
        dgw = jnp.zeros((8, 128), F32)
        for h in range(GDN_HEADS):
            hs = slice(GDN_D * h, GDN_D * (h + 1))
            oh = og_ref[:, hs]
            zh = z_ref[:, hs]
            dyh = dy_ref[:, c + GDN_D * h:c + GDN_D * (h + 1)]
            r = lax.rsqrt(jnp.mean(oh * oh, axis=-1, keepdims=True) + 1e-6)
            xn = oh * r
            dyn = dyh * _silu(zh)
            du_ref[:, GDN_W * (COL_GQ + 3) + GDN_D * h:GDN_W * (COL_GQ + 3) + GDN_D * (h + 1)] = (
                dyh * xn * gw * _dsilu(zh)).astype(BF16)
            dgw = dgw + (dyn * xn).reshape(tm // 8, 8, 128).sum(axis=0)
            dxn = dyn * gw
            dog_ref[:, hs] = r * (dxn - xn * jnp.mean(dxn * xn, axis=-1, keepdims=True))
        acc_ref[...] += dgw
        lane = lax.broadcasted_iota(jnp.int32, (tm, 128), 1)
        delta = jnp.zeros((tm, 128), F32)
        for b in range(ATT_W // 128):
            bs = slice(128 * b, 128 * (b + 1))
            dya = dy_ref[:, c + GDN_W + 128 * b:c + GDN_W + 128 * (b + 1)]
            ag = ag_ref[:, bs]
            oa = oa_ref[:, bs]
            do = dya * _silu(ag)
            do_scr[b] = do
            du_ref[:, ATT_W * (COL_AQ + 3) + 128 * b:ATT_W * (COL_AQ + 3) + 128 * (b + 1)] = (
                dya * oa * _dsilu(ag)).astype(BF16)
            prod = do * oa
            lo = jnp.sum(jnp.where(lane < ATT_HD, prod, 0.0), axis=-1, keepdims=True)
            hi = jnp.sum(jnp.where(lane >= ATT_HD, prod, 0.0), axis=-1, keepdims=True)
            delta = delta + jnp.where(lane == 2 * b, lo, 0.0) + jnp.where(lane == 2 * b + 1, hi, 0.0)
        dl_scr[0] = delta
        for gi, dil in enumerate(DILATIONS):
            _to_view(do_scr, do_outs[gi], dil, BF16)
            _to_view(dl_scr, dl_outs[gi], dil, F32)

        @pl.when(i == nsteps - 1)
        def _():
            dgw_ref[...] = jnp.sum(acc_ref[...], axis=0, keepdims=True)

    wide = pl.BlockSpec((tm, 768), lambda i: (i, 0))
    vec = pl.BlockSpec((1, 128), lambda i: (0, 0))
    outs = pl.pallas_call(
        body, name=name, grid=(nsteps,),
        in_specs=[pl.BlockSpec((tm, D_MODEL), lambda i: (i, 0)), pl.BlockSpec((tm, c), lambda i: (i, 0)),
                  pl.BlockSpec((tm, c), lambda i: (i, 1024 // c)), pl.BlockSpec((tm, 768), lambda i: (i, COL_GQ + 3)),
                  pl.BlockSpec((tm, 768), lambda i: (i, COL_AQ + 3)), wide, vec, wide],
        out_specs=[pl.BlockSpec((tm, IN_WP), lambda i: (i, 0)), pl.BlockSpec((tm, c), lambda i: (i, 0)), wide, vec]
        + [_view_spec(dil, ATT_W) for dil in DILATIONS] + [_view_spec(dil, 128) for dil in DILATIONS],
        out_shape=[jax.ShapeDtypeStruct((s_len, IN_WP), BF16), jax.ShapeDtypeStruct((s_len, c), F32),
                   jax.ShapeDtypeStruct((s_len, GDN_W), F32), jax.ShapeDtypeStruct((1, 128), F32)]
        + [_view_shape(s_len, dil, ATT_W, BF16) for dil in DILATIONS]
        + [_view_shape(s_len, dil, 128, F32) for dil in DILATIONS],
        scratch_shapes=[pltpu.VMEM((8, 128), F32), pltpu.VMEM((nblk, tm, 128), F32), pltpu.VMEM((1, tm, 128), F32)],
        compiler_params=_params(("arbitrary",)),
    )(dy, pw, u, u, u, o_gdn, gnw, o_att)
    return outs[:4], outs[4:4 + ng], outs[4 + ng:]


def _layer_fwd(x, p, tabs):
    h = _rms_fwd(x, p["norm_w"], name="rms_fwd")
    u = _matmul(h, p["wp"], name="in_proj", tk=2048)
    conv, sw = _conf_fwd(u, p["dw_w"], p["dw_b"], p["ln_w"], p["ln_b"], name="conf_fwd")
    pw = _matmul(sw, p["pw_w"], name="conf_pw")
    qn, kn, vc, bg = _gdn_prep_fwd(u, p["cw"], p["al"], p["dtb"], name="gdn_prep_fwd")
    wk, wv, qd, kd, pm, t, g_cum = _gdn_intra_fwd(qn, kn, vc, bg, name="gdn_intra_fwd")
    o_gdn, vn, sprev = _gdn_scan_fwd(wk, wv, qd, kd, pm, g_cum, name="gdn_scan_fwd")
    qkv = _att_prep_fwd(u, tabs, name="att_prep_fwd")
    groups = [_att_fwd(*qkv[gi], dil, name=f"att_fwd_d{dil}") for gi, dil in enumerate(DILATIONS)]
    outs = _assemble_fwd(pw, u, o_gdn, p["gnw"], [g[0] for g in groups], [g[1] for g in groups],
                         name="assemble_fwd")
    y, o_att, lse = outs[0], outs[1], outs[2:]
    x_next = _matmul(y, p["wout"], add=x, name="out_proj", tm=512, tn=2048, tk=2048)
    saved = dict(x=x, h=h, u=u, conv=conv, sw=sw, pw=pw, qn=qn, kn=kn, vc=vc, bg=bg, wk=wk, qd=qd, kd=kd, pm=pm,
                 t=t, g_cum=g_cum, vn=vn, sprev=sprev, o_gdn=o_gdn, qkv=qkv, o_att=o_att, lse=lse, y=y)
    return x_next, saved


def _layer_bwd(dx_out, s, p, tabs, layer, big):
    dy = _matmul(dx_out, p["wout"], tb=True, name="out_proj_dy", tm=512, tn=2048, tk=2048)
    d_wout = _matmul(s["y"], dx_out, ta=True, name="out_proj_dw", tk=2048, stack=(big[1], layer, DEPTH))
    (du, dpw, dog, dgw), do_views, dl_views = _assemble_bwd(dy, s["pw"], s["u"], s["o_gdn"], p["gnw"], s["o_att"],
                                                            name="assemble_bwd")
    dsw = _matmul(dpw, p["pw_w"], tb=True, name="conf_pw_dx")
    d_pw_w = _matmul(s["sw"], dpw, ta=True, name="conf_pw_dw", stack=(big[2], layer, DEPTH))
    dconv, ln_sums = _conf_bwd_ln(dsw, s["conv"], p["ln_w"], p["ln_b"], name="conf_bwd_ln")
    du, d_dw_w = _conf_bwd_conv(s["u"], dconv, p["dw_w"], du, name="conf_bwd_conv")
    dvn, ds_all = _gdn_scan_bwd(dog, s["wk"], s["qd"], s["kd"], s["pm"], s["g_cum"], name="gdn_scan_bwd")
    dqkv, dbg = _gdn_intra_bwd(s["qn"], s["kn"], s["vc"], s["bg"], s["g_cum"], s["t"], dog, dvn, s["vn"],
                               s["sprev"], ds_all, name="gdn_intra_bwd")
    du, d_cw = _gdn_prep_bwd(s["u"], dqkv, p["cw"], du, name="gdn_prep_bwd")
    du, ba_sums = _gdn_ba_bwd(s["u"], dbg, p["al"], p["dtb"], du, name="gdn_ba_bwd")
    dgroups = []
    for gi, dil in enumerate(DILATIONS):
        args = (*s["qkv"][gi], do_views[gi], s["lse"][gi], dl_views[gi], dil)
        dgroups.append(_att_bwd(*args, name=f"att_bwd_d{dil}"))
    du = _att_prep_bwd(dgroups, tabs, du, name="att_prep_bwd")
    dh = _matmul(du, p["wp"], tb=True, name="in_proj_dx", tk=4096)
    d_wp = _matmul(s["h"], du, ta=True, name="in_proj_dw", tk=4096, stack=(big[0], layer, DEPTH))
    dx, d_norm_w = _rms_bwd(s["x"], dh, p["norm_w"], dx_out, name="rms_bwd")
    small = dict(norm_w=d_norm_w, gnw=dgw, ln_sums=ln_sums, dw_w=d_dw_w, cw=d_cw, ba_sums=ba_sums)
    return dx, (d_wp, d_wout, d_pw_w), small


def _trunk(x, target, params, final_norm_w):
    tabs = _rope_tables(x.shape[0])
    layers = [{k: v[l] for k, v in params.items()} for l in range(DEPTH)]
    saved = []
    for p in layers:
        x, s = _layer_fwd(x, p, tabs)
        saved.append(s)
    dx, d_final, loss = _loss_head(x, final_norm_w, target, name="loss_head")
    big = (None, None, None)
    small = [None] * DEPTH
    for l in reversed(range(DEPTH)):
        dx, big, small[l] = _layer_bwd(dx, saved[l], layers[l], tabs, l, big)
    grads = {k: jnp.stack([sm[k] for sm in small]) for k in small[0]}
    grads.update(wp=big[0], wout=big[1], pw_w=big[2])
    return loss[0, 0], dx, grads, d_final


ANY = pl.BlockSpec(memory_space=pl.ANY)


def _position():
    return lax.axis_index("x"), lax.axis_index("y"), lax.axis_index("c")


def _other_chips(x, y):
    return [(1 - x, y), (x, 1 - y), (1 - x, 1 - y)]


def _gather_chips(shards, *, name):
    n = len(shards)
    kinds = 12

    def body(*refs):
        ins, outs = refs[:n], refs[n:2 * n]
        send, recv = refs[2 * n:]
        x, y, c = _position()
        me, sib = (x, y, c), (x, y, 1 - c)
        xn, yn, dg = (1 - x, y), (x, 1 - y), (1 - x, 1 - y)
        pa, pb = 2 * c, 2 * c + 1

        def copy(k, a, chip, layer, to, src=None):
            dst = outs[a].at[2 * chip[0] + chip[1], pl.ds(layer, 1)]
            return pltpu.make_async_remote_copy(
                src_ref=dst if src is None else src, dst_ref=dst, send_sem=send.at[k * n + a],
                recv_sem=recv.at[k * n + a], device_id=to, device_id_type=MESH)

        def own(k, a, layer, chip):
            return copy(k, a, (x, y), layer, (*chip, c), src=ins[a].at[pl.ds(layer, 1)])

        sends = []
        for a in range(n):
            sends += [own(0, a, pa, xn), own(1, a, pb, yn), own(2, a, pb, xn), own(3, a, pa, yn)]
        for cp in sends:
            cp.start()
        arrivals = [(1, yn, pb, (4, xn)), (0, xn, pa, (5, yn)), (2, xn, pb, None), (3, yn, pa, None),
                    (4, dg, pb, None), (5, dg, pa, None)]
        for a in range(n):
            for j, (k, chip, layer, onward) in enumerate(arrivals):
                copy(k, a, chip, layer, me).wait_recv()
                if onward is not None:
                    cp = copy(onward[0], a, chip, layer, (*onward[1], c))
                    cp.start()
                    sends.append(cp)
                cp = copy(6 + j, a, chip, layer, sib)
                cp.start()
                sends.append(cp)
        for a in range(n):
            for j, (k, chip, layer, onward) in enumerate(arrivals):
                copy(6 + j, a, chip, layer + 2 - 4 * c, me).wait_recv()
        for cp in sends:
            cp.wait_send()

    return pl.pallas_call(
        body, name=name, in_specs=[ANY] * n, out_specs=[ANY] * n,
        out_shape=[jax.ShapeDtypeStruct((4,) + s.shape, s.dtype) for s in shards],
        scratch_shapes=[pltpu.SemaphoreType.DMA((kinds * n,)), pltpu.SemaphoreType.DMA((kinds * n,))],
    )(*shards)


def _to_sibling(arrs, *, name):
    n = len(arrs)

    def body(*refs):
        ins, outs = refs[:n], refs[n:2 * n]
        send, recv = refs[2 * n:]
        x, y, c = _position()
        cps = [pltpu.make_async_remote_copy(src_ref=ins[a], dst_ref=outs[a], send_sem=send.at[a],
                                            recv_sem=recv.at[a], device_id=(x, y, 1 - c), device_id_type=MESH)
               for a in range(n)]
        for cp in cps:
            cp.start()
        for cp in cps:
            cp.wait()

    return pl.pallas_call(
        body, name=name, in_specs=[ANY] * n, out_specs=[ANY] * n,
        out_shape=[jax.ShapeDtypeStruct(a.shape, a.dtype) for a in arrs],
        scratch_shapes=[pltpu.SemaphoreType.DMA((n,)), pltpu.SemaphoreType.DMA((n,))],
    )(*arrs)


def _to_chips(arrs, *, name):
    n = len(arrs)

    def body(*refs):
        ins, outs = refs[:n], refs[n:2 * n]
        send, recv = refs[2 * n:]
        x, y, c = _position()
        cps = [pltpu.make_async_remote_copy(
            src_ref=ins[a].at[2 * chip[0] + chip[1]], dst_ref=outs[a].at[j], send_sem=send.at[j * n + a],
            recv_sem=recv.at[j * n + a], device_id=(*chip, c), device_id_type=MESH)
            for j, chip in enumerate(_other_chips(x, y)) for a in range(n)]
        for cp in cps:
            cp.start()
        for cp in cps:
            cp.wait()

    return pl.pallas_call(
        body, name=name, in_specs=[ANY] * n, out_specs=[ANY] * n,
        out_shape=[jax.ShapeDtypeStruct((3,) + a.shape[1:], a.dtype) for a in arrs],
        scratch_shapes=[pltpu.SemaphoreType.DMA((3 * n,)), pltpu.SemaphoreType.DMA((3 * n,))],
    )(*arrs)


def _join_halves(fulls, *, name):
    n = len(fulls)

    def body(*refs):
        ins, outs = refs[:n], refs[n:2 * n]
        send, recv = refs[2 * n:]
        x, y, c = _position()

        def copy(a, rows):
            return pltpu.make_async_remote_copy(
                src_ref=ins[a].at[rows], dst_ref=outs[a].at[rows], send_sem=send.at[a], recv_sem=recv.at[a],
                device_id=(x, y, 1 - c), device_id_type=MESH)

        cps = [copy(a, pl.ds(2 * c, 2)) for a in range(n)]
        for cp in cps:
            cp.start()
        for a in range(n):
            cps[a].wait_send()
            copy(a, pl.ds(2 * (1 - c), 2)).wait_recv()

    return pl.pallas_call(
        body, name=name, in_specs=[ANY] * n, out_specs=[ANY] * n,
        out_shape=[jax.ShapeDtypeStruct(f.shape, f.dtype) for f in fulls],
        scratch_shapes=[pltpu.SemaphoreType.DMA((n,)), pltpu.SemaphoreType.DMA((n,))],
        input_output_aliases={a: a for a in range(n)},
    )(*fulls)


def _allreduce_small(packed, *, name):
    rows = packed.shape[0]
    ndev = 8

    def body(x_ref, sum_ref, all_ref, send, recv, lsem):
        x, y, c = _position()
        me, sib = (x, y, c), (x, y, 1 - c)
        chips = _other_chips(x, y)

        def blk(px, py, pc):
            return all_ref.at[pl.ds((4 * px + 2 * py + pc) * rows, rows), :]

        def copy(k, block, to, src=None):
            return pltpu.make_async_remote_copy(
                src_ref=blk(*block) if src is None else src, dst_ref=blk(*block), send_sem=send.at[k],
                recv_sem=recv.at[k], device_id=to, device_id_type=MESH)

        mine = pltpu.make_async_copy(x_ref, blk(*me), lsem)
        mine.start()
        first = [copy(0, me, sib, src=x_ref)] + [copy(1 + j, me, (*chip, c), src=x_ref) for j, chip in enumerate(chips)]
        for cp in first:
            cp.start()
        passed = [copy(4 + j, (*chip, c), sib) for j, chip in enumerate(chips)]
        for j, chip in enumerate(chips):
            copy(1 + j, (*chip, c), me).wait_recv()
            passed[j].start()
        copy(0, sib, me).wait_recv()
        for j, chip in enumerate(chips):
            copy(4 + j, (*chip, 1 - c), me).wait_recv()
        for cp in first + passed:
            cp.wait_send()
        mine.wait()
        acc = all_ref[0:rows, :]
        for d in range(1, ndev):
            acc = acc + all_ref[d * rows:(d + 1) * rows, :]
        sum_ref[...] = acc

    vm = pl.BlockSpec(memory_space=pltpu.VMEM)
    return pl.pallas_call(
        body, name=name, in_specs=[vm], out_specs=vm, out_shape=jax.ShapeDtypeStruct((rows, 128), F32),
        scratch_shapes=[pltpu.VMEM((ndev * rows, 128), F32), pltpu.SemaphoreType.DMA((7,)),
                        pltpu.SemaphoreType.DMA((7,)), pltpu.SemaphoreType.DMA],
        compiler_params=pltpu.CompilerParams(vmem_limit_bytes=VMEM_LIMIT),
    )(packed)


def _pack(arrs):
    flat = jnp.concatenate([a.reshape(-1) for a in arrs])
    pad = (-flat.shape[0]) % 1024
    return jnp.pad(flat, (0, pad)).reshape(-1, 128)


def _unpack(packed, shapes):
    flat = packed.reshape(-1)
    out, pos = [], 0
    for s in shapes:
        size = math.prod(s)
        out.append(flat[pos:pos + size].reshape(s))
        pos += size
    return out


def _pad_cols(w):
    zeros = jnp.zeros(w.shape[:-1] + (IN_WP - IN_W,), w.dtype)
    return jnp.concatenate([w[..., :ORIG_BA], w[..., ORIG_ATT:], w[..., ORIG_BA:ORIG_ATT], zeros], axis=-1)


def _chip_cols(j):
    per = IN_W // 4
    lo, hi = j * per, (j + 1) * per
    out = []
    for o0, o1, p0 in ((0, ORIG_BA, 0), (ORIG_BA, ORIG_ATT, COL_BA), (ORIG_ATT, IN_W, ORIG_BA)):
        a, b = max(lo, o0), min(hi, o1)
        if a < b:
            out.append((p0 + a - o0, p0 + b - o0))
    return out


def _shards_to_padded(g):
    pieces = []
    for j in range(4):
        loc = 0
        for p0, p1 in _chip_cols(j):
            pieces.append((p0, g[j][..., loc:loc + p1 - p0]))
            loc += p1 - p0
    pieces.sort(key=lambda t: t[0])
    zeros = jnp.zeros(g.shape[1:-1] + (IN_WP - IN_W,), g.dtype)
    return jnp.concatenate([p for _, p in pieces] + [zeros], axis=-1)


def _padded_to_shards(g, dtype):
    return jnp.stack([jnp.concatenate([g[..., p0:p1] for p0, p1 in _chip_cols(j)], axis=-1).astype(dtype)
                      for j in range(4)])


def _unpad_cols(w):
    n_att = IN_W - ORIG_ATT
    return jnp.concatenate([w[..., :ORIG_BA], w[..., COL_BA:COL_BA + ORIG_ATT - ORIG_BA],
                            w[..., ORIG_BA:ORIG_BA + n_att]], axis=-1)


def _lanes(v, first):
    return jnp.pad(v, ((0, 0), (first, 128 - first - v.shape[1])))[:, None, :]


def _by_chip(g, axis):
    shape = g.shape[:axis] + (4, g.shape[axis] // 4) + g.shape[axis + 1:]
    return jnp.moveaxis(g.reshape(shape), axis, 0)


def kernel(x, norm_w, w_in, conv_qkv_w, a_log, dt_bias, gdn_norm_w, conf_dw_w, conf_dw_b, conf_ln_w, conf_ln_b, conf_pw_w, w_out, final_norm_w, loss_target, m_norm_w, m_w_in, m_conv_qkv_w, m_a_log, m_dt_bias, m_gdn_norm_w, m_conf_dw_w, m_conf_dw_b, m_conf_ln_w, m_conf_ln_b, m_conf_pw_w, m_w_out, m_final_norm_w, v_norm_w, v_w_in, v_conv_qkv_w, v_a_log, v_dt_bias, v_gdn_norm_w, v_conf_dw_w, v_conf_dw_b, v_conf_ln_w, v_conf_ln_b, v_conf_pw_w, v_w_out, v_final_norm_w):
    xi, yi, ci = _position()
    chip = 2 * xi + yi

    shards = [w_in.astype(BF16), w_out.astype(BF16), conf_pw_w.astype(BF16), conv_qkv_w, conf_dw_w]
    g_in, g_out, g_pw, g_cw, g_dw = [
        lax.dynamic_update_slice_in_dim(g, s[None], chip, axis=0)
        for g, s in zip(_gather_chips(shards, name="gather_weights"), shards)]
    cw_full = jnp.moveaxis(g_cw, 0, 2).reshape(DEPTH, SHORT_CONV, 3 * GDN_W)
    dw_full = jnp.moveaxis(g_dw, 0, 2).reshape(DEPTH, CONV_WIDTH, CONV_CH)
    params = dict(
        norm_w=norm_w[:, None, :],
        wp=_shards_to_padded(g_in),
        wout=jnp.moveaxis(g_out, 0, 1).reshape(DEPTH, D_MODEL, D_MODEL),
        pw_w=jnp.moveaxis(g_pw, 0, 1).reshape(DEPTH, CONV_CH, CONV_CH),
        cw=jnp.pad(cw_full, ((0, 0), (0, SHALO - SHORT_CONV), (0, 0))),
        dw_w=jnp.pad(dw_full, ((0, 0), (0, HALO - CONV_WIDTH), (0, 0))),
        al=_lanes(a_log, GDN_HEADS), dtb=_lanes(dt_bias, GDN_HEADS), gnw=gdn_norm_w[:, None, :],
        dw_b=conf_dw_b[:, None, :], ln_w=conf_ln_w[:, None, :], ln_b=conf_ln_b[:, None, :],
    )

    loss_part, grad_x, grads, d_final = _trunk(x[0], loss_target[0], params, final_norm_w[None, :])

    def half_by_chip(first, dtype):
        wp, wout, pw = [lax.dynamic_slice_in_dim(grads[k], first, 2, axis=0) for k in ("wp", "wout", "pw_w")]
        return [_padded_to_shards(wp, dtype), _by_chip(wout, 1).astype(dtype), _by_chip(pw, 1).astype(dtype)]

    keep = half_by_chip(2 * ci, F32)
    give = half_by_chip(2 * (1 - ci), BF16)
    got = _to_sibling(give, name="grads_to_sibling")
    pair = [_sum_arrays([k.reshape((8,) + k.shape[2:]), r.reshape((8,) + r.shape[2:])], name=f"pair_sum_{i}",
                        out_dtype=BF16).reshape(k.shape) for i, (k, r) in enumerate(zip(keep, got))]
    arrived = _to_chips(pair, name="grads_to_chips")
    halves = []
    for i, (pr, ar) in enumerate(zip(pair, arrived)):
        own = lax.dynamic_index_in_dim(pr, chip, axis=0, keepdims=False)
        halves.append(_sum_into_half([own, ar[0], ar[1], ar[2]], ci, name=f"chip_sum_{i}"))
    g_w_in, g_w_out, g_pw_w = _join_halves(halves, name="join_halves")

    ba = grads["ba_sums"]
    small = [grads["norm_w"], ba[:, 0:1, :], ba[:, 1:2, :], grads["gnw"], grads["ln_sums"][:, 2:3, :],
             grads["ln_sums"][:, 0:1, :], grads["ln_sums"][:, 1:2, :], d_final,
             grads["cw"][:, :SHORT_CONV, :], grads["dw_w"][:, :CONV_WIDTH, :], loss_part.reshape(1)]
    red = _unpack(_allreduce_small(_pack(small), name="allreduce_small"), [s.shape for s in small])
    loss = red[10][0]
    g_norm_w = red[0][:, 0, :]
    g_a_log = red[1][:, 0, GDN_HEADS:2 * GDN_HEADS]
    g_dt_bias = red[2][:, 0, GDN_HEADS:2 * GDN_HEADS]
    g_gnw, g_dw_b, g_ln_w, g_ln_b = red[3][:, 0, :], red[4][:, 0, :], red[5][:, 0, :], red[6][:, 0, :]
    g_final = red[7][0]
    g_cw = lax.dynamic_slice_in_dim(red[8], chip * (3 * GDN_W // 4), 3 * GDN_W // 4, axis=2)
    g_dw_w = lax.dynamic_slice_in_dim(red[9], chip * (CONV_CH // 4), CONV_CH // 4, axis=2)

    def cols_first(a):
        return jnp.transpose(a, (2, 0, 1))

    def cols_last(a):
        return jnp.transpose(a, (1, 2, 0))

    g_t = cols_first(g_w_in)
    g_w_in = cols_last(g_t)
    d_w_in, nm_w_in, nv_w_in = [cols_last(a) for a in _adamw(
        cols_first(w_in), g_t, cols_first(m_w_in), cols_first(v_w_in), name="adamw_w_in", by_lead=True)]
    d_w_out, nm_w_out, nv_w_out = _adamw(w_out, g_w_out, m_w_out, v_w_out, name="adamw_w_out")
    d_pw_w, nm_pw_w, nv_pw_w = _adamw(conf_pw_w, g_pw_w, m_conf_pw_w, v_conf_pw_w, name="adamw_pw")
    sw = [norm_w, a_log, dt_bias, gdn_norm_w, conf_dw_b, conf_ln_w, conf_ln_b, final_norm_w, conv_qkv_w, conf_dw_w]
    sg = [g_norm_w, g_a_log, g_dt_bias, g_gnw, g_dw_b, g_ln_w, g_ln_b, g_final, g_cw, g_dw_w]
    sm = [m_norm_w, m_a_log, m_dt_bias, m_gdn_norm_w, m_conf_dw_b, m_conf_ln_w, m_conf_ln_b, m_final_norm_w,
          m_conv_qkv_w, m_conf_dw_w]
    sv = [v_norm_w, v_a_log, v_dt_bias, v_gdn_norm_w, v_conf_dw_b, v_conf_ln_w, v_conf_ln_b, v_final_norm_w,
          v_conv_qkv_w, v_conf_dw_w]
    shapes = [a.shape for a in sw]
    packed = _adamw(_pack(sw)[None], _pack(sg)[None], _pack(sm)[None], _pack(sv)[None], name="adamw_small")
    sd, snm, snv = [_unpack(pk[0], shapes) for pk in packed]

    def order(big3, small10):
        s = small10
        return [s[0], big3[0], s[8], s[1], s[2], s[3], s[9], s[4], s[5], s[6], big3[2], big3[1], s[7]]

    return (loss, grad_x[None], *order([g_w_in, g_w_out, g_pw_w], sg),
            *order([d_w_in, d_w_out, d_pw_w], sd), *order([nm_w_in, nm_w_out, nm_pw_w], snm),
            *order([nv_w_in, nv_w_out, nv_pw_w], snv))
```

```python
import functools
import math

import jax
import jax.numpy as jnp
from jax import lax
from jax.experimental import pallas as pl
from jax.experimental.pallas import tpu as pltpu

F32, BF16 = jnp.float32, jnp.bfloat16
HIGHEST = lax.Precision.HIGHEST
MESH = pl.DeviceIdType.MESH

D_MODEL = 2048
DEPTH = 4
CONV_CH = 512
GDN_W = 768
GDN_HEADS = 6
GDN_D = 128
ATT_W = 768
ATT_HEADS = 12
ATT_HD = 64
CONV_WIDTH = 31
SHORT_CONV = 4
GDN_CHUNK = 64
ROPE_THETA = 500000.0
ROPE_DIM = ATT_HD // 4
DIL_PATTERNS = ((128, 1), (512, 4), (2048, 16))
ATT_BLOCK = 128
NEG_INF = -1e30
IN_W = 7692

IN_WP = 8192
COL_BA = 7680
ORIG_BA = 4608
ORIG_ATT = 4620

ADAM_LR = 0.001
ADAM_B1 = 0.9
ADAM_B2 = 0.999
ADAM_EPS = 1e-08
ADAM_WD = 0.01
ADAM_STEP = 10

VMEM_LIMIT = 56 * 1024 * 1024


def _params(sem=None):
    return pltpu.CompilerParams(dimension_semantics=sem, vmem_limit_bytes=VMEM_LIMIT)


def _sigmoid(x):
    return 0.5 * jnp.tanh(0.5 * x) + 0.5


def _silu(x):
    return x * _sigmoid(x)


def _dsilu(x):
    s = _sigmoid(x)
    return s * (1.0 + x * (1.0 - s))


def _dot(a, b, dims, precision=None):
    return lax.dot_general(a, b, (dims, ((), ())), precision=precision, preferred_element_type=F32)


def _nn(a, b, precision=None):
    return _dot(a, b, ((1,), (0,)), precision)


def _nt(a, b, precision=None):
    return _dot(a, b, ((1,), (1,)), precision)


def _tn(a, b, precision=None):
    return _dot(a, b, ((0,), (0,)), precision)


def _matmul(a, b, *, name, ta=False, tb=False, out_dtype=F32, add=None, stack=None, tm=1024, tn=1024, tk=1024):
    if ta:
        k_dim, m_dim = a.shape
    else:
        m_dim, k_dim = a.shape
    n_dim = b.shape[0] if tb else b.shape[1]
    tm, tn, tk = min(tm, m_dim), min(tn, n_dim), min(tk, k_dim)
    assert m_dim % tm == 0 and n_dim % tn == 0 and k_dim % tk == 0, (name, a.shape, b.shape)
    nk = k_dim // tk
    a_spec = pl.BlockSpec((tk, tm), lambda i, j, k: (k, i)) if ta else pl.BlockSpec((tm, tk), lambda i, j, k: (i, k))
    b_spec = pl.BlockSpec((tn, tk), lambda i, j, k: (j, k)) if tb else pl.BlockSpec((tk, tn), lambda i, j, k: (k, j))
    o_spec = pl.BlockSpec((tm, tn), lambda i, j, k: (i, j))
    out_shape = jax.ShapeDtypeStruct((m_dim, n_dim), out_dtype)
    dims = ((0 if ta else 1,), (1 if tb else 0,))
    has_add = add is not None
    ins = [a, b] + ([add] if has_add else [])
    specs = [a_spec, b_spec] + ([o_spec] if has_add else [])
    aliases = {}
    if stack is not None:
        buf, slab, nslabs = stack
        o_spec = pl.BlockSpec((None, tm, tn), lambda i, j, k: (slab, i, j))
        out_shape = jax.ShapeDtypeStruct((nslabs, m_dim, n_dim), out_dtype)
        if buf is not None:
            aliases = {len(ins): 0}
            ins.append(buf)
            specs.append(pl.BlockSpec(memory_space=pl.ANY))
    n_in = len(ins)

    def body(*refs):
        a_ref, b_ref = refs[0], refs[1]
        o_ref = refs[n_in]

        def finish(r):
            if has_add:
                r = r + refs[2][...]
            o_ref[...] = r.astype(out_dtype)

        prod = _dot(a_ref[...].astype(BF16), b_ref[...].astype(BF16), dims)
        if nk == 1:
            finish(prod)
            return
        acc_ref = refs[n_in + 1]
        k = pl.program_id(2)

        @pl.when(k == 0)
        def _():
            acc_ref[...] = prod

        @pl.when(k > 0)
        def _():
            acc_ref[...] += prod

        @pl.when(k == nk - 1)
        def _():
            finish(acc_ref[...])

    return pl.pallas_call(
        body, name=name, grid=(m_dim // tm, n_dim // tn, nk), in_specs=specs, out_specs=o_spec,
        out_shape=out_shape, scratch_shapes=[pltpu.VMEM((tm, tn), F32)] if nk > 1 else [],
        input_output_aliases=aliases,
        compiler_params=_params(("parallel", "parallel", "arbitrary")),
    )(*ins)


def _rms_fwd(x, w, *, name, tm=512):
    s_len, d = x.shape

    def body(x_ref, w_ref, h_ref):
        xv = x_ref[...]
        r = lax.rsqrt(jnp.mean(xv * xv, axis=-1, keepdims=True) + 1e-6)
        h_ref[...] = (xv * r * w_ref[...]).astype(BF16)

    return pl.pallas_call(
        body, name=name, grid=(s_len // tm,),
        in_specs=[pl.BlockSpec((tm, d), lambda i: (i, 0)), pl.BlockSpec((1, d), lambda i: (0, 0))],
        out_specs=pl.BlockSpec((tm, d), lambda i: (i, 0)),
        out_shape=jax.ShapeDtypeStruct((s_len, d), BF16),
        compiler_params=_params(("parallel",)),
    )(x, w)


def _rms_bwd(x, dh, w, dres, *, name, tm=512):
    s_len, d = x.shape
    nsteps = s_len // tm

    def body(x_ref, dh_ref, w_ref, dres_ref, dx_ref, dw_ref, acc_ref):
        i = pl.program_id(0)

        @pl.when(i == 0)
        def _():
            acc_ref[...] = jnp.zeros_like(acc_ref)

        xv = x_ref[...]
        r = lax.rsqrt(jnp.mean(xv * xv, axis=-1, keepdims=True) + 1e-6)
        xn = xv * r
        dy = dh_ref[...]
        dxn = dy * w_ref[...]
        dx_ref[...] = dres_ref[...] + r * (dxn - xn * jnp.mean(dxn * xn, axis=-1, keepdims=True))
        acc_ref[...] += (dy * xn).reshape(tm // 8, 8, d).sum(axis=0)

        @pl.when(i == nsteps - 1)
        def _():
            dw_ref[...] = jnp.sum(acc_ref[...], axis=0, keepdims=True)

    row = pl.BlockSpec((tm, d), lambda i: (i, 0))
    vec = pl.BlockSpec((1, d), lambda i: (0, 0))
    return pl.pallas_call(
        body, name=name, grid=(nsteps,), in_specs=[row, row, vec, row], out_specs=[row, vec],
        out_shape=[jax.ShapeDtypeStruct((s_len, d), F32), jax.ShapeDtypeStruct((1, d), F32)],
        scratch_shapes=[pltpu.VMEM((8, d), F32)],
        compiler_params=_params(("arbitrary",)),
    )(x, dh, w, dres)


def _loss_head(x, w, target, *, name, tm=256):
    s_len, d = x.shape
    nsteps = s_len // tm

    def body(x_ref, w_ref, t_ref, dx_ref, dw_ref, loss_ref, acc_ref, lacc_ref):
        i = pl.program_id(0)

        @pl.when(i == 0)
        def _():
            acc_ref[...] = jnp.zeros_like(acc_ref)
            lacc_ref[...] = jnp.zeros_like(lacc_ref)

        xv = x_ref[...]
        wv = w_ref[...]
        r = lax.rsqrt(jnp.mean(xv * xv, axis=-1, keepdims=True) + 1e-6)
        xn = xv * r
        err = xn * wv - t_ref[...]
        lacc_ref[...] += (err * err).reshape(tm // 8, 8, d).sum(axis=0)
        dy = err * (1.0 / d)
        dxn = dy * wv
        dx_ref[...] = r * (dxn - xn * jnp.mean(dxn * xn, axis=-1, keepdims=True))
        acc_ref[...] += (dy * xn).reshape(tm // 8, 8, d).sum(axis=0)

        @pl.when(i == nsteps - 1)
        def _():
            dw_ref[...] = jnp.sum(acc_ref[...], axis=0, keepdims=True)
            tot = jnp.sum(jnp.sum(lacc_ref[...], axis=0, keepdims=True), axis=1, keepdims=True)
            loss_ref[...] = jnp.broadcast_to(tot * (0.5 / d), (1, 128))

    row = pl.BlockSpec((tm, d), lambda i: (i, 0))
    vec = pl.BlockSpec((1, d), lambda i: (0, 0))
    return pl.pallas_call(
        body, name=name, grid=(nsteps,), in_specs=[row, vec, row],
        out_specs=[row, vec, pl.BlockSpec((1, 128), lambda i: (0, 0))],
        out_shape=[jax.ShapeDtypeStruct((s_len, d), F32), jax.ShapeDtypeStruct((1, d), F32),
                   jax.ShapeDtypeStruct((1, 128), F32)],
        scratch_shapes=[pltpu.VMEM((8, d), F32), pltpu.VMEM((8, d), F32)],
        compiler_params=_params(("arbitrary",)),
    )(x, w, target)


def _rows_block(shape, tr=256):
    lead, rows, cols = shape
    if rows % tr != 0:
        assert rows * cols <= 1 << 20, shape
        tr = rows
    return (lead, rows // tr), pl.BlockSpec((1, tr, cols), lambda a, i: (a, i, 0))


LEAD_BLOCK = 64


def _adamw(w, g, m, v, *, name, by_lead=False):
    if by_lead:
        lead, rows, cols = w.shape
        grid = (pl.cdiv(lead, LEAD_BLOCK), 1)
        spec = pl.BlockSpec((LEAD_BLOCK, rows, cols), lambda a, i: (a, 0, 0))
    else:
        grid, spec = _rows_block(w.shape)
    c1 = 1.0 / (1.0 - ADAM_B1 ** ADAM_STEP)
    c2 = 1.0 / (1.0 - ADAM_B2 ** ADAM_STEP)

    def body(w_ref, g_ref, m_ref, v_ref, d_ref, nm_ref, nv_ref):
        gv = g_ref[...]
        nm = ADAM_B1 * m_ref[...] + (1.0 - ADAM_B1) * gv
        nv = ADAM_B2 * v_ref[...] + (1.0 - ADAM_B2) * (gv * gv)
        nm_ref[...] = nm
        nv_ref[...] = nv
        d_ref[...] = -ADAM_LR * ((nm * c1) / (jnp.sqrt(nv * c2) + ADAM_EPS) + ADAM_WD * w_ref[...])

    out = jax.ShapeDtypeStruct(w.shape, F32)
    return pl.pallas_call(
        body, name=name, grid=grid, in_specs=[spec] * 4, out_specs=[spec] * 3, out_shape=[out] * 3,
        compiler_params=_params(("parallel", "parallel")),
    )(w, g, m, v)


def _sum_into_half(arrs, half, *, name):
    lead, rows, cols = arrs[0].shape
    assert lead == 2
    (_, nr), spec0 = _rows_block(arrs[0].shape)
    tr = spec0.block_shape[1]
    n = len(arrs)

    def body(half_ref, *refs):
        del half_ref
        acc = refs[0][...].astype(F32)
        for r in refs[1:n]:
            acc = acc + r[...].astype(F32)
        refs[n][...] = acc

    spec = pl.BlockSpec((1, tr, cols), lambda a, i, h: (a, i, 0))
    return pl.pallas_call(
        body, name=name,
        grid_spec=pltpu.PrefetchScalarGridSpec(
            num_scalar_prefetch=1, grid=(2, nr), in_specs=[spec] * n,
            out_specs=pl.BlockSpec((1, tr, cols), lambda a, i, h: (2 * h[0] + a, i, 0))),
        out_shape=jax.ShapeDtypeStruct((4, rows, cols), F32),
        compiler_params=_params(("parallel", "parallel")),
    )(jnp.reshape(half, (1,)).astype(jnp.int32), *arrs)


def _sum_arrays(arrs, *, name, out_dtype):
    grid, spec = _rows_block(arrs[0].shape)
    n = len(arrs)

    def body(*refs):
        acc = refs[0][...].astype(F32)
        for r in refs[1:n]:
            acc = acc + r[...].astype(F32)
        refs[n][...] = acc.astype(out_dtype)

    return pl.pallas_call(
        body, name=name, grid=grid, in_specs=[spec] * n, out_specs=spec,
        out_shape=jax.ShapeDtypeStruct(arrs[0].shape, out_dtype),
        compiler_params=_params(("parallel", "parallel")),
    )(*arrs)


HALO = 32


def _shifted_windows(buf, tm, offsets):
    rows = buf.shape[0]
    for b in range(8):
        group = [(k, s) for k, s in enumerate(offsets) if s % 8 == b]
        if not group:
            continue
        rb = buf if b == 0 else pltpu.roll(buf, rows - b, 0)
        for k, s in group:
            yield k, rb[s - b:s - b + tm, :]


def _conf_fwd(u, dw_w, dw_b, ln_w, ln_b, *, name, tm=256):
    s_len = u.shape[0]
    c = CONV_CH

    def body(uc_ref, up_ref, dww_ref, dwb_ref, lnw_ref, lnb_ref, conv_ref, sw_ref, hbuf):
        i = pl.program_id(0)
        hbuf[HALO:, :] = uc_ref[:, :c] * _sigmoid(uc_ref[:, c:])
        hp = up_ref[:, :c] * _sigmoid(up_ref[:, c:])
        hbuf[:HALO, :] = jnp.where(i > 0, hp, 0.0)
        for cb in range(c // 128):
            cs = slice(128 * cb, 128 * (cb + 1))
            acc = jnp.zeros((tm, 128), F32)
            taps = [HALO - CONV_WIDTH + 1 + j for j in range(CONV_WIDTH)]
            for j, win in _shifted_windows(hbuf[:, cs], tm, taps):
                acc = acc + win * dww_ref[j:j + 1, cs]
            conv_ref[:, cs] = acc + dwb_ref[:, cs]
        cv = conv_ref[...]
        mu = jnp.mean(cv, axis=-1, keepdims=True)
        xc = cv - mu
        var = jnp.mean(xc * xc, axis=-1, keepdims=True)
        ln = xc * lax.rsqrt(var + 1e-5) * lnw_ref[...] + lnb_ref[...]
        sw_ref[...] = _silu(ln).astype(BF16)

    vec = pl.BlockSpec((1, c), lambda i: (0, 0))
    return pl.pallas_call(
        body, name=name, grid=(s_len // tm,),
        in_specs=[pl.BlockSpec((tm, 2 * c), lambda i: (i, 0)),
                  pl.BlockSpec((HALO, 2 * c), lambda i: (jnp.maximum(i * (tm // HALO) - 1, 0), 0)),
                  pl.BlockSpec((HALO, c), lambda i: (0, 0)), vec, vec, vec],
        out_specs=[pl.BlockSpec((tm, c), lambda i: (i, 0))] * 2,
        out_shape=[jax.ShapeDtypeStruct((s_len, c), F32), jax.ShapeDtypeStruct((s_len, c), BF16)],
        scratch_shapes=[pltpu.VMEM((tm + HALO, c), F32)],
        compiler_params=_params(("parallel",)),
    )(u, u, dw_w, dw_b, ln_w, ln_b)


def _conf_bwd_ln(d_sw, conv, ln_w, ln_b, *, name, tm=256):
    s_len, c = conv.shape
    nsteps = s_len // tm

    def body(dsw_ref, conv_ref, lnw_ref, lnb_ref, dconv_ref, sums_ref):
        i = pl.program_id(0)

        @pl.when(i == 0)
        def _():
            sums_ref[...] = jnp.zeros_like(sums_ref)

        cv = conv_ref[...]
        mu = jnp.mean(cv, axis=-1, keepdims=True)
        xc = cv - mu
        rs = lax.rsqrt(jnp.mean(xc * xc, axis=-1, keepdims=True) + 1e-5)
        xhat = xc * rs
        lnw = lnw_ref[...]
        ln = xhat * lnw + lnb_ref[...]
        dln = dsw_ref[...] * _dsilu(ln)
        dxh = dln * lnw
        dconv = rs * (dxh - jnp.mean(dxh, axis=-1, keepdims=True)
                      - xhat * jnp.mean(dxh * xhat, axis=-1, keepdims=True))
        dconv_ref[...] = dconv
        sums_ref[0:1, :] += jnp.sum(dln * xhat, axis=0, keepdims=True)
        sums_ref[1:2, :] += jnp.sum(dln, axis=0, keepdims=True)
        sums_ref[2:3, :] += jnp.sum(dconv, axis=0, keepdims=True)

    row = pl.BlockSpec((tm, c), lambda i: (i, 0))
    vec = pl.BlockSpec((1, c), lambda i: (0, 0))
    return pl.pallas_call(
        body, name=name, grid=(nsteps,), in_specs=[row, row, vec, vec],
        out_specs=[row, pl.BlockSpec((8, c), lambda i: (0, 0))],
        out_shape=[jax.ShapeDtypeStruct((s_len, c), F32), jax.ShapeDtypeStruct((8, c), F32)],
        compiler_params=_params(("arbitrary",)),
    )(d_sw, conv, ln_w, ln_b)


def _conf_bwd_conv(u, dconv, dw_w, du, *, name, tm=256):
    s_len = u.shape[0]
    c = CONV_CH
    nsteps = s_len // tm
    off = HALO - CONV_WIDTH + 1

    def body(uc_ref, up_ref, dc_ref, dn_ref, dww_ref, du_in_ref, du_ref, ddw_ref, hbuf, dbuf, wacc):
        del du_in_ref
        i = pl.program_id(0)

        @pl.when(i == 0)
        def _():
            wacc[...] = jnp.zeros_like(wacc)

        hbuf[HALO:, :] = uc_ref[:, :c] * _sigmoid(uc_ref[:, c:])
        hp = up_ref[:, :c] * _sigmoid(up_ref[:, c:])
        hbuf[:HALO, :] = jnp.where(i > 0, hp, 0.0)
        dbuf[:tm, :] = dc_ref[...]
        dbuf[tm:, :] = jnp.where(i < nsteps - 1, dn_ref[...], 0.0)
        for cb in range(c // 128):
            cs = slice(128 * cb, 128 * (cb + 1))
            dcur = dbuf[0:tm, cs]
            acc = jnp.zeros((tm, 128), F32)
            for k, win in _shifted_windows(dbuf[:, cs], tm, list(range(CONV_WIDTH))):
                j = CONV_WIDTH - 1 - k
                acc = acc + win * dww_ref[j:j + 1, cs]
            for j, win in _shifted_windows(hbuf[:, cs], tm, [off + j for j in range(CONV_WIDTH)]):
                wacc[j, :, cs] += (win * dcur).reshape(tm // 8, 8, 128).sum(axis=0)
            a = uc_ref[:, cs]
            sg = _sigmoid(uc_ref[:, c + 128 * cb:c + 128 * (cb + 1)])
            du_ref[:, cs] = (acc * sg).astype(du_ref.dtype)
            du_ref[:, c + 128 * cb:c + 128 * (cb + 1)] = (acc * a * sg * (1.0 - sg)).astype(du_ref.dtype)

        @pl.when(i == nsteps - 1)
        def _():
            for j in range(CONV_WIDTH):
                ddw_ref[j:j + 1, :] = jnp.sum(wacc[j], axis=0, keepdims=True)
            ddw_ref[CONV_WIDTH:, :] = jnp.zeros((HALO - CONV_WIDTH, c), F32)

    return pl.pallas_call(
        body, name=name, grid=(nsteps,),
        in_specs=[pl.BlockSpec((tm, 2 * c), lambda i: (i, 0)),
                  pl.BlockSpec((HALO, 2 * c), lambda i: (jnp.maximum(i * (tm // HALO) - 1, 0), 0)),
                  pl.BlockSpec((tm, c), lambda i: (i, 0)),
                  pl.BlockSpec((HALO, c), lambda i: (jnp.minimum((i + 1) * (tm // HALO), s_len // HALO - 1), 0)),
                  pl.BlockSpec((HALO, c), lambda i: (0, 0)),
                  pl.BlockSpec(memory_space=pl.ANY)],
        out_specs=[pl.BlockSpec((tm, 2 * c), lambda i: (i, 0)), pl.BlockSpec((HALO, c), lambda i: (0, 0))],
        out_shape=[jax.ShapeDtypeStruct(du.shape, du.dtype), jax.ShapeDtypeStruct((HALO, c), F32)],
        scratch_shapes=[pltpu.VMEM((tm + HALO, c), F32), pltpu.VMEM((tm + HALO, c), F32),
                        pltpu.VMEM((CONV_WIDTH, 8, c), F32)],
        input_output_aliases={5: 0},
        compiler_params=_params(("arbitrary",)),
    )(u, u, dconv, dconv, dw_w, du)


COL_GQ = 1536 // GDN_W
COL_AQ = 4608 // ATT_W
SHALO = 8
SCAN_CHUNKS = 8
INTRA_CHUNKS = 4


def _softplus(z):
    return jnp.maximum(z, 0.0) + jnp.log1p(jnp.exp(-jnp.abs(z)))


def _short_conv(buf, cw_ref, part, rows, first):
    acc = jnp.zeros((rows, GDN_W), F32)
    for j, win in _shifted_windows(buf[...], rows, [first + j for j in range(SHORT_CONV)]):
        acc = acc + win * cw_ref[j:j + 1, GDN_W * part:GDN_W * (part + 1)]
    return acc


def _gdn_prep_fwd(u, cw, al, dtb, *, name, tm=256):
    s_len = u.shape[0]
    first = SHALO - SHORT_CONV + 1

    def body(uq, uk, uv, pq, pk, pv, uba, cw_ref, al_ref, dtb_ref, qn_ref, kn_ref, vc_ref, bg_ref, buf):
        i = pl.program_id(0)

        def conv(cur, prev, part):
            buf[SHALO:, :] = cur[...]
            buf[:SHALO, :] = jnp.where(i > 0, prev[...], 0.0)
            return _silu(_short_conv(buf, cw_ref, part, tm, first))

        for part, (cur, prev, out, scale) in enumerate(
                ((uq, pq, qn_ref, GDN_D ** -0.5), (uk, pk, kn_ref, 1.0))):
            y = conv(cur, prev, part)
            for h in range(GDN_HEADS):
                hs = slice(GDN_D * h, GDN_D * (h + 1))
                yh = y[:, hs]
                out[:, hs] = yh * (lax.rsqrt(jnp.sum(yh * yh, axis=-1, keepdims=True) + 1e-6) * scale)
        vc_ref[...] = conv(uv, pv, 2)
        ba = uba[...]
        lane = lax.broadcasted_iota(jnp.int32, ba.shape, 1)
        g = -jnp.exp(al_ref[...]) * _softplus(ba + dtb_ref[...])
        bg_ref[...] = jnp.where(lane < GDN_HEADS, _sigmoid(ba), jnp.where(lane < 2 * GDN_HEADS, g, 0.0))

    def cur(col):
        return pl.BlockSpec((tm, GDN_W), lambda i: (i, col))

    def prev(col):
        return pl.BlockSpec((SHALO, GDN_W), lambda i: (jnp.maximum(i * (tm // SHALO) - 1, 0), col))

    vec = pl.BlockSpec((1, 128), lambda i: (0, 0))
    row = pl.BlockSpec((tm, GDN_W), lambda i: (i, 0))
    wide = jax.ShapeDtypeStruct((s_len, GDN_W), F32)
    return pl.pallas_call(
        body, name=name, grid=(s_len // tm,),
        in_specs=[cur(COL_GQ), cur(COL_GQ + 1), cur(COL_GQ + 2), prev(COL_GQ), prev(COL_GQ + 1), prev(COL_GQ + 2),
                  pl.BlockSpec((tm, 128), lambda i: (i, COL_BA // 128)),
                  pl.BlockSpec((SHALO, 3 * GDN_W), lambda i: (0, 0)), vec, vec],
        out_specs=[row, row, row, pl.BlockSpec((tm, 128), lambda i: (i, 0))],
        out_shape=[wide, wide, wide, jax.ShapeDtypeStruct((s_len, 128), F32)],
        scratch_shapes=[pltpu.VMEM((tm + SHALO, GDN_W), F32)],
        compiler_params=_params(("parallel",)),
    )(u, u, u, u, u, u, u, cw, al, dtb)


def _chunk_masks():
    c = GDN_CHUNK
    row = lax.broadcasted_iota(jnp.int32, (c, c), 0)
    col = lax.broadcasted_iota(jnp.int32, (c, c), 1)
    return row >= col, row > col


def _cum_decay(bg):
    c = GDN_CHUNK
    causal, _ = _chunk_masks()
    g_cum = _nn(causal.astype(F32), bg, HIGHEST)
    sel = (lax.broadcasted_iota(jnp.int32, (8, 128), 0) + GDN_HEADS
           == lax.broadcasted_iota(jnp.int32, (8, 128), 1)).astype(F32)
    return g_cum, _nt(sel, g_cum, HIGHEST)


def _bdot(a, b, ca, cb):
    return lax.dot_general(a, b, (((ca,), (cb,)), ((0,), (0,))), preferred_element_type=F32)


def _bnn(a, b):
    return _bdot(a, b, 2, 1)


def _bnt(a, b):
    return _bdot(a, b, 2, 2)


def _btn(a, b):
    return _bdot(a, b, 1, 1)


def _split(a):
    hi = a.astype(BF16)
    return hi, (a - hi.astype(F32)).astype(BF16)


def _bnn3(a, b):
    ah, al = _split(a)
    bh, bl = _split(b)
    return _bnn(ah, bh) + (_bnn(al, bh) + _bnn(ah, bl))


def _heads(ref, rows=slice(None)):
    return jnp.stack([ref[rows, GDN_D * h:GDN_D * (h + 1)] for h in range(GDN_HEADS)])


def _head_columns(a, first):
    return jnp.stack([a[:, first + h:first + h + 1] for h in range(GDN_HEADS)])


def _chunk_decay(g_cum, g_rows, bg):
    causal, _ = _chunk_masks()
    gc = _head_columns(g_cum, GDN_HEADS)
    gr = jnp.stack([g_rows[h:h + 1, :] for h in range(GDN_HEADS)])
    dec = jnp.where(causal, jnp.exp(jnp.where(causal, gc - gr, 0.0)), 0.0)
    return gc, _head_columns(bg, 0), dec


def _gdn_intra_fwd(qn, kn, vc, bg, *, name):
    s_len = qn.shape[0]
    c = GDN_CHUNK
    nch = INTRA_CHUNKS
    nsteps = s_len // (c * nch)

    def body(q_ref, k_ref, v_ref, bg_ref, wk_ref, wv_ref, qd_ref, kd_ref, p_ref, t_ref, g_ref):
        causal, strict = _chunk_masks()
        eye = (lax.broadcasted_iota(jnp.int32, (c, c), 0) == lax.broadcasted_iota(jnp.int32, (c, c), 1)).astype(F32)
        parts = []
        for ch in range(nch):
            rs = slice(c * ch, c * (ch + 1))
            bg = bg_ref[rs, :]
            g_cum, g_rows = _cum_decay(bg)
            g_ref[rs, :] = g_cum
            parts.append(_chunk_decay(g_cum, g_rows, bg) + (_heads(q_ref, rs), _heads(k_ref, rs), _heads(v_ref, rs)))
        gc, bc, dec, q, k, v = [jnp.concatenate([p[i] for p in parts], axis=0) for i in range(6)]
        k16 = k.astype(BF16)
        low = jnp.where(strict, bc * _bnt(k16, k16) * dec, 0.0)
        pw = -low
        t = eye + pw
        for _ in range(5):
            pw = _bnn3(pw, pw)
            t = t + _bnn3(t, pw)
        t16 = t.astype(BF16)
        eg = jnp.exp(gc)
        wk = _bnn(t16, (k * (bc * eg)).astype(BF16))
        wv = _bnn(t16, (v * bc).astype(BF16))
        pm = jnp.where(causal, _bnt(q.astype(BF16), k16) * dec, 0.0).astype(BF16)
        qd = q * eg
        kd = k * jnp.exp(gc[:, c - 1:c, :] - gc)
        for idx in range(nch * GDN_HEADS):
            ch, h = divmod(idx, GDN_HEADS)
            rs = slice(c * ch, c * (ch + 1))
            hs = slice(GDN_D * h, GDN_D * (h + 1))
            t_ref[h, rs, :] = t[idx]
            p_ref[h, rs, :] = pm[idx]
            wk_ref[rs, hs] = wk[idx].astype(BF16)
            wv_ref[rs, hs] = wv[idx]
            qd_ref[rs, hs] = qd[idx].astype(BF16)
            kd_ref[rs, hs] = kd[idx].astype(BF16)

    row = pl.BlockSpec((c * nch, GDN_W), lambda n: (n, 0))
    sq = pl.BlockSpec((GDN_HEADS, c * nch, c), lambda n: (0, n, 0))
    narrow = pl.BlockSpec((c * nch, 128), lambda n: (n, 0))
    w16 = jax.ShapeDtypeStruct((s_len, GDN_W), BF16)
    return pl.pallas_call(
        body, name=name, grid=(nsteps,), in_specs=[row, row, row, narrow],
        out_specs=[row, row, row, row, sq, sq, narrow],
        out_shape=[w16, jax.ShapeDtypeStruct((s_len, GDN_W), F32), w16, w16,
                   jax.ShapeDtypeStruct((GDN_HEADS, s_len, c), BF16),
                   jax.ShapeDtypeStruct((GDN_HEADS, s_len, c), F32),
                   jax.ShapeDtypeStruct((s_len, 128), F32)],
        compiler_params=_params(("parallel",)),
    )(qn, kn, vc, bg)


def _gdn_scan_fwd(wk, wv, qd, kd, p, g_cum, *, name):
    s_len = wk.shape[0]
    c = GDN_CHUNK
    nchunks = s_len // c
    nch = SCAN_CHUNKS

    def body(wk_ref, wv_ref, qd_ref, kd_ref, p_ref, g_ref, o_ref, vn_ref, sp_ref, st):
        @pl.when(pl.program_id(0) == 0)
        def _():
            st[...] = jnp.zeros_like(st)

        s = st[...]
        for ch in range(nch):
            rs = slice(c * ch, c * (ch + 1))
            sp_ref[ch] = s
            s16 = s.astype(BF16)
            vn16 = (_heads(wv_ref, rs) - _bnn(_heads(wk_ref, rs), s16)).astype(BF16)
            o = _bnn(_heads(qd_ref, rs), s16) + _bnn(p_ref[:, rs, :], vn16)
            gl = jnp.exp(_head_columns(g_ref[c * ch + c - 1:c * ch + c, :], GDN_HEADS))
            s = s * gl + _btn(_heads(kd_ref, rs), vn16)
            for h in range(GDN_HEADS):
                hs = slice(GDN_D * h, GDN_D * (h + 1))
                vn_ref[rs, hs] = vn16[h]
                o_ref[rs, hs] = o[h]
        st[...] = s

    row = pl.BlockSpec((c * nch, GDN_W), lambda n: (n, 0))
    return pl.pallas_call(
        body, name=name, grid=(nchunks // nch,),
        in_specs=[row, row, row, row, pl.BlockSpec((GDN_HEADS, c * nch, c), lambda n: (0, n, 0)),
                  pl.BlockSpec((c * nch, 128), lambda n: (n, 0))],
        out_specs=[row, row, pl.BlockSpec((nch, GDN_HEADS, GDN_D, GDN_D), lambda n: (n, 0, 0, 0))],
        out_shape=[jax.ShapeDtypeStruct((s_len, GDN_W), F32), jax.ShapeDtypeStruct((s_len, GDN_W), BF16),
                   jax.ShapeDtypeStruct((nchunks, GDN_HEADS, GDN_D, GDN_D), F32)],
        scratch_shapes=[pltpu.VMEM((GDN_HEADS, GDN_D, GDN_D), F32)],
        compiler_params=_params(("arbitrary",)),
    )(wk, wv, qd, kd, p, g_cum)


def _gdn_scan_bwd(do, wk, qd, kd, p, g_cum, *, name):
    s_len = wk.shape[0]
    c = GDN_CHUNK
    nchunks = s_len // c
    nch = SCAN_CHUNKS

    def body(do_ref, wk_ref, qd_ref, kd_ref, p_ref, g_ref, dvn_ref, ds_ref, dst):
        @pl.when(pl.program_id(0) == 0)
        def _():
            dst[...] = jnp.zeros_like(dst)

        ds = dst[...]
        for ch in reversed(range(nch)):
            rs = slice(c * ch, c * (ch + 1))
            ds_ref[ch] = ds
            do16 = _heads(do_ref, rs).astype(BF16)
            dvn16 = (_btn(p_ref[:, rs, :], do16) + _bnn(_heads(kd_ref, rs), ds.astype(BF16))).astype(BF16)
            gl = jnp.exp(_head_columns(g_ref[c * ch + c - 1:c * ch + c, :], GDN_HEADS))
            ds = _btn(_heads(qd_ref, rs), do16) + ds * gl - _btn(_heads(wk_ref, rs), dvn16)
            for h in range(GDN_HEADS):
                dvn_ref[rs, GDN_D * h:GDN_D * (h + 1)] = dvn16[h]
        dst[...] = ds

    last = nchunks // nch - 1
    row = pl.BlockSpec((c * nch, GDN_W), lambda n: (last - n, 0))
    return pl.pallas_call(
        body, name=name, grid=(nchunks // nch,),
        in_specs=[row, row, row, row, pl.BlockSpec((GDN_HEADS, c * nch, c), lambda n: (0, last - n, 0)),
                  pl.BlockSpec((c * nch, 128), lambda n: (last - n, 0))],
        out_specs=[row, pl.BlockSpec((nch, GDN_HEADS, GDN_D, GDN_D), lambda n: (last - n, 0, 0, 0))],
        out_shape=[jax.ShapeDtypeStruct((s_len, GDN_W), BF16),
                   jax.ShapeDtypeStruct((nchunks, GDN_HEADS, GDN_D, GDN_D), F32)],
        scratch_shapes=[pltpu.VMEM((GDN_HEADS, GDN_D, GDN_D), F32)],
        compiler_params=_params(("arbitrary",)),
    )(do, wk, qd, kd, p, g_cum)


def _gdn_intra_bwd(qn, kn, vc, bg, g_cum, t, do, dvn, vn, sprev, ds_all, *, name):
    s_len = qn.shape[0]
    c = GDN_CHUNK
    nch = INTRA_CHUNKS
    nsteps = s_len // (c * nch)
    nb = nch * GDN_HEADS

    def body(q_ref, k_ref, v_ref, bg_ref, g_ref, t_ref, do_ref, dvn_ref, vn_ref, sp_ref, ds_ref,
             dqkv_ref, dbg_ref):
        causal, strict = _chunk_masks()
        lane = lax.broadcasted_iota(jnp.int32, (c, 128), 1)
        rowi = lax.broadcasted_iota(jnp.int32, (c, 128), 0)
        parts = []
        for ch in range(nch):
            rs = slice(c * ch, c * (ch + 1))
            bg = bg_ref[rs, :]
            _, g_rows = _cum_decay(bg)
            parts.append(_chunk_decay(g_ref[rs, :], g_rows, bg) + tuple(
                _heads(r, rs) for r in (q_ref, k_ref, v_ref, do_ref, dvn_ref, vn_ref)) + (t_ref[:, rs, :],))
        gc, bc, dec, q, k, v, do, dvn16, vn16, tm = [jnp.concatenate([p[i] for p in parts], axis=0)
                                                     for i in range(10)]
        q16, k16 = q.astype(BF16), k.astype(BF16)
        kk = _bnt(k16, k16)
        low = jnp.where(strict, bc * kk * dec, 0.0)
        eg = jnp.exp(gc)
        g_last = gc[:, c - 1:c, :]
        kdec = jnp.exp(g_last - gc)
        kb, vb, qd, kd = k * (bc * eg), v * bc, q * eg, k * kdec
        pm = jnp.where(causal, _bnt(q16, k16) * dec, 0.0)
        s = sp_ref[...].reshape(nb, GDN_D, GDN_D)
        ds = ds_ref[...].reshape(nb, GDN_D, GDN_D)
        s16, ds16 = s.astype(BF16), ds.astype(BF16)
        do16 = do.astype(BF16)
        t16 = tm.astype(BF16)

        dqd = _bnt(do16, s16)
        dp = jnp.where(causal, _bnt(do16, vn16), 0.0)
        dkd = _bnt(vn16, ds16)
        dgl = jnp.sum(jnp.sum(s * ds, axis=2, keepdims=True), axis=1, keepdims=True) * jnp.exp(g_last)
        dwk16 = (-_bnt(dvn16, s16)).astype(BF16)
        dt = _bnt(dwk16, kb.astype(BF16)) + _bnt(dvn16, vb.astype(BF16))
        dkb = _btn(t16, dwk16)
        dvb = _btn(t16, dvn16)
        th, tl = _split(tm)
        dth, dtl = _split(dt)
        xm = _btn(th, dth) + (_btn(tl, dth) + _btn(th, dtl))
        xh, xl = _split(xm)
        dlow = jnp.where(strict, -(_bnt(xh, th) + (_bnt(xl, th) + _bnt(xh, tl))), 0.0)
        dkk16 = (dlow * bc * dec).astype(BF16)
        dqk16 = (dp * dec).astype(BF16)

        dq = _bnn(dqk16, k16) + dqd * eg
        dk = _btn(dqk16, q16) + _bnn(dkk16, k16) + _btn(dkk16, k16) + dkb * (bc * eg) + dkd * kdec
        dv = dvb * bc
        for idx in range(nb):
            ch, h = divmod(idx, GDN_HEADS)
            rs = slice(c * ch, c * (ch + 1))
            hs = slice(GDN_D * h, GDN_D * (h + 1))
            dqkv_ref[0, rs, hs] = dq[idx]
            dqkv_ref[1, rs, hs] = dk[idx]
            dqkv_ref[2, rs, hs] = dv[idx]

        dbeta = (jnp.sum(dlow * kk * dec, axis=2, keepdims=True)
                 + jnp.sum(dkb * k, axis=2, keepdims=True) * eg + jnp.sum(dvb * v, axis=2, keepdims=True))
        mm = dlow * low + dp * pm
        mh, ml = _split(mm)
        ones16 = jnp.ones((nb, c, 128), BF16)
        col_sum = (_btn(mh, ones16) + _btn(ml, ones16))[:, :, 0:1]
        dkd_sum = jnp.sum(dkd * kd, axis=2, keepdims=True)
        dg = (jnp.sum(mm, axis=2, keepdims=True) - col_sum + jnp.sum(dkb * kb, axis=2, keepdims=True)
              + jnp.sum(dqd * qd, axis=2, keepdims=True) - dkd_sum)
        tail = jnp.sum(dkd_sum, axis=1, keepdims=True) + dgl
        upper = (lax.broadcasted_iota(jnp.int32, (c, c), 0) <= lax.broadcasted_iota(jnp.int32, (c, c), 1)).astype(F32)
        for ch in range(nch):
            dbeta_all = jnp.zeros((c, 128), F32)
            dg_all = jnp.zeros((c, 128), F32)
            for h in range(GDN_HEADS):
                idx = ch * GDN_HEADS + h
                dbeta_all = dbeta_all + jnp.where(lane == h, dbeta[idx], 0.0)
                dg_all = dg_all + jnp.where(lane == GDN_HEADS + h,
                                            dg[idx] + jnp.where(rowi == c - 1, tail[idx], 0.0), 0.0)
            dbg_ref[c * ch:c * (ch + 1), :] = dbeta_all + _nn(upper, dg_all, HIGHEST)

    row = pl.BlockSpec((c * nch, GDN_W), lambda n: (n, 0))
    narrow = pl.BlockSpec((c * nch, 128), lambda n: (n, 0))
    state = pl.BlockSpec((nch, GDN_HEADS, GDN_D, GDN_D), lambda n: (n, 0, 0, 0))
    return pl.pallas_call(
        body, name=name, grid=(nsteps,),
        in_specs=[row, row, row, narrow, narrow, pl.BlockSpec((GDN_HEADS, c * nch, c), lambda n: (0, n, 0)),
                  row, row, row, state, state],
        out_specs=[pl.BlockSpec((3, c * nch, GDN_W), lambda n: (0, n, 0)), narrow],
        out_shape=[jax.ShapeDtypeStruct((3, s_len, GDN_W), F32), jax.ShapeDtypeStruct((s_len, 128), F32)],
        compiler_params=_params(("parallel",)),
    )(qn, kn, vc, bg, g_cum, t, do, dvn, vn, sprev, ds_all)


def _gdn_prep_bwd(u, dqkv, cw, du, *, name, tm=256):
    s_len = u.shape[0]
    nsteps = s_len // tm
    ext = tm + SHALO

    def body(uc, up, un, dc, dn, cw_ref, du_in_ref, du_ref, dcw_ref, xbuf, dbuf, pbuf, wacc):
        del du_in_ref
        part = pl.program_id(0)
        i = pl.program_id(1)

        @pl.when(i == 0)
        def _():
            wacc[...] = jnp.zeros_like(wacc)

        xbuf[:SHALO, :] = jnp.where(i > 0, up[...], 0.0)
        xbuf[SHALO:SHALO + tm, :] = uc[...]
        xbuf[SHALO + tm:, :] = jnp.where(i < nsteps - 1, un[...], 0.0)
        dbuf[:tm, :] = dc[...]
        dbuf[tm:, :] = jnp.where(i < nsteps - 1, dn[...], 0.0)
        first = SHALO - SHORT_CONV + 1
        w = [cw_ref[j:j + 1, :] for j in range(SHORT_CONV)]
        taps = [first + j for j in range(SHORT_CONV)]
        xv = xbuf[...]
        pre = jnp.zeros((ext, GDN_W), F32)
        for j, win in _shifted_windows(xv, ext, taps):
            pre = pre + win * w[j]
        y = _silu(pre)
        dout = dbuf[...]
        scale = jnp.where(part == 0, GDN_D ** -0.5, 1.0)
        for h in range(GDN_HEADS):
            hs = slice(GDN_D * h, GDN_D * (h + 1))
            yh, dh = y[:, hs], dout[:, hs]
            rs = lax.rsqrt(jnp.sum(yh * yh, axis=-1, keepdims=True) + 1e-6)
            dyn = scale * rs * (dh - yh * (rs * rs) * jnp.sum(dh * yh, axis=-1, keepdims=True))
            dy = jnp.where(part < 2, dyn, dh)
            pbuf[:, hs] = dy * _dsilu(pre[:, hs])
        acc = jnp.zeros((tm, GDN_W), F32)
        dpre = pbuf[0:tm, :]
        for k, win in _shifted_windows(pbuf[...], tm, list(range(SHORT_CONV))):
            acc = acc + win * w[SHORT_CONV - 1 - k]
        for j, win in _shifted_windows(xv, tm, taps):
            wacc[j] += (win * dpre).reshape(tm // 8, 8, GDN_W).sum(axis=0)
        du_ref[...] = acc.astype(du_ref.dtype)

        @pl.when(i == nsteps - 1)
        def _():
            for j in range(SHORT_CONV):
                dcw_ref[j:j + 1, :] = jnp.sum(wacc[j], axis=0, keepdims=True)
            dcw_ref[SHORT_CONV:, :] = jnp.zeros((SHALO - SHORT_CONV, GDN_W), F32)

    per = tm // SHALO
    return pl.pallas_call(
        body, name=name, grid=(3, nsteps),
        in_specs=[pl.BlockSpec((tm, GDN_W), lambda p, i: (i, COL_GQ + p)),
                  pl.BlockSpec((SHALO, GDN_W), lambda p, i: (jnp.maximum(i * per - 1, 0), COL_GQ + p)),
                  pl.BlockSpec((SHALO, GDN_W), lambda p, i: (jnp.minimum((i + 1) * per, s_len // SHALO - 1), COL_GQ + p)),
                  pl.BlockSpec((None, tm, GDN_W), lambda p, i: (p, i, 0)),
                  pl.BlockSpec((None, SHALO, GDN_W), lambda p, i: (p, jnp.minimum((i + 1) * per, s_len // SHALO - 1), 0)),
                  pl.BlockSpec((SHALO, GDN_W), lambda p, i: (0, p)),
                  pl.BlockSpec(memory_space=pl.ANY)],
        out_specs=[pl.BlockSpec((tm, GDN_W), lambda p, i: (i, COL_GQ + p)),
                   pl.BlockSpec((SHALO, GDN_W), lambda p, i: (0, p))],
        out_shape=[jax.ShapeDtypeStruct(du.shape, du.dtype), jax.ShapeDtypeStruct((SHALO, 3 * GDN_W), F32)],
        scratch_shapes=[pltpu.VMEM((tm + 2 * SHALO, GDN_W), F32), pltpu.VMEM((ext, GDN_W), F32),
                        pltpu.VMEM((ext, GDN_W), F32), pltpu.VMEM((SHORT_CONV, 8, GDN_W), F32)],
        input_output_aliases={6: 0},
        compiler_params=_params(("arbitrary", "arbitrary")),
    )(u, u, u, dqkv, dqkv, cw, du)


def _gdn_ba_bwd(u, dbg, al, dtb, du, *, name, tm=256):
    s_len = u.shape[0]
    nsteps = s_len // tm
    wpad = IN_WP - COL_BA

    def body(uba, dbg_ref, al_ref, dtb_ref, du_in_ref, du_ref, sums_ref):
        del du_in_ref
        i = pl.program_id(0)

        @pl.when(i == 0)
        def _():
            sums_ref[...] = jnp.zeros_like(sums_ref)

        ba = uba[...]
        dbg = dbg_ref[...]
        lane = lax.broadcasted_iota(jnp.int32, ba.shape, 1)
        is_g = (lane >= GDN_HEADS) & (lane < 2 * GDN_HEADS)
        beta = _sigmoid(ba)
        z = ba + dtb_ref[...]
        ea = jnp.exp(al_ref[...])
        g = -ea * _softplus(z)
        dz = jnp.where(is_g, dbg * (-ea) * _sigmoid(z), 0.0)
        du_ref[:, :128] = jnp.where(lane < GDN_HEADS, dbg * beta * (1.0 - beta), dz).astype(du_ref.dtype)
        du_ref[:, 128:] = jnp.zeros((tm, wpad - 128), du_ref.dtype)
        sums_ref[0:1, :] += jnp.sum(jnp.where(is_g, dbg * g, 0.0), axis=0, keepdims=True)
        sums_ref[1:2, :] += jnp.sum(dz, axis=0, keepdims=True)

    vec = pl.BlockSpec((1, 128), lambda i: (0, 0))
    return pl.pallas_call(
        body, name=name, grid=(nsteps,),
        in_specs=[pl.BlockSpec((tm, 128), lambda i: (i, COL_BA // 128)), pl.BlockSpec((tm, 128), lambda i: (i, 0)),
                  vec, vec, pl.BlockSpec(memory_space=pl.ANY)],
        out_specs=[pl.BlockSpec((tm, wpad), lambda i: (i, COL_BA // wpad)), pl.BlockSpec((8, 128), lambda i: (0, 0))],
        out_shape=[jax.ShapeDtypeStruct(du.shape, du.dtype), jax.ShapeDtypeStruct((8, 128), F32)],
        input_output_aliases={4: 0},
        compiler_params=_params(("arbitrary",)),
    )(u, dbg, al, dtb, du)


def _rope_tables(s_len):
    half = ROPE_DIM // 2
    inv = ROPE_THETA ** (-jnp.arange(half, dtype=F32) / half)
    ang = jnp.arange(s_len, dtype=F32)[:, None] * inv[None, :]
    cos, sin = jnp.cos(ang), jnp.sin(ang)
    one = jnp.ones((s_len, ATT_HD - ROPE_DIM), F32)
    zero = jnp.zeros((s_len, ATT_HD - ROPE_DIM), F32)
    zh = jnp.zeros((s_len, half), F32)
    c = jnp.concatenate([cos, cos, one], axis=1)
    s1 = jnp.concatenate([-sin, zh, zero], axis=1)
    s2 = jnp.concatenate([zh, sin, zero], axis=1)
    return tuple(jnp.concatenate([t, t], axis=1) for t in (c, s1, s2))


def _rope(x, c, s1, s2):
    return x * c + pltpu.roll(x, 128 - ROPE_DIM // 2, 1) * s1 + pltpu.roll(x, ROPE_DIM // 2, 1) * s2


def _rope_t(dy, c, s1, s2):
    return dy * c + pltpu.roll(dy * s1, ROPE_DIM // 2, 1) + pltpu.roll(dy * s2, 128 - ROPE_DIM // 2, 1)


DILATIONS = tuple(d for _, d in DIL_PATTERNS)
ATT_QBLOCKS = 2
VIEW_ROWS = 256


def _to_view(scr, out_ref, dil, dtype):
    nblk, rows, _ = scr.shape
    width = nblk * 128
    for b in range(nblk):
        if dil == 1:
            out_ref[:, 128 * b:128 * (b + 1)] = scr[b].astype(dtype)
            continue
        for r in range(dil):
            out_ref[:, r * width + 128 * b:r * width + 128 * (b + 1)] = (
                scr.at[b][pl.ds(r, rows // dil, stride=dil), :].astype(dtype))


def _from_view(in_ref, scr, dil):
    nblk, rows, _ = scr.shape
    width = nblk * 128
    for b in range(nblk):
        for r in range(dil):
            scr.at[b][pl.ds(r, rows // dil, stride=dil), :] = in_ref[:, r * width + 128 * b:r * width + 128 * (b + 1)]


def _view_spec(dil, width, tm=VIEW_ROWS):
    return pl.BlockSpec((tm // dil, dil * width), lambda i: (i, 0))


def _view_shape(s_len, dil, width, dtype):
    return jax.ShapeDtypeStruct((s_len // dil, dil * width), dtype)


def _att_prep_fwd(u, tabs, *, name):
    s_len = u.shape[0]
    tm = VIEW_ROWS
    scale = ATT_HD ** -0.5
    nblk = ATT_W // 128

    def body(uq, uk, uv, c_ref, s1_ref, s2_ref, *rest):
        outs, scr = rest[:-1], rest[-1]
        c, s1, s2 = c_ref[...], s1_ref[...], s2_ref[...]
        for part, src in enumerate((uq, uk, uv)):
            for b in range(nblk):
                xb = src[:, 128 * b:128 * (b + 1)]
                if part == 0:
                    xb = _rope(xb, c, s1, s2) * scale
                elif part == 1:
                    xb = _rope(xb, c, s1, s2)
                scr[b] = xb
            for gi, dil in enumerate(DILATIONS):
                _to_view(scr, outs[3 * gi + part], dil, BF16)

    tab = pl.BlockSpec((tm, 128), lambda i: (i, 0))
    outs = pl.pallas_call(
        body, name=name, grid=(s_len // tm,),
        in_specs=[pl.BlockSpec((tm, ATT_W), lambda i, col=COL_AQ + j: (i, col)) for j in range(3)] + [tab] * 3,
        out_specs=[_view_spec(dil, ATT_W) for dil in DILATIONS for _ in range(3)],
        out_shape=[_view_shape(s_len, dil, ATT_W, BF16) for dil in DILATIONS for _ in range(3)],
        scratch_shapes=[pltpu.VMEM((nblk, tm, 128), F32)],
        compiler_params=_params(("parallel",)),
    )(u, u, u, *tabs)
    return [outs[3 * gi:3 * gi + 3] for gi in range(len(DILATIONS))]


def _stack_heads(x):
    lane = lax.broadcasted_iota(jnp.int32, (1, 128), 1)
    zero = jnp.zeros_like(x)
    return jnp.concatenate([jnp.where(lane < ATT_HD, x, zero), jnp.where(lane >= ATT_HD, x, zero)], axis=0)


def _att_fwd(qr, kr, vb, dil, *, name):
    lr = qr.shape[0]
    blk = ATT_BLOCK
    qb = ATT_QBLOCKS
    nsteps = lr // (blk * qb)

    def body(q_ref, kp_ref, kc_ref, vp_ref, vc_ref, o_ref, lse_ref):
        n = pl.program_id(1)
        qi = lax.broadcasted_iota(jnp.int32, (blk, 2 * blk), 0)
        ki = lax.broadcasted_iota(jnp.int32, (blk, 2 * blk), 1)
        dist = qi + blk - ki
        band = (dist >= 0) & (dist <= blk)
        lane = lax.broadcasted_iota(jnp.int32, (blk, 128), 1)
        for sub in range(qb):
            rs = slice(blk * sub, blk * (sub + 1))
            valid = band if sub > 0 else band & ((ki >= blk) | (n > 0))
            valid = jnp.concatenate([valid, valid], axis=0)
            lse_all = jnp.zeros((blk, 128), F32)
            for hp in range(ATT_HEADS // 2):
                bs = slice(128 * hp, 128 * (hp + 1))
                if sub == 0:
                    kb = jnp.concatenate([kp_ref[:, bs], kc_ref[0:blk, bs]], axis=0)
                    vv = jnp.concatenate([vp_ref[:, bs], vc_ref[0:blk, bs]], axis=0)
                else:
                    kb = kc_ref[blk * (sub - 1):blk * (sub + 1), bs]
                    vv = vc_ref[blk * (sub - 1):blk * (sub + 1), bs]
                s = jnp.where(valid, _nt(_stack_heads(q_ref[rs, bs]), kb), NEG_INF)
                m = jnp.max(s, axis=-1, keepdims=True)
                p = jnp.exp(s - m)
                l = jnp.sum(p, axis=-1, keepdims=True)
                o = _nn((p * (1.0 / l)).astype(BF16), vv)
                o_ref[rs, bs] = jnp.where(lane < ATT_HD, o[:blk], o[blk:])
                lse = m + jnp.log(l)
                lse_all = (lse_all + jnp.where(lane == 2 * hp, lse[:blk], 0.0)
                           + jnp.where(lane == 2 * hp + 1, lse[blk:], 0.0))
            lse_ref[rs, :] = lse_all

    cur = pl.BlockSpec((blk * qb, ATT_W), lambda r, n: (n, r))
    prev = pl.BlockSpec((blk, ATT_W), lambda r, n: (jnp.maximum(qb * n - 1, 0), r))
    return pl.pallas_call(
        body, name=name, grid=(dil, nsteps), in_specs=[cur, prev, cur, prev, cur],
        out_specs=[cur, pl.BlockSpec((blk * qb, 128), lambda r, n: (n, r))],
        out_shape=[jax.ShapeDtypeStruct(qr.shape, F32), jax.ShapeDtypeStruct((lr, dil * 128), F32)],
        compiler_params=_params(("parallel", "parallel")),
    )(qr, kr, kr, vb, vb)


def _att_bwd(qr, kr, vb, do, lse, delta, dil, *, name):
    lr = qr.shape[0]
    blk = ATT_BLOCK
    nsteps = lr // (2 * blk)

    def body(q_ref, kp_ref, kc_ref, vp_ref, vc_ref, do_ref, lse_ref, dl_ref, dq_ref, dk_ref, dv_ref, carry):
        n = pl.program_id(1)

        @pl.when(n == 0)
        def _():
            carry[...] = jnp.zeros_like(carry)

        @pl.when(n == nsteps)
        def _():
            for t, ref in enumerate((dk_ref, dv_ref)):
                ref[:blk, :] = carry[0, t]
                ref[blk:, :] = carry[1, t]

        @pl.when(n < nsteps)
        def _():
            qi = lax.broadcasted_iota(jnp.int32, (blk, 2 * blk), 0)
            ki = lax.broadcasted_iota(jnp.int32, (blk, 2 * blk), 1)
            dist = qi + blk - ki
            band = (dist >= 0) & (dist <= blk)
            lane = lax.broadcasted_iota(jnp.int32, (blk, 128), 1)
            for hp in range(ATT_HEADS // 2):
                bs = slice(128 * hp, 128 * (hp + 1))
                accs = []
                for sub in range(2):
                    rs = slice(blk * sub, blk * (sub + 1))
                    valid = band if sub > 0 else band & ((ki >= blk) | (n > 0))
                    valid = jnp.concatenate([valid, valid], axis=0)
                    if sub == 0:
                        kb = jnp.concatenate([kp_ref[:, bs], kc_ref[0:blk, bs]], axis=0)
                        vv = jnp.concatenate([vp_ref[:, bs], vc_ref[0:blk, bs]], axis=0)
                    else:
                        kb, vv = kc_ref[:, bs], vc_ref[:, bs]
                    q2 = _stack_heads(q_ref[rs, bs])
                    do2 = _stack_heads(do_ref[rs, bs])
                    lse2 = jnp.concatenate([lse_ref[rs, 2 * hp:2 * hp + 1], lse_ref[rs, 2 * hp + 1:2 * hp + 2]], axis=0)
                    dl2 = jnp.concatenate([dl_ref[rs, 2 * hp:2 * hp + 1], dl_ref[rs, 2 * hp + 1:2 * hp + 2]], axis=0)
                    p = jnp.where(valid, jnp.exp(_nt(q2, kb) - lse2), 0.0)
                    ds16 = (p * (_nt(do2, vv) - dl2)).astype(BF16)
                    dq2 = _nn(ds16, kb)
                    dq_ref[rs, bs] = jnp.where(lane < ATT_HD, dq2[:blk], dq2[blk:])
                    accs.append((_tn(ds16, q2), _tn(p.astype(BF16), do2)))
                for t, ref in enumerate((dk_ref, dv_ref)):
                    first, second = accs[0][t], accs[1][t]
                    ref[:blk, bs] = carry[0, t, :, bs]
                    ref[blk:, bs] = carry[1, t, :, bs] + first[:blk]
                    carry[0, t, :, bs] = first[blk:] + second[:blk]
                    carry[1, t, :, bs] = second[blk:]

    def at(n):
        return jnp.minimum(n, nsteps - 1)

    cur = pl.BlockSpec((2 * blk, ATT_W), lambda r, n: (at(n), r))
    prev = pl.BlockSpec((blk, ATT_W), lambda r, n: (jnp.maximum(2 * at(n) - 1, 0), r))
    nar = pl.BlockSpec((2 * blk, 128), lambda r, n: (at(n), r))
    late = pl.BlockSpec((2 * blk, ATT_W), lambda r, n: (jnp.maximum(n - 1, 0), r))
    out = jax.ShapeDtypeStruct(qr.shape, F32)
    return pl.pallas_call(
        body, name=name, grid=(dil, nsteps + 1), in_specs=[cur, prev, cur, prev, cur, cur, nar, nar],
        out_specs=[cur, late, late], out_shape=[out, out, out],
        scratch_shapes=[pltpu.VMEM((2, 2, blk, ATT_W), F32)],
        compiler_params=_params(("parallel", "arbitrary")),
    )(qr, kr, kr, vb, vb, do, lse, delta)


def _att_prep_bwd(dgroups, tabs, du, *, name):
    s_len = du.shape[0]
    tm = VIEW_ROWS
    scale = ATT_HD ** -0.5
    nblk = ATT_W // 128
    ng = len(DILATIONS)

    def body(*refs):
        grads = refs[:3 * ng]
        c_ref, s1_ref, s2_ref, _, du_ref = refs[3 * ng:3 * ng + 5]
        scrs = refs[3 * ng + 5:]
        c, s1, s2 = c_ref[...], s1_ref[...], s2_ref[...]
        for part in range(3):
            for gi, dil in enumerate(DILATIONS):
                if dil > 1:
                    _from_view(grads[3 * gi + part], scrs[gi], dil)
            for b in range(nblk):
                tot = None
                for gi, dil in enumerate(DILATIONS):
                    term = grads[3 * gi + part][:, 128 * b:128 * (b + 1)] if dil == 1 else scrs[gi][b]
                    tot = term if tot is None else tot + term
                if part == 0:
                    tot = _rope_t(tot * scale, c, s1, s2)
                elif part == 1:
                    tot = _rope_t(tot, c, s1, s2)
                du_ref[:, ATT_W * part + 128 * b:ATT_W * part + 128 * (b + 1)] = tot.astype(du_ref.dtype)

    tab = pl.BlockSpec((tm, 128), lambda i: (i, 0))
    return pl.pallas_call(
        body, name=name, grid=(s_len // tm,),
        in_specs=[_view_spec(dil, ATT_W) for dil in DILATIONS for _ in range(3)] + [tab] * 3
        + [pl.BlockSpec(memory_space=pl.ANY)],
        out_specs=pl.BlockSpec((tm, 3 * ATT_W), lambda i: (i, COL_AQ // 3)),
        out_shape=jax.ShapeDtypeStruct(du.shape, du.dtype),
        scratch_shapes=[pltpu.VMEM((nblk, tm, 128), F32) for _ in DILATIONS],
        input_output_aliases={3 * ng + 3: 0},
        compiler_params=_params(("parallel",)),
    )(*[a for g in dgroups for a in g], *tabs, du)


def _head_weights(w, b):
    lane = lax.broadcasted_iota(jnp.int32, (1, 128), 1)
    return jnp.where(lane < ATT_HD, w[:, 2 * b:2 * b + 1], w[:, 2 * b + 1:2 * b + 2])


def _assemble_fwd(pw, u, o_gdn, gnw, o_groups, lse_groups, *, name):
    s_len = u.shape[0]
    tm = VIEW_ROWS
    c = CONV_CH
    nblk = ATT_W // 128
    ng = len(DILATIONS)

    def body(*refs):
        pw_ref, cg_ref, z_ref, ag_ref, og_ref, gnw_ref = refs[:6]
        o_refs, l_refs = refs[6:6 + ng], refs[6 + ng:6 + 2 * ng]
        y_ref, oa_ref = refs[6 + 2 * ng:8 + 2 * ng]
        lse_outs = refs[8 + 2 * ng:8 + 3 * ng]
        o_scr, l_scr = refs[8 + 3 * ng:8 + 4 * ng], refs[8 + 4 * ng:8 + 5 * ng]
        lse_scr = refs[8 + 5 * ng]
        y_ref[:, :c] = (pw_ref[...] * _silu(cg_ref[...])).astype(BF16)
        gw = gnw_ref[...]
        for h in range(GDN_HEADS):
            hs = slice(GDN_D * h, GDN_D * (h + 1))
            oh = og_ref[:, hs]
            yn = oh * lax.rsqrt(jnp.mean(oh * oh, axis=-1, keepdims=True) + 1e-6) * gw
            y_ref[:, c + GDN_D * h:c + GDN_D * (h + 1)] = (yn * _silu(z_ref[:, hs])).astype(BF16)
        for gi, dil in enumerate(DILATIONS):
            if dil > 1:
                _from_view(o_refs[gi], o_scr[gi], dil)
                _from_view(l_refs[gi], l_scr[gi], dil)
        ls = [l_refs[gi][...] if dil == 1 else l_scr[gi][0] for gi, dil in enumerate(DILATIONS)]
        m = functools.reduce(jnp.maximum, ls)
        es = [jnp.exp(l - m) for l in ls]
        den = functools.reduce(lambda a, b: a + b, es)
        lse_scr[0] = m + jnp.log(den)
        ws = [e / den for e in es]
        for b in range(nblk):
            bs = slice(128 * b, 128 * (b + 1))
            o = None
            for gi, dil in enumerate(DILATIONS):
                term = _head_weights(ws[gi], b) * (o_refs[gi][:, bs] if dil == 1 else o_scr[gi][b])
                o = term if o is None else o + term
            oa_ref[:, bs] = o
            y_ref[:, c + GDN_W + 128 * b:c + GDN_W + 128 * (b + 1)] = (o * _silu(ag_ref[:, bs])).astype(BF16)
        for gi, dil in enumerate(DILATIONS):
            _to_view(lse_scr, lse_outs[gi], dil, F32)

    wide = pl.BlockSpec((tm, 768), lambda i: (i, 0))
    return pl.pallas_call(
        body, name=name, grid=(s_len // tm,),
        in_specs=[pl.BlockSpec((tm, c), lambda i: (i, 0)), pl.BlockSpec((tm, c), lambda i: (i, 1024 // c)),
                  pl.BlockSpec((tm, 768), lambda i: (i, COL_GQ + 3)), pl.BlockSpec((tm, 768), lambda i: (i, COL_AQ + 3)),
                  wide, pl.BlockSpec((1, 128), lambda i: (0, 0))]
        + [_view_spec(dil, ATT_W) for dil in DILATIONS] + [_view_spec(dil, 128) for dil in DILATIONS],
        out_specs=[pl.BlockSpec((tm, D_MODEL), lambda i: (i, 0)), wide] + [_view_spec(dil, 128) for dil in DILATIONS],
        out_shape=[jax.ShapeDtypeStruct((s_len, D_MODEL), BF16), jax.ShapeDtypeStruct((s_len, ATT_W), F32)]
        + [_view_shape(s_len, dil, 128, F32) for dil in DILATIONS],
        scratch_shapes=[pltpu.VMEM((nblk, tm, 128), F32) for _ in DILATIONS]
        + [pltpu.VMEM((1, tm, 128), F32) for _ in DILATIONS] + [pltpu.VMEM((1, tm, 128), F32)],
        compiler_params=_params(("parallel",)),
    )(pw, u, u, u, o_gdn, gnw, *o_groups, *lse_groups)


def _assemble_bwd(dy, pw, u, o_gdn, gnw, o_att, *, name):
    s_len = u.shape[0]
    tm = VIEW_ROWS
    c = CONV_CH
    nsteps = s_len // tm
    nblk = ATT_W // 128
    ng = len(DILATIONS)

    def body(dy_ref, pw_ref, cg_ref, z_ref, ag_ref, og_ref, gnw_ref, oa_ref,
             du_ref, dpw_ref, dog_ref, dgw_ref, *rest):
        do_outs, dl_outs = rest[:ng], rest[ng:2 * ng]
        acc_ref, do_scr, dl_scr = rest[2 * ng:]
        i = pl.program_id(0)

        @pl.when(i == 0)
        def _():
            acc_ref[...] = jnp.zeros_like(acc_ref)

        du_ref[...] = jnp.zeros_like(du_ref)
        dyc = dy_ref[:, :c]
        cg = cg_ref[...]
        dpw_ref[...] = dyc * _silu(cg)
        du_ref[:, 1024:1024 + c] = (dyc * pw_ref[...] * _dsilu(cg)).astype(BF16)
        gw = gnw_ref[...]
        dgw = jnp.zeros((8, 128), F32)
        for h in range(GDN_HEADS):
            hs = slice(GDN_D * h, GDN_D * (h + 1))
            oh = og_ref[:, hs]
            zh = z_ref[:, hs]
            dyh = dy_ref[:, c + GDN_D * h:c + GDN_D * (h + 1)]
            r = lax.rsqrt(jnp.mean(oh * oh, axis=-1, keepdims=True) + 1e-6)
            xn = oh * r
            dyn = dyh * _silu(zh)
            du_ref[:, GDN_W * (COL_GQ + 3) + GDN_D * h:GDN_W * (COL_GQ + 3) + GDN_D * (h + 1)] = (
                dyh * xn * gw * _dsilu(zh)).astype(BF16)
            dgw = dgw + (dyn * xn).reshape(tm // 8, 8, 128).sum(axis=0)
            dxn = dyn * gw
            dog_ref[:, hs] = r * (dxn - xn * jnp.mean(dxn * xn, axis=-1, keepdims=True))
        acc_ref[...] += dgw
        lane = lax.broadcasted_iota(jnp.int32, (tm, 128), 1)
        delta = jnp.zeros((tm, 128), F32)
        for b in range(ATT_W // 128):
            bs = slice(128 * b, 128 * (b + 1))
            dya = dy_ref[:, c + GDN_W + 128 * b:c + GDN_W + 128 * (b + 1)]
            ag = ag_ref[:, bs]
            oa = oa_ref[:, bs]
            do = dya * _silu(ag)
            do_scr[b] = do
            du_ref[:, ATT_W * (COL_AQ + 3) + 128 * b:ATT_W * (COL_AQ + 3) + 128 * (b + 1)] = (
                dya * oa * _dsilu(ag)).astype(BF16)
            prod = do * oa
            lo = jnp.sum(jnp.where(lane < ATT_HD, prod, 0.0), axis=-1, keepdims=True)
            hi = jnp.sum(jnp.where(lane >= ATT_HD, prod, 0.0), axis=-1, keepdims=True)
            delta = delta + jnp.where(lane == 2 * b, lo, 0.0) + jnp.where(lane == 2 * b + 1, hi, 0.0)
        dl_scr[0] = delta
        for gi, dil in enumerate(DILATIONS):
            _to_view(do_scr, do_outs[gi], dil, BF16)
            _to_view(dl_scr, dl_outs[gi], dil, F32)

        @pl.when(i == nsteps - 1)
        def _():
            dgw_ref[...] = jnp.sum(acc_ref[...], axis=0, keepdims=True)

    wide = pl.BlockSpec((tm, 768), lambda i: (i, 0))
    vec = pl.BlockSpec((1, 128), lambda i: (0, 0))
    outs = pl.pallas_call(
        body, name=name, grid=(nsteps,),
        in_specs=[pl.BlockSpec((tm, D_MODEL), lambda i: (i, 0)), pl.BlockSpec((tm, c), lambda i: (i, 0)),
                  pl.BlockSpec((tm, c), lambda i: (i, 1024 // c)), pl.BlockSpec((tm, 768), lambda i: (i, COL_GQ + 3)),
                  pl.BlockSpec((tm, 768), lambda i: (i, COL_AQ + 3)), wide, vec, wide],
        out_specs=[pl.BlockSpec((tm, IN_WP), lambda i: (i, 0)), pl.BlockSpec((tm, c), lambda i: (i, 0)), wide, vec]
        + [_view_spec(dil, ATT_W) for dil in DILATIONS] + [_view_spec(dil, 128) for dil in DILATIONS],
        out_shape=[jax.ShapeDtypeStruct((s_len, IN_WP), BF16), jax.ShapeDtypeStruct((s_len, c), F32),
                   jax.ShapeDtypeStruct((s_len, GDN_W), F32), jax.ShapeDtypeStruct((1, 128), F32)]
        + [_view_shape(s_len, dil, ATT_W, BF16) for dil in DILATIONS]
        + [_view_shape(s_len, dil, 128, F32) for dil in DILATIONS],
        scratch_shapes=[pltpu.VMEM((8, 128), F32), pltpu.VMEM((nblk, tm, 128), F32), pltpu.VMEM((1, tm, 128), F32)],
        compiler_params=_params(("arbitrary",)),
    )(dy, pw, u, u, u, o_gdn, gnw, o_att)
    return outs[:4], outs[4:4 + ng], outs[4 + ng:]


def _layer_fwd(x, p, tabs):
    h = _rms_fwd(x, p["norm_w"], name="rms_fwd")
    u = _matmul(h, p["wp"], name="in_proj", tk=2048)
    conv, sw = _conf_fwd(u, p["dw_w"], p["dw_b"], p["ln_w"], p["ln_b"], name="conf_fwd")
    pw = _matmul(sw, p["pw_w"], name="conf_pw")
    qn, kn, vc, bg = _gdn_prep_fwd(u, p["cw"], p["al"], p["dtb"], name="gdn_prep_fwd")
    wk, wv, qd, kd, pm, t, g_cum = _gdn_intra_fwd(qn, kn, vc, bg, name="gdn_intra_fwd")
    o_gdn, vn, sprev = _gdn_scan_fwd(wk, wv, qd, kd, pm, g_cum, name="gdn_scan_fwd")
    qkv = _att_prep_fwd(u, tabs, name="att_prep_fwd")
    groups = [_att_fwd(*qkv[gi], dil, name=f"att_fwd_d{dil}") for gi, dil in enumerate(DILATIONS)]
    outs = _assemble_fwd(pw, u, o_gdn, p["gnw"], [g[0] for g in groups], [g[1] for g in groups],
                         name="assemble_fwd")
    y, o_att, lse = outs[0], outs[1], outs[2:]
    x_next = _matmul(y, p["wout"], add=x, name="out_proj", tm=512, tn=2048, tk=2048)
    saved = dict(x=x, h=h, u=u, conv=conv, sw=sw, pw=pw, qn=qn, kn=kn, vc=vc, bg=bg, wk=wk, qd=qd, kd=kd, pm=pm,
                 t=t, g_cum=g_cum, vn=vn, sprev=sprev, o_gdn=o_gdn, qkv=qkv, o_att=o_att, lse=lse, y=y)
    return x_next, saved


def _layer_bwd(dx_out, s, p, tabs, layer, big):
    dy = _matmul(dx_out, p["wout"], tb=True, name="out_proj_dy", tm=512, tn=2048, tk=2048)
    d_wout = _matmul(s["y"], dx_out, ta=True, name="out_proj_dw", tk=2048, stack=(big[1], layer, DEPTH))
    (du, dpw, dog, dgw), do_views, dl_views = _assemble_bwd(dy, s["pw"], s["u"], s["o_gdn"], p["gnw"], s["o_att"],
                                                            name="assemble_bwd")
    dsw = _matmul(dpw, p["pw_w"], tb=True, name="conf_pw_dx")
    d_pw_w = _matmul(s["sw"], dpw, ta=True, name="conf_pw_dw", stack=(big[2], layer, DEPTH))
    dconv, ln_sums = _conf_bwd_ln(dsw, s["conv"], p["ln_w"], p["ln_b"], name="conf_bwd_ln")
    du, d_dw_w = _conf_bwd_conv(s["u"], dconv, p["dw_w"], du, name="conf_bwd_conv")
    dvn, ds_all = _gdn_scan_bwd(dog, s["wk"], s["qd"], s["kd"], s["pm"], s["g_cum"], name="gdn_scan_bwd")
    dqkv, dbg = _gdn_intra_bwd(s["qn"], s["kn"], s["vc"], s["bg"], s["g_cum"], s["t"], dog, dvn, s["vn"],
                               s["sprev"], ds_all, name="gdn_intra_bwd")
    du, d_cw = _gdn_prep_bwd(s["u"], dqkv, p["cw"], du, name="gdn_prep_bwd")
    du, ba_sums = _gdn_ba_bwd(s["u"], dbg, p["al"], p["dtb"], du, name="gdn_ba_bwd")
    dgroups = []
    for gi, dil in enumerate(DILATIONS):
        args = (*s["qkv"][gi], do_views[gi], s["lse"][gi], dl_views[gi], dil)
        dgroups.append(_att_bwd(*args, name=f"att_bwd_d{dil}"))
    du = _att_prep_bwd(dgroups, tabs, du, name="att_prep_bwd")
    dh = _matmul(du, p["wp"], tb=True, name="in_proj_dx", tk=4096)
    d_wp = _matmul(s["h"], du, ta=True, name="in_proj_dw", tk=4096, stack=(big[0], layer, DEPTH))
    dx, d_norm_w = _rms_bwd(s["x"], dh, p["norm_w"], dx_out, name="rms_bwd")
    small = dict(norm_w=d_norm_w, gnw=dgw, ln_sums=ln_sums, dw_w=d_dw_w, cw=d_cw, ba_sums=ba_sums)
    return dx, (d_wp, d_wout, d_pw_w), small


def _trunk(x, target, params, final_norm_w):
    tabs = _rope_tables(x.shape[0])
    layers = [{k: v[l] for k, v in params.items()} for l in range(DEPTH)]
    saved = []
    for p in layers:
        x, s = _layer_fwd(x, p, tabs)
        saved.append(s)
    dx, d_final, loss = _loss_head(x, final_norm_w, target, name="loss_head")
    big = (None, None, None)
    small = [None] * DEPTH
    for l in reversed(range(DEPTH)):
        dx, big, small[l] = _layer_bwd(dx, saved[l], layers[l], tabs, l, big)
    grads = {k: jnp.stack([sm[k] for sm in small]) for k in small[0]}
    grads.update(wp=big[0], wout=big[1], pw_w=big[2])
    return loss[0, 0], dx, grads, d_final


ANY = pl.BlockSpec(memory_space=pl.ANY)


def _position():
    return lax.axis_index("x"), lax.axis_index("y"), lax.axis_index("c")


def _other_chips(x, y):
    return [(1 - x, y), (x, 1 - y), (1 - x, 1 - y)]


def _gather_chips(shards, *, name):
    n = len(shards)
    kinds = 12

    def body(*refs):
        ins, outs = refs[:n], refs[n:2 * n]
        send, recv = refs[2 * n:]
        x, y, c = _position()
        me, sib = (x, y, c), (x, y, 1 - c)
        xn, yn, dg = (1 - x, y), (x, 1 - y), (1 - x, 1 - y)
        pa, pb = 2 * c, 2 * c + 1

        def copy(k, a, chip, layer, to, src=None):
            dst = outs[a].at[2 * chip[0] + chip[1], pl.ds(layer, 1)]
            return pltpu.make_async_remote_copy(
                src_ref=dst if src is None else src, dst_ref=dst, send_sem=send.at[k * n + a],
                recv_sem=recv.at[k * n + a], device_id=to, device_id_type=MESH)

        def own(k, a, layer, chip):
            return copy(k, a, (x, y), layer, (*chip, c), src=ins[a].at[pl.ds(layer, 1)])

        sends = []
        for a in range(n):
            sends += [own(0, a, pa, xn), own(1, a, pb, yn), own(2, a, pb, xn), own(3, a, pa, yn)]
        for cp in sends:
            cp.start()
        arrivals = [(1, yn, pb, (4, xn)), (0, xn, pa, (5, yn)), (2, xn, pb, None), (3, yn, pa, None),
                    (4, dg, pb, None), (5, dg, pa, None)]
        for a in range(n):
            for j, (k, chip, layer, onward) in enumerate(arrivals):
                copy(k, a, chip, layer, me).wait_recv()
                if onward is not None:
                    cp = copy(onward[0], a, chip, layer, (*onward[1], c))
                    cp.start()
                    sends.append(cp)
                cp = copy(6 + j, a, chip, layer, sib)
                cp.start()
                sends.append(cp)
        for a in range(n):
            for j, (k, chip, layer, onward) in enumerate(arrivals):
                copy(6 + j, a, chip, layer + 2 - 4 * c, me).wait_recv()
        for cp in sends:
            cp.wait_send()

    return pl.pallas_call(
        body, name=name, in_specs=[ANY] * n, out_specs=[ANY] * n,
        out_shape=[jax.ShapeDtypeStruct((4,) + s.shape, s.dtype) for s in shards],
        scratch_shapes=[pltpu.SemaphoreType.DMA((kinds * n,)), pltpu.SemaphoreType.DMA((kinds * n,))],
    )(*shards)


def _to_sibling(arrs, *, name):
    n = len(arrs)

    def body(*refs):
        ins, outs = refs[:n], refs[n:2 * n]
        send, recv = refs[2 * n:]
        x, y, c = _position()
        cps = [pltpu.make_async_remote_copy(src_ref=ins[a], dst_ref=outs[a], send_sem=send.at[a],
                                            recv_sem=recv.at[a], device_id=(x, y, 1 - c), device_id_type=MESH)
               for a in range(n)]
        for cp in cps:
            cp.start()
        for cp in cps:
            cp.wait()

    return pl.pallas_call(
        body, name=name, in_specs=[ANY] * n, out_specs=[ANY] * n,
        out_shape=[jax.ShapeDtypeStruct(a.shape, a.dtype) for a in arrs],
        scratch_shapes=[pltpu.SemaphoreType.DMA((n,)), pltpu.SemaphoreType.DMA((n,))],
    )(*arrs)


def _to_chips(arrs, *, name):
    n = len(arrs)

    def body(*refs):
        ins, outs = refs[:n], refs[n:2 * n]
        send, recv = refs[2 * n:]
        x, y, c = _position()
        cps = [pltpu.make_async_remote_copy(
            src_ref=ins[a].at[2 * chip[0] + chip[1]], dst_ref=outs[a].at[j], send_sem=send.at[j * n + a],
            recv_sem=recv.at[j * n + a], device_id=(*chip, c), device_id_type=MESH)
            for j, chip in enumerate(_other_chips(x, y)) for a in range(n)]
        for cp in cps:
            cp.start()
        for cp in cps:
            cp.wait()

    return pl.pallas_call(
        body, name=name, in_specs=[ANY] * n, out_specs=[ANY] * n,
        out_shape=[jax.ShapeDtypeStruct((3,) + a.shape[1:], a.dtype) for a in arrs],
        scratch_shapes=[pltpu.SemaphoreType.DMA((3 * n,)), pltpu.SemaphoreType.DMA((3 * n,))],
    )(*arrs)


def _join_halves(fulls, *, name):
    n = len(fulls)

    def body(*refs):
        ins, outs = refs[:n], refs[n:2 * n]
        send, recv = refs[2 * n:]
        x, y, c = _position()

        def copy(a, rows):
            return pltpu.make_async_remote_copy(
                src_ref=ins[a].at[rows], dst_ref=outs[a].at[rows], send_sem=send.at[a], recv_sem=recv.at[a],
                device_id=(x, y, 1 - c), device_id_type=MESH)

        cps = [copy(a, pl.ds(2 * c, 2)) for a in range(n)]
        for cp in cps:
            cp.start()
        for a in range(n):
            cps[a].wait_send()
            copy(a, pl.ds(2 * (1 - c), 2)).wait_recv()

    return pl.pallas_call(
        body, name=name, in_specs=[ANY] * n, out_specs=[ANY] * n,
        out_shape=[jax.ShapeDtypeStruct(f.shape, f.dtype) for f in fulls],
        scratch_shapes=[pltpu.SemaphoreType.DMA((n,)), pltpu.SemaphoreType.DMA((n,))],
        input_output_aliases={a: a for a in range(n)},
    )(*fulls)


def _allreduce_small(packed, *, name):
    rows = packed.shape[0]
    ndev = 8

    def body(x_ref, sum_ref, all_ref, send, recv, lsem):
        x, y, c = _position()
        me, sib = (x, y, c), (x, y, 1 - c)
        chips = _other_chips(x, y)

        def blk(px, py, pc):
            return all_ref.at[pl.ds((4 * px + 2 * py + pc) * rows, rows), :]

        def copy(k, block, to, src=None):
            return pltpu.make_async_remote_copy(
                src_ref=blk(*block) if src is None else src, dst_ref=blk(*block), send_sem=send.at[k],
                recv_sem=recv.at[k], device_id=to, device_id_type=MESH)

        mine = pltpu.make_async_copy(x_ref, blk(*me), lsem)
        mine.start()
        first = [copy(0, me, sib, src=x_ref)] + [copy(1 + j, me, (*chip, c), src=x_ref) for j, chip in enumerate(chips)]
        for cp in first:
            cp.start()
        passed = [copy(4 + j, (*chip, c), sib) for j, chip in enumerate(chips)]
        for j, chip in enumerate(chips):
            copy(1 + j, (*chip, c), me).wait_recv()
            passed[j].start()
        copy(0, sib, me).wait_recv()
        for j, chip in enumerate(chips):
            copy(4 + j, (*chip, 1 - c), me).wait_recv()
        for cp in first + passed:
            cp.wait_send()
        mine.wait()
        acc = all_ref[0:rows, :]
        for d in range(1, ndev):
            acc = acc + all_ref[d * rows:(d + 1) * rows, :]
        sum_ref[...] = acc

    vm = pl.BlockSpec(memory_space=pltpu.VMEM)
    return pl.pallas_call(
        body, name=name, in_specs=[vm], out_specs=vm, out_shape=jax.ShapeDtypeStruct((rows, 128), F32),
        scratch_shapes=[pltpu.VMEM((ndev * rows, 128), F32), pltpu.SemaphoreType.DMA((7,)),
                        pltpu.SemaphoreType.DMA((7,)), pltpu.SemaphoreType.DMA],
        compiler_params=pltpu.CompilerParams(vmem_limit_bytes=VMEM_LIMIT),
    )(packed)


def _pack(arrs):
    flat = jnp.concatenate([a.reshape(-1) for a in arrs])
    pad = (-flat.shape[0]) % 1024
    return jnp.pad(flat, (0, pad)).reshape(-1, 128)


def _unpack(packed, shapes):
    flat = packed.reshape(-1)
    out, pos = [], 0
    for s in shapes:
        size = math.prod(s)
        out.append(flat[pos:pos + size].reshape(s))
        pos += size
    return out


def _pad_cols(w):
    zeros = jnp.zeros(w.shape[:-1] + (IN_WP - IN_W,), w.dtype)
    return jnp.concatenate([w[..., :ORIG_BA], w[..., ORIG_ATT:], w[..., ORIG_BA:ORIG_ATT], zeros], axis=-1)


def _chip_cols(j):
    per = IN_W // 4
    lo, hi = j * per, (j + 1) * per
    out = []
    for o0, o1, p0 in ((0, ORIG_BA, 0), (ORIG_BA, ORIG_ATT, COL_BA), (ORIG_ATT, IN_W, ORIG_BA)):
        a, b = max(lo, o0), min(hi, o1)
        if a < b:
            out.append((p0 + a - o0, p0 + b - o0))
    return out


def _shards_to_padded(g):
    pieces = []
    for j in range(4):
        loc = 0
        for p0, p1 in _chip_cols(j):
            pieces.append((p0, g[j][..., loc:loc + p1 - p0]))
            loc += p1 - p0
    pieces.sort(key=lambda t: t[0])
    zeros = jnp.zeros(g.shape[1:-1] + (IN_WP - IN_W,), g.dtype)
    return jnp.concatenate([p for _, p in pieces] + [zeros], axis=-1)


def _padded_to_shards(g, dtype):
    return jnp.stack([jnp.concatenate([g[..., p0:p1] for p0, p1 in _chip_cols(j)], axis=-1).astype(dtype)
                      for j in range(4)])


def _unpad_cols(w):
    n_att = IN_W - ORIG_ATT
    return jnp.concatenate([w[..., :ORIG_BA], w[..., COL_BA:COL_BA + ORIG_ATT - ORIG_BA],
                            w[..., ORIG_BA:ORIG_BA + n_att]], axis=-1)


def _lanes(v, first):
    return jnp.pad(v, ((0, 0), (first, 128 - first - v.shape[1])))[:, None, :]


def _by_chip(g, axis):
    shape = g.shape[:axis] + (4, g.shape[axis] // 4) + g.shape[axis + 1:]
    return jnp.moveaxis(g.reshape(shape), axis, 0)


def kernel(x, norm_w, w_in, conv_qkv_w, a_log, dt_bias, gdn_norm_w, conf_dw_w, conf_dw_b, conf_ln_w, conf_ln_b, conf_pw_w, w_out, final_norm_w, loss_target, m_norm_w, m_w_in, m_conv_qkv_w, m_a_log, m_dt_bias, m_gdn_norm_w, m_conf_dw_w, m_conf_dw_b, m_conf_ln_w, m_conf_ln_b, m_conf_pw_w, m_w_out, m_final_norm_w, v_norm_w, v_w_in, v_conv_qkv_w, v_a_log, v_dt_bias, v_gdn_norm_w, v_conf_dw_w, v_conf_dw_b, v_conf_ln_w, v_conf_ln_b, v_conf_pw_w, v_w_out, v_final_norm_w):
    xi, yi, ci = _position()
    chip = 2 * xi + yi

    shards = [w_in.astype(BF16), w_out.astype(BF16), conf_pw_w.astype(BF16), conv_qkv_w, conf_dw_w]
    g_in, g_out, g_pw, g_cw, g_dw = [
        lax.dynamic_update_slice_in_dim(g, s[None], chip, axis=0)
        for g, s in zip(_gather_chips(shards, name="gather_weights"), shards)]
    cw_full = jnp.moveaxis(g_cw, 0, 2).reshape(DEPTH, SHORT_CONV, 3 * GDN_W)
    dw_full = jnp.moveaxis(g_dw, 0, 2).reshape(DEPTH, CONV_WIDTH, CONV_CH)
    params = dict(
        norm_w=norm_w[:, None, :],
        wp=_shards_to_padded(g_in),
        wout=jnp.moveaxis(g_out, 0, 1).reshape(DEPTH, D_MODEL, D_MODEL),
        pw_w=jnp.moveaxis(g_pw, 0, 1).reshape(DEPTH, CONV_CH, CONV_CH),
        cw=jnp.pad(cw_full, ((0, 0), (0, SHALO - SHORT_CONV), (0, 0))),
        dw_w=jnp.pad(dw_full, ((0, 0), (0, HALO - CONV_WIDTH), (0, 0))),
        al=_lanes(a_log, GDN_HEADS), dtb=_lanes(dt_bias, GDN_HEADS), gnw=gdn_norm_w[:, None, :],
        dw_b=conf_dw_b[:, None, :], ln_w=conf_ln_w[:, None, :], ln_b=conf_ln_b[:, None, :],
    )

    loss_part, grad_x, grads, d_final = _trunk(x[0], loss_target[0], params, final_norm_w[None, :])

    def half_by_chip(first, dtype):
        wp, wout, pw = [lax.dynamic_slice_in_dim(grads[k], first, 2, axis=0) for k in ("wp", "wout", "pw_w")]
        return [_padded_to_shards(wp, dtype), _by_chip(wout, 1).astype(dtype), _by_chip(pw, 1).astype(dtype)]

    keep = half_by_chip(2 * ci, F32)
    give = half_by_chip(2 * (1 - ci), BF16)
    got = _to_sibling(give, name="grads_to_sibling")
    pair = [_sum_arrays([k.reshape((8,) + k.shape[2:]), r.reshape((8,) + r.shape[2:])], name=f"pair_sum_{i}",
                        out_dtype=BF16).reshape(k.shape) for i, (k, r) in enumerate(zip(keep, got))]
    arrived = _to_chips(pair, name="grads_to_chips")
    halves = []
    for i, (pr, ar) in enumerate(zip(pair, arrived)):
        own = lax.dynamic_index_in_dim(pr, chip, axis=0, keepdims=False)
        halves.append(_sum_into_half([own, ar[0], ar[1], ar[2]], ci, name=f"chip_sum_{i}"))
    g_w_in, g_w_out, g_pw_w = _join_halves(halves, name="join_halves")

    ba = grads["ba_sums"]
    small = [grads["norm_w"], ba[:, 0:1, :], ba[:, 1:2, :], grads["gnw"], grads["ln_sums"][:, 2:3, :],
             grads["ln_sums"][:, 0:1, :], grads["ln_sums"][:, 1:2, :], d_final,
             grads["cw"][:, :SHORT_CONV, :], grads["dw_w"][:, :CONV_WIDTH, :], loss_part.reshape(1)]
    red = _unpack(_allreduce_small(_pack(small), name="allreduce_small"), [s.shape for s in small])
    loss = red[10][0]
    g_norm_w = red[0][:, 0, :]
    g_a_log = red[1][:, 0, GDN_HEADS:2 * GDN_HEADS]
    g_dt_bias = red[2][:, 0, GDN_HEADS:2 * GDN_HEADS]
    g_gnw, g_dw_b, g_ln_w, g_ln_b = red[3][:, 0, :], red[4][:, 0, :], red[5][:, 0, :], red[6][:, 0, :]
    g_final = red[7][0]
    g_cw = lax.dynamic_slice_in_dim(red[8], chip * (3 * GDN_W // 4), 3 * GDN_W // 4, axis=2)
    g_dw_w = lax.dynamic_slice_in_dim(red[9], chip * (CONV_CH // 4), CONV_CH // 4, axis=2)

    def cols_first(a):
        return jnp.transpose(a, (2, 0, 1))

    def cols_last(a):
        return jnp.transpose(a, (1, 2, 0))

    g_t = cols_first(g_w_in)
    g_w_in = cols_last(g_t)
    d_w_in, nm_w_in, nv_w_in = [cols_last(a) for a in _adamw(
        cols_first(w_in), g_t, cols_first(m_w_in), cols_first(v_w_in), name="adamw_w_in", by_lead=True)]
    d_w_out, nm_w_out, nv_w_out = _adamw(w_out, g_w_out, m_w_out, v_w_out, name="adamw_w_out")
    d_pw_w, nm_pw_w, nv_pw_w = _adamw(conf_pw_w, g_pw_w, m_conf_pw_w, v_conf_pw_w, name="adamw_pw")
    sw = [norm_w, a_log, dt_bias, gdn_norm_w, conf_dw_b, conf_ln_w, conf_ln_b, final_norm_w, conv_qkv_w, conf_dw_w]
    sg = [g_norm_w, g_a_log, g_dt_bias, g_gnw, g_dw_b, g_ln_w, g_ln_b, g_final, g_cw, g_dw_w]
    sm = [m_norm_w, m_a_log, m_dt_bias, m_gdn_norm_w, m_conf_dw_b, m_conf_ln_w, m_conf_ln_b, m_final_norm_w,
          m_conv_qkv_w, m_conf_dw_w]
    sv = [v_norm_w, v_a_log, v_dt_bias, v_gdn_norm_w, v_conf_dw_b, v_conf_ln_w, v_conf_ln_b, v_final_norm_w,
          v_conv_qkv_w, v_conf_dw_w]
    shapes = [a.shape for a in sw]
    packed = _adamw(_pack(sw)[None], _pack(sg)[None], _pack(sm)[None], _pack(sv)[None], name="adamw_small")
    sd, snm, snv = [_unpack(pk[0], shapes) for pk in packed]

    def order(big3, small10):
        s = small10
        return [s[0], big3[0], s[8], s[1], s[2], s[3], s[9], s[4], s[5], s[6], big3[2], big3[1], s[7]]

    return (loss, grad_x[None], *order([g_w_in, g_w_out, g_pw_w], sg),
            *order([d_w_in, d_w_out, d_pw_w], sd), *order([nm_w_in, nm_w_out, nm_pw_w], snm),
            *order([nv_w_in, nv_w_out, nv_pw_w], snv))
```
